```python
import math
import jax, jax.numpy as jnp
from jax import lax
import numpy as np

D_MODEL = 1024
BATCH = 2
SEQ = 8192
DEPTH = 1

ATTN_HEADS = 8
ATTN_HEAD_DIM = 64
ATTN_W = ATTN_HEADS * ATTN_HEAD_DIM
IDX_HEADS = 4
IDX_HEAD_DIM = 64
IDX_Q_W = IDX_HEADS * IDX_HEAD_DIM
TOPK_MAX = 256
Q_BLOCK = 128
RET_HEADS = 8
RET_QK_DIM = 64
RET_V_DIM = 128
RET_QK_W = RET_HEADS * RET_QK_DIM
RET_V_W = RET_HEADS * RET_V_DIM
RET_CHUNK = 128
ROPE_BASE = 10000.0
D_FF = 4 * D_MODEL
NUM_BUCKETS = 32
MAX_DISTANCE = 128
LN_EPS = 1e-5
DEEPNORM_ALPHA = (2.0 * DEPTH) ** 0.25
DEEPNORM_BETA = (8.0 * DEPTH) ** -0.25
IN_SIZES = (ATTN_W, ATTN_W, ATTN_W,
            IDX_Q_W, IDX_HEAD_DIM, IDX_HEADS,
            RET_QK_W, RET_QK_W, RET_V_W, RET_V_W,
            D_MODEL, D_MODEL)
IN_WIDTH = 3 * ATTN_W + IDX_Q_W + IDX_HEAD_DIM + IDX_HEADS + 2 * RET_QK_W + 2 * RET_V_W + 2 * D_MODEL

kernel_name = 'hybrid_dsa_retention_block'


def layer_norm(x, g, b):
    xf = x.astype(jnp.float32)
    mu = jnp.mean(xf, axis=-1, keepdims=True)
    var = jnp.mean(jnp.square(xf - mu), axis=-1, keepdims=True)
    return ((xf - mu) * lax.rsqrt(var + LN_EPS) * g.astype(jnp.float32) + b.astype(jnp.float32)).astype(x.dtype)


def head_norm(x):
    xf = x.astype(jnp.float32)
    mu = jnp.mean(xf, axis=-1, keepdims=True)
    var = jnp.mean(jnp.square(xf - mu), axis=-1, keepdims=True)
    return (xf - mu) * lax.rsqrt(var + LN_EPS)


def t5_bucket(rel):
    n = jnp.maximum(rel, 0)
    max_exact = NUM_BUCKETS // 2
    nf = jnp.maximum(n, 1).astype(jnp.float32)
    large = max_exact + (jnp.log(nf / max_exact) / math.log(MAX_DISTANCE / max_exact)
                         * (NUM_BUCKETS - max_exact)).astype(jnp.int32)
    large = jnp.minimum(large, NUM_BUCKETS - 1)
    return jnp.where(n < max_exact, n, large)


def rope(x, pos):
    half = x.shape[-1] // 2
    inv = ROPE_BASE ** (-jnp.arange(half, dtype=jnp.float32) / half)
    ang = pos.astype(jnp.float32)[:, :, None, None] * inv
    cos, sin = jnp.cos(ang), jnp.sin(ang)
    x1 = x[..., :half].astype(jnp.float32)
    x2 = x[..., half:].astype(jnp.float32)
    return jnp.concatenate([x1 * cos - x2 * sin, x1 * sin + x2 * cos], axis=-1).astype(x.dtype)


def sparse_attention(q, k, v, iq, ik, iw, pos, rel_bias):
    B, S, H, dh = q.shape
    k_sel = min(TOPK_MAX, S // 4)
    n_blk = S // Q_BLOCK
    key_idx = jnp.arange(S)

    def block(start):
        qb = lax.dynamic_slice_in_dim(q, start, Q_BLOCK, axis=1)
        iqb = lax.dynamic_slice_in_dim(iq, start, Q_BLOCK, axis=1)
        iwb = lax.dynamic_slice_in_dim(iw, start, Q_BLOCK, axis=1)
        posb = lax.dynamic_slice_in_dim(pos, start, Q_BLOCK, axis=1)
        t = start + jnp.arange(Q_BLOCK)
        rel_scores = jax.nn.relu(jnp.einsum('bqhd,bsd->bqhs', iqb, ik).astype(jnp.float32)) * (IDX_HEAD_DIM ** -0.5)
        score = jnp.einsum('bqhs,bqh->bqs', rel_scores, iwb.astype(jnp.float32) * (IDX_HEADS ** -0.5))
        causal = key_idx[None, :] <= t[:, None]
        score = jnp.where(causal[None], score, -jnp.inf)
        _, sel = lax.top_k(score, k_sel)
        valid = sel <= t[None, :, None]
        k_g = jax.vmap(lambda kk, ii: kk[ii])(k, sel)
        v_g = jax.vmap(lambda vv, ii: vv[ii])(v, sel)
        pos_g = jax.vmap(lambda pp, ii: pp[ii])(pos, sel)
        bias = rel_bias[t5_bucket(posb[:, :, None] - pos_g)]
        logits = (jnp.einsum('bqhd,bqkhd->bqhk', qb, k_g).astype(jnp.float32) * (dh ** -0.5)
                  + bias.astype(jnp.float32).transpose(0, 1, 3, 2))
        logits = jnp.where(valid[:, :, None, :], logits, -jnp.inf)
        p = jax.nn.softmax(logits, axis=-1).astype(v.dtype)
        return jnp.einsum('bqhk,bqkhd->bqhd', p, v_g)

    out = lax.map(block, jnp.arange(n_blk) * Q_BLOCK)
    return out.transpose(1, 0, 2, 3, 4).reshape(B, S, H * dh)


def retention(q, k, v):
    B, S, H, dk = q.shape
    dv = v.shape[-1]
    C = RET_CHUNK
    N = S // C
    gamma = 1.0 - 2.0 ** (-5.0 - jnp.arange(H, dtype=jnp.float32))
    log_g = jnp.log(gamma)
    n = jnp.arange(C, dtype=jnp.float32)
    diff = n[:, None] - n[None, :]
    decay_in = jnp.where(diff[None] >= 0, jnp.exp(log_g[:, None, None] * jnp.maximum(diff, 0.0)[None]), 0.0)
    xi = jnp.exp(log_g[None, :] * (n[:, None] + 1.0))
    zeta = jnp.exp(log_g[None, :] * (C - 1.0 - n[:, None]))
    g_chunk = jnp.exp(log_g * C)

    def to_chunks(a):
        return a.astype(jnp.float32).reshape(B, N, C, H, a.shape[-1]).transpose(1, 0, 2, 3, 4)

    def step(R, inp):
        qi, ki, vi = inp
        inner = jnp.einsum('bnhd,bmhd->bhnm', qi, ki) * decay_in[None]
        o = (jnp.einsum('bhnm,bmhv->bnhv', inner, vi)
             + jnp.einsum('bnhd,bhdv->bnhv', qi, R) * xi[None, :, :, None])
        R = R * g_chunk[None, :, None, None] + jnp.einsum('bmhd,bmhv->bhdv', ki * zeta[None, :, :, None], vi)
        return R, o

    R0 = jnp.zeros((B, H, dk, dv), jnp.float32)
    _, o = lax.scan(step, R0, (to_chunks(q), to_chunks(k), to_chunks(v)))
    return o.transpose(1, 0, 2, 3, 4).reshape(B, S, H, dv)


def setup_inputs(seed: int = 0) -> dict:
    key = jax.random.key(seed)
    ks = jax.random.split(key, 16)

    def nrm(k, shape, scale):
        return jax.random.normal(k, shape, jnp.float32) * scale

    return {
        'x': nrm(ks[0], (BATCH, SEQ, D_MODEL), 1.0),
        'positions': jnp.broadcast_to(jnp.arange(SEQ, dtype=jnp.int32), (BATCH, SEQ)),
        'w_in': nrm(ks[1], (DEPTH, D_MODEL, IN_WIDTH), D_MODEL ** -0.5),
        'rel_bias': nrm(ks[2], (NUM_BUCKETS, ATTN_HEADS), 0.5),
        'idx_k_ln_g': 1.0 + nrm(ks[3], (DEPTH, IDX_HEAD_DIM), 0.05),
        'idx_k_ln_b': nrm(ks[4], (DEPTH, IDX_HEAD_DIM), 0.02),
        'w_attn_branch': nrm(ks[5], (DEPTH, ATTN_W, D_MODEL), ATTN_W ** -0.5),
        'w_ret_branch': nrm(ks[6], (DEPTH, RET_V_W, D_MODEL), RET_V_W ** -0.5),
        'w_out': nrm(ks[7], (DEPTH, D_MODEL, D_MODEL), DEEPNORM_BETA * D_MODEL ** -0.5),
        'ln_mix_g': 1.0 + nrm(ks[8], (DEPTH, D_MODEL), 0.05),
        'ln_mix_b': nrm(ks[9], (DEPTH, D_MODEL), 0.02),
        'w_up': nrm(ks[10], (DEPTH, D_MODEL, D_FF), D_MODEL ** -0.5),
        'w_down': nrm(ks[11], (DEPTH, D_FF, D_MODEL), DEEPNORM_BETA * D_FF ** -0.5),
        'ln_ffn_g': 1.0 + nrm(ks[12], (DEPTH, D_MODEL), 0.05),
        'ln_ffn_b': nrm(ks[13], (DEPTH, D_MODEL), 0.02),
    }


def reference(x, positions, w_in, rel_bias, idx_k_ln_g, idx_k_ln_b, w_attn_branch, w_ret_branch,
              w_out, ln_mix_g, ln_mix_b, w_up, w_down, ln_ffn_g, ln_ffn_b):
    B, S, _ = x.shape
    offsets = [int(o) for o in np.cumsum(IN_SIZES)[:-1]]
    for l in range(DEPTH):
        proj = jnp.einsum('bsd,de->bse', x, w_in[l])
        (q_a, k_a, v_a, iq, ik, iw, q_r, k_r, v_r, g_r, gate_a, gate_r) = jnp.split(proj, offsets, axis=-1)
        q_a = q_a.reshape(B, S, ATTN_HEADS, ATTN_HEAD_DIM)
        k_a = k_a.reshape(B, S, ATTN_HEADS, ATTN_HEAD_DIM)
        v_a = v_a.reshape(B, S, ATTN_HEADS, ATTN_HEAD_DIM)
        iq = iq.reshape(B, S, IDX_HEADS, IDX_HEAD_DIM)
        ik = layer_norm(ik, idx_k_ln_g[l], idx_k_ln_b[l])
        y_a = sparse_attention(q_a, k_a, v_a, iq, ik, iw, positions, rel_bias)

        q_r = rope(q_r.reshape(B, S, RET_HEADS, RET_QK_DIM), positions)
        k_r = rope(k_r.reshape(B, S, RET_HEADS, RET_QK_DIM), positions) * (RET_QK_DIM ** -0.5)
        v_r = v_r.reshape(B, S, RET_HEADS, RET_V_DIM)
        ret = head_norm(retention(q_r, k_r, v_r)).reshape(B, S, RET_V_W)
        y_r = (jax.nn.silu(g_r.astype(jnp.float32)) * ret).astype(x.dtype)

        h = (jax.nn.sigmoid(gate_a) * jnp.einsum('bse,ed->bsd', y_a, w_attn_branch[l])
             + jax.nn.sigmoid(gate_r) * jnp.einsum('bse,ed->bsd', y_r, w_ret_branch[l]))
        mix = jnp.einsum('bsd,de->bse', h, w_out[l])
        x = layer_norm(DEEPNORM_ALPHA * x + mix, ln_mix_g[l], ln_mix_b[l])

        hid = jax.nn.relu(jnp.einsum('bsd,df->bsf', x, w_up[l]))
        ffn = jnp.einsum('bsf,fd->bsd', hid * hid, w_down[l])
        x = layer_norm(DEEPNORM_ALPHA * x + ffn, ln_ffn_g[l], ln_ffn_b[l])
    return x
```

```python
import functools
import math

import numpy as np
import jax
import jax.numpy as jnp
from jax import lax
from jax.experimental import pallas as pl
from jax.experimental.pallas import tpu as pltpu

F32 = jnp.float32
BF16 = jnp.bfloat16

ATTN_HEADS = 8
ATTN_HEAD_DIM = 64
ATTN_W = ATTN_HEADS * ATTN_HEAD_DIM
IDX_HEADS = 4
IDX_HEAD_DIM = 64
IDX_Q_W = IDX_HEADS * IDX_HEAD_DIM
TOPK_MAX = 256
RET_HEADS = 8
RET_QK_DIM = 64
RET_V_DIM = 128
RET_QK_W = RET_HEADS * RET_QK_DIM
RET_V_W = RET_HEADS * RET_V_DIM
RET_CHUNK = 128
ROPE_BASE = 10000.0
NUM_BUCKETS = 32
MAX_DISTANCE = 128
LN_EPS = 1e-5
DEPTH = 1
DEEPNORM_ALPHA = (2.0 * DEPTH) ** 0.25

LANES = 128
VMEM_LIMIT = 56 * 1024 * 1024

TQ = 128
CK = 512
NEG = -1e30
BISECT_ROUNDS = 20
BIAS_TABLE_N = 128
FAR_N = 113


def _cparams(n_grid):
    return pltpu.CompilerParams(
        dimension_semantics=("arbitrary",) * n_grid,
        vmem_limit_bytes=VMEM_LIMIT)


def _trig_kernel(pos_ref, inv_ref, cos_ref, sin_ref):
    ang = pos_ref[...] * inv_ref[...]
    cos_ref[...] = jnp.cos(ang)
    sin_ref[...] = jnp.sin(ang)


def _rope_tables(positions):
    B, S = positions.shape
    half = RET_QK_DIM // 2
    inv = ROPE_BASE ** (-jnp.arange(half, dtype=F32) / half)
    per_row = LANES // half
    rows = B * S // per_row
    pos_e = jnp.repeat(positions.astype(F32).reshape(rows, per_row), half, axis=1)
    inv_e = jnp.tile(inv, per_row).reshape(1, LANES)
    tr = min(rows, 1024)
    cos, sin = pl.pallas_call(
        _trig_kernel,
        grid=(rows // tr,),
        in_specs=[pl.BlockSpec((tr, LANES), lambda i: (i, 0)),
                  pl.BlockSpec((1, LANES), lambda i: (0, 0))],
        out_specs=[pl.BlockSpec((tr, LANES), lambda i: (i, 0))] * 2,
        out_shape=[jax.ShapeDtypeStruct((rows, LANES), F32)] * 2,
        compiler_params=_cparams(1),
        name="rope_tables",
    )(pos_e, inv_e)
    reps = RET_QK_W // half
    cos = jnp.tile(cos.reshape(B * S, half), (1, reps))
    sin = jnp.tile(sin.reshape(B * S, half), (1, reps))
    return cos, sin


def _proj_kernel(x_ref, w_ref, o_ref, *, scale):
    acc = jnp.dot(x_ref[...], w_ref[...], preferred_element_type=F32)
    if scale != 1.0:
        acc = acc * scale
    o_ref[...] = acc.astype(o_ref.dtype)


def _proj(xb, w, out_dtype, scale=1.0, tm=1024):
    T, D = xb.shape
    N = w.shape[1]
    tn = min(N, 512)
    tm = min(tm, T)
    return pl.pallas_call(
        functools.partial(_proj_kernel, scale=scale),
        grid=(T // tm, N // tn),
        in_specs=[pl.BlockSpec((tm, D), lambda i, j: (i, 0)),
                  pl.BlockSpec((D, tn), lambda i, j: (0, j))],
        out_specs=pl.BlockSpec((tm, tn), lambda i, j: (i, j)),
        out_shape=jax.ShapeDtypeStruct((T, N), out_dtype),
        compiler_params=_cparams(2),
        name="proj",
    )(xb, w)


def _proj_rope_kernel(x_ref, w_ref, wr_ref, cos_ref, sin_ref, o_ref, *, scale):
    x = x_ref[...]
    a = jnp.dot(x, w_ref[...], preferred_element_type=F32)
    r = jnp.dot(x, wr_ref[...], preferred_element_type=F32)
    out = a * cos_ref[...] + r * sin_ref[...]
    if scale != 1.0:
        out = out * scale
    o_ref[...] = out.astype(o_ref.dtype)


def _proj_rope(xb, w, w_rot, cos, sin, scale, tm=1024):
    T, D = xb.shape
    N = w.shape[1]
    tm = min(tm, T)
    return pl.pallas_call(
        functools.partial(_proj_rope_kernel, scale=scale),
        grid=(T // tm,),
        in_specs=[pl.BlockSpec((tm, D), lambda i: (i, 0)),
                  pl.BlockSpec((D, N), lambda i: (0, 0)),
                  pl.BlockSpec((D, N), lambda i: (0, 0)),
                  pl.BlockSpec((tm, N), lambda i: (i, 0)),
                  pl.BlockSpec((tm, N), lambda i: (i, 0))],
        out_specs=pl.BlockSpec((tm, N), lambda i: (i, 0)),
        out_shape=jax.ShapeDtypeStruct((T, N), BF16),
        compiler_params=_cparams(1),
        name="proj_rope",
    )(xb, w, w_rot, cos, sin)


def _proj_idx_kernel(x_ref, w_ref, g_ref, b_ref, o_ref, *, iw_scale):
    acc = jnp.dot(x_ref[...], w_ref[...], preferred_element_type=F32)
    lane = lax.broadcasted_iota(jnp.int32, acc.shape, 1)
    is_k = lane < IDX_HEAD_DIM
    mu = jnp.sum(jnp.where(is_k, acc, 0.0), axis=-1, keepdims=True) / IDX_HEAD_DIM
    d = acc - mu
    var = jnp.sum(jnp.where(is_k, d * d, 0.0), axis=-1, keepdims=True) / IDX_HEAD_DIM
    ln = d * lax.rsqrt(var + LN_EPS) * g_ref[...] + b_ref[...]
    o_ref[...] = jnp.where(is_k, ln, acc * iw_scale)


def _proj_idx(xb, w_pad, g_pad, b_pad, iw_scale, tm=1024):
    T, D = xb.shape
    tm = min(tm, T)
    return pl.pallas_call(
        functools.partial(_proj_idx_kernel, iw_scale=iw_scale),
        grid=(T // tm,),
        in_specs=[pl.BlockSpec((tm, D), lambda i: (i, 0)),
                  pl.BlockSpec((D, LANES), lambda i: (0, 0)),
                  pl.BlockSpec((1, LANES), lambda i: (0, 0)),
                  pl.BlockSpec((1, LANES), lambda i: (0, 0))],
        out_specs=pl.BlockSpec((tm, LANES), lambda i: (i, 0)),
        out_shape=jax.ShapeDtypeStruct((T, LANES), F32),
        compiler_params=_cparams(1),
        name="proj_idx",
    )(xb, w_pad, g_pad, b_pad)


def _t5_bucket_table():
    n = np.arange(BIAS_TABLE_N)
    max_exact = NUM_BUCKETS // 2
    nf = np.maximum(n, 1).astype(np.float64)
    large = max_exact + (np.log(nf / max_exact) / math.log(MAX_DISTANCE / max_exact)
                         * (NUM_BUCKETS - max_exact)).astype(np.int64)
    large = np.minimum(large, NUM_BUCKETS - 1)
    bucket = np.where(n < max_exact, n, large)
    assert np.all(bucket[FAR_N:] == NUM_BUCKETS - 1) and bucket[FAR_N - 1] != NUM_BUCKETS - 1
    return bucket.astype(np.int32)


def _attn_kernel(far_ref, iq_ref, iw_ref, posq_ref, posk_ref, ikT_ref, q_ref, kT_ref, v_ref, tbl_ref,
                 o_ref, sc_ref, m_ref, l_ref, acc_ref, *, k_sel, seq):
    i = pl.program_id(1)
    nch = (i * TQ + TQ + CK - 1) // CK
    n_sub = CK // LANES
    row_t = i * TQ + lax.broadcasted_iota(jnp.int32, (TQ, 1), 0)
    kf = float(k_sel)

    def lane_fold(a):
        out = a[:, 0:LANES]
        for j in range(1, n_sub):
            out = out + a[:, j * LANES:(j + 1) * LANES]
        return out

    iq = iq_ref[...]
    iw = iw_ref[...]
    iq_h = [iq[:, h * IDX_HEAD_DIM:(h + 1) * IDX_HEAD_DIM] for h in range(IDX_HEADS)]
    iw_h = [iw[:, h:h + 1] for h in range(IDX_HEADS)]

    def score_body(c, carry):
        mn, mx = carry
        off = pl.multiple_of(c * CK, CK)
        ik = ikT_ref[:, pl.ds(off, CK)]
        s = None
        for h in range(IDX_HEADS):
            z = jnp.dot(iq_h[h], ik, preferred_element_type=F32)
            t = jnp.maximum(z, 0.0) * iw_h[h]
            s = t if s is None else s + t
        col = off + lax.broadcasted_iota(jnp.int32, (1, CK), 1)
        causal = col <= row_t
        sc_ref[:, pl.ds(off, CK)] = jnp.where(causal, s, -jnp.inf)
        mn = jnp.minimum(mn, jnp.min(jnp.where(causal, s, jnp.inf), axis=-1, keepdims=True))
        mx = jnp.maximum(mx, jnp.max(jnp.where(causal, s, -jnp.inf), axis=-1, keepdims=True))
        return mn, mx

    mn, mx = lax.fori_loop(0, nch, score_body,
                           (jnp.full((TQ, 1), jnp.inf, F32), jnp.full((TQ, 1), -jnp.inf, F32)))

    def count(pred_fn):
        def body(c, acc):
            off = pl.multiple_of(c * CK, CK)
            blk = sc_ref[:, pl.ds(off, CK)]
            col = off + lax.broadcasted_iota(jnp.int32, (1, CK), 1)
            return acc + lane_fold(jnp.where(pred_fn(blk, col), 1.0, 0.0))
        acc = lax.fori_loop(0, nch, body, jnp.zeros((TQ, LANES), F32))
        return jnp.sum(acc, axis=-1, keepdims=True)

    def bisect_round(_, st):
        lo, hi, c_lo = st
        mid = 0.5 * (lo + hi)
        c = count(lambda blk, col: blk >= mid)
        ok = c >= kf
        return jnp.where(ok, mid, lo), jnp.where(ok, hi, mid), jnp.where(ok, c, c_lo)

    c_all = (row_t + 1).astype(F32)
    lo, hi, c_lo = lax.fori_loop(0, BISECT_ROUNDS, bisect_round, (mn, mx, c_all))

    def min_ge(th):
        def body(c, acc):
            off = pl.multiple_of(c * CK, CK)
            blk = sc_ref[:, pl.ds(off, CK)]
            return jnp.minimum(acc, jnp.min(jnp.where(blk >= th, blk, jnp.inf), axis=-1, keepdims=True))
        return lax.fori_loop(0, nch, body, jnp.full((TQ, 1), jnp.inf, F32))

    cur0 = min_ge(lo)

    def walk_cond(st):
        return st[3] > 0.0

    def walk_body(st):
        cur, c_ge, _, _ = st

        def body(c, carry):
            cnt, nxt = carry
            off = pl.multiple_of(c * CK, CK)
            blk = sc_ref[:, pl.ds(off, CK)]
            gt = blk > cur
            cnt = cnt + lane_fold(jnp.where(gt, 1.0, 0.0))
            nxt = jnp.minimum(nxt, jnp.min(jnp.where(gt, blk, jnp.inf), axis=-1, keepdims=True))
            return cnt, nxt

        cnt, nxt = lax.fori_loop(0, nch, body,
                                 (jnp.zeros((TQ, LANES), F32), jnp.full((TQ, 1), jnp.inf, F32)))
        c_gt = jnp.sum(cnt, axis=-1, keepdims=True)
        adv = c_gt >= kf
        cur = jnp.where(adv, nxt, cur)
        c_ge = jnp.where(adv, c_gt, c_ge)
        return cur, c_ge, c_gt, jnp.max(jnp.where(adv, 1.0, 0.0))

    tau, c_ge, c_gt, _ = lax.while_loop(
        walk_cond, walk_body, (cur0, c_lo, jnp.zeros((TQ, 1), F32), jnp.float32(1.0)))

    need = kf - c_gt
    trim = (c_ge - c_gt) > need

    def tie_bisect():
        def rnd(_, st):
            jlo, jhi = st
            mid = (jlo + jhi) >> 1
            c = count(lambda blk, col: (blk == tau) & (col <= mid))
            ok = c >= need
            return jnp.where(ok, jlo, mid), jnp.where(ok, mid, jhi)
        n_rounds = max(1, int(math.ceil(math.log2(seq))))
        _, jhi = lax.fori_loop(0, n_rounds, rnd,
                               (jnp.full((TQ, 1), -1, jnp.int32), jnp.full((TQ, 1), seq - 1, jnp.int32)))
        return jnp.where(trim, jhi, seq)

    jstar = lax.cond(jnp.max(jnp.where(trim, 1.0, 0.0)) > 0.0, tie_bisect,
                     lambda: jnp.full((TQ, 1), seq, jnp.int32))

    def mask_body(c, _):
        off = pl.multiple_of(c * CK, CK)
        blk = sc_ref[:, pl.ds(off, CK)]
        col = off + lax.broadcasted_iota(jnp.int32, (1, CK), 1)
        sel = (blk > tau) | ((blk == tau) & (col <= jstar))
        sc_ref[:, pl.ds(off, CK)] = jnp.where(sel, 0.0, NEG)
        return 0

    lax.fori_loop(0, nch, mask_body, 0)

    m_ref[...] = jnp.full(m_ref.shape, NEG, F32)
    l_ref[...] = jnp.zeros(l_ref.shape, F32)
    acc_ref[...] = jnp.zeros(acc_ref.shape, F32)

    q = q_ref[...]
    q_h = [q[:, h * ATTN_HEAD_DIM:(h + 1) * ATTN_HEAD_DIM] for h in range(ATTN_HEADS)]
    pq = posq_ref[...]
    pq_min = jnp.min(pq)

    def attn_chunk(c, _):
        off = pl.multiple_of(c * CK, CK)
        mb = sc_ref[:, pl.ds(off, CK)]
        pk = posk_ref[:, pl.ds(off, CK)]

        def run(const_bias):
            if not const_bias:
                n = jnp.clip(pq - pk, 0, BIAS_TABLE_N - 1)
            for h in range(ATTN_HEADS):
                kT = kT_ref[h * ATTN_HEAD_DIM:(h + 1) * ATTN_HEAD_DIM, pl.ds(off, CK)]
                s = jnp.dot(q_h[h], kT, preferred_element_type=F32)
                if const_bias:
                    s = s + far_ref[h]
                else:
                    tb = jnp.broadcast_to(tbl_ref[h:h + 1, :], (TQ, LANES))
                    bias = jnp.concatenate(
                        [jnp.take_along_axis(tb, n[:, j * LANES:(j + 1) * LANES], axis=1)
                         for j in range(n_sub)], axis=1)
                    s = s + bias
                s = s + mb
                m_prev = m_ref[h]
                m_new = jnp.maximum(m_prev, jnp.max(s, axis=-1, keepdims=True))
                alpha = jnp.exp(m_prev - m_new)
                p = jnp.exp(s - m_new)
                l_ref[h] = alpha * l_ref[h] + jnp.sum(p, axis=-1, keepdims=True)
                pair = h // 2
                vv = v_ref[pl.ds(off, CK), pair * LANES:(pair + 1) * LANES]
                acc_ref[h] = alpha * acc_ref[h] + jnp.dot(p.astype(BF16), vv, preferred_element_type=F32)
                m_ref[h] = m_new

        is_far = (pq_min - jnp.max(pk)) >= FAR_N
        lax.cond(is_far, lambda: run(True), lambda: run(False))
        return 0

    lax.fori_loop(0, nch, attn_chunk, 0)

    lane = lax.broadcasted_iota(jnp.int32, (TQ, LANES), 1)
    for pair in range(ATTN_HEADS // 2):
        a0 = acc_ref[2 * pair] / l_ref[2 * pair]
        a1 = acc_ref[2 * pair + 1] / l_ref[2 * pair + 1]
        o_ref[:, pair * LANES:(pair + 1) * LANES] = jnp.where(lane < ATTN_HEAD_DIM, a0, a1).astype(o_ref.dtype)


def _sparse_attention(q, kT, v, iq, ikT, iw, positions, rel_bias):
    B, S, _ = q.shape
    k_sel = min(TOPK_MAX, S // 4)
    bucket = _t5_bucket_table()
    tbl = rel_bias[bucket].T.astype(F32)
    far = rel_bias[NUM_BUCKETS - 1].astype(F32)
    pos_q = positions.reshape(B, S, 1)
    pos_k = positions.reshape(B, 1, S)
    sq = pl.Squeezed()
    return pl.pallas_call(
        functools.partial(_attn_kernel, k_sel=k_sel, seq=S),
        grid=(B, S // TQ),
        in_specs=[pl.BlockSpec(memory_space=pltpu.SMEM),
                  pl.BlockSpec((sq, TQ, IDX_Q_W), lambda b, i: (b, i, 0)),
                  pl.BlockSpec((sq, TQ, IDX_HEADS), lambda b, i: (b, i, 0)),
                  pl.BlockSpec((sq, TQ, 1), lambda b, i: (b, i, 0)),
                  pl.BlockSpec((sq, 1, S), lambda b, i: (b, 0, 0)),
                  pl.BlockSpec((sq, IDX_HEAD_DIM, S), lambda b, i: (b, 0, 0)),
                  pl.BlockSpec((sq, TQ, ATTN_W), lambda b, i: (b, i, 0)),
                  pl.BlockSpec((sq, ATTN_W, S), lambda b, i: (b, 0, 0)),
                  pl.BlockSpec((sq, S, ATTN_W), lambda b, i: (b, 0, 0)),
                  pl.BlockSpec((ATTN_HEADS, BIAS_TABLE_N), lambda b, i: (0, 0))],
        out_specs=pl.BlockSpec((sq, TQ, ATTN_W), lambda b, i: (b, i, 0)),
        out_shape=jax.ShapeDtypeStruct((B, S, ATTN_W), BF16),
        scratch_shapes=[pltpu.VMEM((TQ, S), F32),
                        pltpu.VMEM((ATTN_HEADS, TQ, 1), F32),
                        pltpu.VMEM((ATTN_HEADS, TQ, 1), F32),
                        pltpu.VMEM((ATTN_HEADS, TQ, LANES), F32)],
        compiler_params=_cparams(2),
        name="sparse_attention",
    )(far, iq, iw, pos_q, pos_k, ikT, q, kT, v, tbl)


def _retention_kernel(gch_ref, q_ref, kT_ref, v_ref, g_ref, decay_ref, xi_ref, zeta_ref, o_ref, r_ref):
    @pl.when(pl.program_id(1) == 0)
    def _():
        r_ref[...] = jnp.zeros(r_ref.shape, F32)

    C = RET_CHUNK
    lane = lax.broadcasted_iota(jnp.int32, (C, LANES), 1)
    for pair in range(RET_HEADS // 2):
        q_pair = q_ref[:, pair * LANES:(pair + 1) * LANES]
        kT_pair = kT_ref[pair * LANES:(pair + 1) * LANES, :]
        r_pair = jnp.concatenate([r_ref[2 * pair], r_ref[2 * pair + 1]], axis=0).astype(BF16)
        for sub in range(2):
            h = 2 * pair + sub
            in_head = (lane >= sub * RET_QK_DIM) & (lane < (sub + 1) * RET_QK_DIM)
            qm = jnp.where(in_head, q_pair, jnp.zeros_like(q_pair))
            v_h = v_ref[:, h * RET_V_DIM:(h + 1) * RET_V_DIM]
            inner = jnp.dot(qm, kT_pair, preferred_element_type=F32) * decay_ref[h]
            o = (jnp.dot(inner.astype(BF16), v_h, preferred_element_type=F32)
                 + jnp.dot(qm, r_pair, preferred_element_type=F32) * xi_ref[h])
            kz = (kT_pair[sub * RET_QK_DIM:(sub + 1) * RET_QK_DIM, :].astype(F32) * zeta_ref[h]).astype(BF16)
            r_ref[h] = r_ref[h] * gch_ref[h] + jnp.dot(kz, v_h, preferred_element_type=F32)
            mu = jnp.mean(o, axis=-1, keepdims=True)
            d = o - mu
            var = jnp.mean(d * d, axis=-1, keepdims=True)
            hn = d * lax.rsqrt(var + LN_EPS)
            g = g_ref[:, h * RET_V_DIM:(h + 1) * RET_V_DIM]
            silu = g / (1.0 + jnp.exp(-g))
            o_ref[:, h * RET_V_DIM:(h + 1) * RET_V_DIM] = (silu * hn).astype(o_ref.dtype)


def _retention(q, kT, v, g):
    B, S, _ = q.shape
    C = RET_CHUNK
    H = RET_HEADS
    gamma = 1.0 - 2.0 ** (-5.0 - jnp.arange(H, dtype=F32))
    log_g = jnp.log(gamma)
    n = jnp.arange(C, dtype=F32)
    diff = n[:, None] - n[None, :]
    decay_in = jnp.where(diff[None] >= 0, jnp.exp(log_g[:, None, None] * jnp.maximum(diff, 0.0)[None]), 0.0)
    xi = jnp.exp(log_g[None, :] * (n[:, None] + 1.0))
    zeta = jnp.exp(log_g[None, :] * (C - 1.0 - n[:, None]))
    g_chunk = jnp.exp(log_g * C)
    xi_b = jnp.broadcast_to(xi.T[:, :, None], (H, C, RET_V_DIM))
    zeta_r = zeta.T.reshape(H, 1, C)
    sq = pl.Squeezed()
    return pl.pallas_call(
        _retention_kernel,
        grid=(B, S // C),
        in_specs=[pl.BlockSpec(memory_space=pltpu.SMEM),
                  pl.BlockSpec((sq, C, RET_QK_W), lambda b, i: (b, i, 0)),
                  pl.BlockSpec((sq, RET_QK_W, C), lambda b, i: (b, 0, i)),
                  pl.BlockSpec((sq, C, RET_V_W), lambda b, i: (b, i, 0)),
                  pl.BlockSpec((sq, C, RET_V_W), lambda b, i: (b, i, 0)),
                  pl.BlockSpec((H, C, C), lambda b, i: (0, 0, 0)),
                  pl.BlockSpec((H, C, RET_V_DIM), lambda b, i: (0, 0, 0)),
                  pl.BlockSpec((H, 1, C), lambda b, i: (0, 0, 0))],
        out_specs=pl.BlockSpec((sq, C, RET_V_W), lambda b, i: (b, i, 0)),
        out_shape=jax.ShapeDtypeStruct((B, S, RET_V_W), BF16),
        scratch_shapes=[pltpu.VMEM((H, RET_QK_DIM, RET_V_DIM), F32)],
        compiler_params=_cparams(2),
        name="retention",
    )(g_chunk, q, kT, v, g, decay_in, xi_b, zeta_r)


def _layer_norm(z, g, b):
    mu = jnp.mean(z, axis=-1, keepdims=True)
    d = z - mu
    var = jnp.mean(d * d, axis=-1, keepdims=True)
    return d * lax.rsqrt(var + LN_EPS) * g + b


def _merge_kernel(x_ref, ya_ref, yr_ref, ga_ref, gr_ref, wa_ref, wr_ref, wo_ref, g_ref, b_ref,
                  x1_ref, x1b_ref):
    a = jnp.dot(ya_ref[...], wa_ref[...], preferred_element_type=F32)
    r = jnp.dot(yr_ref[...], wr_ref[...], preferred_element_type=F32)
    sig_a = 1.0 / (1.0 + jnp.exp(-ga_ref[...]))
    sig_r = 1.0 / (1.0 + jnp.exp(-gr_ref[...]))
    h = sig_a * a + sig_r * r
    mix = jnp.dot(h.astype(BF16), wo_ref[...], preferred_element_type=F32)
    x1 = _layer_norm(DEEPNORM_ALPHA * x_ref[...] + mix, g_ref[...], b_ref[...])
    x1_ref[...] = x1
    x1b_ref[...] = x1.astype(BF16)


def _merge(x, ya, yr, ga, gr, wa, wr, wo, g, b, tm=512):
    T, D = x.shape
    tm = min(tm, T)
    row = lambda i: (i, 0)
    fixed = lambda i: (0, 0)
    return pl.pallas_call(
        _merge_kernel,
        grid=(T // tm,),
        in_specs=[pl.BlockSpec((tm, D), row),
                  pl.BlockSpec((tm, ya.shape[1]), row),
                  pl.BlockSpec((tm, yr.shape[1]), row),
                  pl.BlockSpec((tm, D), row),
                  pl.BlockSpec((tm, D), row),
                  pl.BlockSpec(wa.shape, fixed),
                  pl.BlockSpec(wr.shape, fixed),
                  pl.BlockSpec(wo.shape, fixed),
                  pl.BlockSpec((1, D), fixed),
                  pl.BlockSpec((1, D), fixed)],
        out_specs=[pl.BlockSpec((tm, D), row), pl.BlockSpec((tm, D), row)],
        out_shape=[jax.ShapeDtypeStruct((T, D), F32), jax.ShapeDtypeStruct((T, D), BF16)],
        compiler_params=_cparams(1),
        name="merge",
    )(x, ya, yr, ga, gr, wa, wr, wo, g, b)


def _ffn_kernel(x1b_ref, x1_ref, wu_ref, wd_ref, g_ref, b_ref, o_ref, acc_ref):
    f = pl.program_id(1)

    @pl.when(f == 0)
    def _():
        acc_ref[...] = jnp.zeros(acc_ref.shape, F32)

    hid = jnp.maximum(jnp.dot(x1b_ref[...], wu_ref[...], preferred_element_type=F32), 0.0)
    acc_ref[...] += jnp.dot((hid * hid).astype(BF16), wd_ref[...], preferred_element_type=F32)

    @pl.when(f == pl.num_programs(1) - 1)
    def _():
        o_ref[...] = _layer_norm(DEEPNORM_ALPHA * x1_ref[...] + acc_ref[...], g_ref[...], b_ref[...])


def _ffn(x1b, x1, wu, wd, g, b, tm=512, tf=1024):
    T, D = x1.shape
    F = wu.shape[1]
    tm = min(tm, T)
    return pl.pallas_call(
        _ffn_kernel,
        grid=(T // tm, F // tf),
        in_specs=[pl.BlockSpec((tm, D), lambda i, f: (i, 0)),
                  pl.BlockSpec((tm, D), lambda i, f: (i, 0)),
                  pl.BlockSpec((D, tf), lambda i, f: (0, f)),
                  pl.BlockSpec((tf, D), lambda i, f: (f, 0)),
                  pl.BlockSpec((1, D), lambda i, f: (0, 0)),
                  pl.BlockSpec((1, D), lambda i, f: (0, 0))],
        out_specs=pl.BlockSpec((tm, D), lambda i, f: (i, 0)),
        out_shape=jax.ShapeDtypeStruct((T, D), F32),
        scratch_shapes=[pltpu.VMEM((tm, D), F32)],
        compiler_params=_cparams(2),
        name="ffn",
    )(x1b, x1, wu, wd, g, b)


def _rot_half_weight(w):
    D, N = w.shape
    half = RET_QK_DIM // 2
    wh = w.reshape(D, N // RET_QK_DIM, 2, half)
    return jnp.stack([-wh[:, :, 1], wh[:, :, 0]], axis=2).reshape(D, N)


def kernel(x, positions, w_in, rel_bias, idx_k_ln_g, idx_k_ln_b, w_attn_branch, w_ret_branch,
           w_out, ln_mix_g, ln_mix_b, w_up, w_down, ln_ffn_g, ln_ffn_b):
    B, S, D = x.shape
    T = B * S
    sizes = (ATTN_W, ATTN_W, ATTN_W, IDX_Q_W, IDX_HEAD_DIM, IDX_HEADS,
             RET_QK_W, RET_QK_W, RET_V_W, RET_V_W, D, D)
    offs = [0] + [int(o) for o in np.cumsum(sizes)]
    cos, sin = _rope_tables(positions)
    xf = x.reshape(T, D)
    for l in range(DEPTH):
        wl = w_in[l]
        cols = [wl[:, offs[k]:offs[k + 1]] for k in range(len(sizes))]
        (w_qa, w_ka, w_va, w_iq, w_ik, w_iw, w_qr, w_kr, w_vr, w_gr, w_ga, w_gtr) = cols
        xb = xf.astype(BF16)

        q_a = _proj(xb, w_qa.astype(BF16), BF16, scale=ATTN_HEAD_DIM ** -0.5).reshape(B, S, ATTN_W)
        k_a = _proj(xb, w_ka.astype(BF16), BF16).reshape(B, S, ATTN_W)
        v_a = _proj(xb, w_va.astype(BF16), BF16).reshape(B, S, ATTN_W)
        iq = _proj(xb, w_iq.astype(BF16), BF16).reshape(B, S, IDX_Q_W)
        pad = LANES - IDX_HEAD_DIM - IDX_HEADS
        w_idx = jnp.concatenate([w_ik, w_iw, jnp.zeros((D, pad), F32)], axis=1).astype(BF16)
        g_pad = jnp.concatenate([idx_k_ln_g[l], jnp.zeros((LANES - IDX_HEAD_DIM,), F32)]).reshape(1, LANES)
        b_pad = jnp.concatenate([idx_k_ln_b[l], jnp.zeros((LANES - IDX_HEAD_DIM,), F32)]).reshape(1, LANES)
        idx = _proj_idx(xb, w_idx, g_pad, b_pad,
                        (IDX_HEAD_DIM ** -0.5) * (IDX_HEADS ** -0.5)).reshape(B, S, LANES)
        ikT = jnp.swapaxes(idx[:, :, :IDX_HEAD_DIM], 1, 2).astype(BF16)
        iw = idx[:, :, IDX_HEAD_DIM:IDX_HEAD_DIM + IDX_HEADS]
        kT_a = jnp.swapaxes(k_a, 1, 2)
        y_a = _sparse_attention(q_a, kT_a, v_a, iq, ikT, iw, positions, rel_bias)

        q_r = _proj_rope(xb, w_qr.astype(BF16), _rot_half_weight(w_qr).astype(BF16), cos, sin, 1.0)
        k_r = _proj_rope(xb, w_kr.astype(BF16), _rot_half_weight(w_kr).astype(BF16), cos, sin,
                         RET_QK_DIM ** -0.5)
        v_r = _proj(xb, w_vr.astype(BF16), BF16).reshape(B, S, RET_V_W)
        g_r = _proj(xb, w_gr.astype(BF16), F32).reshape(B, S, RET_V_W)
        kT_r = jnp.swapaxes(k_r.reshape(B, S, RET_QK_W), 1, 2)
        y_r = _retention(q_r.reshape(B, S, RET_QK_W), kT_r, v_r, g_r)

        gate_a = _proj(xb, w_ga.astype(BF16), F32)
        gate_r = _proj(xb, w_gtr.astype(BF16), F32)
        x1, x1b = _merge(xf, y_a.reshape(T, ATTN_W), y_r.reshape(T, RET_V_W), gate_a, gate_r,
                         w_attn_branch[l].astype(BF16), w_ret_branch[l].astype(BF16),
                         w_out[l].astype(BF16), ln_mix_g[l].reshape(1, D), ln_mix_b[l].reshape(1, D))
        xf = _ffn(x1b, x1, w_up[l].astype(BF16), w_down[l].astype(BF16),
                  ln_ffn_g[l].reshape(1, D), ln_ffn_b[l].reshape(1, D))
    return xf.reshape(B, S, D)
```

```python
import functools
import math

import numpy as np
import jax
import jax.numpy as jnp
from jax import lax
from jax.experimental import pallas as pl
from jax.experimental.pallas import tpu as pltpu

F32 = jnp.float32
BF16 = jnp.bfloat16

ATTN_HEADS = 8
ATTN_HEAD_DIM = 64
ATTN_W = ATTN_HEADS * ATTN_HEAD_DIM
IDX_HEADS = 4
IDX_HEAD_DIM = 64
IDX_Q_W = IDX_HEADS * IDX_HEAD_DIM
TOPK_MAX = 256
RET_HEADS = 8
RET_QK_DIM = 64
RET_V_DIM = 128
RET_QK_W = RET_HEADS * RET_QK_DIM
RET_V_W = RET_HEADS * RET_V_DIM
RET_CHUNK = 128
ROPE_BASE = 10000.0
NUM_BUCKETS = 32
MAX_DISTANCE = 128
LN_EPS = 1e-5
DEPTH = 1
DEEPNORM_ALPHA = (2.0 * DEPTH) ** 0.25

LANES = 128
SUBLANES = 8
VMEM_LIMIT = 56 * 1024 * 1024

TQ = 128
CK = 512
NEG = -1e30
BISECT_ROUNDS = 20
BIAS_TABLE_N = 128
FAR_N = 113


def _cparams(n_grid):
    return pltpu.CompilerParams(
        dimension_semantics=("arbitrary",) * n_grid,
        vmem_limit_bytes=VMEM_LIMIT)


def _trig_kernel(pos_ref, inv_ref, cos_ref, sin_ref):
    ang = pos_ref[...] * inv_ref[...]
    cos_ref[...] = jnp.cos(ang)
    sin_ref[...] = jnp.sin(ang)


def _rope_tables(positions):
    B, S = positions.shape
    half = RET_QK_DIM // 2
    inv = ROPE_BASE ** (-jnp.arange(half, dtype=F32) / half)
    per_row = LANES // half
    rows = B * S // per_row
    pos_e = jnp.repeat(positions.astype(F32).reshape(rows, per_row), half, axis=1)
    inv_e = jnp.tile(inv, per_row).reshape(1, LANES)
    tr = min(rows, 1024)
    cos, sin = pl.pallas_call(
        _trig_kernel,
        grid=(rows // tr,),
        in_specs=[pl.BlockSpec((tr, LANES), lambda i: (i, 0)),
                  pl.BlockSpec((1, LANES), lambda i: (0, 0))],
        out_specs=[pl.BlockSpec((tr, LANES), lambda i: (i, 0))] * 2,
        out_shape=[jax.ShapeDtypeStruct((rows, LANES), F32)] * 2,
        compiler_params=_cparams(1),
        name="rope_tables",
    )(pos_e, inv_e)
    reps = RET_QK_W // half
    cos = jnp.tile(cos.reshape(B * S, half), (1, reps))
    sin = jnp.tile(sin.reshape(B * S, half), (1, reps))
    return cos, sin


def _proj_kernel(x_ref, w_ref, o_ref, *, scale):
    acc = jnp.dot(x_ref[...], w_ref[...], preferred_element_type=F32)
    if scale != 1.0:
        acc = acc * scale
    o_ref[...] = acc.astype(o_ref.dtype)


def _proj(xb, w, out_dtype, scale=1.0, tm=1024):
    T, D = xb.shape
    N = w.shape[1]
    tn = min(N, 512)
    tm = min(tm, T)
    return pl.pallas_call(
        functools.partial(_proj_kernel, scale=scale),
        grid=(T // tm, N // tn),
        in_specs=[pl.BlockSpec((tm, D), lambda i, j: (i, 0)),
                  pl.BlockSpec((D, tn), lambda i, j: (0, j))],
        out_specs=pl.BlockSpec((tm, tn), lambda i, j: (i, j)),
        out_shape=jax.ShapeDtypeStruct((T, N), out_dtype),
        compiler_params=_cparams(2),
        name="proj",
    )(xb, w)


def _proj_t_kernel(wT_ref, x_ref, o_ref, *, scale):
    acc = lax.dot_general(wT_ref[...], x_ref[...], (((1,), (1,)), ((), ())), preferred_element_type=F32)
    if scale != 1.0:
        acc = acc * scale
    o_ref[...] = acc.astype(o_ref.dtype)


def _proj_t(xb, wT, B, S, out_dtype, scale=1.0, tm=1024):
    T, D = xb.shape
    N = wT.shape[0]
    tn = min(N, 512)
    tm = min(tm, S)
    nsb = S // tm
    return pl.pallas_call(
        functools.partial(_proj_t_kernel, scale=scale),
        grid=(T // tm, N // tn),
        in_specs=[pl.BlockSpec((tn, D), lambda i, j: (j, 0)),
                  pl.BlockSpec((tm, D), lambda i, j: (i, 0))],
        out_specs=pl.BlockSpec((pl.Squeezed(), tn, tm), lambda i, j: (i // nsb, j, i % nsb)),
        out_shape=jax.ShapeDtypeStruct((B, N, S), out_dtype),
        compiler_params=_cparams(2),
        name="proj_t",
    )(wT, xb)


def _proj_rope_kernel(x_ref, w_ref, wr_ref, cos_ref, sin_ref, o_ref, *, scale):
    x = x_ref[...]
    a = jnp.dot(x, w_ref[...], preferred_element_type=F32)
    r = jnp.dot(x, wr_ref[...], preferred_element_type=F32)
    out = a * cos_ref[...] + r * sin_ref[...]
    if scale != 1.0:
        out = out * scale
    o_ref[...] = out.astype(o_ref.dtype)


def _proj_rope(xb, w, w_rot, cos, sin, scale, tm=1024):
    T, D = xb.shape
    N = w.shape[1]
    tm = min(tm, T)
    return pl.pallas_call(
        functools.partial(_proj_rope_kernel, scale=scale),
        grid=(T // tm,),
        in_specs=[pl.BlockSpec((tm, D), lambda i: (i, 0)),
                  pl.BlockSpec((D, N), lambda i: (0, 0)),
                  pl.BlockSpec((D, N), lambda i: (0, 0)),
                  pl.BlockSpec((tm, N), lambda i: (i, 0)),
                  pl.BlockSpec((tm, N), lambda i: (i, 0))],
        out_specs=pl.BlockSpec((tm, N), lambda i: (i, 0)),
        out_shape=jax.ShapeDtypeStruct((T, N), BF16),
        compiler_params=_cparams(1),
        name="proj_rope",
    )(xb, w, w_rot, cos, sin)


def _proj_idx_kernel(x_ref, w_ref, g_ref, b_ref, o_ref, *, iw_scale):
    acc = jnp.dot(x_ref[...], w_ref[...], preferred_element_type=F32)
    lane = lax.broadcasted_iota(jnp.int32, acc.shape, 1)
    is_k = lane < IDX_HEAD_DIM
    mu = jnp.sum(jnp.where(is_k, acc, 0.0), axis=-1, keepdims=True) / IDX_HEAD_DIM
    d = acc - mu
    var = jnp.sum(jnp.where(is_k, d * d, 0.0), axis=-1, keepdims=True) / IDX_HEAD_DIM
    ln = d * lax.rsqrt(var + LN_EPS) * g_ref[...] + b_ref[...]
    o_ref[...] = jnp.where(is_k, ln, acc * iw_scale)


def _proj_idx(xb, w_pad, g_pad, b_pad, iw_scale, tm=1024):
    T, D = xb.shape
    tm = min(tm, T)
    return pl.pallas_call(
        functools.partial(_proj_idx_kernel, iw_scale=iw_scale),
        grid=(T // tm,),
        in_specs=[pl.BlockSpec((tm, D), lambda i: (i, 0)),
                  pl.BlockSpec((D, LANES), lambda i: (0, 0)),
                  pl.BlockSpec((1, LANES), lambda i: (0, 0)),
                  pl.BlockSpec((1, LANES), lambda i: (0, 0))],
        out_specs=pl.BlockSpec((tm, LANES), lambda i: (i, 0)),
        out_shape=jax.ShapeDtypeStruct((T, LANES), F32),
        compiler_params=_cparams(1),
        name="proj_idx",
    )(xb, w_pad, g_pad, b_pad)


def _t5_bucket_table():
    n = np.arange(BIAS_TABLE_N)
    max_exact = NUM_BUCKETS // 2
    nf = np.maximum(n, 1).astype(np.float64)
    large = max_exact + (np.log(nf / max_exact) / math.log(MAX_DISTANCE / max_exact)
                         * (NUM_BUCKETS - max_exact)).astype(np.int64)
    large = np.minimum(large, NUM_BUCKETS - 1)
    bucket = np.where(n < max_exact, n, large)
    assert np.all(bucket[FAR_N:] == NUM_BUCKETS - 1) and bucket[FAR_N - 1] != NUM_BUCKETS - 1
    return bucket.astype(np.int32)


def _fold_rows(a, op):
    parts = [a[r:r + SUBLANES] for r in range(0, a.shape[0], SUBLANES)]
    while len(parts) > 1:
        nxt = [op(parts[k], parts[k + 1]) for k in range(0, len(parts) - 1, 2)]
        if len(parts) % 2:
            nxt.append(parts[-1])
        parts = nxt
    return parts[0]


def _attn_kernel(far_ref, iqT_ref, iwT_ref, posq_ref, posqc_ref, posk_ref, ik_ref, qT_ref, k_ref, vT_ref,
                 tbl_ref, o_ref, sc_ref, s_ref, qm_ref, m_ref, l_ref, *acc_refs, k_sel, seq):
    i = pl.program_id(1)
    nch = (i * TQ + TQ + CK - 1) // CK
    q_idx = i * TQ + lax.broadcasted_iota(jnp.int32, (1, TQ), 1)
    kf = float(k_sel)

    def chunk_off(c):
        return pl.multiple_of(c * CK, CK)

    def key_idx(off):
        return off + lax.broadcasted_iota(jnp.int32, (CK, TQ), 0)

    def col_reduce(part, op):
        return op(part, axis=0, keepdims=True)

    iqT = iqT_ref[...]
    iwT = iwT_ref[...]

    def score_body(c, carry):
        mn, mx = carry
        off = chunk_off(c)
        ikc = ik_ref[pl.ds(off, CK), :]
        s = None
        for h in range(IDX_HEADS):
            z = jnp.dot(ikc, iqT[h * IDX_HEAD_DIM:(h + 1) * IDX_HEAD_DIM, :], preferred_element_type=F32)
            t = jnp.maximum(z, 0.0) * iwT[h:h + 1, :]
            s = t if s is None else s + t
        causal = key_idx(off) <= q_idx
        s_lo = jnp.where(causal, s, -jnp.inf)
        sc_ref[pl.ds(off, CK), :] = s_lo
        mn = jnp.minimum(mn, _fold_rows(jnp.where(causal, s, jnp.inf), jnp.minimum))
        mx = jnp.maximum(mx, _fold_rows(s_lo, jnp.maximum))
        return mn, mx

    mn8, mx8 = lax.fori_loop(0, nch, score_body,
                             (jnp.full((SUBLANES, TQ), jnp.inf, F32), jnp.full((SUBLANES, TQ), -jnp.inf, F32)))
    mn = col_reduce(mn8, jnp.min)
    mx = col_reduce(mx8, jnp.max)

    def count(pred_fn):
        def body(c, acc):
            off = chunk_off(c)
            blk = sc_ref[pl.ds(off, CK), :]
            return acc + _fold_rows(jnp.where(pred_fn(blk, off), 1.0, 0.0), jnp.add)
        acc = lax.fori_loop(0, nch, body, jnp.zeros((SUBLANES, TQ), F32))
        return col_reduce(acc, jnp.sum)

    def bisect_round(_, st):
        lo, hi, c_lo = st
        mid = 0.5 * (lo + hi)
        c = count(lambda blk, off: blk >= mid)
        ok = c >= kf
        return jnp.where(ok, mid, lo), jnp.where(ok, hi, mid), jnp.where(ok, c, c_lo)

    c_all = (q_idx + 1).astype(F32)
    lo, hi, c_lo = lax.fori_loop(0, BISECT_ROUNDS, bisect_round, (mn, mx, c_all))

    def min_ge_body(c, acc):
        blk = sc_ref[pl.ds(chunk_off(c), CK), :]
        return jnp.minimum(acc, _fold_rows(jnp.where(blk >= lo, blk, jnp.inf), jnp.minimum))

    cur0 = col_reduce(lax.fori_loop(0, nch, min_ge_body, jnp.full((SUBLANES, TQ), jnp.inf, F32)), jnp.min)

    def walk_cond(st):
        return st[3] > 0.0

    def walk_body(st):
        cur, c_ge, _, _ = st

        def body(c, carry):
            cnt, nxt = carry
            blk = sc_ref[pl.ds(chunk_off(c), CK), :]
            gt = blk > cur
            cnt = cnt + _fold_rows(jnp.where(gt, 1.0, 0.0), jnp.add)
            nxt = jnp.minimum(nxt, _fold_rows(jnp.where(gt, blk, jnp.inf), jnp.minimum))
            return cnt, nxt

        cnt, nxt = lax.fori_loop(0, nch, body,
                                 (jnp.zeros((SUBLANES, TQ), F32), jnp.full((SUBLANES, TQ), jnp.inf, F32)))
        c_gt = col_reduce(cnt, jnp.sum)
        nxt = col_reduce(nxt, jnp.min)
        adv = c_gt >= kf
        cur = jnp.where(adv, nxt, cur)
        c_ge = jnp.where(adv, c_gt, c_ge)
        return cur, c_ge, c_gt, jnp.max(jnp.where(adv, 1.0, 0.0))

    tau, c_ge, c_gt, _ = lax.while_loop(
        walk_cond, walk_body, (cur0, c_lo, jnp.zeros((1, TQ), F32), jnp.float32(1.0)))

    need = kf - c_gt
    trim = (c_ge - c_gt) > need

    def tie_bisect():
        def rnd(_, st):
            jlo, jhi = st
            mid = (jlo + jhi) >> 1
            c = count(lambda blk, off: (blk == tau) & (key_idx(off) <= mid))
            ok = c >= need
            return jnp.where(ok, jlo, mid), jnp.where(ok, mid, jhi)
        n_rounds = max(1, int(math.ceil(math.log2(seq))))
        _, jhi = lax.fori_loop(0, n_rounds, rnd,
                               (jnp.full((1, TQ), -1, jnp.int32), jnp.full((1, TQ), seq - 1, jnp.int32)))
        return jnp.where(trim, jhi, seq)

    jstar = lax.cond(jnp.max(jnp.where(trim, 1.0, 0.0)) > 0.0, tie_bisect,
                     lambda: jnp.full((1, TQ), seq, jnp.int32))

    def mask_body(c, _):
        off = chunk_off(c)
        blk = sc_ref[pl.ds(off, CK), :]
        sel = (blk > tau) | ((blk == tau) & (key_idx(off) <= jstar))
        sc_ref[pl.ds(off, CK), :] = jnp.where(sel, 0.0, NEG)
        return 0

    lax.fori_loop(0, nch, mask_body, 0)

    m_ref[...] = jnp.full(m_ref.shape, NEG, F32)
    l_ref[...] = jnp.zeros(l_ref.shape, F32)
    for acc in acc_refs:
        acc[...] = jnp.zeros(acc.shape, F32)

    rowi = lax.broadcasted_iota(jnp.int32, (LANES, TQ), 0)
    for h in range(ATTN_HEADS):
        pair, sub = divmod(h, 2)
        qp = qT_ref[pair * LANES:(pair + 1) * LANES, :]
        in_head = (rowi >= sub * ATTN_HEAD_DIM) & (rowi < (sub + 1) * ATTN_HEAD_DIM)
        qm_ref[h] = jnp.where(in_head, qp, jnp.zeros_like(qp))

    pq_row = posq_ref[...]
    pq_col = posqc_ref[...]
    pq_min = jnp.min(pq_row)
    n_sub = CK // LANES

    def attn_chunk(c, _):
        off = chunk_off(c)
        mb = sc_ref[pl.ds(off, CK), :]
        pk_row = posk_ref[:, pl.ds(off, CK)]

        def run(const_bias):
            shift = [far_ref[h] if const_bias else 0.0 for h in range(ATTN_HEADS)]
            if not const_bias:
                n_qk = jnp.clip(pq_col - pk_row, 0, BIAS_TABLE_N - 1).astype(F32)
                n_kq = [n_qk[:, j * LANES:(j + 1) * LANES].T.astype(jnp.int32) for j in range(n_sub)]
            m_prev = m_ref[...]
            m_cur = []
            for h in range(ATTN_HEADS):
                pair = h // 2
                kc = k_ref[pl.ds(off, CK), pair * LANES:(pair + 1) * LANES]
                s = jnp.dot(kc, qm_ref[h], preferred_element_type=F32) + mb
                if not const_bias:
                    tb = jnp.broadcast_to(tbl_ref[h:h + 1, :], (LANES, BIAS_TABLE_N))
                    s = s + jnp.concatenate(
                        [jnp.take_along_axis(tb, n_kq[j], axis=1) for j in range(n_sub)], axis=0)
                s_ref[h] = s
                m_cur.append(col_reduce(_fold_rows(s, jnp.maximum), jnp.max) + shift[h])
            m_new = jnp.maximum(m_prev, jnp.concatenate(m_cur, axis=0))
            alpha = jnp.exp(m_prev - m_new)
            l_cur = []
            for h in range(ATTN_HEADS):
                p = jnp.exp(s_ref[h] - (m_new[h:h + 1, :] - shift[h]))
                l_cur.append(col_reduce(_fold_rows(p, jnp.add), jnp.sum))
                vTh = vT_ref[h * ATTN_HEAD_DIM:(h + 1) * ATTN_HEAD_DIM, pl.ds(off, CK)]
                acc = acc_refs[h]
                acc[...] = alpha[h:h + 1, :] * acc[...] + jnp.dot(vTh, p.astype(BF16),
                                                                  preferred_element_type=F32)
            l_ref[...] = alpha * l_ref[...] + jnp.concatenate(l_cur, axis=0)
            m_ref[...] = m_new

        is_far = (pq_min - jnp.max(pk_row)) >= FAR_N
        lax.cond(is_far, lambda: run(True), lambda: run(False))
        return 0

    lax.fori_loop(0, nch, attn_chunk, 0)

    outT = jnp.concatenate([acc_refs[h][...] / l_ref[h:h + 1, :] for h in range(ATTN_HEADS)], axis=0)
    o_ref[...] = outT.T.astype(o_ref.dtype)


def _sparse_attention(qT, k, vT, iqT, ik, iwT, positions, rel_bias):
    B, S, _ = k.shape
    k_sel = min(TOPK_MAX, S // 4)
    bucket = _t5_bucket_table()
    tbl = rel_bias[bucket].T.astype(F32)
    far = rel_bias[NUM_BUCKETS - 1].astype(F32)
    pos_row = positions.reshape(B, 1, S)
    pos_col = positions.reshape(B, S, 1)
    sq = pl.Squeezed()
    return pl.pallas_call(
        functools.partial(_attn_kernel, k_sel=k_sel, seq=S),
        grid=(B, S // TQ),
        in_specs=[pl.BlockSpec(memory_space=pltpu.SMEM),
                  pl.BlockSpec((sq, IDX_Q_W, TQ), lambda b, i: (b, 0, i)),
                  pl.BlockSpec((sq, SUBLANES, TQ), lambda b, i: (b, 0, i)),
                  pl.BlockSpec((sq, 1, TQ), lambda b, i: (b, 0, i)),
                  pl.BlockSpec((sq, TQ, 1), lambda b, i: (b, i, 0)),
                  pl.BlockSpec((sq, 1, S), lambda b, i: (b, 0, 0)),
                  pl.BlockSpec((sq, S, IDX_HEAD_DIM), lambda b, i: (b, 0, 0)),
                  pl.BlockSpec((sq, ATTN_W, TQ), lambda b, i: (b, 0, i)),
                  pl.BlockSpec((sq, S, ATTN_W), lambda b, i: (b, 0, 0)),
                  pl.BlockSpec((sq, ATTN_W, S), lambda b, i: (b, 0, 0)),
                  pl.BlockSpec((ATTN_HEADS, BIAS_TABLE_N), lambda b, i: (0, 0))],
        out_specs=pl.BlockSpec((sq, TQ, ATTN_W), lambda b, i: (b, i, 0)),
        out_shape=jax.ShapeDtypeStruct((B, S, ATTN_W), BF16),
        scratch_shapes=[pltpu.VMEM((S, TQ), F32),
                        pltpu.VMEM((ATTN_HEADS, CK, TQ), F32),
                        pltpu.VMEM((ATTN_HEADS, LANES, TQ), BF16),
                        pltpu.VMEM((ATTN_HEADS, TQ), F32),
                        pltpu.VMEM((ATTN_HEADS, TQ), F32)]
                       + [pltpu.VMEM((ATTN_HEAD_DIM, TQ), F32)] * ATTN_HEADS,
        compiler_params=_cparams(2),
        name="sparse_attention",
    )(far, iqT, iwT, pos_row, pos_col, pos_row, ik, qT, k, vT, tbl)


def _retention_kernel(gch_ref, q_ref, kT_ref, v_ref, g_ref, decay_ref, xi_ref, zeta_ref, o_ref, r_ref):
    @pl.when(pl.program_id(1) == 0)
    def _():
        r_ref[...] = jnp.zeros(r_ref.shape, F32)

    C = RET_CHUNK
    lane = lax.broadcasted_iota(jnp.int32, (C, LANES), 1)
    for pair in range(RET_HEADS // 2):
        q_pair = q_ref[:, pair * LANES:(pair + 1) * LANES]
        kT_pair = kT_ref[pair * LANES:(pair + 1) * LANES, :]
        r_pair = jnp.concatenate([r_ref[2 * pair], r_ref[2 * pair + 1]], axis=0).astype(BF16)
        for sub in range(2):
            h = 2 * pair + sub
            in_head = (lane >= sub * RET_QK_DIM) & (lane < (sub + 1) * RET_QK_DIM)
            qm = jnp.where(in_head, q_pair, jnp.zeros_like(q_pair))
            v_h = v_ref[:, h * RET_V_DIM:(h + 1) * RET_V_DIM]
            inner = jnp.dot(qm, kT_pair, preferred_element_type=F32) * decay_ref[h]
            o = (jnp.dot(inner.astype(BF16), v_h, preferred_element_type=F32)
                 + jnp.dot(qm, r_pair, preferred_element_type=F32) * xi_ref[h])
            kz = (kT_pair[sub * RET_QK_DIM:(sub + 1) * RET_QK_DIM, :].astype(F32) * zeta_ref[h]).astype(BF16)
            r_ref[h] = r_ref[h] * gch_ref[h] + jnp.dot(kz, v_h, preferred_element_type=F32)
            mu = jnp.mean(o, axis=-1, keepdims=True)
            d = o - mu
            var = jnp.mean(d * d, axis=-1, keepdims=True)
            hn = d * lax.rsqrt(var + LN_EPS)
            g = g_ref[:, h * RET_V_DIM:(h + 1) * RET_V_DIM]
            silu = g / (1.0 + jnp.exp(-g))
            o_ref[:, h * RET_V_DIM:(h + 1) * RET_V_DIM] = (silu * hn).astype(o_ref.dtype)


def _retention(q, kT, v, g):
    B, S, _ = q.shape
    C = RET_CHUNK
    H = RET_HEADS
    gamma = 1.0 - 2.0 ** (-5.0 - jnp.arange(H, dtype=F32))
    log_g = jnp.log(gamma)
    n = jnp.arange(C, dtype=F32)
    diff = n[:, None] - n[None, :]
    decay_in = jnp.where(diff[None] >= 0, jnp.exp(log_g[:, None, None] * jnp.maximum(diff, 0.0)[None]), 0.0)
    xi = jnp.exp(log_g[None, :] * (n[:, None] + 1.0))
    zeta = jnp.exp(log_g[None, :] * (C - 1.0 - n[:, None]))
    g_chunk = jnp.exp(log_g * C)
    xi_b = jnp.broadcast_to(xi.T[:, :, None], (H, C, RET_V_DIM))
    zeta_r = zeta.T.reshape(H, 1, C)
    sq = pl.Squeezed()
    return pl.pallas_call(
        _retention_kernel,
        grid=(B, S // C),
        in_specs=[pl.BlockSpec(memory_space=pltpu.SMEM),
                  pl.BlockSpec((sq, C, RET_QK_W), lambda b, i: (b, i, 0)),
                  pl.BlockSpec((sq, RET_QK_W, C), lambda b, i: (b, 0, i)),
                  pl.BlockSpec((sq, C, RET_V_W), lambda b, i: (b, i, 0)),
                  pl.BlockSpec((sq, C, RET_V_W), lambda b, i: (b, i, 0)),
                  pl.BlockSpec((H, C, C), lambda b, i: (0, 0, 0)),
                  pl.BlockSpec((H, C, RET_V_DIM), lambda b, i: (0, 0, 0)),
                  pl.BlockSpec((H, 1, C), lambda b, i: (0, 0, 0))],
        out_specs=pl.BlockSpec((sq, C, RET_V_W), lambda b, i: (b, i, 0)),
        out_shape=jax.ShapeDtypeStruct((B, S, RET_V_W), BF16),
        scratch_shapes=[pltpu.VMEM((H, RET_QK_DIM, RET_V_DIM), F32)],
        compiler_params=_cparams(2),
        name="retention",
    )(g_chunk, q, kT, v, g, decay_in, xi_b, zeta_r)


def _layer_norm(z, g, b):
    mu = jnp.mean(z, axis=-1, keepdims=True)
    d = z - mu
    var = jnp.mean(d * d, axis=-1, keepdims=True)
    return d * lax.rsqrt(var + LN_EPS) * g + b


def _merge_kernel(x_ref, ya_ref, yr_ref, ga_ref, gr_ref, wa_ref, wr_ref, wo_ref, g_ref, b_ref,
                  x1_ref, x1b_ref):
    a = jnp.dot(ya_ref[...], wa_ref[...], preferred_element_type=F32)
    r = jnp.dot(yr_ref[...], wr_ref[...], preferred_element_type=F32)
    sig_a = 1.0 / (1.0 + jnp.exp(-ga_ref[...]))
    sig_r = 1.0 / (1.0 + jnp.exp(-gr_ref[...]))
    h = sig_a * a + sig_r * r
    mix = jnp.dot(h.astype(BF16), wo_ref[...], preferred_element_type=F32)
    x1 = _layer_norm(DEEPNORM_ALPHA * x_ref[...] + mix, g_ref[...], b_ref[...])
    x1_ref[...] = x1
    x1b_ref[...] = x1.astype(BF16)


def _merge(x, ya, yr, ga, gr, wa, wr, wo, g, b, tm=512):
    T, D = x.shape
    tm = min(tm, T)
    row = lambda i: (i, 0)
    fixed = lambda i: (0, 0)
    return pl.pallas_call(
        _merge_kernel,
        grid=(T // tm,),
        in_specs=[pl.BlockSpec((tm, D), row),
                  pl.BlockSpec((tm, ya.shape[1]), row),
                  pl.BlockSpec((tm, yr.shape[1]), row),
                  pl.BlockSpec((tm, D), row),
                  pl.BlockSpec((tm, D), row),
                  pl.BlockSpec(wa.shape, fixed),
                  pl.BlockSpec(wr.shape, fixed),
                  pl.BlockSpec(wo.shape, fixed),
                  pl.BlockSpec((1, D), fixed),
                  pl.BlockSpec((1, D), fixed)],
        out_specs=[pl.BlockSpec((tm, D), row), pl.BlockSpec((tm, D), row)],
        out_shape=[jax.ShapeDtypeStruct((T, D), F32), jax.ShapeDtypeStruct((T, D), BF16)],
        compiler_params=_cparams(1),
        name="merge",
    )(x, ya, yr, ga, gr, wa, wr, wo, g, b)


def _ffn_kernel(x1b_ref, x1_ref, wu_ref, wd_ref, g_ref, b_ref, o_ref, acc_ref):
    f = pl.program_id(1)

    @pl.when(f == 0)
    def _():
        acc_ref[...] = jnp.zeros(acc_ref.shape, F32)

    hid = jnp.maximum(jnp.dot(x1b_ref[...], wu_ref[...], preferred_element_type=F32), 0.0)
    acc_ref[...] += jnp.dot((hid * hid).astype(BF16), wd_ref[...], preferred_element_type=F32)

    @pl.when(f == pl.num_programs(1) - 1)
    def _():
        o_ref[...] = _layer_norm(DEEPNORM_ALPHA * x1_ref[...] + acc_ref[...], g_ref[...], b_ref[...])


def _ffn(x1b, x1, wu, wd, g, b, tm=512, tf=1024):
    T, D = x1.shape
    F = wu.shape[1]
    tm = min(tm, T)
    return pl.pallas_call(
        _ffn_kernel,
        grid=(T // tm, F // tf),
        in_specs=[pl.BlockSpec((tm, D), lambda i, f: (i, 0)),
                  pl.BlockSpec((tm, D), lambda i, f: (i, 0)),
                  pl.BlockSpec((D, tf), lambda i, f: (0, f)),
                  pl.BlockSpec((tf, D), lambda i, f: (f, 0)),
                  pl.BlockSpec((1, D), lambda i, f: (0, 0)),
                  pl.BlockSpec((1, D), lambda i, f: (0, 0))],
        out_specs=pl.BlockSpec((tm, D), lambda i, f: (i, 0)),
        out_shape=jax.ShapeDtypeStruct((T, D), F32),
        scratch_shapes=[pltpu.VMEM((tm, D), F32)],
        compiler_params=_cparams(2),
        name="ffn",
    )(x1b, x1, wu, wd, g, b)


def _rot_half_weight(w):
    D, N = w.shape
    half = RET_QK_DIM // 2
    wh = w.reshape(D, N // RET_QK_DIM, 2, half)
    return jnp.stack([-wh[:, :, 1], wh[:, :, 0]], axis=2).reshape(D, N)


def kernel(x, positions, w_in, rel_bias, idx_k_ln_g, idx_k_ln_b, w_attn_branch, w_ret_branch,
           w_out, ln_mix_g, ln_mix_b, w_up, w_down, ln_ffn_g, ln_ffn_b):
    B, S, D = x.shape
    T = B * S
    sizes = (ATTN_W, ATTN_W, ATTN_W, IDX_Q_W, IDX_HEAD_DIM, IDX_HEADS,
             RET_QK_W, RET_QK_W, RET_V_W, RET_V_W, D, D)
    offs = [0] + [int(o) for o in np.cumsum(sizes)]
    cos, sin = _rope_tables(positions)
    xf = x.reshape(T, D)
    for l in range(DEPTH):
        wl = w_in[l]
        cols = [wl[:, offs[k]:offs[k + 1]] for k in range(len(sizes))]
        (w_qa, w_ka, w_va, w_iq, w_ik, w_iw, w_qr, w_kr, w_vr, w_gr, w_ga, w_gtr) = cols
        xb = xf.astype(BF16)

        qT_a = _proj_t(xb, w_qa.T.astype(BF16), B, S, BF16, scale=ATTN_HEAD_DIM ** -0.5)
        k_a = _proj(xb, w_ka.astype(BF16), BF16).reshape(B, S, ATTN_W)
        vT_a = _proj_t(xb, w_va.T.astype(BF16), B, S, BF16)
        iqT = _proj_t(xb, w_iq.T.astype(BF16), B, S, BF16)
        pad = LANES - IDX_HEAD_DIM - IDX_HEADS
        w_idx = jnp.concatenate([w_ik, w_iw, jnp.zeros((D, pad), F32)], axis=1).astype(BF16)
        g_pad = jnp.concatenate([idx_k_ln_g[l], jnp.zeros((LANES - IDX_HEAD_DIM,), F32)]).reshape(1, LANES)
        b_pad = jnp.concatenate([idx_k_ln_b[l], jnp.zeros((LANES - IDX_HEAD_DIM,), F32)]).reshape(1, LANES)
        idx = _proj_idx(xb, w_idx, g_pad, b_pad,
                        (IDX_HEAD_DIM ** -0.5) * (IDX_HEADS ** -0.5)).reshape(B, S, LANES)
        ik = idx[:, :, :IDX_HEAD_DIM].astype(BF16)
        iwT = jnp.swapaxes(idx[:, :, IDX_HEAD_DIM:IDX_HEAD_DIM + SUBLANES], 1, 2)
        y_a = _sparse_attention(qT_a, k_a, vT_a, iqT, ik, iwT, positions, rel_bias)

        q_r = _proj_rope(xb, w_qr.astype(BF16), _rot_half_weight(w_qr).astype(BF16), cos, sin, 1.0)
        k_r = _proj_rope(xb, w_kr.astype(BF16), _rot_half_weight(w_kr).astype(BF16), cos, sin,
                         RET_QK_DIM ** -0.5)
        v_r = _proj(xb, w_vr.astype(BF16), BF16).reshape(B, S, RET_V_W)
        g_r = _proj(xb, w_gr.astype(BF16), F32).reshape(B, S, RET_V_W)
        kT_r = jnp.swapaxes(k_r.reshape(B, S, RET_QK_W), 1, 2)
        y_r = _retention(q_r.reshape(B, S, RET_QK_W), kT_r, v_r, g_r)

        gate_a = _proj(xb, w_ga.astype(BF16), F32)
        gate_r = _proj(xb, w_gtr.astype(BF16), F32)
        x1, x1b = _merge(xf, y_a.reshape(T, ATTN_W), y_r.reshape(T, RET_V_W), gate_a, gate_r,
                         w_attn_branch[l].astype(BF16), w_ret_branch[l].astype(BF16),
                         w_out[l].astype(BF16), ln_mix_g[l].reshape(1, D), ln_mix_b[l].reshape(1, D))
        xf = _ffn(x1b, x1, w_up[l].astype(BF16), w_down[l].astype(BF16),
                  ln_ffn_g[l].reshape(1, D), ln_ffn_b[l].reshape(1, D))
    return xf.reshape(B, S, D)
```

```python
import functools
import math

import numpy as np
import jax
import jax.numpy as jnp
from jax import lax
from jax.experimental import pallas as pl
from jax.experimental.pallas import tpu as pltpu

F32 = jnp.float32
BF16 = jnp.bfloat16

ATTN_HEADS = 8
ATTN_HEAD_DIM = 64
ATTN_W = ATTN_HEADS * ATTN_HEAD_DIM
IDX_HEADS = 4
IDX_HEAD_DIM = 64
IDX_Q_W = IDX_HEADS * IDX_HEAD_DIM
TOPK_MAX = 256
RET_HEADS = 8
RET_QK_DIM = 64
RET_V_DIM = 128
RET_QK_W = RET_HEADS * RET_QK_DIM
RET_V_W = RET_HEADS * RET_V_DIM
RET_CHUNK = 128
ROPE_BASE = 10000.0
NUM_BUCKETS = 32
MAX_DISTANCE = 128
LN_EPS = 1e-5
DEPTH = 1
DEEPNORM_ALPHA = (2.0 * DEPTH) ** 0.25

LANES = 128
SUBLANES = 8
VMEM_LIMIT = 56 * 1024 * 1024

TQ = 128
CK = 512
NEG = -1e30
NO_KEY = 1e30
ALL_KEYS = float(2 ** 30)
BISECT_ROUNDS = 18
BIAS_TABLE_N = 128
FAR_N = 113


def _cparams(n_grid):
    return pltpu.CompilerParams(
        dimension_semantics=("arbitrary",) * n_grid,
        vmem_limit_bytes=VMEM_LIMIT)


def _trig_kernel(pos_ref, inv_ref, cos_ref, sin_ref):
    ang = pos_ref[...] * inv_ref[...]
    cos_ref[...] = jnp.cos(ang)
    sin_ref[...] = jnp.sin(ang)


def _rope_tables(positions):
    B, S = positions.shape
    half = RET_QK_DIM // 2
    inv = ROPE_BASE ** (-jnp.arange(half, dtype=F32) / half)
    per_row = LANES // half
    rows = B * S // per_row
    pos_e = jnp.repeat(positions.astype(F32).reshape(rows, per_row), half, axis=1)
    inv_e = jnp.tile(inv, per_row).reshape(1, LANES)
    tr = min(rows, 1024)
    cos, sin = pl.pallas_call(
        _trig_kernel,
        grid=(rows // tr,),
        in_specs=[pl.BlockSpec((tr, LANES), lambda i: (i, 0)),
                  pl.BlockSpec((1, LANES), lambda i: (0, 0))],
        out_specs=[pl.BlockSpec((tr, LANES), lambda i: (i, 0))] * 2,
        out_shape=[jax.ShapeDtypeStruct((rows, LANES), F32)] * 2,
        compiler_params=_cparams(1),
        name="rope_tables",
    )(pos_e, inv_e)
    reps = RET_QK_W // half
    cos = jnp.tile(cos.reshape(B * S, half), (1, reps))
    sin = jnp.tile(sin.reshape(B * S, half), (1, reps))
    return cos, sin


def _proj_kernel(x_ref, w_ref, o_ref, *, scale):
    acc = jnp.dot(x_ref[...], w_ref[...], preferred_element_type=F32)
    if scale != 1.0:
        acc = acc * scale
    o_ref[...] = acc.astype(o_ref.dtype)


def _proj(xb, w, out_dtype, scale=1.0, tm=1024):
    T, D = xb.shape
    N = w.shape[1]
    tn = min(N, 512)
    tm = min(tm, T)
    return pl.pallas_call(
        functools.partial(_proj_kernel, scale=scale),
        grid=(T // tm, N // tn),
        in_specs=[pl.BlockSpec((tm, D), lambda i, j: (i, 0)),
                  pl.BlockSpec((D, tn), lambda i, j: (0, j))],
        out_specs=pl.BlockSpec((tm, tn), lambda i, j: (i, j)),
        out_shape=jax.ShapeDtypeStruct((T, N), out_dtype),
        compiler_params=_cparams(2),
        name="proj",
    )(xb, w)


def _proj_t_kernel(wT_ref, x_ref, o_ref, *, scale):
    acc = lax.dot_general(wT_ref[...], x_ref[...], (((1,), (1,)), ((), ())), preferred_element_type=F32)
    if scale != 1.0:
        acc = acc * scale
    o_ref[...] = acc.astype(o_ref.dtype)


def _proj_t(xb, wT, B, S, out_dtype, scale=1.0, tm=1024):
    T, D = xb.shape
    N = wT.shape[0]
    tn = min(N, 512)
    tm = min(tm, S)
    nsb = S // tm
    return pl.pallas_call(
        functools.partial(_proj_t_kernel, scale=scale),
        grid=(T // tm, N // tn),
        in_specs=[pl.BlockSpec((tn, D), lambda i, j: (j, 0)),
                  pl.BlockSpec((tm, D), lambda i, j: (i, 0))],
        out_specs=pl.BlockSpec((pl.Squeezed(), tn, tm), lambda i, j: (i // nsb, j, i % nsb)),
        out_shape=jax.ShapeDtypeStruct((B, N, S), out_dtype),
        compiler_params=_cparams(2),
        name="proj_t",
    )(wT, xb)


def _proj_rope_kernel(x_ref, w_ref, wr_ref, cos_ref, sin_ref, o_ref, *, scale):
    x = x_ref[...]
    a = jnp.dot(x, w_ref[...], preferred_element_type=F32)
    r = jnp.dot(x, wr_ref[...], preferred_element_type=F32)
    out = a * cos_ref[...] + r * sin_ref[...]
    if scale != 1.0:
        out = out * scale
    o_ref[...] = out.astype(o_ref.dtype)


def _proj_rope(xb, w, w_rot, cos, sin, scale, tm=1024):
    T, D = xb.shape
    N = w.shape[1]
    tm = min(tm, T)
    return pl.pallas_call(
        functools.partial(_proj_rope_kernel, scale=scale),
        grid=(T // tm,),
        in_specs=[pl.BlockSpec((tm, D), lambda i: (i, 0)),
                  pl.BlockSpec((D, N), lambda i: (0, 0)),
                  pl.BlockSpec((D, N), lambda i: (0, 0)),
                  pl.BlockSpec((tm, N), lambda i: (i, 0)),
                  pl.BlockSpec((tm, N), lambda i: (i, 0))],
        out_specs=pl.BlockSpec((tm, N), lambda i: (i, 0)),
        out_shape=jax.ShapeDtypeStruct((T, N), BF16),
        compiler_params=_cparams(1),
        name="proj_rope",
    )(xb, w, w_rot, cos, sin)


def _proj_idx_kernel(x_ref, w_ref, g_ref, b_ref, o_ref, *, iw_scale):
    acc = jnp.dot(x_ref[...], w_ref[...], preferred_element_type=F32)
    lane = lax.broadcasted_iota(jnp.int32, acc.shape, 1)
    is_k = lane < IDX_HEAD_DIM
    mu = jnp.sum(jnp.where(is_k, acc, 0.0), axis=-1, keepdims=True) / IDX_HEAD_DIM
    d = acc - mu
    var = jnp.sum(jnp.where(is_k, d * d, 0.0), axis=-1, keepdims=True) / IDX_HEAD_DIM
    ln = d * lax.rsqrt(var + LN_EPS) * g_ref[...] + b_ref[...]
    o_ref[...] = jnp.where(is_k, ln, acc * iw_scale)


def _proj_idx(xb, w_pad, g_pad, b_pad, iw_scale, tm=1024):
    T, D = xb.shape
    tm = min(tm, T)
    return pl.pallas_call(
        functools.partial(_proj_idx_kernel, iw_scale=iw_scale),
        grid=(T // tm,),
        in_specs=[pl.BlockSpec((tm, D), lambda i: (i, 0)),
                  pl.BlockSpec((D, LANES), lambda i: (0, 0)),
                  pl.BlockSpec((1, LANES), lambda i: (0, 0)),
                  pl.BlockSpec((1, LANES), lambda i: (0, 0))],
        out_specs=pl.BlockSpec((tm, LANES), lambda i: (i, 0)),
        out_shape=jax.ShapeDtypeStruct((T, LANES), F32),
        compiler_params=_cparams(1),
        name="proj_idx",
    )(xb, w_pad, g_pad, b_pad)


def _t5_bucket_table():
    n = np.arange(BIAS_TABLE_N)
    max_exact = NUM_BUCKETS // 2
    nf = np.maximum(n, 1).astype(np.float64)
    large = max_exact + (np.log(nf / max_exact) / math.log(MAX_DISTANCE / max_exact)
                         * (NUM_BUCKETS - max_exact)).astype(np.int64)
    large = np.minimum(large, NUM_BUCKETS - 1)
    bucket = np.where(n < max_exact, n, large)
    assert np.all(bucket[FAR_N:] == NUM_BUCKETS - 1) and bucket[FAR_N - 1] != NUM_BUCKETS - 1
    return bucket.astype(np.int32)


def _fold_rows(a, op):
    parts = [a[r:r + SUBLANES] for r in range(0, a.shape[0], SUBLANES)]
    while len(parts) > 1:
        nxt = [op(parts[k], parts[k + 1]) for k in range(0, len(parts) - 1, 2)]
        if len(parts) % 2:
            nxt.append(parts[-1])
        parts = nxt
    return parts[0]


def _attn_kernel(far_ref, iqT_ref, iwT_ref, posq_ref, posqc_ref, posk_ref, ik_ref, qT_ref, k_ref, vT_ref,
                 tbl_ref, o_ref, sc_ref, s_ref, b_ref, qm_ref, m_ref, l_ref, *acc_refs, k_sel, seq):
    i = pl.program_id(1)
    nch = (i * TQ + TQ + CK - 1) // CK
    q_idx = i * TQ + lax.broadcasted_iota(jnp.int32, (1, TQ), 1)
    kf = float(k_sel)

    def chunk_off(c):
        return pl.multiple_of(c * CK, CK)

    def key_idx(off):
        return off + lax.broadcasted_iota(jnp.int32, (CK, TQ), 0)

    def col_reduce(part, op):
        return op(part, axis=0, keepdims=True)

    iqT = iqT_ref[...]
    iwT = iwT_ref[...]

    def score_body(c, carry):
        mn, mx = carry
        off = chunk_off(c)
        ikc = ik_ref[pl.ds(off, CK), :]
        s = None
        for h in range(IDX_HEADS):
            z = jnp.dot(ikc, iqT[h * IDX_HEAD_DIM:(h + 1) * IDX_HEAD_DIM, :], preferred_element_type=F32)
            t = jnp.maximum(z, 0.0) * iwT[h:h + 1, :]
            s = t if s is None else s + t
        causal = key_idx(off) <= q_idx
        s_lo = jnp.where(causal, s, -jnp.inf)
        sc_ref[pl.ds(off, CK), :] = s_lo
        mn = jnp.minimum(mn, _fold_rows(jnp.where(causal, s, jnp.inf), jnp.minimum))
        mx = jnp.maximum(mx, _fold_rows(s_lo, jnp.maximum))
        return mn, mx

    mn8, mx8 = lax.fori_loop(0, nch, score_body,
                             (jnp.full((SUBLANES, TQ), jnp.inf, F32), jnp.full((SUBLANES, TQ), -jnp.inf, F32)))
    mn = col_reduce(mn8, jnp.min)
    mx = col_reduce(mx8, jnp.max)

    def count(pred_fn):
        def body(c, acc):
            off = chunk_off(c)
            blk = sc_ref[pl.ds(off, CK), :]
            return acc + _fold_rows(jnp.where(pred_fn(blk, off), 1.0, 0.0), jnp.add)
        acc = lax.fori_loop(0, nch, body, jnp.zeros((SUBLANES, TQ), F32))
        return col_reduce(acc, jnp.sum)

    def bisect_round(_, st):
        lo, hi, c_lo = st
        mid = 0.5 * (lo + hi)
        c = count(lambda blk, off: blk >= mid)
        ok = c >= kf
        return jnp.where(ok, mid, lo), jnp.where(ok, hi, mid), jnp.where(ok, c, c_lo)

    c_all = (q_idx + 1).astype(F32)
    lo, hi, c_lo = lax.fori_loop(0, BISECT_ROUNDS, bisect_round, (mn, mx, c_all))

    def min_ge_body(c, acc):
        blk = sc_ref[pl.ds(chunk_off(c), CK), :]
        return jnp.minimum(acc, _fold_rows(jnp.where(blk >= lo, blk, jnp.inf), jnp.minimum))

    cur0 = col_reduce(lax.fori_loop(0, nch, min_ge_body, jnp.full((SUBLANES, TQ), jnp.inf, F32)), jnp.min)

    def walk_cond(st):
        return st[3] > 0.0

    def walk_body(st):
        cur, c_ge, _, _ = st

        def body(c, carry):
            cnt, nxt = carry
            blk = sc_ref[pl.ds(chunk_off(c), CK), :]
            gt = blk > cur
            cnt = cnt + _fold_rows(jnp.where(gt, 1.0, 0.0), jnp.add)
            nxt = jnp.minimum(nxt, _fold_rows(jnp.where(gt, blk, jnp.inf), jnp.minimum))
            return cnt, nxt

        cnt, nxt = lax.fori_loop(0, nch, body,
                                 (jnp.zeros((SUBLANES, TQ), F32), jnp.full((SUBLANES, TQ), jnp.inf, F32)))
        c_gt = col_reduce(cnt, jnp.sum)
        nxt = col_reduce(nxt, jnp.min)
        adv = c_gt >= kf
        cur = jnp.where(adv, nxt, cur)
        c_ge = jnp.where(adv, c_gt, c_ge)
        return cur, c_ge, c_gt, jnp.max(jnp.where(adv, 1.0, 0.0))

    tau, c_ge, c_gt, _ = lax.while_loop(
        walk_cond, walk_body, (cur0, c_lo, jnp.zeros((1, TQ), F32), jnp.float32(1.0)))

    trim = c_ge > kf

    def mask_with_ties():
        def encode_body(c, _):
            off = chunk_off(c)
            blk = sc_ref[pl.ds(off, CK), :]
            tie_code = jnp.where(blk == tau, key_idx(off).astype(F32), NO_KEY)
            sc_ref[pl.ds(off, CK), :] = jnp.where(blk > tau, -1.0, tie_code)
            return 0

        lax.fori_loop(0, nch, encode_body, 0)

        def rnd(_, st):
            jlo, jhi = st
            mid = jnp.floor(0.5 * (jlo + jhi))
            ok = count(lambda blk, off: blk <= mid) >= kf
            return jnp.where(ok, jlo, mid), jnp.where(ok, mid, jhi)

        n_keys = (i + 1) * TQ
        n_rounds = sum(((n_keys - 1) >> b > 0).astype(jnp.int32) for b in range(max(1, seq.bit_length())))
        _, jhi = lax.fori_loop(0, n_rounds, rnd,
                               (jnp.full((1, TQ), -1.0, F32), jnp.full((1, TQ), 1.0, F32) * (n_keys - 1)))
        jstar = jnp.where(trim, jhi, ALL_KEYS)

        def mask_body(c, _):
            off = chunk_off(c)
            sc_ref[pl.ds(off, CK), :] = jnp.where(sc_ref[pl.ds(off, CK), :] <= jstar, 0.0, NEG)
            return 0

        lax.fori_loop(0, nch, mask_body, 0)

    def mask_plain():
        def mask_body(c, _):
            off = chunk_off(c)
            sc_ref[pl.ds(off, CK), :] = jnp.where(sc_ref[pl.ds(off, CK), :] >= tau, 0.0, NEG)
            return 0

        lax.fori_loop(0, nch, mask_body, 0)

    lax.cond(jnp.max(jnp.where(trim, 1.0, 0.0)) > 0.0, mask_with_ties, mask_plain)

    m_ref[...] = jnp.full(m_ref.shape, NEG, F32)
    l_ref[...] = jnp.zeros(l_ref.shape, F32)
    for acc in acc_refs:
        acc[...] = jnp.zeros(acc.shape, F32)

    rowi = lax.broadcasted_iota(jnp.int32, (LANES, TQ), 0)
    for h in range(ATTN_HEADS):
        pair, sub = divmod(h, 2)
        qp = qT_ref[pair * LANES:(pair + 1) * LANES, :]
        in_head = (rowi >= sub * ATTN_HEAD_DIM) & (rowi < (sub + 1) * ATTN_HEAD_DIM)
        qm_ref[h] = jnp.where(in_head, qp, jnp.zeros_like(qp))

    pq_row = posq_ref[...]
    pq_col = posqc_ref[...]
    pq_min = jnp.min(pq_row)
    n_sub = CK // LANES

    def attn_chunk(c, _):
        off = chunk_off(c)
        mb = sc_ref[pl.ds(off, CK), :]
        pk_row = posk_ref[:, pl.ds(off, CK)]

        def run(const_bias):
            shift = [far_ref[h] if const_bias else 0.0 for h in range(ATTN_HEADS)]
            if not const_bias:
                for j in range(n_sub):
                    rows = slice(j * LANES, (j + 1) * LANES)
                    pk_sub = pk_row[:, rows]
                    all_far = (pq_min - jnp.max(pk_sub)) >= FAR_N
                    all_masked = (off + j * LANES) > (i * TQ + TQ - 1)

                    def fill_const(rows=rows):
                        for h in range(ATTN_HEADS):
                            b_ref[h, rows, :] = jnp.full((LANES, TQ), far_ref[h], F32)

                    def fill_lookup(rows=rows, pk_sub=pk_sub):
                        n_qk = jnp.clip(pq_col - pk_sub, 0, BIAS_TABLE_N - 1).astype(F32)
                        n_kq = n_qk.T.astype(jnp.int32)
                        for h in range(ATTN_HEADS):
                            tb = jnp.broadcast_to(tbl_ref[h:h + 1, :], (LANES, BIAS_TABLE_N))
                            b_ref[h, rows, :] = jnp.take_along_axis(tb, n_kq, axis=1)

                    lax.cond(all_far | all_masked, fill_const, fill_lookup)
            m_prev = m_ref[...]
            m_cur = []
            for h in range(ATTN_HEADS):
                pair = h // 2
                kc = k_ref[pl.ds(off, CK), pair * LANES:(pair + 1) * LANES]
                s = jnp.dot(kc, qm_ref[h], preferred_element_type=F32) + mb
                if not const_bias:
                    s = s + b_ref[h]
                s_ref[h] = s
                m_cur.append(col_reduce(_fold_rows(s, jnp.maximum), jnp.max) + shift[h])
            m_new = jnp.maximum(m_prev, jnp.concatenate(m_cur, axis=0))
            alpha = jnp.exp(m_prev - m_new)
            l_cur = []
            for h in range(ATTN_HEADS):
                p = jnp.exp(s_ref[h] - (m_new[h:h + 1, :] - shift[h]))
                l_cur.append(col_reduce(_fold_rows(p, jnp.add), jnp.sum))
                vTh = vT_ref[h * ATTN_HEAD_DIM:(h + 1) * ATTN_HEAD_DIM, pl.ds(off, CK)]
                acc = acc_refs[h]
                acc[...] = alpha[h:h + 1, :] * acc[...] + jnp.dot(vTh, p.astype(BF16),
                                                                  preferred_element_type=F32)
            l_ref[...] = alpha * l_ref[...] + jnp.concatenate(l_cur, axis=0)
            m_ref[...] = m_new

        is_far = (pq_min - jnp.max(pk_row)) >= FAR_N
        lax.cond(is_far, lambda: run(True), lambda: run(False))
        return 0

    lax.fori_loop(0, nch, attn_chunk, 0)

    outT = jnp.concatenate([acc_refs[h][...] / l_ref[h:h + 1, :] for h in range(ATTN_HEADS)], axis=0)
    o_ref[...] = outT.T.astype(o_ref.dtype)


def _sparse_attention(qT, k, vT, iqT, ik, iwT, positions, rel_bias):
    B, S, _ = k.shape
    k_sel = min(TOPK_MAX, S // 4)
    bucket = _t5_bucket_table()
    tbl = rel_bias[bucket].T.astype(F32)
    far = rel_bias[NUM_BUCKETS - 1].astype(F32)
    pos_row = positions.reshape(B, 1, S)
    pos_col = positions.reshape(B, S, 1)
    sq = pl.Squeezed()
    return pl.pallas_call(
        functools.partial(_attn_kernel, k_sel=k_sel, seq=S),
        grid=(B, S // TQ),
        in_specs=[pl.BlockSpec(memory_space=pltpu.SMEM),
                  pl.BlockSpec((sq, IDX_Q_W, TQ), lambda b, i: (b, 0, i)),
                  pl.BlockSpec((sq, SUBLANES, TQ), lambda b, i: (b, 0, i)),
                  pl.BlockSpec((sq, 1, TQ), lambda b, i: (b, 0, i)),
                  pl.BlockSpec((sq, TQ, 1), lambda b, i: (b, i, 0)),
                  pl.BlockSpec((sq, 1, S), lambda b, i: (b, 0, 0)),
                  pl.BlockSpec((sq, S, IDX_HEAD_DIM), lambda b, i: (b, 0, 0)),
                  pl.BlockSpec((sq, ATTN_W, TQ), lambda b, i: (b, 0, i)),
                  pl.BlockSpec((sq, S, ATTN_W), lambda b, i: (b, 0, 0)),
                  pl.BlockSpec((sq, ATTN_W, S), lambda b, i: (b, 0, 0)),
                  pl.BlockSpec((ATTN_HEADS, BIAS_TABLE_N), lambda b, i: (0, 0))],
        out_specs=pl.BlockSpec((sq, TQ, ATTN_W), lambda b, i: (b, i, 0)),
        out_shape=jax.ShapeDtypeStruct((B, S, ATTN_W), BF16),
        scratch_shapes=[pltpu.VMEM((S, TQ), F32),
                        pltpu.VMEM((ATTN_HEADS, CK, TQ), F32),
                        pltpu.VMEM((ATTN_HEADS, CK, TQ), F32),
                        pltpu.VMEM((ATTN_HEADS, LANES, TQ), BF16),
                        pltpu.VMEM((ATTN_HEADS, TQ), F32),
                        pltpu.VMEM((ATTN_HEADS, TQ), F32)]
                       + [pltpu.VMEM((ATTN_HEAD_DIM, TQ), F32)] * ATTN_HEADS,
        compiler_params=_cparams(2),
        name="sparse_attention",
    )(far, iqT, iwT, pos_row, pos_col, pos_row, ik, qT, k, vT, tbl)


def _retention_kernel(gch_ref, q_ref, kT_ref, v_ref, g_ref, decay_ref, xi_ref, zeta_ref, o_ref, r_ref):
    @pl.when(pl.program_id(1) == 0)
    def _():
        r_ref[...] = jnp.zeros(r_ref.shape, F32)

    C = RET_CHUNK
    lane = lax.broadcasted_iota(jnp.int32, (C, LANES), 1)
    for pair in range(RET_HEADS // 2):
        q_pair = q_ref[:, pair * LANES:(pair + 1) * LANES]
        kT_pair = kT_ref[pair * LANES:(pair + 1) * LANES, :]
        r_pair = jnp.concatenate([r_ref[2 * pair], r_ref[2 * pair + 1]], axis=0).astype(BF16)
        for sub in range(2):
            h = 2 * pair + sub
            in_head = (lane >= sub * RET_QK_DIM) & (lane < (sub + 1) * RET_QK_DIM)
            qm = jnp.where(in_head, q_pair, jnp.zeros_like(q_pair))
            v_h = v_ref[:, h * RET_V_DIM:(h + 1) * RET_V_DIM]
            inner = jnp.dot(qm, kT_pair, preferred_element_type=F32) * decay_ref[h]
            o = (jnp.dot(inner.astype(BF16), v_h, preferred_element_type=F32)
                 + jnp.dot(qm, r_pair, preferred_element_type=F32) * xi_ref[h])
            kz = (kT_pair[sub * RET_QK_DIM:(sub + 1) * RET_QK_DIM, :].astype(F32) * zeta_ref[h]).astype(BF16)
            r_ref[h] = r_ref[h] * gch_ref[h] + jnp.dot(kz, v_h, preferred_element_type=F32)
            mu = jnp.mean(o, axis=-1, keepdims=True)
            d = o - mu
            var = jnp.mean(d * d, axis=-1, keepdims=True)
            hn = d * lax.rsqrt(var + LN_EPS)
            g = g_ref[:, h * RET_V_DIM:(h + 1) * RET_V_DIM]
            silu = g / (1.0 + jnp.exp(-g))
            o_ref[:, h * RET_V_DIM:(h + 1) * RET_V_DIM] = (silu * hn).astype(o_ref.dtype)


def _retention(q, kT, v, g):
    B, S, _ = q.shape
    C = RET_CHUNK
    H = RET_HEADS
    gamma = 1.0 - 2.0 ** (-5.0 - jnp.arange(H, dtype=F32))
    log_g = jnp.log(gamma)
    n = jnp.arange(C, dtype=F32)
    diff = n[:, None] - n[None, :]
    decay_in = jnp.where(diff[None] >= 0, jnp.exp(log_g[:, None, None] * jnp.maximum(diff, 0.0)[None]), 0.0)
    xi = jnp.exp(log_g[None, :] * (n[:, None] + 1.0))
    zeta = jnp.exp(log_g[None, :] * (C - 1.0 - n[:, None]))
    g_chunk = jnp.exp(log_g * C)
    xi_b = jnp.broadcast_to(xi.T[:, :, None], (H, C, RET_V_DIM))
    zeta_r = zeta.T.reshape(H, 1, C)
    sq = pl.Squeezed()
    return pl.pallas_call(
        _retention_kernel,
        grid=(B, S // C),
        in_specs=[pl.BlockSpec(memory_space=pltpu.SMEM),
                  pl.BlockSpec((sq, C, RET_QK_W), lambda b, i: (b, i, 0)),
                  pl.BlockSpec((sq, RET_QK_W, C), lambda b, i: (b, 0, i)),
                  pl.BlockSpec((sq, C, RET_V_W), lambda b, i: (b, i, 0)),
                  pl.BlockSpec((sq, C, RET_V_W), lambda b, i: (b, i, 0)),
                  pl.BlockSpec((H, C, C), lambda b, i: (0, 0, 0)),
                  pl.BlockSpec((H, C, RET_V_DIM), lambda b, i: (0, 0, 0)),
                  pl.BlockSpec((H, 1, C), lambda b, i: (0, 0, 0))],
        out_specs=pl.BlockSpec((sq, C, RET_V_W), lambda b, i: (b, i, 0)),
        out_shape=jax.ShapeDtypeStruct((B, S, RET_V_W), BF16),
        scratch_shapes=[pltpu.VMEM((H, RET_QK_DIM, RET_V_DIM), F32)],
        compiler_params=_cparams(2),
        name="retention",
    )(g_chunk, q, kT, v, g, decay_in, xi_b, zeta_r)


def _layer_norm(z, g, b):
    mu = jnp.mean(z, axis=-1, keepdims=True)
    d = z - mu
    var = jnp.mean(d * d, axis=-1, keepdims=True)
    return d * lax.rsqrt(var + LN_EPS) * g + b


def _merge_kernel(x_ref, ya_ref, yr_ref, ga_ref, gr_ref, wa_ref, wr_ref, wo_ref, g_ref, b_ref,
                  x1_ref, x1b_ref):
    a = jnp.dot(ya_ref[...], wa_ref[...], preferred_element_type=F32)
    r = jnp.dot(yr_ref[...], wr_ref[...], preferred_element_type=F32)
    sig_a = 1.0 / (1.0 + jnp.exp(-ga_ref[...]))
    sig_r = 1.0 / (1.0 + jnp.exp(-gr_ref[...]))
    h = sig_a * a + sig_r * r
    mix = jnp.dot(h.astype(BF16), wo_ref[...], preferred_element_type=F32)
    x1 = _layer_norm(DEEPNORM_ALPHA * x_ref[...] + mix, g_ref[...], b_ref[...])
    x1_ref[...] = x1
    x1b_ref[...] = x1.astype(BF16)


def _merge(x, ya, yr, ga, gr, wa, wr, wo, g, b, tm=512):
    T, D = x.shape
    tm = min(tm, T)
    row = lambda i: (i, 0)
    fixed = lambda i: (0, 0)
    return pl.pallas_call(
        _merge_kernel,
        grid=(T // tm,),
        in_specs=[pl.BlockSpec((tm, D), row),
                  pl.BlockSpec((tm, ya.shape[1]), row),
                  pl.BlockSpec((tm, yr.shape[1]), row),
                  pl.BlockSpec((tm, D), row),
                  pl.BlockSpec((tm, D), row),
                  pl.BlockSpec(wa.shape, fixed),
                  pl.BlockSpec(wr.shape, fixed),
                  pl.BlockSpec(wo.shape, fixed),
                  pl.BlockSpec((1, D), fixed),
                  pl.BlockSpec((1, D), fixed)],
        out_specs=[pl.BlockSpec((tm, D), row), pl.BlockSpec((tm, D), row)],
        out_shape=[jax.ShapeDtypeStruct((T, D), F32), jax.ShapeDtypeStruct((T, D), BF16)],
        compiler_params=_cparams(1),
        name="merge",
    )(x, ya, yr, ga, gr, wa, wr, wo, g, b)


def _ffn_kernel(x1b_ref, x1_ref, wu_ref, wd_ref, g_ref, b_ref, o_ref, acc_ref):
    f = pl.program_id(1)

    @pl.when(f == 0)
    def _():
        acc_ref[...] = jnp.zeros(acc_ref.shape, F32)

    hid = jnp.maximum(jnp.dot(x1b_ref[...], wu_ref[...], preferred_element_type=F32), 0.0)
    acc_ref[...] += jnp.dot((hid * hid).astype(BF16), wd_ref[...], preferred_element_type=F32)

    @pl.when(f == pl.num_programs(1) - 1)
    def _():
        o_ref[...] = _layer_norm(DEEPNORM_ALPHA * x1_ref[...] + acc_ref[...], g_ref[...], b_ref[...])


def _ffn(x1b, x1, wu, wd, g, b, tm=512, tf=1024):
    T, D = x1.shape
    F = wu.shape[1]
    tm = min(tm, T)
    return pl.pallas_call(
        _ffn_kernel,
        grid=(T // tm, F // tf),
        in_specs=[pl.BlockSpec((tm, D), lambda i, f: (i, 0)),
                  pl.BlockSpec((tm, D), lambda i, f: (i, 0)),
                  pl.BlockSpec((D, tf), lambda i, f: (0, f)),
                  pl.BlockSpec((tf, D), lambda i, f: (f, 0)),
                  pl.BlockSpec((1, D), lambda i, f: (0, 0)),
                  pl.BlockSpec((1, D), lambda i, f: (0, 0))],
        out_specs=pl.BlockSpec((tm, D), lambda i, f: (i, 0)),
        out_shape=jax.ShapeDtypeStruct((T, D), F32),
        scratch_shapes=[pltpu.VMEM((tm, D), F32)],
        compiler_params=_cparams(2),
        name="ffn",
    )(x1b, x1, wu, wd, g, b)


def _rot_half_weight(w):
    D, N = w.shape
    half = RET_QK_DIM // 2
    wh = w.reshape(D, N // RET_QK_DIM, 2, half)
    return jnp.stack([-wh[:, :, 1], wh[:, :, 0]], axis=2).reshape(D, N)


def kernel(x, positions, w_in, rel_bias, idx_k_ln_g, idx_k_ln_b, w_attn_branch, w_ret_branch,
           w_out, ln_mix_g, ln_mix_b, w_up, w_down, ln_ffn_g, ln_ffn_b):
    B, S, D = x.shape
    T = B * S
    sizes = (ATTN_W, ATTN_W, ATTN_W, IDX_Q_W, IDX_HEAD_DIM, IDX_HEADS,
             RET_QK_W, RET_QK_W, RET_V_W, RET_V_W, D, D)
    offs = [0] + [int(o) for o in np.cumsum(sizes)]
    cos, sin = _rope_tables(positions)
    xf = x.reshape(T, D)
    for l in range(DEPTH):
        wl = w_in[l]
        cols = [wl[:, offs[k]:offs[k + 1]] for k in range(len(sizes))]
        (w_qa, w_ka, w_va, w_iq, w_ik, w_iw, w_qr, w_kr, w_vr, w_gr, w_ga, w_gtr) = cols
        xb = xf.astype(BF16)

        qT_a = _proj_t(xb, w_qa.T.astype(BF16), B, S, BF16, scale=ATTN_HEAD_DIM ** -0.5)
        k_a = _proj(xb, w_ka.astype(BF16), BF16).reshape(B, S, ATTN_W)
        vT_a = _proj_t(xb, w_va.T.astype(BF16), B, S, BF16)
        iqT = _proj_t(xb, w_iq.T.astype(BF16), B, S, BF16)
        pad = LANES - IDX_HEAD_DIM - IDX_HEADS
        w_idx = jnp.concatenate([w_ik, w_iw, jnp.zeros((D, pad), F32)], axis=1).astype(BF16)
        g_pad = jnp.concatenate([idx_k_ln_g[l], jnp.zeros((LANES - IDX_HEAD_DIM,), F32)]).reshape(1, LANES)
        b_pad = jnp.concatenate([idx_k_ln_b[l], jnp.zeros((LANES - IDX_HEAD_DIM,), F32)]).reshape(1, LANES)
        idx = _proj_idx(xb, w_idx, g_pad, b_pad,
                        (IDX_HEAD_DIM ** -0.5) * (IDX_HEADS ** -0.5)).reshape(B, S, LANES)
        ik = idx[:, :, :IDX_HEAD_DIM].astype(BF16)
        iwT = jnp.swapaxes(idx[:, :, IDX_HEAD_DIM:IDX_HEAD_DIM + SUBLANES], 1, 2)
        y_a = _sparse_attention(qT_a, k_a, vT_a, iqT, ik, iwT, positions, rel_bias)

        q_r = _proj_rope(xb, w_qr.astype(BF16), _rot_half_weight(w_qr).astype(BF16), cos, sin, 1.0)
        k_r = _proj_rope(xb, w_kr.astype(BF16), _rot_half_weight(w_kr).astype(BF16), cos, sin,
                         RET_QK_DIM ** -0.5)
        v_r = _proj(xb, w_vr.astype(BF16), BF16).reshape(B, S, RET_V_W)
        g_r = _proj(xb, w_gr.astype(BF16), F32).reshape(B, S, RET_V_W)
        kT_r = jnp.swapaxes(k_r.reshape(B, S, RET_QK_W), 1, 2)
        y_r = _retention(q_r.reshape(B, S, RET_QK_W), kT_r, v_r, g_r)

        gate_a = _proj(xb, w_ga.astype(BF16), F32)
        gate_r = _proj(xb, w_gtr.astype(BF16), F32)
        x1, x1b = _merge(xf, y_a.reshape(T, ATTN_W), y_r.reshape(T, RET_V_W), gate_a, gate_r,
                         w_attn_branch[l].astype(BF16), w_ret_branch[l].astype(BF16),
                         w_out[l].astype(BF16), ln_mix_g[l].reshape(1, D), ln_mix_b[l].reshape(1, D))
        xf = _ffn(x1b, x1, w_up[l].astype(BF16), w_down[l].astype(BF16),
                  ln_ffn_g[l].reshape(1, D), ln_ffn_b[l].reshape(1, D))
    return xf.reshape(B, S, D)
```

```python
import functools
import math

import numpy as np
import jax
import jax.numpy as jnp
from jax import lax
from jax.experimental import pallas as pl
from jax.experimental.pallas import tpu as pltpu

F32 = jnp.float32
BF16 = jnp.bfloat16

ATTN_HEADS = 8
ATTN_HEAD_DIM = 64
ATTN_W = ATTN_HEADS * ATTN_HEAD_DIM
IDX_HEADS = 4
IDX_HEAD_DIM = 64
IDX_Q_W = IDX_HEADS * IDX_HEAD_DIM
TOPK_MAX = 256
RET_HEADS = 8
RET_QK_DIM = 64
RET_V_DIM = 128
RET_QK_W = RET_HEADS * RET_QK_DIM
RET_V_W = RET_HEADS * RET_V_DIM
RET_CHUNK = 128
ROPE_BASE = 10000.0
NUM_BUCKETS = 32
MAX_DISTANCE = 128
LN_EPS = 1e-5
DEPTH = 1
DEEPNORM_ALPHA = (2.0 * DEPTH) ** 0.25

LANES = 128
SUBLANES = 8
VMEM_LIMIT = 56 * 1024 * 1024

TQ = 128
CK = 512
NEG = -1e30
NO_KEY = 1e30
ALL_KEYS = float(2 ** 30)
BISECT_ROUNDS = 18
BIAS_TABLE_N = 128
FAR_N = 113


def _cparams(n_grid):
    return pltpu.CompilerParams(
        dimension_semantics=("arbitrary",) * n_grid,
        vmem_limit_bytes=VMEM_LIMIT)


def _trig_kernel(pos_ref, inv_ref, cos_ref, sin_ref):
    ang = pos_ref[...] * inv_ref[...]
    cos_ref[...] = jnp.cos(ang)
    sin_ref[...] = jnp.sin(ang)


def _rope_tables(positions):
    B, S = positions.shape
    half = RET_QK_DIM // 2
    inv = ROPE_BASE ** (-jnp.arange(half, dtype=F32) / half)
    per_row = LANES // half
    rows = B * S // per_row
    pos_e = jnp.repeat(positions.astype(F32).reshape(rows, per_row), half, axis=1)
    inv_e = jnp.tile(inv, per_row).reshape(1, LANES)
    tr = min(rows, 1024)
    cos, sin = pl.pallas_call(
        _trig_kernel,
        grid=(rows // tr,),
        in_specs=[pl.BlockSpec((tr, LANES), lambda i: (i, 0)),
                  pl.BlockSpec((1, LANES), lambda i: (0, 0))],
        out_specs=[pl.BlockSpec((tr, LANES), lambda i: (i, 0))] * 2,
        out_shape=[jax.ShapeDtypeStruct((rows, LANES), F32)] * 2,
        compiler_params=_cparams(1),
        name="rope_tables",
    )(pos_e, inv_e)
    cos = jnp.tile(cos.reshape(B * S, half), (1, per_row))
    sin = jnp.tile(sin.reshape(B * S, half), (1, per_row))
    return cos, sin


def _proj_kernel(x_ref, w_ref, o_ref):
    o_ref[...] = jnp.dot(x_ref[...], w_ref[...], preferred_element_type=F32).astype(o_ref.dtype)


def _proj(xb, w, out_dtype, tm=1024, tn=512):
    T, D = xb.shape
    N = w.shape[1]
    tn = min(N, tn)
    tm = min(tm, T)
    return pl.pallas_call(
        _proj_kernel,
        grid=(T // tm, N // tn),
        in_specs=[pl.BlockSpec((tm, D), lambda i, j: (i, 0)),
                  pl.BlockSpec((D, tn), lambda i, j: (0, j))],
        out_specs=pl.BlockSpec((tm, tn), lambda i, j: (i, j)),
        out_shape=jax.ShapeDtypeStruct((T, N), out_dtype),
        compiler_params=_cparams(2),
        name="proj",
    )(xb, w)


def _proj_gates_kernel(x_ref, w_ref, o_ref):
    acc = jnp.dot(x_ref[...], w_ref[...], preferred_element_type=F32)
    sig = 1.0 / (1.0 + jnp.exp(-acc))
    o_ref[...] = jnp.where(pl.program_id(1) == 0, acc * sig, sig).astype(o_ref.dtype)


def _proj_gates(xb, w, tn, tm=1024):
    T, D = xb.shape
    N = w.shape[1]
    tm = min(tm, T)
    return pl.pallas_call(
        _proj_gates_kernel,
        grid=(T // tm, N // tn),
        in_specs=[pl.BlockSpec((tm, D), lambda i, j: (i, 0)),
                  pl.BlockSpec((D, tn), lambda i, j: (0, j))],
        out_specs=pl.BlockSpec((tm, tn), lambda i, j: (i, j)),
        out_shape=jax.ShapeDtypeStruct((T, N), BF16),
        compiler_params=_cparams(2),
        name="proj_gates",
    )(xb, w)


def _proj_t_kernel(wT_ref, x_ref, o_ref):
    acc = lax.dot_general(wT_ref[...], x_ref[...], (((1,), (1,)), ((), ())), preferred_element_type=F32)
    o_ref[...] = acc.astype(o_ref.dtype)


def _proj_t(xb, wT, B, S, out_dtype, tm=1024, tn=256):
    T, D = xb.shape
    N = wT.shape[0]
    tn = min(N, tn)
    tm = min(tm, S)
    nsb = S // tm
    return pl.pallas_call(
        _proj_t_kernel,
        grid=(T // tm, N // tn),
        in_specs=[pl.BlockSpec((tn, D), lambda i, j: (j, 0)),
                  pl.BlockSpec((tm, D), lambda i, j: (i, 0))],
        out_specs=pl.BlockSpec((pl.Squeezed(), tn, tm), lambda i, j: (i // nsb, j, i % nsb)),
        out_shape=jax.ShapeDtypeStruct((B, N, S), out_dtype),
        compiler_params=_cparams(2),
        name="proj_t",
    )(wT, xb)


def _proj_rope_kernel(x_ref, w_ref, wr_ref, cos_ref, sin_ref, o_ref):
    x = x_ref[...]
    a = jnp.dot(x, w_ref[...], preferred_element_type=F32)
    r = jnp.dot(x, wr_ref[...], preferred_element_type=F32)
    reps = a.shape[1] // LANES
    cos = jnp.concatenate([cos_ref[...]] * reps, axis=1)
    sin = jnp.concatenate([sin_ref[...]] * reps, axis=1)
    o_ref[...] = (a * cos + r * sin).astype(o_ref.dtype)


def _proj_rope(xb, w, w_rot, cos, sin, tm=1024, tn=512):
    T, D = xb.shape
    N = w.shape[1]
    tm = min(tm, T)
    return pl.pallas_call(
        _proj_rope_kernel,
        grid=(T // tm, N // tn),
        in_specs=[pl.BlockSpec((tm, D), lambda i, j: (i, 0)),
                  pl.BlockSpec((D, tn), lambda i, j: (0, j)),
                  pl.BlockSpec((D, tn), lambda i, j: (0, j)),
                  pl.BlockSpec((tm, LANES), lambda i, j: (i, 0)),
                  pl.BlockSpec((tm, LANES), lambda i, j: (i, 0))],
        out_specs=pl.BlockSpec((tm, tn), lambda i, j: (i, j)),
        out_shape=jax.ShapeDtypeStruct((T, N), BF16),
        compiler_params=_cparams(2),
        name="proj_rope",
    )(xb, w, w_rot, cos, sin)


def _proj_idx_kernel(x_ref, w_ref, g_ref, b_ref, o_ref, *, iw_scale):
    acc = jnp.dot(x_ref[...], w_ref[...], preferred_element_type=F32)
    lane = lax.broadcasted_iota(jnp.int32, acc.shape, 1)
    is_k = lane < IDX_HEAD_DIM
    mu = jnp.sum(jnp.where(is_k, acc, 0.0), axis=-1, keepdims=True) / IDX_HEAD_DIM
    d = acc - mu
    var = jnp.sum(jnp.where(is_k, d * d, 0.0), axis=-1, keepdims=True) / IDX_HEAD_DIM
    ln = d * lax.rsqrt(var + LN_EPS) * g_ref[...] + b_ref[...]
    o_ref[...] = jnp.where(is_k, ln, acc * iw_scale)


def _proj_idx(xb, w_pad, g_pad, b_pad, iw_scale, tm=1024):
    T, D = xb.shape
    tm = min(tm, T)
    return pl.pallas_call(
        functools.partial(_proj_idx_kernel, iw_scale=iw_scale),
        grid=(T // tm,),
        in_specs=[pl.BlockSpec((tm, D), lambda i: (i, 0)),
                  pl.BlockSpec((D, LANES), lambda i: (0, 0)),
                  pl.BlockSpec((1, LANES), lambda i: (0, 0)),
                  pl.BlockSpec((1, LANES), lambda i: (0, 0))],
        out_specs=pl.BlockSpec((tm, LANES), lambda i: (i, 0)),
        out_shape=jax.ShapeDtypeStruct((T, LANES), F32),
        compiler_params=_cparams(1),
        name="proj_idx",
    )(xb, w_pad, g_pad, b_pad)


def _t5_bucket_table():
    n = np.arange(BIAS_TABLE_N)
    max_exact = NUM_BUCKETS // 2
    nf = np.maximum(n, 1).astype(np.float64)
    large = max_exact + (np.log(nf / max_exact) / math.log(MAX_DISTANCE / max_exact)
                         * (NUM_BUCKETS - max_exact)).astype(np.int64)
    large = np.minimum(large, NUM_BUCKETS - 1)
    bucket = np.where(n < max_exact, n, large)
    assert np.all(bucket[FAR_N:] == NUM_BUCKETS - 1) and bucket[FAR_N - 1] != NUM_BUCKETS - 1
    return bucket.astype(np.int32)


def _fold_rows(a, op):
    parts = [a[r:r + SUBLANES] for r in range(0, a.shape[0], SUBLANES)]
    while len(parts) > 1:
        nxt = [op(parts[k], parts[k + 1]) for k in range(0, len(parts) - 1, 2)]
        if len(parts) % 2:
            nxt.append(parts[-1])
        parts = nxt
    return parts[0]


def _attn_kernel(far_ref, iqT_ref, iwT_ref, posq_ref, posqc_ref, posk_ref, ik_ref, qT_ref, k_ref, vT_ref,
                 tbl_ref, o_ref, sc_ref, s_ref, b_ref, qm_ref, m_ref, l_ref, *acc_refs, k_sel, seq):
    i = pl.program_id(1)
    nch = (i * TQ + TQ + CK - 1) // CK
    q_idx = i * TQ + lax.broadcasted_iota(jnp.int32, (1, TQ), 1)
    kf = float(k_sel)

    def chunk_off(c):
        return pl.multiple_of(c * CK, CK)

    def key_idx(off):
        return off + lax.broadcasted_iota(jnp.int32, (CK, TQ), 0)

    def col_reduce(part, op):
        return op(part, axis=0, keepdims=True)

    iqT = iqT_ref[...]
    iwT = iwT_ref[...]

    def score_body(c, carry):
        mn, mx = carry
        off = chunk_off(c)
        ikc = ik_ref[pl.ds(off, CK), :]
        s = None
        for h in range(IDX_HEADS):
            z = jnp.dot(ikc, iqT[h * IDX_HEAD_DIM:(h + 1) * IDX_HEAD_DIM, :], preferred_element_type=F32)
            t = jnp.maximum(z, 0.0) * iwT[h:h + 1, :]
            s = t if s is None else s + t
        causal = key_idx(off) <= q_idx
        s_lo = jnp.where(causal, s, -jnp.inf)
        sc_ref[pl.ds(off, CK), :] = s_lo
        mn = jnp.minimum(mn, _fold_rows(jnp.where(causal, s, jnp.inf), jnp.minimum))
        mx = jnp.maximum(mx, _fold_rows(s_lo, jnp.maximum))
        return mn, mx

    mn8, mx8 = lax.fori_loop(0, nch, score_body,
                             (jnp.full((SUBLANES, TQ), jnp.inf, F32), jnp.full((SUBLANES, TQ), -jnp.inf, F32)))
    mn = col_reduce(mn8, jnp.min)
    mx = col_reduce(mx8, jnp.max)

    def count(pred_fn):
        def body(c, acc):
            off = chunk_off(c)
            blk = sc_ref[pl.ds(off, CK), :]
            return acc + _fold_rows(jnp.where(pred_fn(blk, off), 1.0, 0.0), jnp.add)
        acc = lax.fori_loop(0, nch, body, jnp.zeros((SUBLANES, TQ), F32))
        return col_reduce(acc, jnp.sum)

    def bisect_round(_, st):
        lo, hi, c_lo = st
        mid = 0.5 * (lo + hi)
        c = count(lambda blk, off: blk >= mid)
        ok = c >= kf
        return jnp.where(ok, mid, lo), jnp.where(ok, hi, mid), jnp.where(ok, c, c_lo)

    c_all = (q_idx + 1).astype(F32)
    lo, hi, c_lo = lax.fori_loop(0, BISECT_ROUNDS, bisect_round, (mn, mx, c_all))

    def min_ge_body(c, acc):
        blk = sc_ref[pl.ds(chunk_off(c), CK), :]
        return jnp.minimum(acc, _fold_rows(jnp.where(blk >= lo, blk, jnp.inf), jnp.minimum))

    cur0 = col_reduce(lax.fori_loop(0, nch, min_ge_body, jnp.full((SUBLANES, TQ), jnp.inf, F32)), jnp.min)

    def walk_cond(st):
        return st[3] > 0.0

    def walk_body(st):
        cur, c_ge, _, _ = st

        def body(c, carry):
            cnt, nxt = carry
            blk = sc_ref[pl.ds(chunk_off(c), CK), :]
            gt = blk > cur
            cnt = cnt + _fold_rows(jnp.where(gt, 1.0, 0.0), jnp.add)
            nxt = jnp.minimum(nxt, _fold_rows(jnp.where(gt, blk, jnp.inf), jnp.minimum))
            return cnt, nxt

        cnt, nxt = lax.fori_loop(0, nch, body,
                                 (jnp.zeros((SUBLANES, TQ), F32), jnp.full((SUBLANES, TQ), jnp.inf, F32)))
        c_gt = col_reduce(cnt, jnp.sum)
        nxt = col_reduce(nxt, jnp.min)
        adv = c_gt >= kf
        cur = jnp.where(adv, nxt, cur)
        c_ge = jnp.where(adv, c_gt, c_ge)
        return cur, c_ge, c_gt, jnp.max(jnp.where(adv, 1.0, 0.0))

    tau, c_ge, c_gt, _ = lax.while_loop(
        walk_cond, walk_body, (cur0, c_lo, jnp.zeros((1, TQ), F32), jnp.float32(1.0)))

    trim = c_ge > kf

    def mask_with_ties():
        def encode_body(c, _):
            off = chunk_off(c)
            blk = sc_ref[pl.ds(off, CK), :]
            tie_code = jnp.where(blk == tau, key_idx(off).astype(F32), NO_KEY)
            sc_ref[pl.ds(off, CK), :] = jnp.where(blk > tau, -1.0, tie_code)
            return 0

        lax.fori_loop(0, nch, encode_body, 0)

        def rnd(_, st):
            jlo, jhi = st
            mid = jnp.floor(0.5 * (jlo + jhi))
            ok = count(lambda blk, off: blk <= mid) >= kf
            return jnp.where(ok, jlo, mid), jnp.where(ok, mid, jhi)

        n_keys = (i + 1) * TQ
        n_rounds = sum(((n_keys - 1) >> b > 0).astype(jnp.int32) for b in range(max(1, seq.bit_length())))
        _, jhi = lax.fori_loop(0, n_rounds, rnd,
                               (jnp.full((1, TQ), -1.0, F32), jnp.full((1, TQ), 1.0, F32) * (n_keys - 1)))
        jstar = jnp.where(trim, jhi, ALL_KEYS)

        def mask_body(c, _):
            off = chunk_off(c)
            sc_ref[pl.ds(off, CK), :] = jnp.where(sc_ref[pl.ds(off, CK), :] <= jstar, 0.0, NEG)
            return 0

        lax.fori_loop(0, nch, mask_body, 0)

    def mask_plain():
        def mask_body(c, _):
            off = chunk_off(c)
            sc_ref[pl.ds(off, CK), :] = jnp.where(sc_ref[pl.ds(off, CK), :] >= tau, 0.0, NEG)
            return 0

        lax.fori_loop(0, nch, mask_body, 0)

    lax.cond(jnp.max(jnp.where(trim, 1.0, 0.0)) > 0.0, mask_with_ties, mask_plain)

    m_ref[...] = jnp.full(m_ref.shape, NEG, F32)
    l_ref[...] = jnp.zeros(l_ref.shape, F32)
    for acc in acc_refs:
        acc[...] = jnp.zeros(acc.shape, F32)

    rowi = lax.broadcasted_iota(jnp.int32, (LANES, TQ), 0)
    for h in range(ATTN_HEADS):
        pair, sub = divmod(h, 2)
        qp = qT_ref[pair * LANES:(pair + 1) * LANES, :]
        in_head = (rowi >= sub * ATTN_HEAD_DIM) & (rowi < (sub + 1) * ATTN_HEAD_DIM)
        qm_ref[h] = jnp.where(in_head, qp, jnp.zeros_like(qp))

    pq_row = posq_ref[...]
    pq_col = posqc_ref[...]
    pq_min = jnp.min(pq_row)
    n_sub = CK // LANES

    def attn_chunk(c, _):
        off = chunk_off(c)
        mb = sc_ref[pl.ds(off, CK), :]
        pk_row = posk_ref[:, pl.ds(off, CK)]

        def run(const_bias):
            shift = [far_ref[h] if const_bias else 0.0 for h in range(ATTN_HEADS)]
            if not const_bias:
                for j in range(n_sub):
                    rows = slice(j * LANES, (j + 1) * LANES)
                    pk_sub = pk_row[:, rows]
                    all_far = (pq_min - jnp.max(pk_sub)) >= FAR_N
                    all_masked = (off + j * LANES) > (i * TQ + TQ - 1)

                    def fill_const(rows=rows):
                        for h in range(ATTN_HEADS):
                            b_ref[h, rows, :] = jnp.full((LANES, TQ), far_ref[h], F32)

                    def fill_lookup(rows=rows, pk_sub=pk_sub):
                        n_qk = jnp.clip(pq_col - pk_sub, 0, BIAS_TABLE_N - 1).astype(F32)
                        n_kq = n_qk.T.astype(jnp.int32)
                        for h in range(ATTN_HEADS):
                            tb = jnp.broadcast_to(tbl_ref[h:h + 1, :], (LANES, BIAS_TABLE_N))
                            b_ref[h, rows, :] = jnp.take_along_axis(tb, n_kq, axis=1)

                    lax.cond(all_far | all_masked, fill_const, fill_lookup)
            m_prev = m_ref[...]
            m_cur = []
            for h in range(ATTN_HEADS):
                pair = h // 2
                kc = k_ref[pl.ds(off, CK), pair * LANES:(pair + 1) * LANES]
                s = jnp.dot(kc, qm_ref[h], preferred_element_type=F32) + mb
                if not const_bias:
                    s = s + b_ref[h]
                s_ref[h] = s
                m_cur.append(col_reduce(_fold_rows(s, jnp.maximum), jnp.max) + shift[h])
            m_new = jnp.maximum(m_prev, jnp.concatenate(m_cur, axis=0))
            alpha = jnp.exp(m_prev - m_new)
            l_cur = []
            for h in range(ATTN_HEADS):
                p = jnp.exp(s_ref[h] - (m_new[h:h + 1, :] - shift[h]))
                l_cur.append(col_reduce(_fold_rows(p, jnp.add), jnp.sum))
                vTh = vT_ref[h * ATTN_HEAD_DIM:(h + 1) * ATTN_HEAD_DIM, pl.ds(off, CK)]
                acc = acc_refs[h]
                acc[...] = alpha[h:h + 1, :] * acc[...] + jnp.dot(vTh, p.astype(BF16),
                                                                  preferred_element_type=F32)
            l_ref[...] = alpha * l_ref[...] + jnp.concatenate(l_cur, axis=0)
            m_ref[...] = m_new

        is_far = (pq_min - jnp.max(pk_row)) >= FAR_N
        lax.cond(is_far, lambda: run(True), lambda: run(False))
        return 0

    lax.fori_loop(0, nch, attn_chunk, 0)

    outT = jnp.concatenate([acc_refs[h][...] / l_ref[h:h + 1, :] for h in range(ATTN_HEADS)], axis=0)
    o_ref[...] = outT.T.astype(o_ref.dtype)


def _sparse_attention(fm, tokb, ik, iwT, positions, rel_bias):
    B, S, _ = tokb.shape
    k_sel = min(TOPK_MAX, S // 4)
    bucket = _t5_bucket_table()
    tbl = rel_bias[bucket].T.astype(F32)
    far = rel_bias[NUM_BUCKETS - 1].astype(F32)
    pos_row = positions.reshape(B, 1, S)
    pos_col = positions.reshape(B, S, 1)
    sq = pl.Squeezed()
    iq_blk = (2 * ATTN_W) // IDX_Q_W
    k_blk = (tokb.shape[2] - ATTN_W) // ATTN_W
    return pl.pallas_call(
        functools.partial(_attn_kernel, k_sel=k_sel, seq=S),
        grid=(B, S // TQ),
        in_specs=[pl.BlockSpec(memory_space=pltpu.SMEM),
                  pl.BlockSpec((sq, IDX_Q_W, TQ), lambda b, i: (b, iq_blk, i)),
                  pl.BlockSpec((sq, SUBLANES, TQ), lambda b, i: (b, 0, i)),
                  pl.BlockSpec((sq, 1, TQ), lambda b, i: (b, 0, i)),
                  pl.BlockSpec((sq, TQ, 1), lambda b, i: (b, i, 0)),
                  pl.BlockSpec((sq, 1, S), lambda b, i: (b, 0, 0)),
                  pl.BlockSpec((sq, S, IDX_HEAD_DIM), lambda b, i: (b, 0, 0)),
                  pl.BlockSpec((sq, ATTN_W, TQ), lambda b, i: (b, 0, i)),
                  pl.BlockSpec((sq, S, ATTN_W), lambda b, i: (b, 0, k_blk)),
                  pl.BlockSpec((sq, ATTN_W, S), lambda b, i: (b, 1, 0)),
                  pl.BlockSpec((ATTN_HEADS, BIAS_TABLE_N), lambda b, i: (0, 0))],
        out_specs=pl.BlockSpec((sq, TQ, ATTN_W), lambda b, i: (b, i, 0)),
        out_shape=jax.ShapeDtypeStruct((B, S, ATTN_W), BF16),
        scratch_shapes=[pltpu.VMEM((S, TQ), F32),
                        pltpu.VMEM((ATTN_HEADS, CK, TQ), F32),
                        pltpu.VMEM((ATTN_HEADS, CK, TQ), F32),
                        pltpu.VMEM((ATTN_HEADS, LANES, TQ), BF16),
                        pltpu.VMEM((ATTN_HEADS, TQ), F32),
                        pltpu.VMEM((ATTN_HEADS, TQ), F32)]
                       + [pltpu.VMEM((ATTN_HEAD_DIM, TQ), F32)] * ATTN_HEADS,
        compiler_params=_cparams(2),
        name="sparse_attention",
    )(far, fm, iwT, pos_row, pos_col, pos_row, ik, fm, tokb, fm, tbl)


def _retention_kernel(q_ref, k_ref, v_ref, g_ref, decay_ref, xi_ref, zeta_ref, gch_ref, o_ref, r_ref):
    @pl.when(pl.program_id(1) == 0)
    def _():
        r_ref[...] = jnp.zeros(r_ref.shape, F32)

    C = RET_CHUNK
    lane = lax.broadcasted_iota(jnp.int32, (C, LANES), 1)
    row = lax.broadcasted_iota(jnp.int32, (LANES, RET_V_DIM), 0)
    for pair in range(RET_HEADS // 2):
        q_pair = q_ref[:, pair * LANES:(pair + 1) * LANES]
        k_pair = k_ref[:, pair * LANES:(pair + 1) * LANES]
        v_pair = v_ref[:, 2 * pair * RET_V_DIM:(2 * pair + 2) * RET_V_DIM]
        r_pair = r_ref[pair]
        r_bf = r_pair.astype(BF16)
        for sub in range(2):
            h = 2 * pair + sub
            in_head = (lane >= sub * RET_QK_DIM) & (lane < (sub + 1) * RET_QK_DIM)
            qm = jnp.where(in_head, q_pair, jnp.zeros_like(q_pair))
            v_h = v_pair[:, sub * RET_V_DIM:(sub + 1) * RET_V_DIM]
            inner = lax.dot_general(qm, k_pair, (((1,), (1,)), ((), ())),
                                    preferred_element_type=F32) * decay_ref[h]
            o = (jnp.dot(inner.astype(BF16), v_h, preferred_element_type=F32)
                 + jnp.dot(qm, r_bf, preferred_element_type=F32) * xi_ref[h])
            mu = jnp.mean(o, axis=-1, keepdims=True)
            d = o - mu
            var = jnp.mean(d * d, axis=-1, keepdims=True)
            hn = d * lax.rsqrt(var + LN_EPS)
            gate = g_ref[:, h * RET_V_DIM:(h + 1) * RET_V_DIM].astype(F32)
            o_ref[:, h * RET_V_DIM:(h + 1) * RET_V_DIM] = (gate * hn).astype(o_ref.dtype)
        kz = (k_pair.astype(F32) * zeta_ref[pair]).astype(BF16)
        upd = lax.dot_general(kz, v_pair, (((0,), (0,)), ((), ())), preferred_element_type=F32)
        r_ref[pair] = r_pair * gch_ref[pair] + jnp.where(row < RET_QK_DIM, upd[:, :RET_V_DIM], upd[:, RET_V_DIM:])


def _retention(qk, tokb, gates, B, S):
    C = RET_CHUNK
    H = RET_HEADS
    nc = S // C
    gamma = 1.0 - 2.0 ** (-5.0 - jnp.arange(H, dtype=F32))
    log_g = jnp.log(gamma)
    n = jnp.arange(C, dtype=F32)
    diff = n[:, None] - n[None, :]
    decay_in = jnp.where(diff[None] >= 0, jnp.exp(log_g[:, None, None] * jnp.maximum(diff, 0.0)[None]), 0.0)
    xi = jnp.exp(log_g[None, :] * (n[:, None] + 1.0))
    zeta = jnp.exp(log_g[None, :] * (C - 1.0 - n[:, None]))
    g_chunk = jnp.exp(log_g * C)
    xi_b = jnp.broadcast_to(xi.T[:, :, None], (H, C, RET_V_DIM))
    zeta_b = jnp.repeat(zeta, RET_QK_DIM, axis=1).reshape(C, H // 2, LANES).transpose(1, 0, 2)
    gch_b = jnp.broadcast_to(jnp.repeat(g_chunk, RET_QK_DIM).reshape(H // 2, LANES, 1),
                             (H // 2, LANES, RET_V_DIM))
    return pl.pallas_call(
        _retention_kernel,
        grid=(B, nc),
        in_specs=[pl.BlockSpec((C, RET_QK_W), lambda b, i: (b * nc + i, 0)),
                  pl.BlockSpec((C, RET_QK_W), lambda b, i: (b * nc + i, 1)),
                  pl.BlockSpec((C, RET_V_W), lambda b, i: (b * nc + i, 0)),
                  pl.BlockSpec((C, RET_V_W), lambda b, i: (b * nc + i, 0)),
                  pl.BlockSpec((H, C, C), lambda b, i: (0, 0, 0)),
                  pl.BlockSpec((H, C, RET_V_DIM), lambda b, i: (0, 0, 0)),
                  pl.BlockSpec((H // 2, C, LANES), lambda b, i: (0, 0, 0)),
                  pl.BlockSpec((H // 2, LANES, RET_V_DIM), lambda b, i: (0, 0, 0))],
        out_specs=pl.BlockSpec((C, RET_V_W), lambda b, i: (b * nc + i, 0)),
        out_shape=jax.ShapeDtypeStruct((B * S, RET_V_W), BF16),
        scratch_shapes=[pltpu.VMEM((H // 2, LANES, RET_V_DIM), F32)],
        compiler_params=_cparams(2),
        name="retention",
    )(qk, qk, tokb, gates, decay_in, xi_b, zeta_b, gch_b)


def _layer_norm(z, g, b):
    mu = jnp.mean(z, axis=-1, keepdims=True)
    d = z - mu
    var = jnp.mean(d * d, axis=-1, keepdims=True)
    return d * lax.rsqrt(var + LN_EPS) * g + b


def _merge_kernel(x_ref, ya_ref, yr_ref, ga_ref, gr_ref, wa_ref, wr_ref, wo_ref, g_ref, b_ref,
                  x1_ref, x1b_ref):
    a = jnp.dot(ya_ref[...], wa_ref[...], preferred_element_type=F32)
    r = jnp.dot(yr_ref[...], wr_ref[...], preferred_element_type=F32)
    h = ga_ref[...].astype(F32) * a + gr_ref[...].astype(F32) * r
    mix = jnp.dot(h.astype(BF16), wo_ref[...], preferred_element_type=F32)
    x1 = _layer_norm(DEEPNORM_ALPHA * x_ref[...] + mix, g_ref[...], b_ref[...])
    x1_ref[...] = x1
    x1b_ref[...] = x1.astype(BF16)


def _merge(x, ya, yr, gates, wa, wr, wo, g, b, tm=512):
    T, D = x.shape
    tm = min(tm, T)
    row = lambda i: (i, 0)
    fixed = lambda i: (0, 0)
    return pl.pallas_call(
        _merge_kernel,
        grid=(T // tm,),
        in_specs=[pl.BlockSpec((tm, D), row),
                  pl.BlockSpec((tm, ya.shape[1]), row),
                  pl.BlockSpec((tm, yr.shape[1]), row),
                  pl.BlockSpec((tm, D), lambda i: (i, 1)),
                  pl.BlockSpec((tm, D), lambda i: (i, 2)),
                  pl.BlockSpec(wa.shape, fixed),
                  pl.BlockSpec(wr.shape, fixed),
                  pl.BlockSpec(wo.shape, fixed),
                  pl.BlockSpec((1, D), fixed),
                  pl.BlockSpec((1, D), fixed)],
        out_specs=[pl.BlockSpec((tm, D), row), pl.BlockSpec((tm, D), row)],
        out_shape=[jax.ShapeDtypeStruct((T, D), F32), jax.ShapeDtypeStruct((T, D), BF16)],
        compiler_params=_cparams(1),
        name="merge",
    )(x, ya, yr, gates, gates, wa, wr, wo, g, b)


def _ffn_kernel(x1b_ref, x1_ref, wu_ref, wd_ref, g_ref, b_ref, o_ref, acc_ref):
    f = pl.program_id(1)

    @pl.when(f == 0)
    def _():
        acc_ref[...] = jnp.zeros(acc_ref.shape, F32)

    hid = jnp.maximum(jnp.dot(x1b_ref[...], wu_ref[...], preferred_element_type=F32), 0.0)
    acc_ref[...] += jnp.dot((hid * hid).astype(BF16), wd_ref[...], preferred_element_type=F32)

    @pl.when(f == pl.num_programs(1) - 1)
    def _():
        o_ref[...] = _layer_norm(DEEPNORM_ALPHA * x1_ref[...] + acc_ref[...], g_ref[...], b_ref[...])


def _ffn(x1b, x1, wu, wd, g, b, tm=512, tf=1024):
    T, D = x1.shape
    F = wu.shape[1]
    tm = min(tm, T)
    return pl.pallas_call(
        _ffn_kernel,
        grid=(T // tm, F // tf),
        in_specs=[pl.BlockSpec((tm, D), lambda i, f: (i, 0)),
                  pl.BlockSpec((tm, D), lambda i, f: (i, 0)),
                  pl.BlockSpec((D, tf), lambda i, f: (0, f)),
                  pl.BlockSpec((tf, D), lambda i, f: (f, 0)),
                  pl.BlockSpec((1, D), lambda i, f: (0, 0)),
                  pl.BlockSpec((1, D), lambda i, f: (0, 0))],
        out_specs=pl.BlockSpec((tm, D), lambda i, f: (i, 0)),
        out_shape=jax.ShapeDtypeStruct((T, D), F32),
        scratch_shapes=[pltpu.VMEM((tm, D), F32)],
        compiler_params=_cparams(2),
        name="ffn",
    )(x1b, x1, wu, wd, g, b)


def _rot_half_weight(w):
    D, N = w.shape
    half = RET_QK_DIM // 2
    wh = w.reshape(D, N // RET_QK_DIM, 2, half)
    return jnp.stack([-wh[:, :, 1], wh[:, :, 0]], axis=2).reshape(D, N)


def kernel(x, positions, w_in, rel_bias, idx_k_ln_g, idx_k_ln_b, w_attn_branch, w_ret_branch,
           w_out, ln_mix_g, ln_mix_b, w_up, w_down, ln_ffn_g, ln_ffn_b):
    B, S, D = x.shape
    T = B * S
    sizes = (ATTN_W, ATTN_W, ATTN_W, IDX_Q_W, IDX_HEAD_DIM, IDX_HEADS,
             RET_QK_W, RET_QK_W, RET_V_W, RET_V_W, D, D)
    offs = [0] + [int(o) for o in np.cumsum(sizes)]
    cos, sin = _rope_tables(positions)
    xf = x.reshape(T, D)
    for l in range(DEPTH):
        wl = w_in[l]
        cols = [wl[:, offs[k]:offs[k + 1]] for k in range(len(sizes))]
        (w_qa, w_ka, w_va, w_iq, w_ik, w_iw, w_qr, w_kr, w_vr, w_gr, w_ga, w_gtr) = cols
        xb = xf.astype(BF16)
        w_qa = w_qa * (ATTN_HEAD_DIM ** -0.5)
        w_kr = w_kr * (RET_QK_DIM ** -0.5)

        fm = _proj_t(xb, jnp.concatenate([w_qa, w_va, w_iq], axis=1).T.astype(BF16), B, S, BF16)
        tokb = _proj(xb, jnp.concatenate([w_vr, w_ka], axis=1).astype(BF16), BF16)
        gates = _proj_gates(xb, jnp.concatenate([w_gr, w_ga, w_gtr], axis=1).astype(BF16), tn=D)
        w_rope = jnp.concatenate([w_qr, w_kr], axis=1)
        w_rope_rot = jnp.concatenate([_rot_half_weight(w_qr), _rot_half_weight(w_kr)], axis=1)
        qk_r = _proj_rope(xb, w_rope.astype(BF16), w_rope_rot.astype(BF16), cos, sin)
        pad = LANES - IDX_HEAD_DIM - IDX_HEADS
        w_idx = jnp.concatenate([w_ik, w_iw, jnp.zeros((D, pad), F32)], axis=1).astype(BF16)
        g_pad = jnp.concatenate([idx_k_ln_g[l], jnp.zeros((LANES - IDX_HEAD_DIM,), F32)]).reshape(1, LANES)
        b_pad = jnp.concatenate([idx_k_ln_b[l], jnp.zeros((LANES - IDX_HEAD_DIM,), F32)]).reshape(1, LANES)
        idx = _proj_idx(xb, w_idx, g_pad, b_pad,
                        (IDX_HEAD_DIM ** -0.5) * (IDX_HEADS ** -0.5)).reshape(B, S, LANES)
        ik = idx[:, :, :IDX_HEAD_DIM].astype(BF16)
        iwT = jnp.swapaxes(idx[:, :, IDX_HEAD_DIM:IDX_HEAD_DIM + SUBLANES], 1, 2)

        y_a = _sparse_attention(fm, tokb.reshape(B, S, -1), ik, iwT, positions, rel_bias)
        y_r = _retention(qk_r, tokb, gates, B, S)
        x1, x1b = _merge(xf, y_a.reshape(T, ATTN_W), y_r, gates,
                         w_attn_branch[l].astype(BF16), w_ret_branch[l].astype(BF16),
                         w_out[l].astype(BF16), ln_mix_g[l].reshape(1, D), ln_mix_b[l].reshape(1, D))
        xf = _ffn(x1b, x1, w_up[l].astype(BF16), w_down[l].astype(BF16),
                  ln_ffn_g[l].reshape(1, D), ln_ffn_b[l].reshape(1, D))
    return xf.reshape(B, S, D)
```

```python
import functools
import math

import numpy as np
import jax
import jax.numpy as jnp
from jax import lax
from jax.experimental import pallas as pl
from jax.experimental.pallas import tpu as pltpu

F32 = jnp.float32
BF16 = jnp.bfloat16

ATTN_HEADS = 8
ATTN_HEAD_DIM = 64
ATTN_W = ATTN_HEADS * ATTN_HEAD_DIM
IDX_HEADS = 4
IDX_HEAD_DIM = 64
IDX_Q_W = IDX_HEADS * IDX_HEAD_DIM
TOPK_MAX = 256
RET_HEADS = 8
RET_QK_DIM = 64
RET_V_DIM = 128
RET_QK_W = RET_HEADS * RET_QK_DIM
RET_V_W = RET_HEADS * RET_V_DIM
RET_CHUNK = 128
ROPE_BASE = 10000.0
NUM_BUCKETS = 32
MAX_DISTANCE = 128
LN_EPS = 1e-5
DEPTH = 1
DEEPNORM_ALPHA = (2.0 * DEPTH) ** 0.25

LANES = 128
SUBLANES = 8
VMEM_LIMIT = 56 * 1024 * 1024

TQ = 128
CK = 512
NEG = -1e30
BISECT_ROUNDS = 18
BIAS_TABLE_N = 128
FAR_N = 113


def _cparams(n_grid):
    return pltpu.CompilerParams(
        dimension_semantics=("arbitrary",) * n_grid,
        vmem_limit_bytes=VMEM_LIMIT)


def _trig_kernel(pos_ref, inv_ref, cos_ref, sin_ref):
    ang = pos_ref[...] * inv_ref[...]
    cos_ref[...] = jnp.cos(ang)
    sin_ref[...] = jnp.sin(ang)


def _rope_tables(positions):
    B, S = positions.shape
    half = RET_QK_DIM // 2
    inv = ROPE_BASE ** (-jnp.arange(half, dtype=F32) / half)
    per_row = LANES // half
    rows = B * S // per_row
    pos_e = jnp.repeat(positions.astype(F32).reshape(rows, per_row), half, axis=1)
    inv_e = jnp.tile(inv, per_row).reshape(1, LANES)
    tr = min(rows, 1024)
    cos, sin = pl.pallas_call(
        _trig_kernel,
        grid=(rows // tr,),
        in_specs=[pl.BlockSpec((tr, LANES), lambda i: (i, 0)),
                  pl.BlockSpec((1, LANES), lambda i: (0, 0))],
        out_specs=[pl.BlockSpec((tr, LANES), lambda i: (i, 0))] * 2,
        out_shape=[jax.ShapeDtypeStruct((rows, LANES), F32)] * 2,
        compiler_params=_cparams(1),
        name="rope_tables",
    )(pos_e, inv_e)
    cos = jnp.tile(cos.reshape(B * S, half), (1, per_row))
    sin = jnp.tile(sin.reshape(B * S, half), (1, per_row))
    return cos, sin


def _proj_kernel(x_ref, w_ref, o_ref):
    o_ref[...] = jnp.dot(x_ref[...], w_ref[...], preferred_element_type=F32).astype(o_ref.dtype)


def _proj(xb, w, out_dtype, tm=1024, tn=512):
    T, D = xb.shape
    N = w.shape[1]
    tn = min(N, tn)
    tm = min(tm, T)
    return pl.pallas_call(
        _proj_kernel,
        grid=(T // tm, N // tn),
        in_specs=[pl.BlockSpec((tm, D), lambda i, j: (i, 0)),
                  pl.BlockSpec((D, tn), lambda i, j: (0, j))],
        out_specs=pl.BlockSpec((tm, tn), lambda i, j: (i, j)),
        out_shape=jax.ShapeDtypeStruct((T, N), out_dtype),
        compiler_params=_cparams(2),
        name="proj",
    )(xb, w)


def _proj_gates_kernel(x_ref, w_ref, o_ref):
    acc = jnp.dot(x_ref[...], w_ref[...], preferred_element_type=F32)
    sig = 1.0 / (1.0 + jnp.exp(-acc))
    o_ref[...] = jnp.where(pl.program_id(1) == 0, acc * sig, sig).astype(o_ref.dtype)


def _proj_gates(xb, w, tn, tm=1024):
    T, D = xb.shape
    N = w.shape[1]
    tm = min(tm, T)
    return pl.pallas_call(
        _proj_gates_kernel,
        grid=(T // tm, N // tn),
        in_specs=[pl.BlockSpec((tm, D), lambda i, j: (i, 0)),
                  pl.BlockSpec((D, tn), lambda i, j: (0, j))],
        out_specs=pl.BlockSpec((tm, tn), lambda i, j: (i, j)),
        out_shape=jax.ShapeDtypeStruct((T, N), BF16),
        compiler_params=_cparams(2),
        name="proj_gates",
    )(xb, w)


def _proj_t_kernel(wT_ref, x_ref, o_ref):
    acc = lax.dot_general(wT_ref[...], x_ref[...], (((1,), (1,)), ((), ())), preferred_element_type=F32)
    o_ref[...] = acc.astype(o_ref.dtype)


def _proj_t(xb, wT, B, S, out_dtype, tm=1024, tn=640):
    T, D = xb.shape
    N = wT.shape[0]
    tn = min(N, tn)
    tm = min(tm, S)
    nsb = S // tm
    return pl.pallas_call(
        _proj_t_kernel,
        grid=(T // tm, N // tn),
        in_specs=[pl.BlockSpec((tn, D), lambda i, j: (j, 0)),
                  pl.BlockSpec((tm, D), lambda i, j: (i, 0))],
        out_specs=pl.BlockSpec((pl.Squeezed(), tn, tm), lambda i, j: (i // nsb, j, i % nsb)),
        out_shape=jax.ShapeDtypeStruct((B, N, S), out_dtype),
        compiler_params=_cparams(2),
        name="proj_t",
    )(wT, xb)


def _proj_rope_kernel(x_ref, w_ref, wr_ref, cos_ref, sin_ref, o_ref):
    x = x_ref[...]
    a = jnp.dot(x, w_ref[...], preferred_element_type=F32)
    r = jnp.dot(x, wr_ref[...], preferred_element_type=F32)
    reps = a.shape[1] // LANES
    cos = jnp.concatenate([cos_ref[...]] * reps, axis=1)
    sin = jnp.concatenate([sin_ref[...]] * reps, axis=1)
    o_ref[...] = (a * cos + r * sin).astype(o_ref.dtype)


def _proj_rope(xb, w, w_rot, cos, sin, tm=1024, tn=512):
    T, D = xb.shape
    N = w.shape[1]
    tm = min(tm, T)
    return pl.pallas_call(
        _proj_rope_kernel,
        grid=(T // tm, N // tn),
        in_specs=[pl.BlockSpec((tm, D), lambda i, j: (i, 0)),
                  pl.BlockSpec((D, tn), lambda i, j: (0, j)),
                  pl.BlockSpec((D, tn), lambda i, j: (0, j)),
                  pl.BlockSpec((tm, LANES), lambda i, j: (i, 0)),
                  pl.BlockSpec((tm, LANES), lambda i, j: (i, 0))],
        out_specs=pl.BlockSpec((tm, tn), lambda i, j: (i, j)),
        out_shape=jax.ShapeDtypeStruct((T, N), BF16),
        compiler_params=_cparams(2),
        name="proj_rope",
    )(xb, w, w_rot, cos, sin)


def _proj_idx_kernel(x_ref, w_ref, g_ref, b_ref, o_ref, *, iw_scale):
    acc = jnp.dot(x_ref[...], w_ref[...], preferred_element_type=F32)
    lane = lax.broadcasted_iota(jnp.int32, acc.shape, 1)
    is_k = lane < IDX_HEAD_DIM
    mu = jnp.sum(jnp.where(is_k, acc, 0.0), axis=-1, keepdims=True) / IDX_HEAD_DIM
    d = acc - mu
    var = jnp.sum(jnp.where(is_k, d * d, 0.0), axis=-1, keepdims=True) / IDX_HEAD_DIM
    ln = d * lax.rsqrt(var + LN_EPS) * g_ref[...] + b_ref[...]
    o_ref[...] = jnp.where(is_k, ln, acc * iw_scale)


def _proj_idx(xb, w_pad, g_pad, b_pad, iw_scale, tm=1024):
    T, D = xb.shape
    tm = min(tm, T)
    return pl.pallas_call(
        functools.partial(_proj_idx_kernel, iw_scale=iw_scale),
        grid=(T // tm,),
        in_specs=[pl.BlockSpec((tm, D), lambda i: (i, 0)),
                  pl.BlockSpec((D, LANES), lambda i: (0, 0)),
                  pl.BlockSpec((1, LANES), lambda i: (0, 0)),
                  pl.BlockSpec((1, LANES), lambda i: (0, 0))],
        out_specs=pl.BlockSpec((tm, LANES), lambda i: (i, 0)),
        out_shape=jax.ShapeDtypeStruct((T, LANES), F32),
        compiler_params=_cparams(1),
        name="proj_idx",
    )(xb, w_pad, g_pad, b_pad)


def _t5_bucket_table():
    n = np.arange(BIAS_TABLE_N)
    max_exact = NUM_BUCKETS // 2
    nf = np.maximum(n, 1).astype(np.float64)
    large = max_exact + (np.log(nf / max_exact) / math.log(MAX_DISTANCE / max_exact)
                         * (NUM_BUCKETS - max_exact)).astype(np.int64)
    large = np.minimum(large, NUM_BUCKETS - 1)
    bucket = np.where(n < max_exact, n, large)
    assert np.all(bucket[FAR_N:] == NUM_BUCKETS - 1) and bucket[FAR_N - 1] != NUM_BUCKETS - 1
    return bucket.astype(np.int32)


def _fold_rows(a, op):
    parts = [a[r:r + SUBLANES] for r in range(0, a.shape[0], SUBLANES)]
    while len(parts) > 1:
        nxt = [op(parts[k], parts[k + 1]) for k in range(0, len(parts) - 1, 2)]
        if len(parts) % 2:
            nxt.append(parts[-1])
        parts = nxt
    return parts[0]


def _attn_kernel(far_ref, iqT_ref, iwT_ref, posq_ref, posqc_ref, posk_ref, ik_ref, qT_ref, k_ref, vT_ref,
                 tbl_ref, tril_ref, o_ref, sc_ref, s_ref, b_ref, qm_ref, m_ref, l_ref, *acc_refs, k_sel, seq):
    i = pl.program_id(1)
    nch = (i * TQ + TQ + CK - 1) // CK
    q_idx = i * TQ + lax.broadcasted_iota(jnp.int32, (1, TQ), 1)
    kf = float(k_sel)

    def chunk_off(c):
        return pl.multiple_of(c * CK, CK)

    def key_idx(off):
        return off + lax.broadcasted_iota(jnp.int32, (CK, TQ), 0)

    def col_reduce(part, op):
        return op(part, axis=0, keepdims=True)

    iqT = iqT_ref[...]
    iwT = iwT_ref[...]

    def score_body(c, carry):
        mn, mx = carry
        off = chunk_off(c)
        ikc = ik_ref[pl.ds(off, CK), :]
        s = None
        for h in range(IDX_HEADS):
            z = jnp.dot(ikc, iqT[h * IDX_HEAD_DIM:(h + 1) * IDX_HEAD_DIM, :], preferred_element_type=F32)
            t = jnp.maximum(z, 0.0) * iwT[h:h + 1, :]
            s = t if s is None else s + t
        causal = key_idx(off) <= q_idx
        s_lo = jnp.where(causal, s, -jnp.inf)
        sc_ref[pl.ds(off, CK), :] = s_lo
        mn = jnp.minimum(mn, _fold_rows(jnp.where(causal, s, jnp.inf), jnp.minimum))
        mx = jnp.maximum(mx, _fold_rows(s_lo, jnp.maximum))
        return mn, mx

    mn8, mx8 = lax.fori_loop(0, nch, score_body,
                             (jnp.full((SUBLANES, TQ), jnp.inf, F32), jnp.full((SUBLANES, TQ), -jnp.inf, F32)))
    mn = col_reduce(mn8, jnp.min)
    mx = col_reduce(mx8, jnp.max)

    def count(pred_fn):
        def body(c, acc):
            off = chunk_off(c)
            blk = sc_ref[pl.ds(off, CK), :]
            return acc + _fold_rows(jnp.where(pred_fn(blk, off), 1.0, 0.0), jnp.add)
        acc = lax.fori_loop(0, nch, body, jnp.zeros((SUBLANES, TQ), F32))
        return col_reduce(acc, jnp.sum)

    def bisect_round(_, st):
        lo, hi, c_lo = st
        mid = 0.5 * (lo + hi)
        c = count(lambda blk, off: blk >= mid)
        ok = c >= kf
        return jnp.where(ok, mid, lo), jnp.where(ok, hi, mid), jnp.where(ok, c, c_lo)

    c_all = (q_idx + 1).astype(F32)
    lo, hi, c_lo = lax.fori_loop(0, BISECT_ROUNDS, bisect_round, (mn, mx, c_all))

    def min_ge_body(c, acc):
        blk = sc_ref[pl.ds(chunk_off(c), CK), :]
        return jnp.minimum(acc, _fold_rows(jnp.where(blk >= lo, blk, jnp.inf), jnp.minimum))

    cur0 = col_reduce(lax.fori_loop(0, nch, min_ge_body, jnp.full((SUBLANES, TQ), jnp.inf, F32)), jnp.min)

    def walk_cond(st):
        return st[3] > 0.0

    def walk_body(st):
        cur, c_ge, _, _ = st

        def body(c, carry):
            cnt, nxt = carry
            blk = sc_ref[pl.ds(chunk_off(c), CK), :]
            gt = blk > cur
            cnt = cnt + _fold_rows(jnp.where(gt, 1.0, 0.0), jnp.add)
            nxt = jnp.minimum(nxt, _fold_rows(jnp.where(gt, blk, jnp.inf), jnp.minimum))
            return cnt, nxt

        cnt, nxt = lax.fori_loop(0, nch, body,
                                 (jnp.zeros((SUBLANES, TQ), F32), jnp.full((SUBLANES, TQ), jnp.inf, F32)))
        c_gt = col_reduce(cnt, jnp.sum)
        nxt = col_reduce(nxt, jnp.min)
        adv = c_gt >= kf
        cur = jnp.where(adv, nxt, cur)
        c_ge = jnp.where(adv, c_gt, c_ge)
        return cur, c_ge, c_gt, jnp.max(jnp.where(adv, 1.0, 0.0))

    tau, c_ge, c_gt, _ = lax.while_loop(
        walk_cond, walk_body, (cur0, c_lo, jnp.zeros((1, TQ), F32), jnp.float32(1.0)))

    trim = c_ge > kf

    def mask_with_ties():
        room = kf - c_gt

        def mask_body(c, seen):
            off = chunk_off(c)
            for j in range(CK // LANES):
                rows = pl.ds(off + j * LANES, LANES)
                blk = sc_ref[rows, :]
                tie = blk == tau
                rank = seen + jnp.dot(tril_ref[...], jnp.where(tie, 1.0, 0.0).astype(BF16),
                                      preferred_element_type=F32)
                keep = (blk > tau) | (tie & (rank <= room))
                sc_ref[rows, :] = jnp.where(keep, 0.0, NEG)
                seen = rank[LANES - 1:LANES, :]
            return seen

        lax.fori_loop(0, nch, mask_body, jnp.zeros((1, TQ), F32))

    def mask_plain():
        def mask_body(c, _):
            off = chunk_off(c)
            sc_ref[pl.ds(off, CK), :] = jnp.where(sc_ref[pl.ds(off, CK), :] >= tau, 0.0, NEG)
            return 0

        lax.fori_loop(0, nch, mask_body, 0)

    lax.cond(jnp.max(jnp.where(trim, 1.0, 0.0)) > 0.0, mask_with_ties, mask_plain)

    m_ref[...] = jnp.full(m_ref.shape, NEG, F32)
    l_ref[...] = jnp.zeros(l_ref.shape, F32)
    for acc in acc_refs:
        acc[...] = jnp.zeros(acc.shape, F32)

    rowi = lax.broadcasted_iota(jnp.int32, (LANES, TQ), 0)
    for h in range(ATTN_HEADS):
        pair, sub = divmod(h, 2)
        qp = qT_ref[pair * LANES:(pair + 1) * LANES, :]
        in_head = (rowi >= sub * ATTN_HEAD_DIM) & (rowi < (sub + 1) * ATTN_HEAD_DIM)
        qm_ref[h] = jnp.where(in_head, qp, jnp.zeros_like(qp))

    pq_row = posq_ref[...]
    pq_col = posqc_ref[...]
    pq_min = jnp.min(pq_row)
    n_sub = CK // LANES

    def attn_chunk(c, _):
        off = chunk_off(c)
        mb = sc_ref[pl.ds(off, CK), :]
        pk_row = posk_ref[:, pl.ds(off, CK)]

        def run(const_bias):
            shift = [far_ref[h] if const_bias else 0.0 for h in range(ATTN_HEADS)]
            if not const_bias:
                for j in range(n_sub):
                    rows = slice(j * LANES, (j + 1) * LANES)
                    pk_sub = pk_row[:, rows]
                    all_far = (pq_min - jnp.max(pk_sub)) >= FAR_N
                    all_masked = (off + j * LANES) > (i * TQ + TQ - 1)

                    def fill_const(rows=rows):
                        for h in range(ATTN_HEADS):
                            b_ref[h, rows, :] = jnp.full((LANES, TQ), far_ref[h], F32)

                    def fill_lookup(rows=rows, pk_sub=pk_sub):
                        n_qk = jnp.clip(pq_col - pk_sub, 0, BIAS_TABLE_N - 1).astype(F32)
                        n_kq = n_qk.T.astype(jnp.int32)
                        for h in range(ATTN_HEADS):
                            tb = jnp.broadcast_to(tbl_ref[h:h + 1, :], (LANES, BIAS_TABLE_N))
                            b_ref[h, rows, :] = jnp.take_along_axis(tb, n_kq, axis=1)

                    lax.cond(all_far | all_masked, fill_const, fill_lookup)
            m_prev = m_ref[...]
            m_cur = []
            for h in range(ATTN_HEADS):
                pair = h // 2
                kc = k_ref[pl.ds(off, CK), pair * LANES:(pair + 1) * LANES]
                s = jnp.dot(kc, qm_ref[h], preferred_element_type=F32) + mb
                if not const_bias:
                    s = s + b_ref[h]
                s_ref[h] = s
                m_cur.append(col_reduce(_fold_rows(s, jnp.maximum), jnp.max) + shift[h])
            m_new = jnp.maximum(m_prev, jnp.concatenate(m_cur, axis=0))
            alpha = jnp.exp(m_prev - m_new)
            l_cur = []
            for h in range(ATTN_HEADS):
                p = jnp.exp(s_ref[h] - (m_new[h:h + 1, :] - shift[h]))
                l_cur.append(col_reduce(_fold_rows(p, jnp.add), jnp.sum))
                vTh = vT_ref[h * ATTN_HEAD_DIM:(h + 1) * ATTN_HEAD_DIM, pl.ds(off, CK)]
                acc = acc_refs[h]
                acc[...] = alpha[h:h + 1, :] * acc[...] + jnp.dot(vTh, p.astype(BF16),
                                                                  preferred_element_type=F32)
            l_ref[...] = alpha * l_ref[...] + jnp.concatenate(l_cur, axis=0)
            m_ref[...] = m_new

        is_far = (pq_min - jnp.max(pk_row)) >= FAR_N
        lax.cond(is_far, lambda: run(True), lambda: run(False))
        return 0

    lax.fori_loop(0, nch, attn_chunk, 0)

    outT = jnp.concatenate([acc_refs[h][...] / l_ref[h:h + 1, :] for h in range(ATTN_HEADS)], axis=0)
    o_ref[...] = outT.T.astype(o_ref.dtype)


def _sparse_attention(fm, tokb, ik, iwT, positions, rel_bias):
    B, S, _ = tokb.shape
    k_sel = min(TOPK_MAX, S // 4)
    bucket = _t5_bucket_table()
    tbl = rel_bias[bucket].T.astype(F32)
    far = rel_bias[NUM_BUCKETS - 1].astype(F32)
    pos_row = positions.reshape(B, 1, S)
    pos_col = positions.reshape(B, S, 1)
    tril = jnp.tril(jnp.ones((LANES, LANES), BF16))
    sq = pl.Squeezed()
    iq_blk = (2 * ATTN_W) // IDX_Q_W
    k_blk = (tokb.shape[2] - ATTN_W) // ATTN_W
    return pl.pallas_call(
        functools.partial(_attn_kernel, k_sel=k_sel, seq=S),
        grid=(B, S // TQ),
        in_specs=[pl.BlockSpec(memory_space=pltpu.SMEM),
                  pl.BlockSpec((sq, IDX_Q_W, TQ), lambda b, i: (b, iq_blk, i)),
                  pl.BlockSpec((sq, SUBLANES, TQ), lambda b, i: (b, 0, i)),
                  pl.BlockSpec((sq, 1, TQ), lambda b, i: (b, 0, i)),
                  pl.BlockSpec((sq, TQ, 1), lambda b, i: (b, i, 0)),
                  pl.BlockSpec((sq, 1, S), lambda b, i: (b, 0, 0)),
                  pl.BlockSpec((sq, S, IDX_HEAD_DIM), lambda b, i: (b, 0, 0)),
                  pl.BlockSpec((sq, ATTN_W, TQ), lambda b, i: (b, 0, i)),
                  pl.BlockSpec((sq, S, ATTN_W), lambda b, i: (b, 0, k_blk)),
                  pl.BlockSpec((sq, ATTN_W, S), lambda b, i: (b, 1, 0)),
                  pl.BlockSpec((ATTN_HEADS, BIAS_TABLE_N), lambda b, i: (0, 0)),
                  pl.BlockSpec((LANES, LANES), lambda b, i: (0, 0))],
        out_specs=pl.BlockSpec((sq, TQ, ATTN_W), lambda b, i: (b, i, 0)),
        out_shape=jax.ShapeDtypeStruct((B, S, ATTN_W), BF16),
        scratch_shapes=[pltpu.VMEM((S, TQ), F32),
                        pltpu.VMEM((ATTN_HEADS, CK, TQ), F32),
                        pltpu.VMEM((ATTN_HEADS, CK, TQ), F32),
                        pltpu.VMEM((ATTN_HEADS, LANES, TQ), BF16),
                        pltpu.VMEM((ATTN_HEADS, TQ), F32),
                        pltpu.VMEM((ATTN_HEADS, TQ), F32)]
                       + [pltpu.VMEM((ATTN_HEAD_DIM, TQ), F32)] * ATTN_HEADS,
        compiler_params=_cparams(2),
        name="sparse_attention",
    )(far, fm, iwT, pos_row, pos_col, pos_row, ik, fm, tokb, fm, tbl, tril)


def _retention_kernel(q_ref, k_ref, v_ref, g_ref, decay_ref, xi_ref, zeta_ref, gch_ref, o_ref, r_ref):
    @pl.when(pl.program_id(1) == 0)
    def _():
        r_ref[...] = jnp.zeros(r_ref.shape, F32)

    C = RET_CHUNK
    lane = lax.broadcasted_iota(jnp.int32, (C, LANES), 1)
    row = lax.broadcasted_iota(jnp.int32, (LANES, RET_V_DIM), 0)
    for pair in range(RET_HEADS // 2):
        q_pair = q_ref[:, pair * LANES:(pair + 1) * LANES]
        k_pair = k_ref[:, pair * LANES:(pair + 1) * LANES]
        v_pair = v_ref[:, 2 * pair * RET_V_DIM:(2 * pair + 2) * RET_V_DIM]
        r_pair = r_ref[pair]
        r_bf = r_pair.astype(BF16)
        for sub in range(2):
            h = 2 * pair + sub
            in_head = (lane >= sub * RET_QK_DIM) & (lane < (sub + 1) * RET_QK_DIM)
            qm = jnp.where(in_head, q_pair, jnp.zeros_like(q_pair))
            v_h = v_pair[:, sub * RET_V_DIM:(sub + 1) * RET_V_DIM]
            inner = lax.dot_general(qm, k_pair, (((1,), (1,)), ((), ())),
                                    preferred_element_type=F32) * decay_ref[h]
            o = (jnp.dot(inner.astype(BF16), v_h, preferred_element_type=F32)
                 + jnp.dot(qm, r_bf, preferred_element_type=F32) * xi_ref[h])
            mu = jnp.mean(o, axis=-1, keepdims=True)
            d = o - mu
            var = jnp.mean(d * d, axis=-1, keepdims=True)
            hn = d * lax.rsqrt(var + LN_EPS)
            gate = g_ref[:, h * RET_V_DIM:(h + 1) * RET_V_DIM].astype(F32)
            o_ref[:, h * RET_V_DIM:(h + 1) * RET_V_DIM] = (gate * hn).astype(o_ref.dtype)
        kz = (k_pair.astype(F32) * zeta_ref[pair]).astype(BF16)
        upd = lax.dot_general(kz, v_pair, (((0,), (0,)), ((), ())), preferred_element_type=F32)
        r_ref[pair] = r_pair * gch_ref[pair] + jnp.where(row < RET_QK_DIM, upd[:, :RET_V_DIM], upd[:, RET_V_DIM:])


def _retention(qk, tokb, gates, B, S):
    C = RET_CHUNK
    H = RET_HEADS
    nc = S // C
    gamma = 1.0 - 2.0 ** (-5.0 - jnp.arange(H, dtype=F32))
    log_g = jnp.log(gamma)
    n = jnp.arange(C, dtype=F32)
    diff = n[:, None] - n[None, :]
    decay_in = jnp.where(diff[None] >= 0, jnp.exp(log_g[:, None, None] * jnp.maximum(diff, 0.0)[None]), 0.0)
    xi = jnp.exp(log_g[None, :] * (n[:, None] + 1.0))
    zeta = jnp.exp(log_g[None, :] * (C - 1.0 - n[:, None]))
    g_chunk = jnp.exp(log_g * C)
    xi_b = jnp.broadcast_to(xi.T[:, :, None], (H, C, RET_V_DIM))
    zeta_b = jnp.repeat(zeta, RET_QK_DIM, axis=1).reshape(C, H // 2, LANES).transpose(1, 0, 2)
    gch_b = jnp.broadcast_to(jnp.repeat(g_chunk, RET_QK_DIM).reshape(H // 2, LANES, 1),
                             (H // 2, LANES, RET_V_DIM))
    return pl.pallas_call(
        _retention_kernel,
        grid=(B, nc),
        in_specs=[pl.BlockSpec((C, RET_QK_W), lambda b, i: (b * nc + i, 0)),
                  pl.BlockSpec((C, RET_QK_W), lambda b, i: (b * nc + i, 1)),
                  pl.BlockSpec((C, RET_V_W), lambda b, i: (b * nc + i, 0)),
                  pl.BlockSpec((C, RET_V_W), lambda b, i: (b * nc + i, 0)),
                  pl.BlockSpec((H, C, C), lambda b, i: (0, 0, 0)),
                  pl.BlockSpec((H, C, RET_V_DIM), lambda b, i: (0, 0, 0)),
                  pl.BlockSpec((H // 2, C, LANES), lambda b, i: (0, 0, 0)),
                  pl.BlockSpec((H // 2, LANES, RET_V_DIM), lambda b, i: (0, 0, 0))],
        out_specs=pl.BlockSpec((C, RET_V_W), lambda b, i: (b * nc + i, 0)),
        out_shape=jax.ShapeDtypeStruct((B * S, RET_V_W), BF16),
        scratch_shapes=[pltpu.VMEM((H // 2, LANES, RET_V_DIM), F32)],
        compiler_params=_cparams(2),
        name="retention",
    )(qk, qk, tokb, gates, decay_in, xi_b, zeta_b, gch_b)


def _layer_norm(z, g, b):
    mu = jnp.mean(z, axis=-1, keepdims=True)
    d = z - mu
    var = jnp.mean(d * d, axis=-1, keepdims=True)
    return d * lax.rsqrt(var + LN_EPS) * g + b


def _merge_kernel(x_ref, ya_ref, yr_ref, ga_ref, gr_ref, wa_ref, wr_ref, wo_ref, g_ref, b_ref,
                  x1_ref, x1b_ref):
    a = jnp.dot(ya_ref[...], wa_ref[...], preferred_element_type=F32)
    r = jnp.dot(yr_ref[...], wr_ref[...], preferred_element_type=F32)
    h = ga_ref[...].astype(F32) * a + gr_ref[...].astype(F32) * r
    mix = jnp.dot(h.astype(BF16), wo_ref[...], preferred_element_type=F32)
    x1 = _layer_norm(DEEPNORM_ALPHA * x_ref[...] + mix, g_ref[...], b_ref[...])
    x1_ref[...] = x1
    x1b_ref[...] = x1.astype(BF16)


def _merge(x, ya, yr, gates, wa, wr, wo, g, b, tm=512):
    T, D = x.shape
    tm = min(tm, T)
    row = lambda i: (i, 0)
    fixed = lambda i: (0, 0)
    return pl.pallas_call(
        _merge_kernel,
        grid=(T // tm,),
        in_specs=[pl.BlockSpec((tm, D), row),
                  pl.BlockSpec((tm, ya.shape[1]), row),
                  pl.BlockSpec((tm, yr.shape[1]), row),
                  pl.BlockSpec((tm, D), lambda i: (i, 1)),
                  pl.BlockSpec((tm, D), lambda i: (i, 2)),
                  pl.BlockSpec(wa.shape, fixed),
                  pl.BlockSpec(wr.shape, fixed),
                  pl.BlockSpec(wo.shape, fixed),
                  pl.BlockSpec((1, D), fixed),
                  pl.BlockSpec((1, D), fixed)],
        out_specs=[pl.BlockSpec((tm, D), row), pl.BlockSpec((tm, D), row)],
        out_shape=[jax.ShapeDtypeStruct((T, D), F32), jax.ShapeDtypeStruct((T, D), BF16)],
        compiler_params=_cparams(1),
        name="merge",
    )(x, ya, yr, gates, gates, wa, wr, wo, g, b)


def _ffn_kernel(x1b_ref, x1_ref, wu_ref, wd_ref, g_ref, b_ref, o_ref, acc_ref):
    f = pl.program_id(1)

    @pl.when(f == 0)
    def _():
        acc_ref[...] = jnp.zeros(acc_ref.shape, F32)

    hid = jnp.maximum(jnp.dot(x1b_ref[...], wu_ref[...], preferred_element_type=F32), 0.0)
    acc_ref[...] += jnp.dot((hid * hid).astype(BF16), wd_ref[...], preferred_element_type=F32)

    @pl.when(f == pl.num_programs(1) - 1)
    def _():
        o_ref[...] = _layer_norm(DEEPNORM_ALPHA * x1_ref[...] + acc_ref[...], g_ref[...], b_ref[...])


def _ffn(x1b, x1, wu, wd, g, b, tm=1024, tf=1024):
    T, D = x1.shape
    F = wu.shape[1]
    tm = min(tm, T)
    return pl.pallas_call(
        _ffn_kernel,
        grid=(T // tm, F // tf),
        in_specs=[pl.BlockSpec((tm, D), lambda i, f: (i, 0)),
                  pl.BlockSpec((tm, D), lambda i, f: (i, 0)),
                  pl.BlockSpec((D, tf), lambda i, f: (0, f)),
                  pl.BlockSpec((tf, D), lambda i, f: (f, 0)),
                  pl.BlockSpec((1, D), lambda i, f: (0, 0)),
                  pl.BlockSpec((1, D), lambda i, f: (0, 0))],
        out_specs=pl.BlockSpec((tm, D), lambda i, f: (i, 0)),
        out_shape=jax.ShapeDtypeStruct((T, D), F32),
        scratch_shapes=[pltpu.VMEM((tm, D), F32)],
        compiler_params=_cparams(2),
        name="ffn",
    )(x1b, x1, wu, wd, g, b)


def _rot_half_weight(w):
    D, N = w.shape
    half = RET_QK_DIM // 2
    wh = w.reshape(D, N // RET_QK_DIM, 2, half)
    return jnp.stack([-wh[:, :, 1], wh[:, :, 0]], axis=2).reshape(D, N)


def kernel(x, positions, w_in, rel_bias, idx_k_ln_g, idx_k_ln_b, w_attn_branch, w_ret_branch,
           w_out, ln_mix_g, ln_mix_b, w_up, w_down, ln_ffn_g, ln_ffn_b):
    B, S, D = x.shape
    T = B * S
    sizes = (ATTN_W, ATTN_W, ATTN_W, IDX_Q_W, IDX_HEAD_DIM, IDX_HEADS,
             RET_QK_W, RET_QK_W, RET_V_W, RET_V_W, D, D)
    offs = [0] + [int(o) for o in np.cumsum(sizes)]
    cos, sin = _rope_tables(positions)
    xf = x.reshape(T, D)
    for l in range(DEPTH):
        wl = w_in[l].astype(BF16)
        cols = [wl[:, offs[k]:offs[k + 1]] for k in range(len(sizes))]
        (w_qa, w_ka, w_va, w_iq, w_ik, w_iw, w_qr, w_kr, w_vr, w_gr, w_ga, w_gtr) = cols
        xb = xf.astype(BF16)
        w_qa = w_qa * (ATTN_HEAD_DIM ** -0.5)
        w_kr = w_kr * (RET_QK_DIM ** -0.5)

        fm = _proj_t(xb, jnp.concatenate([w_qa, w_va, w_iq], axis=1).T.astype(BF16), B, S, BF16)
        tokb = _proj(xb, jnp.concatenate([w_vr, w_ka], axis=1).astype(BF16), BF16)
        gates = _proj_gates(xb, jnp.concatenate([w_gr, w_ga, w_gtr], axis=1).astype(BF16), tn=D)
        w_rope = jnp.concatenate([w_qr, w_kr], axis=1)
        w_rope_rot = jnp.concatenate([_rot_half_weight(w_qr), _rot_half_weight(w_kr)], axis=1)
        qk_r = _proj_rope(xb, w_rope.astype(BF16), w_rope_rot.astype(BF16), cos, sin)
        pad = LANES - IDX_HEAD_DIM - IDX_HEADS
        w_idx = jnp.concatenate([w_ik, w_iw, jnp.zeros((D, pad), BF16)], axis=1)
        g_pad = jnp.concatenate([idx_k_ln_g[l], jnp.zeros((LANES - IDX_HEAD_DIM,), F32)]).reshape(1, LANES)
        b_pad = jnp.concatenate([idx_k_ln_b[l], jnp.zeros((LANES - IDX_HEAD_DIM,), F32)]).reshape(1, LANES)
        idx = _proj_idx(xb, w_idx, g_pad, b_pad,
                        (IDX_HEAD_DIM ** -0.5) * (IDX_HEADS ** -0.5)).reshape(B, S, LANES)
        ik = idx[:, :, :IDX_HEAD_DIM].astype(BF16)
        iwT = jnp.swapaxes(idx[:, :, IDX_HEAD_DIM:IDX_HEAD_DIM + SUBLANES], 1, 2)

        y_a = _sparse_attention(fm, tokb.reshape(B, S, -1), ik, iwT, positions, rel_bias)
        y_r = _retention(qk_r, tokb, gates, B, S)
        x1, x1b = _merge(xf, y_a.reshape(T, ATTN_W), y_r, gates,
                         w_attn_branch[l].astype(BF16), w_ret_branch[l].astype(BF16),
                         w_out[l].astype(BF16), ln_mix_g[l].reshape(1, D), ln_mix_b[l].reshape(1, D))
        xf = _ffn(x1b, x1, w_up[l].astype(BF16), w_down[l].astype(BF16),
                  ln_ffn_g[l].reshape(1, D), ln_ffn_b[l].reshape(1, D))
    return xf.reshape(B, S, D)
```

```python
import functools
import math

import numpy as np
import jax
import jax.numpy as jnp
from jax import lax
from jax.experimental import pallas as pl
from jax.experimental.pallas import tpu as pltpu

F32 = jnp.float32
BF16 = jnp.bfloat16

ATTN_HEADS = 8
ATTN_HEAD_DIM = 64
ATTN_W = ATTN_HEADS * ATTN_HEAD_DIM
IDX_HEADS = 4
IDX_HEAD_DIM = 64
IDX_Q_W = IDX_HEADS * IDX_HEAD_DIM
TOPK_MAX = 256
RET_HEADS = 8
RET_QK_DIM = 64
RET_V_DIM = 128
RET_QK_W = RET_HEADS * RET_QK_DIM
RET_V_W = RET_HEADS * RET_V_DIM
RET_CHUNK = 128
ROPE_BASE = 10000.0
NUM_BUCKETS = 32
MAX_DISTANCE = 128
LN_EPS = 1e-5
DEPTH = 1
DEEPNORM_ALPHA = (2.0 * DEPTH) ** 0.25

LANES = 128
SUBLANES = 8
VMEM_LIMIT = 56 * 1024 * 1024

TQ = 128
CK = 512
NEG = -1e30
LOG2E = math.log2(math.e)
BISECT_ROUNDS = 18
BIAS_TABLE_N = 128
FAR_N = 113


def _cparams(n_grid):
    return pltpu.CompilerParams(
        dimension_semantics=("arbitrary",) * n_grid,
        vmem_limit_bytes=VMEM_LIMIT)


def _trig_kernel(pos_ref, inv_ref, cos_ref, sin_ref):
    ang = pos_ref[...] * inv_ref[...]
    cos_ref[...] = jnp.cos(ang)
    sin_ref[...] = jnp.sin(ang)


def _rope_tables(positions):
    B, S = positions.shape
    half = RET_QK_DIM // 2
    inv = ROPE_BASE ** (-jnp.arange(half, dtype=F32) / half)
    per_row = LANES // half
    rows = B * S // per_row
    pos_e = jnp.repeat(positions.astype(F32).reshape(rows, per_row), half, axis=1)
    inv_e = jnp.tile(inv, per_row).reshape(1, LANES)
    tr = min(rows, 1024)
    cos, sin = pl.pallas_call(
        _trig_kernel,
        grid=(rows // tr,),
        in_specs=[pl.BlockSpec((tr, LANES), lambda i: (i, 0)),
                  pl.BlockSpec((1, LANES), lambda i: (0, 0))],
        out_specs=[pl.BlockSpec((tr, LANES), lambda i: (i, 0))] * 2,
        out_shape=[jax.ShapeDtypeStruct((rows, LANES), F32)] * 2,
        compiler_params=_cparams(1),
        name="rope_tables",
    )(pos_e, inv_e)
    cos = jnp.tile(cos.reshape(B * S, half), (1, per_row))
    sin = jnp.tile(sin.reshape(B * S, half), (1, per_row))
    return cos, sin


def _proj_kernel(x_ref, w_ref, o_ref):
    o_ref[...] = jnp.dot(x_ref[...], w_ref[...], preferred_element_type=F32).astype(o_ref.dtype)


def _proj(xb, w, out_dtype, tm=1024, tn=512):
    T, D = xb.shape
    N = w.shape[1]
    tn = min(N, tn)
    tm = min(tm, T)
    return pl.pallas_call(
        _proj_kernel,
        grid=(T // tm, N // tn),
        in_specs=[pl.BlockSpec((tm, D), lambda i, j: (i, 0)),
                  pl.BlockSpec((D, tn), lambda i, j: (0, j))],
        out_specs=pl.BlockSpec((tm, tn), lambda i, j: (i, j)),
        out_shape=jax.ShapeDtypeStruct((T, N), out_dtype),
        compiler_params=_cparams(2),
        name="proj",
    )(xb, w)


def _proj_gates_kernel(x_ref, w_ref, o_ref):
    acc = jnp.dot(x_ref[...], w_ref[...], preferred_element_type=F32)
    sig = 1.0 / (1.0 + jnp.exp(-acc))
    o_ref[...] = jnp.where(pl.program_id(1) == 0, acc * sig, sig).astype(o_ref.dtype)


def _proj_gates(xb, w, tn, tm=1024):
    T, D = xb.shape
    N = w.shape[1]
    tm = min(tm, T)
    return pl.pallas_call(
        _proj_gates_kernel,
        grid=(T // tm, N // tn),
        in_specs=[pl.BlockSpec((tm, D), lambda i, j: (i, 0)),
                  pl.BlockSpec((D, tn), lambda i, j: (0, j))],
        out_specs=pl.BlockSpec((tm, tn), lambda i, j: (i, j)),
        out_shape=jax.ShapeDtypeStruct((T, N), BF16),
        compiler_params=_cparams(2),
        name="proj_gates",
    )(xb, w)


def _proj_t_kernel(wT_ref, x_ref, o_ref):
    acc = lax.dot_general(wT_ref[...], x_ref[...], (((1,), (1,)), ((), ())), preferred_element_type=F32)
    o_ref[...] = acc.astype(o_ref.dtype)


def _proj_t(xb, wT, B, S, out_dtype, tm=1024, tn=640):
    T, D = xb.shape
    N = wT.shape[0]
    tn = min(N, tn)
    tm = min(tm, S)
    nsb = S // tm
    return pl.pallas_call(
        _proj_t_kernel,
        grid=(T // tm, N // tn),
        in_specs=[pl.BlockSpec((tn, D), lambda i, j: (j, 0)),
                  pl.BlockSpec((tm, D), lambda i, j: (i, 0))],
        out_specs=pl.BlockSpec((pl.Squeezed(), tn, tm), lambda i, j: (i // nsb, j, i % nsb)),
        out_shape=jax.ShapeDtypeStruct((B, N, S), out_dtype),
        compiler_params=_cparams(2),
        name="proj_t",
    )(wT, xb)


def _proj_rope_kernel(x_ref, w_ref, wr_ref, cos_ref, sin_ref, o_ref):
    x = x_ref[...]
    a = jnp.dot(x, w_ref[...], preferred_element_type=F32)
    r = jnp.dot(x, wr_ref[...], preferred_element_type=F32)
    reps = a.shape[1] // LANES
    cos = jnp.concatenate([cos_ref[...]] * reps, axis=1)
    sin = jnp.concatenate([sin_ref[...]] * reps, axis=1)
    o_ref[...] = (a * cos + r * sin).astype(o_ref.dtype)


def _proj_rope(xb, w, w_rot, cos, sin, tm=1024, tn=512):
    T, D = xb.shape
    N = w.shape[1]
    tm = min(tm, T)
    return pl.pallas_call(
        _proj_rope_kernel,
        grid=(T // tm, N // tn),
        in_specs=[pl.BlockSpec((tm, D), lambda i, j: (i, 0)),
                  pl.BlockSpec((D, tn), lambda i, j: (0, j)),
                  pl.BlockSpec((D, tn), lambda i, j: (0, j)),
                  pl.BlockSpec((tm, LANES), lambda i, j: (i, 0)),
                  pl.BlockSpec((tm, LANES), lambda i, j: (i, 0))],
        out_specs=pl.BlockSpec((tm, tn), lambda i, j: (i, j)),
        out_shape=jax.ShapeDtypeStruct((T, N), BF16),
        compiler_params=_cparams(2),
        name="proj_rope",
    )(xb, w, w_rot, cos, sin)


def _proj_idx_kernel(x_ref, w_ref, g_ref, b_ref, o_ref, *, iw_scale):
    acc = jnp.dot(x_ref[...], w_ref[...], preferred_element_type=F32)
    lane = lax.broadcasted_iota(jnp.int32, acc.shape, 1)
    is_k = lane < IDX_HEAD_DIM
    mu = jnp.sum(jnp.where(is_k, acc, 0.0), axis=-1, keepdims=True) / IDX_HEAD_DIM
    d = acc - mu
    var = jnp.sum(jnp.where(is_k, d * d, 0.0), axis=-1, keepdims=True) / IDX_HEAD_DIM
    ln = d * lax.rsqrt(var + LN_EPS) * g_ref[...] + b_ref[...]
    o_ref[...] = jnp.where(is_k, ln, acc * iw_scale)


def _proj_idx(xb, w_pad, g_pad, b_pad, iw_scale, tm=1024):
    T, D = xb.shape
    tm = min(tm, T)
    return pl.pallas_call(
        functools.partial(_proj_idx_kernel, iw_scale=iw_scale),
        grid=(T // tm,),
        in_specs=[pl.BlockSpec((tm, D), lambda i: (i, 0)),
                  pl.BlockSpec((D, LANES), lambda i: (0, 0)),
                  pl.BlockSpec((1, LANES), lambda i: (0, 0)),
                  pl.BlockSpec((1, LANES), lambda i: (0, 0))],
        out_specs=pl.BlockSpec((tm, LANES), lambda i: (i, 0)),
        out_shape=jax.ShapeDtypeStruct((T, LANES), F32),
        compiler_params=_cparams(1),
        name="proj_idx",
    )(xb, w_pad, g_pad, b_pad)


def _t5_bucket_table():
    n = np.arange(BIAS_TABLE_N)
    max_exact = NUM_BUCKETS // 2
    nf = np.maximum(n, 1).astype(np.float64)
    large = max_exact + (np.log(nf / max_exact) / math.log(MAX_DISTANCE / max_exact)
                         * (NUM_BUCKETS - max_exact)).astype(np.int64)
    large = np.minimum(large, NUM_BUCKETS - 1)
    bucket = np.where(n < max_exact, n, large)
    assert np.all(bucket[FAR_N:] == NUM_BUCKETS - 1) and bucket[FAR_N - 1] != NUM_BUCKETS - 1
    return bucket.astype(np.int32)


def _fold_rows(a, op):
    parts = [a[r:r + SUBLANES] for r in range(0, a.shape[0], SUBLANES)]
    while len(parts) > 1:
        nxt = [op(parts[k], parts[k + 1]) for k in range(0, len(parts) - 1, 2)]
        if len(parts) % 2:
            nxt.append(parts[-1])
        parts = nxt
    return parts[0]


def _attn_kernel(far_ref, iqT_ref, iwT_ref, posq_ref, posqc_ref, posk_ref, ik_ref, qT_ref, k_ref, vT_ref,
                 tbl_ref, tril_ref, o_ref, sc_ref, s_ref, b_ref, qm_ref, m_ref, l_ref, *acc_refs, k_sel, seq):
    i = pl.program_id(1)
    nch = (i * TQ + TQ + CK - 1) // CK
    q_idx = i * TQ + lax.broadcasted_iota(jnp.int32, (1, TQ), 1)
    kf = float(k_sel)

    def chunk_off(c):
        return pl.multiple_of(c * CK, CK)

    def key_idx(off):
        return off + lax.broadcasted_iota(jnp.int32, (CK, TQ), 0)

    def col_reduce(part, op):
        return op(part, axis=0, keepdims=True)

    iqT = iqT_ref[...]
    iwT = iwT_ref[...]

    def score_body(c, carry):
        mn, mx = carry
        off = chunk_off(c)
        ikc = ik_ref[pl.ds(off, CK), :]
        s = None
        for h in range(IDX_HEADS):
            z = jnp.dot(ikc, iqT[h * IDX_HEAD_DIM:(h + 1) * IDX_HEAD_DIM, :], preferred_element_type=F32)
            t = jnp.maximum(z, 0.0) * iwT[h:h + 1, :]
            s = t if s is None else s + t
        causal = key_idx(off) <= q_idx
        s_lo = jnp.where(causal, s, -jnp.inf)
        sc_ref[pl.ds(off, CK), :] = s_lo
        mn = jnp.minimum(mn, _fold_rows(jnp.where(causal, s, jnp.inf), jnp.minimum))
        mx = jnp.maximum(mx, _fold_rows(s_lo, jnp.maximum))
        return mn, mx

    mn8, mx8 = lax.fori_loop(0, nch, score_body,
                             (jnp.full((SUBLANES, TQ), jnp.inf, F32), jnp.full((SUBLANES, TQ), -jnp.inf, F32)))
    mn = col_reduce(mn8, jnp.min)
    mx = col_reduce(mx8, jnp.max)

    def count(pred_fn):
        def body(c, acc):
            off = chunk_off(c)
            blk = sc_ref[pl.ds(off, CK), :]
            return acc + _fold_rows(jnp.where(pred_fn(blk, off), 1.0, 0.0), jnp.add)
        acc = lax.fori_loop(0, nch, body, jnp.zeros((SUBLANES, TQ), F32))
        return col_reduce(acc, jnp.sum)

    def bisect_round(_, st):
        lo, hi, c_lo = st
        mid = 0.5 * (lo + hi)
        c = count(lambda blk, off: blk >= mid)
        ok = c >= kf
        return jnp.where(ok, mid, lo), jnp.where(ok, hi, mid), jnp.where(ok, c, c_lo)

    c_all = (q_idx + 1).astype(F32)
    lo, hi, c_lo = lax.fori_loop(0, BISECT_ROUNDS, bisect_round, (mn, mx, c_all))

    def min_ge_body(c, acc):
        blk = sc_ref[pl.ds(chunk_off(c), CK), :]
        return jnp.minimum(acc, _fold_rows(jnp.where(blk >= lo, blk, jnp.inf), jnp.minimum))

    cur0 = col_reduce(lax.fori_loop(0, nch, min_ge_body, jnp.full((SUBLANES, TQ), jnp.inf, F32)), jnp.min)

    def walk_cond(st):
        return st[3] > 0.0

    def walk_body(st):
        cur, c_ge, _, _ = st

        def body(c, carry):
            cnt, nxt = carry
            blk = sc_ref[pl.ds(chunk_off(c), CK), :]
            gt = blk > cur
            cnt = cnt + _fold_rows(jnp.where(gt, 1.0, 0.0), jnp.add)
            nxt = jnp.minimum(nxt, _fold_rows(jnp.where(gt, blk, jnp.inf), jnp.minimum))
            return cnt, nxt

        cnt, nxt = lax.fori_loop(0, nch, body,
                                 (jnp.zeros((SUBLANES, TQ), F32), jnp.full((SUBLANES, TQ), jnp.inf, F32)))
        c_gt = col_reduce(cnt, jnp.sum)
        nxt = col_reduce(nxt, jnp.min)
        adv = c_gt >= kf
        cur = jnp.where(adv, nxt, cur)
        c_ge = jnp.where(adv, c_gt, c_ge)
        return cur, c_ge, c_gt, jnp.max(jnp.where(adv, 1.0, 0.0))

    tau, c_ge, c_gt, _ = lax.while_loop(
        walk_cond, walk_body, (cur0, c_lo, jnp.zeros((1, TQ), F32), jnp.float32(1.0)))

    trim = c_ge > kf

    def mask_with_ties():
        room = kf - c_gt

        def mask_body(c, seen):
            off = chunk_off(c)
            for j in range(CK // LANES):
                rows = pl.ds(off + j * LANES, LANES)
                blk = sc_ref[rows, :]
                tie = blk == tau
                rank = seen + jnp.dot(tril_ref[...], jnp.where(tie, 1.0, 0.0).astype(BF16),
                                      preferred_element_type=F32)
                keep = (blk > tau) | (tie & (rank <= room))
                sc_ref[rows, :] = jnp.where(keep, 0.0, NEG)
                seen = rank[LANES - 1:LANES, :]
            return seen

        lax.fori_loop(0, nch, mask_body, jnp.zeros((1, TQ), F32))

    def mask_plain():
        def mask_body(c, _):
            off = chunk_off(c)
            sc_ref[pl.ds(off, CK), :] = jnp.where(sc_ref[pl.ds(off, CK), :] >= tau, 0.0, NEG)
            return 0

        lax.fori_loop(0, nch, mask_body, 0)

    lax.cond(jnp.max(jnp.where(trim, 1.0, 0.0)) > 0.0, mask_with_ties, mask_plain)

    m_ref[...] = jnp.full(m_ref.shape, NEG, F32)
    l_ref[...] = jnp.zeros(l_ref.shape, F32)
    for acc in acc_refs:
        acc[...] = jnp.zeros(acc.shape, F32)

    rowi = lax.broadcasted_iota(jnp.int32, (LANES, TQ), 0)
    for pair in range(ATTN_HEADS // 2):
        qp = qT_ref[pair * LANES:(pair + 1) * LANES, :]
        zero = jnp.zeros_like(qp)
        qm_ref[pair] = jnp.concatenate([jnp.where(rowi < ATTN_HEAD_DIM, qp, zero),
                                        jnp.where(rowi >= ATTN_HEAD_DIM, qp, zero)], axis=1)
    ones_rows = jnp.ones((2 * SUBLANES, CK), BF16)

    pq_row = posq_ref[...]
    pq_col = posqc_ref[...]
    pq_min = jnp.min(pq_row)
    n_sub = CK // LANES

    def attn_chunk(c, _):
        off = chunk_off(c)
        mb = sc_ref[pl.ds(off, CK), :]
        pk_row = posk_ref[:, pl.ds(off, CK)]

        def run(const_bias):
            shift = [far_ref[h] if const_bias else 0.0 for h in range(ATTN_HEADS)]
            if not const_bias:
                for j in range(n_sub):
                    rows = slice(j * LANES, (j + 1) * LANES)
                    pk_sub = pk_row[:, rows]
                    all_far = (pq_min - jnp.max(pk_sub)) >= FAR_N
                    all_masked = (off + j * LANES) > (i * TQ + TQ - 1)

                    def fill_const(rows=rows):
                        for h in range(ATTN_HEADS):
                            b_ref[h, rows, :] = jnp.full((LANES, TQ), far_ref[h], F32)

                    def fill_lookup(rows=rows, pk_sub=pk_sub):
                        n_qk = jnp.clip(pq_col - pk_sub, 0, BIAS_TABLE_N - 1).astype(F32)
                        n_kq = n_qk.T.astype(jnp.int32)
                        for h in range(ATTN_HEADS):
                            tb = jnp.broadcast_to(tbl_ref[h:h + 1, :], (LANES, BIAS_TABLE_N))
                            b_ref[h, rows, :] = jnp.take_along_axis(tb, n_kq, axis=1)

                    lax.cond(all_far | all_masked, fill_const, fill_lookup)
            m_prev = m_ref[...]
            l_prev = l_ref[...]
            m_new, alpha, l_new = {}, {}, {}
            n_pairs = ATTN_HEADS // 2

            def logits(pair):
                kc = k_ref[pl.ds(off, CK), pair * LANES:(pair + 1) * LANES]
                s2 = jnp.dot(kc, qm_ref[pair], preferred_element_type=F32)
                for sub, h in enumerate((2 * pair, 2 * pair + 1)):
                    s = s2[:, sub * TQ:(sub + 1) * TQ] + mb
                    if not const_bias:
                        s = s + b_ref[h]
                    s_ref[h] = s
                    m_cur = col_reduce(_fold_rows(s, jnp.maximum), jnp.max) + shift[h]
                    m_new[h] = jnp.maximum(m_prev[h:h + 1, :], m_cur)
                    alpha[h] = jnp.exp2(m_prev[h:h + 1, :] - m_new[h])

            def weights_times_v(pair):
                heads = (2 * pair, 2 * pair + 1)
                p2 = jnp.concatenate([jnp.exp2(s_ref[h] - (m_new[h] - shift[h])).astype(BF16) for h in heads],
                                     axis=1)
                lhs = jnp.concatenate([vT_ref[pair * LANES:(pair + 1) * LANES, pl.ds(off, CK)], ones_rows],
                                      axis=0)
                out = jnp.dot(lhs, p2, preferred_element_type=F32)
                for sub, h in enumerate(heads):
                    cols = slice(sub * TQ, (sub + 1) * TQ)
                    acc = acc_refs[h]
                    acc[...] = alpha[h] * acc[...] + out[sub * ATTN_HEAD_DIM:(sub + 1) * ATTN_HEAD_DIM, cols]
                    l_new[h] = alpha[h] * l_prev[h:h + 1, :] + out[LANES:LANES + 1, cols]

            for pair in range(n_pairs):
                logits(pair)
            for pair in range(n_pairs):
                weights_times_v(pair)
            l_ref[...] = jnp.concatenate([l_new[h] for h in range(ATTN_HEADS)], axis=0)
            m_ref[...] = jnp.concatenate([m_new[h] for h in range(ATTN_HEADS)], axis=0)

        is_far = (pq_min - jnp.max(pk_row)) >= FAR_N
        lax.cond(is_far, lambda: run(True), lambda: run(False))
        return 0

    lax.fori_loop(0, nch, attn_chunk, 0)

    outT = jnp.concatenate([acc_refs[h][...] / l_ref[h:h + 1, :] for h in range(ATTN_HEADS)], axis=0)
    o_ref[...] = outT.T.astype(o_ref.dtype)


def _sparse_attention(fm, tokb, ik, iwT, positions, rel_bias):
    B, S, _ = tokb.shape
    k_sel = min(TOPK_MAX, S // 4)
    bucket = _t5_bucket_table()
    tbl = rel_bias[bucket].T.astype(F32) * LOG2E
    far = rel_bias[NUM_BUCKETS - 1].astype(F32) * LOG2E
    pos_row = positions.reshape(B, 1, S)
    pos_col = positions.reshape(B, S, 1)
    tril = jnp.tril(jnp.ones((LANES, LANES), BF16))
    sq = pl.Squeezed()
    iq_blk = (2 * ATTN_W) // IDX_Q_W
    k_blk = (tokb.shape[2] - ATTN_W) // ATTN_W
    return pl.pallas_call(
        functools.partial(_attn_kernel, k_sel=k_sel, seq=S),
        grid=(B, S // TQ),
        in_specs=[pl.BlockSpec(memory_space=pltpu.SMEM),
                  pl.BlockSpec((sq, IDX_Q_W, TQ), lambda b, i: (b, iq_blk, i)),
                  pl.BlockSpec((sq, SUBLANES, TQ), lambda b, i: (b, 0, i)),
                  pl.BlockSpec((sq, 1, TQ), lambda b, i: (b, 0, i)),
                  pl.BlockSpec((sq, TQ, 1), lambda b, i: (b, i, 0)),
                  pl.BlockSpec((sq, 1, S), lambda b, i: (b, 0, 0)),
                  pl.BlockSpec((sq, S, IDX_HEAD_DIM), lambda b, i: (b, 0, 0)),
                  pl.BlockSpec((sq, ATTN_W, TQ), lambda b, i: (b, 0, i)),
                  pl.BlockSpec((sq, S, ATTN_W), lambda b, i: (b, 0, k_blk)),
                  pl.BlockSpec((sq, ATTN_W, S), lambda b, i: (b, 1, 0)),
                  pl.BlockSpec((ATTN_HEADS, BIAS_TABLE_N), lambda b, i: (0, 0)),
                  pl.BlockSpec((LANES, LANES), lambda b, i: (0, 0))],
        out_specs=pl.BlockSpec((sq, TQ, ATTN_W), lambda b, i: (b, i, 0)),
        out_shape=jax.ShapeDtypeStruct((B, S, ATTN_W), BF16),
        scratch_shapes=[pltpu.VMEM((S, TQ), F32),
                        pltpu.VMEM((ATTN_HEADS, CK, TQ), F32),
                        pltpu.VMEM((ATTN_HEADS, CK, TQ), F32),
                        pltpu.VMEM((ATTN_HEADS // 2, LANES, 2 * TQ), BF16),
                        pltpu.VMEM((ATTN_HEADS, TQ), F32),
                        pltpu.VMEM((ATTN_HEADS, TQ), F32)]
                       + [pltpu.VMEM((ATTN_HEAD_DIM, TQ), F32)] * ATTN_HEADS,
        compiler_params=_cparams(2),
        name="sparse_attention",
    )(far, fm, iwT, pos_row, pos_col, pos_row, ik, fm, tokb, fm, tbl, tril)


def _retention_kernel(q_ref, k_ref, v_ref, g_ref, decay_ref, xi_ref, zeta_ref, gch_ref, o_ref, r_ref):
    @pl.when(pl.program_id(1) == 0)
    def _():
        r_ref[...] = jnp.zeros(r_ref.shape, F32)

    C = RET_CHUNK
    lane = lax.broadcasted_iota(jnp.int32, (C, LANES), 1)
    row = lax.broadcasted_iota(jnp.int32, (LANES, RET_V_DIM), 0)
    for pair in range(RET_HEADS // 2):
        q_pair = q_ref[:, pair * LANES:(pair + 1) * LANES]
        k_pair = k_ref[:, pair * LANES:(pair + 1) * LANES]
        v_pair = v_ref[:, 2 * pair * RET_V_DIM:(2 * pair + 2) * RET_V_DIM]
        r_pair = r_ref[pair]
        r_bf = r_pair.astype(BF16)
        for sub in range(2):
            h = 2 * pair + sub
            in_head = (lane >= sub * RET_QK_DIM) & (lane < (sub + 1) * RET_QK_DIM)
            qm = jnp.where(in_head, q_pair, jnp.zeros_like(q_pair))
            v_h = v_pair[:, sub * RET_V_DIM:(sub + 1) * RET_V_DIM]
            inner = lax.dot_general(qm, k_pair, (((1,), (1,)), ((), ())),
                                    preferred_element_type=F32) * decay_ref[h]
            o = (jnp.dot(inner.astype(BF16), v_h, preferred_element_type=F32)
                 + jnp.dot(qm, r_bf, preferred_element_type=F32) * xi_ref[h])
            mu = jnp.mean(o, axis=-1, keepdims=True)
            d = o - mu
            var = jnp.mean(d * d, axis=-1, keepdims=True)
            hn = d * lax.rsqrt(var + LN_EPS)
            gate = g_ref[:, h * RET_V_DIM:(h + 1) * RET_V_DIM].astype(F32)
            o_ref[:, h * RET_V_DIM:(h + 1) * RET_V_DIM] = (gate * hn).astype(o_ref.dtype)
        kz = (k_pair.astype(F32) * zeta_ref[pair]).astype(BF16)
        upd = lax.dot_general(kz, v_pair, (((0,), (0,)), ((), ())), preferred_element_type=F32)
        r_ref[pair] = r_pair * gch_ref[pair] + jnp.where(row < RET_QK_DIM, upd[:, :RET_V_DIM], upd[:, RET_V_DIM:])


def _retention(qk, tokb, gates, B, S):
    C = RET_CHUNK
    H = RET_HEADS
    nc = S // C
    gamma = 1.0 - 2.0 ** (-5.0 - jnp.arange(H, dtype=F32))
    log_g = jnp.log(gamma)
    n = jnp.arange(C, dtype=F32)
    diff = n[:, None] - n[None, :]
    decay_in = jnp.where(diff[None] >= 0, jnp.exp(log_g[:, None, None] * jnp.maximum(diff, 0.0)[None]), 0.0)
    xi = jnp.exp(log_g[None, :] * (n[:, None] + 1.0))
    zeta = jnp.exp(log_g[None, :] * (C - 1.0 - n[:, None]))
    g_chunk = jnp.exp(log_g * C)
    xi_b = jnp.broadcast_to(xi.T[:, :, None], (H, C, RET_V_DIM))
    zeta_b = jnp.repeat(zeta, RET_QK_DIM, axis=1).reshape(C, H // 2, LANES).transpose(1, 0, 2)
    gch_b = jnp.broadcast_to(jnp.repeat(g_chunk, RET_QK_DIM).reshape(H // 2, LANES, 1),
                             (H // 2, LANES, RET_V_DIM))
    return pl.pallas_call(
        _retention_kernel,
        grid=(B, nc),
        in_specs=[pl.BlockSpec((C, RET_QK_W), lambda b, i: (b * nc + i, 0)),
                  pl.BlockSpec((C, RET_QK_W), lambda b, i: (b * nc + i, 1)),
                  pl.BlockSpec((C, RET_V_W), lambda b, i: (b * nc + i, 0)),
                  pl.BlockSpec((C, RET_V_W), lambda b, i: (b * nc + i, 0)),
                  pl.BlockSpec((H, C, C), lambda b, i: (0, 0, 0)),
                  pl.BlockSpec((H, C, RET_V_DIM), lambda b, i: (0, 0, 0)),
                  pl.BlockSpec((H // 2, C, LANES), lambda b, i: (0, 0, 0)),
                  pl.BlockSpec((H // 2, LANES, RET_V_DIM), lambda b, i: (0, 0, 0))],
        out_specs=pl.BlockSpec((C, RET_V_W), lambda b, i: (b * nc + i, 0)),
        out_shape=jax.ShapeDtypeStruct((B * S, RET_V_W), BF16),
        scratch_shapes=[pltpu.VMEM((H // 2, LANES, RET_V_DIM), F32)],
        compiler_params=_cparams(2),
        name="retention",
    )(qk, qk, tokb, gates, decay_in, xi_b, zeta_b, gch_b)


def _layer_norm(z, g, b):
    mu = jnp.mean(z, axis=-1, keepdims=True)
    d = z - mu
    var = jnp.mean(d * d, axis=-1, keepdims=True)
    return d * lax.rsqrt(var + LN_EPS) * g + b


def _merge_kernel(x_ref, ya_ref, yr_ref, ga_ref, gr_ref, wa_ref, wr_ref, wo_ref, g_ref, b_ref,
                  x1_ref, x1b_ref):
    a = jnp.dot(ya_ref[...], wa_ref[...], preferred_element_type=F32)
    r = jnp.dot(yr_ref[...], wr_ref[...], preferred_element_type=F32)
    h = ga_ref[...].astype(F32) * a + gr_ref[...].astype(F32) * r
    mix = jnp.dot(h.astype(BF16), wo_ref[...], preferred_element_type=F32)
    x1 = _layer_norm(DEEPNORM_ALPHA * x_ref[...] + mix, g_ref[...], b_ref[...])
    x1_ref[...] = x1
    x1b_ref[...] = x1.astype(BF16)


def _merge(x, ya, yr, gates, wa, wr, wo, g, b, tm=512):
    T, D = x.shape
    tm = min(tm, T)
    row = lambda i: (i, 0)
    fixed = lambda i: (0, 0)
    return pl.pallas_call(
        _merge_kernel,
        grid=(T // tm,),
        in_specs=[pl.BlockSpec((tm, D), row),
                  pl.BlockSpec((tm, ya.shape[1]), row),
                  pl.BlockSpec((tm, yr.shape[1]), row),
                  pl.BlockSpec((tm, D), lambda i: (i, 1)),
                  pl.BlockSpec((tm, D), lambda i: (i, 2)),
                  pl.BlockSpec(wa.shape, fixed),
                  pl.BlockSpec(wr.shape, fixed),
                  pl.BlockSpec(wo.shape, fixed),
                  pl.BlockSpec((1, D), fixed),
                  pl.BlockSpec((1, D), fixed)],
        out_specs=[pl.BlockSpec((tm, D), row), pl.BlockSpec((tm, D), row)],
        out_shape=[jax.ShapeDtypeStruct((T, D), F32), jax.ShapeDtypeStruct((T, D), BF16)],
        compiler_params=_cparams(1),
        name="merge",
    )(x, ya, yr, gates, gates, wa, wr, wo, g, b)


def _ffn_kernel(x1b_ref, x1_ref, wu_ref, wd_ref, g_ref, b_ref, o_ref, acc_ref):
    f = pl.program_id(1)

    @pl.when(f == 0)
    def _():
        acc_ref[...] = jnp.zeros(acc_ref.shape, F32)

    hid = jnp.maximum(jnp.dot(x1b_ref[...], wu_ref[...], preferred_element_type=F32), 0.0)
    acc_ref[...] += jnp.dot((hid * hid).astype(BF16), wd_ref[...], preferred_element_type=F32)

    @pl.when(f == pl.num_programs(1) - 1)
    def _():
        o_ref[...] = _layer_norm(DEEPNORM_ALPHA * x1_ref[...] + acc_ref[...], g_ref[...], b_ref[...])


def _ffn(x1b, x1, wu, wd, g, b, tm=1024, tf=1024):
    T, D = x1.shape
    F = wu.shape[1]
    tm = min(tm, T)
    return pl.pallas_call(
        _ffn_kernel,
        grid=(T // tm, F // tf),
        in_specs=[pl.BlockSpec((tm, D), lambda i, f: (i, 0)),
                  pl.BlockSpec((tm, D), lambda i, f: (i, 0)),
                  pl.BlockSpec((D, tf), lambda i, f: (0, f)),
                  pl.BlockSpec((tf, D), lambda i, f: (f, 0)),
                  pl.BlockSpec((1, D), lambda i, f: (0, 0)),
                  pl.BlockSpec((1, D), lambda i, f: (0, 0))],
        out_specs=pl.BlockSpec((tm, D), lambda i, f: (i, 0)),
        out_shape=jax.ShapeDtypeStruct((T, D), F32),
        scratch_shapes=[pltpu.VMEM((tm, D), F32)],
        compiler_params=_cparams(2),
        name="ffn",
    )(x1b, x1, wu, wd, g, b)


def _rot_half_weight(w):
    D, N = w.shape
    half = RET_QK_DIM // 2
    wh = w.reshape(D, N // RET_QK_DIM, 2, half)
    return jnp.stack([-wh[:, :, 1], wh[:, :, 0]], axis=2).reshape(D, N)


def kernel(x, positions, w_in, rel_bias, idx_k_ln_g, idx_k_ln_b, w_attn_branch, w_ret_branch,
           w_out, ln_mix_g, ln_mix_b, w_up, w_down, ln_ffn_g, ln_ffn_b):
    B, S, D = x.shape
    T = B * S
    sizes = (ATTN_W, ATTN_W, ATTN_W, IDX_Q_W, IDX_HEAD_DIM, IDX_HEADS,
             RET_QK_W, RET_QK_W, RET_V_W, RET_V_W, D, D)
    offs = [0] + [int(o) for o in np.cumsum(sizes)]
    cos, sin = _rope_tables(positions)
    xf = x.reshape(T, D)
    for l in range(DEPTH):
        wl = w_in[l].astype(BF16)
        cols = [wl[:, offs[k]:offs[k + 1]] for k in range(len(sizes))]
        (w_qa, w_ka, w_va, w_iq, w_ik, w_iw, w_qr, w_kr, w_vr, w_gr, w_ga, w_gtr) = cols
        xb = xf.astype(BF16)
        w_qa = (w_in[l][:, offs[0]:offs[1]] * (ATTN_HEAD_DIM ** -0.5 * LOG2E)).astype(BF16)
        w_kr = w_kr * (RET_QK_DIM ** -0.5)

        fm = _proj_t(xb, jnp.concatenate([w_qa, w_va, w_iq], axis=1).T.astype(BF16), B, S, BF16)
        tokb = _proj(xb, jnp.concatenate([w_vr, w_ka], axis=1).astype(BF16), BF16)
        gates = _proj_gates(xb, jnp.concatenate([w_gr, w_ga, w_gtr], axis=1).astype(BF16), tn=D)
        w_rope = jnp.concatenate([w_qr, w_kr], axis=1)
        w_rope_rot = jnp.concatenate([_rot_half_weight(w_qr), _rot_half_weight(w_kr)], axis=1)
        qk_r = _proj_rope(xb, w_rope.astype(BF16), w_rope_rot.astype(BF16), cos, sin)
        pad = LANES - IDX_HEAD_DIM - IDX_HEADS
        w_idx = jnp.concatenate([w_ik, w_iw, jnp.zeros((D, pad), BF16)], axis=1)
        g_pad = jnp.concatenate([idx_k_ln_g[l], jnp.zeros((LANES - IDX_HEAD_DIM,), F32)]).reshape(1, LANES)
        b_pad = jnp.concatenate([idx_k_ln_b[l], jnp.zeros((LANES - IDX_HEAD_DIM,), F32)]).reshape(1, LANES)
        idx = _proj_idx(xb, w_idx, g_pad, b_pad,
                        (IDX_HEAD_DIM ** -0.5) * (IDX_HEADS ** -0.5)).reshape(B, S, LANES)
        ik = idx[:, :, :IDX_HEAD_DIM].astype(BF16)
        iwT = jnp.swapaxes(idx[:, :, IDX_HEAD_DIM:IDX_HEAD_DIM + SUBLANES], 1, 2)

        y_a = _sparse_attention(fm, tokb.reshape(B, S, -1), ik, iwT, positions, rel_bias)
        y_r = _retention(qk_r, tokb, gates, B, S)
        x1, x1b = _merge(xf, y_a.reshape(T, ATTN_W), y_r, gates,
                         w_attn_branch[l].astype(BF16), w_ret_branch[l].astype(BF16),
                         w_out[l].astype(BF16), ln_mix_g[l].reshape(1, D), ln_mix_b[l].reshape(1, D))
        xf = _ffn(x1b, x1, w_up[l].astype(BF16), w_down[l].astype(BF16),
                  ln_ffn_g[l].reshape(1, D), ln_ffn_b[l].reshape(1, D))
    return xf.reshape(B, S, D)
```

```python
import functools
import math

import numpy as np
import jax
import jax.numpy as jnp
from jax import lax
from jax.experimental import pallas as pl
from jax.experimental.pallas import tpu as pltpu

F32 = jnp.float32
BF16 = jnp.bfloat16

ATTN_HEADS = 8
ATTN_HEAD_DIM = 64
ATTN_W = ATTN_HEADS * ATTN_HEAD_DIM
IDX_HEADS = 4
IDX_HEAD_DIM = 64
IDX_Q_W = IDX_HEADS * IDX_HEAD_DIM
TOPK_MAX = 256
RET_HEADS = 8
RET_QK_DIM = 64
RET_V_DIM = 128
RET_QK_W = RET_HEADS * RET_QK_DIM
RET_V_W = RET_HEADS * RET_V_DIM
RET_CHUNK = 128
ROPE_BASE = 10000.0
NUM_BUCKETS = 32
MAX_DISTANCE = 128
LN_EPS = 1e-5
DEPTH = 1
DEEPNORM_ALPHA = (2.0 * DEPTH) ** 0.25

LANES = 128
SUBLANES = 8
VMEM_LIMIT = 56 * 1024 * 1024

TQ = 128
CK = 512
NEG = -1e30
LOG2E = math.log2(math.e)
BISECT_ROUNDS = 18
BIAS_TABLE_N = 128
FAR_N = 113


def _cparams(n_grid):
    return pltpu.CompilerParams(
        dimension_semantics=("arbitrary",) * n_grid,
        vmem_limit_bytes=VMEM_LIMIT)


def _trig_kernel(pos_ref, inv_ref, cos_ref, sin_ref):
    ang = pos_ref[...] * inv_ref[...]
    cos_ref[...] = jnp.cos(ang)
    sin_ref[...] = jnp.sin(ang)


def _rope_tables(positions):
    B, S = positions.shape
    half = RET_QK_DIM // 2
    inv = ROPE_BASE ** (-jnp.arange(half, dtype=F32) / half)
    per_row = LANES // half
    rows = B * S // per_row
    pos_e = jnp.repeat(positions.astype(F32).reshape(rows, per_row), half, axis=1)
    inv_e = jnp.tile(inv, per_row).reshape(1, LANES)
    tr = min(rows, 1024)
    cos, sin = pl.pallas_call(
        _trig_kernel,
        grid=(rows // tr,),
        in_specs=[pl.BlockSpec((tr, LANES), lambda i: (i, 0)),
                  pl.BlockSpec((1, LANES), lambda i: (0, 0))],
        out_specs=[pl.BlockSpec((tr, LANES), lambda i: (i, 0))] * 2,
        out_shape=[jax.ShapeDtypeStruct((rows, LANES), F32)] * 2,
        compiler_params=_cparams(1),
        name="rope_tables",
    )(pos_e, inv_e)
    cos = jnp.tile(cos.reshape(B * S, half), (1, per_row))
    sin = jnp.tile(sin.reshape(B * S, half), (1, per_row))
    return cos, sin


def _proj_kernel(x_ref, w_ref, o_ref):
    o_ref[...] = jnp.dot(x_ref[...], w_ref[...], preferred_element_type=F32).astype(o_ref.dtype)


def _proj(xb, w, out_dtype, tm=1024, tn=512):
    T, D = xb.shape
    N = w.shape[1]
    tn = min(N, tn)
    tm = min(tm, T)
    return pl.pallas_call(
        _proj_kernel,
        grid=(T // tm, N // tn),
        in_specs=[pl.BlockSpec((tm, D), lambda i, j: (i, 0)),
                  pl.BlockSpec((D, tn), lambda i, j: (0, j))],
        out_specs=pl.BlockSpec((tm, tn), lambda i, j: (i, j)),
        out_shape=jax.ShapeDtypeStruct((T, N), out_dtype),
        compiler_params=_cparams(2),
        name="proj",
    )(xb, w)


def _proj_gates_kernel(x_ref, w_ref, o_ref):
    acc = jnp.dot(x_ref[...], w_ref[...], preferred_element_type=F32)
    sig = 1.0 / (1.0 + jnp.exp(-acc))
    o_ref[...] = jnp.where(pl.program_id(1) == 0, acc * sig, sig).astype(o_ref.dtype)


def _proj_gates(xb, w, tn, tm=1024):
    T, D = xb.shape
    N = w.shape[1]
    tm = min(tm, T)
    return pl.pallas_call(
        _proj_gates_kernel,
        grid=(T // tm, N // tn),
        in_specs=[pl.BlockSpec((tm, D), lambda i, j: (i, 0)),
                  pl.BlockSpec((D, tn), lambda i, j: (0, j))],
        out_specs=pl.BlockSpec((tm, tn), lambda i, j: (i, j)),
        out_shape=jax.ShapeDtypeStruct((T, N), BF16),
        compiler_params=_cparams(2),
        name="proj_gates",
    )(xb, w)


def _proj_t_kernel(wT_ref, x_ref, o_ref):
    acc = lax.dot_general(wT_ref[...], x_ref[...], (((1,), (1,)), ((), ())), preferred_element_type=F32)
    o_ref[...] = acc.astype(o_ref.dtype)


def _proj_t(xb, wT, B, S, out_dtype, tm=1024, tn=640):
    T, D = xb.shape
    N = wT.shape[0]
    tn = min(N, tn)
    tm = min(tm, S)
    nsb = S // tm
    return pl.pallas_call(
        _proj_t_kernel,
        grid=(T // tm, N // tn),
        in_specs=[pl.BlockSpec((tn, D), lambda i, j: (j, 0)),
                  pl.BlockSpec((tm, D), lambda i, j: (i, 0))],
        out_specs=pl.BlockSpec((pl.Squeezed(), tn, tm), lambda i, j: (i // nsb, j, i % nsb)),
        out_shape=jax.ShapeDtypeStruct((B, N, S), out_dtype),
        compiler_params=_cparams(2),
        name="proj_t",
    )(wT, xb)


def _proj_rope_kernel(x_ref, w_ref, wr_ref, cos_ref, sin_ref, o_ref):
    x = x_ref[...]
    a = jnp.dot(x, w_ref[...], preferred_element_type=F32)
    r = jnp.dot(x, wr_ref[...], preferred_element_type=F32)
    reps = a.shape[1] // LANES
    cos = jnp.concatenate([cos_ref[...]] * reps, axis=1)
    sin = jnp.concatenate([sin_ref[...]] * reps, axis=1)
    o_ref[...] = (a * cos + r * sin).astype(o_ref.dtype)


def _proj_rope(xb, w, w_rot, cos, sin, tm=1024, tn=512):
    T, D = xb.shape
    N = w.shape[1]
    tm = min(tm, T)
    return pl.pallas_call(
        _proj_rope_kernel,
        grid=(T // tm, N // tn),
        in_specs=[pl.BlockSpec((tm, D), lambda i, j: (i, 0)),
                  pl.BlockSpec((D, tn), lambda i, j: (0, j)),
                  pl.BlockSpec((D, tn), lambda i, j: (0, j)),
                  pl.BlockSpec((tm, LANES), lambda i, j: (i, 0)),
                  pl.BlockSpec((tm, LANES), lambda i, j: (i, 0))],
        out_specs=pl.BlockSpec((tm, tn), lambda i, j: (i, j)),
        out_shape=jax.ShapeDtypeStruct((T, N), BF16),
        compiler_params=_cparams(2),
        name="proj_rope",
    )(xb, w, w_rot, cos, sin)


def _proj_idx_kernel(x_ref, w_ref, g_ref, b_ref, o_ref, *, iw_scale):
    acc = jnp.dot(x_ref[...], w_ref[...], preferred_element_type=F32)
    lane = lax.broadcasted_iota(jnp.int32, acc.shape, 1)
    is_k = lane < IDX_HEAD_DIM
    mu = jnp.sum(jnp.where(is_k, acc, 0.0), axis=-1, keepdims=True) / IDX_HEAD_DIM
    d = acc - mu
    var = jnp.sum(jnp.where(is_k, d * d, 0.0), axis=-1, keepdims=True) / IDX_HEAD_DIM
    ln = d * lax.rsqrt(var + LN_EPS) * g_ref[...] + b_ref[...]
    o_ref[...] = jnp.where(is_k, ln, acc * iw_scale)


def _proj_idx(xb, w_pad, g_pad, b_pad, iw_scale, tm=1024):
    T, D = xb.shape
    tm = min(tm, T)
    return pl.pallas_call(
        functools.partial(_proj_idx_kernel, iw_scale=iw_scale),
        grid=(T // tm,),
        in_specs=[pl.BlockSpec((tm, D), lambda i: (i, 0)),
                  pl.BlockSpec((D, LANES), lambda i: (0, 0)),
                  pl.BlockSpec((1, LANES), lambda i: (0, 0)),
                  pl.BlockSpec((1, LANES), lambda i: (0, 0))],
        out_specs=pl.BlockSpec((tm, LANES), lambda i: (i, 0)),
        out_shape=jax.ShapeDtypeStruct((T, LANES), F32),
        compiler_params=_cparams(1),
        name="proj_idx",
    )(xb, w_pad, g_pad, b_pad)


def _t5_bucket_table():
    n = np.arange(BIAS_TABLE_N)
    max_exact = NUM_BUCKETS // 2
    nf = np.maximum(n, 1).astype(np.float64)
    large = max_exact + (np.log(nf / max_exact) / math.log(MAX_DISTANCE / max_exact)
                         * (NUM_BUCKETS - max_exact)).astype(np.int64)
    large = np.minimum(large, NUM_BUCKETS - 1)
    bucket = np.where(n < max_exact, n, large)
    assert np.all(bucket[FAR_N:] == NUM_BUCKETS - 1) and bucket[FAR_N - 1] != NUM_BUCKETS - 1
    return bucket.astype(np.int32)


def _fold_rows(a, op):
    parts = [a[r:r + SUBLANES] for r in range(0, a.shape[0], SUBLANES)]
    while len(parts) > 1:
        nxt = [op(parts[k], parts[k + 1]) for k in range(0, len(parts) - 1, 2)]
        if len(parts) % 2:
            nxt.append(parts[-1])
        parts = nxt
    return parts[0]


def _attn_kernel(far_ref, pqmin_ref, ckmax_ref, iqT_ref, iwT_ref, posq_ref, posqc_ref, posk_ref, ik_ref, qT_ref, k_ref, vT_ref,
                 tbl_ref, tril_ref, o_ref, sc_ref, s_ref, s2_ref, b_ref, qm_ref, m_ref, l_ref, *acc_refs, k_sel, seq):
    i = pl.program_id(1)
    nch = (i * TQ + TQ + CK - 1) // CK
    q_idx = i * TQ + lax.broadcasted_iota(jnp.int32, (1, TQ), 1)
    kf = float(k_sel)

    def chunk_off(c):
        return pl.multiple_of(c * CK, CK)

    def key_idx(off):
        return off + lax.broadcasted_iota(jnp.int32, (CK, TQ), 0)

    def col_reduce(part, op):
        return op(part, axis=0, keepdims=True)

    iqT = iqT_ref[...]
    iwT = iwT_ref[...]

    def score_body(c, carry):
        mn, mx = carry
        off = chunk_off(c)
        ikc = ik_ref[pl.ds(off, CK), :]
        s = None
        for h in range(IDX_HEADS):
            z = jnp.dot(ikc, iqT[h * IDX_HEAD_DIM:(h + 1) * IDX_HEAD_DIM, :], preferred_element_type=F32)
            t = jnp.maximum(z, 0.0) * iwT[h:h + 1, :]
            s = t if s is None else s + t
        causal = key_idx(off) <= q_idx
        s_lo = jnp.where(causal, s, -jnp.inf)
        sc_ref[pl.ds(off, CK), :] = s_lo
        mn = jnp.minimum(mn, _fold_rows(jnp.where(causal, s, jnp.inf), jnp.minimum))
        mx = jnp.maximum(mx, _fold_rows(s_lo, jnp.maximum))
        return mn, mx

    mn8, mx8 = lax.fori_loop(0, nch, score_body,
                             (jnp.full((SUBLANES, TQ), jnp.inf, F32), jnp.full((SUBLANES, TQ), -jnp.inf, F32)))
    mn = col_reduce(mn8, jnp.min)
    mx = col_reduce(mx8, jnp.max)

    def count(pred_fn):
        def body(c, acc):
            off = chunk_off(c)
            blk = sc_ref[pl.ds(off, CK), :]
            return acc + _fold_rows(jnp.where(pred_fn(blk, off), 1.0, 0.0), jnp.add)
        acc = lax.fori_loop(0, nch, body, jnp.zeros((SUBLANES, TQ), F32))
        return col_reduce(acc, jnp.sum)

    def bisect_round(_, st):
        lo, hi, c_lo = st
        mid = 0.5 * (lo + hi)
        c = count(lambda blk, off: blk >= mid)
        ok = c >= kf
        return jnp.where(ok, mid, lo), jnp.where(ok, hi, mid), jnp.where(ok, c, c_lo)

    c_all = (q_idx + 1).astype(F32)
    lo, hi, c_lo = lax.fori_loop(0, BISECT_ROUNDS, bisect_round, (mn, mx, c_all))

    def min_ge_body(c, acc):
        blk = sc_ref[pl.ds(chunk_off(c), CK), :]
        return jnp.minimum(acc, _fold_rows(jnp.where(blk >= lo, blk, jnp.inf), jnp.minimum))

    cur0 = col_reduce(lax.fori_loop(0, nch, min_ge_body, jnp.full((SUBLANES, TQ), jnp.inf, F32)), jnp.min)

    def walk_cond(st):
        return st[3] > 0.0

    def walk_body(st):
        cur, c_ge, _, _ = st

        def body(c, carry):
            cnt, nxt = carry
            blk = sc_ref[pl.ds(chunk_off(c), CK), :]
            gt = blk > cur
            cnt = cnt + _fold_rows(jnp.where(gt, 1.0, 0.0), jnp.add)
            nxt = jnp.minimum(nxt, _fold_rows(jnp.where(gt, blk, jnp.inf), jnp.minimum))
            return cnt, nxt

        cnt, nxt = lax.fori_loop(0, nch, body,
                                 (jnp.zeros((SUBLANES, TQ), F32), jnp.full((SUBLANES, TQ), jnp.inf, F32)))
        c_gt = col_reduce(cnt, jnp.sum)
        nxt = col_reduce(nxt, jnp.min)
        adv = c_gt >= kf
        cur = jnp.where(adv, nxt, cur)
        c_ge = jnp.where(adv, c_gt, c_ge)
        return cur, c_ge, c_gt, jnp.max(jnp.where(adv, 1.0, 0.0))

    tau, c_ge, c_gt, _ = lax.while_loop(
        walk_cond, walk_body, (cur0, c_lo, jnp.zeros((1, TQ), F32), jnp.float32(1.0)))

    trim = c_ge > kf

    def mask_with_ties():
        room = kf - c_gt

        def mask_body(c, seen):
            off = chunk_off(c)
            for j in range(CK // LANES):
                rows = pl.ds(off + j * LANES, LANES)
                blk = sc_ref[rows, :]
                tie = blk == tau
                rank = seen + jnp.dot(tril_ref[...], jnp.where(tie, 1.0, 0.0).astype(BF16),
                                      preferred_element_type=F32)
                keep = (blk > tau) | (tie & (rank <= room))
                sc_ref[rows, :] = jnp.where(keep, 0.0, NEG)
                seen = rank[LANES - 1:LANES, :]
            return seen

        lax.fori_loop(0, nch, mask_body, jnp.zeros((1, TQ), F32))

    def mask_plain():
        def mask_body(c, _):
            off = chunk_off(c)
            sc_ref[pl.ds(off, CK), :] = jnp.where(sc_ref[pl.ds(off, CK), :] >= tau, 0.0, NEG)
            return 0

        lax.fori_loop(0, nch, mask_body, 0)

    lax.cond(jnp.max(jnp.where(trim, 1.0, 0.0)) > 0.0, mask_with_ties, mask_plain)

    m_ref[...] = jnp.full(m_ref.shape, NEG, F32)
    l_ref[...] = jnp.zeros(l_ref.shape, F32)
    for acc in acc_refs:
        acc[...] = jnp.zeros(acc.shape, F32)

    rowi = lax.broadcasted_iota(jnp.int32, (LANES, TQ), 0)
    for pair in range(ATTN_HEADS // 2):
        qp = qT_ref[pair * LANES:(pair + 1) * LANES, :]
        zero = jnp.zeros_like(qp)
        qm_ref[pair] = jnp.concatenate([jnp.where(rowi < ATTN_HEAD_DIM, qp, zero),
                                        jnp.where(rowi >= ATTN_HEAD_DIM, qp, zero)], axis=1)
    ones_rows = jnp.ones((2 * SUBLANES, CK), BF16)

    pq_col = posqc_ref[...]
    batch = pl.program_id(0)
    pq_min = pqmin_ref[batch, i]
    n_sub = CK // LANES
    n_pairs = ATTN_HEADS // 2

    def chunk_is_far(c):
        return (pq_min - ckmax_ref[batch, jnp.minimum(c, seq // CK - 1)]) >= FAR_N

    def stage_bias(c):
        off = chunk_off(c)
        pk_row = posk_ref[:, pl.ds(off, CK)]
        for j in range(n_sub):
            rows = slice(j * LANES, (j + 1) * LANES)
            pk_sub = pk_row[:, rows]
            all_far = (pq_min - jnp.max(pk_sub)) >= FAR_N
            all_masked = (off + j * LANES) > (i * TQ + TQ - 1)

            def fill_const(rows=rows):
                for h in range(ATTN_HEADS):
                    b_ref[h, rows, :] = jnp.full((LANES, TQ), far_ref[h], F32)

            def fill_lookup(rows=rows, pk_sub=pk_sub):
                n_qk = jnp.clip(pq_col - pk_sub, 0, BIAS_TABLE_N - 1).astype(F32)
                n_kq = n_qk.T.astype(jnp.int32)
                for h in range(ATTN_HEADS):
                    tb = jnp.broadcast_to(tbl_ref[h:h + 1, :], (LANES, BIAS_TABLE_N))
                    b_ref[h, rows, :] = jnp.take_along_axis(tb, n_kq, axis=1)

            lax.cond(all_far | all_masked, fill_const, fill_lookup)

    def logits_phase(c, s_buf, const_bias):
        off = chunk_off(c)
        mb = sc_ref[pl.ds(off, CK), :]
        m_cur = []
        for pair in range(n_pairs):
            kc = k_ref[pl.ds(off, CK), pair * LANES:(pair + 1) * LANES]
            s2 = jnp.dot(kc, qm_ref[pair], preferred_element_type=F32)
            for sub, h in enumerate((2 * pair, 2 * pair + 1)):
                s = s2[:, sub * TQ:(sub + 1) * TQ] + mb
                if not const_bias:
                    s = s + b_ref[h]
                s_buf[h] = s
                top = col_reduce(_fold_rows(s, jnp.maximum), jnp.max)
                m_cur.append(top + far_ref[h] if const_bias else top)
        return jnp.concatenate(m_cur, axis=0)

    def update_phase(c, s_buf, m_cur, const_bias):
        off = chunk_off(c)
        m_prev = m_ref[...]
        m_new = jnp.maximum(m_prev, m_cur)
        alpha = jnp.exp2(m_prev - m_new)
        l_prev = l_ref[...]
        l_new = []
        for pair in range(n_pairs):
            heads = (2 * pair, 2 * pair + 1)
            sub_m = [m_new[h:h + 1, :] - far_ref[h] if const_bias else m_new[h:h + 1, :] for h in heads]
            p2 = jnp.concatenate([jnp.exp2(s_buf[h] - sm).astype(BF16) for h, sm in zip(heads, sub_m)],
                                 axis=1)
            lhs = jnp.concatenate([vT_ref[pair * LANES:(pair + 1) * LANES, pl.ds(off, CK)], ones_rows],
                                  axis=0)
            out = jnp.dot(lhs, p2, preferred_element_type=F32)
            for sub, h in enumerate(heads):
                cols = slice(sub * TQ, (sub + 1) * TQ)
                acc = acc_refs[h]
                acc[...] = alpha[h:h + 1, :] * acc[...] + out[sub * ATTN_HEAD_DIM:(sub + 1) * ATTN_HEAD_DIM, cols]
                l_new.append(alpha[h:h + 1, :] * l_prev[h:h + 1, :] + out[LANES:LANES + 1, cols])
        l_ref[...] = jnp.concatenate(l_new, axis=0)
        m_ref[...] = m_new

    n_far = lax.while_loop(lambda c: (c < nch) & chunk_is_far(c), lambda c: c + 1, jnp.int32(0))
    n_steps = n_far // 2
    m_first = lax.cond(n_steps > 0, lambda: logits_phase(0, s_ref, True),
                       lambda: jnp.zeros((ATTN_HEADS, TQ), F32))

    def far_step(t, m_even):
        c = 2 * t
        m_odd = logits_phase(c + 1, s2_ref, True)
        update_phase(c, s_ref, m_even, True)
        m_even = logits_phase(jnp.minimum(c + 2, 2 * n_steps - 2), s_ref, True)
        update_phase(c + 1, s2_ref, m_odd, True)
        return m_even

    lax.fori_loop(0, n_steps, far_step, m_first)

    def tail_chunk(c, _):
        def run(const_bias):
            if not const_bias:
                stage_bias(c)
            update_phase(c, s_ref, logits_phase(c, s_ref, const_bias), const_bias)

        lax.cond(chunk_is_far(c), lambda: run(True), lambda: run(False))
        return 0

    lax.fori_loop(2 * n_steps, nch, tail_chunk, 0)

    outT = jnp.concatenate([acc_refs[h][...] / l_ref[h:h + 1, :] for h in range(ATTN_HEADS)], axis=0)
    o_ref[...] = outT.T.astype(o_ref.dtype)


def _sparse_attention(fm, tokb, ik, iwT, positions, rel_bias):
    B, S, _ = tokb.shape
    k_sel = min(TOPK_MAX, S // 4)
    bucket = _t5_bucket_table()
    tbl = rel_bias[bucket].T.astype(F32) * LOG2E
    far = rel_bias[NUM_BUCKETS - 1].astype(F32) * LOG2E
    pos_row = positions.reshape(B, 1, S)
    pos_col = positions.reshape(B, S, 1)
    pq_min = jnp.min(positions.reshape(B, S // TQ, TQ), axis=-1)
    ck_max = jnp.max(positions.reshape(B, S // CK, CK), axis=-1)
    tril = jnp.tril(jnp.ones((LANES, LANES), BF16))
    sq = pl.Squeezed()
    iq_blk = (2 * ATTN_W) // IDX_Q_W
    k_blk = (tokb.shape[2] - ATTN_W) // ATTN_W
    return pl.pallas_call(
        functools.partial(_attn_kernel, k_sel=k_sel, seq=S),
        grid=(B, S // TQ),
        in_specs=[pl.BlockSpec(memory_space=pltpu.SMEM),
                  pl.BlockSpec(memory_space=pltpu.SMEM),
                  pl.BlockSpec(memory_space=pltpu.SMEM),
                  pl.BlockSpec((sq, IDX_Q_W, TQ), lambda b, i: (b, iq_blk, i)),
                  pl.BlockSpec((sq, SUBLANES, TQ), lambda b, i: (b, 0, i)),
                  pl.BlockSpec((sq, 1, TQ), lambda b, i: (b, 0, i)),
                  pl.BlockSpec((sq, TQ, 1), lambda b, i: (b, i, 0)),
                  pl.BlockSpec((sq, 1, S), lambda b, i: (b, 0, 0)),
                  pl.BlockSpec((sq, S, IDX_HEAD_DIM), lambda b, i: (b, 0, 0)),
                  pl.BlockSpec((sq, ATTN_W, TQ), lambda b, i: (b, 0, i)),
                  pl.BlockSpec((sq, S, ATTN_W), lambda b, i: (b, 0, k_blk)),
                  pl.BlockSpec((sq, ATTN_W, S), lambda b, i: (b, 1, 0)),
                  pl.BlockSpec((ATTN_HEADS, BIAS_TABLE_N), lambda b, i: (0, 0)),
                  pl.BlockSpec((LANES, LANES), lambda b, i: (0, 0))],
        out_specs=pl.BlockSpec((sq, TQ, ATTN_W), lambda b, i: (b, i, 0)),
        out_shape=jax.ShapeDtypeStruct((B, S, ATTN_W), BF16),
        scratch_shapes=[pltpu.VMEM((S, TQ), F32),
                        pltpu.VMEM((ATTN_HEADS, CK, TQ), F32),
                        pltpu.VMEM((ATTN_HEADS, CK, TQ), F32),
                        pltpu.VMEM((ATTN_HEADS, CK, TQ), F32),
                        pltpu.VMEM((ATTN_HEADS // 2, LANES, 2 * TQ), BF16),
                        pltpu.VMEM((ATTN_HEADS, TQ), F32),
                        pltpu.VMEM((ATTN_HEADS, TQ), F32)]
                       + [pltpu.VMEM((ATTN_HEAD_DIM, TQ), F32)] * ATTN_HEADS,
        compiler_params=_cparams(2),
        name="sparse_attention",
    )(far, pq_min, ck_max, fm, iwT, pos_row, pos_col, pos_row, ik, fm, tokb, fm, tbl, tril)


def _retention_kernel(q_ref, k_ref, v_ref, g_ref, decay_ref, xi_ref, zeta_ref, gch_ref, o_ref, r_ref):
    @pl.when(pl.program_id(1) == 0)
    def _():
        r_ref[...] = jnp.zeros(r_ref.shape, F32)

    C = RET_CHUNK
    lane = lax.broadcasted_iota(jnp.int32, (C, LANES), 1)
    row = lax.broadcasted_iota(jnp.int32, (LANES, RET_V_DIM), 0)
    for pair in range(RET_HEADS // 2):
        q_pair = q_ref[:, pair * LANES:(pair + 1) * LANES]
        k_pair = k_ref[:, pair * LANES:(pair + 1) * LANES]
        v_pair = v_ref[:, 2 * pair * RET_V_DIM:(2 * pair + 2) * RET_V_DIM]
        r_pair = r_ref[pair]
        r_bf = r_pair.astype(BF16)
        for sub in range(2):
            h = 2 * pair + sub
            in_head = (lane >= sub * RET_QK_DIM) & (lane < (sub + 1) * RET_QK_DIM)
            qm = jnp.where(in_head, q_pair, jnp.zeros_like(q_pair))
            v_h = v_pair[:, sub * RET_V_DIM:(sub + 1) * RET_V_DIM]
            inner = lax.dot_general(qm, k_pair, (((1,), (1,)), ((), ())),
                                    preferred_element_type=F32) * decay_ref[h]
            o = (jnp.dot(inner.astype(BF16), v_h, preferred_element_type=F32)
                 + jnp.dot(qm, r_bf, preferred_element_type=F32) * xi_ref[h])
            mu = jnp.mean(o, axis=-1, keepdims=True)
            d = o - mu
            var = jnp.mean(d * d, axis=-1, keepdims=True)
            hn = d * lax.rsqrt(var + LN_EPS)
            gate = g_ref[:, h * RET_V_DIM:(h + 1) * RET_V_DIM].astype(F32)
            o_ref[:, h * RET_V_DIM:(h + 1) * RET_V_DIM] = (gate * hn).astype(o_ref.dtype)
        kz = (k_pair.astype(F32) * zeta_ref[pair]).astype(BF16)
        upd = lax.dot_general(kz, v_pair, (((0,), (0,)), ((), ())), preferred_element_type=F32)
        r_ref[pair] = r_pair * gch_ref[pair] + jnp.where(row < RET_QK_DIM, upd[:, :RET_V_DIM], upd[:, RET_V_DIM:])


def _retention(qk, tokb, gates, B, S):
    C = RET_CHUNK
    H = RET_HEADS
    nc = S // C
    gamma = 1.0 - 2.0 ** (-5.0 - jnp.arange(H, dtype=F32))
    log_g = jnp.log(gamma)
    n = jnp.arange(C, dtype=F32)
    diff = n[:, None] - n[None, :]
    decay_in = jnp.where(diff[None] >= 0, jnp.exp(log_g[:, None, None] * jnp.maximum(diff, 0.0)[None]), 0.0)
    xi = jnp.exp(log_g[None, :] * (n[:, None] + 1.0))
    zeta = jnp.exp(log_g[None, :] * (C - 1.0 - n[:, None]))
    g_chunk = jnp.exp(log_g * C)
    xi_b = jnp.broadcast_to(xi.T[:, :, None], (H, C, RET_V_DIM))
    zeta_b = jnp.repeat(zeta, RET_QK_DIM, axis=1).reshape(C, H // 2, LANES).transpose(1, 0, 2)
    gch_b = jnp.broadcast_to(jnp.repeat(g_chunk, RET_QK_DIM).reshape(H // 2, LANES, 1),
                             (H // 2, LANES, RET_V_DIM))
    return pl.pallas_call(
        _retention_kernel,
        grid=(B, nc),
        in_specs=[pl.BlockSpec((C, RET_QK_W), lambda b, i: (b * nc + i, 0)),
                  pl.BlockSpec((C, RET_QK_W), lambda b, i: (b * nc + i, 1)),
                  pl.BlockSpec((C, RET_V_W), lambda b, i: (b * nc + i, 0)),
                  pl.BlockSpec((C, RET_V_W), lambda b, i: (b * nc + i, 0)),
                  pl.BlockSpec((H, C, C), lambda b, i: (0, 0, 0)),
                  pl.BlockSpec((H, C, RET_V_DIM), lambda b, i: (0, 0, 0)),
                  pl.BlockSpec((H // 2, C, LANES), lambda b, i: (0, 0, 0)),
                  pl.BlockSpec((H // 2, LANES, RET_V_DIM), lambda b, i: (0, 0, 0))],
        out_specs=pl.BlockSpec((C, RET_V_W), lambda b, i: (b * nc + i, 0)),
        out_shape=jax.ShapeDtypeStruct((B * S, RET_V_W), BF16),
        scratch_shapes=[pltpu.VMEM((H // 2, LANES, RET_V_DIM), F32)],
        compiler_params=_cparams(2),
        name="retention",
    )(qk, qk, tokb, gates, decay_in, xi_b, zeta_b, gch_b)


def _layer_norm(z, g, b):
    mu = jnp.mean(z, axis=-1, keepdims=True)
    d = z - mu
    var = jnp.mean(d * d, axis=-1, keepdims=True)
    return d * lax.rsqrt(var + LN_EPS) * g + b


def _merge_kernel(x_ref, ya_ref, yr_ref, ga_ref, gr_ref, wa_ref, wr_ref, wo_ref, g_ref, b_ref,
                  x1_ref, x1b_ref):
    a = jnp.dot(ya_ref[...], wa_ref[...], preferred_element_type=F32)
    r = jnp.dot(yr_ref[...], wr_ref[...], preferred_element_type=F32)
    h = ga_ref[...].astype(F32) * a + gr_ref[...].astype(F32) * r
    mix = jnp.dot(h.astype(BF16), wo_ref[...], preferred_element_type=F32)
    x1 = _layer_norm(DEEPNORM_ALPHA * x_ref[...] + mix, g_ref[...], b_ref[...])
    x1_ref[...] = x1
    x1b_ref[...] = x1.astype(BF16)


def _merge(x, ya, yr, gates, wa, wr, wo, g, b, tm=512):
    T, D = x.shape
    tm = min(tm, T)
    row = lambda i: (i, 0)
    fixed = lambda i: (0, 0)
    return pl.pallas_call(
        _merge_kernel,
        grid=(T // tm,),
        in_specs=[pl.BlockSpec((tm, D), row),
                  pl.BlockSpec((tm, ya.shape[1]), row),
                  pl.BlockSpec((tm, yr.shape[1]), row),
                  pl.BlockSpec((tm, D), lambda i: (i, 1)),
                  pl.BlockSpec((tm, D), lambda i: (i, 2)),
                  pl.BlockSpec(wa.shape, fixed),
                  pl.BlockSpec(wr.shape, fixed),
                  pl.BlockSpec(wo.shape, fixed),
                  pl.BlockSpec((1, D), fixed),
                  pl.BlockSpec((1, D), fixed)],
        out_specs=[pl.BlockSpec((tm, D), row), pl.BlockSpec((tm, D), row)],
        out_shape=[jax.ShapeDtypeStruct((T, D), F32), jax.ShapeDtypeStruct((T, D), BF16)],
        compiler_params=_cparams(1),
        name="merge",
    )(x, ya, yr, gates, gates, wa, wr, wo, g, b)


def _ffn_kernel(x1b_ref, x1_ref, wu_ref, wd_ref, g_ref, b_ref, o_ref, acc_ref):
    f = pl.program_id(1)

    @pl.when(f == 0)
    def _():
        acc_ref[...] = jnp.zeros(acc_ref.shape, F32)

    hid = jnp.maximum(jnp.dot(x1b_ref[...], wu_ref[...], preferred_element_type=F32), 0.0)
    acc_ref[...] += jnp.dot((hid * hid).astype(BF16), wd_ref[...], preferred_element_type=F32)

    @pl.when(f == pl.num_programs(1) - 1)
    def _():
        o_ref[...] = _layer_norm(DEEPNORM_ALPHA * x1_ref[...] + acc_ref[...], g_ref[...], b_ref[...])


def _ffn(x1b, x1, wu, wd, g, b, tm=1024, tf=1024):
    T, D = x1.shape
    F = wu.shape[1]
    tm = min(tm, T)
    return pl.pallas_call(
        _ffn_kernel,
        grid=(T // tm, F // tf),
        in_specs=[pl.BlockSpec((tm, D), lambda i, f: (i, 0)),
                  pl.BlockSpec((tm, D), lambda i, f: (i, 0)),
                  pl.BlockSpec((D, tf), lambda i, f: (0, f)),
                  pl.BlockSpec((tf, D), lambda i, f: (f, 0)),
                  pl.BlockSpec((1, D), lambda i, f: (0, 0)),
                  pl.BlockSpec((1, D), lambda i, f: (0, 0))],
        out_specs=pl.BlockSpec((tm, D), lambda i, f: (i, 0)),
        out_shape=jax.ShapeDtypeStruct((T, D), F32),
        scratch_shapes=[pltpu.VMEM((tm, D), F32)],
        compiler_params=_cparams(2),
        name="ffn",
    )(x1b, x1, wu, wd, g, b)


def _rot_half_weight(w):
    D, N = w.shape
    half = RET_QK_DIM // 2
    wh = w.reshape(D, N // RET_QK_DIM, 2, half)
    return jnp.stack([-wh[:, :, 1], wh[:, :, 0]], axis=2).reshape(D, N)


def kernel(x, positions, w_in, rel_bias, idx_k_ln_g, idx_k_ln_b, w_attn_branch, w_ret_branch,
           w_out, ln_mix_g, ln_mix_b, w_up, w_down, ln_ffn_g, ln_ffn_b):
    B, S, D = x.shape
    T = B * S
    sizes = (ATTN_W, ATTN_W, ATTN_W, IDX_Q_W, IDX_HEAD_DIM, IDX_HEADS,
             RET_QK_W, RET_QK_W, RET_V_W, RET_V_W, D, D)
    offs = [0] + [int(o) for o in np.cumsum(sizes)]
    cos, sin = _rope_tables(positions)
    xf = x.reshape(T, D)
    for l in range(DEPTH):
        wl = w_in[l].astype(BF16)
        cols = [wl[:, offs[k]:offs[k + 1]] for k in range(len(sizes))]
        (w_qa, w_ka, w_va, w_iq, w_ik, w_iw, w_qr, w_kr, w_vr, w_gr, w_ga, w_gtr) = cols
        xb = xf.astype(BF16)
        w_qa = (w_in[l][:, offs[0]:offs[1]] * (ATTN_HEAD_DIM ** -0.5 * LOG2E)).astype(BF16)
        w_kr = w_kr * (RET_QK_DIM ** -0.5)

        fm = _proj_t(xb, jnp.concatenate([w_qa, w_va, w_iq], axis=1).T.astype(BF16), B, S, BF16)
        tokb = _proj(xb, jnp.concatenate([w_vr, w_ka], axis=1).astype(BF16), BF16)
        gates = _proj_gates(xb, jnp.concatenate([w_gr, w_ga, w_gtr], axis=1).astype(BF16), tn=D)
        w_rope = jnp.concatenate([w_qr, w_kr], axis=1)
        w_rope_rot = jnp.concatenate([_rot_half_weight(w_qr), _rot_half_weight(w_kr)], axis=1)
        qk_r = _proj_rope(xb, w_rope.astype(BF16), w_rope_rot.astype(BF16), cos, sin)
        pad = LANES - IDX_HEAD_DIM - IDX_HEADS
        w_idx = jnp.concatenate([w_ik, w_iw, jnp.zeros((D, pad), BF16)], axis=1)
        g_pad = jnp.concatenate([idx_k_ln_g[l], jnp.zeros((LANES - IDX_HEAD_DIM,), F32)]).reshape(1, LANES)
        b_pad = jnp.concatenate([idx_k_ln_b[l], jnp.zeros((LANES - IDX_HEAD_DIM,), F32)]).reshape(1, LANES)
        idx = _proj_idx(xb, w_idx, g_pad, b_pad,
                        (IDX_HEAD_DIM ** -0.5) * (IDX_HEADS ** -0.5)).reshape(B, S, LANES)
        ik = idx[:, :, :IDX_HEAD_DIM].astype(BF16)
        iwT = jnp.swapaxes(idx[:, :, IDX_HEAD_DIM:IDX_HEAD_DIM + SUBLANES], 1, 2)

        y_a = _sparse_attention(fm, tokb.reshape(B, S, -1), ik, iwT, positions, rel_bias)
        y_r = _retention(qk_r, tokb, gates, B, S)
        x1, x1b = _merge(xf, y_a.reshape(T, ATTN_W), y_r, gates,
                         w_attn_branch[l].astype(BF16), w_ret_branch[l].astype(BF16),
                         w_out[l].astype(BF16), ln_mix_g[l].reshape(1, D), ln_mix_b[l].reshape(1, D))
        xf = _ffn(x1b, x1, w_up[l].astype(BF16), w_down[l].astype(BF16),
                  ln_ffn_g[l].reshape(1, D), ln_ffn_b[l].reshape(1, D))
    return xf.reshape(B, S, D)
```

```python
import functools
import math

import numpy as np
import jax
import jax.numpy as jnp
from jax import lax
from jax.experimental import pallas as pl
from jax.experimental.pallas import tpu as pltpu

F32 = jnp.float32
BF16 = jnp.bfloat16

ATTN_HEADS = 8
ATTN_HEAD_DIM = 64
ATTN_W = ATTN_HEADS * ATTN_HEAD_DIM
IDX_HEADS = 4
IDX_HEAD_DIM = 64
IDX_Q_W = IDX_HEADS * IDX_HEAD_DIM
TOPK_MAX = 256
RET_HEADS = 8
RET_QK_DIM = 64
RET_V_DIM = 128
RET_QK_W = RET_HEADS * RET_QK_DIM
RET_V_W = RET_HEADS * RET_V_DIM
RET_CHUNK = 128
ROPE_BASE = 10000.0
NUM_BUCKETS = 32
MAX_DISTANCE = 128
LN_EPS = 1e-5
DEPTH = 1
DEEPNORM_ALPHA = (2.0 * DEPTH) ** 0.25

LANES = 128
SUBLANES = 8
VMEM_LIMIT = 56 * 1024 * 1024

TQ = 128
CK = 512
NEG = -1e30
LOG2E = math.log2(math.e)
BISECT_ROUNDS = 18
BIAS_TABLE_N = 128
FAR_N = 113


def _cparams(n_grid):
    return pltpu.CompilerParams(
        dimension_semantics=("arbitrary",) * n_grid,
        vmem_limit_bytes=VMEM_LIMIT)


def _trig_kernel(pos_ref, inv_ref, cos_ref, sin_ref):
    ang = pos_ref[...] * inv_ref[...]
    cos_ref[...] = jnp.cos(ang)
    sin_ref[...] = jnp.sin(ang)


def _rope_tables(positions):
    B, S = positions.shape
    half = RET_QK_DIM // 2
    inv = ROPE_BASE ** (-jnp.arange(half, dtype=F32) / half)
    per_row = LANES // half
    rows = B * S // per_row
    pos_e = jnp.repeat(positions.astype(F32).reshape(rows, per_row), half, axis=1)
    inv_e = jnp.tile(inv, per_row).reshape(1, LANES)
    tr = min(rows, 1024)
    cos, sin = pl.pallas_call(
        _trig_kernel,
        grid=(rows // tr,),
        in_specs=[pl.BlockSpec((tr, LANES), lambda i: (i, 0)),
                  pl.BlockSpec((1, LANES), lambda i: (0, 0))],
        out_specs=[pl.BlockSpec((tr, LANES), lambda i: (i, 0))] * 2,
        out_shape=[jax.ShapeDtypeStruct((rows, LANES), F32)] * 2,
        compiler_params=_cparams(1),
        name="rope_tables",
    )(pos_e, inv_e)
    cos = jnp.tile(cos.reshape(B * S, half), (1, per_row))
    sin = jnp.tile(sin.reshape(B * S, half), (1, per_row))
    return cos, sin


def _proj_kernel(x_ref, w_ref, o_ref):
    o_ref[...] = jnp.dot(x_ref[...], w_ref[...], preferred_element_type=F32).astype(o_ref.dtype)


def _proj(xb, w, out_dtype, tm=1024, tn=512):
    T, D = xb.shape
    N = w.shape[1]
    tn = min(N, tn)
    tm = min(tm, T)
    return pl.pallas_call(
        _proj_kernel,
        grid=(T // tm, N // tn),
        in_specs=[pl.BlockSpec((tm, D), lambda i, j: (i, 0)),
                  pl.BlockSpec((D, tn), lambda i, j: (0, j))],
        out_specs=pl.BlockSpec((tm, tn), lambda i, j: (i, j)),
        out_shape=jax.ShapeDtypeStruct((T, N), out_dtype),
        compiler_params=_cparams(2),
        name="proj",
    )(xb, w)


def _proj_gates_kernel(x_ref, w_ref, o_ref):
    acc = jnp.dot(x_ref[...], w_ref[...], preferred_element_type=F32)
    sig = 1.0 / (1.0 + jnp.exp(-acc))
    o_ref[...] = jnp.where(pl.program_id(1) == 0, acc * sig, sig).astype(o_ref.dtype)


def _proj_gates(xb, w, tn, tm=1024):
    T, D = xb.shape
    N = w.shape[1]
    tm = min(tm, T)
    return pl.pallas_call(
        _proj_gates_kernel,
        grid=(T // tm, N // tn),
        in_specs=[pl.BlockSpec((tm, D), lambda i, j: (i, 0)),
                  pl.BlockSpec((D, tn), lambda i, j: (0, j))],
        out_specs=pl.BlockSpec((tm, tn), lambda i, j: (i, j)),
        out_shape=jax.ShapeDtypeStruct((T, N), BF16),
        compiler_params=_cparams(2),
        name="proj_gates",
    )(xb, w)


def _proj_t_kernel(wT_ref, x_ref, o_ref):
    acc = lax.dot_general(wT_ref[...], x_ref[...], (((1,), (1,)), ((), ())), preferred_element_type=F32)
    o_ref[...] = acc.astype(o_ref.dtype)


def _proj_t(xb, wT, B, S, out_dtype, tm=1024, tn=640):
    T, D = xb.shape
    N = wT.shape[0]
    tn = min(N, tn)
    tm = min(tm, S)
    nsb = S // tm
    return pl.pallas_call(
        _proj_t_kernel,
        grid=(T // tm, N // tn),
        in_specs=[pl.BlockSpec((tn, D), lambda i, j: (j, 0)),
                  pl.BlockSpec((tm, D), lambda i, j: (i, 0))],
        out_specs=pl.BlockSpec((pl.Squeezed(), tn, tm), lambda i, j: (i // nsb, j, i % nsb)),
        out_shape=jax.ShapeDtypeStruct((B, N, S), out_dtype),
        compiler_params=_cparams(2),
        name="proj_t",
    )(wT, xb)


def _proj_rope_kernel(x_ref, w_ref, wr_ref, cos_ref, sin_ref, o_ref):
    x = x_ref[...]
    a = jnp.dot(x, w_ref[...], preferred_element_type=F32)
    r = jnp.dot(x, wr_ref[...], preferred_element_type=F32)
    reps = a.shape[1] // LANES
    cos = jnp.concatenate([cos_ref[...]] * reps, axis=1)
    sin = jnp.concatenate([sin_ref[...]] * reps, axis=1)
    o_ref[...] = (a * cos + r * sin).astype(o_ref.dtype)


def _proj_rope(xb, w, w_rot, cos, sin, tm=1024, tn=512):
    T, D = xb.shape
    N = w.shape[1]
    tm = min(tm, T)
    return pl.pallas_call(
        _proj_rope_kernel,
        grid=(T // tm, N // tn),
        in_specs=[pl.BlockSpec((tm, D), lambda i, j: (i, 0)),
                  pl.BlockSpec((D, tn), lambda i, j: (0, j)),
                  pl.BlockSpec((D, tn), lambda i, j: (0, j)),
                  pl.BlockSpec((tm, LANES), lambda i, j: (i, 0)),
                  pl.BlockSpec((tm, LANES), lambda i, j: (i, 0))],
        out_specs=pl.BlockSpec((tm, tn), lambda i, j: (i, j)),
        out_shape=jax.ShapeDtypeStruct((T, N), BF16),
        compiler_params=_cparams(2),
        name="proj_rope",
    )(xb, w, w_rot, cos, sin)


def _proj_idx_kernel(x_ref, w_ref, g_ref, b_ref, o_ref, *, iw_scale):
    acc = jnp.dot(x_ref[...], w_ref[...], preferred_element_type=F32)
    lane = lax.broadcasted_iota(jnp.int32, acc.shape, 1)
    is_k = lane < IDX_HEAD_DIM
    mu = jnp.sum(jnp.where(is_k, acc, 0.0), axis=-1, keepdims=True) / IDX_HEAD_DIM
    d = acc - mu
    var = jnp.sum(jnp.where(is_k, d * d, 0.0), axis=-1, keepdims=True) / IDX_HEAD_DIM
    ln = d * lax.rsqrt(var + LN_EPS) * g_ref[...] + b_ref[...]
    o_ref[...] = jnp.where(is_k, ln, acc * iw_scale)


def _proj_idx(xb, w_pad, g_pad, b_pad, iw_scale, tm=1024):
    T, D = xb.shape
    tm = min(tm, T)
    return pl.pallas_call(
        functools.partial(_proj_idx_kernel, iw_scale=iw_scale),
        grid=(T // tm,),
        in_specs=[pl.BlockSpec((tm, D), lambda i: (i, 0)),
                  pl.BlockSpec((D, LANES), lambda i: (0, 0)),
                  pl.BlockSpec((1, LANES), lambda i: (0, 0)),
                  pl.BlockSpec((1, LANES), lambda i: (0, 0))],
        out_specs=pl.BlockSpec((tm, LANES), lambda i: (i, 0)),
        out_shape=jax.ShapeDtypeStruct((T, LANES), F32),
        compiler_params=_cparams(1),
        name="proj_idx",
    )(xb, w_pad, g_pad, b_pad)


def _t5_bucket_table():
    n = np.arange(BIAS_TABLE_N)
    max_exact = NUM_BUCKETS // 2
    nf = np.maximum(n, 1).astype(np.float64)
    large = max_exact + (np.log(nf / max_exact) / math.log(MAX_DISTANCE / max_exact)
                         * (NUM_BUCKETS - max_exact)).astype(np.int64)
    large = np.minimum(large, NUM_BUCKETS - 1)
    bucket = np.where(n < max_exact, n, large)
    assert np.all(bucket[FAR_N:] == NUM_BUCKETS - 1) and bucket[FAR_N - 1] != NUM_BUCKETS - 1
    return bucket.astype(np.int32)


def _fold_rows(a, op):
    parts = [a[r:r + SUBLANES] for r in range(0, a.shape[0], SUBLANES)]
    while len(parts) > 1:
        nxt = [op(parts[k], parts[k + 1]) for k in range(0, len(parts) - 1, 2)]
        if len(parts) % 2:
            nxt.append(parts[-1])
        parts = nxt
    return parts[0]


def _attn_kernel(biasc_ref, pinfo_ref, ckmax_ref, iqT_ref, iwT_ref, posq_ref, posqc_ref, posk_ref, ik_ref, qT_ref, k_ref, vT_ref,
                 tbl_ref, toep_ref, tril_ref, o_ref, sc_ref, s_ref, s2_ref, b_ref, qm_ref, m_ref, l_ref, *acc_refs, k_sel, seq):
    i = pl.program_id(1)
    nch = (i * TQ + TQ + CK - 1) // CK
    q_idx = i * TQ + lax.broadcasted_iota(jnp.int32, (1, TQ), 1)
    kf = float(k_sel)

    def chunk_off(c):
        return pl.multiple_of(c * CK, CK)

    def key_idx(off):
        return off + lax.broadcasted_iota(jnp.int32, (CK, TQ), 0)

    def col_reduce(part, op):
        return op(part, axis=0, keepdims=True)

    iqT = iqT_ref[...]
    iwT = iwT_ref[...]

    def score_body(c, carry):
        mn, mx = carry
        off = chunk_off(c)
        ikc = ik_ref[pl.ds(off, CK), :]
        s = None
        for h in range(IDX_HEADS):
            z = jnp.dot(ikc, iqT[h * IDX_HEAD_DIM:(h + 1) * IDX_HEAD_DIM, :], preferred_element_type=F32)
            t = jnp.maximum(z, 0.0) * iwT[h:h + 1, :]
            s = t if s is None else s + t
        causal = key_idx(off) <= q_idx
        s_lo = jnp.where(causal, s, -jnp.inf)
        sc_ref[pl.ds(off, CK), :] = s_lo
        mn = jnp.minimum(mn, _fold_rows(jnp.where(causal, s, jnp.inf), jnp.minimum))
        mx = jnp.maximum(mx, _fold_rows(s_lo, jnp.maximum))
        return mn, mx

    mn8, mx8 = lax.fori_loop(0, nch, score_body,
                             (jnp.full((SUBLANES, TQ), jnp.inf, F32), jnp.full((SUBLANES, TQ), -jnp.inf, F32)))
    mn = col_reduce(mn8, jnp.min)
    mx = col_reduce(mx8, jnp.max)

    def count(pred_fn):
        def body(c, acc):
            off = chunk_off(c)
            blk = sc_ref[pl.ds(off, CK), :]
            return acc + _fold_rows(jnp.where(pred_fn(blk, off), 1.0, 0.0), jnp.add)
        acc = lax.fori_loop(0, nch, body, jnp.zeros((SUBLANES, TQ), F32))
        return col_reduce(acc, jnp.sum)

    def bisect_round(_, st):
        lo, hi, c_lo = st
        mid = 0.5 * (lo + hi)
        c = count(lambda blk, off: blk >= mid)
        ok = c >= kf
        return jnp.where(ok, mid, lo), jnp.where(ok, hi, mid), jnp.where(ok, c, c_lo)

    c_all = (q_idx + 1).astype(F32)
    lo, hi, c_lo = lax.fori_loop(0, BISECT_ROUNDS, bisect_round, (mn, mx, c_all))

    def min_ge_body(c, acc):
        blk = sc_ref[pl.ds(chunk_off(c), CK), :]
        return jnp.minimum(acc, _fold_rows(jnp.where(blk >= lo, blk, jnp.inf), jnp.minimum))

    cur0 = col_reduce(lax.fori_loop(0, nch, min_ge_body, jnp.full((SUBLANES, TQ), jnp.inf, F32)), jnp.min)

    def walk_cond(st):
        return st[3] > 0.0

    def walk_body(st):
        cur, c_ge, _, _ = st

        def body(c, carry):
            cnt, nxt = carry
            blk = sc_ref[pl.ds(chunk_off(c), CK), :]
            gt = blk > cur
            cnt = cnt + _fold_rows(jnp.where(gt, 1.0, 0.0), jnp.add)
            nxt = jnp.minimum(nxt, _fold_rows(jnp.where(gt, blk, jnp.inf), jnp.minimum))
            return cnt, nxt

        cnt, nxt = lax.fori_loop(0, nch, body,
                                 (jnp.zeros((SUBLANES, TQ), F32), jnp.full((SUBLANES, TQ), jnp.inf, F32)))
        c_gt = col_reduce(cnt, jnp.sum)
        nxt = col_reduce(nxt, jnp.min)
        adv = c_gt >= kf
        cur = jnp.where(adv, nxt, cur)
        c_ge = jnp.where(adv, c_gt, c_ge)
        return cur, c_ge, c_gt, jnp.max(jnp.where(adv, 1.0, 0.0))

    tau, c_ge, c_gt, _ = lax.while_loop(
        walk_cond, walk_body, (cur0, c_lo, jnp.zeros((1, TQ), F32), jnp.float32(1.0)))

    trim = c_ge > kf

    def mask_with_ties():
        room = kf - c_gt

        def mask_body(c, seen):
            off = chunk_off(c)
            for j in range(CK // LANES):
                rows = pl.ds(off + j * LANES, LANES)
                blk = sc_ref[rows, :]
                tie = blk == tau
                rank = seen + jnp.dot(tril_ref[...], jnp.where(tie, 1.0, 0.0).astype(BF16),
                                      preferred_element_type=F32)
                keep = (blk > tau) | (tie & (rank <= room))
                sc_ref[rows, :] = jnp.where(keep, 0.0, NEG)
                seen = rank[LANES - 1:LANES, :]
            return seen

        lax.fori_loop(0, nch, mask_body, jnp.zeros((1, TQ), F32))

    def mask_plain():
        def mask_body(c, _):
            off = chunk_off(c)
            sc_ref[pl.ds(off, CK), :] = jnp.where(sc_ref[pl.ds(off, CK), :] >= tau, 0.0, NEG)
            return 0

        lax.fori_loop(0, nch, mask_body, 0)

    lax.cond(jnp.max(jnp.where(trim, 1.0, 0.0)) > 0.0, mask_with_ties, mask_plain)

    m_ref[...] = jnp.full(m_ref.shape, NEG, F32)
    l_ref[...] = jnp.zeros(l_ref.shape, F32)
    for acc in acc_refs:
        acc[...] = jnp.zeros(acc.shape, F32)

    rowi = lax.broadcasted_iota(jnp.int32, (LANES, TQ), 0)
    for pair in range(ATTN_HEADS // 2):
        qp = qT_ref[pair * LANES:(pair + 1) * LANES, :]
        zero = jnp.zeros_like(qp)
        qm_ref[pair] = jnp.concatenate([jnp.where(rowi < ATTN_HEAD_DIM, qp, zero),
                                        jnp.where(rowi >= ATTN_HEAD_DIM, qp, zero)], axis=1)
    ones_rows = jnp.ones((2 * SUBLANES, CK), BF16)

    pq_col = posqc_ref[...]
    batch = pl.program_id(0)
    pq_first, pq_consec, pq_min = (pinfo_ref[batch, r, i] for r in (0, 1, 3))
    far_bias = [biasc_ref[0, h] for h in range(ATTN_HEADS)]
    zero_bias = [biasc_ref[1, h] for h in range(ATTN_HEADS)]
    q_ge_k = (lax.broadcasted_iota(jnp.int32, (LANES, TQ), 1) >= lax.broadcasted_iota(jnp.int32, (LANES, TQ), 0))
    n_sub = CK // LANES
    n_pairs = ATTN_HEADS // 2

    def chunk_is_far(c):
        return (pq_min - ckmax_ref[batch, jnp.minimum(c, seq // CK - 1)]) >= FAR_N

    def stage_bias(c):
        off = chunk_off(c)
        pk_row = posk_ref[:, pl.ds(off, CK)]
        for j in range(n_sub):
            rows = slice(j * LANES, (j + 1) * LANES)
            g = c * n_sub + j
            pk_first, pk_consec, pk_max = (pinfo_ref[batch, r, g] for r in (0, 1, 2))
            all_far = (pq_min - pk_max) >= FAR_N
            all_masked = (off + j * LANES) > (i * TQ + TQ - 1)
            consecutive = (pq_consec > 0) & (pk_consec > 0)
            gap = pq_first - pk_first

            def fill_const(rows=rows):
                for h in range(ATTN_HEADS):
                    b_ref[h, rows, :] = jnp.full((LANES, TQ), far_bias[h], F32)

            def fill_gap0(rows=rows):
                for h in range(ATTN_HEADS):
                    b_ref[h, rows, :] = jnp.where(q_ge_k, toep_ref[h], zero_bias[h])

            def fill_gap128(rows=rows):
                for h in range(ATTN_HEADS):
                    b_ref[h, rows, :] = jnp.where(q_ge_k, far_bias[h], toep_ref[h])

            def fill_lookup(rows=rows):
                pk_sub = pk_row[:, rows]
                n_qk = jnp.clip(pq_col - pk_sub, 0, BIAS_TABLE_N - 1).astype(F32)
                n_kq = n_qk.T.astype(jnp.int32)
                for h in range(ATTN_HEADS):
                    tb = jnp.broadcast_to(tbl_ref[h:h + 1, :], (LANES, BIAS_TABLE_N))
                    b_ref[h, rows, :] = jnp.take_along_axis(tb, n_kq, axis=1)

            def fill_near(fill_gap0=fill_gap0, fill_gap128=fill_gap128, fill_lookup=fill_lookup,
                          consecutive=consecutive, gap=gap):
                lax.cond(consecutive & (gap == 0), fill_gap0,
                         lambda: lax.cond(consecutive & (gap == LANES), fill_gap128, fill_lookup))

            lax.cond(all_far | all_masked, fill_const, fill_near)

    def logits_phase(c, s_buf, const_bias):
        off = chunk_off(c)
        mb = sc_ref[pl.ds(off, CK), :]
        m_cur = []
        for pair in range(n_pairs):
            kc = k_ref[pl.ds(off, CK), pair * LANES:(pair + 1) * LANES]
            s2 = jnp.dot(kc, qm_ref[pair], preferred_element_type=F32)
            for sub, h in enumerate((2 * pair, 2 * pair + 1)):
                s = s2[:, sub * TQ:(sub + 1) * TQ] + mb
                if not const_bias:
                    s = s + b_ref[h]
                s_buf[h] = s
                top = col_reduce(_fold_rows(s, jnp.maximum), jnp.max)
                m_cur.append(top + far_bias[h] if const_bias else top)
        return m_cur

    def update_phase(c, s_buf, m_cur, const_bias):
        off = chunk_off(c)
        m_prev = m_ref[...]
        l_prev = l_ref[...]
        if isinstance(m_cur, list):
            m_new = [jnp.maximum(m_prev[h:h + 1, :], m_cur[h]) for h in range(ATTN_HEADS)]
            alpha = [jnp.exp2(m_prev[h:h + 1, :] - m_new[h]) for h in range(ATTN_HEADS)]
        else:
            m_all = jnp.maximum(m_prev, m_cur)
            a_all = jnp.exp2(m_prev - m_all)
            m_new = [m_all[h:h + 1, :] for h in range(ATTN_HEADS)]
            alpha = [a_all[h:h + 1, :] for h in range(ATTN_HEADS)]
        l_new = []
        for pair in range(n_pairs):
            heads = (2 * pair, 2 * pair + 1)
            sub_m = [m_new[h] - far_bias[h] if const_bias else m_new[h] for h in heads]
            p2 = jnp.concatenate([jnp.exp2(s_buf[h] - sm).astype(BF16) for h, sm in zip(heads, sub_m)],
                                 axis=1)
            lhs = jnp.concatenate([vT_ref[pair * LANES:(pair + 1) * LANES, pl.ds(off, CK)], ones_rows],
                                  axis=0)
            out = jnp.dot(lhs, p2, preferred_element_type=F32)
            for sub, h in enumerate(heads):
                cols = slice(sub * TQ, (sub + 1) * TQ)
                acc = acc_refs[h]
                acc[...] = alpha[h] * acc[...] + out[sub * ATTN_HEAD_DIM:(sub + 1) * ATTN_HEAD_DIM, cols]
                l_new.append(alpha[h] * l_prev[h:h + 1, :] + out[LANES:LANES + 1, cols])
        l_ref[...] = jnp.concatenate(l_new, axis=0)
        m_ref[...] = jnp.concatenate(m_new, axis=0)

    n_far = lax.while_loop(lambda c: (c < nch) & chunk_is_far(c), lambda c: c + 1, jnp.int32(0))
    n_steps = n_far // 2
    def far_logits(c, s_buf):
        return jnp.concatenate(logits_phase(c, s_buf, True), axis=0)

    m_first = lax.cond(n_steps > 0, lambda: far_logits(0, s_ref), lambda: jnp.zeros((ATTN_HEADS, TQ), F32))

    def far_step(t, m_even):
        c = 2 * t
        m_odd = far_logits(c + 1, s2_ref)
        update_phase(c, s_ref, m_even, True)
        m_even = far_logits(jnp.minimum(c + 2, 2 * n_steps - 2), s_ref)
        update_phase(c + 1, s2_ref, m_odd, True)
        return m_even

    lax.fori_loop(0, n_steps, far_step, m_first)

    def tail_chunk(c, _):
        def run(const_bias):
            if not const_bias:
                stage_bias(c)
            update_phase(c, s_ref, logits_phase(c, s_ref, const_bias), const_bias)

        lax.cond(chunk_is_far(c), lambda: run(True), lambda: run(False))
        return 0

    lax.fori_loop(2 * n_steps, nch, tail_chunk, 0)

    outT = jnp.concatenate([acc_refs[h][...] / l_ref[h:h + 1, :] for h in range(ATTN_HEADS)], axis=0)
    o_ref[...] = outT.T.astype(o_ref.dtype)


def _sparse_attention(fm, tokb, ik, iwT, positions, rel_bias):
    B, S, _ = tokb.shape
    k_sel = min(TOPK_MAX, S // 4)
    bucket = _t5_bucket_table()
    tbl = rel_bias[bucket].T.astype(F32) * LOG2E
    bias_c = jnp.stack([tbl[:, BIAS_TABLE_N - 1], tbl[:, 0]])
    lane = np.arange(LANES)
    toep = tbl[:, (lane[None, :] - lane[:, None]) % BIAS_TABLE_N]
    pos_row = positions.reshape(B, 1, S)
    pos_col = positions.reshape(B, S, 1)
    pos_t = positions.reshape(B, S // LANES, LANES)
    consec = jnp.all(pos_t - pos_t[:, :, :1] == jnp.arange(LANES, dtype=positions.dtype), axis=-1)
    pinfo = jnp.stack([pos_t[:, :, 0], consec.astype(jnp.int32), jnp.max(pos_t, axis=-1),
                       jnp.min(pos_t, axis=-1)], axis=1).astype(jnp.int32)
    ck_max = jnp.max(positions.reshape(B, S // CK, CK), axis=-1)
    tril = jnp.tril(jnp.ones((LANES, LANES), BF16))
    sq = pl.Squeezed()
    iq_blk = (2 * ATTN_W) // IDX_Q_W
    k_blk = (tokb.shape[2] - ATTN_W) // ATTN_W
    return pl.pallas_call(
        functools.partial(_attn_kernel, k_sel=k_sel, seq=S),
        grid=(B, S // TQ),
        in_specs=[pl.BlockSpec(memory_space=pltpu.SMEM),
                  pl.BlockSpec(memory_space=pltpu.SMEM),
                  pl.BlockSpec(memory_space=pltpu.SMEM),
                  pl.BlockSpec((sq, IDX_Q_W, TQ), lambda b, i: (b, iq_blk, i)),
                  pl.BlockSpec((sq, SUBLANES, TQ), lambda b, i: (b, 0, i)),
                  pl.BlockSpec((sq, 1, TQ), lambda b, i: (b, 0, i)),
                  pl.BlockSpec((sq, TQ, 1), lambda b, i: (b, i, 0)),
                  pl.BlockSpec((sq, 1, S), lambda b, i: (b, 0, 0)),
                  pl.BlockSpec((sq, S, IDX_HEAD_DIM), lambda b, i: (b, 0, 0)),
                  pl.BlockSpec((sq, ATTN_W, TQ), lambda b, i: (b, 0, i)),
                  pl.BlockSpec((sq, S, ATTN_W), lambda b, i: (b, 0, k_blk)),
                  pl.BlockSpec((sq, ATTN_W, S), lambda b, i: (b, 1, 0)),
                  pl.BlockSpec((ATTN_HEADS, BIAS_TABLE_N), lambda b, i: (0, 0)),
                  pl.BlockSpec((ATTN_HEADS, LANES, LANES), lambda b, i: (0, 0, 0)),
                  pl.BlockSpec((LANES, LANES), lambda b, i: (0, 0))],
        out_specs=pl.BlockSpec((sq, TQ, ATTN_W), lambda b, i: (b, i, 0)),
        out_shape=jax.ShapeDtypeStruct((B, S, ATTN_W), BF16),
        scratch_shapes=[pltpu.VMEM((S, TQ), F32),
                        pltpu.VMEM((ATTN_HEADS, CK, TQ), F32),
                        pltpu.VMEM((ATTN_HEADS, CK, TQ), F32),
                        pltpu.VMEM((ATTN_HEADS, CK, TQ), F32),
                        pltpu.VMEM((ATTN_HEADS // 2, LANES, 2 * TQ), BF16),
                        pltpu.VMEM((ATTN_HEADS, TQ), F32),
                        pltpu.VMEM((ATTN_HEADS, TQ), F32)]
                       + [pltpu.VMEM((ATTN_HEAD_DIM, TQ), F32)] * ATTN_HEADS,
        compiler_params=_cparams(2),
        name="sparse_attention",
    )(bias_c, pinfo, ck_max, fm, iwT, pos_row, pos_col, pos_row, ik, fm, tokb, fm, tbl, toep, tril)


def _retention_kernel(q_ref, k_ref, v_ref, g_ref, decay_ref, xi_ref, zeta_ref, gch_ref, o_ref, r_ref):
    @pl.when(pl.program_id(1) == 0)
    def _():
        r_ref[...] = jnp.zeros(r_ref.shape, F32)

    C = RET_CHUNK
    lane = lax.broadcasted_iota(jnp.int32, (C, LANES), 1)
    row = lax.broadcasted_iota(jnp.int32, (LANES, RET_V_DIM), 0)
    for pair in range(RET_HEADS // 2):
        q_pair = q_ref[:, pair * LANES:(pair + 1) * LANES]
        k_pair = k_ref[:, pair * LANES:(pair + 1) * LANES]
        v_pair = v_ref[:, 2 * pair * RET_V_DIM:(2 * pair + 2) * RET_V_DIM]
        r_pair = r_ref[pair]
        r_bf = r_pair.astype(BF16)
        for sub in range(2):
            h = 2 * pair + sub
            in_head = (lane >= sub * RET_QK_DIM) & (lane < (sub + 1) * RET_QK_DIM)
            qm = jnp.where(in_head, q_pair, jnp.zeros_like(q_pair))
            v_h = v_pair[:, sub * RET_V_DIM:(sub + 1) * RET_V_DIM]
            inner = lax.dot_general(qm, k_pair, (((1,), (1,)), ((), ())),
                                    preferred_element_type=F32) * decay_ref[h]
            o = (jnp.dot(inner.astype(BF16), v_h, preferred_element_type=F32)
                 + jnp.dot(qm, r_bf, preferred_element_type=F32) * xi_ref[h])
            mu = jnp.mean(o, axis=-1, keepdims=True)
            d = o - mu
            var = jnp.mean(d * d, axis=-1, keepdims=True)
            hn = d * lax.rsqrt(var + LN_EPS)
            gate = g_ref[:, h * RET_V_DIM:(h + 1) * RET_V_DIM].astype(F32)
            o_ref[:, h * RET_V_DIM:(h + 1) * RET_V_DIM] = (gate * hn).astype(o_ref.dtype)
        kz = (k_pair.astype(F32) * zeta_ref[pair]).astype(BF16)
        upd = lax.dot_general(kz, v_pair, (((0,), (0,)), ((), ())), preferred_element_type=F32)
        r_ref[pair] = r_pair * gch_ref[pair] + jnp.where(row < RET_QK_DIM, upd[:, :RET_V_DIM], upd[:, RET_V_DIM:])


def _retention(qk, tokb, gates, B, S):
    C = RET_CHUNK
    H = RET_HEADS
    nc = S // C
    gamma = 1.0 - 2.0 ** (-5.0 - jnp.arange(H, dtype=F32))
    log_g = jnp.log(gamma)
    n = jnp.arange(C, dtype=F32)
    diff = n[:, None] - n[None, :]
    decay_in = jnp.where(diff[None] >= 0, jnp.exp(log_g[:, None, None] * jnp.maximum(diff, 0.0)[None]), 0.0)
    xi = jnp.exp(log_g[None, :] * (n[:, None] + 1.0))
    zeta = jnp.exp(log_g[None, :] * (C - 1.0 - n[:, None]))
    g_chunk = jnp.exp(log_g * C)
    xi_b = jnp.broadcast_to(xi.T[:, :, None], (H, C, RET_V_DIM))
    zeta_b = jnp.repeat(zeta, RET_QK_DIM, axis=1).reshape(C, H // 2, LANES).transpose(1, 0, 2)
    gch_b = jnp.broadcast_to(jnp.repeat(g_chunk, RET_QK_DIM).reshape(H // 2, LANES, 1),
                             (H // 2, LANES, RET_V_DIM))
    return pl.pallas_call(
        _retention_kernel,
        grid=(B, nc),
        in_specs=[pl.BlockSpec((C, RET_QK_W), lambda b, i: (b * nc + i, 0)),
                  pl.BlockSpec((C, RET_QK_W), lambda b, i: (b * nc + i, 1)),
                  pl.BlockSpec((C, RET_V_W), lambda b, i: (b * nc + i, 0)),
                  pl.BlockSpec((C, RET_V_W), lambda b, i: (b * nc + i, 0)),
                  pl.BlockSpec((H, C, C), lambda b, i: (0, 0, 0)),
                  pl.BlockSpec((H, C, RET_V_DIM), lambda b, i: (0, 0, 0)),
                  pl.BlockSpec((H // 2, C, LANES), lambda b, i: (0, 0, 0)),
                  pl.BlockSpec((H // 2, LANES, RET_V_DIM), lambda b, i: (0, 0, 0))],
        out_specs=pl.BlockSpec((C, RET_V_W), lambda b, i: (b * nc + i, 0)),
        out_shape=jax.ShapeDtypeStruct((B * S, RET_V_W), BF16),
        scratch_shapes=[pltpu.VMEM((H // 2, LANES, RET_V_DIM), F32)],
        compiler_params=_cparams(2),
        name="retention",
    )(qk, qk, tokb, gates, decay_in, xi_b, zeta_b, gch_b)


def _layer_norm(z, g, b):
    mu = jnp.mean(z, axis=-1, keepdims=True)
    d = z - mu
    var = jnp.mean(d * d, axis=-1, keepdims=True)
    return d * lax.rsqrt(var + LN_EPS) * g + b


def _merge_kernel(x_ref, ya_ref, yr_ref, ga_ref, gr_ref, wa_ref, wr_ref, wo_ref, g_ref, b_ref,
                  x1_ref, x1b_ref):
    a = jnp.dot(ya_ref[...], wa_ref[...], preferred_element_type=F32)
    r = jnp.dot(yr_ref[...], wr_ref[...], preferred_element_type=F32)
    h = ga_ref[...].astype(F32) * a + gr_ref[...].astype(F32) * r
    mix = jnp.dot(h.astype(BF16), wo_ref[...], preferred_element_type=F32)
    x1 = _layer_norm(DEEPNORM_ALPHA * x_ref[...] + mix, g_ref[...], b_ref[...])
    x1_ref[...] = x1
    x1b_ref[...] = x1.astype(BF16)


def _merge(x, ya, yr, gates, wa, wr, wo, g, b, tm=512):
    T, D = x.shape
    tm = min(tm, T)
    row = lambda i: (i, 0)
    fixed = lambda i: (0, 0)
    return pl.pallas_call(
        _merge_kernel,
        grid=(T // tm,),
        in_specs=[pl.BlockSpec((tm, D), row),
                  pl.BlockSpec((tm, ya.shape[1]), row),
                  pl.BlockSpec((tm, yr.shape[1]), row),
                  pl.BlockSpec((tm, D), lambda i: (i, 1)),
                  pl.BlockSpec((tm, D), lambda i: (i, 2)),
                  pl.BlockSpec(wa.shape, fixed),
                  pl.BlockSpec(wr.shape, fixed),
                  pl.BlockSpec(wo.shape, fixed),
                  pl.BlockSpec((1, D), fixed),
                  pl.BlockSpec((1, D), fixed)],
        out_specs=[pl.BlockSpec((tm, D), row), pl.BlockSpec((tm, D), row)],
        out_shape=[jax.ShapeDtypeStruct((T, D), F32), jax.ShapeDtypeStruct((T, D), BF16)],
        compiler_params=_cparams(1),
        name="merge",
    )(x, ya, yr, gates, gates, wa, wr, wo, g, b)


def _ffn_kernel(x1b_ref, x1_ref, wu_ref, wd_ref, g_ref, b_ref, o_ref, acc_ref):
    f = pl.program_id(1)

    @pl.when(f == 0)
    def _():
        acc_ref[...] = jnp.zeros(acc_ref.shape, F32)

    hid = jnp.maximum(jnp.dot(x1b_ref[...], wu_ref[...], preferred_element_type=F32), 0.0)
    acc_ref[...] += jnp.dot((hid * hid).astype(BF16), wd_ref[...], preferred_element_type=F32)

    @pl.when(f == pl.num_programs(1) - 1)
    def _():
        o_ref[...] = _layer_norm(DEEPNORM_ALPHA * x1_ref[...] + acc_ref[...], g_ref[...], b_ref[...])


def _ffn(x1b, x1, wu, wd, g, b, tm=1024, tf=1024):
    T, D = x1.shape
    F = wu.shape[1]
    tm = min(tm, T)
    return pl.pallas_call(
        _ffn_kernel,
        grid=(T // tm, F // tf),
        in_specs=[pl.BlockSpec((tm, D), lambda i, f: (i, 0)),
                  pl.BlockSpec((tm, D), lambda i, f: (i, 0)),
                  pl.BlockSpec((D, tf), lambda i, f: (0, f)),
                  pl.BlockSpec((tf, D), lambda i, f: (f, 0)),
                  pl.BlockSpec((1, D), lambda i, f: (0, 0)),
                  pl.BlockSpec((1, D), lambda i, f: (0, 0))],
        out_specs=pl.BlockSpec((tm, D), lambda i, f: (i, 0)),
        out_shape=jax.ShapeDtypeStruct((T, D), F32),
        scratch_shapes=[pltpu.VMEM((tm, D), F32)],
        compiler_params=_cparams(2),
        name="ffn",
    )(x1b, x1, wu, wd, g, b)


def _rot_half_weight(w):
    D, N = w.shape
    half = RET_QK_DIM // 2
    wh = w.reshape(D, N // RET_QK_DIM, 2, half)
    return jnp.stack([-wh[:, :, 1], wh[:, :, 0]], axis=2).reshape(D, N)


def kernel(x, positions, w_in, rel_bias, idx_k_ln_g, idx_k_ln_b, w_attn_branch, w_ret_branch,
           w_out, ln_mix_g, ln_mix_b, w_up, w_down, ln_ffn_g, ln_ffn_b):
    B, S, D = x.shape
    T = B * S
    sizes = (ATTN_W, ATTN_W, ATTN_W, IDX_Q_W, IDX_HEAD_DIM, IDX_HEADS,
             RET_QK_W, RET_QK_W, RET_V_W, RET_V_W, D, D)
    offs = [0] + [int(o) for o in np.cumsum(sizes)]
    cos, sin = _rope_tables(positions)
    xf = x.reshape(T, D)
    for l in range(DEPTH):
        wl = w_in[l].astype(BF16)
        cols = [wl[:, offs[k]:offs[k + 1]] for k in range(len(sizes))]
        (w_qa, w_ka, w_va, w_iq, w_ik, w_iw, w_qr, w_kr, w_vr, w_gr, w_ga, w_gtr) = cols
        xb = xf.astype(BF16)
        w_qa = (w_in[l][:, offs[0]:offs[1]] * (ATTN_HEAD_DIM ** -0.5 * LOG2E)).astype(BF16)
        w_kr = w_kr * (RET_QK_DIM ** -0.5)

        fm = _proj_t(xb, jnp.concatenate([w_qa, w_va, w_iq], axis=1).T.astype(BF16), B, S, BF16)
        tokb = _proj(xb, jnp.concatenate([w_vr, w_ka], axis=1).astype(BF16), BF16)
        gates = _proj_gates(xb, jnp.concatenate([w_gr, w_ga, w_gtr], axis=1).astype(BF16), tn=D)
        w_rope = jnp.concatenate([w_qr, w_kr], axis=1)
        w_rope_rot = jnp.concatenate([_rot_half_weight(w_qr), _rot_half_weight(w_kr)], axis=1)
        qk_r = _proj_rope(xb, w_rope.astype(BF16), w_rope_rot.astype(BF16), cos, sin)
        pad = LANES - IDX_HEAD_DIM - IDX_HEADS
        w_idx = jnp.concatenate([w_ik, w_iw, jnp.zeros((D, pad), BF16)], axis=1)
        g_pad = jnp.concatenate([idx_k_ln_g[l], jnp.zeros((LANES - IDX_HEAD_DIM,), F32)]).reshape(1, LANES)
        b_pad = jnp.concatenate([idx_k_ln_b[l], jnp.zeros((LANES - IDX_HEAD_DIM,), F32)]).reshape(1, LANES)
        idx = _proj_idx(xb, w_idx, g_pad, b_pad,
                        (IDX_HEAD_DIM ** -0.5) * (IDX_HEADS ** -0.5)).reshape(B, S, LANES)
        ik = idx[:, :, :IDX_HEAD_DIM].astype(BF16)
        iwT = jnp.swapaxes(idx[:, :, IDX_HEAD_DIM:IDX_HEAD_DIM + SUBLANES], 1, 2)

        y_a = _sparse_attention(fm, tokb.reshape(B, S, -1), ik, iwT, positions, rel_bias)
        y_r = _retention(qk_r, tokb, gates, B, S)
        x1, x1b = _merge(xf, y_a.reshape(T, ATTN_W), y_r, gates,
                         w_attn_branch[l].astype(BF16), w_ret_branch[l].astype(BF16),
                         w_out[l].astype(BF16), ln_mix_g[l].reshape(1, D), ln_mix_b[l].reshape(1, D))
        xf = _ffn(x1b, x1, w_up[l].astype(BF16), w_down[l].astype(BF16),
                  ln_ffn_g[l].reshape(1, D), ln_ffn_b[l].reshape(1, D))
    return xf.reshape(B, S, D)
```

```python
import functools
import math

import numpy as np
import jax
import jax.numpy as jnp
from jax import lax
from jax.experimental import pallas as pl
from jax.experimental.pallas import tpu as pltpu

F32 = jnp.float32
BF16 = jnp.bfloat16

ATTN_HEADS = 8
ATTN_HEAD_DIM = 64
ATTN_W = ATTN_HEADS * ATTN_HEAD_DIM
IDX_HEADS = 4
IDX_HEAD_DIM = 64
IDX_Q_W = IDX_HEADS * IDX_HEAD_DIM
TOPK_MAX = 256
RET_HEADS = 8
RET_QK_DIM = 64
RET_V_DIM = 128
RET_QK_W = RET_HEADS * RET_QK_DIM
RET_V_W = RET_HEADS * RET_V_DIM
RET_CHUNK = 128
ROPE_BASE = 10000.0
NUM_BUCKETS = 32
MAX_DISTANCE = 128
LN_EPS = 1e-5
DEPTH = 1
DEEPNORM_ALPHA = (2.0 * DEPTH) ** 0.25

LANES = 128
SUBLANES = 8
VMEM_LIMIT = 56 * 1024 * 1024

TQ = 128
CK = 512
NEG = -1e30
LOG2E = math.log2(math.e)
BISECT_ROUNDS = 18
BIAS_TABLE_N = 128
FAR_N = 113


def _cparams(n_grid):
    return pltpu.CompilerParams(
        dimension_semantics=("arbitrary",) * n_grid,
        vmem_limit_bytes=VMEM_LIMIT)


def _trig_kernel(pos_ref, inv_ref, cos_ref, sin_ref):
    ang = pos_ref[...] * inv_ref[...]
    cos_ref[...] = jnp.cos(ang)
    sin_ref[...] = jnp.sin(ang)


def _rope_tables(positions):
    B, S = positions.shape
    half = RET_QK_DIM // 2
    inv = ROPE_BASE ** (-jnp.arange(half, dtype=F32) / half)
    per_row = LANES // half
    rows = B * S // per_row
    pos_e = jnp.repeat(positions.astype(F32).reshape(rows, per_row), half, axis=1)
    inv_e = jnp.tile(inv, per_row).reshape(1, LANES)
    tr = min(rows, 1024)
    cos, sin = pl.pallas_call(
        _trig_kernel,
        grid=(rows // tr,),
        in_specs=[pl.BlockSpec((tr, LANES), lambda i: (i, 0)),
                  pl.BlockSpec((1, LANES), lambda i: (0, 0))],
        out_specs=[pl.BlockSpec((tr, LANES), lambda i: (i, 0))] * 2,
        out_shape=[jax.ShapeDtypeStruct((rows, LANES), F32)] * 2,
        compiler_params=_cparams(1),
        name="rope_tables",
    )(pos_e, inv_e)
    cos = jnp.tile(cos.reshape(B * S, half), (1, per_row))
    sin = jnp.tile(sin.reshape(B * S, half), (1, per_row))
    return cos, sin


def _proj_kernel(x_ref, w_ref, o_ref):
    o_ref[...] = jnp.dot(x_ref[...], w_ref[...], preferred_element_type=F32).astype(o_ref.dtype)


def _proj(xb, w, out_dtype, tm=1024, tn=512):
    T, D = xb.shape
    N = w.shape[1]
    tn = min(N, tn)
    tm = min(tm, T)
    return pl.pallas_call(
        _proj_kernel,
        grid=(T // tm, N // tn),
        in_specs=[pl.BlockSpec((tm, D), lambda i, j: (i, 0)),
                  pl.BlockSpec((D, tn), lambda i, j: (0, j))],
        out_specs=pl.BlockSpec((tm, tn), lambda i, j: (i, j)),
        out_shape=jax.ShapeDtypeStruct((T, N), out_dtype),
        compiler_params=_cparams(2),
        name="proj",
    )(xb, w)


def _proj_gates_kernel(x_ref, w_ref, o_ref):
    acc = jnp.dot(x_ref[...], w_ref[...], preferred_element_type=F32)
    sig = 1.0 / (1.0 + jnp.exp(-acc))
    o_ref[...] = jnp.where(pl.program_id(1) == 0, acc * sig, sig).astype(o_ref.dtype)


def _proj_gates(xb, w, tn, tm=1024):
    T, D = xb.shape
    N = w.shape[1]
    tm = min(tm, T)
    return pl.pallas_call(
        _proj_gates_kernel,
        grid=(T // tm, N // tn),
        in_specs=[pl.BlockSpec((tm, D), lambda i, j: (i, 0)),
                  pl.BlockSpec((D, tn), lambda i, j: (0, j))],
        out_specs=pl.BlockSpec((tm, tn), lambda i, j: (i, j)),
        out_shape=jax.ShapeDtypeStruct((T, N), BF16),
        compiler_params=_cparams(2),
        name="proj_gates",
    )(xb, w)


def _proj_t_kernel(wT_ref, x_ref, o_ref):
    acc = lax.dot_general(wT_ref[...], x_ref[...], (((1,), (1,)), ((), ())), preferred_element_type=F32)
    o_ref[...] = acc.astype(o_ref.dtype)


def _proj_t(xb, wT, B, S, out_dtype, tm=1024, tn=640):
    T, D = xb.shape
    N = wT.shape[0]
    tn = min(N, tn)
    tm = min(tm, S)
    nsb = S // tm
    return pl.pallas_call(
        _proj_t_kernel,
        grid=(T // tm, N // tn),
        in_specs=[pl.BlockSpec((tn, D), lambda i, j: (j, 0)),
                  pl.BlockSpec((tm, D), lambda i, j: (i, 0))],
        out_specs=pl.BlockSpec((pl.Squeezed(), tn, tm), lambda i, j: (i // nsb, j, i % nsb)),
        out_shape=jax.ShapeDtypeStruct((B, N, S), out_dtype),
        compiler_params=_cparams(2),
        name="proj_t",
    )(wT, xb)


def _proj_rope_kernel(x_ref, w_ref, wr_ref, cos_ref, sin_ref, o_ref):
    x = x_ref[...]
    a = jnp.dot(x, w_ref[...], preferred_element_type=F32)
    r = jnp.dot(x, wr_ref[...], preferred_element_type=F32)
    reps = a.shape[1] // LANES
    cos = jnp.concatenate([cos_ref[...]] * reps, axis=1)
    sin = jnp.concatenate([sin_ref[...]] * reps, axis=1)
    o_ref[...] = (a * cos + r * sin).astype(o_ref.dtype)


def _proj_rope(xb, w, w_rot, cos, sin, tm=1024, tn=512):
    T, D = xb.shape
    N = w.shape[1]
    tm = min(tm, T)
    return pl.pallas_call(
        _proj_rope_kernel,
        grid=(T // tm, N // tn),
        in_specs=[pl.BlockSpec((tm, D), lambda i, j: (i, 0)),
                  pl.BlockSpec((D, tn), lambda i, j: (0, j)),
                  pl.BlockSpec((D, tn), lambda i, j: (0, j)),
                  pl.BlockSpec((tm, LANES), lambda i, j: (i, 0)),
                  pl.BlockSpec((tm, LANES), lambda i, j: (i, 0))],
        out_specs=pl.BlockSpec((tm, tn), lambda i, j: (i, j)),
        out_shape=jax.ShapeDtypeStruct((T, N), BF16),
        compiler_params=_cparams(2),
        name="proj_rope",
    )(xb, w, w_rot, cos, sin)


def _proj_idx_kernel(x_ref, w_ref, g_ref, b_ref, o_ref, *, iw_scale):
    acc = jnp.dot(x_ref[...], w_ref[...], preferred_element_type=F32)
    lane = lax.broadcasted_iota(jnp.int32, acc.shape, 1)
    is_k = lane < IDX_HEAD_DIM
    mu = jnp.sum(jnp.where(is_k, acc, 0.0), axis=-1, keepdims=True) / IDX_HEAD_DIM
    d = acc - mu
    var = jnp.sum(jnp.where(is_k, d * d, 0.0), axis=-1, keepdims=True) / IDX_HEAD_DIM
    ln = d * lax.rsqrt(var + LN_EPS) * g_ref[...] + b_ref[...]
    o_ref[...] = jnp.where(is_k, ln, acc * iw_scale)


def _proj_idx(xb, w_pad, g_pad, b_pad, iw_scale, tm=1024):
    T, D = xb.shape
    tm = min(tm, T)
    return pl.pallas_call(
        functools.partial(_proj_idx_kernel, iw_scale=iw_scale),
        grid=(T // tm,),
        in_specs=[pl.BlockSpec((tm, D), lambda i: (i, 0)),
                  pl.BlockSpec((D, LANES), lambda i: (0, 0)),
                  pl.BlockSpec((1, LANES), lambda i: (0, 0)),
                  pl.BlockSpec((1, LANES), lambda i: (0, 0))],
        out_specs=pl.BlockSpec((tm, LANES), lambda i: (i, 0)),
        out_shape=jax.ShapeDtypeStruct((T, LANES), F32),
        compiler_params=_cparams(1),
        name="proj_idx",
    )(xb, w_pad, g_pad, b_pad)


def _t5_bucket_table():
    n = np.arange(BIAS_TABLE_N)
    max_exact = NUM_BUCKETS // 2
    nf = np.maximum(n, 1).astype(np.float64)
    large = max_exact + (np.log(nf / max_exact) / math.log(MAX_DISTANCE / max_exact)
                         * (NUM_BUCKETS - max_exact)).astype(np.int64)
    large = np.minimum(large, NUM_BUCKETS - 1)
    bucket = np.where(n < max_exact, n, large)
    assert np.all(bucket[FAR_N:] == NUM_BUCKETS - 1) and bucket[FAR_N - 1] != NUM_BUCKETS - 1
    return bucket.astype(np.int32)


def _fold_rows(a, op):
    parts = [a[r:r + SUBLANES] for r in range(0, a.shape[0], SUBLANES)]
    while len(parts) > 1:
        nxt = [op(parts[k], parts[k + 1]) for k in range(0, len(parts) - 1, 2)]
        if len(parts) % 2:
            nxt.append(parts[-1])
        parts = nxt
    return parts[0]


def _attn_kernel(biasc_ref, pinfo_ref, ckmax_ref, iqT_ref, iwT_ref, posqc_ref, posk_ref, ik_ref, qT_ref, k_ref, vT_ref,
                 tbl_ref, toep_ref, tril_ref, o_ref, sc_ref, s_ref, s2_ref, b_ref, qm_ref, m_ref, l_ref, *acc_refs, k_sel, seq):
    i = pl.program_id(1)
    nch = (i * TQ + TQ + CK - 1) // CK
    q_idx = i * TQ + lax.broadcasted_iota(jnp.int32, (1, TQ), 1)
    kf = float(k_sel)

    def chunk_off(c):
        return pl.multiple_of(c * CK, CK)

    def key_idx(off):
        return off + lax.broadcasted_iota(jnp.int32, (CK, TQ), 0)

    def col_reduce(part, op):
        return op(part, axis=0, keepdims=True)

    iqT = iqT_ref[...]
    iwT = iwT_ref[...]
    iq_wide = jnp.concatenate([iqT[h * IDX_HEAD_DIM:(h + 1) * IDX_HEAD_DIM, :] for h in range(IDX_HEADS)], axis=1)

    def score_body(c, carry):
        mn, mx = carry
        off = chunk_off(c)
        ikc = ik_ref[pl.ds(off, CK), :]
        z = jnp.dot(ikc, iq_wide, preferred_element_type=F32)
        s = None
        for h in range(IDX_HEADS):
            t = jnp.maximum(z[:, h * TQ:(h + 1) * TQ], 0.0) * iwT[h:h + 1, :]
            s = t if s is None else s + t
        causal = key_idx(off) <= q_idx
        s_lo = jnp.where(causal, s, -jnp.inf)
        sc_ref[pl.ds(off, CK), :] = s_lo
        mn = jnp.minimum(mn, _fold_rows(jnp.where(causal, s, jnp.inf), jnp.minimum))
        mx = jnp.maximum(mx, _fold_rows(s_lo, jnp.maximum))
        return mn, mx

    mn8, mx8 = lax.fori_loop(0, nch, score_body,
                             (jnp.full((SUBLANES, TQ), jnp.inf, F32), jnp.full((SUBLANES, TQ), -jnp.inf, F32)))
    mn = col_reduce(mn8, jnp.min)
    mx = col_reduce(mx8, jnp.max)

    def count(pred_fn):
        def body(c, acc):
            off = chunk_off(c)
            blk = sc_ref[pl.ds(off, CK), :]
            return acc + _fold_rows(jnp.where(pred_fn(blk, off), 1.0, 0.0), jnp.add)
        acc = lax.fori_loop(0, nch, body, jnp.zeros((SUBLANES, TQ), F32))
        return col_reduce(acc, jnp.sum)

    def bisect_round(_, st):
        lo, hi, c_lo = st
        mid = 0.5 * (lo + hi)
        c = count(lambda blk, off: blk >= mid)
        ok = c >= kf
        return jnp.where(ok, mid, lo), jnp.where(ok, hi, mid), jnp.where(ok, c, c_lo)

    c_all = (q_idx + 1).astype(F32)
    lo, hi, c_lo = lax.fori_loop(0, BISECT_ROUNDS, bisect_round, (mn, mx, c_all))

    def min_ge_body(c, acc):
        blk = sc_ref[pl.ds(chunk_off(c), CK), :]
        return jnp.minimum(acc, _fold_rows(jnp.where(blk >= lo, blk, jnp.inf), jnp.minimum))

    cur0 = col_reduce(lax.fori_loop(0, nch, min_ge_body, jnp.full((SUBLANES, TQ), jnp.inf, F32)), jnp.min)

    def walk_cond(st):
        return st[3] > 0.0

    def walk_body(st):
        cur, c_ge, _, _ = st

        def body(c, carry):
            cnt, nxt = carry
            blk = sc_ref[pl.ds(chunk_off(c), CK), :]
            gt = blk > cur
            cnt = cnt + _fold_rows(jnp.where(gt, 1.0, 0.0), jnp.add)
            nxt = jnp.minimum(nxt, _fold_rows(jnp.where(gt, blk, jnp.inf), jnp.minimum))
            return cnt, nxt

        cnt, nxt = lax.fori_loop(0, nch, body,
                                 (jnp.zeros((SUBLANES, TQ), F32), jnp.full((SUBLANES, TQ), jnp.inf, F32)))
        c_gt = col_reduce(cnt, jnp.sum)
        nxt = col_reduce(nxt, jnp.min)
        adv = c_gt >= kf
        cur = jnp.where(adv, nxt, cur)
        c_ge = jnp.where(adv, c_gt, c_ge)
        return cur, c_ge, c_gt, jnp.max(jnp.where(adv, 1.0, 0.0))

    tau, c_ge, c_gt, _ = lax.while_loop(
        walk_cond, walk_body, (cur0, c_lo, jnp.zeros((1, TQ), F32), jnp.float32(1.0)))

    trim = c_ge > kf

    def mask_with_ties():
        room = kf - c_gt

        def mask_body(c, seen):
            off = chunk_off(c)
            for j in range(CK // LANES):
                rows = pl.ds(off + j * LANES, LANES)
                blk = sc_ref[rows, :]
                tie = blk == tau
                rank = seen + jnp.dot(tril_ref[...], jnp.where(tie, 1.0, 0.0).astype(BF16),
                                      preferred_element_type=F32)
                keep = (blk > tau) | (tie & (rank <= room))
                sc_ref[rows, :] = jnp.where(keep, 0.0, NEG)
                seen = rank[LANES - 1:LANES, :]
            return seen

        lax.fori_loop(0, nch, mask_body, jnp.zeros((1, TQ), F32))

    def mask_plain():
        def mask_body(c, _):
            off = chunk_off(c)
            sc_ref[pl.ds(off, CK), :] = jnp.where(sc_ref[pl.ds(off, CK), :] >= tau, 0.0, NEG)
            return 0

        lax.fori_loop(0, nch, mask_body, 0)

    lax.cond(jnp.max(jnp.where(trim, 1.0, 0.0)) > 0.0, mask_with_ties, mask_plain)

    m_ref[...] = jnp.full(m_ref.shape, NEG, F32)
    l_ref[...] = jnp.zeros(l_ref.shape, F32)
    for acc in acc_refs:
        acc[...] = jnp.zeros(acc.shape, F32)

    rowi = lax.broadcasted_iota(jnp.int32, (LANES, TQ), 0)
    for pair in range(ATTN_HEADS // 2):
        qp = qT_ref[pair * LANES:(pair + 1) * LANES, :]
        zero = jnp.zeros_like(qp)
        qm_ref[pair] = jnp.concatenate([jnp.where(rowi < ATTN_HEAD_DIM, qp, zero),
                                        jnp.where(rowi >= ATTN_HEAD_DIM, qp, zero)], axis=1)
    ones_rows = jnp.ones((2 * SUBLANES, CK), BF16)

    pq_col = posqc_ref[...]
    batch = pl.program_id(0)
    pq_first, pq_consec, pq_min = (pinfo_ref[batch, r, i] for r in (0, 1, 3))
    far_bias = [biasc_ref[0, h] for h in range(ATTN_HEADS)]
    zero_bias = [biasc_ref[1, h] for h in range(ATTN_HEADS)]
    q_ge_k = (lax.broadcasted_iota(jnp.int32, (LANES, TQ), 1) >= lax.broadcasted_iota(jnp.int32, (LANES, TQ), 0))
    n_sub = CK // LANES
    n_pairs = ATTN_HEADS // 2

    def chunk_is_far(c):
        return (pq_min - ckmax_ref[batch, jnp.minimum(c, seq // CK - 1)]) >= FAR_N

    def stage_bias(c):
        off = chunk_off(c)
        pk_row = posk_ref[:, pl.ds(off, CK)]
        for j in range(n_sub):
            rows = slice(j * LANES, (j + 1) * LANES)
            g = c * n_sub + j
            pk_first, pk_consec, pk_max = (pinfo_ref[batch, r, g] for r in (0, 1, 2))
            all_far = (pq_min - pk_max) >= FAR_N
            all_masked = (off + j * LANES) > (i * TQ + TQ - 1)
            consecutive = (pq_consec > 0) & (pk_consec > 0)
            gap = pq_first - pk_first

            def fill_const(rows=rows):
                for h in range(ATTN_HEADS):
                    b_ref[h, rows, :] = jnp.full((LANES, TQ), far_bias[h], F32)

            def fill_gap0(rows=rows):
                for h in range(ATTN_HEADS):
                    b_ref[h, rows, :] = jnp.where(q_ge_k, toep_ref[h], zero_bias[h])

            def fill_gap128(rows=rows):
                for h in range(ATTN_HEADS):
                    b_ref[h, rows, :] = jnp.where(q_ge_k, far_bias[h], toep_ref[h])

            def fill_lookup(rows=rows):
                pk_sub = pk_row[:, rows]
                n_qk = jnp.clip(pq_col - pk_sub, 0, BIAS_TABLE_N - 1).astype(F32)
                n_kq = n_qk.T.astype(jnp.int32)
                for h in range(ATTN_HEADS):
                    tb = jnp.broadcast_to(tbl_ref[h:h + 1, :], (LANES, BIAS_TABLE_N))
                    b_ref[h, rows, :] = jnp.take_along_axis(tb, n_kq, axis=1)

            def fill_near(fill_gap0=fill_gap0, fill_gap128=fill_gap128, fill_lookup=fill_lookup,
                          consecutive=consecutive, gap=gap):
                lax.cond(consecutive & (gap == 0), fill_gap0,
                         lambda: lax.cond(consecutive & (gap == LANES), fill_gap128, fill_lookup))

            lax.cond(all_far | all_masked, fill_const, fill_near)

    def logits_phase(c, s_buf, const_bias):
        off = chunk_off(c)
        mb = sc_ref[pl.ds(off, CK), :]
        m_cur = []
        for pair in range(n_pairs):
            kc = k_ref[pl.ds(off, CK), pair * LANES:(pair + 1) * LANES]
            s2 = jnp.dot(kc, qm_ref[pair], preferred_element_type=F32)
            for sub, h in enumerate((2 * pair, 2 * pair + 1)):
                s = s2[:, sub * TQ:(sub + 1) * TQ] + mb
                if not const_bias:
                    s = s + b_ref[h]
                s_buf[h] = s
                top = col_reduce(_fold_rows(s, jnp.maximum), jnp.max)
                m_cur.append(top + far_bias[h] if const_bias else top)
        return m_cur

    def update_phase(c, s_buf, m_cur, const_bias):
        off = chunk_off(c)
        m_prev = m_ref[...]
        l_prev = l_ref[...]
        if isinstance(m_cur, list):
            m_new = [jnp.maximum(m_prev[h:h + 1, :], m_cur[h]) for h in range(ATTN_HEADS)]
            alpha = [jnp.exp2(m_prev[h:h + 1, :] - m_new[h]) for h in range(ATTN_HEADS)]
        else:
            m_all = jnp.maximum(m_prev, m_cur)
            a_all = jnp.exp2(m_prev - m_all)
            m_new = [m_all[h:h + 1, :] for h in range(ATTN_HEADS)]
            alpha = [a_all[h:h + 1, :] for h in range(ATTN_HEADS)]
        l_new = []
        for pair in range(n_pairs):
            heads = (2 * pair, 2 * pair + 1)
            sub_m = [m_new[h] - far_bias[h] if const_bias else m_new[h] for h in heads]
            p2 = jnp.concatenate([jnp.exp2(s_buf[h] - sm).astype(BF16) for h, sm in zip(heads, sub_m)],
                                 axis=1)
            lhs = jnp.concatenate([vT_ref[pair * LANES:(pair + 1) * LANES, pl.ds(off, CK)], ones_rows],
                                  axis=0)
            out = jnp.dot(lhs, p2, preferred_element_type=F32)
            for sub, h in enumerate(heads):
                cols = slice(sub * TQ, (sub + 1) * TQ)
                acc = acc_refs[h]
                acc[...] = alpha[h] * acc[...] + out[sub * ATTN_HEAD_DIM:(sub + 1) * ATTN_HEAD_DIM, cols]
                l_new.append(alpha[h] * l_prev[h:h + 1, :] + out[LANES:LANES + 1, cols])
        l_ref[...] = jnp.concatenate(l_new, axis=0)
        m_ref[...] = jnp.concatenate(m_new, axis=0)

    n_far = lax.while_loop(lambda c: (c < nch) & chunk_is_far(c), lambda c: c + 1, jnp.int32(0))
    n_steps = n_far // 2
    def far_logits(c, s_buf):
        return jnp.concatenate(logits_phase(c, s_buf, True), axis=0)

    m_first = lax.cond(n_steps > 0, lambda: far_logits(0, s_ref), lambda: jnp.zeros((ATTN_HEADS, TQ), F32))

    def far_step(t, m_even):
        c = 2 * t
        m_odd = far_logits(c + 1, s2_ref)
        update_phase(c, s_ref, m_even, True)
        m_even = far_logits(jnp.minimum(c + 2, 2 * n_steps - 2), s_ref)
        update_phase(c + 1, s2_ref, m_odd, True)
        return m_even

    lax.fori_loop(0, n_steps, far_step, m_first)

    def tail_chunk(c, _):
        def run(const_bias):
            if not const_bias:
                stage_bias(c)
            update_phase(c, s_ref, logits_phase(c, s_ref, const_bias), const_bias)

        lax.cond(chunk_is_far(c), lambda: run(True), lambda: run(False))
        return 0

    lax.fori_loop(2 * n_steps, nch, tail_chunk, 0)

    outT = jnp.concatenate([acc_refs[h][...] / l_ref[h:h + 1, :] for h in range(ATTN_HEADS)], axis=0)
    o_ref[...] = outT.T.astype(o_ref.dtype)


def _toeplitz_kernel(tbl_ref, o_ref):
    q = lax.broadcasted_iota(jnp.int32, (LANES, LANES), 1)
    k = lax.broadcasted_iota(jnp.int32, (LANES, LANES), 0)
    idx = (q - k) & (BIAS_TABLE_N - 1)
    for h in range(ATTN_HEADS):
        tb = jnp.broadcast_to(tbl_ref[h:h + 1, :], (LANES, BIAS_TABLE_N))
        o_ref[h] = jnp.take_along_axis(tb, idx, axis=1)


def _sparse_attention(fm, tokb, ik, iwT, positions, rel_bias):
    B, S, _ = tokb.shape
    k_sel = min(TOPK_MAX, S // 4)
    bucket = _t5_bucket_table()
    tbl = rel_bias[bucket].T.astype(F32) * LOG2E
    bias_c = jnp.stack([tbl[:, BIAS_TABLE_N - 1], tbl[:, 0]])
    toep = pl.pallas_call(
        _toeplitz_kernel,
        out_shape=jax.ShapeDtypeStruct((ATTN_HEADS, LANES, LANES), F32),
        name="bias_toeplitz",
    )(tbl)
    pos_row = positions.reshape(B, 1, S)
    pos_col = positions.reshape(B, S, 1)
    pos_t = positions.reshape(B, S // LANES, LANES)
    consec = jnp.all(pos_t - pos_t[:, :, :1] == jnp.arange(LANES, dtype=positions.dtype), axis=-1)
    pinfo = jnp.stack([pos_t[:, :, 0], consec.astype(jnp.int32), jnp.max(pos_t, axis=-1),
                       jnp.min(pos_t, axis=-1)], axis=1).astype(jnp.int32)
    ck_max = jnp.max(positions.reshape(B, S // CK, CK), axis=-1)
    tril = jnp.tril(jnp.ones((LANES, LANES), BF16))
    sq = pl.Squeezed()
    iq_blk = (2 * ATTN_W) // IDX_Q_W
    k_blk = (tokb.shape[2] - ATTN_W) // ATTN_W
    return pl.pallas_call(
        functools.partial(_attn_kernel, k_sel=k_sel, seq=S),
        grid=(B, S // TQ),
        in_specs=[pl.BlockSpec(memory_space=pltpu.SMEM),
                  pl.BlockSpec(memory_space=pltpu.SMEM),
                  pl.BlockSpec(memory_space=pltpu.SMEM),
                  pl.BlockSpec((sq, IDX_Q_W, TQ), lambda b, i: (b, iq_blk, i)),
                  pl.BlockSpec((sq, SUBLANES, TQ), lambda b, i: (b, 0, i)),
                  pl.BlockSpec((sq, TQ, 1), lambda b, i: (b, i, 0)),
                  pl.BlockSpec((sq, 1, S), lambda b, i: (b, 0, 0)),
                  pl.BlockSpec((sq, S, IDX_HEAD_DIM), lambda b, i: (b, 0, 0)),
                  pl.BlockSpec((sq, ATTN_W, TQ), lambda b, i: (b, 0, i)),
                  pl.BlockSpec((sq, S, ATTN_W), lambda b, i: (b, 0, k_blk)),
                  pl.BlockSpec((sq, ATTN_W, S), lambda b, i: (b, 1, 0)),
                  pl.BlockSpec((ATTN_HEADS, BIAS_TABLE_N), lambda b, i: (0, 0)),
                  pl.BlockSpec((ATTN_HEADS, LANES, LANES), lambda b, i: (0, 0, 0)),
                  pl.BlockSpec((LANES, LANES), lambda b, i: (0, 0))],
        out_specs=pl.BlockSpec((sq, TQ, ATTN_W), lambda b, i: (b, i, 0)),
        out_shape=jax.ShapeDtypeStruct((B, S, ATTN_W), BF16),
        scratch_shapes=[pltpu.VMEM((S, TQ), F32),
                        pltpu.VMEM((ATTN_HEADS, CK, TQ), F32),
                        pltpu.VMEM((ATTN_HEADS, CK, TQ), F32),
                        pltpu.VMEM((ATTN_HEADS, CK, TQ), F32),
                        pltpu.VMEM((ATTN_HEADS // 2, LANES, 2 * TQ), BF16),
                        pltpu.VMEM((ATTN_HEADS, TQ), F32),
                        pltpu.VMEM((ATTN_HEADS, TQ), F32)]
                       + [pltpu.VMEM((ATTN_HEAD_DIM, TQ), F32)] * ATTN_HEADS,
        compiler_params=_cparams(2),
        name="sparse_attention",
    )(bias_c, pinfo, ck_max, fm, iwT, pos_col, pos_row, ik, fm, tokb, fm, tbl, toep, tril)


def _retention_kernel(q_ref, k_ref, v_ref, g_ref, decay_ref, xi_ref, zeta_ref, gch_ref, o_ref, r_ref):
    @pl.when(pl.program_id(1) == 0)
    def _():
        r_ref[...] = jnp.zeros(r_ref.shape, F32)

    C = RET_CHUNK
    lane = lax.broadcasted_iota(jnp.int32, (C, LANES), 1)
    row = lax.broadcasted_iota(jnp.int32, (LANES, RET_V_DIM), 0)
    for pair in range(RET_HEADS // 2):
        q_pair = q_ref[:, pair * LANES:(pair + 1) * LANES]
        k_pair = k_ref[:, pair * LANES:(pair + 1) * LANES]
        v_pair = v_ref[:, 2 * pair * RET_V_DIM:(2 * pair + 2) * RET_V_DIM]
        r_pair = r_ref[pair]
        r_bf = r_pair.astype(BF16)
        for sub in range(2):
            h = 2 * pair + sub
            in_head = (lane >= sub * RET_QK_DIM) & (lane < (sub + 1) * RET_QK_DIM)
            qm = jnp.where(in_head, q_pair, jnp.zeros_like(q_pair))
            v_h = v_pair[:, sub * RET_V_DIM:(sub + 1) * RET_V_DIM]
            inner = lax.dot_general(qm, k_pair, (((1,), (1,)), ((), ())),
                                    preferred_element_type=F32) * decay_ref[h]
            o = (jnp.dot(inner.astype(BF16), v_h, preferred_element_type=F32)
                 + jnp.dot(qm, r_bf, preferred_element_type=F32) * xi_ref[h])
            mu = jnp.mean(o, axis=-1, keepdims=True)
            d = o - mu
            var = jnp.mean(d * d, axis=-1, keepdims=True)
            hn = d * lax.rsqrt(var + LN_EPS)
            gate = g_ref[:, h * RET_V_DIM:(h + 1) * RET_V_DIM].astype(F32)
            o_ref[:, h * RET_V_DIM:(h + 1) * RET_V_DIM] = (gate * hn).astype(o_ref.dtype)
        kz = (k_pair.astype(F32) * zeta_ref[pair]).astype(BF16)
        upd = lax.dot_general(kz, v_pair, (((0,), (0,)), ((), ())), preferred_element_type=F32)
        r_ref[pair] = r_pair * gch_ref[pair] + jnp.where(row < RET_QK_DIM, upd[:, :RET_V_DIM], upd[:, RET_V_DIM:])


def _retention(qk, tokb, gates, B, S):
    C = RET_CHUNK
    H = RET_HEADS
    nc = S // C
    gamma = 1.0 - 2.0 ** (-5.0 - jnp.arange(H, dtype=F32))
    log_g = jnp.log(gamma)
    n = jnp.arange(C, dtype=F32)
    diff = n[:, None] - n[None, :]
    decay_in = jnp.where(diff[None] >= 0, jnp.exp(log_g[:, None, None] * jnp.maximum(diff, 0.0)[None]), 0.0)
    xi = jnp.exp(log_g[None, :] * (n[:, None] + 1.0))
    zeta = jnp.exp(log_g[None, :] * (C - 1.0 - n[:, None]))
    g_chunk = jnp.exp(log_g * C)
    xi_b = jnp.broadcast_to(xi.T[:, :, None], (H, C, RET_V_DIM))
    zeta_b = jnp.repeat(zeta, RET_QK_DIM, axis=1).reshape(C, H // 2, LANES).transpose(1, 0, 2)
    gch_b = jnp.broadcast_to(jnp.repeat(g_chunk, RET_QK_DIM).reshape(H // 2, LANES, 1),
                             (H // 2, LANES, RET_V_DIM))
    return pl.pallas_call(
        _retention_kernel,
        grid=(B, nc),
        in_specs=[pl.BlockSpec((C, RET_QK_W), lambda b, i: (b * nc + i, 0)),
                  pl.BlockSpec((C, RET_QK_W), lambda b, i: (b * nc + i, 1)),
                  pl.BlockSpec((C, RET_V_W), lambda b, i: (b * nc + i, 0)),
                  pl.BlockSpec((C, RET_V_W), lambda b, i: (b * nc + i, 0)),
                  pl.BlockSpec((H, C, C), lambda b, i: (0, 0, 0)),
                  pl.BlockSpec((H, C, RET_V_DIM), lambda b, i: (0, 0, 0)),
                  pl.BlockSpec((H // 2, C, LANES), lambda b, i: (0, 0, 0)),
                  pl.BlockSpec((H // 2, LANES, RET_V_DIM), lambda b, i: (0, 0, 0))],
        out_specs=pl.BlockSpec((C, RET_V_W), lambda b, i: (b * nc + i, 0)),
        out_shape=jax.ShapeDtypeStruct((B * S, RET_V_W), BF16),
        scratch_shapes=[pltpu.VMEM((H // 2, LANES, RET_V_DIM), F32)],
        compiler_params=_cparams(2),
        name="retention",
    )(qk, qk, tokb, gates, decay_in, xi_b, zeta_b, gch_b)


def _layer_norm(z, g, b):
    mu = jnp.mean(z, axis=-1, keepdims=True)
    d = z - mu
    var = jnp.mean(d * d, axis=-1, keepdims=True)
    return d * lax.rsqrt(var + LN_EPS) * g + b


def _merge_kernel(x_ref, ya_ref, yr_ref, ga_ref, gr_ref, wa_ref, wr_ref, wo_ref, g_ref, b_ref,
                  x1_ref, x1b_ref):
    a = jnp.dot(ya_ref[...], wa_ref[...], preferred_element_type=F32)
    r = jnp.dot(yr_ref[...], wr_ref[...], preferred_element_type=F32)
    h = ga_ref[...].astype(F32) * a + gr_ref[...].astype(F32) * r
    mix = jnp.dot(h.astype(BF16), wo_ref[...], preferred_element_type=F32)
    x1 = _layer_norm(DEEPNORM_ALPHA * x_ref[...] + mix, g_ref[...], b_ref[...])
    x1_ref[...] = x1
    x1b_ref[...] = x1.astype(BF16)


def _merge(x, ya, yr, gates, wa, wr, wo, g, b, tm=512):
    T, D = x.shape
    tm = min(tm, T)
    row = lambda i: (i, 0)
    fixed = lambda i: (0, 0)
    return pl.pallas_call(
        _merge_kernel,
        grid=(T // tm,),
        in_specs=[pl.BlockSpec((tm, D), row),
                  pl.BlockSpec((tm, ya.shape[1]), row),
                  pl.BlockSpec((tm, yr.shape[1]), row),
                  pl.BlockSpec((tm, D), lambda i: (i, 1)),
                  pl.BlockSpec((tm, D), lambda i: (i, 2)),
                  pl.BlockSpec(wa.shape, fixed),
                  pl.BlockSpec(wr.shape, fixed),
                  pl.BlockSpec(wo.shape, fixed),
                  pl.BlockSpec((1, D), fixed),
                  pl.BlockSpec((1, D), fixed)],
        out_specs=[pl.BlockSpec((tm, D), row), pl.BlockSpec((tm, D), row)],
        out_shape=[jax.ShapeDtypeStruct((T, D), F32), jax.ShapeDtypeStruct((T, D), BF16)],
        compiler_params=_cparams(1),
        name="merge",
    )(x, ya, yr, gates, gates, wa, wr, wo, g, b)


def _ffn_kernel(x1b_ref, x1_ref, wu_ref, wd_ref, g_ref, b_ref, o_ref, acc_ref):
    f = pl.program_id(1)

    @pl.when(f == 0)
    def _():
        acc_ref[...] = jnp.zeros(acc_ref.shape, F32)

    hid = jnp.maximum(jnp.dot(x1b_ref[...], wu_ref[...], preferred_element_type=F32), 0.0)
    acc_ref[...] += jnp.dot((hid * hid).astype(BF16), wd_ref[...], preferred_element_type=F32)

    @pl.when(f == pl.num_programs(1) - 1)
    def _():
        o_ref[...] = _layer_norm(DEEPNORM_ALPHA * x1_ref[...] + acc_ref[...], g_ref[...], b_ref[...])


def _ffn(x1b, x1, wu, wd, g, b, tm=1024, tf=1024):
    T, D = x1.shape
    F = wu.shape[1]
    tm = min(tm, T)
    return pl.pallas_call(
        _ffn_kernel,
        grid=(T // tm, F // tf),
        in_specs=[pl.BlockSpec((tm, D), lambda i, f: (i, 0)),
                  pl.BlockSpec((tm, D), lambda i, f: (i, 0)),
                  pl.BlockSpec((D, tf), lambda i, f: (0, f)),
                  pl.BlockSpec((tf, D), lambda i, f: (f, 0)),
                  pl.BlockSpec((1, D), lambda i, f: (0, 0)),
                  pl.BlockSpec((1, D), lambda i, f: (0, 0))],
        out_specs=pl.BlockSpec((tm, D), lambda i, f: (i, 0)),
        out_shape=jax.ShapeDtypeStruct((T, D), F32),
        scratch_shapes=[pltpu.VMEM((tm, D), F32)],
        compiler_params=_cparams(2),
        name="ffn",
    )(x1b, x1, wu, wd, g, b)


def _rot_half_weight(w):
    D, N = w.shape
    half = RET_QK_DIM // 2
    wh = w.reshape(D, N // RET_QK_DIM, 2, half)
    return jnp.stack([-wh[:, :, 1], wh[:, :, 0]], axis=2).reshape(D, N)


def kernel(x, positions, w_in, rel_bias, idx_k_ln_g, idx_k_ln_b, w_attn_branch, w_ret_branch,
           w_out, ln_mix_g, ln_mix_b, w_up, w_down, ln_ffn_g, ln_ffn_b):
    B, S, D = x.shape
    T = B * S
    sizes = (ATTN_W, ATTN_W, ATTN_W, IDX_Q_W, IDX_HEAD_DIM, IDX_HEADS,
             RET_QK_W, RET_QK_W, RET_V_W, RET_V_W, D, D)
    offs = [0] + [int(o) for o in np.cumsum(sizes)]
    cos, sin = _rope_tables(positions)
    xf = x.reshape(T, D)
    for l in range(DEPTH):
        wl = w_in[l].astype(BF16)
        cols = [wl[:, offs[k]:offs[k + 1]] for k in range(len(sizes))]
        (w_qa, w_ka, w_va, w_iq, w_ik, w_iw, w_qr, w_kr, w_vr, w_gr, w_ga, w_gtr) = cols
        xb = xf.astype(BF16)
        w_qa = (w_in[l][:, offs[0]:offs[1]] * (ATTN_HEAD_DIM ** -0.5 * LOG2E)).astype(BF16)
        w_kr = w_kr * (RET_QK_DIM ** -0.5)

        fm = _proj_t(xb, jnp.concatenate([w_qa, w_va, w_iq], axis=1).T.astype(BF16), B, S, BF16)
        tokb = _proj(xb, jnp.concatenate([w_vr, w_ka], axis=1).astype(BF16), BF16)
        gates = _proj_gates(xb, jnp.concatenate([w_gr, w_ga, w_gtr], axis=1).astype(BF16), tn=D)
        w_rope = jnp.concatenate([w_qr, w_kr], axis=1)
        w_rope_rot = jnp.concatenate([_rot_half_weight(w_qr), _rot_half_weight(w_kr)], axis=1)
        qk_r = _proj_rope(xb, w_rope.astype(BF16), w_rope_rot.astype(BF16), cos, sin)
        pad = LANES - IDX_HEAD_DIM - IDX_HEADS
        w_idx = jnp.concatenate([w_ik, w_iw, jnp.zeros((D, pad), BF16)], axis=1)
        g_pad = jnp.concatenate([idx_k_ln_g[l], jnp.zeros((LANES - IDX_HEAD_DIM,), F32)]).reshape(1, LANES)
        b_pad = jnp.concatenate([idx_k_ln_b[l], jnp.zeros((LANES - IDX_HEAD_DIM,), F32)]).reshape(1, LANES)
        idx = _proj_idx(xb, w_idx, g_pad, b_pad,
                        (IDX_HEAD_DIM ** -0.5) * (IDX_HEADS ** -0.5)).reshape(B, S, LANES)
        ik = idx[:, :, :IDX_HEAD_DIM].astype(BF16)
        iwT = jnp.swapaxes(idx[:, :, IDX_HEAD_DIM:IDX_HEAD_DIM + SUBLANES], 1, 2)

        y_a = _sparse_attention(fm, tokb.reshape(B, S, -1), ik, iwT, positions, rel_bias)
        y_r = _retention(qk_r, tokb, gates, B, S)
        x1, x1b = _merge(xf, y_a.reshape(T, ATTN_W), y_r, gates,
                         w_attn_branch[l].astype(BF16), w_ret_branch[l].astype(BF16),
                         w_out[l].astype(BF16), ln_mix_g[l].reshape(1, D), ln_mix_b[l].reshape(1, D))
        xf = _ffn(x1b, x1, w_up[l].astype(BF16), w_down[l].astype(BF16),
                  ln_ffn_g[l].reshape(1, D), ln_ffn_b[l].reshape(1, D))
    return xf.reshape(B, S, D)
```

```python
import functools
import math

import numpy as np
import jax
import jax.numpy as jnp
from jax import lax
from jax.experimental import pallas as pl
from jax.experimental.pallas import tpu as pltpu

F32 = jnp.float32
BF16 = jnp.bfloat16

ATTN_HEADS = 8
ATTN_HEAD_DIM = 64
ATTN_W = ATTN_HEADS * ATTN_HEAD_DIM
IDX_HEADS = 4
IDX_HEAD_DIM = 64
IDX_Q_W = IDX_HEADS * IDX_HEAD_DIM
TOPK_MAX = 256
RET_HEADS = 8
RET_QK_DIM = 64
RET_V_DIM = 128
RET_QK_W = RET_HEADS * RET_QK_DIM
RET_V_W = RET_HEADS * RET_V_DIM
RET_CHUNK = 128
ROPE_BASE = 10000.0
NUM_BUCKETS = 32
MAX_DISTANCE = 128
LN_EPS = 1e-5
DEPTH = 1
DEEPNORM_ALPHA = (2.0 * DEPTH) ** 0.25

LANES = 128
SUBLANES = 8
VMEM_LIMIT = 56 * 1024 * 1024

TQ = 128
CK = 512
NEG = -1e30
LOG2E = math.log2(math.e)
BISECT_ROUNDS = 18
BIAS_TABLE_N = 128
FAR_N = 113


def _cparams(n_grid):
    return pltpu.CompilerParams(
        dimension_semantics=("arbitrary",) * n_grid,
        vmem_limit_bytes=VMEM_LIMIT)


def _trig_kernel(pos_ref, inv_ref, cos_ref, sin_ref):
    ang = pos_ref[...] * inv_ref[...]
    cos_ref[...] = jnp.cos(ang)
    sin_ref[...] = jnp.sin(ang)


def _rope_tables(positions):
    B, S = positions.shape
    half = RET_QK_DIM // 2
    inv = ROPE_BASE ** (-jnp.arange(half, dtype=F32) / half)
    per_row = LANES // half
    rows = B * S // per_row
    pos_e = jnp.repeat(positions.astype(F32).reshape(rows, per_row), half, axis=1)
    inv_e = jnp.tile(inv, per_row).reshape(1, LANES)
    tr = min(rows, 1024)
    cos, sin = pl.pallas_call(
        _trig_kernel,
        grid=(rows // tr,),
        in_specs=[pl.BlockSpec((tr, LANES), lambda i: (i, 0)),
                  pl.BlockSpec((1, LANES), lambda i: (0, 0))],
        out_specs=[pl.BlockSpec((tr, LANES), lambda i: (i, 0))] * 2,
        out_shape=[jax.ShapeDtypeStruct((rows, LANES), F32)] * 2,
        compiler_params=_cparams(1),
        name="rope_tables",
    )(pos_e, inv_e)
    cos = jnp.tile(cos.reshape(B * S, half), (1, per_row))
    sin = jnp.tile(sin.reshape(B * S, half), (1, per_row))
    return cos, sin


def _x_wt(x, wT):
    return lax.dot_general(x, wT, (((1,), (1,)), ((), ())), preferred_element_type=F32)


def _proj_kernel(x_ref, w_ref, o_ref):
    o_ref[...] = _x_wt(x_ref[...], w_ref[...]).astype(o_ref.dtype)


def _proj(xb, wT, out_dtype, tm=1024, tn=512):
    T, D = xb.shape
    N = wT.shape[0]
    tn = min(N, tn)
    tm = min(tm, T)
    return pl.pallas_call(
        _proj_kernel,
        grid=(T // tm, N // tn),
        in_specs=[pl.BlockSpec((tm, D), lambda i, j: (i, 0)),
                  pl.BlockSpec((tn, D), lambda i, j: (j, 0))],
        out_specs=pl.BlockSpec((tm, tn), lambda i, j: (i, j)),
        out_shape=jax.ShapeDtypeStruct((T, N), out_dtype),
        compiler_params=_cparams(2),
        name="proj",
    )(xb, wT)


def _proj_gates_kernel(x_ref, w_ref, o_ref):
    acc = _x_wt(x_ref[...], w_ref[...])
    sig = 1.0 / (1.0 + jnp.exp(-acc))
    o_ref[...] = jnp.where(pl.program_id(1) == 0, acc * sig, sig).astype(o_ref.dtype)


def _proj_gates(xb, wT, tn, tm=1024):
    T, D = xb.shape
    N = wT.shape[0]
    tm = min(tm, T)
    return pl.pallas_call(
        _proj_gates_kernel,
        grid=(T // tm, N // tn),
        in_specs=[pl.BlockSpec((tm, D), lambda i, j: (i, 0)),
                  pl.BlockSpec((tn, D), lambda i, j: (j, 0))],
        out_specs=pl.BlockSpec((tm, tn), lambda i, j: (i, j)),
        out_shape=jax.ShapeDtypeStruct((T, N), BF16),
        compiler_params=_cparams(2),
        name="proj_gates",
    )(xb, wT)


def _proj_t_kernel(wT_ref, x_ref, o_ref):
    acc = lax.dot_general(wT_ref[...], x_ref[...], (((1,), (1,)), ((), ())), preferred_element_type=F32)
    o_ref[...] = acc.astype(o_ref.dtype)


def _proj_t(xb, wT, B, S, out_dtype, tm=1024, tn=640):
    T, D = xb.shape
    N = wT.shape[0]
    tn = min(N, tn)
    tm = min(tm, S)
    nsb = S // tm
    return pl.pallas_call(
        _proj_t_kernel,
        grid=(T // tm, N // tn),
        in_specs=[pl.BlockSpec((tn, D), lambda i, j: (j, 0)),
                  pl.BlockSpec((tm, D), lambda i, j: (i, 0))],
        out_specs=pl.BlockSpec((pl.Squeezed(), tn, tm), lambda i, j: (i // nsb, j, i % nsb)),
        out_shape=jax.ShapeDtypeStruct((B, N, S), out_dtype),
        compiler_params=_cparams(2),
        name="proj_t",
    )(wT, xb)


def _proj_rope_kernel(x_ref, w_ref, wr_ref, cos_ref, sin_ref, o_ref):
    x = x_ref[...]
    a = _x_wt(x, w_ref[...])
    r = _x_wt(x, wr_ref[...])
    reps = a.shape[1] // LANES
    cos = jnp.concatenate([cos_ref[...]] * reps, axis=1)
    sin = jnp.concatenate([sin_ref[...]] * reps, axis=1)
    o_ref[...] = (a * cos + r * sin).astype(o_ref.dtype)


def _proj_rope(xb, wT, wT_rot, cos, sin, tm=1024, tn=512):
    T, D = xb.shape
    N = wT.shape[0]
    tm = min(tm, T)
    return pl.pallas_call(
        _proj_rope_kernel,
        grid=(T // tm, N // tn),
        in_specs=[pl.BlockSpec((tm, D), lambda i, j: (i, 0)),
                  pl.BlockSpec((tn, D), lambda i, j: (j, 0)),
                  pl.BlockSpec((tn, D), lambda i, j: (j, 0)),
                  pl.BlockSpec((tm, LANES), lambda i, j: (i, 0)),
                  pl.BlockSpec((tm, LANES), lambda i, j: (i, 0))],
        out_specs=pl.BlockSpec((tm, tn), lambda i, j: (i, j)),
        out_shape=jax.ShapeDtypeStruct((T, N), BF16),
        compiler_params=_cparams(2),
        name="proj_rope",
    )(xb, wT, wT_rot, cos, sin)


def _proj_idx_kernel(x_ref, w_ref, g_ref, b_ref, o_ref, *, iw_scale):
    acc = _x_wt(x_ref[...], w_ref[...])
    lane = lax.broadcasted_iota(jnp.int32, acc.shape, 1)
    is_k = lane < IDX_HEAD_DIM
    mu = jnp.sum(jnp.where(is_k, acc, 0.0), axis=-1, keepdims=True) / IDX_HEAD_DIM
    d = acc - mu
    var = jnp.sum(jnp.where(is_k, d * d, 0.0), axis=-1, keepdims=True) / IDX_HEAD_DIM
    ln = d * lax.rsqrt(var + LN_EPS) * g_ref[...] + b_ref[...]
    o_ref[...] = jnp.where(is_k, ln, acc * iw_scale)


def _proj_idx(xb, w_pad, g_pad, b_pad, iw_scale, tm=1024):
    T, D = xb.shape
    tm = min(tm, T)
    return pl.pallas_call(
        functools.partial(_proj_idx_kernel, iw_scale=iw_scale),
        grid=(T // tm,),
        in_specs=[pl.BlockSpec((tm, D), lambda i: (i, 0)),
                  pl.BlockSpec((LANES, D), lambda i: (0, 0)),
                  pl.BlockSpec((1, LANES), lambda i: (0, 0)),
                  pl.BlockSpec((1, LANES), lambda i: (0, 0))],
        out_specs=pl.BlockSpec((tm, LANES), lambda i: (i, 0)),
        out_shape=jax.ShapeDtypeStruct((T, LANES), F32),
        compiler_params=_cparams(1),
        name="proj_idx",
    )(xb, w_pad, g_pad, b_pad)


def _t5_bucket_table():
    n = np.arange(BIAS_TABLE_N)
    max_exact = NUM_BUCKETS // 2
    nf = np.maximum(n, 1).astype(np.float64)
    large = max_exact + (np.log(nf / max_exact) / math.log(MAX_DISTANCE / max_exact)
                         * (NUM_BUCKETS - max_exact)).astype(np.int64)
    large = np.minimum(large, NUM_BUCKETS - 1)
    bucket = np.where(n < max_exact, n, large)
    assert np.all(bucket[FAR_N:] == NUM_BUCKETS - 1) and bucket[FAR_N - 1] != NUM_BUCKETS - 1
    return bucket.astype(np.int32)


def _fold_rows(a, op):
    parts = [a[r:r + SUBLANES] for r in range(0, a.shape[0], SUBLANES)]
    while len(parts) > 1:
        nxt = [op(parts[k], parts[k + 1]) for k in range(0, len(parts) - 1, 2)]
        if len(parts) % 2:
            nxt.append(parts[-1])
        parts = nxt
    return parts[0]


def _attn_kernel(biasc_ref, pinfo_ref, ckmax_ref, iqT_ref, iwT_ref, posqc_ref, posk_ref, ik_ref, qT_ref, k_ref, vT_ref,
                 tbl_ref, toep_ref, tril_ref, o_ref, sc_ref, s_ref, s2_ref, b_ref, qm_ref, m_ref, l_ref, *acc_refs, k_sel, seq):
    i = pl.program_id(1)
    nch = (i * TQ + TQ + CK - 1) // CK
    q_idx = i * TQ + lax.broadcasted_iota(jnp.int32, (1, TQ), 1)
    kf = float(k_sel)

    def chunk_off(c):
        return pl.multiple_of(c * CK, CK)

    def key_idx(off):
        return off + lax.broadcasted_iota(jnp.int32, (CK, TQ), 0)

    def col_reduce(part, op):
        return op(part, axis=0, keepdims=True)

    iqT = iqT_ref[...]
    iwT = iwT_ref[...]
    iq_wide = jnp.concatenate([iqT[h * IDX_HEAD_DIM:(h + 1) * IDX_HEAD_DIM, :] for h in range(IDX_HEADS)], axis=1)

    def score_body(c, carry):
        mn, mx = carry
        off = chunk_off(c)
        ikc = ik_ref[pl.ds(off, CK), :]
        z = jnp.dot(ikc, iq_wide, preferred_element_type=F32)
        s = None
        for h in range(IDX_HEADS):
            t = jnp.maximum(z[:, h * TQ:(h + 1) * TQ], 0.0) * iwT[h:h + 1, :]
            s = t if s is None else s + t
        causal = key_idx(off) <= q_idx
        s_lo = jnp.where(causal, s, -jnp.inf)
        sc_ref[pl.ds(off, CK), :] = s_lo
        mn = jnp.minimum(mn, _fold_rows(jnp.where(causal, s, jnp.inf), jnp.minimum))
        mx = jnp.maximum(mx, _fold_rows(s_lo, jnp.maximum))
        return mn, mx

    mn8, mx8 = lax.fori_loop(0, nch, score_body,
                             (jnp.full((SUBLANES, TQ), jnp.inf, F32), jnp.full((SUBLANES, TQ), -jnp.inf, F32)))
    mn = col_reduce(mn8, jnp.min)
    mx = col_reduce(mx8, jnp.max)

    def count(pred_fn):
        def body(c, acc):
            off = chunk_off(c)
            blk = sc_ref[pl.ds(off, CK), :]
            return acc + _fold_rows(jnp.where(pred_fn(blk, off), 1.0, 0.0), jnp.add)
        acc = lax.fori_loop(0, nch, body, jnp.zeros((SUBLANES, TQ), F32))
        return col_reduce(acc, jnp.sum)

    def bisect_round(_, st):
        lo, hi, c_lo = st
        mid = 0.5 * (lo + hi)
        c = count(lambda blk, off: blk >= mid)
        ok = c >= kf
        return jnp.where(ok, mid, lo), jnp.where(ok, hi, mid), jnp.where(ok, c, c_lo)

    c_all = (q_idx + 1).astype(F32)
    lo, hi, c_lo = lax.fori_loop(0, BISECT_ROUNDS, bisect_round, (mn, mx, c_all))

    def min_ge_body(c, acc):
        blk = sc_ref[pl.ds(chunk_off(c), CK), :]
        return jnp.minimum(acc, _fold_rows(jnp.where(blk >= lo, blk, jnp.inf), jnp.minimum))

    cur0 = col_reduce(lax.fori_loop(0, nch, min_ge_body, jnp.full((SUBLANES, TQ), jnp.inf, F32)), jnp.min)

    def walk_cond(st):
        return st[3] > 0.0

    def walk_body(st):
        cur, c_ge, _, _ = st

        def body(c, carry):
            cnt, nxt = carry
            blk = sc_ref[pl.ds(chunk_off(c), CK), :]
            gt = blk > cur
            cnt = cnt + _fold_rows(jnp.where(gt, 1.0, 0.0), jnp.add)
            nxt = jnp.minimum(nxt, _fold_rows(jnp.where(gt, blk, jnp.inf), jnp.minimum))
            return cnt, nxt

        cnt, nxt = lax.fori_loop(0, nch, body,
                                 (jnp.zeros((SUBLANES, TQ), F32), jnp.full((SUBLANES, TQ), jnp.inf, F32)))
        c_gt = col_reduce(cnt, jnp.sum)
        nxt = col_reduce(nxt, jnp.min)
        adv = c_gt >= kf
        cur = jnp.where(adv, nxt, cur)
        c_ge = jnp.where(adv, c_gt, c_ge)
        return cur, c_ge, c_gt, jnp.max(jnp.where(adv, 1.0, 0.0))

    tau, c_ge, c_gt, _ = lax.while_loop(
        walk_cond, walk_body, (cur0, c_lo, jnp.zeros((1, TQ), F32), jnp.float32(1.0)))

    trim = c_ge > kf

    def mask_with_ties():
        room = kf - c_gt

        def mask_body(c, seen):
            off = chunk_off(c)
            for j in range(CK // LANES):
                rows = pl.ds(off + j * LANES, LANES)
                blk = sc_ref[rows, :]
                tie = blk == tau
                rank = seen + jnp.dot(tril_ref[...], jnp.where(tie, 1.0, 0.0).astype(BF16),
                                      preferred_element_type=F32)
                keep = (blk > tau) | (tie & (rank <= room))
                sc_ref[rows, :] = jnp.where(keep, 0.0, NEG)
                seen = rank[LANES - 1:LANES, :]
            return seen

        lax.fori_loop(0, nch, mask_body, jnp.zeros((1, TQ), F32))

    def mask_plain():
        def mask_body(c, _):
            off = chunk_off(c)
            sc_ref[pl.ds(off, CK), :] = jnp.where(sc_ref[pl.ds(off, CK), :] >= tau, 0.0, NEG)
            return 0

        lax.fori_loop(0, nch, mask_body, 0)

    lax.cond(jnp.max(jnp.where(trim, 1.0, 0.0)) > 0.0, mask_with_ties, mask_plain)

    m_ref[...] = jnp.full(m_ref.shape, NEG, F32)
    l_ref[...] = jnp.zeros(l_ref.shape, F32)
    for acc in acc_refs:
        acc[...] = jnp.zeros(acc.shape, F32)

    rowi = lax.broadcasted_iota(jnp.int32, (LANES, TQ), 0)
    for pair in range(ATTN_HEADS // 2):
        qp = qT_ref[pair * LANES:(pair + 1) * LANES, :]
        zero = jnp.zeros_like(qp)
        qm_ref[pair] = jnp.concatenate([jnp.where(rowi < ATTN_HEAD_DIM, qp, zero),
                                        jnp.where(rowi >= ATTN_HEAD_DIM, qp, zero)], axis=1)
    ones_rows = jnp.ones((2 * SUBLANES, CK), BF16)

    pq_col = posqc_ref[...]
    batch = pl.program_id(0)
    pq_first, pq_consec, pq_min = (pinfo_ref[batch, r, i] for r in (0, 1, 3))
    far_bias = [biasc_ref[0, h] for h in range(ATTN_HEADS)]
    zero_bias = [biasc_ref[1, h] for h in range(ATTN_HEADS)]
    q_ge_k = (lax.broadcasted_iota(jnp.int32, (LANES, TQ), 1) >= lax.broadcasted_iota(jnp.int32, (LANES, TQ), 0))
    n_sub = CK // LANES
    n_pairs = ATTN_HEADS // 2

    def chunk_is_far(c):
        return (pq_min - ckmax_ref[batch, jnp.minimum(c, seq // CK - 1)]) >= FAR_N

    def stage_bias(c):
        off = chunk_off(c)
        pk_row = posk_ref[:, pl.ds(off, CK)]
        for j in range(n_sub):
            rows = slice(j * LANES, (j + 1) * LANES)
            g = c * n_sub + j
            pk_first, pk_consec, pk_max = (pinfo_ref[batch, r, g] for r in (0, 1, 2))
            all_far = (pq_min - pk_max) >= FAR_N
            all_masked = (off + j * LANES) > (i * TQ + TQ - 1)
            consecutive = (pq_consec > 0) & (pk_consec > 0)
            gap = pq_first - pk_first

            def fill_const(rows=rows):
                for h in range(ATTN_HEADS):
                    b_ref[h, rows, :] = jnp.full((LANES, TQ), far_bias[h], F32)

            def fill_gap0(rows=rows):
                for h in range(ATTN_HEADS):
                    b_ref[h, rows, :] = jnp.where(q_ge_k, toep_ref[h], zero_bias[h])

            def fill_gap128(rows=rows):
                for h in range(ATTN_HEADS):
                    b_ref[h, rows, :] = jnp.where(q_ge_k, far_bias[h], toep_ref[h])

            def fill_lookup(rows=rows):
                pk_sub = pk_row[:, rows]
                n_qk = jnp.clip(pq_col - pk_sub, 0, BIAS_TABLE_N - 1).astype(F32)
                n_kq = n_qk.T.astype(jnp.int32)
                for h in range(ATTN_HEADS):
                    tb = jnp.broadcast_to(tbl_ref[h:h + 1, :], (LANES, BIAS_TABLE_N))
                    b_ref[h, rows, :] = jnp.take_along_axis(tb, n_kq, axis=1)

            def fill_near(fill_gap0=fill_gap0, fill_gap128=fill_gap128, fill_lookup=fill_lookup,
                          consecutive=consecutive, gap=gap):
                lax.cond(consecutive & (gap == 0), fill_gap0,
                         lambda: lax.cond(consecutive & (gap == LANES), fill_gap128, fill_lookup))

            lax.cond(all_far | all_masked, fill_const, fill_near)

    def logits_phase(c, s_buf, const_bias):
        off = chunk_off(c)
        mb = sc_ref[pl.ds(off, CK), :]
        m_cur = []
        for pair in range(n_pairs):
            kc = k_ref[pl.ds(off, CK), pair * LANES:(pair + 1) * LANES]
            s2 = jnp.dot(kc, qm_ref[pair], preferred_element_type=F32)
            for sub, h in enumerate((2 * pair, 2 * pair + 1)):
                s = s2[:, sub * TQ:(sub + 1) * TQ] + mb
                if not const_bias:
                    s = s + b_ref[h]
                s_buf[h] = s
                top = col_reduce(_fold_rows(s, jnp.maximum), jnp.max)
                m_cur.append(top + far_bias[h] if const_bias else top)
        return m_cur

    def update_phase(c, s_buf, m_cur, const_bias):
        off = chunk_off(c)
        m_prev = m_ref[...]
        l_prev = l_ref[...]
        if isinstance(m_cur, list):
            m_new = [jnp.maximum(m_prev[h:h + 1, :], m_cur[h]) for h in range(ATTN_HEADS)]
            alpha = [jnp.exp2(m_prev[h:h + 1, :] - m_new[h]) for h in range(ATTN_HEADS)]
        else:
            m_all = jnp.maximum(m_prev, m_cur)
            a_all = jnp.exp2(m_prev - m_all)
            m_new = [m_all[h:h + 1, :] for h in range(ATTN_HEADS)]
            alpha = [a_all[h:h + 1, :] for h in range(ATTN_HEADS)]
        l_new = []
        for pair in range(n_pairs):
            heads = (2 * pair, 2 * pair + 1)
            sub_m = [m_new[h] - far_bias[h] if const_bias else m_new[h] for h in heads]
            p2 = jnp.concatenate([jnp.exp2(s_buf[h] - sm).astype(BF16) for h, sm in zip(heads, sub_m)],
                                 axis=1)
            lhs = jnp.concatenate([vT_ref[pair * LANES:(pair + 1) * LANES, pl.ds(off, CK)], ones_rows],
                                  axis=0)
            out = jnp.dot(lhs, p2, preferred_element_type=F32)
            for sub, h in enumerate(heads):
                cols = slice(sub * TQ, (sub + 1) * TQ)
                acc = acc_refs[h]
                acc[...] = alpha[h] * acc[...] + out[sub * ATTN_HEAD_DIM:(sub + 1) * ATTN_HEAD_DIM, cols]
                l_new.append(alpha[h] * l_prev[h:h + 1, :] + out[LANES:LANES + 1, cols])
        l_ref[...] = jnp.concatenate(l_new, axis=0)
        m_ref[...] = jnp.concatenate(m_new, axis=0)

    n_far = lax.while_loop(lambda c: (c < nch) & chunk_is_far(c), lambda c: c + 1, jnp.int32(0))
    n_steps = n_far // 2
    def far_logits(c, s_buf):
        return jnp.concatenate(logits_phase(c, s_buf, True), axis=0)

    m_first = lax.cond(n_steps > 0, lambda: far_logits(0, s_ref), lambda: jnp.zeros((ATTN_HEADS, TQ), F32))

    def far_step(t, m_even):
        c = 2 * t
        m_odd = far_logits(c + 1, s2_ref)
        update_phase(c, s_ref, m_even, True)
        m_even = far_logits(jnp.minimum(c + 2, 2 * n_steps - 2), s_ref)
        update_phase(c + 1, s2_ref, m_odd, True)
        return m_even

    lax.fori_loop(0, n_steps, far_step, m_first)

    def tail_chunk(c, _):
        def run(const_bias):
            if not const_bias:
                stage_bias(c)
            update_phase(c, s_ref, logits_phase(c, s_ref, const_bias), const_bias)

        lax.cond(chunk_is_far(c), lambda: run(True), lambda: run(False))
        return 0

    lax.fori_loop(2 * n_steps, nch, tail_chunk, 0)

    outT = jnp.concatenate([acc_refs[h][...] / l_ref[h:h + 1, :] for h in range(ATTN_HEADS)], axis=0)
    o_ref[...] = outT.T.astype(o_ref.dtype)


def _toeplitz_kernel(tbl_ref, o_ref):
    q = lax.broadcasted_iota(jnp.int32, (LANES, LANES), 1)
    k = lax.broadcasted_iota(jnp.int32, (LANES, LANES), 0)
    idx = (q - k) & (BIAS_TABLE_N - 1)
    for h in range(ATTN_HEADS):
        tb = jnp.broadcast_to(tbl_ref[h:h + 1, :], (LANES, BIAS_TABLE_N))
        o_ref[h] = jnp.take_along_axis(tb, idx, axis=1)


def _sparse_attention(fm, tokb, ik, iwT, positions, rel_bias):
    B, S, _ = tokb.shape
    k_sel = min(TOPK_MAX, S // 4)
    bucket = _t5_bucket_table()
    tbl = rel_bias[bucket].T.astype(F32) * LOG2E
    bias_c = jnp.stack([tbl[:, BIAS_TABLE_N - 1], tbl[:, 0]])
    toep = pl.pallas_call(
        _toeplitz_kernel,
        out_shape=jax.ShapeDtypeStruct((ATTN_HEADS, LANES, LANES), F32),
        name="bias_toeplitz",
    )(tbl)
    pos_row = positions.reshape(B, 1, S)
    pos_col = positions.reshape(B, S, 1)
    pos_t = positions.reshape(B, S // LANES, LANES)
    consec = jnp.all(pos_t - pos_t[:, :, :1] == jnp.arange(LANES, dtype=positions.dtype), axis=-1)
    pinfo = jnp.stack([pos_t[:, :, 0], consec.astype(jnp.int32), jnp.max(pos_t, axis=-1),
                       jnp.min(pos_t, axis=-1)], axis=1).astype(jnp.int32)
    ck_max = jnp.max(positions.reshape(B, S // CK, CK), axis=-1)
    tril = jnp.tril(jnp.ones((LANES, LANES), BF16))
    sq = pl.Squeezed()
    iq_blk = (2 * ATTN_W) // IDX_Q_W
    k_blk = (tokb.shape[2] - ATTN_W) // ATTN_W
    return pl.pallas_call(
        functools.partial(_attn_kernel, k_sel=k_sel, seq=S),
        grid=(B, S // TQ),
        in_specs=[pl.BlockSpec(memory_space=pltpu.SMEM),
                  pl.BlockSpec(memory_space=pltpu.SMEM),
                  pl.BlockSpec(memory_space=pltpu.SMEM),
                  pl.BlockSpec((sq, IDX_Q_W, TQ), lambda b, i: (b, iq_blk, i)),
                  pl.BlockSpec((sq, SUBLANES, TQ), lambda b, i: (b, 0, i)),
                  pl.BlockSpec((sq, TQ, 1), lambda b, i: (b, i, 0)),
                  pl.BlockSpec((sq, 1, S), lambda b, i: (b, 0, 0)),
                  pl.BlockSpec((sq, S, IDX_HEAD_DIM), lambda b, i: (b, 0, 0)),
                  pl.BlockSpec((sq, ATTN_W, TQ), lambda b, i: (b, 0, i)),
                  pl.BlockSpec((sq, S, ATTN_W), lambda b, i: (b, 0, k_blk)),
                  pl.BlockSpec((sq, ATTN_W, S), lambda b, i: (b, 1, 0)),
                  pl.BlockSpec((ATTN_HEADS, BIAS_TABLE_N), lambda b, i: (0, 0)),
                  pl.BlockSpec((ATTN_HEADS, LANES, LANES), lambda b, i: (0, 0, 0)),
                  pl.BlockSpec((LANES, LANES), lambda b, i: (0, 0))],
        out_specs=pl.BlockSpec((sq, TQ, ATTN_W), lambda b, i: (b, i, 0)),
        out_shape=jax.ShapeDtypeStruct((B, S, ATTN_W), BF16),
        scratch_shapes=[pltpu.VMEM((S, TQ), F32),
                        pltpu.VMEM((ATTN_HEADS, CK, TQ), F32),
                        pltpu.VMEM((ATTN_HEADS, CK, TQ), F32),
                        pltpu.VMEM((ATTN_HEADS, CK, TQ), F32),
                        pltpu.VMEM((ATTN_HEADS // 2, LANES, 2 * TQ), BF16),
                        pltpu.VMEM((ATTN_HEADS, TQ), F32),
                        pltpu.VMEM((ATTN_HEADS, TQ), F32)]
                       + [pltpu.VMEM((ATTN_HEAD_DIM, TQ), F32)] * ATTN_HEADS,
        compiler_params=_cparams(2),
        name="sparse_attention",
    )(bias_c, pinfo, ck_max, fm, iwT, pos_col, pos_row, ik, fm, tokb, fm, tbl, toep, tril)


def _retention_kernel(q_ref, k_ref, v_ref, g_ref, decay_ref, xi_ref, zeta_ref, gch_ref, o_ref, r_ref):
    @pl.when(pl.program_id(0) == 0)
    def _():
        r_ref[...] = jnp.zeros(r_ref.shape, F32)

    C = RET_CHUNK
    lane = lax.broadcasted_iota(jnp.int32, (C, LANES), 1)
    row = lax.broadcasted_iota(jnp.int32, (LANES, RET_V_DIM), 0)
    for pair in range(RET_HEADS // 2):
        for b in range(q_ref.shape[0]):
            q_pair = q_ref[b, :, pair * LANES:(pair + 1) * LANES]
            k_pair = k_ref[b, :, pair * LANES:(pair + 1) * LANES]
            v_pair = v_ref[b, :, 2 * pair * RET_V_DIM:(2 * pair + 2) * RET_V_DIM]
            r_pair = r_ref[b, pair]
            r_bf = r_pair.astype(BF16)
            for sub in range(2):
                h = 2 * pair + sub
                in_head = (lane >= sub * RET_QK_DIM) & (lane < (sub + 1) * RET_QK_DIM)
                qm = jnp.where(in_head, q_pair, jnp.zeros_like(q_pair))
                v_h = v_pair[:, sub * RET_V_DIM:(sub + 1) * RET_V_DIM]
                inner = lax.dot_general(qm, k_pair, (((1,), (1,)), ((), ())),
                                        preferred_element_type=F32) * decay_ref[h]
                o = (jnp.dot(inner.astype(BF16), v_h, preferred_element_type=F32)
                     + jnp.dot(qm, r_bf, preferred_element_type=F32) * xi_ref[h])
                mu = jnp.mean(o, axis=-1, keepdims=True)
                d = o - mu
                var = jnp.mean(d * d, axis=-1, keepdims=True)
                hn = d * lax.rsqrt(var + LN_EPS)
                gate = g_ref[b, :, h * RET_V_DIM:(h + 1) * RET_V_DIM].astype(F32)
                o_ref[b, :, h * RET_V_DIM:(h + 1) * RET_V_DIM] = (gate * hn).astype(o_ref.dtype)
            kz = (k_pair.astype(F32) * zeta_ref[pair]).astype(BF16)
            upd = lax.dot_general(kz, v_pair, (((0,), (0,)), ((), ())), preferred_element_type=F32)
            r_ref[b, pair] = (r_pair * gch_ref[pair]
                              + jnp.where(row < RET_QK_DIM, upd[:, :RET_V_DIM], upd[:, RET_V_DIM:]))


def _retention(qk, tokb, gates, B, S):
    C = RET_CHUNK
    H = RET_HEADS
    nc = S // C
    gamma = 1.0 - 2.0 ** (-5.0 - jnp.arange(H, dtype=F32))
    log_g = jnp.log(gamma)
    n = jnp.arange(C, dtype=F32)
    diff = n[:, None] - n[None, :]
    decay_in = jnp.where(diff[None] >= 0, jnp.exp(log_g[:, None, None] * jnp.maximum(diff, 0.0)[None]), 0.0)
    xi = jnp.exp(log_g[None, :] * (n[:, None] + 1.0))
    zeta = jnp.exp(log_g[None, :] * (C - 1.0 - n[:, None]))
    g_chunk = jnp.exp(log_g * C)
    xi_b = jnp.broadcast_to(xi.T[:, :, None], (H, C, RET_V_DIM))
    zeta_b = jnp.repeat(zeta, RET_QK_DIM, axis=1).reshape(C, H // 2, LANES).transpose(1, 0, 2)
    gch_b = jnp.broadcast_to(jnp.repeat(g_chunk, RET_QK_DIM).reshape(H // 2, LANES, 1),
                             (H // 2, LANES, RET_V_DIM))
    qk3, tok3, gate3 = (a.reshape(B, S, a.shape[-1]) for a in (qk, tokb, gates))
    out = pl.pallas_call(
        _retention_kernel,
        grid=(nc,),
        in_specs=[pl.BlockSpec((B, C, RET_QK_W), lambda i: (0, i, 0)),
                  pl.BlockSpec((B, C, RET_QK_W), lambda i: (0, i, 1)),
                  pl.BlockSpec((B, C, RET_V_W), lambda i: (0, i, 0)),
                  pl.BlockSpec((B, C, RET_V_W), lambda i: (0, i, 0)),
                  pl.BlockSpec((H, C, C), lambda i: (0, 0, 0)),
                  pl.BlockSpec((H, C, RET_V_DIM), lambda i: (0, 0, 0)),
                  pl.BlockSpec((H // 2, C, LANES), lambda i: (0, 0, 0)),
                  pl.BlockSpec((H // 2, LANES, RET_V_DIM), lambda i: (0, 0, 0))],
        out_specs=pl.BlockSpec((B, C, RET_V_W), lambda i: (0, i, 0)),
        out_shape=jax.ShapeDtypeStruct((B, S, RET_V_W), BF16),
        scratch_shapes=[pltpu.VMEM((B, H // 2, LANES, RET_V_DIM), F32)],
        compiler_params=_cparams(1),
        name="retention",
    )(qk3, qk3, tok3, gate3, decay_in, xi_b, zeta_b, gch_b)
    return out.reshape(B * S, RET_V_W)


def _layer_norm(z, g, b):
    mu = jnp.mean(z, axis=-1, keepdims=True)
    d = z - mu
    var = jnp.mean(d * d, axis=-1, keepdims=True)
    return d * lax.rsqrt(var + LN_EPS) * g + b


def _merge_kernel(x_ref, ya_ref, yr_ref, ga_ref, gr_ref, wa_ref, wr_ref, wo_ref, g_ref, b_ref,
                  x1_ref, x1b_ref):
    a = jnp.dot(ya_ref[...], wa_ref[...], preferred_element_type=F32)
    r = jnp.dot(yr_ref[...], wr_ref[...], preferred_element_type=F32)
    h = ga_ref[...].astype(F32) * a + gr_ref[...].astype(F32) * r
    mix = jnp.dot(h.astype(BF16), wo_ref[...], preferred_element_type=F32)
    x1 = _layer_norm(DEEPNORM_ALPHA * x_ref[...] + mix, g_ref[...], b_ref[...])
    x1_ref[...] = x1
    x1b_ref[...] = x1.astype(BF16)


def _merge(x, ya, yr, gates, wa, wr, wo, g, b, tm=512):
    T, D = x.shape
    tm = min(tm, T)
    row = lambda i: (i, 0)
    fixed = lambda i: (0, 0)
    return pl.pallas_call(
        _merge_kernel,
        grid=(T // tm,),
        in_specs=[pl.BlockSpec((tm, D), row),
                  pl.BlockSpec((tm, ya.shape[1]), row),
                  pl.BlockSpec((tm, yr.shape[1]), row),
                  pl.BlockSpec((tm, D), lambda i: (i, 1)),
                  pl.BlockSpec((tm, D), lambda i: (i, 2)),
                  pl.BlockSpec(wa.shape, fixed),
                  pl.BlockSpec(wr.shape, fixed),
                  pl.BlockSpec(wo.shape, fixed),
                  pl.BlockSpec((1, D), fixed),
                  pl.BlockSpec((1, D), fixed)],
        out_specs=[pl.BlockSpec((tm, D), row), pl.BlockSpec((tm, D), row)],
        out_shape=[jax.ShapeDtypeStruct((T, D), F32), jax.ShapeDtypeStruct((T, D), BF16)],
        compiler_params=_cparams(1),
        name="merge",
    )(x, ya, yr, gates, gates, wa, wr, wo, g, b)


def _ffn_kernel(x1b_ref, x1_ref, wu_ref, wd_ref, g_ref, b_ref, o_ref, acc_ref):
    f = pl.program_id(1)

    @pl.when(f == 0)
    def _():
        acc_ref[...] = jnp.zeros(acc_ref.shape, F32)

    hid = jnp.maximum(jnp.dot(x1b_ref[...], wu_ref[...], preferred_element_type=F32), 0.0)
    acc_ref[...] += jnp.dot((hid * hid).astype(BF16), wd_ref[...], preferred_element_type=F32)

    @pl.when(f == pl.num_programs(1) - 1)
    def _():
        o_ref[...] = _layer_norm(DEEPNORM_ALPHA * x1_ref[...] + acc_ref[...], g_ref[...], b_ref[...])


def _ffn(x1b, x1, wu, wd, g, b, tm=1024, tf=1024):
    T, D = x1.shape
    F = wu.shape[1]
    tm = min(tm, T)
    return pl.pallas_call(
        _ffn_kernel,
        grid=(T // tm, F // tf),
        in_specs=[pl.BlockSpec((tm, D), lambda i, f: (i, 0)),
                  pl.BlockSpec((tm, D), lambda i, f: (i, 0)),
                  pl.BlockSpec((D, tf), lambda i, f: (0, f)),
                  pl.BlockSpec((tf, D), lambda i, f: (f, 0)),
                  pl.BlockSpec((1, D), lambda i, f: (0, 0)),
                  pl.BlockSpec((1, D), lambda i, f: (0, 0))],
        out_specs=pl.BlockSpec((tm, D), lambda i, f: (i, 0)),
        out_shape=jax.ShapeDtypeStruct((T, D), F32),
        scratch_shapes=[pltpu.VMEM((tm, D), F32)],
        compiler_params=_cparams(2),
        name="ffn",
    )(x1b, x1, wu, wd, g, b)


def _rot_half_weight(wT):
    N, D = wT.shape
    half = RET_QK_DIM // 2
    wh = wT.reshape(N // RET_QK_DIM, 2, half, D)
    return jnp.stack([-wh[:, 1], wh[:, 0]], axis=1).reshape(N, D)


def kernel(x, positions, w_in, rel_bias, idx_k_ln_g, idx_k_ln_b, w_attn_branch, w_ret_branch,
           w_out, ln_mix_g, ln_mix_b, w_up, w_down, ln_ffn_g, ln_ffn_b):
    B, S, D = x.shape
    T = B * S
    sizes = (ATTN_W, ATTN_W, ATTN_W, IDX_Q_W, IDX_HEAD_DIM, IDX_HEADS,
             RET_QK_W, RET_QK_W, RET_V_W, RET_V_W, D, D)
    offs = [0] + [int(o) for o in np.cumsum(sizes)]
    cos, sin = _rope_tables(positions)
    xf = x.reshape(T, D)
    for l in range(DEPTH):
        wT_f32 = jnp.swapaxes(w_in[l], 0, 1)
        wT = wT_f32.astype(BF16)
        rows = [wT[offs[k]:offs[k + 1]] for k in range(len(sizes))]
        (w_qa, w_ka, w_va, w_iq, w_ik, w_iw, w_qr, w_kr, w_vr, w_gr, w_ga, w_gtr) = rows
        xb = xf.astype(BF16)
        w_qa = (wT_f32[offs[0]:offs[1]] * (ATTN_HEAD_DIM ** -0.5 * LOG2E)).astype(BF16)
        w_kr = w_kr * (RET_QK_DIM ** -0.5)

        fm = _proj_t(xb, jnp.concatenate([w_qa, w_va, w_iq], axis=0), B, S, BF16, tn=1280)
        tokb = _proj(xb, jnp.concatenate([w_vr, w_ka], axis=0), BF16, tn=1536)
        gates = _proj_gates(xb, jnp.concatenate([w_gr, w_ga, w_gtr], axis=0), tn=D)
        w_rope = jnp.concatenate([w_qr, w_kr], axis=0)
        w_rope_rot = jnp.concatenate([_rot_half_weight(w_qr), _rot_half_weight(w_kr)], axis=0)
        qk_r = _proj_rope(xb, w_rope, w_rope_rot, cos, sin, tn=1024)
        pad = LANES - IDX_HEAD_DIM - IDX_HEADS
        w_idx = jnp.concatenate([w_ik, w_iw, jnp.zeros((pad, D), BF16)], axis=0)
        g_pad = jnp.concatenate([idx_k_ln_g[l], jnp.zeros((LANES - IDX_HEAD_DIM,), F32)]).reshape(1, LANES)
        b_pad = jnp.concatenate([idx_k_ln_b[l], jnp.zeros((LANES - IDX_HEAD_DIM,), F32)]).reshape(1, LANES)
        idx = _proj_idx(xb, w_idx, g_pad, b_pad,
                        (IDX_HEAD_DIM ** -0.5) * (IDX_HEADS ** -0.5)).reshape(B, S, LANES)
        ik = idx[:, :, :IDX_HEAD_DIM].astype(BF16)
        iwT = jnp.swapaxes(idx[:, :, IDX_HEAD_DIM:IDX_HEAD_DIM + SUBLANES], 1, 2)

        y_a = _sparse_attention(fm, tokb.reshape(B, S, -1), ik, iwT, positions, rel_bias)
        y_r = _retention(qk_r, tokb, gates, B, S)
        x1, x1b = _merge(xf, y_a.reshape(T, ATTN_W), y_r, gates,
                         w_attn_branch[l].astype(BF16), w_ret_branch[l].astype(BF16),
                         w_out[l].astype(BF16), ln_mix_g[l].reshape(1, D), ln_mix_b[l].reshape(1, D))
        xf = _ffn(x1b, x1, w_up[l].astype(BF16), w_down[l].astype(BF16),
                  ln_ffn_g[l].reshape(1, D), ln_ffn_b[l].reshape(1, D))
    return xf.reshape(B, S, D)
```

```python
import functools
import math

import numpy as np
import jax
import jax.numpy as jnp
from jax import lax
from jax.experimental import pallas as pl
from jax.experimental.pallas import tpu as pltpu

F32 = jnp.float32
BF16 = jnp.bfloat16

ATTN_HEADS = 8
ATTN_HEAD_DIM = 64
ATTN_W = ATTN_HEADS * ATTN_HEAD_DIM
IDX_HEADS = 4
IDX_HEAD_DIM = 64
IDX_Q_W = IDX_HEADS * IDX_HEAD_DIM
TOPK_MAX = 256
RET_HEADS = 8
RET_QK_DIM = 64
RET_V_DIM = 128
RET_QK_W = RET_HEADS * RET_QK_DIM
RET_V_W = RET_HEADS * RET_V_DIM
RET_CHUNK = 128
ROPE_BASE = 10000.0
NUM_BUCKETS = 32
MAX_DISTANCE = 128
LN_EPS = 1e-5
DEPTH = 1
DEEPNORM_ALPHA = (2.0 * DEPTH) ** 0.25

LANES = 128
SUBLANES = 8
VMEM_LIMIT = 56 * 1024 * 1024

TQ = 128
CK = 512
NEG = -1e30
LOG2E = math.log2(math.e)
BISECT_ROUNDS = 18
BIAS_TABLE_N = 128
FAR_N = 113


def _cparams(n_grid):
    return pltpu.CompilerParams(
        dimension_semantics=("arbitrary",) * n_grid,
        vmem_limit_bytes=VMEM_LIMIT)


def _trig_kernel(pos_ref, inv_ref, cos_ref, sin_ref):
    ang = pos_ref[...] * inv_ref[...]
    cos_ref[...] = jnp.cos(ang)
    sin_ref[...] = jnp.sin(ang)


def _rope_tables(positions):
    B, S = positions.shape
    half = RET_QK_DIM // 2
    inv = ROPE_BASE ** (-jnp.arange(half, dtype=F32) / half)
    per_row = LANES // half
    rows = B * S // per_row
    pos_e = jnp.repeat(positions.astype(F32).reshape(rows, per_row), half, axis=1)
    inv_e = jnp.tile(inv, per_row).reshape(1, LANES)
    tr = min(rows, 1024)
    cos, sin = pl.pallas_call(
        _trig_kernel,
        grid=(rows // tr,),
        in_specs=[pl.BlockSpec((tr, LANES), lambda i: (i, 0)),
                  pl.BlockSpec((1, LANES), lambda i: (0, 0))],
        out_specs=[pl.BlockSpec((tr, LANES), lambda i: (i, 0))] * 2,
        out_shape=[jax.ShapeDtypeStruct((rows, LANES), F32)] * 2,
        compiler_params=_cparams(1),
        name="rope_tables",
    )(pos_e, inv_e)
    cos = jnp.tile(cos.reshape(B * S, half), (1, per_row))
    sin = jnp.tile(sin.reshape(B * S, half), (1, per_row))
    return cos, sin


def _x_wt(x, wT):
    return lax.dot_general(x, wT, (((1,), (1,)), ((), ())), preferred_element_type=F32)


def _proj_kernel(x_ref, w_ref, o_ref):
    o_ref[...] = _x_wt(x_ref[...], w_ref[...]).astype(o_ref.dtype)


def _proj(xb, wT, out_dtype, tm=1024, tn=512):
    T, D = xb.shape
    N = wT.shape[0]
    tn = min(N, tn)
    tm = min(tm, T)
    return pl.pallas_call(
        _proj_kernel,
        grid=(T // tm, N // tn),
        in_specs=[pl.BlockSpec((tm, D), lambda i, j: (i, 0)),
                  pl.BlockSpec((tn, D), lambda i, j: (j, 0))],
        out_specs=pl.BlockSpec((tm, tn), lambda i, j: (i, j)),
        out_shape=jax.ShapeDtypeStruct((T, N), out_dtype),
        compiler_params=_cparams(2),
        name="proj",
    )(xb, wT)


def _proj_gates_kernel(x_ref, w_ref, o_ref):
    acc = _x_wt(x_ref[...], w_ref[...])
    sig = 0.5 * jnp.tanh(0.5 * acc) + 0.5
    o_ref[...] = jnp.where(pl.program_id(1) == 0, acc * sig, sig).astype(o_ref.dtype)


def _proj_gates(xb, wT, tn, tm=1024):
    T, D = xb.shape
    N = wT.shape[0]
    tm = min(tm, T)
    return pl.pallas_call(
        _proj_gates_kernel,
        grid=(T // tm, N // tn),
        in_specs=[pl.BlockSpec((tm, D), lambda i, j: (i, 0)),
                  pl.BlockSpec((tn, D), lambda i, j: (j, 0))],
        out_specs=pl.BlockSpec((tm, tn), lambda i, j: (i, j)),
        out_shape=jax.ShapeDtypeStruct((T, N), BF16),
        compiler_params=_cparams(2),
        name="proj_gates",
    )(xb, wT)


def _proj_t_kernel(wT_ref, x_ref, o_ref):
    acc = lax.dot_general(wT_ref[...], x_ref[...], (((1,), (1,)), ((), ())), preferred_element_type=F32)
    o_ref[...] = acc.astype(o_ref.dtype)


def _proj_t(xb, wT, B, S, out_dtype, tm=1024, tn=640):
    T, D = xb.shape
    N = wT.shape[0]
    tn = min(N, tn)
    tm = min(tm, S)
    nsb = S // tm
    return pl.pallas_call(
        _proj_t_kernel,
        grid=(T // tm, N // tn),
        in_specs=[pl.BlockSpec((tn, D), lambda i, j: (j, 0)),
                  pl.BlockSpec((tm, D), lambda i, j: (i, 0))],
        out_specs=pl.BlockSpec((pl.Squeezed(), tn, tm), lambda i, j: (i // nsb, j, i % nsb)),
        out_shape=jax.ShapeDtypeStruct((B, N, S), out_dtype),
        compiler_params=_cparams(2),
        name="proj_t",
    )(wT, xb)


def _proj_rope_kernel(x_ref, w_ref, wr_ref, cos_ref, sin_ref, o_ref):
    x = x_ref[...]
    a = _x_wt(x, w_ref[...])
    r = _x_wt(x, wr_ref[...])
    reps = a.shape[1] // LANES
    cos = jnp.concatenate([cos_ref[...]] * reps, axis=1)
    sin = jnp.concatenate([sin_ref[...]] * reps, axis=1)
    o_ref[...] = (a * cos + r * sin).astype(o_ref.dtype)


def _proj_rope(xb, wT, wT_rot, cos, sin, tm=1024, tn=512):
    T, D = xb.shape
    N = wT.shape[0]
    tm = min(tm, T)
    return pl.pallas_call(
        _proj_rope_kernel,
        grid=(T // tm, N // tn),
        in_specs=[pl.BlockSpec((tm, D), lambda i, j: (i, 0)),
                  pl.BlockSpec((tn, D), lambda i, j: (j, 0)),
                  pl.BlockSpec((tn, D), lambda i, j: (j, 0)),
                  pl.BlockSpec((tm, LANES), lambda i, j: (i, 0)),
                  pl.BlockSpec((tm, LANES), lambda i, j: (i, 0))],
        out_specs=pl.BlockSpec((tm, tn), lambda i, j: (i, j)),
        out_shape=jax.ShapeDtypeStruct((T, N), BF16),
        compiler_params=_cparams(2),
        name="proj_rope",
    )(xb, wT, wT_rot, cos, sin)


def _proj_idx_kernel(x_ref, w_ref, g_ref, b_ref, o_ref, *, iw_scale):
    acc = _x_wt(x_ref[...], w_ref[...])
    lane = lax.broadcasted_iota(jnp.int32, acc.shape, 1)
    is_k = lane < IDX_HEAD_DIM
    mu = jnp.sum(jnp.where(is_k, acc, 0.0), axis=-1, keepdims=True) / IDX_HEAD_DIM
    d = acc - mu
    var = jnp.sum(jnp.where(is_k, d * d, 0.0), axis=-1, keepdims=True) / IDX_HEAD_DIM
    ln = d * lax.rsqrt(var + LN_EPS) * g_ref[...] + b_ref[...]
    o_ref[...] = jnp.where(is_k, ln, acc * iw_scale)


def _proj_idx(xb, w_pad, g_pad, b_pad, iw_scale, tm=1024):
    T, D = xb.shape
    tm = min(tm, T)
    return pl.pallas_call(
        functools.partial(_proj_idx_kernel, iw_scale=iw_scale),
        grid=(T // tm,),
        in_specs=[pl.BlockSpec((tm, D), lambda i: (i, 0)),
                  pl.BlockSpec((LANES, D), lambda i: (0, 0)),
                  pl.BlockSpec((1, LANES), lambda i: (0, 0)),
                  pl.BlockSpec((1, LANES), lambda i: (0, 0))],
        out_specs=pl.BlockSpec((tm, LANES), lambda i: (i, 0)),
        out_shape=jax.ShapeDtypeStruct((T, LANES), F32),
        compiler_params=_cparams(1),
        name="proj_idx",
    )(xb, w_pad, g_pad, b_pad)


def _t5_bucket_table():
    n = np.arange(BIAS_TABLE_N)
    max_exact = NUM_BUCKETS // 2
    nf = np.maximum(n, 1).astype(np.float64)
    large = max_exact + (np.log(nf / max_exact) / math.log(MAX_DISTANCE / max_exact)
                         * (NUM_BUCKETS - max_exact)).astype(np.int64)
    large = np.minimum(large, NUM_BUCKETS - 1)
    bucket = np.where(n < max_exact, n, large)
    assert np.all(bucket[FAR_N:] == NUM_BUCKETS - 1) and bucket[FAR_N - 1] != NUM_BUCKETS - 1
    return bucket.astype(np.int32)


def _fold_rows(a, op):
    parts = [a[r:r + SUBLANES] for r in range(0, a.shape[0], SUBLANES)]
    while len(parts) > 1:
        nxt = [op(parts[k], parts[k + 1]) for k in range(0, len(parts) - 1, 2)]
        if len(parts) % 2:
            nxt.append(parts[-1])
        parts = nxt
    return parts[0]


def _attn_kernel(biasc_ref, pinfo_ref, ckmax_ref, iqT_ref, iwT_ref, posqc_ref, posk_ref, ik_ref, qT_ref, k_ref, vT_ref,
                 tbl_ref, toep_ref, tril_ref, o_ref, sc_ref, s_ref, s2_ref, b_ref, qm_ref, m_ref, l_ref, *acc_refs, k_sel, seq):
    i = pl.program_id(1)
    nch = (i * TQ + TQ + CK - 1) // CK
    q_idx = i * TQ + lax.broadcasted_iota(jnp.int32, (1, TQ), 1)
    kf = float(k_sel)

    def chunk_off(c):
        return pl.multiple_of(c * CK, CK)

    def key_idx(off):
        return off + lax.broadcasted_iota(jnp.int32, (CK, TQ), 0)

    def col_reduce(part, op):
        return op(part, axis=0, keepdims=True)

    iqT = iqT_ref[...]
    iwT = iwT_ref[...]
    iq_wide = jnp.concatenate([iqT[h * IDX_HEAD_DIM:(h + 1) * IDX_HEAD_DIM, :] for h in range(IDX_HEADS)], axis=1)

    def score_body(c, carry, masked):
        mn, mx = carry
        off = chunk_off(c)
        ikc = ik_ref[pl.ds(off, CK), :]
        z = jnp.dot(ikc, iq_wide, preferred_element_type=F32)
        s = None
        for h in range(IDX_HEADS):
            t = jnp.maximum(z[:, h * TQ:(h + 1) * TQ], 0.0) * iwT[h:h + 1, :]
            s = t if s is None else s + t
        if masked:
            causal = key_idx(off) <= q_idx
            s_lo = jnp.where(causal, s, -jnp.inf)
            s_hi = jnp.where(causal, s, jnp.inf)
        else:
            s_lo = s_hi = s
        sc_ref[pl.ds(off, CK), :] = s_lo
        mn = jnp.minimum(mn, _fold_rows(s_hi, jnp.minimum))
        mx = jnp.maximum(mx, _fold_rows(s_lo, jnp.maximum))
        return mn, mx

    mn8, mx8 = lax.fori_loop(0, nch - 1, functools.partial(score_body, masked=False),
                             (jnp.full((SUBLANES, TQ), jnp.inf, F32), jnp.full((SUBLANES, TQ), -jnp.inf, F32)))
    mn8, mx8 = score_body(nch - 1, (mn8, mx8), masked=True)
    mn = col_reduce(mn8, jnp.min)
    mx = col_reduce(mx8, jnp.max)

    def over_chunks(fn, init):
        return lax.fori_loop(0, nch, lambda c, carry: fn(sc_ref[pl.ds(chunk_off(c), CK), :], carry), init)

    def count(pred_fn):
        acc = over_chunks(lambda blk, acc: acc + _fold_rows(jnp.where(pred_fn(blk), 1.0, 0.0), jnp.add),
                          jnp.zeros((SUBLANES, TQ), F32))
        return col_reduce(acc, jnp.sum)

    def bisect_round(_, st):
        lo, hi, c_lo = st
        mid = 0.5 * (lo + hi)
        c = count(lambda blk: blk >= mid)
        ok = c >= kf
        return jnp.where(ok, mid, lo), jnp.where(ok, hi, mid), jnp.where(ok, c, c_lo)

    c_all = (q_idx + 1).astype(F32)
    lo, hi, c_lo = lax.fori_loop(0, BISECT_ROUNDS, bisect_round, (mn, mx, c_all))

    cur0 = col_reduce(over_chunks(
        lambda blk, acc: jnp.minimum(acc, _fold_rows(jnp.where(blk >= lo, blk, jnp.inf), jnp.minimum)),
        jnp.full((SUBLANES, TQ), jnp.inf, F32)), jnp.min)

    def walk_cond(st):
        return st[3] > 0.0

    def walk_body(st):
        cur, c_ge, _, _ = st

        def body(blk, carry):
            cnt, nxt = carry
            gt = blk > cur
            cnt = cnt + _fold_rows(jnp.where(gt, 1.0, 0.0), jnp.add)
            nxt = jnp.minimum(nxt, _fold_rows(jnp.where(gt, blk, jnp.inf), jnp.minimum))
            return cnt, nxt

        cnt, nxt = over_chunks(body, (jnp.zeros((SUBLANES, TQ), F32), jnp.full((SUBLANES, TQ), jnp.inf, F32)))
        c_gt = col_reduce(cnt, jnp.sum)
        nxt = col_reduce(nxt, jnp.min)
        adv = c_gt >= kf
        cur = jnp.where(adv, nxt, cur)
        c_ge = jnp.where(adv, c_gt, c_ge)
        return cur, c_ge, c_gt, jnp.max(jnp.where(adv, 1.0, 0.0))

    tau, c_ge, c_gt, _ = lax.while_loop(
        walk_cond, walk_body, (cur0, c_lo, jnp.zeros((1, TQ), F32), jnp.float32(1.0)))

    trim = c_ge > kf

    def mask_with_ties():
        room = kf - c_gt

        def mask_body(c, seen):
            off = chunk_off(c)
            for j in range(CK // LANES):
                rows = pl.ds(off + j * LANES, LANES)
                blk = sc_ref[rows, :]
                tie = blk == tau
                rank = seen + jnp.dot(tril_ref[...], jnp.where(tie, 1.0, 0.0).astype(BF16),
                                      preferred_element_type=F32)
                keep = (blk > tau) | (tie & (rank <= room))
                sc_ref[rows, :] = jnp.where(keep, 0.0, NEG)
                seen = rank[LANES - 1:LANES, :]
            return seen

        lax.fori_loop(0, nch, mask_body, jnp.zeros((1, TQ), F32))

    def mask_plain():
        def mask_body(c, _):
            off = chunk_off(c)
            sc_ref[pl.ds(off, CK), :] = jnp.where(sc_ref[pl.ds(off, CK), :] >= tau, 0.0, NEG)
            return 0

        lax.fori_loop(0, nch, mask_body, 0)

    lax.cond(jnp.max(jnp.where(trim, 1.0, 0.0)) > 0.0, mask_with_ties, mask_plain)

    m_ref[...] = jnp.full(m_ref.shape, NEG, F32)
    l_ref[...] = jnp.zeros(l_ref.shape, F32)
    for acc in acc_refs:
        acc[...] = jnp.zeros(acc.shape, F32)

    rowi = lax.broadcasted_iota(jnp.int32, (LANES, TQ), 0)
    for pair in range(ATTN_HEADS // 2):
        qp = qT_ref[pair * LANES:(pair + 1) * LANES, :]
        zero = jnp.zeros_like(qp)
        qm_ref[pair] = jnp.concatenate([jnp.where(rowi < ATTN_HEAD_DIM, qp, zero),
                                        jnp.where(rowi >= ATTN_HEAD_DIM, qp, zero)], axis=1)
    ones_rows = jnp.ones((2 * SUBLANES, CK), BF16)

    pq_col = posqc_ref[...]
    batch = pl.program_id(0)
    pq_first, pq_consec, pq_min = (pinfo_ref[batch, r, i] for r in (0, 1, 3))
    far_bias = [biasc_ref[0, h] for h in range(ATTN_HEADS)]
    zero_bias = [biasc_ref[1, h] for h in range(ATTN_HEADS)]
    q_ge_k = (lax.broadcasted_iota(jnp.int32, (LANES, TQ), 1) >= lax.broadcasted_iota(jnp.int32, (LANES, TQ), 0))
    n_sub = CK // LANES
    n_pairs = ATTN_HEADS // 2

    def chunk_is_far(c):
        return (pq_min - ckmax_ref[batch, jnp.minimum(c, seq // CK - 1)]) >= FAR_N

    def stage_bias(c):
        off = chunk_off(c)
        pk_row = posk_ref[:, pl.ds(off, CK)]
        for j in range(n_sub):
            rows = slice(j * LANES, (j + 1) * LANES)
            g = c * n_sub + j
            pk_first, pk_consec, pk_max = (pinfo_ref[batch, r, g] for r in (0, 1, 2))
            all_far = (pq_min - pk_max) >= FAR_N
            all_masked = (off + j * LANES) > (i * TQ + TQ - 1)
            consecutive = (pq_consec > 0) & (pk_consec > 0)
            gap = pq_first - pk_first

            def fill_const(rows=rows):
                for h in range(ATTN_HEADS):
                    b_ref[h, rows, :] = jnp.full((LANES, TQ), far_bias[h], F32)

            def fill_gap0(rows=rows):
                for h in range(ATTN_HEADS):
                    b_ref[h, rows, :] = jnp.where(q_ge_k, toep_ref[h], zero_bias[h])

            def fill_gap128(rows=rows):
                for h in range(ATTN_HEADS):
                    b_ref[h, rows, :] = jnp.where(q_ge_k, far_bias[h], toep_ref[h])

            def fill_lookup(rows=rows):
                pk_sub = pk_row[:, rows]
                n_qk = jnp.clip(pq_col - pk_sub, 0, BIAS_TABLE_N - 1).astype(F32)
                n_kq = n_qk.T.astype(jnp.int32)
                for h in range(ATTN_HEADS):
                    tb = jnp.broadcast_to(tbl_ref[h:h + 1, :], (LANES, BIAS_TABLE_N))
                    b_ref[h, rows, :] = jnp.take_along_axis(tb, n_kq, axis=1)

            def fill_near(fill_gap0=fill_gap0, fill_gap128=fill_gap128, fill_lookup=fill_lookup,
                          consecutive=consecutive, gap=gap):
                lax.cond(consecutive & (gap == 0), fill_gap0,
                         lambda: lax.cond(consecutive & (gap == LANES), fill_gap128, fill_lookup))

            lax.cond(all_far | all_masked, fill_const, fill_near)

    def logits_phase(c, s_buf, const_bias):
        off = chunk_off(c)
        mb = sc_ref[pl.ds(off, CK), :]
        m_cur = []
        for pair in range(n_pairs):
            kc = k_ref[pl.ds(off, CK), pair * LANES:(pair + 1) * LANES]
            s2 = jnp.dot(kc, qm_ref[pair], preferred_element_type=F32)
            for sub, h in enumerate((2 * pair, 2 * pair + 1)):
                s = s2[:, sub * TQ:(sub + 1) * TQ] + mb
                if not const_bias:
                    s = s + b_ref[h]
                s_buf[h] = s
                top = col_reduce(_fold_rows(s, jnp.maximum), jnp.max)
                m_cur.append(top + far_bias[h] if const_bias else top)
        return m_cur

    def update_phase(c, s_buf, m_cur, const_bias):
        off = chunk_off(c)
        m_prev = m_ref[...]
        l_prev = l_ref[...]
        if isinstance(m_cur, list):
            m_new = [jnp.maximum(m_prev[h:h + 1, :], m_cur[h]) for h in range(ATTN_HEADS)]
            alpha = [jnp.exp2(m_prev[h:h + 1, :] - m_new[h]) for h in range(ATTN_HEADS)]
        else:
            m_all = jnp.maximum(m_prev, m_cur)
            a_all = jnp.exp2(m_prev - m_all)
            m_new = [m_all[h:h + 1, :] for h in range(ATTN_HEADS)]
            alpha = [a_all[h:h + 1, :] for h in range(ATTN_HEADS)]
        l_new = []
        for pair in range(n_pairs):
            heads = (2 * pair, 2 * pair + 1)
            sub_m = [m_new[h] - far_bias[h] if const_bias else m_new[h] for h in heads]
            p2 = jnp.concatenate([jnp.exp2(s_buf[h] - sm).astype(BF16) for h, sm in zip(heads, sub_m)],
                                 axis=1)
            lhs = jnp.concatenate([vT_ref[pair * LANES:(pair + 1) * LANES, pl.ds(off, CK)], ones_rows],
                                  axis=0)
            out = jnp.dot(lhs, p2, preferred_element_type=F32)
            for sub, h in enumerate(heads):
                cols = slice(sub * TQ, (sub + 1) * TQ)
                acc = acc_refs[h]
                acc[...] = alpha[h] * acc[...] + out[sub * ATTN_HEAD_DIM:(sub + 1) * ATTN_HEAD_DIM, cols]
                l_new.append(alpha[h] * l_prev[h:h + 1, :] + out[LANES:LANES + 1, cols])
        l_ref[...] = jnp.concatenate(l_new, axis=0)
        m_ref[...] = jnp.concatenate(m_new, axis=0)

    n_far = lax.while_loop(lambda c: (c < nch) & chunk_is_far(c), lambda c: c + 1, jnp.int32(0))
    n_steps = n_far // 2
    def far_logits(c, s_buf):
        return jnp.concatenate(logits_phase(c, s_buf, True), axis=0)

    m_first = lax.cond(n_steps > 0, lambda: far_logits(0, s_ref), lambda: jnp.zeros((ATTN_HEADS, TQ), F32))

    def far_step(t, m_even):
        c = 2 * t
        m_odd = far_logits(c + 1, s2_ref)
        update_phase(c, s_ref, m_even, True)
        m_even = far_logits(jnp.minimum(c + 2, 2 * n_steps - 2), s_ref)
        update_phase(c + 1, s2_ref, m_odd, True)
        return m_even

    lax.fori_loop(0, n_steps, far_step, m_first)

    def tail_chunk(c, _):
        def run(const_bias):
            if not const_bias:
                stage_bias(c)
            update_phase(c, s_ref, logits_phase(c, s_ref, const_bias), const_bias)

        lax.cond(chunk_is_far(c), lambda: run(True), lambda: run(False))
        return 0

    lax.fori_loop(2 * n_steps, nch, tail_chunk, 0)

    outT = jnp.concatenate([acc_refs[h][...] / l_ref[h:h + 1, :] for h in range(ATTN_HEADS)], axis=0)
    o_ref[...] = outT.T.astype(o_ref.dtype)


def _toeplitz_kernel(tbl_ref, o_ref):
    q = lax.broadcasted_iota(jnp.int32, (LANES, LANES), 1)
    k = lax.broadcasted_iota(jnp.int32, (LANES, LANES), 0)
    idx = (q - k) & (BIAS_TABLE_N - 1)
    for h in range(ATTN_HEADS):
        tb = jnp.broadcast_to(tbl_ref[h:h + 1, :], (LANES, BIAS_TABLE_N))
        o_ref[h] = jnp.take_along_axis(tb, idx, axis=1)


def _sparse_attention(fm, tokb, ik, iwT, positions, rel_bias):
    B, S, _ = tokb.shape
    k_sel = min(TOPK_MAX, S // 4)
    bucket = _t5_bucket_table()
    tbl = rel_bias[bucket].T.astype(F32) * LOG2E
    bias_c = jnp.stack([tbl[:, BIAS_TABLE_N - 1], tbl[:, 0]])
    toep = pl.pallas_call(
        _toeplitz_kernel,
        out_shape=jax.ShapeDtypeStruct((ATTN_HEADS, LANES, LANES), F32),
        name="bias_toeplitz",
    )(tbl)
    pos_row = positions.reshape(B, 1, S)
    pos_col = positions.reshape(B, S, 1)
    pos_t = positions.reshape(B, S // LANES, LANES)
    consec = jnp.all(pos_t - pos_t[:, :, :1] == jnp.arange(LANES, dtype=positions.dtype), axis=-1)
    pinfo = jnp.stack([pos_t[:, :, 0], consec.astype(jnp.int32), jnp.max(pos_t, axis=-1),
                       jnp.min(pos_t, axis=-1)], axis=1).astype(jnp.int32)
    ck_max = jnp.max(positions.reshape(B, S // CK, CK), axis=-1)
    tril = jnp.tril(jnp.ones((LANES, LANES), BF16))
    sq = pl.Squeezed()
    iq_blk = (2 * ATTN_W) // IDX_Q_W
    k_blk = (tokb.shape[2] - ATTN_W) // ATTN_W
    return pl.pallas_call(
        functools.partial(_attn_kernel, k_sel=k_sel, seq=S),
        grid=(B, S // TQ),
        in_specs=[pl.BlockSpec(memory_space=pltpu.SMEM),
                  pl.BlockSpec(memory_space=pltpu.SMEM),
                  pl.BlockSpec(memory_space=pltpu.SMEM),
                  pl.BlockSpec((sq, IDX_Q_W, TQ), lambda b, i: (b, iq_blk, i)),
                  pl.BlockSpec((sq, SUBLANES, TQ), lambda b, i: (b, 0, i)),
                  pl.BlockSpec((sq, TQ, 1), lambda b, i: (b, i, 0)),
                  pl.BlockSpec((sq, 1, S), lambda b, i: (b, 0, 0)),
                  pl.BlockSpec((sq, S, IDX_HEAD_DIM), lambda b, i: (b, 0, 0)),
                  pl.BlockSpec((sq, ATTN_W, TQ), lambda b, i: (b, 0, i)),
                  pl.BlockSpec((sq, S, ATTN_W), lambda b, i: (b, 0, k_blk)),
                  pl.BlockSpec((sq, ATTN_W, S), lambda b, i: (b, 1, 0)),
                  pl.BlockSpec((ATTN_HEADS, BIAS_TABLE_N), lambda b, i: (0, 0)),
                  pl.BlockSpec((ATTN_HEADS, LANES, LANES), lambda b, i: (0, 0, 0)),
                  pl.BlockSpec((LANES, LANES), lambda b, i: (0, 0))],
        out_specs=pl.BlockSpec((sq, TQ, ATTN_W), lambda b, i: (b, i, 0)),
        out_shape=jax.ShapeDtypeStruct((B, S, ATTN_W), BF16),
        scratch_shapes=[pltpu.VMEM((S, TQ), F32),
                        pltpu.VMEM((ATTN_HEADS, CK, TQ), F32),
                        pltpu.VMEM((ATTN_HEADS, CK, TQ), F32),
                        pltpu.VMEM((ATTN_HEADS, CK, TQ), F32),
                        pltpu.VMEM((ATTN_HEADS // 2, LANES, 2 * TQ), BF16),
                        pltpu.VMEM((ATTN_HEADS, TQ), F32),
                        pltpu.VMEM((ATTN_HEADS, TQ), F32)]
                       + [pltpu.VMEM((ATTN_HEAD_DIM, TQ), F32)] * ATTN_HEADS,
        compiler_params=_cparams(2),
        name="sparse_attention",
    )(bias_c, pinfo, ck_max, fm, iwT, pos_col, pos_row, ik, fm, tokb, fm, tbl, toep, tril)


def _retention_kernel(q_ref, k_ref, v_ref, g_ref, decay_ref, xi_ref, zeta_ref, gch_ref, o_ref, r_ref):
    @pl.when(pl.program_id(0) == 0)
    def _():
        r_ref[...] = jnp.zeros(r_ref.shape, F32)

    C = RET_CHUNK
    lane = lax.broadcasted_iota(jnp.int32, (C, LANES), 1)
    row = lax.broadcasted_iota(jnp.int32, (LANES, RET_V_DIM), 0)
    for pair in range(RET_HEADS // 2):
        for b in range(q_ref.shape[0]):
            q_pair = q_ref[b, :, pair * LANES:(pair + 1) * LANES]
            k_pair = k_ref[b, :, pair * LANES:(pair + 1) * LANES]
            v_pair = v_ref[b, :, 2 * pair * RET_V_DIM:(2 * pair + 2) * RET_V_DIM]
            r_pair = r_ref[b, pair]
            r_bf = r_pair.astype(BF16)
            for sub in range(2):
                h = 2 * pair + sub
                in_head = (lane >= sub * RET_QK_DIM) & (lane < (sub + 1) * RET_QK_DIM)
                qm = jnp.where(in_head, q_pair, jnp.zeros_like(q_pair))
                v_h = v_pair[:, sub * RET_V_DIM:(sub + 1) * RET_V_DIM]
                inner = lax.dot_general(qm, k_pair, (((1,), (1,)), ((), ())),
                                        preferred_element_type=F32) * decay_ref[h]
                o = (jnp.dot(inner.astype(BF16), v_h, preferred_element_type=F32)
                     + jnp.dot(qm, r_bf, preferred_element_type=F32) * xi_ref[h])
                mu = jnp.mean(o, axis=-1, keepdims=True)
                d = o - mu
                var = jnp.mean(d * d, axis=-1, keepdims=True)
                hn = d * lax.rsqrt(var + LN_EPS)
                gate = g_ref[b, :, h * RET_V_DIM:(h + 1) * RET_V_DIM].astype(F32)
                o_ref[b, :, h * RET_V_DIM:(h + 1) * RET_V_DIM] = (gate * hn).astype(o_ref.dtype)
            kz = (k_pair.astype(F32) * zeta_ref[pair]).astype(BF16)
            upd = lax.dot_general(kz, v_pair, (((0,), (0,)), ((), ())), preferred_element_type=F32)
            r_ref[b, pair] = (r_pair * gch_ref[pair]
                              + jnp.where(row < RET_QK_DIM, upd[:, :RET_V_DIM], upd[:, RET_V_DIM:]))


def _retention(qk, tokb, gates, B, S):
    C = RET_CHUNK
    H = RET_HEADS
    nc = S // C
    gamma = 1.0 - 2.0 ** (-5.0 - jnp.arange(H, dtype=F32))
    log_g = jnp.log(gamma)
    n = jnp.arange(C, dtype=F32)
    diff = n[:, None] - n[None, :]
    decay_in = jnp.where(diff[None] >= 0, jnp.exp(log_g[:, None, None] * jnp.maximum(diff, 0.0)[None]), 0.0)
    xi = jnp.exp(log_g[None, :] * (n[:, None] + 1.0))
    zeta = jnp.exp(log_g[None, :] * (C - 1.0 - n[:, None]))
    g_chunk = jnp.exp(log_g * C)
    xi_b = jnp.broadcast_to(xi.T[:, :, None], (H, C, RET_V_DIM))
    zeta_b = jnp.repeat(zeta, RET_QK_DIM, axis=1).reshape(C, H // 2, LANES).transpose(1, 0, 2)
    gch_b = jnp.broadcast_to(jnp.repeat(g_chunk, RET_QK_DIM).reshape(H // 2, LANES, 1),
                             (H // 2, LANES, RET_V_DIM))
    qk3, tok3, gate3 = (a.reshape(B, S, a.shape[-1]) for a in (qk, tokb, gates))
    out = pl.pallas_call(
        _retention_kernel,
        grid=(nc,),
        in_specs=[pl.BlockSpec((B, C, RET_QK_W), lambda i: (0, i, 0)),
                  pl.BlockSpec((B, C, RET_QK_W), lambda i: (0, i, 1)),
                  pl.BlockSpec((B, C, RET_V_W), lambda i: (0, i, 0)),
                  pl.BlockSpec((B, C, RET_V_W), lambda i: (0, i, 0)),
                  pl.BlockSpec((H, C, C), lambda i: (0, 0, 0)),
                  pl.BlockSpec((H, C, RET_V_DIM), lambda i: (0, 0, 0)),
                  pl.BlockSpec((H // 2, C, LANES), lambda i: (0, 0, 0)),
                  pl.BlockSpec((H // 2, LANES, RET_V_DIM), lambda i: (0, 0, 0))],
        out_specs=pl.BlockSpec((B, C, RET_V_W), lambda i: (0, i, 0)),
        out_shape=jax.ShapeDtypeStruct((B, S, RET_V_W), BF16),
        scratch_shapes=[pltpu.VMEM((B, H // 2, LANES, RET_V_DIM), F32)],
        compiler_params=_cparams(1),
        name="retention",
    )(qk3, qk3, tok3, gate3, decay_in, xi_b, zeta_b, gch_b)
    return out.reshape(B * S, RET_V_W)


def _layer_norm(z, g, b):
    mu = jnp.mean(z, axis=-1, keepdims=True)
    d = z - mu
    var = jnp.mean(d * d, axis=-1, keepdims=True)
    return d * lax.rsqrt(var + LN_EPS) * g + b


def _merge_kernel(x_ref, ya_ref, yr_ref, ga_ref, gr_ref, wa_ref, wr_ref, wo_ref, g_ref, b_ref,
                  x1_ref, x1b_ref):
    a = jnp.dot(ya_ref[...], wa_ref[...], preferred_element_type=F32)
    r = jnp.dot(yr_ref[...], wr_ref[...], preferred_element_type=F32)
    h = ga_ref[...].astype(F32) * a + gr_ref[...].astype(F32) * r
    mix = jnp.dot(h.astype(BF16), wo_ref[...], preferred_element_type=F32)
    x1 = _layer_norm(DEEPNORM_ALPHA * x_ref[...] + mix, g_ref[...], b_ref[...])
    x1_ref[...] = x1
    x1b_ref[...] = x1.astype(BF16)


def _merge(x, ya, yr, gates, wa, wr, wo, g, b, tm=512):
    T, D = x.shape
    tm = min(tm, T)
    row = lambda i: (i, 0)
    fixed = lambda i: (0, 0)
    return pl.pallas_call(
        _merge_kernel,
        grid=(T // tm,),
        in_specs=[pl.BlockSpec((tm, D), row),
                  pl.BlockSpec((tm, ya.shape[1]), row),
                  pl.BlockSpec((tm, yr.shape[1]), row),
                  pl.BlockSpec((tm, D), lambda i: (i, 1)),
                  pl.BlockSpec((tm, D), lambda i: (i, 2)),
                  pl.BlockSpec(wa.shape, fixed),
                  pl.BlockSpec(wr.shape, fixed),
                  pl.BlockSpec(wo.shape, fixed),
                  pl.BlockSpec((1, D), fixed),
                  pl.BlockSpec((1, D), fixed)],
        out_specs=[pl.BlockSpec((tm, D), row), pl.BlockSpec((tm, D), row)],
        out_shape=[jax.ShapeDtypeStruct((T, D), F32), jax.ShapeDtypeStruct((T, D), BF16)],
        compiler_params=_cparams(1),
        name="merge",
    )(x, ya, yr, gates, gates, wa, wr, wo, g, b)


def _ffn_kernel(x1b_ref, x1_ref, wu_ref, wd_ref, g_ref, b_ref, o_ref, acc_ref):
    f = pl.program_id(1)

    @pl.when(f == 0)
    def _():
        acc_ref[...] = jnp.zeros(acc_ref.shape, F32)

    hid = jnp.maximum(jnp.dot(x1b_ref[...], wu_ref[...], preferred_element_type=F32), 0.0)
    acc_ref[...] += jnp.dot((hid * hid).astype(BF16), wd_ref[...], preferred_element_type=F32)

    @pl.when(f == pl.num_programs(1) - 1)
    def _():
        o_ref[...] = _layer_norm(DEEPNORM_ALPHA * x1_ref[...] + acc_ref[...], g_ref[...], b_ref[...])


def _ffn(x1b, x1, wu, wd, g, b, tm=1024, tf=1024):
    T, D = x1.shape
    F = wu.shape[1]
    tm = min(tm, T)
    return pl.pallas_call(
        _ffn_kernel,
        grid=(T // tm, F // tf),
        in_specs=[pl.BlockSpec((tm, D), lambda i, f: (i, 0)),
                  pl.BlockSpec((tm, D), lambda i, f: (i, 0)),
                  pl.BlockSpec((D, tf), lambda i, f: (0, f)),
                  pl.BlockSpec((tf, D), lambda i, f: (f, 0)),
                  pl.BlockSpec((1, D), lambda i, f: (0, 0)),
                  pl.BlockSpec((1, D), lambda i, f: (0, 0))],
        out_specs=pl.BlockSpec((tm, D), lambda i, f: (i, 0)),
        out_shape=jax.ShapeDtypeStruct((T, D), F32),
        scratch_shapes=[pltpu.VMEM((tm, D), F32)],
        compiler_params=_cparams(2),
        name="ffn",
    )(x1b, x1, wu, wd, g, b)


def _rot_half_weight(wT):
    N, D = wT.shape
    half = RET_QK_DIM // 2
    wh = wT.reshape(N // RET_QK_DIM, 2, half, D)
    return jnp.stack([-wh[:, 1], wh[:, 0]], axis=1).reshape(N, D)


def kernel(x, positions, w_in, rel_bias, idx_k_ln_g, idx_k_ln_b, w_attn_branch, w_ret_branch,
           w_out, ln_mix_g, ln_mix_b, w_up, w_down, ln_ffn_g, ln_ffn_b):
    B, S, D = x.shape
    T = B * S
    sizes = (ATTN_W, ATTN_W, ATTN_W, IDX_Q_W, IDX_HEAD_DIM, IDX_HEADS,
             RET_QK_W, RET_QK_W, RET_V_W, RET_V_W, D, D)
    offs = [0] + [int(o) for o in np.cumsum(sizes)]
    cos, sin = _rope_tables(positions)
    xf = x.reshape(T, D)
    for l in range(DEPTH):
        wT_f32 = jnp.swapaxes(w_in[l], 0, 1)
        wT = wT_f32.astype(BF16)
        rows = [wT[offs[k]:offs[k + 1]] for k in range(len(sizes))]
        (w_qa, w_ka, w_va, w_iq, w_ik, w_iw, w_qr, w_kr, w_vr, w_gr, w_ga, w_gtr) = rows
        xb = xf.astype(BF16)
        w_qa = (wT_f32[offs[0]:offs[1]] * (ATTN_HEAD_DIM ** -0.5 * LOG2E)).astype(BF16)
        w_kr = w_kr * (RET_QK_DIM ** -0.5)

        fm = _proj_t(xb, jnp.concatenate([w_qa, w_va, w_iq], axis=0), B, S, BF16, tn=1280)
        tokb = _proj(xb, jnp.concatenate([w_vr, w_ka], axis=0), BF16, tn=1536)
        gates = _proj_gates(xb, jnp.concatenate([w_gr, w_ga, w_gtr], axis=0), tn=D)
        w_rope = jnp.concatenate([w_qr, w_kr], axis=0)
        w_rope_rot = jnp.concatenate([_rot_half_weight(w_qr), _rot_half_weight(w_kr)], axis=0)
        qk_r = _proj_rope(xb, w_rope, w_rope_rot, cos, sin, tn=1024)
        pad = LANES - IDX_HEAD_DIM - IDX_HEADS
        w_idx = jnp.concatenate([w_ik, w_iw, jnp.zeros((pad, D), BF16)], axis=0)
        g_pad = jnp.concatenate([idx_k_ln_g[l], jnp.zeros((LANES - IDX_HEAD_DIM,), F32)]).reshape(1, LANES)
        b_pad = jnp.concatenate([idx_k_ln_b[l], jnp.zeros((LANES - IDX_HEAD_DIM,), F32)]).reshape(1, LANES)
        idx = _proj_idx(xb, w_idx, g_pad, b_pad,
                        (IDX_HEAD_DIM ** -0.5) * (IDX_HEADS ** -0.5)).reshape(B, S, LANES)
        ik = idx[:, :, :IDX_HEAD_DIM].astype(BF16)
        iwT = jnp.swapaxes(idx[:, :, IDX_HEAD_DIM:IDX_HEAD_DIM + SUBLANES], 1, 2)

        y_a = _sparse_attention(fm, tokb.reshape(B, S, -1), ik, iwT, positions, rel_bias)
        y_r = _retention(qk_r, tokb, gates, B, S)
        x1, x1b = _merge(xf, y_a.reshape(T, ATTN_W), y_r, gates,
                         w_attn_branch[l].astype(BF16), w_ret_branch[l].astype(BF16),
                         w_out[l].astype(BF16), ln_mix_g[l].reshape(1, D), ln_mix_b[l].reshape(1, D))
        xf = _ffn(x1b, x1, w_up[l].astype(BF16), w_down[l].astype(BF16),
                  ln_ffn_g[l].reshape(1, D), ln_ffn_b[l].reshape(1, D))
    return xf.reshape(B, S, D)
```

```python
import functools
import math

import numpy as np
import jax
import jax.numpy as jnp
from jax import lax
from jax.experimental import pallas as pl
from jax.experimental.pallas import tpu as pltpu

F32 = jnp.float32
BF16 = jnp.bfloat16

ATTN_HEADS = 8
ATTN_HEAD_DIM = 64
ATTN_W = ATTN_HEADS * ATTN_HEAD_DIM
IDX_HEADS = 4
IDX_HEAD_DIM = 64
IDX_Q_W = IDX_HEADS * IDX_HEAD_DIM
TOPK_MAX = 256
RET_HEADS = 8
RET_QK_DIM = 64
RET_V_DIM = 128
RET_QK_W = RET_HEADS * RET_QK_DIM
RET_V_W = RET_HEADS * RET_V_DIM
RET_CHUNK = 128
ROPE_BASE = 10000.0
NUM_BUCKETS = 32
MAX_DISTANCE = 128
LN_EPS = 1e-5
DEPTH = 1
DEEPNORM_ALPHA = (2.0 * DEPTH) ** 0.25

LANES = 128
SUBLANES = 8
VMEM_LIMIT = 56 * 1024 * 1024

TQ = 128
CK = 512
NEG = -1e30
LOG2E = math.log2(math.e)
BISECT_ROUNDS = 16
BIAS_TABLE_N = 128
FAR_N = 113


def _cparams(n_grid):
    return pltpu.CompilerParams(
        dimension_semantics=("arbitrary",) * n_grid,
        vmem_limit_bytes=VMEM_LIMIT)


def _trig_kernel(pos_ref, inv_ref, cos_ref, sin_ref):
    ang = pos_ref[...] * inv_ref[...]
    cos_ref[...] = jnp.cos(ang)
    sin_ref[...] = jnp.sin(ang)


def _rope_tables(positions):
    B, S = positions.shape
    half = RET_QK_DIM // 2
    inv = ROPE_BASE ** (-jnp.arange(half, dtype=F32) / half)
    per_row = LANES // half
    rows = B * S // per_row
    pos_e = jnp.repeat(positions.astype(F32).reshape(rows, per_row), half, axis=1)
    inv_e = jnp.tile(inv, per_row).reshape(1, LANES)
    tr = min(rows, 1024)
    cos, sin = pl.pallas_call(
        _trig_kernel,
        grid=(rows // tr,),
        in_specs=[pl.BlockSpec((tr, LANES), lambda i: (i, 0)),
                  pl.BlockSpec((1, LANES), lambda i: (0, 0))],
        out_specs=[pl.BlockSpec((tr, LANES), lambda i: (i, 0))] * 2,
        out_shape=[jax.ShapeDtypeStruct((rows, LANES), F32)] * 2,
        compiler_params=_cparams(1),
        name="rope_tables",
    )(pos_e, inv_e)
    cos = jnp.tile(cos.reshape(B * S, half), (1, per_row))
    sin = jnp.tile(sin.reshape(B * S, half), (1, per_row))
    return cos, sin


def _x_wt(x, wT):
    return lax.dot_general(x, wT, (((1,), (1,)), ((), ())), preferred_element_type=F32)


def _proj_kernel(x_ref, w_ref, o_ref):
    o_ref[...] = _x_wt(x_ref[...], w_ref[...]).astype(o_ref.dtype)


def _proj(xb, wT, out_dtype, tm=1024, tn=512):
    T, D = xb.shape
    N = wT.shape[0]
    tn = min(N, tn)
    tm = min(tm, T)
    return pl.pallas_call(
        _proj_kernel,
        grid=(T // tm, N // tn),
        in_specs=[pl.BlockSpec((tm, D), lambda i, j: (i, 0)),
                  pl.BlockSpec((tn, D), lambda i, j: (j, 0))],
        out_specs=pl.BlockSpec((tm, tn), lambda i, j: (i, j)),
        out_shape=jax.ShapeDtypeStruct((T, N), out_dtype),
        compiler_params=_cparams(2),
        name="proj",
    )(xb, wT)


def _proj_gates_kernel(x_ref, w_ref, o_ref):
    acc = _x_wt(x_ref[...], w_ref[...])
    sig = 0.5 * jnp.tanh(0.5 * acc) + 0.5
    o_ref[...] = jnp.where(pl.program_id(1) == 0, acc * sig, sig).astype(o_ref.dtype)


def _proj_gates(xb, wT, tn, tm=1024):
    T, D = xb.shape
    N = wT.shape[0]
    tm = min(tm, T)
    return pl.pallas_call(
        _proj_gates_kernel,
        grid=(T // tm, N // tn),
        in_specs=[pl.BlockSpec((tm, D), lambda i, j: (i, 0)),
                  pl.BlockSpec((tn, D), lambda i, j: (j, 0))],
        out_specs=pl.BlockSpec((tm, tn), lambda i, j: (i, j)),
        out_shape=jax.ShapeDtypeStruct((T, N), BF16),
        compiler_params=_cparams(2),
        name="proj_gates",
    )(xb, wT)


def _proj_t_kernel(wT_ref, x_ref, o_ref):
    acc = lax.dot_general(wT_ref[...], x_ref[...], (((1,), (1,)), ((), ())), preferred_element_type=F32)
    o_ref[...] = acc.astype(o_ref.dtype)


def _proj_t(xb, wT, B, S, out_dtype, tm=1024, tn=640):
    T, D = xb.shape
    N = wT.shape[0]
    tn = min(N, tn)
    tm = min(tm, S)
    nsb = S // tm
    return pl.pallas_call(
        _proj_t_kernel,
        grid=(T // tm, N // tn),
        in_specs=[pl.BlockSpec((tn, D), lambda i, j: (j, 0)),
                  pl.BlockSpec((tm, D), lambda i, j: (i, 0))],
        out_specs=pl.BlockSpec((pl.Squeezed(), tn, tm), lambda i, j: (i // nsb, j, i % nsb)),
        out_shape=jax.ShapeDtypeStruct((B, N, S), out_dtype),
        compiler_params=_cparams(2),
        name="proj_t",
    )(wT, xb)


def _proj_rope_kernel(x_ref, w_ref, wr_ref, cos_ref, sin_ref, o_ref):
    x = x_ref[...]
    a = _x_wt(x, w_ref[...])
    r = _x_wt(x, wr_ref[...])
    reps = a.shape[1] // LANES
    cos = jnp.concatenate([cos_ref[...]] * reps, axis=1)
    sin = jnp.concatenate([sin_ref[...]] * reps, axis=1)
    o_ref[...] = (a * cos + r * sin).astype(o_ref.dtype)


def _proj_rope(xb, wT, wT_rot, cos, sin, tm=1024, tn=512):
    T, D = xb.shape
    N = wT.shape[0]
    tm = min(tm, T)
    return pl.pallas_call(
        _proj_rope_kernel,
        grid=(T // tm, N // tn),
        in_specs=[pl.BlockSpec((tm, D), lambda i, j: (i, 0)),
                  pl.BlockSpec((tn, D), lambda i, j: (j, 0)),
                  pl.BlockSpec((tn, D), lambda i, j: (j, 0)),
                  pl.BlockSpec((tm, LANES), lambda i, j: (i, 0)),
                  pl.BlockSpec((tm, LANES), lambda i, j: (i, 0))],
        out_specs=pl.BlockSpec((tm, tn), lambda i, j: (i, j)),
        out_shape=jax.ShapeDtypeStruct((T, N), BF16),
        compiler_params=_cparams(2),
        name="proj_rope",
    )(xb, wT, wT_rot, cos, sin)


def _proj_idx_kernel(x_ref, w_ref, g_ref, b_ref, o_ref, *, iw_scale):
    acc = _x_wt(x_ref[...], w_ref[...])
    lane = lax.broadcasted_iota(jnp.int32, acc.shape, 1)
    is_k = lane < IDX_HEAD_DIM
    mu = jnp.sum(jnp.where(is_k, acc, 0.0), axis=-1, keepdims=True) / IDX_HEAD_DIM
    d = acc - mu
    var = jnp.sum(jnp.where(is_k, d * d, 0.0), axis=-1, keepdims=True) / IDX_HEAD_DIM
    ln = d * lax.rsqrt(var + LN_EPS) * g_ref[...] + b_ref[...]
    o_ref[...] = jnp.where(is_k, ln, acc * iw_scale)


def _proj_idx(xb, w_pad, g_pad, b_pad, iw_scale, tm=1024):
    T, D = xb.shape
    tm = min(tm, T)
    return pl.pallas_call(
        functools.partial(_proj_idx_kernel, iw_scale=iw_scale),
        grid=(T // tm,),
        in_specs=[pl.BlockSpec((tm, D), lambda i: (i, 0)),
                  pl.BlockSpec((LANES, D), lambda i: (0, 0)),
                  pl.BlockSpec((1, LANES), lambda i: (0, 0)),
                  pl.BlockSpec((1, LANES), lambda i: (0, 0))],
        out_specs=pl.BlockSpec((tm, LANES), lambda i: (i, 0)),
        out_shape=jax.ShapeDtypeStruct((T, LANES), F32),
        compiler_params=_cparams(1),
        name="proj_idx",
    )(xb, w_pad, g_pad, b_pad)


def _t5_bucket_table():
    n = np.arange(BIAS_TABLE_N)
    max_exact = NUM_BUCKETS // 2
    nf = np.maximum(n, 1).astype(np.float64)
    large = max_exact + (np.log(nf / max_exact) / math.log(MAX_DISTANCE / max_exact)
                         * (NUM_BUCKETS - max_exact)).astype(np.int64)
    large = np.minimum(large, NUM_BUCKETS - 1)
    bucket = np.where(n < max_exact, n, large)
    assert np.all(bucket[FAR_N:] == NUM_BUCKETS - 1) and bucket[FAR_N - 1] != NUM_BUCKETS - 1
    return bucket.astype(np.int32)


def _fold_rows(a, op):
    parts = [a[r:r + SUBLANES] for r in range(0, a.shape[0], SUBLANES)]
    while len(parts) > 1:
        nxt = [op(parts[k], parts[k + 1]) for k in range(0, len(parts) - 1, 2)]
        if len(parts) % 2:
            nxt.append(parts[-1])
        parts = nxt
    return parts[0]


def _attn_kernel(biasc_ref, pinfo_ref, ckmax_ref, iqT_ref, iwT_ref, posqc_ref, posk_ref, ik_ref, qT_ref, k_ref, vT_ref,
                 tbl_ref, toep_ref, tril_ref, o_ref, sc_ref, s_ref, s2_ref, b_ref, qm_ref, m_ref, l_ref, *acc_refs, k_sel, seq):
    i = pl.program_id(1)
    nch = (i * TQ + TQ + CK - 1) // CK
    q_idx = i * TQ + lax.broadcasted_iota(jnp.int32, (1, TQ), 1)
    kf = float(k_sel)

    def chunk_off(c):
        return pl.multiple_of(c * CK, CK)

    def key_idx(off):
        return off + lax.broadcasted_iota(jnp.int32, (CK, TQ), 0)

    def col_reduce(part, op):
        return op(part, axis=0, keepdims=True)

    iqT = iqT_ref[...]
    iwT = iwT_ref[...]
    iq_wide = jnp.concatenate([iqT[h * IDX_HEAD_DIM:(h + 1) * IDX_HEAD_DIM, :] for h in range(IDX_HEADS)], axis=1)

    def score_body(c, carry, masked):
        mn, mx = carry
        off = chunk_off(c)
        ikc = ik_ref[pl.ds(off, CK), :]
        z = jnp.dot(ikc, iq_wide, preferred_element_type=F32)
        s = None
        for h in range(IDX_HEADS):
            t = jnp.maximum(z[:, h * TQ:(h + 1) * TQ], 0.0) * iwT[h:h + 1, :]
            s = t if s is None else s + t
        if masked:
            causal = key_idx(off) <= q_idx
            s_lo = jnp.where(causal, s, -jnp.inf)
            s_hi = jnp.where(causal, s, jnp.inf)
        else:
            s_lo = s_hi = s
        sc_ref[pl.ds(off, CK), :] = s_lo
        mn = jnp.minimum(mn, _fold_rows(s_hi, jnp.minimum))
        mx = jnp.maximum(mx, _fold_rows(s_lo, jnp.maximum))
        return mn, mx

    mn8, mx8 = lax.fori_loop(0, nch - 1, functools.partial(score_body, masked=False),
                             (jnp.full((SUBLANES, TQ), jnp.inf, F32), jnp.full((SUBLANES, TQ), -jnp.inf, F32)))
    mn8, mx8 = score_body(nch - 1, (mn8, mx8), masked=True)
    mn = col_reduce(mn8, jnp.min)
    mx = col_reduce(mx8, jnp.max)

    def over_chunks(fn, init):
        return lax.fori_loop(0, nch, lambda c, carry: fn(sc_ref[pl.ds(chunk_off(c), CK), :], carry), init)

    def count(pred_fn):
        acc = over_chunks(lambda blk, acc: acc + _fold_rows(jnp.where(pred_fn(blk), 1.0, 0.0), jnp.add),
                          jnp.zeros((SUBLANES, TQ), F32))
        return col_reduce(acc, jnp.sum)

    def bisect_round(_, st):
        lo, hi, c_lo = st
        mid = 0.5 * (lo + hi)
        c = count(lambda blk: blk >= mid)
        ok = c >= kf
        return jnp.where(ok, mid, lo), jnp.where(ok, hi, mid), jnp.where(ok, c, c_lo)

    c_all = (q_idx + 1).astype(F32)
    lo, hi, c_lo = lax.fori_loop(0, BISECT_ROUNDS, bisect_round, (mn, mx, c_all))

    cur0 = col_reduce(over_chunks(
        lambda blk, acc: jnp.minimum(acc, _fold_rows(jnp.where(blk >= lo, blk, jnp.inf), jnp.minimum)),
        jnp.full((SUBLANES, TQ), jnp.inf, F32)), jnp.min)

    def walk_cond(st):
        return st[3] > 0.0

    def walk_body(st):
        cur, c_ge, _, _ = st

        def body(blk, carry):
            cnt, nxt = carry
            gt = blk > cur
            cnt = cnt + _fold_rows(jnp.where(gt, 1.0, 0.0), jnp.add)
            nxt = jnp.minimum(nxt, _fold_rows(jnp.where(gt, blk, jnp.inf), jnp.minimum))
            return cnt, nxt

        cnt, nxt = over_chunks(body, (jnp.zeros((SUBLANES, TQ), F32), jnp.full((SUBLANES, TQ), jnp.inf, F32)))
        c_gt = col_reduce(cnt, jnp.sum)
        nxt = col_reduce(nxt, jnp.min)
        adv = c_gt >= kf
        cur = jnp.where(adv, nxt, cur)
        c_ge = jnp.where(adv, c_gt, c_ge)
        return cur, c_ge, c_gt, jnp.max(jnp.where(adv, 1.0, 0.0))

    tau, c_ge, c_gt, _ = lax.while_loop(
        walk_cond, walk_body, (cur0, c_lo, jnp.zeros((1, TQ), F32), jnp.float32(1.0)))

    room = kf - c_gt

    def selection_mask(c, seen):
        off = chunk_off(c)
        tiles = range(CK // LANES)
        blks = [sc_ref[pl.ds(off + j * LANES, LANES), :] for j in tiles]
        ties = [blk == tau for blk in blks]
        local = [jnp.dot(tril_ref[...], jnp.where(tie, 1.0, 0.0).astype(BF16), preferred_element_type=F32)
                 for tie in ties]
        masks = []
        for j in tiles:
            keep = (blks[j] > tau) | (ties[j] & (local[j] + seen <= room))
            masks.append(jnp.where(keep, 0.0, NEG))
            seen = seen + local[j][LANES - 1:LANES, :]
        return jnp.concatenate(masks, axis=0), seen

    m_ref[...] = jnp.full(m_ref.shape, NEG, F32)
    l_ref[...] = jnp.zeros(l_ref.shape, F32)
    for acc in acc_refs:
        acc[...] = jnp.zeros(acc.shape, F32)

    rowi = lax.broadcasted_iota(jnp.int32, (LANES, TQ), 0)
    for pair in range(ATTN_HEADS // 2):
        qp = qT_ref[pair * LANES:(pair + 1) * LANES, :]
        zero = jnp.zeros_like(qp)
        qm_ref[pair] = jnp.concatenate([jnp.where(rowi < ATTN_HEAD_DIM, qp, zero),
                                        jnp.where(rowi >= ATTN_HEAD_DIM, qp, zero)], axis=1)
    ones_rows = jnp.ones((2 * SUBLANES, CK), BF16)

    pq_col = posqc_ref[...]
    batch = pl.program_id(0)
    pq_first, pq_consec, pq_min = (pinfo_ref[batch, r, i] for r in (0, 1, 3))
    far_bias = [biasc_ref[0, h] for h in range(ATTN_HEADS)]
    zero_bias = [biasc_ref[1, h] for h in range(ATTN_HEADS)]
    q_ge_k = (lax.broadcasted_iota(jnp.int32, (LANES, TQ), 1) >= lax.broadcasted_iota(jnp.int32, (LANES, TQ), 0))
    n_sub = CK // LANES
    n_pairs = ATTN_HEADS // 2

    def chunk_is_far(c):
        return (pq_min - ckmax_ref[batch, jnp.minimum(c, seq // CK - 1)]) >= FAR_N

    def stage_bias(c):
        off = chunk_off(c)
        pk_row = posk_ref[:, pl.ds(off, CK)]
        for j in range(n_sub):
            rows = slice(j * LANES, (j + 1) * LANES)
            g = c * n_sub + j
            pk_first, pk_consec, pk_max = (pinfo_ref[batch, r, g] for r in (0, 1, 2))
            all_far = (pq_min - pk_max) >= FAR_N
            all_masked = (off + j * LANES) > (i * TQ + TQ - 1)
            consecutive = (pq_consec > 0) & (pk_consec > 0)
            gap = pq_first - pk_first

            def fill_const(rows=rows):
                for h in range(ATTN_HEADS):
                    b_ref[h, rows, :] = jnp.full((LANES, TQ), far_bias[h], F32)

            def fill_gap0(rows=rows):
                for h in range(ATTN_HEADS):
                    b_ref[h, rows, :] = jnp.where(q_ge_k, toep_ref[h], zero_bias[h])

            def fill_gap128(rows=rows):
                for h in range(ATTN_HEADS):
                    b_ref[h, rows, :] = jnp.where(q_ge_k, far_bias[h], toep_ref[h])

            def fill_lookup(rows=rows):
                pk_sub = pk_row[:, rows]
                n_qk = jnp.clip(pq_col - pk_sub, 0, BIAS_TABLE_N - 1).astype(F32)
                n_kq = n_qk.T.astype(jnp.int32)
                for h in range(ATTN_HEADS):
                    tb = jnp.broadcast_to(tbl_ref[h:h + 1, :], (LANES, BIAS_TABLE_N))
                    b_ref[h, rows, :] = jnp.take_along_axis(tb, n_kq, axis=1)

            def fill_near(fill_gap0=fill_gap0, fill_gap128=fill_gap128, fill_lookup=fill_lookup,
                          consecutive=consecutive, gap=gap):
                lax.cond(consecutive & (gap == 0), fill_gap0,
                         lambda: lax.cond(consecutive & (gap == LANES), fill_gap128, fill_lookup))

            lax.cond(all_far | all_masked, fill_const, fill_near)

    def logits_phase(c, s_buf, const_bias, seen):
        off = chunk_off(c)
        mb, seen = selection_mask(c, seen)
        m_cur = []
        for pair in range(n_pairs):
            kc = k_ref[pl.ds(off, CK), pair * LANES:(pair + 1) * LANES]
            s2 = jnp.dot(kc, qm_ref[pair], preferred_element_type=F32)
            for sub, h in enumerate((2 * pair, 2 * pair + 1)):
                s = s2[:, sub * TQ:(sub + 1) * TQ] + mb
                if not const_bias:
                    s = s + b_ref[h]
                s_buf[h] = s
                top = col_reduce(_fold_rows(s, jnp.maximum), jnp.max)
                m_cur.append(top + far_bias[h] if const_bias else top)
        return m_cur, seen

    def update_phase(c, s_buf, m_cur, const_bias):
        off = chunk_off(c)
        m_prev = m_ref[...]
        l_prev = l_ref[...]
        if isinstance(m_cur, list):
            m_new = [jnp.maximum(m_prev[h:h + 1, :], m_cur[h]) for h in range(ATTN_HEADS)]
            alpha = [jnp.exp2(m_prev[h:h + 1, :] - m_new[h]) for h in range(ATTN_HEADS)]
        else:
            m_all = jnp.maximum(m_prev, m_cur)
            a_all = jnp.exp2(m_prev - m_all)
            m_new = [m_all[h:h + 1, :] for h in range(ATTN_HEADS)]
            alpha = [a_all[h:h + 1, :] for h in range(ATTN_HEADS)]
        l_new = []
        for pair in range(n_pairs):
            heads = (2 * pair, 2 * pair + 1)
            sub_m = [m_new[h] - far_bias[h] if const_bias else m_new[h] for h in heads]
            p2 = jnp.concatenate([jnp.exp2(s_buf[h] - sm).astype(BF16) for h, sm in zip(heads, sub_m)],
                                 axis=1)
            lhs = jnp.concatenate([vT_ref[pair * LANES:(pair + 1) * LANES, pl.ds(off, CK)], ones_rows],
                                  axis=0)
            out = jnp.dot(lhs, p2, preferred_element_type=F32)
            for sub, h in enumerate(heads):
                cols = slice(sub * TQ, (sub + 1) * TQ)
                acc = acc_refs[h]
                acc[...] = alpha[h] * acc[...] + out[sub * ATTN_HEAD_DIM:(sub + 1) * ATTN_HEAD_DIM, cols]
                l_new.append(alpha[h] * l_prev[h:h + 1, :] + out[LANES:LANES + 1, cols])
        l_ref[...] = jnp.concatenate(l_new, axis=0)
        m_ref[...] = jnp.concatenate(m_new, axis=0)

    n_far = lax.while_loop(lambda c: (c < nch) & chunk_is_far(c), lambda c: c + 1, jnp.int32(0))
    n_steps = n_far // 2
    def far_logits(c, s_buf, seen):
        m_cur, seen = logits_phase(c, s_buf, True, seen)
        return jnp.concatenate(m_cur, axis=0), seen

    no_ties = jnp.zeros((1, TQ), F32)
    m_first, seen_first = lax.cond(n_steps > 0, lambda: far_logits(0, s_ref, no_ties),
                                   lambda: (jnp.zeros((ATTN_HEADS, TQ), F32), no_ties))

    def far_step(t, carry):
        m_even, seen_even, _ = carry
        c = 2 * t
        m_odd, seen_odd = far_logits(c + 1, s2_ref, seen_even)
        update_phase(c, s_ref, m_even, True)
        m_even, seen_even = far_logits(jnp.minimum(c + 2, 2 * n_steps - 2), s_ref, seen_odd)
        update_phase(c + 1, s2_ref, m_odd, True)
        return m_even, seen_even, seen_odd

    _, _, seen = lax.fori_loop(0, n_steps, far_step, (m_first, seen_first, no_ties))

    def tail_chunk(c, seen):
        def run(const_bias):
            if not const_bias:
                stage_bias(c)
            m_cur, seen_out = logits_phase(c, s_ref, const_bias, seen)
            update_phase(c, s_ref, m_cur, const_bias)
            return seen_out

        return lax.cond(chunk_is_far(c), lambda: run(True), lambda: run(False))

    lax.fori_loop(2 * n_steps, nch, tail_chunk, seen)

    outT = jnp.concatenate([acc_refs[h][...] / l_ref[h:h + 1, :] for h in range(ATTN_HEADS)], axis=0)
    o_ref[...] = outT.T.astype(o_ref.dtype)


def _toeplitz_kernel(tbl_ref, o_ref):
    q = lax.broadcasted_iota(jnp.int32, (LANES, LANES), 1)
    k = lax.broadcasted_iota(jnp.int32, (LANES, LANES), 0)
    idx = (q - k) & (BIAS_TABLE_N - 1)
    for h in range(ATTN_HEADS):
        tb = jnp.broadcast_to(tbl_ref[h:h + 1, :], (LANES, BIAS_TABLE_N))
        o_ref[h] = jnp.take_along_axis(tb, idx, axis=1)


def _sparse_attention(fm, tokb, ik, iwT, positions, rel_bias):
    B, S, _ = tokb.shape
    k_sel = min(TOPK_MAX, S // 4)
    bucket = _t5_bucket_table()
    tbl = rel_bias[bucket].T.astype(F32) * LOG2E
    bias_c = jnp.stack([tbl[:, BIAS_TABLE_N - 1], tbl[:, 0]])
    toep = pl.pallas_call(
        _toeplitz_kernel,
        out_shape=jax.ShapeDtypeStruct((ATTN_HEADS, LANES, LANES), F32),
        name="bias_toeplitz",
    )(tbl)
    pos_row = positions.reshape(B, 1, S)
    pos_col = positions.reshape(B, S, 1)
    pos_t = positions.reshape(B, S // LANES, LANES)
    consec = jnp.all(pos_t - pos_t[:, :, :1] == jnp.arange(LANES, dtype=positions.dtype), axis=-1)
    pinfo = jnp.stack([pos_t[:, :, 0], consec.astype(jnp.int32), jnp.max(pos_t, axis=-1),
                       jnp.min(pos_t, axis=-1)], axis=1).astype(jnp.int32)
    ck_max = jnp.max(positions.reshape(B, S // CK, CK), axis=-1)
    tril = jnp.tril(jnp.ones((LANES, LANES), BF16))
    sq = pl.Squeezed()
    iq_blk = (2 * ATTN_W) // IDX_Q_W
    k_blk = (tokb.shape[2] - ATTN_W) // ATTN_W
    return pl.pallas_call(
        functools.partial(_attn_kernel, k_sel=k_sel, seq=S),
        grid=(B, S // TQ),
        in_specs=[pl.BlockSpec(memory_space=pltpu.SMEM),
                  pl.BlockSpec(memory_space=pltpu.SMEM),
                  pl.BlockSpec(memory_space=pltpu.SMEM),
                  pl.BlockSpec((sq, IDX_Q_W, TQ), lambda b, i: (b, iq_blk, i)),
                  pl.BlockSpec((sq, SUBLANES, TQ), lambda b, i: (b, 0, i)),
                  pl.BlockSpec((sq, TQ, 1), lambda b, i: (b, i, 0)),
                  pl.BlockSpec((sq, 1, S), lambda b, i: (b, 0, 0)),
                  pl.BlockSpec((sq, S, IDX_HEAD_DIM), lambda b, i: (b, 0, 0)),
                  pl.BlockSpec((sq, ATTN_W, TQ), lambda b, i: (b, 0, i)),
                  pl.BlockSpec((sq, S, ATTN_W), lambda b, i: (b, 0, k_blk)),
                  pl.BlockSpec((sq, ATTN_W, S), lambda b, i: (b, 1, 0)),
                  pl.BlockSpec((ATTN_HEADS, BIAS_TABLE_N), lambda b, i: (0, 0)),
                  pl.BlockSpec((ATTN_HEADS, LANES, LANES), lambda b, i: (0, 0, 0)),
                  pl.BlockSpec((LANES, LANES), lambda b, i: (0, 0))],
        out_specs=pl.BlockSpec((sq, TQ, ATTN_W), lambda b, i: (b, i, 0)),
        out_shape=jax.ShapeDtypeStruct((B, S, ATTN_W), BF16),
        scratch_shapes=[pltpu.VMEM((S, TQ), F32),
                        pltpu.VMEM((ATTN_HEADS, CK, TQ), F32),
                        pltpu.VMEM((ATTN_HEADS, CK, TQ), F32),
                        pltpu.VMEM((ATTN_HEADS, CK, TQ), F32),
                        pltpu.VMEM((ATTN_HEADS // 2, LANES, 2 * TQ), BF16),
                        pltpu.VMEM((ATTN_HEADS, TQ), F32),
                        pltpu.VMEM((ATTN_HEADS, TQ), F32)]
                       + [pltpu.VMEM((ATTN_HEAD_DIM, TQ), F32)] * ATTN_HEADS,
        compiler_params=_cparams(2),
        name="sparse_attention",
    )(bias_c, pinfo, ck_max, fm, iwT, pos_col, pos_row, ik, fm, tokb, fm, tbl, toep, tril)


def _retention_kernel(q_ref, k_ref, v_ref, g_ref, decay_ref, xi_ref, zeta_ref, gch_ref, o_ref, r_ref):
    @pl.when(pl.program_id(0) == 0)
    def _():
        r_ref[...] = jnp.zeros(r_ref.shape, F32)

    C = RET_CHUNK
    lane = lax.broadcasted_iota(jnp.int32, (C, LANES), 1)
    row = lax.broadcasted_iota(jnp.int32, (LANES, RET_V_DIM), 0)
    for pair in range(RET_HEADS // 2):
        for b in range(q_ref.shape[0]):
            q_pair = q_ref[b, :, pair * LANES:(pair + 1) * LANES]
            k_pair = k_ref[b, :, pair * LANES:(pair + 1) * LANES]
            v_pair = v_ref[b, :, 2 * pair * RET_V_DIM:(2 * pair + 2) * RET_V_DIM]
            r_pair = r_ref[b, pair]
            r_bf = r_pair.astype(BF16)
            for sub in range(2):
                h = 2 * pair + sub
                in_head = (lane >= sub * RET_QK_DIM) & (lane < (sub + 1) * RET_QK_DIM)
                qm = jnp.where(in_head, q_pair, jnp.zeros_like(q_pair))
                v_h = v_pair[:, sub * RET_V_DIM:(sub + 1) * RET_V_DIM]
                inner = lax.dot_general(qm, k_pair, (((1,), (1,)), ((), ())),
                                        preferred_element_type=F32) * decay_ref[h]
                o = (jnp.dot(inner.astype(BF16), v_h, preferred_element_type=F32)
                     + jnp.dot(qm, r_bf, preferred_element_type=F32) * xi_ref[h])
                mu = jnp.mean(o, axis=-1, keepdims=True)
                d = o - mu
                var = jnp.mean(d * d, axis=-1, keepdims=True)
                hn = d * lax.rsqrt(var + LN_EPS)
                gate = g_ref[b, :, h * RET_V_DIM:(h + 1) * RET_V_DIM].astype(F32)
                o_ref[b, :, h * RET_V_DIM:(h + 1) * RET_V_DIM] = (gate * hn).astype(o_ref.dtype)
            kz = (k_pair.astype(F32) * zeta_ref[pair]).astype(BF16)
            upd = lax.dot_general(kz, v_pair, (((0,), (0,)), ((), ())), preferred_element_type=F32)
            r_ref[b, pair] = (r_pair * gch_ref[pair]
                              + jnp.where(row < RET_QK_DIM, upd[:, :RET_V_DIM], upd[:, RET_V_DIM:]))


def _retention(qk, tokb, gates, B, S):
    C = RET_CHUNK
    H = RET_HEADS
    nc = S // C
    gamma = 1.0 - 2.0 ** (-5.0 - jnp.arange(H, dtype=F32))
    log_g = jnp.log(gamma)
    n = jnp.arange(C, dtype=F32)
    diff = n[:, None] - n[None, :]
    decay_in = jnp.where(diff[None] >= 0, jnp.exp(log_g[:, None, None] * jnp.maximum(diff, 0.0)[None]), 0.0)
    xi = jnp.exp(log_g[None, :] * (n[:, None] + 1.0))
    zeta = jnp.exp(log_g[None, :] * (C - 1.0 - n[:, None]))
    g_chunk = jnp.exp(log_g * C)
    xi_b = jnp.broadcast_to(xi.T[:, :, None], (H, C, RET_V_DIM))
    zeta_b = jnp.repeat(zeta, RET_QK_DIM, axis=1).reshape(C, H // 2, LANES).transpose(1, 0, 2)
    gch_b = jnp.broadcast_to(jnp.repeat(g_chunk, RET_QK_DIM).reshape(H // 2, LANES, 1),
                             (H // 2, LANES, RET_V_DIM))
    qk3, tok3, gate3 = (a.reshape(B, S, a.shape[-1]) for a in (qk, tokb, gates))
    out = pl.pallas_call(
        _retention_kernel,
        grid=(nc,),
        in_specs=[pl.BlockSpec((B, C, RET_QK_W), lambda i: (0, i, 0)),
                  pl.BlockSpec((B, C, RET_QK_W), lambda i: (0, i, 1)),
                  pl.BlockSpec((B, C, RET_V_W), lambda i: (0, i, 0)),
                  pl.BlockSpec((B, C, RET_V_W), lambda i: (0, i, 0)),
                  pl.BlockSpec((H, C, C), lambda i: (0, 0, 0)),
                  pl.BlockSpec((H, C, RET_V_DIM), lambda i: (0, 0, 0)),
                  pl.BlockSpec((H // 2, C, LANES), lambda i: (0, 0, 0)),
                  pl.BlockSpec((H // 2, LANES, RET_V_DIM), lambda i: (0, 0, 0))],
        out_specs=pl.BlockSpec((B, C, RET_V_W), lambda i: (0, i, 0)),
        out_shape=jax.ShapeDtypeStruct((B, S, RET_V_W), BF16),
        scratch_shapes=[pltpu.VMEM((B, H // 2, LANES, RET_V_DIM), F32)],
        compiler_params=_cparams(1),
        name="retention",
    )(qk3, qk3, tok3, gate3, decay_in, xi_b, zeta_b, gch_b)
    return out.reshape(B * S, RET_V_W)


def _layer_norm(z, g, b):
    mu = jnp.mean(z, axis=-1, keepdims=True)
    d = z - mu
    var = jnp.mean(d * d, axis=-1, keepdims=True)
    return d * lax.rsqrt(var + LN_EPS) * g + b


def _merge_kernel(x_ref, ya_ref, yr_ref, ga_ref, gr_ref, wa_ref, wr_ref, wo_ref, g_ref, b_ref,
                  x1_ref, x1b_ref):
    a = jnp.dot(ya_ref[...], wa_ref[...], preferred_element_type=F32)
    r = jnp.dot(yr_ref[...], wr_ref[...], preferred_element_type=F32)
    h = ga_ref[...].astype(F32) * a + gr_ref[...].astype(F32) * r
    mix = jnp.dot(h.astype(BF16), wo_ref[...], preferred_element_type=F32)
    x1 = _layer_norm(DEEPNORM_ALPHA * x_ref[...] + mix, g_ref[...], b_ref[...])
    x1_ref[...] = x1
    x1b_ref[...] = x1.astype(BF16)


def _merge(x, ya, yr, gates, wa, wr, wo, g, b, tm=512):
    T, D = x.shape
    tm = min(tm, T)
    row = lambda i: (i, 0)
    fixed = lambda i: (0, 0)
    return pl.pallas_call(
        _merge_kernel,
        grid=(T // tm,),
        in_specs=[pl.BlockSpec((tm, D), row),
                  pl.BlockSpec((tm, ya.shape[1]), row),
                  pl.BlockSpec((tm, yr.shape[1]), row),
                  pl.BlockSpec((tm, D), lambda i: (i, 1)),
                  pl.BlockSpec((tm, D), lambda i: (i, 2)),
                  pl.BlockSpec(wa.shape, fixed),
                  pl.BlockSpec(wr.shape, fixed),
                  pl.BlockSpec(wo.shape, fixed),
                  pl.BlockSpec((1, D), fixed),
                  pl.BlockSpec((1, D), fixed)],
        out_specs=[pl.BlockSpec((tm, D), row), pl.BlockSpec((tm, D), row)],
        out_shape=[jax.ShapeDtypeStruct((T, D), F32), jax.ShapeDtypeStruct((T, D), BF16)],
        compiler_params=_cparams(1),
        name="merge",
    )(x, ya, yr, gates, gates, wa, wr, wo, g, b)


def _ffn_kernel(x1b_ref, x1_ref, wu_ref, wd_ref, g_ref, b_ref, o_ref, acc_ref):
    f = pl.program_id(1)

    @pl.when(f == 0)
    def _():
        acc_ref[...] = jnp.zeros(acc_ref.shape, F32)

    hid = jnp.maximum(jnp.dot(x1b_ref[...], wu_ref[...], preferred_element_type=F32), 0.0)
    acc_ref[...] += jnp.dot((hid * hid).astype(BF16), wd_ref[...], preferred_element_type=F32)

    @pl.when(f == pl.num_programs(1) - 1)
    def _():
        o_ref[...] = _layer_norm(DEEPNORM_ALPHA * x1_ref[...] + acc_ref[...], g_ref[...], b_ref[...])


def _ffn(x1b, x1, wu, wd, g, b, tm=1024, tf=1024):
    T, D = x1.shape
    F = wu.shape[1]
    tm = min(tm, T)
    return pl.pallas_call(
        _ffn_kernel,
        grid=(T // tm, F // tf),
        in_specs=[pl.BlockSpec((tm, D), lambda i, f: (i, 0)),
                  pl.BlockSpec((tm, D), lambda i, f: (i, 0)),
                  pl.BlockSpec((D, tf), lambda i, f: (0, f)),
                  pl.BlockSpec((tf, D), lambda i, f: (f, 0)),
                  pl.BlockSpec((1, D), lambda i, f: (0, 0)),
                  pl.BlockSpec((1, D), lambda i, f: (0, 0))],
        out_specs=pl.BlockSpec((tm, D), lambda i, f: (i, 0)),
        out_shape=jax.ShapeDtypeStruct((T, D), F32),
        scratch_shapes=[pltpu.VMEM((tm, D), F32)],
        compiler_params=_cparams(2),
        name="ffn",
    )(x1b, x1, wu, wd, g, b)


def _rot_half_weight(wT):
    N, D = wT.shape
    half = RET_QK_DIM // 2
    wh = wT.reshape(N // RET_QK_DIM, 2, half, D)
    return jnp.stack([-wh[:, 1], wh[:, 0]], axis=1).reshape(N, D)


def kernel(x, positions, w_in, rel_bias, idx_k_ln_g, idx_k_ln_b, w_attn_branch, w_ret_branch,
           w_out, ln_mix_g, ln_mix_b, w_up, w_down, ln_ffn_g, ln_ffn_b):
    B, S, D = x.shape
    T = B * S
    sizes = (ATTN_W, ATTN_W, ATTN_W, IDX_Q_W, IDX_HEAD_DIM, IDX_HEADS,
             RET_QK_W, RET_QK_W, RET_V_W, RET_V_W, D, D)
    offs = [0] + [int(o) for o in np.cumsum(sizes)]
    cos, sin = _rope_tables(positions)
    xf = x.reshape(T, D)
    for l in range(DEPTH):
        wT_f32 = jnp.swapaxes(w_in[l], 0, 1)
        wT = wT_f32.astype(BF16)
        rows = [wT[offs[k]:offs[k + 1]] for k in range(len(sizes))]
        (w_qa, w_ka, w_va, w_iq, w_ik, w_iw, w_qr, w_kr, w_vr, w_gr, w_ga, w_gtr) = rows
        xb = xf.astype(BF16)
        w_qa = (wT_f32[offs[0]:offs[1]] * (ATTN_HEAD_DIM ** -0.5 * LOG2E)).astype(BF16)
        w_kr = w_kr * (RET_QK_DIM ** -0.5)

        fm = _proj_t(xb, jnp.concatenate([w_qa, w_va, w_iq], axis=0), B, S, BF16, tn=1280)
        tokb = _proj(xb, jnp.concatenate([w_vr, w_ka], axis=0), BF16, tn=1536)
        gates = _proj_gates(xb, jnp.concatenate([w_gr, w_ga, w_gtr], axis=0), tn=D)
        w_rope = jnp.concatenate([w_qr, w_kr], axis=0)
        w_rope_rot = jnp.concatenate([_rot_half_weight(w_qr), _rot_half_weight(w_kr)], axis=0)
        qk_r = _proj_rope(xb, w_rope, w_rope_rot, cos, sin, tn=1024)
        pad = LANES - IDX_HEAD_DIM - IDX_HEADS
        w_idx = jnp.concatenate([w_ik, w_iw, jnp.zeros((pad, D), BF16)], axis=0)
        g_pad = jnp.concatenate([idx_k_ln_g[l], jnp.zeros((LANES - IDX_HEAD_DIM,), F32)]).reshape(1, LANES)
        b_pad = jnp.concatenate([idx_k_ln_b[l], jnp.zeros((LANES - IDX_HEAD_DIM,), F32)]).reshape(1, LANES)
        idx = _proj_idx(xb, w_idx, g_pad, b_pad,
                        (IDX_HEAD_DIM ** -0.5) * (IDX_HEADS ** -0.5)).reshape(B, S, LANES)
        ik = idx[:, :, :IDX_HEAD_DIM].astype(BF16)
        iwT = jnp.swapaxes(idx[:, :, IDX_HEAD_DIM:IDX_HEAD_DIM + SUBLANES], 1, 2)

        y_a = _sparse_attention(fm, tokb.reshape(B, S, -1), ik, iwT, positions, rel_bias)
        y_r = _retention(qk_r, tokb, gates, B, S)
        x1, x1b = _merge(xf, y_a.reshape(T, ATTN_W), y_r, gates,
                         w_attn_branch[l].astype(BF16), w_ret_branch[l].astype(BF16),
                         w_out[l].astype(BF16), ln_mix_g[l].reshape(1, D), ln_mix_b[l].reshape(1, D))
        xf = _ffn(x1b, x1, w_up[l].astype(BF16), w_down[l].astype(BF16),
                  ln_ffn_g[l].reshape(1, D), ln_ffn_b[l].reshape(1, D))
    return xf.reshape(B, S, D)
```

```python
import functools
import math

import numpy as np
import jax
import jax.numpy as jnp
from jax import lax
from jax.experimental import pallas as pl
from jax.experimental.pallas import tpu as pltpu

F32 = jnp.float32
BF16 = jnp.bfloat16

ATTN_HEADS = 8
ATTN_HEAD_DIM = 64
ATTN_W = ATTN_HEADS * ATTN_HEAD_DIM
IDX_HEADS = 4
IDX_HEAD_DIM = 64
IDX_Q_W = IDX_HEADS * IDX_HEAD_DIM
TOPK_MAX = 256
RET_HEADS = 8
RET_QK_DIM = 64
RET_V_DIM = 128
RET_QK_W = RET_HEADS * RET_QK_DIM
RET_V_W = RET_HEADS * RET_V_DIM
RET_CHUNK = 128
ROPE_BASE = 10000.0
NUM_BUCKETS = 32
MAX_DISTANCE = 128
LN_EPS = 1e-5
DEPTH = 1
DEEPNORM_ALPHA = (2.0 * DEPTH) ** 0.25

LANES = 128
SUBLANES = 8
VMEM_LIMIT = 56 * 1024 * 1024

TQ = 128
CK = 512
NEG = -1e30
LOG2E = math.log2(math.e)
BISECT_ROUNDS = 18
BIAS_TABLE_N = 128
FAR_N = 113


def _cparams(n_grid):
    return pltpu.CompilerParams(
        dimension_semantics=("arbitrary",) * n_grid,
        vmem_limit_bytes=VMEM_LIMIT)


def _trig_kernel(pos_ref, inv_ref, cos_ref, sin_ref):
    ang = pos_ref[...] * inv_ref[...]
    cos_ref[...] = jnp.cos(ang)
    sin_ref[...] = jnp.sin(ang)


def _rope_tables(positions):
    B, S = positions.shape
    half = RET_QK_DIM // 2
    inv = ROPE_BASE ** (-jnp.arange(half, dtype=F32) / half)
    per_row = LANES // half
    rows = B * S // per_row
    pos_e = jnp.repeat(positions.astype(F32).reshape(rows, per_row), half, axis=1)
    inv_e = jnp.tile(inv, per_row).reshape(1, LANES)
    tr = min(rows, 1024)
    cos, sin = pl.pallas_call(
        _trig_kernel,
        grid=(rows // tr,),
        in_specs=[pl.BlockSpec((tr, LANES), lambda i: (i, 0)),
                  pl.BlockSpec((1, LANES), lambda i: (0, 0))],
        out_specs=[pl.BlockSpec((tr, LANES), lambda i: (i, 0))] * 2,
        out_shape=[jax.ShapeDtypeStruct((rows, LANES), F32)] * 2,
        compiler_params=_cparams(1),
        name="rope_tables",
    )(pos_e, inv_e)
    cos = jnp.tile(cos.reshape(B * S, half), (1, per_row))
    sin = jnp.tile(sin.reshape(B * S, half), (1, per_row))
    return cos, sin


def _x_wt(x, wT):
    return lax.dot_general(x, wT, (((1,), (1,)), ((), ())), preferred_element_type=F32)


def _proj_kernel(x_ref, w_ref, o_ref):
    o_ref[...] = _x_wt(x_ref[...], w_ref[...]).astype(o_ref.dtype)


def _proj(xb, wT, out_dtype, tm=1024, tn=512):
    T, D = xb.shape
    N = wT.shape[0]
    tn = min(N, tn)
    tm = min(tm, T)
    return pl.pallas_call(
        _proj_kernel,
        grid=(T // tm, N // tn),
        in_specs=[pl.BlockSpec((tm, D), lambda i, j: (i, 0)),
                  pl.BlockSpec((tn, D), lambda i, j: (j, 0))],
        out_specs=pl.BlockSpec((tm, tn), lambda i, j: (i, j)),
        out_shape=jax.ShapeDtypeStruct((T, N), out_dtype),
        compiler_params=_cparams(2),
        name="proj",
    )(xb, wT)


def _proj_gates_kernel(x_ref, w_ref, o_ref):
    acc = _x_wt(x_ref[...], w_ref[...])
    sig = 0.5 * jnp.tanh(0.5 * acc) + 0.5
    o_ref[...] = jnp.where(pl.program_id(1) == 0, acc * sig, sig).astype(o_ref.dtype)


def _proj_gates(xb, wT, tn, tm=1024):
    T, D = xb.shape
    N = wT.shape[0]
    tm = min(tm, T)
    return pl.pallas_call(
        _proj_gates_kernel,
        grid=(T // tm, N // tn),
        in_specs=[pl.BlockSpec((tm, D), lambda i, j: (i, 0)),
                  pl.BlockSpec((tn, D), lambda i, j: (j, 0))],
        out_specs=pl.BlockSpec((tm, tn), lambda i, j: (i, j)),
        out_shape=jax.ShapeDtypeStruct((T, N), BF16),
        compiler_params=_cparams(2),
        name="proj_gates",
    )(xb, wT)


def _proj_t_kernel(wT_ref, x_ref, o_ref):
    acc = lax.dot_general(wT_ref[...], x_ref[...], (((1,), (1,)), ((), ())), preferred_element_type=F32)
    o_ref[...] = acc.astype(o_ref.dtype)


def _proj_t(xb, wT, B, S, out_dtype, tm=1024, tn=640):
    T, D = xb.shape
    N = wT.shape[0]
    tn = min(N, tn)
    tm = min(tm, S)
    nsb = S // tm
    return pl.pallas_call(
        _proj_t_kernel,
        grid=(T // tm, N // tn),
        in_specs=[pl.BlockSpec((tn, D), lambda i, j: (j, 0)),
                  pl.BlockSpec((tm, D), lambda i, j: (i, 0))],
        out_specs=pl.BlockSpec((pl.Squeezed(), tn, tm), lambda i, j: (i // nsb, j, i % nsb)),
        out_shape=jax.ShapeDtypeStruct((B, N, S), out_dtype),
        compiler_params=_cparams(2),
        name="proj_t",
    )(wT, xb)


def _proj_rope_kernel(x_ref, w_ref, wr_ref, cos_ref, sin_ref, o_ref):
    x = x_ref[...]
    a = _x_wt(x, w_ref[...])
    r = _x_wt(x, wr_ref[...])
    reps = a.shape[1] // LANES
    cos = jnp.concatenate([cos_ref[...]] * reps, axis=1)
    sin = jnp.concatenate([sin_ref[...]] * reps, axis=1)
    o_ref[...] = (a * cos + r * sin).astype(o_ref.dtype)


def _proj_rope(xb, wT, wT_rot, cos, sin, tm=1024, tn=512):
    T, D = xb.shape
    N = wT.shape[0]
    tm = min(tm, T)
    return pl.pallas_call(
        _proj_rope_kernel,
        grid=(T // tm, N // tn),
        in_specs=[pl.BlockSpec((tm, D), lambda i, j: (i, 0)),
                  pl.BlockSpec((tn, D), lambda i, j: (j, 0)),
                  pl.BlockSpec((tn, D), lambda i, j: (j, 0)),
                  pl.BlockSpec((tm, LANES), lambda i, j: (i, 0)),
                  pl.BlockSpec((tm, LANES), lambda i, j: (i, 0))],
        out_specs=pl.BlockSpec((tm, tn), lambda i, j: (i, j)),
        out_shape=jax.ShapeDtypeStruct((T, N), BF16),
        compiler_params=_cparams(2),
        name="proj_rope",
    )(xb, wT, wT_rot, cos, sin)


def _proj_idx_kernel(x_ref, w_ref, g_ref, b_ref, o_ref, *, iw_scale):
    acc = _x_wt(x_ref[...], w_ref[...])
    lane = lax.broadcasted_iota(jnp.int32, acc.shape, 1)
    is_k = lane < IDX_HEAD_DIM
    mu = jnp.sum(jnp.where(is_k, acc, 0.0), axis=-1, keepdims=True) / IDX_HEAD_DIM
    d = acc - mu
    var = jnp.sum(jnp.where(is_k, d * d, 0.0), axis=-1, keepdims=True) / IDX_HEAD_DIM
    ln = d * lax.rsqrt(var + LN_EPS) * g_ref[...] + b_ref[...]
    o_ref[...] = jnp.where(is_k, ln, acc * iw_scale)


def _proj_idx(xb, w_pad, g_pad, b_pad, iw_scale, tm=1024):
    T, D = xb.shape
    tm = min(tm, T)
    return pl.pallas_call(
        functools.partial(_proj_idx_kernel, iw_scale=iw_scale),
        grid=(T // tm,),
        in_specs=[pl.BlockSpec((tm, D), lambda i: (i, 0)),
                  pl.BlockSpec((LANES, D), lambda i: (0, 0)),
                  pl.BlockSpec((1, LANES), lambda i: (0, 0)),
                  pl.BlockSpec((1, LANES), lambda i: (0, 0))],
        out_specs=pl.BlockSpec((tm, LANES), lambda i: (i, 0)),
        out_shape=jax.ShapeDtypeStruct((T, LANES), F32),
        compiler_params=_cparams(1),
        name="proj_idx",
    )(xb, w_pad, g_pad, b_pad)


def _t5_bucket_table():
    n = np.arange(BIAS_TABLE_N)
    max_exact = NUM_BUCKETS // 2
    nf = np.maximum(n, 1).astype(np.float64)
    large = max_exact + (np.log(nf / max_exact) / math.log(MAX_DISTANCE / max_exact)
                         * (NUM_BUCKETS - max_exact)).astype(np.int64)
    large = np.minimum(large, NUM_BUCKETS - 1)
    bucket = np.where(n < max_exact, n, large)
    assert np.all(bucket[FAR_N:] == NUM_BUCKETS - 1) and bucket[FAR_N - 1] != NUM_BUCKETS - 1
    return bucket.astype(np.int32)


def _fold_rows(a, op):
    parts = [a[r:r + SUBLANES] for r in range(0, a.shape[0], SUBLANES)]
    while len(parts) > 1:
        nxt = [op(parts[k], parts[k + 1]) for k in range(0, len(parts) - 1, 2)]
        if len(parts) % 2:
            nxt.append(parts[-1])
        parts = nxt
    return parts[0]


def _attn_kernel(biasc_ref, pinfo_ref, ckmax_ref, iqT_ref, iwT_ref, posqc_ref, posk_ref, ik_ref, qT_ref, k_ref, vT_ref,
                 tbl_ref, toep_ref, tril_ref, o_ref, sc_ref, s_ref, s2_ref, b_ref, qm_ref, m_ref, l_ref, *acc_refs, k_sel, seq):
    i = pl.program_id(1)
    nch = (i * TQ + TQ + CK - 1) // CK
    q_idx = i * TQ + lax.broadcasted_iota(jnp.int32, (1, TQ), 1)
    kf = float(k_sel)

    def chunk_off(c):
        return pl.multiple_of(c * CK, CK)

    def key_idx(off):
        return off + lax.broadcasted_iota(jnp.int32, (CK, TQ), 0)

    def col_reduce(part, op):
        return op(part, axis=0, keepdims=True)

    iqT = iqT_ref[...]
    iwT = iwT_ref[...]
    iq_wide = jnp.concatenate([iqT[h * IDX_HEAD_DIM:(h + 1) * IDX_HEAD_DIM, :] for h in range(IDX_HEADS)], axis=1)

    def score_body(c, carry, masked):
        mn, mx = carry
        off = chunk_off(c)
        ikc = ik_ref[pl.ds(off, CK), :]
        z = jnp.dot(ikc, iq_wide, preferred_element_type=F32)
        s = None
        for h in range(IDX_HEADS):
            t = jnp.maximum(z[:, h * TQ:(h + 1) * TQ], 0.0) * iwT[h:h + 1, :]
            s = t if s is None else s + t
        if masked:
            causal = key_idx(off) <= q_idx
            s_lo = jnp.where(causal, s, -jnp.inf)
            s_hi = jnp.where(causal, s, jnp.inf)
        else:
            s_lo = s_hi = s
        sc_ref[pl.ds(off, CK), :] = s_lo
        mn = jnp.minimum(mn, _fold_rows(s_hi, jnp.minimum))
        mx = jnp.maximum(mx, _fold_rows(s_lo, jnp.maximum))
        return mn, mx

    mn8, mx8 = lax.fori_loop(0, nch - 1, functools.partial(score_body, masked=False),
                             (jnp.full((SUBLANES, TQ), jnp.inf, F32), jnp.full((SUBLANES, TQ), -jnp.inf, F32)))
    mn8, mx8 = score_body(nch - 1, (mn8, mx8), masked=True)
    mn = col_reduce(mn8, jnp.min)
    mx = col_reduce(mx8, jnp.max)

    def over_chunks(fn, init):
        return lax.fori_loop(0, nch, lambda c, carry: fn(sc_ref[pl.ds(chunk_off(c), CK), :], carry), init)

    def count(pred_fn):
        acc = over_chunks(lambda blk, acc: acc + _fold_rows(jnp.where(pred_fn(blk), 1.0, 0.0), jnp.add),
                          jnp.zeros((SUBLANES, TQ), F32))
        return col_reduce(acc, jnp.sum)

    def bisect_round(_, st):
        lo, hi, c_lo = st
        mid = 0.5 * (lo + hi)
        c = count(lambda blk: blk >= mid)
        ok = c >= kf
        return jnp.where(ok, mid, lo), jnp.where(ok, hi, mid), jnp.where(ok, c, c_lo)

    c_all = (q_idx + 1).astype(F32)
    lo, hi, c_lo = lax.fori_loop(0, BISECT_ROUNDS, bisect_round, (mn, mx, c_all))

    cur0 = col_reduce(over_chunks(
        lambda blk, acc: jnp.minimum(acc, _fold_rows(jnp.where(blk >= lo, blk, jnp.inf), jnp.minimum)),
        jnp.full((SUBLANES, TQ), jnp.inf, F32)), jnp.min)

    def walk_cond(st):
        return st[3] > 0.0

    def walk_body(st):
        cur, c_ge, _, _ = st

        def body(blk, carry):
            cnt, nxt = carry
            gt = blk > cur
            cnt = cnt + _fold_rows(jnp.where(gt, 1.0, 0.0), jnp.add)
            nxt = jnp.minimum(nxt, _fold_rows(jnp.where(gt, blk, jnp.inf), jnp.minimum))
            return cnt, nxt

        cnt, nxt = over_chunks(body, (jnp.zeros((SUBLANES, TQ), F32), jnp.full((SUBLANES, TQ), jnp.inf, F32)))
        c_gt = col_reduce(cnt, jnp.sum)
        nxt = col_reduce(nxt, jnp.min)
        adv = c_gt >= kf
        cur = jnp.where(adv, nxt, cur)
        c_ge = jnp.where(adv, c_gt, c_ge)
        return cur, c_ge, c_gt, jnp.max(jnp.where(adv, 1.0, 0.0))

    tau, c_ge, c_gt, _ = lax.while_loop(
        walk_cond, walk_body, (cur0, c_lo, jnp.zeros((1, TQ), F32), jnp.float32(1.0)))

    room = kf - c_gt

    def selection_mask(c, seen):
        off = chunk_off(c)
        tiles = range(CK // LANES)
        blks = [sc_ref[pl.ds(off + j * LANES, LANES), :] for j in tiles]
        ties = [blk == tau for blk in blks]
        local = [jnp.dot(tril_ref[...], jnp.where(tie, 1.0, 0.0).astype(BF16), preferred_element_type=F32)
                 for tie in ties]
        masks = []
        for j in tiles:
            keep = (blks[j] > tau) | (ties[j] & (local[j] + seen <= room))
            masks.append(jnp.where(keep, 0.0, NEG))
            seen = seen + local[j][LANES - 1:LANES, :]
        return jnp.concatenate(masks, axis=0), seen

    m_ref[...] = jnp.full(m_ref.shape, NEG, F32)
    l_ref[...] = jnp.zeros(l_ref.shape, F32)
    for acc in acc_refs:
        acc[...] = jnp.zeros(acc.shape, F32)

    rowi = lax.broadcasted_iota(jnp.int32, (LANES, TQ), 0)
    for pair in range(ATTN_HEADS // 2):
        qp = qT_ref[pair * LANES:(pair + 1) * LANES, :]
        zero = jnp.zeros_like(qp)
        qm_ref[pair] = jnp.concatenate([jnp.where(rowi < ATTN_HEAD_DIM, qp, zero),
                                        jnp.where(rowi >= ATTN_HEAD_DIM, qp, zero)], axis=1)
    ones_rows = jnp.ones((2 * SUBLANES, CK), BF16)

    pq_col = posqc_ref[...]
    batch = pl.program_id(0)
    pq_first, pq_consec, pq_min = (pinfo_ref[batch, r, i] for r in (0, 1, 3))
    far_bias = [biasc_ref[0, h] for h in range(ATTN_HEADS)]
    zero_bias = [biasc_ref[1, h] for h in range(ATTN_HEADS)]
    q_ge_k = (lax.broadcasted_iota(jnp.int32, (LANES, TQ), 1) >= lax.broadcasted_iota(jnp.int32, (LANES, TQ), 0))
    n_sub = CK // LANES
    n_pairs = ATTN_HEADS // 2

    def chunk_is_far(c):
        return (pq_min - ckmax_ref[batch, jnp.minimum(c, seq // CK - 1)]) >= FAR_N

    def stage_bias(c):
        off = chunk_off(c)
        pk_row = posk_ref[:, pl.ds(off, CK)]
        for j in range(n_sub):
            rows = slice(j * LANES, (j + 1) * LANES)
            g = c * n_sub + j
            pk_first, pk_consec, pk_max = (pinfo_ref[batch, r, g] for r in (0, 1, 2))
            all_far = (pq_min - pk_max) >= FAR_N
            all_masked = (off + j * LANES) > (i * TQ + TQ - 1)
            consecutive = (pq_consec > 0) & (pk_consec > 0)
            gap = pq_first - pk_first

            def fill_const(rows=rows):
                for h in range(ATTN_HEADS):
                    b_ref[h, rows, :] = jnp.full((LANES, TQ), far_bias[h], F32)

            def fill_gap0(rows=rows):
                for h in range(ATTN_HEADS):
                    b_ref[h, rows, :] = jnp.where(q_ge_k, toep_ref[h], zero_bias[h])

            def fill_gap128(rows=rows):
                for h in range(ATTN_HEADS):
                    b_ref[h, rows, :] = jnp.where(q_ge_k, far_bias[h], toep_ref[h])

            def fill_lookup(rows=rows):
                pk_sub = pk_row[:, rows]
                n_qk = jnp.clip(pq_col - pk_sub, 0, BIAS_TABLE_N - 1).astype(F32)
                n_kq = n_qk.T.astype(jnp.int32)
                for h in range(ATTN_HEADS):
                    tb = jnp.broadcast_to(tbl_ref[h:h + 1, :], (LANES, BIAS_TABLE_N))
                    b_ref[h, rows, :] = jnp.take_along_axis(tb, n_kq, axis=1)

            def fill_near(fill_gap0=fill_gap0, fill_gap128=fill_gap128, fill_lookup=fill_lookup,
                          consecutive=consecutive, gap=gap):
                lax.cond(consecutive & (gap == 0), fill_gap0,
                         lambda: lax.cond(consecutive & (gap == LANES), fill_gap128, fill_lookup))

            lax.cond(all_far | all_masked, fill_const, fill_near)

    def logits_phase(c, s_buf, const_bias, seen):
        off = chunk_off(c)
        mb, seen = selection_mask(c, seen)
        m_cur = []
        for pair in range(n_pairs):
            kc = k_ref[pl.ds(off, CK), pair * LANES:(pair + 1) * LANES]
            s2 = jnp.dot(kc, qm_ref[pair], preferred_element_type=F32)
            for sub, h in enumerate((2 * pair, 2 * pair + 1)):
                s = s2[:, sub * TQ:(sub + 1) * TQ] + mb
                if not const_bias:
                    s = s + b_ref[h]
                s_buf[h] = s
                top = col_reduce(_fold_rows(s, jnp.maximum), jnp.max)
                m_cur.append(top + far_bias[h] if const_bias else top)
        return m_cur, seen

    def update_phase(c, s_buf, m_cur, const_bias):
        off = chunk_off(c)
        m_prev = m_ref[...]
        l_prev = l_ref[...]
        if isinstance(m_cur, list):
            m_new = [jnp.maximum(m_prev[h:h + 1, :], m_cur[h]) for h in range(ATTN_HEADS)]
            alpha = [jnp.exp2(m_prev[h:h + 1, :] - m_new[h]) for h in range(ATTN_HEADS)]
        else:
            m_all = jnp.maximum(m_prev, m_cur)
            a_all = jnp.exp2(m_prev - m_all)
            m_new = [m_all[h:h + 1, :] for h in range(ATTN_HEADS)]
            alpha = [a_all[h:h + 1, :] for h in range(ATTN_HEADS)]
        l_new = []
        for pair in range(n_pairs):
            heads = (2 * pair, 2 * pair + 1)
            sub_m = [m_new[h] - far_bias[h] if const_bias else m_new[h] for h in heads]
            p2 = jnp.concatenate([jnp.exp2(s_buf[h] - sm).astype(BF16) for h, sm in zip(heads, sub_m)],
                                 axis=1)
            lhs = jnp.concatenate([vT_ref[pair * LANES:(pair + 1) * LANES, pl.ds(off, CK)], ones_rows],
                                  axis=0)
            out = jnp.dot(lhs, p2, preferred_element_type=F32)
            for sub, h in enumerate(heads):
                cols = slice(sub * TQ, (sub + 1) * TQ)
                acc = acc_refs[h]
                acc[...] = alpha[h] * acc[...] + out[sub * ATTN_HEAD_DIM:(sub + 1) * ATTN_HEAD_DIM, cols]
                l_new.append(alpha[h] * l_prev[h:h + 1, :] + out[LANES:LANES + 1, cols])
        l_ref[...] = jnp.concatenate(l_new, axis=0)
        m_ref[...] = jnp.concatenate(m_new, axis=0)

    n_far = lax.while_loop(lambda c: (c < nch) & chunk_is_far(c), lambda c: c + 1, jnp.int32(0))
    n_steps = n_far // 2
    def far_logits(c, s_buf, seen):
        m_cur, seen = logits_phase(c, s_buf, True, seen)
        return jnp.concatenate(m_cur, axis=0), seen

    no_ties = jnp.zeros((1, TQ), F32)
    m_first, seen_first = lax.cond(n_steps > 0, lambda: far_logits(0, s_ref, no_ties),
                                   lambda: (jnp.zeros((ATTN_HEADS, TQ), F32), no_ties))

    def far_step(t, carry):
        m_even, seen_even, _ = carry
        c = 2 * t
        m_odd, seen_odd = far_logits(c + 1, s2_ref, seen_even)
        update_phase(c, s_ref, m_even, True)
        m_even, seen_even = far_logits(jnp.minimum(c + 2, 2 * n_steps - 2), s_ref, seen_odd)
        update_phase(c + 1, s2_ref, m_odd, True)
        return m_even, seen_even, seen_odd

    _, _, seen = lax.fori_loop(0, n_steps, far_step, (m_first, seen_first, no_ties))

    def tail_chunk(c, seen):
        def run(const_bias):
            if not const_bias:
                stage_bias(c)
            m_cur, seen_out = logits_phase(c, s_ref, const_bias, seen)
            update_phase(c, s_ref, m_cur, const_bias)
            return seen_out

        return lax.cond(chunk_is_far(c), lambda: run(True), lambda: run(False))

    lax.fori_loop(2 * n_steps, nch, tail_chunk, seen)

    outT = jnp.concatenate([acc_refs[h][...] / l_ref[h:h + 1, :] for h in range(ATTN_HEADS)], axis=0)
    o_ref[...] = outT.T.astype(o_ref.dtype)


def _toeplitz_kernel(tbl_ref, o_ref):
    q = lax.broadcasted_iota(jnp.int32, (LANES, LANES), 1)
    k = lax.broadcasted_iota(jnp.int32, (LANES, LANES), 0)
    idx = (q - k) & (BIAS_TABLE_N - 1)
    for h in range(ATTN_HEADS):
        tb = jnp.broadcast_to(tbl_ref[h:h + 1, :], (LANES, BIAS_TABLE_N))
        o_ref[h] = jnp.take_along_axis(tb, idx, axis=1)


def _sparse_attention(fm, tokb, ik, iwT, positions, rel_bias):
    B, S, _ = tokb.shape
    k_sel = min(TOPK_MAX, S // 4)
    bucket = _t5_bucket_table()
    tbl = rel_bias[bucket].T.astype(F32) * LOG2E
    bias_c = jnp.stack([tbl[:, BIAS_TABLE_N - 1], tbl[:, 0]])
    toep = pl.pallas_call(
        _toeplitz_kernel,
        out_shape=jax.ShapeDtypeStruct((ATTN_HEADS, LANES, LANES), F32),
        name="bias_toeplitz",
    )(tbl)
    pos_row = positions.reshape(B, 1, S)
    pos_col = positions.reshape(B, S, 1)
    pos_t = positions.reshape(B, S // LANES, LANES)
    consec = jnp.all(pos_t - pos_t[:, :, :1] == jnp.arange(LANES, dtype=positions.dtype), axis=-1)
    pinfo = jnp.stack([pos_t[:, :, 0], consec.astype(jnp.int32), jnp.max(pos_t, axis=-1),
                       jnp.min(pos_t, axis=-1)], axis=1).astype(jnp.int32)
    ck_max = jnp.max(positions.reshape(B, S // CK, CK), axis=-1)
    tril = jnp.tril(jnp.ones((LANES, LANES), BF16))
    sq = pl.Squeezed()
    iq_blk = (2 * ATTN_W) // IDX_Q_W
    k_blk = (tokb.shape[2] - ATTN_W) // ATTN_W
    return pl.pallas_call(
        functools.partial(_attn_kernel, k_sel=k_sel, seq=S),
        grid=(B, S // TQ),
        in_specs=[pl.BlockSpec(memory_space=pltpu.SMEM),
                  pl.BlockSpec(memory_space=pltpu.SMEM),
                  pl.BlockSpec(memory_space=pltpu.SMEM),
                  pl.BlockSpec((sq, IDX_Q_W, TQ), lambda b, i: (b, iq_blk, i)),
                  pl.BlockSpec((sq, SUBLANES, TQ), lambda b, i: (b, 0, i)),
                  pl.BlockSpec((sq, TQ, 1), lambda b, i: (b, i, 0)),
                  pl.BlockSpec((sq, 1, S), lambda b, i: (b, 0, 0)),
                  pl.BlockSpec((sq, S, IDX_HEAD_DIM), lambda b, i: (b, 0, 0)),
                  pl.BlockSpec((sq, ATTN_W, TQ), lambda b, i: (b, 0, i)),
                  pl.BlockSpec((sq, S, ATTN_W), lambda b, i: (b, 0, k_blk)),
                  pl.BlockSpec((sq, ATTN_W, S), lambda b, i: (b, 1, 0)),
                  pl.BlockSpec((ATTN_HEADS, BIAS_TABLE_N), lambda b, i: (0, 0)),
                  pl.BlockSpec((ATTN_HEADS, LANES, LANES), lambda b, i: (0, 0, 0)),
                  pl.BlockSpec((LANES, LANES), lambda b, i: (0, 0))],
        out_specs=pl.BlockSpec((sq, TQ, ATTN_W), lambda b, i: (b, i, 0)),
        out_shape=jax.ShapeDtypeStruct((B, S, ATTN_W), BF16),
        scratch_shapes=[pltpu.VMEM((S, TQ), F32),
                        pltpu.VMEM((ATTN_HEADS, CK, TQ), F32),
                        pltpu.VMEM((ATTN_HEADS, CK, TQ), F32),
                        pltpu.VMEM((ATTN_HEADS, CK, TQ), F32),
                        pltpu.VMEM((ATTN_HEADS // 2, LANES, 2 * TQ), BF16),
                        pltpu.VMEM((ATTN_HEADS, TQ), F32),
                        pltpu.VMEM((ATTN_HEADS, TQ), F32)]
                       + [pltpu.VMEM((ATTN_HEAD_DIM, TQ), F32)] * ATTN_HEADS,
        compiler_params=_cparams(2),
        name="sparse_attention",
    )(bias_c, pinfo, ck_max, fm, iwT, pos_col, pos_row, ik, fm, tokb, fm, tbl, toep, tril)


def _retention_kernel(q_ref, k_ref, v_ref, g_ref, decay_ref, xi_ref, zeta_ref, gch_ref, o_ref, r_ref):
    @pl.when(pl.program_id(0) == 0)
    def _():
        r_ref[...] = jnp.zeros(r_ref.shape, F32)

    C = RET_CHUNK
    lane = lax.broadcasted_iota(jnp.int32, (C, LANES), 1)
    row = lax.broadcasted_iota(jnp.int32, (LANES, RET_V_DIM), 0)
    for pair in range(RET_HEADS // 2):
        for b in range(q_ref.shape[0]):
            q_pair = q_ref[b, :, pair * LANES:(pair + 1) * LANES]
            k_pair = k_ref[b, :, pair * LANES:(pair + 1) * LANES]
            v_pair = v_ref[b, :, 2 * pair * RET_V_DIM:(2 * pair + 2) * RET_V_DIM]
            r_pair = r_ref[b, pair]
            r_bf = r_pair.astype(BF16)
            for sub in range(2):
                h = 2 * pair + sub
                in_head = (lane >= sub * RET_QK_DIM) & (lane < (sub + 1) * RET_QK_DIM)
                qm = jnp.where(in_head, q_pair, jnp.zeros_like(q_pair))
                v_h = v_pair[:, sub * RET_V_DIM:(sub + 1) * RET_V_DIM]
                inner = lax.dot_general(qm, k_pair, (((1,), (1,)), ((), ())),
                                        preferred_element_type=F32) * decay_ref[h]
                o = (jnp.dot(inner.astype(BF16), v_h, preferred_element_type=F32)
                     + jnp.dot(qm, r_bf, preferred_element_type=F32) * xi_ref[h])
                mu = jnp.mean(o, axis=-1, keepdims=True)
                d = o - mu
                var = jnp.mean(d * d, axis=-1, keepdims=True)
                hn = d * lax.rsqrt(var + LN_EPS)
                gate = g_ref[b, :, h * RET_V_DIM:(h + 1) * RET_V_DIM].astype(F32)
                o_ref[b, :, h * RET_V_DIM:(h + 1) * RET_V_DIM] = (gate * hn).astype(o_ref.dtype)
            kz = (k_pair.astype(F32) * zeta_ref[pair]).astype(BF16)
            upd = lax.dot_general(kz, v_pair, (((0,), (0,)), ((), ())), preferred_element_type=F32)
            r_ref[b, pair] = (r_pair * gch_ref[pair]
                              + jnp.where(row < RET_QK_DIM, upd[:, :RET_V_DIM], upd[:, RET_V_DIM:]))


def _retention(qk, tokb, gates, B, S):
    C = RET_CHUNK
    H = RET_HEADS
    nc = S // C
    gamma = 1.0 - 2.0 ** (-5.0 - jnp.arange(H, dtype=F32))
    log_g = jnp.log(gamma)
    n = jnp.arange(C, dtype=F32)
    diff = n[:, None] - n[None, :]
    decay_in = jnp.where(diff[None] >= 0, jnp.exp(log_g[:, None, None] * jnp.maximum(diff, 0.0)[None]), 0.0)
    xi = jnp.exp(log_g[None, :] * (n[:, None] + 1.0))
    zeta = jnp.exp(log_g[None, :] * (C - 1.0 - n[:, None]))
    g_chunk = jnp.exp(log_g * C)
    xi_b = jnp.broadcast_to(xi.T[:, :, None], (H, C, RET_V_DIM))
    zeta_b = jnp.repeat(zeta, RET_QK_DIM, axis=1).reshape(C, H // 2, LANES).transpose(1, 0, 2)
    gch_b = jnp.broadcast_to(jnp.repeat(g_chunk, RET_QK_DIM).reshape(H // 2, LANES, 1),
                             (H // 2, LANES, RET_V_DIM))
    qk3, tok3, gate3 = (a.reshape(B, S, a.shape[-1]) for a in (qk, tokb, gates))
    out = pl.pallas_call(
        _retention_kernel,
        grid=(nc,),
        in_specs=[pl.BlockSpec((B, C, RET_QK_W), lambda i: (0, i, 0)),
                  pl.BlockSpec((B, C, RET_QK_W), lambda i: (0, i, 1)),
                  pl.BlockSpec((B, C, RET_V_W), lambda i: (0, i, 0)),
                  pl.BlockSpec((B, C, RET_V_W), lambda i: (0, i, 0)),
                  pl.BlockSpec((H, C, C), lambda i: (0, 0, 0)),
                  pl.BlockSpec((H, C, RET_V_DIM), lambda i: (0, 0, 0)),
                  pl.BlockSpec((H // 2, C, LANES), lambda i: (0, 0, 0)),
                  pl.BlockSpec((H // 2, LANES, RET_V_DIM), lambda i: (0, 0, 0))],
        out_specs=pl.BlockSpec((B, C, RET_V_W), lambda i: (0, i, 0)),
        out_shape=jax.ShapeDtypeStruct((B, S, RET_V_W), BF16),
        scratch_shapes=[pltpu.VMEM((B, H // 2, LANES, RET_V_DIM), F32)],
        compiler_params=_cparams(1),
        name="retention",
    )(qk3, qk3, tok3, gate3, decay_in, xi_b, zeta_b, gch_b)
    return out.reshape(B * S, RET_V_W)


def _layer_norm(z, g, b):
    mu = jnp.mean(z, axis=-1, keepdims=True)
    d = z - mu
    var = jnp.mean(d * d, axis=-1, keepdims=True)
    return d * lax.rsqrt(var + LN_EPS) * g + b


def _merge_kernel(x_ref, ya_ref, yr_ref, ga_ref, gr_ref, wa_ref, wr_ref, wo_ref, g_ref, b_ref,
                  x1_ref, x1b_ref):
    a = jnp.dot(ya_ref[...], wa_ref[...], preferred_element_type=F32)
    r = jnp.dot(yr_ref[...], wr_ref[...], preferred_element_type=F32)
    h = ga_ref[...].astype(F32) * a + gr_ref[...].astype(F32) * r
    mix = jnp.dot(h.astype(BF16), wo_ref[...], preferred_element_type=F32)
    x1 = _layer_norm(DEEPNORM_ALPHA * x_ref[...] + mix, g_ref[...], b_ref[...])
    x1_ref[...] = x1
    x1b_ref[...] = x1.astype(BF16)


def _merge(x, ya, yr, gates, wa, wr, wo, g, b, tm=512):
    T, D = x.shape
    tm = min(tm, T)
    row = lambda i: (i, 0)
    fixed = lambda i: (0, 0)
    return pl.pallas_call(
        _merge_kernel,
        grid=(T // tm,),
        in_specs=[pl.BlockSpec((tm, D), row),
                  pl.BlockSpec((tm, ya.shape[1]), row),
                  pl.BlockSpec((tm, yr.shape[1]), row),
                  pl.BlockSpec((tm, D), lambda i: (i, 1)),
                  pl.BlockSpec((tm, D), lambda i: (i, 2)),
                  pl.BlockSpec(wa.shape, fixed),
                  pl.BlockSpec(wr.shape, fixed),
                  pl.BlockSpec(wo.shape, fixed),
                  pl.BlockSpec((1, D), fixed),
                  pl.BlockSpec((1, D), fixed)],
        out_specs=[pl.BlockSpec((tm, D), row), pl.BlockSpec((tm, D), row)],
        out_shape=[jax.ShapeDtypeStruct((T, D), F32), jax.ShapeDtypeStruct((T, D), BF16)],
        compiler_params=_cparams(1),
        name="merge",
    )(x, ya, yr, gates, gates, wa, wr, wo, g, b)


def _ffn_kernel(x1b_ref, x1_ref, wu_ref, wd_ref, g_ref, b_ref, o_ref, acc_ref):
    f = pl.program_id(1)

    @pl.when(f == 0)
    def _():
        acc_ref[...] = jnp.zeros(acc_ref.shape, F32)

    hid = jnp.maximum(jnp.dot(x1b_ref[...], wu_ref[...], preferred_element_type=F32), 0.0)
    acc_ref[...] += jnp.dot((hid * hid).astype(BF16), wd_ref[...], preferred_element_type=F32)

    @pl.when(f == pl.num_programs(1) - 1)
    def _():
        o_ref[...] = _layer_norm(DEEPNORM_ALPHA * x1_ref[...] + acc_ref[...], g_ref[...], b_ref[...])


def _ffn(x1b, x1, wu, wd, g, b, tm=1024, tf=1024):
    T, D = x1.shape
    F = wu.shape[1]
    tm = min(tm, T)
    return pl.pallas_call(
        _ffn_kernel,
        grid=(T // tm, F // tf),
        in_specs=[pl.BlockSpec((tm, D), lambda i, f: (i, 0)),
                  pl.BlockSpec((tm, D), lambda i, f: (i, 0)),
                  pl.BlockSpec((D, tf), lambda i, f: (0, f)),
                  pl.BlockSpec((tf, D), lambda i, f: (f, 0)),
                  pl.BlockSpec((1, D), lambda i, f: (0, 0)),
                  pl.BlockSpec((1, D), lambda i, f: (0, 0))],
        out_specs=pl.BlockSpec((tm, D), lambda i, f: (i, 0)),
        out_shape=jax.ShapeDtypeStruct((T, D), F32),
        scratch_shapes=[pltpu.VMEM((tm, D), F32)],
        compiler_params=_cparams(2),
        name="ffn",
    )(x1b, x1, wu, wd, g, b)


def _rot_half_weight(wT):
    N, D = wT.shape
    half = RET_QK_DIM // 2
    wh = wT.reshape(N // RET_QK_DIM, 2, half, D)
    return jnp.stack([-wh[:, 1], wh[:, 0]], axis=1).reshape(N, D)


def kernel(x, positions, w_in, rel_bias, idx_k_ln_g, idx_k_ln_b, w_attn_branch, w_ret_branch,
           w_out, ln_mix_g, ln_mix_b, w_up, w_down, ln_ffn_g, ln_ffn_b):
    B, S, D = x.shape
    T = B * S
    sizes = (ATTN_W, ATTN_W, ATTN_W, IDX_Q_W, IDX_HEAD_DIM, IDX_HEADS,
             RET_QK_W, RET_QK_W, RET_V_W, RET_V_W, D, D)
    offs = [0] + [int(o) for o in np.cumsum(sizes)]
    cos, sin = _rope_tables(positions)
    xf = x.reshape(T, D)
    for l in range(DEPTH):
        wT_f32 = jnp.swapaxes(w_in[l], 0, 1)
        wT = wT_f32.astype(BF16)
        rows = [wT[offs[k]:offs[k + 1]] for k in range(len(sizes))]
        (w_qa, w_ka, w_va, w_iq, w_ik, w_iw, w_qr, w_kr, w_vr, w_gr, w_ga, w_gtr) = rows
        xb = xf.astype(BF16)
        w_qa = (wT_f32[offs[0]:offs[1]] * (ATTN_HEAD_DIM ** -0.5 * LOG2E)).astype(BF16)
        w_kr = w_kr * (RET_QK_DIM ** -0.5)

        fm = _proj_t(xb, jnp.concatenate([w_qa, w_va, w_iq], axis=0), B, S, BF16, tn=1280)
        tokb = _proj(xb, jnp.concatenate([w_vr, w_ka], axis=0), BF16, tn=1536)
        gates = _proj_gates(xb, jnp.concatenate([w_gr, w_ga, w_gtr], axis=0), tn=D)
        w_rope = jnp.concatenate([w_qr, w_kr], axis=0)
        w_rope_rot = jnp.concatenate([_rot_half_weight(w_qr), _rot_half_weight(w_kr)], axis=0)
        qk_r = _proj_rope(xb, w_rope, w_rope_rot, cos, sin, tn=1024)
        pad = LANES - IDX_HEAD_DIM - IDX_HEADS
        w_idx = jnp.concatenate([w_ik, w_iw, jnp.zeros((pad, D), BF16)], axis=0)
        g_pad = jnp.concatenate([idx_k_ln_g[l], jnp.zeros((LANES - IDX_HEAD_DIM,), F32)]).reshape(1, LANES)
        b_pad = jnp.concatenate([idx_k_ln_b[l], jnp.zeros((LANES - IDX_HEAD_DIM,), F32)]).reshape(1, LANES)
        idx = _proj_idx(xb, w_idx, g_pad, b_pad,
                        (IDX_HEAD_DIM ** -0.5) * (IDX_HEADS ** -0.5)).reshape(B, S, LANES)
        ik = idx[:, :, :IDX_HEAD_DIM].astype(BF16)
        iwT = jnp.swapaxes(idx[:, :, IDX_HEAD_DIM:IDX_HEAD_DIM + SUBLANES], 1, 2)

        y_a = _sparse_attention(fm, tokb.reshape(B, S, -1), ik, iwT, positions, rel_bias)
        y_r = _retention(qk_r, tokb, gates, B, S)
        x1, x1b = _merge(xf, y_a.reshape(T, ATTN_W), y_r, gates,
                         w_attn_branch[l].astype(BF16), w_ret_branch[l].astype(BF16),
                         w_out[l].astype(BF16), ln_mix_g[l].reshape(1, D), ln_mix_b[l].reshape(1, D))
        xf = _ffn(x1b, x1, w_up[l].astype(BF16), w_down[l].astype(BF16),
                  ln_ffn_g[l].reshape(1, D), ln_ffn_b[l].reshape(1, D))
    return xf.reshape(B, S, D)
```

```python
import functools
import math

import numpy as np
import jax
import jax.numpy as jnp
from jax import lax
from jax.experimental import pallas as pl
from jax.experimental.pallas import tpu as pltpu

F32 = jnp.float32
BF16 = jnp.bfloat16

ATTN_HEADS = 8
ATTN_HEAD_DIM = 64
ATTN_W = ATTN_HEADS * ATTN_HEAD_DIM
IDX_HEADS = 4
IDX_HEAD_DIM = 64
IDX_Q_W = IDX_HEADS * IDX_HEAD_DIM
TOPK_MAX = 256
RET_HEADS = 8
RET_QK_DIM = 64
RET_V_DIM = 128
RET_QK_W = RET_HEADS * RET_QK_DIM
RET_V_W = RET_HEADS * RET_V_DIM
RET_CHUNK = 128
ROPE_BASE = 10000.0
NUM_BUCKETS = 32
MAX_DISTANCE = 128
LN_EPS = 1e-5
DEPTH = 1
DEEPNORM_ALPHA = (2.0 * DEPTH) ** 0.25

LANES = 128
SUBLANES = 8
VMEM_LIMIT = 56 * 1024 * 1024

TQ = 128
CK = 512
NEG = -1e30
LOG2E = math.log2(math.e)
BISECT_ROUNDS = 20
BIAS_TABLE_N = 128
FAR_N = 113


def _cparams(n_grid):
    return pltpu.CompilerParams(
        dimension_semantics=("arbitrary",) * n_grid,
        vmem_limit_bytes=VMEM_LIMIT)


def _trig_kernel(pos_ref, inv_ref, cos_ref, sin_ref):
    ang = pos_ref[...] * inv_ref[...]
    cos_ref[...] = jnp.cos(ang)
    sin_ref[...] = jnp.sin(ang)


def _rope_tables(positions):
    B, S = positions.shape
    half = RET_QK_DIM // 2
    inv = ROPE_BASE ** (-jnp.arange(half, dtype=F32) / half)
    per_row = LANES // half
    rows = B * S // per_row
    pos_e = jnp.repeat(positions.astype(F32).reshape(rows, per_row), half, axis=1)
    inv_e = jnp.tile(inv, per_row).reshape(1, LANES)
    tr = min(rows, 1024)
    cos, sin = pl.pallas_call(
        _trig_kernel,
        grid=(rows // tr,),
        in_specs=[pl.BlockSpec((tr, LANES), lambda i: (i, 0)),
                  pl.BlockSpec((1, LANES), lambda i: (0, 0))],
        out_specs=[pl.BlockSpec((tr, LANES), lambda i: (i, 0))] * 2,
        out_shape=[jax.ShapeDtypeStruct((rows, LANES), F32)] * 2,
        compiler_params=_cparams(1),
        name="rope_tables",
    )(pos_e, inv_e)
    cos = jnp.tile(cos.reshape(B * S, half), (1, per_row))
    sin = jnp.tile(sin.reshape(B * S, half), (1, per_row))
    return cos, sin


def _x_wt(x, wT):
    return lax.dot_general(x, wT, (((1,), (1,)), ((), ())), preferred_element_type=F32)


def _proj_kernel(x_ref, w_ref, o_ref):
    o_ref[...] = _x_wt(x_ref[...], w_ref[...]).astype(o_ref.dtype)


def _proj(xb, wT, out_dtype, tm=1024, tn=512):
    T, D = xb.shape
    N = wT.shape[0]
    tn = min(N, tn)
    tm = min(tm, T)
    return pl.pallas_call(
        _proj_kernel,
        grid=(T // tm, N // tn),
        in_specs=[pl.BlockSpec((tm, D), lambda i, j: (i, 0)),
                  pl.BlockSpec((tn, D), lambda i, j: (j, 0))],
        out_specs=pl.BlockSpec((tm, tn), lambda i, j: (i, j)),
        out_shape=jax.ShapeDtypeStruct((T, N), out_dtype),
        compiler_params=_cparams(2),
        name="proj",
    )(xb, wT)


def _proj_gates_kernel(x_ref, w_ref, o_ref):
    acc = _x_wt(x_ref[...], w_ref[...])
    sig = 0.5 * jnp.tanh(0.5 * acc) + 0.5
    o_ref[...] = jnp.where(pl.program_id(1) == 0, acc * sig, sig).astype(o_ref.dtype)


def _proj_gates(xb, wT, tn, tm=1024):
    T, D = xb.shape
    N = wT.shape[0]
    tm = min(tm, T)
    return pl.pallas_call(
        _proj_gates_kernel,
        grid=(T // tm, N // tn),
        in_specs=[pl.BlockSpec((tm, D), lambda i, j: (i, 0)),
                  pl.BlockSpec((tn, D), lambda i, j: (j, 0))],
        out_specs=pl.BlockSpec((tm, tn), lambda i, j: (i, j)),
        out_shape=jax.ShapeDtypeStruct((T, N), BF16),
        compiler_params=_cparams(2),
        name="proj_gates",
    )(xb, wT)


def _proj_t_kernel(wT_ref, x_ref, o_ref):
    acc = lax.dot_general(wT_ref[...], x_ref[...], (((1,), (1,)), ((), ())), preferred_element_type=F32)
    o_ref[...] = acc.astype(o_ref.dtype)


def _proj_t(xb, wT, B, S, out_dtype, tm=1024, tn=640):
    T, D = xb.shape
    N = wT.shape[0]
    tn = min(N, tn)
    tm = min(tm, S)
    nsb = S // tm
    return pl.pallas_call(
        _proj_t_kernel,
        grid=(T // tm, N // tn),
        in_specs=[pl.BlockSpec((tn, D), lambda i, j: (j, 0)),
                  pl.BlockSpec((tm, D), lambda i, j: (i, 0))],
        out_specs=pl.BlockSpec((pl.Squeezed(), tn, tm), lambda i, j: (i // nsb, j, i % nsb)),
        out_shape=jax.ShapeDtypeStruct((B, N, S), out_dtype),
        compiler_params=_cparams(2),
        name="proj_t",
    )(wT, xb)


def _proj_rope_kernel(x_ref, w_ref, wr_ref, cos_ref, sin_ref, o_ref):
    x = x_ref[...]
    a = _x_wt(x, w_ref[...])
    r = _x_wt(x, wr_ref[...])
    reps = a.shape[1] // LANES
    cos = jnp.concatenate([cos_ref[...]] * reps, axis=1)
    sin = jnp.concatenate([sin_ref[...]] * reps, axis=1)
    o_ref[...] = (a * cos + r * sin).astype(o_ref.dtype)


def _proj_rope(xb, wT, wT_rot, cos, sin, tm=1024, tn=512):
    T, D = xb.shape
    N = wT.shape[0]
    tm = min(tm, T)
    return pl.pallas_call(
        _proj_rope_kernel,
        grid=(T // tm, N // tn),
        in_specs=[pl.BlockSpec((tm, D), lambda i, j: (i, 0)),
                  pl.BlockSpec((tn, D), lambda i, j: (j, 0)),
                  pl.BlockSpec((tn, D), lambda i, j: (j, 0)),
                  pl.BlockSpec((tm, LANES), lambda i, j: (i, 0)),
                  pl.BlockSpec((tm, LANES), lambda i, j: (i, 0))],
        out_specs=pl.BlockSpec((tm, tn), lambda i, j: (i, j)),
        out_shape=jax.ShapeDtypeStruct((T, N), BF16),
        compiler_params=_cparams(2),
        name="proj_rope",
    )(xb, wT, wT_rot, cos, sin)


def _proj_idx_kernel(x_ref, w_ref, g_ref, b_ref, o_ref, *, iw_scale):
    acc = _x_wt(x_ref[...], w_ref[...])
    lane = lax.broadcasted_iota(jnp.int32, acc.shape, 1)
    is_k = lane < IDX_HEAD_DIM
    mu = jnp.sum(jnp.where(is_k, acc, 0.0), axis=-1, keepdims=True) / IDX_HEAD_DIM
    d = acc - mu
    var = jnp.sum(jnp.where(is_k, d * d, 0.0), axis=-1, keepdims=True) / IDX_HEAD_DIM
    ln = d * lax.rsqrt(var + LN_EPS) * g_ref[...] + b_ref[...]
    o_ref[...] = jnp.where(is_k, ln, acc * iw_scale)


def _proj_idx(xb, w_pad, g_pad, b_pad, iw_scale, tm=1024):
    T, D = xb.shape
    tm = min(tm, T)
    return pl.pallas_call(
        functools.partial(_proj_idx_kernel, iw_scale=iw_scale),
        grid=(T // tm,),
        in_specs=[pl.BlockSpec((tm, D), lambda i: (i, 0)),
                  pl.BlockSpec((LANES, D), lambda i: (0, 0)),
                  pl.BlockSpec((1, LANES), lambda i: (0, 0)),
                  pl.BlockSpec((1, LANES), lambda i: (0, 0))],
        out_specs=pl.BlockSpec((tm, LANES), lambda i: (i, 0)),
        out_shape=jax.ShapeDtypeStruct((T, LANES), F32),
        compiler_params=_cparams(1),
        name="proj_idx",
    )(xb, w_pad, g_pad, b_pad)


def _t5_bucket_table():
    n = np.arange(BIAS_TABLE_N)
    max_exact = NUM_BUCKETS // 2
    nf = np.maximum(n, 1).astype(np.float64)
    large = max_exact + (np.log(nf / max_exact) / math.log(MAX_DISTANCE / max_exact)
                         * (NUM_BUCKETS - max_exact)).astype(np.int64)
    large = np.minimum(large, NUM_BUCKETS - 1)
    bucket = np.where(n < max_exact, n, large)
    assert np.all(bucket[FAR_N:] == NUM_BUCKETS - 1) and bucket[FAR_N - 1] != NUM_BUCKETS - 1
    return bucket.astype(np.int32)


def _fold_rows(a, op):
    parts = [a[r:r + SUBLANES] for r in range(0, a.shape[0], SUBLANES)]
    while len(parts) > 1:
        nxt = [op(parts[k], parts[k + 1]) for k in range(0, len(parts) - 1, 2)]
        if len(parts) % 2:
            nxt.append(parts[-1])
        parts = nxt
    return parts[0]


def _attn_kernel(biasc_ref, pinfo_ref, ckmax_ref, iqT_ref, iwT_ref, posqc_ref, posk_ref, ik_ref, qT_ref, k_ref, vT_ref,
                 tbl_ref, toep_ref, tril_ref, o_ref, sc_ref, s_ref, s2_ref, b_ref, qm_ref, m_ref, l_ref, *acc_refs, k_sel, seq):
    i = pl.program_id(1)
    nch = (i * TQ + TQ + CK - 1) // CK
    q_idx = i * TQ + lax.broadcasted_iota(jnp.int32, (1, TQ), 1)
    kf = float(k_sel)

    def chunk_off(c):
        return pl.multiple_of(c * CK, CK)

    def key_idx(off):
        return off + lax.broadcasted_iota(jnp.int32, (CK, TQ), 0)

    def col_reduce(part, op):
        return op(part, axis=0, keepdims=True)

    iqT = iqT_ref[...]
    iwT = iwT_ref[...]
    iq_wide = jnp.concatenate([iqT[h * IDX_HEAD_DIM:(h + 1) * IDX_HEAD_DIM, :] for h in range(IDX_HEADS)], axis=1)

    def score_body(c, carry, masked):
        mn, mx = carry
        off = chunk_off(c)
        ikc = ik_ref[pl.ds(off, CK), :]
        z = jnp.dot(ikc, iq_wide, preferred_element_type=F32)
        s = None
        for h in range(IDX_HEADS):
            t = jnp.maximum(z[:, h * TQ:(h + 1) * TQ], 0.0) * iwT[h:h + 1, :]
            s = t if s is None else s + t
        if masked:
            causal = key_idx(off) <= q_idx
            s_lo = jnp.where(causal, s, -jnp.inf)
            s_hi = jnp.where(causal, s, jnp.inf)
        else:
            s_lo = s_hi = s
        sc_ref[pl.ds(off, CK), :] = s_lo
        mn = jnp.minimum(mn, _fold_rows(s_hi, jnp.minimum))
        mx = jnp.maximum(mx, _fold_rows(s_lo, jnp.maximum))
        return mn, mx

    mn8, mx8 = lax.fori_loop(0, nch - 1, functools.partial(score_body, masked=False),
                             (jnp.full((SUBLANES, TQ), jnp.inf, F32), jnp.full((SUBLANES, TQ), -jnp.inf, F32)))
    mn8, mx8 = score_body(nch - 1, (mn8, mx8), masked=True)
    mn = col_reduce(mn8, jnp.min)
    mx = col_reduce(mx8, jnp.max)

    def over_chunks(fn, init):
        return lax.fori_loop(0, nch, lambda c, carry: fn(sc_ref[pl.ds(chunk_off(c), CK), :], carry), init)

    def count(pred_fn):
        acc = over_chunks(lambda blk, acc: acc + _fold_rows(jnp.where(pred_fn(blk), 1.0, 0.0), jnp.add),
                          jnp.zeros((SUBLANES, TQ), F32))
        return col_reduce(acc, jnp.sum)

    def bisect_round(_, st):
        lo, hi, c_lo = st
        mid = 0.5 * (lo + hi)
        c = count(lambda blk: blk >= mid)
        ok = c >= kf
        return jnp.where(ok, mid, lo), jnp.where(ok, hi, mid), jnp.where(ok, c, c_lo)

    c_all = (q_idx + 1).astype(F32)
    lo, hi, c_lo = lax.fori_loop(0, BISECT_ROUNDS, bisect_round, (mn, mx, c_all))

    cur0 = col_reduce(over_chunks(
        lambda blk, acc: jnp.minimum(acc, _fold_rows(jnp.where(blk >= lo, blk, jnp.inf), jnp.minimum)),
        jnp.full((SUBLANES, TQ), jnp.inf, F32)), jnp.min)

    def walk_cond(st):
        return st[3] > 0.0

    def walk_body(st):
        cur, c_ge, _, _ = st

        def body(blk, carry):
            cnt, nxt = carry
            gt = blk > cur
            cnt = cnt + _fold_rows(jnp.where(gt, 1.0, 0.0), jnp.add)
            nxt = jnp.minimum(nxt, _fold_rows(jnp.where(gt, blk, jnp.inf), jnp.minimum))
            return cnt, nxt

        cnt, nxt = over_chunks(body, (jnp.zeros((SUBLANES, TQ), F32), jnp.full((SUBLANES, TQ), jnp.inf, F32)))
        c_gt = col_reduce(cnt, jnp.sum)
        nxt = col_reduce(nxt, jnp.min)
        adv = c_gt >= kf
        cur = jnp.where(adv, nxt, cur)
        c_ge = jnp.where(adv, c_gt, c_ge)
        return cur, c_ge, c_gt, jnp.max(jnp.where(adv, 1.0, 0.0))

    tau, c_ge, c_gt, _ = lax.while_loop(
        walk_cond, walk_body, (cur0, c_lo, jnp.zeros((1, TQ), F32), jnp.float32(1.0)))

    room = kf - c_gt

    def selection_mask(c, seen):
        off = chunk_off(c)
        tiles = range(CK // LANES)
        blks = [sc_ref[pl.ds(off + j * LANES, LANES), :] for j in tiles]
        ties = [blk == tau for blk in blks]
        local = [jnp.dot(tril_ref[...], jnp.where(tie, 1.0, 0.0).astype(BF16), preferred_element_type=F32)
                 for tie in ties]
        masks = []
        for j in tiles:
            keep = (blks[j] > tau) | (ties[j] & (local[j] + seen <= room))
            masks.append(jnp.where(keep, 0.0, NEG))
            seen = seen + local[j][LANES - 1:LANES, :]
        return jnp.concatenate(masks, axis=0), seen

    m_ref[...] = jnp.full(m_ref.shape, NEG, F32)
    l_ref[...] = jnp.zeros(l_ref.shape, F32)
    for acc in acc_refs:
        acc[...] = jnp.zeros(acc.shape, F32)

    rowi = lax.broadcasted_iota(jnp.int32, (LANES, TQ), 0)
    for pair in range(ATTN_HEADS // 2):
        qp = qT_ref[pair * LANES:(pair + 1) * LANES, :]
        zero = jnp.zeros_like(qp)
        qm_ref[pair] = jnp.concatenate([jnp.where(rowi < ATTN_HEAD_DIM, qp, zero),
                                        jnp.where(rowi >= ATTN_HEAD_DIM, qp, zero)], axis=1)
    ones_rows = jnp.ones((2 * SUBLANES, CK), BF16)

    pq_col = posqc_ref[...]
    batch = pl.program_id(0)
    pq_first, pq_consec, pq_min = (pinfo_ref[batch, r, i] for r in (0, 1, 3))
    far_bias = [biasc_ref[0, h] for h in range(ATTN_HEADS)]
    zero_bias = [biasc_ref[1, h] for h in range(ATTN_HEADS)]
    q_ge_k = (lax.broadcasted_iota(jnp.int32, (LANES, TQ), 1) >= lax.broadcasted_iota(jnp.int32, (LANES, TQ), 0))
    n_sub = CK // LANES
    n_pairs = ATTN_HEADS // 2

    def chunk_is_far(c):
        return (pq_min - ckmax_ref[batch, jnp.minimum(c, seq // CK - 1)]) >= FAR_N

    def stage_bias(c):
        off = chunk_off(c)
        pk_row = posk_ref[:, pl.ds(off, CK)]
        for j in range(n_sub):
            rows = slice(j * LANES, (j + 1) * LANES)
            g = c * n_sub + j
            pk_first, pk_consec, pk_max = (pinfo_ref[batch, r, g] for r in (0, 1, 2))
            all_far = (pq_min - pk_max) >= FAR_N
            all_masked = (off + j * LANES) > (i * TQ + TQ - 1)
            consecutive = (pq_consec > 0) & (pk_consec > 0)
            gap = pq_first - pk_first

            def fill_const(rows=rows):
                for h in range(ATTN_HEADS):
                    b_ref[h, rows, :] = jnp.full((LANES, TQ), far_bias[h], F32)

            def fill_gap0(rows=rows):
                for h in range(ATTN_HEADS):
                    b_ref[h, rows, :] = jnp.where(q_ge_k, toep_ref[h], zero_bias[h])

            def fill_gap128(rows=rows):
                for h in range(ATTN_HEADS):
                    b_ref[h, rows, :] = jnp.where(q_ge_k, far_bias[h], toep_ref[h])

            def fill_lookup(rows=rows):
                pk_sub = pk_row[:, rows]
                n_qk = jnp.clip(pq_col - pk_sub, 0, BIAS_TABLE_N - 1).astype(F32)
                n_kq = n_qk.T.astype(jnp.int32)
                for h in range(ATTN_HEADS):
                    tb = jnp.broadcast_to(tbl_ref[h:h + 1, :], (LANES, BIAS_TABLE_N))
                    b_ref[h, rows, :] = jnp.take_along_axis(tb, n_kq, axis=1)

            def fill_near(fill_gap0=fill_gap0, fill_gap128=fill_gap128, fill_lookup=fill_lookup,
                          consecutive=consecutive, gap=gap):
                lax.cond(consecutive & (gap == 0), fill_gap0,
                         lambda: lax.cond(consecutive & (gap == LANES), fill_gap128, fill_lookup))

            lax.cond(all_far | all_masked, fill_const, fill_near)

    def logits_phase(c, s_buf, const_bias, seen):
        off = chunk_off(c)
        mb, seen = selection_mask(c, seen)
        m_cur = []
        for pair in range(n_pairs):
            kc = k_ref[pl.ds(off, CK), pair * LANES:(pair + 1) * LANES]
            s2 = jnp.dot(kc, qm_ref[pair], preferred_element_type=F32)
            for sub, h in enumerate((2 * pair, 2 * pair + 1)):
                s = s2[:, sub * TQ:(sub + 1) * TQ] + mb
                if not const_bias:
                    s = s + b_ref[h]
                s_buf[h] = s
                top = col_reduce(_fold_rows(s, jnp.maximum), jnp.max)
                m_cur.append(top + far_bias[h] if const_bias else top)
        return m_cur, seen

    def update_phase(c, s_buf, m_cur, const_bias):
        off = chunk_off(c)
        m_prev = m_ref[...]
        l_prev = l_ref[...]
        if isinstance(m_cur, list):
            m_new = [jnp.maximum(m_prev[h:h + 1, :], m_cur[h]) for h in range(ATTN_HEADS)]
            alpha = [jnp.exp2(m_prev[h:h + 1, :] - m_new[h]) for h in range(ATTN_HEADS)]
        else:
            m_all = jnp.maximum(m_prev, m_cur)
            a_all = jnp.exp2(m_prev - m_all)
            m_new = [m_all[h:h + 1, :] for h in range(ATTN_HEADS)]
            alpha = [a_all[h:h + 1, :] for h in range(ATTN_HEADS)]
        l_new = []
        for pair in range(n_pairs):
            heads = (2 * pair, 2 * pair + 1)
            sub_m = [m_new[h] - far_bias[h] if const_bias else m_new[h] for h in heads]
            p2 = jnp.concatenate([jnp.exp2(s_buf[h] - sm).astype(BF16) for h, sm in zip(heads, sub_m)],
                                 axis=1)
            lhs = jnp.concatenate([vT_ref[pair * LANES:(pair + 1) * LANES, pl.ds(off, CK)], ones_rows],
                                  axis=0)
            out = jnp.dot(lhs, p2, preferred_element_type=F32)
            for sub, h in enumerate(heads):
                cols = slice(sub * TQ, (sub + 1) * TQ)
                acc = acc_refs[h]
                acc[...] = alpha[h] * acc[...] + out[sub * ATTN_HEAD_DIM:(sub + 1) * ATTN_HEAD_DIM, cols]
                l_new.append(alpha[h] * l_prev[h:h + 1, :] + out[LANES:LANES + 1, cols])
        l_ref[...] = jnp.concatenate(l_new, axis=0)
        m_ref[...] = jnp.concatenate(m_new, axis=0)

    n_far = lax.while_loop(lambda c: (c < nch) & chunk_is_far(c), lambda c: c + 1, jnp.int32(0))
    n_steps = n_far // 2
    def far_logits(c, s_buf, seen):
        m_cur, seen = logits_phase(c, s_buf, True, seen)
        return jnp.concatenate(m_cur, axis=0), seen

    no_ties = jnp.zeros((1, TQ), F32)
    m_first, seen_first = lax.cond(n_steps > 0, lambda: far_logits(0, s_ref, no_ties),
                                   lambda: (jnp.zeros((ATTN_HEADS, TQ), F32), no_ties))

    def far_step(t, carry):
        m_even, seen_even, _ = carry
        c = 2 * t
        m_odd, seen_odd = far_logits(c + 1, s2_ref, seen_even)
        update_phase(c, s_ref, m_even, True)
        m_even, seen_even = far_logits(jnp.minimum(c + 2, 2 * n_steps - 2), s_ref, seen_odd)
        update_phase(c + 1, s2_ref, m_odd, True)
        return m_even, seen_even, seen_odd

    _, _, seen = lax.fori_loop(0, n_steps, far_step, (m_first, seen_first, no_ties))

    def tail_chunk(c, seen):
        def run(const_bias):
            if not const_bias:
                stage_bias(c)
            m_cur, seen_out = logits_phase(c, s_ref, const_bias, seen)
            update_phase(c, s_ref, m_cur, const_bias)
            return seen_out

        return lax.cond(chunk_is_far(c), lambda: run(True), lambda: run(False))

    lax.fori_loop(2 * n_steps, nch, tail_chunk, seen)

    outT = jnp.concatenate([acc_refs[h][...] / l_ref[h:h + 1, :] for h in range(ATTN_HEADS)], axis=0)
    o_ref[...] = outT.T.astype(o_ref.dtype)


def _toeplitz_kernel(tbl_ref, o_ref):
    q = lax.broadcasted_iota(jnp.int32, (LANES, LANES), 1)
    k = lax.broadcasted_iota(jnp.int32, (LANES, LANES), 0)
    idx = (q - k) & (BIAS_TABLE_N - 1)
    for h in range(ATTN_HEADS):
        tb = jnp.broadcast_to(tbl_ref[h:h + 1, :], (LANES, BIAS_TABLE_N))
        o_ref[h] = jnp.take_along_axis(tb, idx, axis=1)


def _sparse_attention(fm, tokb, ik, iwT, positions, rel_bias):
    B, S, _ = tokb.shape
    k_sel = min(TOPK_MAX, S // 4)
    bucket = _t5_bucket_table()
    tbl = rel_bias[bucket].T.astype(F32) * LOG2E
    bias_c = jnp.stack([tbl[:, BIAS_TABLE_N - 1], tbl[:, 0]])
    toep = pl.pallas_call(
        _toeplitz_kernel,
        out_shape=jax.ShapeDtypeStruct((ATTN_HEADS, LANES, LANES), F32),
        name="bias_toeplitz",
    )(tbl)
    pos_row = positions.reshape(B, 1, S)
    pos_col = positions.reshape(B, S, 1)
    pos_t = positions.reshape(B, S // LANES, LANES)
    consec = jnp.all(pos_t - pos_t[:, :, :1] == jnp.arange(LANES, dtype=positions.dtype), axis=-1)
    pinfo = jnp.stack([pos_t[:, :, 0], consec.astype(jnp.int32), jnp.max(pos_t, axis=-1),
                       jnp.min(pos_t, axis=-1)], axis=1).astype(jnp.int32)
    ck_max = jnp.max(positions.reshape(B, S // CK, CK), axis=-1)
    tril = jnp.tril(jnp.ones((LANES, LANES), BF16))
    sq = pl.Squeezed()
    iq_blk = (2 * ATTN_W) // IDX_Q_W
    k_blk = (tokb.shape[2] - ATTN_W) // ATTN_W
    return pl.pallas_call(
        functools.partial(_attn_kernel, k_sel=k_sel, seq=S),
        grid=(B, S // TQ),
        in_specs=[pl.BlockSpec(memory_space=pltpu.SMEM),
                  pl.BlockSpec(memory_space=pltpu.SMEM),
                  pl.BlockSpec(memory_space=pltpu.SMEM),
                  pl.BlockSpec((sq, IDX_Q_W, TQ), lambda b, i: (b, iq_blk, i)),
                  pl.BlockSpec((sq, SUBLANES, TQ), lambda b, i: (b, 0, i)),
                  pl.BlockSpec((sq, TQ, 1), lambda b, i: (b, i, 0)),
                  pl.BlockSpec((sq, 1, S), lambda b, i: (b, 0, 0)),
                  pl.BlockSpec((sq, S, IDX_HEAD_DIM), lambda b, i: (b, 0, 0)),
                  pl.BlockSpec((sq, ATTN_W, TQ), lambda b, i: (b, 0, i)),
                  pl.BlockSpec((sq, S, ATTN_W), lambda b, i: (b, 0, k_blk)),
                  pl.BlockSpec((sq, ATTN_W, S), lambda b, i: (b, 1, 0)),
                  pl.BlockSpec((ATTN_HEADS, BIAS_TABLE_N), lambda b, i: (0, 0)),
                  pl.BlockSpec((ATTN_HEADS, LANES, LANES), lambda b, i: (0, 0, 0)),
                  pl.BlockSpec((LANES, LANES), lambda b, i: (0, 0))],
        out_specs=pl.BlockSpec((sq, TQ, ATTN_W), lambda b, i: (b, i, 0)),
        out_shape=jax.ShapeDtypeStruct((B, S, ATTN_W), BF16),
        scratch_shapes=[pltpu.VMEM((S, TQ), F32),
                        pltpu.VMEM((ATTN_HEADS, CK, TQ), F32),
                        pltpu.VMEM((ATTN_HEADS, CK, TQ), F32),
                        pltpu.VMEM((ATTN_HEADS, CK, TQ), F32),
                        pltpu.VMEM((ATTN_HEADS // 2, LANES, 2 * TQ), BF16),
                        pltpu.VMEM((ATTN_HEADS, TQ), F32),
                        pltpu.VMEM((ATTN_HEADS, TQ), F32)]
                       + [pltpu.VMEM((ATTN_HEAD_DIM, TQ), F32)] * ATTN_HEADS,
        compiler_params=_cparams(2),
        name="sparse_attention",
    )(bias_c, pinfo, ck_max, fm, iwT, pos_col, pos_row, ik, fm, tokb, fm, tbl, toep, tril)


def _retention_kernel(q_ref, k_ref, v_ref, g_ref, decay_ref, xi_ref, zeta_ref, gch_ref, o_ref, r_ref):
    @pl.when(pl.program_id(0) == 0)
    def _():
        r_ref[...] = jnp.zeros(r_ref.shape, F32)

    C = RET_CHUNK
    lane = lax.broadcasted_iota(jnp.int32, (C, LANES), 1)
    row = lax.broadcasted_iota(jnp.int32, (LANES, RET_V_DIM), 0)
    for pair in range(RET_HEADS // 2):
        for b in range(q_ref.shape[0]):
            q_pair = q_ref[b, :, pair * LANES:(pair + 1) * LANES]
            k_pair = k_ref[b, :, pair * LANES:(pair + 1) * LANES]
            v_pair = v_ref[b, :, 2 * pair * RET_V_DIM:(2 * pair + 2) * RET_V_DIM]
            r_pair = r_ref[b, pair]
            r_bf = r_pair.astype(BF16)
            for sub in range(2):
                h = 2 * pair + sub
                in_head = (lane >= sub * RET_QK_DIM) & (lane < (sub + 1) * RET_QK_DIM)
                qm = jnp.where(in_head, q_pair, jnp.zeros_like(q_pair))
                v_h = v_pair[:, sub * RET_V_DIM:(sub + 1) * RET_V_DIM]
                inner = lax.dot_general(qm, k_pair, (((1,), (1,)), ((), ())),
                                        preferred_element_type=F32) * decay_ref[h]
                o = (jnp.dot(inner.astype(BF16), v_h, preferred_element_type=F32)
                     + jnp.dot(qm, r_bf, preferred_element_type=F32) * xi_ref[h])
                mu = jnp.mean(o, axis=-1, keepdims=True)
                d = o - mu
                var = jnp.mean(d * d, axis=-1, keepdims=True)
                hn = d * lax.rsqrt(var + LN_EPS)
                gate = g_ref[b, :, h * RET_V_DIM:(h + 1) * RET_V_DIM].astype(F32)
                o_ref[b, :, h * RET_V_DIM:(h + 1) * RET_V_DIM] = (gate * hn).astype(o_ref.dtype)
            kz = (k_pair.astype(F32) * zeta_ref[pair]).astype(BF16)
            upd = lax.dot_general(kz, v_pair, (((0,), (0,)), ((), ())), preferred_element_type=F32)
            r_ref[b, pair] = (r_pair * gch_ref[pair]
                              + jnp.where(row < RET_QK_DIM, upd[:, :RET_V_DIM], upd[:, RET_V_DIM:]))


def _retention(qk, tokb, gates, B, S):
    C = RET_CHUNK
    H = RET_HEADS
    nc = S // C
    gamma = 1.0 - 2.0 ** (-5.0 - jnp.arange(H, dtype=F32))
    log_g = jnp.log(gamma)
    n = jnp.arange(C, dtype=F32)
    diff = n[:, None] - n[None, :]
    decay_in = jnp.where(diff[None] >= 0, jnp.exp(log_g[:, None, None] * jnp.maximum(diff, 0.0)[None]), 0.0)
    xi = jnp.exp(log_g[None, :] * (n[:, None] + 1.0))
    zeta = jnp.exp(log_g[None, :] * (C - 1.0 - n[:, None]))
    g_chunk = jnp.exp(log_g * C)
    xi_b = jnp.broadcast_to(xi.T[:, :, None], (H, C, RET_V_DIM))
    zeta_b = jnp.repeat(zeta, RET_QK_DIM, axis=1).reshape(C, H // 2, LANES).transpose(1, 0, 2)
    gch_b = jnp.broadcast_to(jnp.repeat(g_chunk, RET_QK_DIM).reshape(H // 2, LANES, 1),
                             (H // 2, LANES, RET_V_DIM))
    qk3, tok3, gate3 = (a.reshape(B, S, a.shape[-1]) for a in (qk, tokb, gates))
    out = pl.pallas_call(
        _retention_kernel,
        grid=(nc,),
        in_specs=[pl.BlockSpec((B, C, RET_QK_W), lambda i: (0, i, 0)),
                  pl.BlockSpec((B, C, RET_QK_W), lambda i: (0, i, 1)),
                  pl.BlockSpec((B, C, RET_V_W), lambda i: (0, i, 0)),
                  pl.BlockSpec((B, C, RET_V_W), lambda i: (0, i, 0)),
                  pl.BlockSpec((H, C, C), lambda i: (0, 0, 0)),
                  pl.BlockSpec((H, C, RET_V_DIM), lambda i: (0, 0, 0)),
                  pl.BlockSpec((H // 2, C, LANES), lambda i: (0, 0, 0)),
                  pl.BlockSpec((H // 2, LANES, RET_V_DIM), lambda i: (0, 0, 0))],
        out_specs=pl.BlockSpec((B, C, RET_V_W), lambda i: (0, i, 0)),
        out_shape=jax.ShapeDtypeStruct((B, S, RET_V_W), BF16),
        scratch_shapes=[pltpu.VMEM((B, H // 2, LANES, RET_V_DIM), F32)],
        compiler_params=_cparams(1),
        name="retention",
    )(qk3, qk3, tok3, gate3, decay_in, xi_b, zeta_b, gch_b)
    return out.reshape(B * S, RET_V_W)


def _layer_norm(z, g, b):
    mu = jnp.mean(z, axis=-1, keepdims=True)
    d = z - mu
    var = jnp.mean(d * d, axis=-1, keepdims=True)
    return d * lax.rsqrt(var + LN_EPS) * g + b


def _merge_kernel(x_ref, ya_ref, yr_ref, ga_ref, gr_ref, wa_ref, wr_ref, wo_ref, g_ref, b_ref,
                  x1_ref, x1b_ref):
    a = jnp.dot(ya_ref[...], wa_ref[...], preferred_element_type=F32)
    r = jnp.dot(yr_ref[...], wr_ref[...], preferred_element_type=F32)
    h = ga_ref[...].astype(F32) * a + gr_ref[...].astype(F32) * r
    mix = jnp.dot(h.astype(BF16), wo_ref[...], preferred_element_type=F32)
    x1 = _layer_norm(DEEPNORM_ALPHA * x_ref[...] + mix, g_ref[...], b_ref[...])
    x1_ref[...] = x1
    x1b_ref[...] = x1.astype(BF16)


def _merge(x, ya, yr, gates, wa, wr, wo, g, b, tm=512):
    T, D = x.shape
    tm = min(tm, T)
    row = lambda i: (i, 0)
    fixed = lambda i: (0, 0)
    return pl.pallas_call(
        _merge_kernel,
        grid=(T // tm,),
        in_specs=[pl.BlockSpec((tm, D), row),
                  pl.BlockSpec((tm, ya.shape[1]), row),
                  pl.BlockSpec((tm, yr.shape[1]), row),
                  pl.BlockSpec((tm, D), lambda i: (i, 1)),
                  pl.BlockSpec((tm, D), lambda i: (i, 2)),
                  pl.BlockSpec(wa.shape, fixed),
                  pl.BlockSpec(wr.shape, fixed),
                  pl.BlockSpec(wo.shape, fixed),
                  pl.BlockSpec((1, D), fixed),
                  pl.BlockSpec((1, D), fixed)],
        out_specs=[pl.BlockSpec((tm, D), row), pl.BlockSpec((tm, D), row)],
        out_shape=[jax.ShapeDtypeStruct((T, D), F32), jax.ShapeDtypeStruct((T, D), BF16)],
        compiler_params=_cparams(1),
        name="merge",
    )(x, ya, yr, gates, gates, wa, wr, wo, g, b)


def _ffn_kernel(x1b_ref, x1_ref, wu_ref, wd_ref, g_ref, b_ref, o_ref, acc_ref):
    f = pl.program_id(1)

    @pl.when(f == 0)
    def _():
        acc_ref[...] = jnp.zeros(acc_ref.shape, F32)

    hid = jnp.maximum(jnp.dot(x1b_ref[...], wu_ref[...], preferred_element_type=F32), 0.0)
    acc_ref[...] += jnp.dot((hid * hid).astype(BF16), wd_ref[...], preferred_element_type=F32)

    @pl.when(f == pl.num_programs(1) - 1)
    def _():
        o_ref[...] = _layer_norm(DEEPNORM_ALPHA * x1_ref[...] + acc_ref[...], g_ref[...], b_ref[...])


def _ffn(x1b, x1, wu, wd, g, b, tm=1024, tf=1024):
    T, D = x1.shape
    F = wu.shape[1]
    tm = min(tm, T)
    return pl.pallas_call(
        _ffn_kernel,
        grid=(T // tm, F // tf),
        in_specs=[pl.BlockSpec((tm, D), lambda i, f: (i, 0)),
                  pl.BlockSpec((tm, D), lambda i, f: (i, 0)),
                  pl.BlockSpec((D, tf), lambda i, f: (0, f)),
                  pl.BlockSpec((tf, D), lambda i, f: (f, 0)),
                  pl.BlockSpec((1, D), lambda i, f: (0, 0)),
                  pl.BlockSpec((1, D), lambda i, f: (0, 0))],
        out_specs=pl.BlockSpec((tm, D), lambda i, f: (i, 0)),
        out_shape=jax.ShapeDtypeStruct((T, D), F32),
        scratch_shapes=[pltpu.VMEM((tm, D), F32)],
        compiler_params=_cparams(2),
        name="ffn",
    )(x1b, x1, wu, wd, g, b)


def _rot_half_weight(wT):
    N, D = wT.shape
    half = RET_QK_DIM // 2
    wh = wT.reshape(N // RET_QK_DIM, 2, half, D)
    return jnp.stack([-wh[:, 1], wh[:, 0]], axis=1).reshape(N, D)


def kernel(x, positions, w_in, rel_bias, idx_k_ln_g, idx_k_ln_b, w_attn_branch, w_ret_branch,
           w_out, ln_mix_g, ln_mix_b, w_up, w_down, ln_ffn_g, ln_ffn_b):
    B, S, D = x.shape
    T = B * S
    sizes = (ATTN_W, ATTN_W, ATTN_W, IDX_Q_W, IDX_HEAD_DIM, IDX_HEADS,
             RET_QK_W, RET_QK_W, RET_V_W, RET_V_W, D, D)
    offs = [0] + [int(o) for o in np.cumsum(sizes)]
    cos, sin = _rope_tables(positions)
    xf = x.reshape(T, D)
    for l in range(DEPTH):
        wT_f32 = jnp.swapaxes(w_in[l], 0, 1)
        wT = wT_f32.astype(BF16)
        rows = [wT[offs[k]:offs[k + 1]] for k in range(len(sizes))]
        (w_qa, w_ka, w_va, w_iq, w_ik, w_iw, w_qr, w_kr, w_vr, w_gr, w_ga, w_gtr) = rows
        xb = xf.astype(BF16)
        w_qa = (wT_f32[offs[0]:offs[1]] * (ATTN_HEAD_DIM ** -0.5 * LOG2E)).astype(BF16)
        w_kr = w_kr * (RET_QK_DIM ** -0.5)

        fm = _proj_t(xb, jnp.concatenate([w_qa, w_va, w_iq], axis=0), B, S, BF16, tn=1280)
        tokb = _proj(xb, jnp.concatenate([w_vr, w_ka], axis=0), BF16, tn=1536)
        gates = _proj_gates(xb, jnp.concatenate([w_gr, w_ga, w_gtr], axis=0), tn=D)
        w_rope = jnp.concatenate([w_qr, w_kr], axis=0)
        w_rope_rot = jnp.concatenate([_rot_half_weight(w_qr), _rot_half_weight(w_kr)], axis=0)
        qk_r = _proj_rope(xb, w_rope, w_rope_rot, cos, sin, tn=1024)
        pad = LANES - IDX_HEAD_DIM - IDX_HEADS
        w_idx = jnp.concatenate([w_ik, w_iw, jnp.zeros((pad, D), BF16)], axis=0)
        g_pad = jnp.concatenate([idx_k_ln_g[l], jnp.zeros((LANES - IDX_HEAD_DIM,), F32)]).reshape(1, LANES)
        b_pad = jnp.concatenate([idx_k_ln_b[l], jnp.zeros((LANES - IDX_HEAD_DIM,), F32)]).reshape(1, LANES)
        idx = _proj_idx(xb, w_idx, g_pad, b_pad,
                        (IDX_HEAD_DIM ** -0.5) * (IDX_HEADS ** -0.5)).reshape(B, S, LANES)
        ik = idx[:, :, :IDX_HEAD_DIM].astype(BF16)
        iwT = jnp.swapaxes(idx[:, :, IDX_HEAD_DIM:IDX_HEAD_DIM + SUBLANES], 1, 2)

        y_a = _sparse_attention(fm, tokb.reshape(B, S, -1), ik, iwT, positions, rel_bias)
        y_r = _retention(qk_r, tokb, gates, B, S)
        x1, x1b = _merge(xf, y_a.reshape(T, ATTN_W), y_r, gates,
                         w_attn_branch[l].astype(BF16), w_ret_branch[l].astype(BF16),
                         w_out[l].astype(BF16), ln_mix_g[l].reshape(1, D), ln_mix_b[l].reshape(1, D))
        xf = _ffn(x1b, x1, w_up[l].astype(BF16), w_down[l].astype(BF16),
                  ln_ffn_g[l].reshape(1, D), ln_ffn_b[l].reshape(1, D))
    return xf.reshape(B, S, D)
```

```python
import functools
import math

import numpy as np
import jax
import jax.numpy as jnp
from jax import lax
from jax.experimental import pallas as pl
from jax.experimental.pallas import tpu as pltpu

F32 = jnp.float32
BF16 = jnp.bfloat16

ATTN_HEADS = 8
ATTN_HEAD_DIM = 64
ATTN_W = ATTN_HEADS * ATTN_HEAD_DIM
IDX_HEADS = 4
IDX_HEAD_DIM = 64
IDX_Q_W = IDX_HEADS * IDX_HEAD_DIM
TOPK_MAX = 256
RET_HEADS = 8
RET_QK_DIM = 64
RET_V_DIM = 128
RET_QK_W = RET_HEADS * RET_QK_DIM
RET_V_W = RET_HEADS * RET_V_DIM
RET_CHUNK = 128
ROPE_BASE = 10000.0
NUM_BUCKETS = 32
MAX_DISTANCE = 128
LN_EPS = 1e-5
DEPTH = 1
DEEPNORM_ALPHA = (2.0 * DEPTH) ** 0.25

LANES = 128
SUBLANES = 8
VMEM_LIMIT = 56 * 1024 * 1024

TQ = 128
CK = 512
NEG = -1e30
LOG2E = math.log2(math.e)
BISECT_ROUNDS = 22
BIAS_TABLE_N = 128
FAR_N = 113


def _cparams(n_grid):
    return pltpu.CompilerParams(
        dimension_semantics=("arbitrary",) * n_grid,
        vmem_limit_bytes=VMEM_LIMIT)


def _trig_kernel(pos_ref, inv_ref, cos_ref, sin_ref):
    ang = pos_ref[...] * inv_ref[...]
    cos_ref[...] = jnp.cos(ang)
    sin_ref[...] = jnp.sin(ang)


def _rope_tables(positions):
    B, S = positions.shape
    half = RET_QK_DIM // 2
    inv = ROPE_BASE ** (-jnp.arange(half, dtype=F32) / half)
    per_row = LANES // half
    rows = B * S // per_row
    pos_e = jnp.repeat(positions.astype(F32).reshape(rows, per_row), half, axis=1)
    inv_e = jnp.tile(inv, per_row).reshape(1, LANES)
    tr = min(rows, 1024)
    cos, sin = pl.pallas_call(
        _trig_kernel,
        grid=(rows // tr,),
        in_specs=[pl.BlockSpec((tr, LANES), lambda i: (i, 0)),
                  pl.BlockSpec((1, LANES), lambda i: (0, 0))],
        out_specs=[pl.BlockSpec((tr, LANES), lambda i: (i, 0))] * 2,
        out_shape=[jax.ShapeDtypeStruct((rows, LANES), F32)] * 2,
        compiler_params=_cparams(1),
        name="rope_tables",
    )(pos_e, inv_e)
    cos = jnp.tile(cos.reshape(B * S, half), (1, per_row))
    sin = jnp.tile(sin.reshape(B * S, half), (1, per_row))
    return cos, sin


def _x_wt(x, wT):
    return lax.dot_general(x, wT, (((1,), (1,)), ((), ())), preferred_element_type=F32)


def _proj_kernel(x_ref, w_ref, o_ref):
    o_ref[...] = _x_wt(x_ref[...], w_ref[...]).astype(o_ref.dtype)


def _proj(xb, wT, out_dtype, tm=1024, tn=512):
    T, D = xb.shape
    N = wT.shape[0]
    tn = min(N, tn)
    tm = min(tm, T)
    return pl.pallas_call(
        _proj_kernel,
        grid=(T // tm, N // tn),
        in_specs=[pl.BlockSpec((tm, D), lambda i, j: (i, 0)),
                  pl.BlockSpec((tn, D), lambda i, j: (j, 0))],
        out_specs=pl.BlockSpec((tm, tn), lambda i, j: (i, j)),
        out_shape=jax.ShapeDtypeStruct((T, N), out_dtype),
        compiler_params=_cparams(2),
        name="proj",
    )(xb, wT)


def _proj_gates_kernel(x_ref, w_ref, o_ref):
    acc = _x_wt(x_ref[...], w_ref[...])
    sig = 0.5 * jnp.tanh(0.5 * acc) + 0.5
    o_ref[...] = jnp.where(pl.program_id(1) == 0, acc * sig, sig).astype(o_ref.dtype)


def _proj_gates(xb, wT, tn, tm=1024):
    T, D = xb.shape
    N = wT.shape[0]
    tm = min(tm, T)
    return pl.pallas_call(
        _proj_gates_kernel,
        grid=(T // tm, N // tn),
        in_specs=[pl.BlockSpec((tm, D), lambda i, j: (i, 0)),
                  pl.BlockSpec((tn, D), lambda i, j: (j, 0))],
        out_specs=pl.BlockSpec((tm, tn), lambda i, j: (i, j)),
        out_shape=jax.ShapeDtypeStruct((T, N), BF16),
        compiler_params=_cparams(2),
        name="proj_gates",
    )(xb, wT)


def _proj_t_kernel(wT_ref, x_ref, o_ref):
    acc = lax.dot_general(wT_ref[...], x_ref[...], (((1,), (1,)), ((), ())), preferred_element_type=F32)
    o_ref[...] = acc.astype(o_ref.dtype)


def _proj_t(xb, wT, B, S, out_dtype, tm=1024, tn=640):
    T, D = xb.shape
    N = wT.shape[0]
    tn = min(N, tn)
    tm = min(tm, S)
    nsb = S // tm
    return pl.pallas_call(
        _proj_t_kernel,
        grid=(T // tm, N // tn),
        in_specs=[pl.BlockSpec((tn, D), lambda i, j: (j, 0)),
                  pl.BlockSpec((tm, D), lambda i, j: (i, 0))],
        out_specs=pl.BlockSpec((pl.Squeezed(), tn, tm), lambda i, j: (i // nsb, j, i % nsb)),
        out_shape=jax.ShapeDtypeStruct((B, N, S), out_dtype),
        compiler_params=_cparams(2),
        name="proj_t",
    )(wT, xb)


def _proj_rope_kernel(x_ref, w_ref, wr_ref, cos_ref, sin_ref, o_ref):
    x = x_ref[...]
    a = _x_wt(x, w_ref[...])
    r = _x_wt(x, wr_ref[...])
    reps = a.shape[1] // LANES
    cos = jnp.concatenate([cos_ref[...]] * reps, axis=1)
    sin = jnp.concatenate([sin_ref[...]] * reps, axis=1)
    o_ref[...] = (a * cos + r * sin).astype(o_ref.dtype)


def _proj_rope(xb, wT, wT_rot, cos, sin, tm=1024, tn=512):
    T, D = xb.shape
    N = wT.shape[0]
    tm = min(tm, T)
    return pl.pallas_call(
        _proj_rope_kernel,
        grid=(T // tm, N // tn),
        in_specs=[pl.BlockSpec((tm, D), lambda i, j: (i, 0)),
                  pl.BlockSpec((tn, D), lambda i, j: (j, 0)),
                  pl.BlockSpec((tn, D), lambda i, j: (j, 0)),
                  pl.BlockSpec((tm, LANES), lambda i, j: (i, 0)),
                  pl.BlockSpec((tm, LANES), lambda i, j: (i, 0))],
        out_specs=pl.BlockSpec((tm, tn), lambda i, j: (i, j)),
        out_shape=jax.ShapeDtypeStruct((T, N), BF16),
        compiler_params=_cparams(2),
        name="proj_rope",
    )(xb, wT, wT_rot, cos, sin)


def _proj_idx_kernel(x_ref, w_ref, g_ref, b_ref, o_ref, *, iw_scale):
    acc = _x_wt(x_ref[...], w_ref[...])
    lane = lax.broadcasted_iota(jnp.int32, acc.shape, 1)
    is_k = lane < IDX_HEAD_DIM
    mu = jnp.sum(jnp.where(is_k, acc, 0.0), axis=-1, keepdims=True) / IDX_HEAD_DIM
    d = acc - mu
    var = jnp.sum(jnp.where(is_k, d * d, 0.0), axis=-1, keepdims=True) / IDX_HEAD_DIM
    ln = d * lax.rsqrt(var + LN_EPS) * g_ref[...] + b_ref[...]
    o_ref[...] = jnp.where(is_k, ln, acc * iw_scale)


def _proj_idx(xb, w_pad, g_pad, b_pad, iw_scale, tm=1024):
    T, D = xb.shape
    tm = min(tm, T)
    return pl.pallas_call(
        functools.partial(_proj_idx_kernel, iw_scale=iw_scale),
        grid=(T // tm,),
        in_specs=[pl.BlockSpec((tm, D), lambda i: (i, 0)),
                  pl.BlockSpec((LANES, D), lambda i: (0, 0)),
                  pl.BlockSpec((1, LANES), lambda i: (0, 0)),
                  pl.BlockSpec((1, LANES), lambda i: (0, 0))],
        out_specs=pl.BlockSpec((tm, LANES), lambda i: (i, 0)),
        out_shape=jax.ShapeDtypeStruct((T, LANES), F32),
        compiler_params=_cparams(1),
        name="proj_idx",
    )(xb, w_pad, g_pad, b_pad)


def _t5_bucket_table():
    n = np.arange(BIAS_TABLE_N)
    max_exact = NUM_BUCKETS // 2
    nf = np.maximum(n, 1).astype(np.float64)
    large = max_exact + (np.log(nf / max_exact) / math.log(MAX_DISTANCE / max_exact)
                         * (NUM_BUCKETS - max_exact)).astype(np.int64)
    large = np.minimum(large, NUM_BUCKETS - 1)
    bucket = np.where(n < max_exact, n, large)
    assert np.all(bucket[FAR_N:] == NUM_BUCKETS - 1) and bucket[FAR_N - 1] != NUM_BUCKETS - 1
    return bucket.astype(np.int32)


def _fold_rows(a, op):
    parts = [a[r:r + SUBLANES] for r in range(0, a.shape[0], SUBLANES)]
    while len(parts) > 1:
        nxt = [op(parts[k], parts[k + 1]) for k in range(0, len(parts) - 1, 2)]
        if len(parts) % 2:
            nxt.append(parts[-1])
        parts = nxt
    return parts[0]


def _attn_kernel(biasc_ref, pinfo_ref, ckmax_ref, iqT_ref, iwT_ref, posqc_ref, posk_ref, ik_ref, qT_ref, k_ref, vT_ref,
                 tbl_ref, toep_ref, tril_ref, o_ref, sc_ref, s_ref, s2_ref, b_ref, qm_ref, m_ref, l_ref, *acc_refs, k_sel, seq):
    i = pl.program_id(1)
    nch = (i * TQ + TQ + CK - 1) // CK
    q_idx = i * TQ + lax.broadcasted_iota(jnp.int32, (1, TQ), 1)
    kf = float(k_sel)

    def chunk_off(c):
        return pl.multiple_of(c * CK, CK)

    def key_idx(off):
        return off + lax.broadcasted_iota(jnp.int32, (CK, TQ), 0)

    def col_reduce(part, op):
        return op(part, axis=0, keepdims=True)

    iqT = iqT_ref[...]
    iwT = iwT_ref[...]
    iq_wide = jnp.concatenate([iqT[h * IDX_HEAD_DIM:(h + 1) * IDX_HEAD_DIM, :] for h in range(IDX_HEADS)], axis=1)

    def score_body(c, carry, masked):
        mn, mx = carry
        off = chunk_off(c)
        ikc = ik_ref[pl.ds(off, CK), :]
        z = jnp.dot(ikc, iq_wide, preferred_element_type=F32)
        s = None
        for h in range(IDX_HEADS):
            t = jnp.maximum(z[:, h * TQ:(h + 1) * TQ], 0.0) * iwT[h:h + 1, :]
            s = t if s is None else s + t
        if masked:
            causal = key_idx(off) <= q_idx
            s_lo = jnp.where(causal, s, -jnp.inf)
            s_hi = jnp.where(causal, s, jnp.inf)
        else:
            s_lo = s_hi = s
        sc_ref[pl.ds(off, CK), :] = s_lo
        mn = jnp.minimum(mn, _fold_rows(s_hi, jnp.minimum))
        mx = jnp.maximum(mx, _fold_rows(s_lo, jnp.maximum))
        return mn, mx

    mn8, mx8 = lax.fori_loop(0, nch - 1, functools.partial(score_body, masked=False),
                             (jnp.full((SUBLANES, TQ), jnp.inf, F32), jnp.full((SUBLANES, TQ), -jnp.inf, F32)))
    mn8, mx8 = score_body(nch - 1, (mn8, mx8), masked=True)
    mn = col_reduce(mn8, jnp.min)
    mx = col_reduce(mx8, jnp.max)

    def over_chunks(fn, init):
        return lax.fori_loop(0, nch, lambda c, carry: fn(sc_ref[pl.ds(chunk_off(c), CK), :], carry), init)

    def count(pred_fn):
        acc = over_chunks(lambda blk, acc: acc + _fold_rows(jnp.where(pred_fn(blk), 1.0, 0.0), jnp.add),
                          jnp.zeros((SUBLANES, TQ), F32))
        return col_reduce(acc, jnp.sum)

    def bisect_round(_, st):
        lo, hi, c_lo = st
        mid = 0.5 * (lo + hi)
        c = count(lambda blk: blk >= mid)
        ok = c >= kf
        return jnp.where(ok, mid, lo), jnp.where(ok, hi, mid), jnp.where(ok, c, c_lo)

    c_all = (q_idx + 1).astype(F32)
    lo, hi, c_lo = lax.fori_loop(0, BISECT_ROUNDS, bisect_round, (mn, mx, c_all))

    cur0 = col_reduce(over_chunks(
        lambda blk, acc: jnp.minimum(acc, _fold_rows(jnp.where(blk >= lo, blk, jnp.inf), jnp.minimum)),
        jnp.full((SUBLANES, TQ), jnp.inf, F32)), jnp.min)

    def walk_cond(st):
        return st[3] > 0.0

    def walk_body(st):
        cur, c_ge, _, _ = st

        def body(blk, carry):
            cnt, nxt = carry
            gt = blk > cur
            cnt = cnt + _fold_rows(jnp.where(gt, 1.0, 0.0), jnp.add)
            nxt = jnp.minimum(nxt, _fold_rows(jnp.where(gt, blk, jnp.inf), jnp.minimum))
            return cnt, nxt

        cnt, nxt = over_chunks(body, (jnp.zeros((SUBLANES, TQ), F32), jnp.full((SUBLANES, TQ), jnp.inf, F32)))
        c_gt = col_reduce(cnt, jnp.sum)
        nxt = col_reduce(nxt, jnp.min)
        adv = c_gt >= kf
        cur = jnp.where(adv, nxt, cur)
        c_ge = jnp.where(adv, c_gt, c_ge)
        return cur, c_ge, c_gt, jnp.max(jnp.where(adv, 1.0, 0.0))

    tau, c_ge, c_gt, _ = lax.while_loop(
        walk_cond, walk_body, (cur0, c_lo, jnp.zeros((1, TQ), F32), jnp.float32(1.0)))

    room = kf - c_gt

    def selection_mask(c, seen):
        off = chunk_off(c)
        tiles = range(CK // LANES)
        blks = [sc_ref[pl.ds(off + j * LANES, LANES), :] for j in tiles]
        ties = [blk == tau for blk in blks]
        local = [jnp.dot(tril_ref[...], jnp.where(tie, 1.0, 0.0).astype(BF16), preferred_element_type=F32)
                 for tie in ties]
        masks = []
        for j in tiles:
            keep = (blks[j] > tau) | (ties[j] & (local[j] + seen <= room))
            masks.append(jnp.where(keep, 0.0, NEG))
            seen = seen + local[j][LANES - 1:LANES, :]
        return jnp.concatenate(masks, axis=0), seen

    m_ref[...] = jnp.full(m_ref.shape, NEG, F32)
    l_ref[...] = jnp.zeros(l_ref.shape, F32)
    for acc in acc_refs:
        acc[...] = jnp.zeros(acc.shape, F32)

    rowi = lax.broadcasted_iota(jnp.int32, (LANES, TQ), 0)
    for pair in range(ATTN_HEADS // 2):
        qp = qT_ref[pair * LANES:(pair + 1) * LANES, :]
        zero = jnp.zeros_like(qp)
        qm_ref[pair] = jnp.concatenate([jnp.where(rowi < ATTN_HEAD_DIM, qp, zero),
                                        jnp.where(rowi >= ATTN_HEAD_DIM, qp, zero)], axis=1)
    ones_rows = jnp.ones((2 * SUBLANES, CK), BF16)

    pq_col = posqc_ref[...]
    batch = pl.program_id(0)
    pq_first, pq_consec, pq_min = (pinfo_ref[batch, r, i] for r in (0, 1, 3))
    far_bias = [biasc_ref[0, h] for h in range(ATTN_HEADS)]
    zero_bias = [biasc_ref[1, h] for h in range(ATTN_HEADS)]
    q_ge_k = (lax.broadcasted_iota(jnp.int32, (LANES, TQ), 1) >= lax.broadcasted_iota(jnp.int32, (LANES, TQ), 0))
    n_sub = CK // LANES
    n_pairs = ATTN_HEADS // 2

    def chunk_is_far(c):
        return (pq_min - ckmax_ref[batch, jnp.minimum(c, seq // CK - 1)]) >= FAR_N

    def stage_bias(c):
        off = chunk_off(c)
        pk_row = posk_ref[:, pl.ds(off, CK)]
        for j in range(n_sub):
            rows = slice(j * LANES, (j + 1) * LANES)
            g = c * n_sub + j
            pk_first, pk_consec, pk_max = (pinfo_ref[batch, r, g] for r in (0, 1, 2))
            all_far = (pq_min - pk_max) >= FAR_N
            all_masked = (off + j * LANES) > (i * TQ + TQ - 1)
            consecutive = (pq_consec > 0) & (pk_consec > 0)
            gap = pq_first - pk_first

            def fill_const(rows=rows):
                for h in range(ATTN_HEADS):
                    b_ref[h, rows, :] = jnp.full((LANES, TQ), far_bias[h], F32)

            def fill_gap0(rows=rows):
                for h in range(ATTN_HEADS):
                    b_ref[h, rows, :] = jnp.where(q_ge_k, toep_ref[h], zero_bias[h])

            def fill_gap128(rows=rows):
                for h in range(ATTN_HEADS):
                    b_ref[h, rows, :] = jnp.where(q_ge_k, far_bias[h], toep_ref[h])

            def fill_lookup(rows=rows):
                pk_sub = pk_row[:, rows]
                n_qk = jnp.clip(pq_col - pk_sub, 0, BIAS_TABLE_N - 1).astype(F32)
                n_kq = n_qk.T.astype(jnp.int32)
                for h in range(ATTN_HEADS):
                    tb = jnp.broadcast_to(tbl_ref[h:h + 1, :], (LANES, BIAS_TABLE_N))
                    b_ref[h, rows, :] = jnp.take_along_axis(tb, n_kq, axis=1)

            def fill_near(fill_gap0=fill_gap0, fill_gap128=fill_gap128, fill_lookup=fill_lookup,
                          consecutive=consecutive, gap=gap):
                lax.cond(consecutive & (gap == 0), fill_gap0,
                         lambda: lax.cond(consecutive & (gap == LANES), fill_gap128, fill_lookup))

            lax.cond(all_far | all_masked, fill_const, fill_near)

    def logits_phase(c, s_buf, const_bias, seen):
        off = chunk_off(c)
        mb, seen = selection_mask(c, seen)
        m_cur = []
        for pair in range(n_pairs):
            kc = k_ref[pl.ds(off, CK), pair * LANES:(pair + 1) * LANES]
            s2 = jnp.dot(kc, qm_ref[pair], preferred_element_type=F32)
            for sub, h in enumerate((2 * pair, 2 * pair + 1)):
                s = s2[:, sub * TQ:(sub + 1) * TQ] + mb
                if not const_bias:
                    s = s + b_ref[h]
                s_buf[h] = s
                top = col_reduce(_fold_rows(s, jnp.maximum), jnp.max)
                m_cur.append(top + far_bias[h] if const_bias else top)
        return m_cur, seen

    def update_phase(c, s_buf, m_cur, const_bias):
        off = chunk_off(c)
        m_prev = m_ref[...]
        l_prev = l_ref[...]
        if isinstance(m_cur, list):
            m_new = [jnp.maximum(m_prev[h:h + 1, :], m_cur[h]) for h in range(ATTN_HEADS)]
            alpha = [jnp.exp2(m_prev[h:h + 1, :] - m_new[h]) for h in range(ATTN_HEADS)]
        else:
            m_all = jnp.maximum(m_prev, m_cur)
            a_all = jnp.exp2(m_prev - m_all)
            m_new = [m_all[h:h + 1, :] for h in range(ATTN_HEADS)]
            alpha = [a_all[h:h + 1, :] for h in range(ATTN_HEADS)]
        l_new = []
        for pair in range(n_pairs):
            heads = (2 * pair, 2 * pair + 1)
            sub_m = [m_new[h] - far_bias[h] if const_bias else m_new[h] for h in heads]
            p2 = jnp.concatenate([jnp.exp2(s_buf[h] - sm).astype(BF16) for h, sm in zip(heads, sub_m)],
                                 axis=1)
            lhs = jnp.concatenate([vT_ref[pair * LANES:(pair + 1) * LANES, pl.ds(off, CK)], ones_rows],
                                  axis=0)
            out = jnp.dot(lhs, p2, preferred_element_type=F32)
            for sub, h in enumerate(heads):
                cols = slice(sub * TQ, (sub + 1) * TQ)
                acc = acc_refs[h]
                acc[...] = alpha[h] * acc[...] + out[sub * ATTN_HEAD_DIM:(sub + 1) * ATTN_HEAD_DIM, cols]
                l_new.append(alpha[h] * l_prev[h:h + 1, :] + out[LANES:LANES + 1, cols])
        l_ref[...] = jnp.concatenate(l_new, axis=0)
        m_ref[...] = jnp.concatenate(m_new, axis=0)

    n_far = lax.while_loop(lambda c: (c < nch) & chunk_is_far(c), lambda c: c + 1, jnp.int32(0))
    n_steps = n_far // 2
    def far_logits(c, s_buf, seen):
        m_cur, seen = logits_phase(c, s_buf, True, seen)
        return jnp.concatenate(m_cur, axis=0), seen

    no_ties = jnp.zeros((1, TQ), F32)
    m_first, seen_first = lax.cond(n_steps > 0, lambda: far_logits(0, s_ref, no_ties),
                                   lambda: (jnp.zeros((ATTN_HEADS, TQ), F32), no_ties))

    def far_step(t, carry):
        m_even, seen_even, _ = carry
        c = 2 * t
        m_odd, seen_odd = far_logits(c + 1, s2_ref, seen_even)
        update_phase(c, s_ref, m_even, True)
        m_even, seen_even = far_logits(jnp.minimum(c + 2, 2 * n_steps - 2), s_ref, seen_odd)
        update_phase(c + 1, s2_ref, m_odd, True)
        return m_even, seen_even, seen_odd

    _, _, seen = lax.fori_loop(0, n_steps, far_step, (m_first, seen_first, no_ties))

    def tail_chunk(c, seen):
        def run(const_bias):
            if not const_bias:
                stage_bias(c)
            m_cur, seen_out = logits_phase(c, s_ref, const_bias, seen)
            update_phase(c, s_ref, m_cur, const_bias)
            return seen_out

        return lax.cond(chunk_is_far(c), lambda: run(True), lambda: run(False))

    lax.fori_loop(2 * n_steps, nch, tail_chunk, seen)

    outT = jnp.concatenate([acc_refs[h][...] / l_ref[h:h + 1, :] for h in range(ATTN_HEADS)], axis=0)
    o_ref[...] = outT.T.astype(o_ref.dtype)


def _toeplitz_kernel(tbl_ref, o_ref):
    q = lax.broadcasted_iota(jnp.int32, (LANES, LANES), 1)
    k = lax.broadcasted_iota(jnp.int32, (LANES, LANES), 0)
    idx = (q - k) & (BIAS_TABLE_N - 1)
    for h in range(ATTN_HEADS):
        tb = jnp.broadcast_to(tbl_ref[h:h + 1, :], (LANES, BIAS_TABLE_N))
        o_ref[h] = jnp.take_along_axis(tb, idx, axis=1)


def _sparse_attention(fm, tokb, ik, iwT, positions, rel_bias):
    B, S, _ = tokb.shape
    k_sel = min(TOPK_MAX, S // 4)
    bucket = _t5_bucket_table()
    tbl = rel_bias[bucket].T.astype(F32) * LOG2E
    bias_c = jnp.stack([tbl[:, BIAS_TABLE_N - 1], tbl[:, 0]])
    toep = pl.pallas_call(
        _toeplitz_kernel,
        out_shape=jax.ShapeDtypeStruct((ATTN_HEADS, LANES, LANES), F32),
        name="bias_toeplitz",
    )(tbl)
    pos_row = positions.reshape(B, 1, S)
    pos_col = positions.reshape(B, S, 1)
    pos_t = positions.reshape(B, S // LANES, LANES)
    consec = jnp.all(pos_t - pos_t[:, :, :1] == jnp.arange(LANES, dtype=positions.dtype), axis=-1)
    pinfo = jnp.stack([pos_t[:, :, 0], consec.astype(jnp.int32), jnp.max(pos_t, axis=-1),
                       jnp.min(pos_t, axis=-1)], axis=1).astype(jnp.int32)
    ck_max = jnp.max(positions.reshape(B, S // CK, CK), axis=-1)
    tril = jnp.tril(jnp.ones((LANES, LANES), BF16))
    sq = pl.Squeezed()
    iq_blk = (2 * ATTN_W) // IDX_Q_W
    k_blk = (tokb.shape[2] - ATTN_W) // ATTN_W
    return pl.pallas_call(
        functools.partial(_attn_kernel, k_sel=k_sel, seq=S),
        grid=(B, S // TQ),
        in_specs=[pl.BlockSpec(memory_space=pltpu.SMEM),
                  pl.BlockSpec(memory_space=pltpu.SMEM),
                  pl.BlockSpec(memory_space=pltpu.SMEM),
                  pl.BlockSpec((sq, IDX_Q_W, TQ), lambda b, i: (b, iq_blk, i)),
                  pl.BlockSpec((sq, SUBLANES, TQ), lambda b, i: (b, 0, i)),
                  pl.BlockSpec((sq, TQ, 1), lambda b, i: (b, i, 0)),
                  pl.BlockSpec((sq, 1, S), lambda b, i: (b, 0, 0)),
                  pl.BlockSpec((sq, S, IDX_HEAD_DIM), lambda b, i: (b, 0, 0)),
                  pl.BlockSpec((sq, ATTN_W, TQ), lambda b, i: (b, 0, i)),
                  pl.BlockSpec((sq, S, ATTN_W), lambda b, i: (b, 0, k_blk)),
                  pl.BlockSpec((sq, ATTN_W, S), lambda b, i: (b, 1, 0)),
                  pl.BlockSpec((ATTN_HEADS, BIAS_TABLE_N), lambda b, i: (0, 0)),
                  pl.BlockSpec((ATTN_HEADS, LANES, LANES), lambda b, i: (0, 0, 0)),
                  pl.BlockSpec((LANES, LANES), lambda b, i: (0, 0))],
        out_specs=pl.BlockSpec((sq, TQ, ATTN_W), lambda b, i: (b, i, 0)),
        out_shape=jax.ShapeDtypeStruct((B, S, ATTN_W), BF16),
        scratch_shapes=[pltpu.VMEM((S, TQ), F32),
                        pltpu.VMEM((ATTN_HEADS, CK, TQ), F32),
                        pltpu.VMEM((ATTN_HEADS, CK, TQ), F32),
                        pltpu.VMEM((ATTN_HEADS, CK, TQ), F32),
                        pltpu.VMEM((ATTN_HEADS // 2, LANES, 2 * TQ), BF16),
                        pltpu.VMEM((ATTN_HEADS, TQ), F32),
                        pltpu.VMEM((ATTN_HEADS, TQ), F32)]
                       + [pltpu.VMEM((ATTN_HEAD_DIM, TQ), F32)] * ATTN_HEADS,
        compiler_params=_cparams(2),
        name="sparse_attention",
    )(bias_c, pinfo, ck_max, fm, iwT, pos_col, pos_row, ik, fm, tokb, fm, tbl, toep, tril)


def _retention_kernel(q_ref, k_ref, v_ref, g_ref, decay_ref, xi_ref, zeta_ref, gch_ref, o_ref, r_ref):
    @pl.when(pl.program_id(0) == 0)
    def _():
        r_ref[...] = jnp.zeros(r_ref.shape, F32)

    C = RET_CHUNK
    lane = lax.broadcasted_iota(jnp.int32, (C, LANES), 1)
    row = lax.broadcasted_iota(jnp.int32, (LANES, RET_V_DIM), 0)
    for pair in range(RET_HEADS // 2):
        for b in range(q_ref.shape[0]):
            q_pair = q_ref[b, :, pair * LANES:(pair + 1) * LANES]
            k_pair = k_ref[b, :, pair * LANES:(pair + 1) * LANES]
            v_pair = v_ref[b, :, 2 * pair * RET_V_DIM:(2 * pair + 2) * RET_V_DIM]
            r_pair = r_ref[b, pair]
            r_bf = r_pair.astype(BF16)
            for sub in range(2):
                h = 2 * pair + sub
                in_head = (lane >= sub * RET_QK_DIM) & (lane < (sub + 1) * RET_QK_DIM)
                qm = jnp.where(in_head, q_pair, jnp.zeros_like(q_pair))
                v_h = v_pair[:, sub * RET_V_DIM:(sub + 1) * RET_V_DIM]
                inner = lax.dot_general(qm, k_pair, (((1,), (1,)), ((), ())),
                                        preferred_element_type=F32) * decay_ref[h]
                o = (jnp.dot(inner.astype(BF16), v_h, preferred_element_type=F32)
                     + jnp.dot(qm, r_bf, preferred_element_type=F32) * xi_ref[h])
                mu = jnp.mean(o, axis=-1, keepdims=True)
                d = o - mu
                var = jnp.mean(d * d, axis=-1, keepdims=True)
                hn = d * lax.rsqrt(var + LN_EPS)
                gate = g_ref[b, :, h * RET_V_DIM:(h + 1) * RET_V_DIM].astype(F32)
                o_ref[b, :, h * RET_V_DIM:(h + 1) * RET_V_DIM] = (gate * hn).astype(o_ref.dtype)
            kz = (k_pair.astype(F32) * zeta_ref[pair]).astype(BF16)
            upd = lax.dot_general(kz, v_pair, (((0,), (0,)), ((), ())), preferred_element_type=F32)
            r_ref[b, pair] = (r_pair * gch_ref[pair]
                              + jnp.where(row < RET_QK_DIM, upd[:, :RET_V_DIM], upd[:, RET_V_DIM:]))


def _retention(qk, tokb, gates, B, S):
    C = RET_CHUNK
    H = RET_HEADS
    nc = S // C
    gamma = 1.0 - 2.0 ** (-5.0 - jnp.arange(H, dtype=F32))
    log_g = jnp.log(gamma)
    n = jnp.arange(C, dtype=F32)
    diff = n[:, None] - n[None, :]
    decay_in = jnp.where(diff[None] >= 0, jnp.exp(log_g[:, None, None] * jnp.maximum(diff, 0.0)[None]), 0.0)
    xi = jnp.exp(log_g[None, :] * (n[:, None] + 1.0))
    zeta = jnp.exp(log_g[None, :] * (C - 1.0 - n[:, None]))
    g_chunk = jnp.exp(log_g * C)
    xi_b = jnp.broadcast_to(xi.T[:, :, None], (H, C, RET_V_DIM))
    zeta_b = jnp.repeat(zeta, RET_QK_DIM, axis=1).reshape(C, H // 2, LANES).transpose(1, 0, 2)
    gch_b = jnp.broadcast_to(jnp.repeat(g_chunk, RET_QK_DIM).reshape(H // 2, LANES, 1),
                             (H // 2, LANES, RET_V_DIM))
    qk3, tok3, gate3 = (a.reshape(B, S, a.shape[-1]) for a in (qk, tokb, gates))
    out = pl.pallas_call(
        _retention_kernel,
        grid=(nc,),
        in_specs=[pl.BlockSpec((B, C, RET_QK_W), lambda i: (0, i, 0)),
                  pl.BlockSpec((B, C, RET_QK_W), lambda i: (0, i, 1)),
                  pl.BlockSpec((B, C, RET_V_W), lambda i: (0, i, 0)),
                  pl.BlockSpec((B, C, RET_V_W), lambda i: (0, i, 0)),
                  pl.BlockSpec((H, C, C), lambda i: (0, 0, 0)),
                  pl.BlockSpec((H, C, RET_V_DIM), lambda i: (0, 0, 0)),
                  pl.BlockSpec((H // 2, C, LANES), lambda i: (0, 0, 0)),
                  pl.BlockSpec((H // 2, LANES, RET_V_DIM), lambda i: (0, 0, 0))],
        out_specs=pl.BlockSpec((B, C, RET_V_W), lambda i: (0, i, 0)),
        out_shape=jax.ShapeDtypeStruct((B, S, RET_V_W), BF16),
        scratch_shapes=[pltpu.VMEM((B, H // 2, LANES, RET_V_DIM), F32)],
        compiler_params=_cparams(1),
        name="retention",
    )(qk3, qk3, tok3, gate3, decay_in, xi_b, zeta_b, gch_b)
    return out.reshape(B * S, RET_V_W)


def _layer_norm(z, g, b):
    mu = jnp.mean(z, axis=-1, keepdims=True)
    d = z - mu
    var = jnp.mean(d * d, axis=-1, keepdims=True)
    return d * lax.rsqrt(var + LN_EPS) * g + b


def _merge_kernel(x_ref, ya_ref, yr_ref, ga_ref, gr_ref, wa_ref, wr_ref, wo_ref, g_ref, b_ref,
                  x1_ref, x1b_ref):
    a = jnp.dot(ya_ref[...], wa_ref[...], preferred_element_type=F32)
    r = jnp.dot(yr_ref[...], wr_ref[...], preferred_element_type=F32)
    h = ga_ref[...].astype(F32) * a + gr_ref[...].astype(F32) * r
    mix = jnp.dot(h.astype(BF16), wo_ref[...], preferred_element_type=F32)
    x1 = _layer_norm(DEEPNORM_ALPHA * x_ref[...] + mix, g_ref[...], b_ref[...])
    x1_ref[...] = x1
    x1b_ref[...] = x1.astype(BF16)


def _merge(x, ya, yr, gates, wa, wr, wo, g, b, tm=512):
    T, D = x.shape
    tm = min(tm, T)
    row = lambda i: (i, 0)
    fixed = lambda i: (0, 0)
    return pl.pallas_call(
        _merge_kernel,
        grid=(T // tm,),
        in_specs=[pl.BlockSpec((tm, D), row),
                  pl.BlockSpec((tm, ya.shape[1]), row),
                  pl.BlockSpec((tm, yr.shape[1]), row),
                  pl.BlockSpec((tm, D), lambda i: (i, 1)),
                  pl.BlockSpec((tm, D), lambda i: (i, 2)),
                  pl.BlockSpec(wa.shape, fixed),
                  pl.BlockSpec(wr.shape, fixed),
                  pl.BlockSpec(wo.shape, fixed),
                  pl.BlockSpec((1, D), fixed),
                  pl.BlockSpec((1, D), fixed)],
        out_specs=[pl.BlockSpec((tm, D), row), pl.BlockSpec((tm, D), row)],
        out_shape=[jax.ShapeDtypeStruct((T, D), F32), jax.ShapeDtypeStruct((T, D), BF16)],
        compiler_params=_cparams(1),
        name="merge",
    )(x, ya, yr, gates, gates, wa, wr, wo, g, b)


def _ffn_kernel(x1b_ref, x1_ref, wu_ref, wd_ref, g_ref, b_ref, o_ref, acc_ref):
    f = pl.program_id(1)

    @pl.when(f == 0)
    def _():
        acc_ref[...] = jnp.zeros(acc_ref.shape, F32)

    hid = jnp.maximum(jnp.dot(x1b_ref[...], wu_ref[...], preferred_element_type=F32), 0.0)
    acc_ref[...] += jnp.dot((hid * hid).astype(BF16), wd_ref[...], preferred_element_type=F32)

    @pl.when(f == pl.num_programs(1) - 1)
    def _():
        o_ref[...] = _layer_norm(DEEPNORM_ALPHA * x1_ref[...] + acc_ref[...], g_ref[...], b_ref[...])


def _ffn(x1b, x1, wu, wd, g, b, tm=1024, tf=1024):
    T, D = x1.shape
    F = wu.shape[1]
    tm = min(tm, T)
    return pl.pallas_call(
        _ffn_kernel,
        grid=(T // tm, F // tf),
        in_specs=[pl.BlockSpec((tm, D), lambda i, f: (i, 0)),
                  pl.BlockSpec((tm, D), lambda i, f: (i, 0)),
                  pl.BlockSpec((D, tf), lambda i, f: (0, f)),
                  pl.BlockSpec((tf, D), lambda i, f: (f, 0)),
                  pl.BlockSpec((1, D), lambda i, f: (0, 0)),
                  pl.BlockSpec((1, D), lambda i, f: (0, 0))],
        out_specs=pl.BlockSpec((tm, D), lambda i, f: (i, 0)),
        out_shape=jax.ShapeDtypeStruct((T, D), F32),
        scratch_shapes=[pltpu.VMEM((tm, D), F32)],
        compiler_params=_cparams(2),
        name="ffn",
    )(x1b, x1, wu, wd, g, b)


def _rot_half_weight(wT):
    N, D = wT.shape
    half = RET_QK_DIM // 2
    wh = wT.reshape(N // RET_QK_DIM, 2, half, D)
    return jnp.stack([-wh[:, 1], wh[:, 0]], axis=1).reshape(N, D)


def kernel(x, positions, w_in, rel_bias, idx_k_ln_g, idx_k_ln_b, w_attn_branch, w_ret_branch,
           w_out, ln_mix_g, ln_mix_b, w_up, w_down, ln_ffn_g, ln_ffn_b):
    B, S, D = x.shape
    T = B * S
    sizes = (ATTN_W, ATTN_W, ATTN_W, IDX_Q_W, IDX_HEAD_DIM, IDX_HEADS,
             RET_QK_W, RET_QK_W, RET_V_W, RET_V_W, D, D)
    offs = [0] + [int(o) for o in np.cumsum(sizes)]
    cos, sin = _rope_tables(positions)
    xf = x.reshape(T, D)
    for l in range(DEPTH):
        wT_f32 = jnp.swapaxes(w_in[l], 0, 1)
        wT = wT_f32.astype(BF16)
        rows = [wT[offs[k]:offs[k + 1]] for k in range(len(sizes))]
        (w_qa, w_ka, w_va, w_iq, w_ik, w_iw, w_qr, w_kr, w_vr, w_gr, w_ga, w_gtr) = rows
        xb = xf.astype(BF16)
        w_qa = (wT_f32[offs[0]:offs[1]] * (ATTN_HEAD_DIM ** -0.5 * LOG2E)).astype(BF16)
        w_kr = w_kr * (RET_QK_DIM ** -0.5)

        fm = _proj_t(xb, jnp.concatenate([w_qa, w_va, w_iq], axis=0), B, S, BF16, tn=1280)
        tokb = _proj(xb, jnp.concatenate([w_vr, w_ka], axis=0), BF16, tn=1536)
        gates = _proj_gates(xb, jnp.concatenate([w_gr, w_ga, w_gtr], axis=0), tn=D)
        w_rope = jnp.concatenate([w_qr, w_kr], axis=0)
        w_rope_rot = jnp.concatenate([_rot_half_weight(w_qr), _rot_half_weight(w_kr)], axis=0)
        qk_r = _proj_rope(xb, w_rope, w_rope_rot, cos, sin, tn=1024)
        pad = LANES - IDX_HEAD_DIM - IDX_HEADS
        w_idx = jnp.concatenate([w_ik, w_iw, jnp.zeros((pad, D), BF16)], axis=0)
        g_pad = jnp.concatenate([idx_k_ln_g[l], jnp.zeros((LANES - IDX_HEAD_DIM,), F32)]).reshape(1, LANES)
        b_pad = jnp.concatenate([idx_k_ln_b[l], jnp.zeros((LANES - IDX_HEAD_DIM,), F32)]).reshape(1, LANES)
        idx = _proj_idx(xb, w_idx, g_pad, b_pad,
                        (IDX_HEAD_DIM ** -0.5) * (IDX_HEADS ** -0.5)).reshape(B, S, LANES)
        ik = idx[:, :, :IDX_HEAD_DIM].astype(BF16)
        iwT = jnp.swapaxes(idx[:, :, IDX_HEAD_DIM:IDX_HEAD_DIM + SUBLANES], 1, 2)

        y_a = _sparse_attention(fm, tokb.reshape(B, S, -1), ik, iwT, positions, rel_bias)
        y_r = _retention(qk_r, tokb, gates, B, S)
        x1, x1b = _merge(xf, y_a.reshape(T, ATTN_W), y_r, gates,
                         w_attn_branch[l].astype(BF16), w_ret_branch[l].astype(BF16),
                         w_out[l].astype(BF16), ln_mix_g[l].reshape(1, D), ln_mix_b[l].reshape(1, D))
        xf = _ffn(x1b, x1, w_up[l].astype(BF16), w_down[l].astype(BF16),
                  ln_ffn_g[l].reshape(1, D), ln_ffn_b[l].reshape(1, D))
    return xf.reshape(B, S, D)
```

```python
import functools
import math

import numpy as np
import jax
import jax.numpy as jnp
from jax import lax
from jax.experimental import pallas as pl
from jax.experimental.pallas import tpu as pltpu

F32 = jnp.float32
BF16 = jnp.bfloat16

ATTN_HEADS = 8
ATTN_HEAD_DIM = 64
ATTN_W = ATTN_HEADS * ATTN_HEAD_DIM
IDX_HEADS = 4
IDX_HEAD_DIM = 64
IDX_Q_W = IDX_HEADS * IDX_HEAD_DIM
TOPK_MAX = 256
RET_HEADS = 8
RET_QK_DIM = 64
RET_V_DIM = 128
RET_QK_W = RET_HEADS * RET_QK_DIM
RET_V_W = RET_HEADS * RET_V_DIM
RET_CHUNK = 128
ROPE_BASE = 10000.0
NUM_BUCKETS = 32
MAX_DISTANCE = 128
LN_EPS = 1e-5
DEPTH = 1
DEEPNORM_ALPHA = (2.0 * DEPTH) ** 0.25

LANES = 128
SUBLANES = 8
VMEM_LIMIT = 56 * 1024 * 1024

TQ = 128
CK = 512
NEG = -1e30
LOG2E = math.log2(math.e)
BISECT_ROUNDS = 20
BIAS_TABLE_N = 128
FAR_N = 113


def _cparams(n_grid):
    return pltpu.CompilerParams(
        dimension_semantics=("arbitrary",) * n_grid,
        vmem_limit_bytes=VMEM_LIMIT)


def _trig_kernel(pos_ref, inv_ref, cos_ref, sin_ref):
    ang = pos_ref[...] * inv_ref[...]
    cos_ref[...] = jnp.cos(ang)
    sin_ref[...] = jnp.sin(ang)


def _rope_tables(positions):
    B, S = positions.shape
    half = RET_QK_DIM // 2
    inv = ROPE_BASE ** (-jnp.arange(half, dtype=F32) / half)
    per_row = LANES // half
    rows = B * S // per_row
    pos_e = jnp.repeat(positions.astype(F32).reshape(rows, per_row), half, axis=1)
    inv_e = jnp.tile(inv, per_row).reshape(1, LANES)
    tr = min(rows, 1024)
    cos, sin = pl.pallas_call(
        _trig_kernel,
        grid=(rows // tr,),
        in_specs=[pl.BlockSpec((tr, LANES), lambda i: (i, 0)),
                  pl.BlockSpec((1, LANES), lambda i: (0, 0))],
        out_specs=[pl.BlockSpec((tr, LANES), lambda i: (i, 0))] * 2,
        out_shape=[jax.ShapeDtypeStruct((rows, LANES), F32)] * 2,
        compiler_params=_cparams(1),
        name="rope_tables",
    )(pos_e, inv_e)
    cos = jnp.tile(cos.reshape(B * S, half), (1, per_row))
    sin = jnp.tile(sin.reshape(B * S, half), (1, per_row))
    return cos, sin


def _x_wt(x, wT):
    return lax.dot_general(x, wT, (((1,), (1,)), ((), ())), preferred_element_type=F32)


def _proj_kernel(x_ref, w_ref, o_ref):
    o_ref[...] = _x_wt(x_ref[...], w_ref[...]).astype(o_ref.dtype)


def _proj(xb, wT, out_dtype, tm=1024, tn=512):
    T, D = xb.shape
    N = wT.shape[0]
    tn = min(N, tn)
    tm = min(tm, T)
    return pl.pallas_call(
        _proj_kernel,
        grid=(T // tm, N // tn),
        in_specs=[pl.BlockSpec((tm, D), lambda i, j: (i, 0)),
                  pl.BlockSpec((tn, D), lambda i, j: (j, 0))],
        out_specs=pl.BlockSpec((tm, tn), lambda i, j: (i, j)),
        out_shape=jax.ShapeDtypeStruct((T, N), out_dtype),
        compiler_params=_cparams(2),
        name="proj",
    )(xb, wT)


def _proj_gates_kernel(x_ref, w_ref, o_ref):
    acc = _x_wt(x_ref[...], w_ref[...])
    sig = 0.5 * jnp.tanh(0.5 * acc) + 0.5
    o_ref[...] = jnp.where(pl.program_id(1) == 0, acc * sig, sig).astype(o_ref.dtype)


def _proj_gates(xb, wT, tn, tm=1024):
    T, D = xb.shape
    N = wT.shape[0]
    tm = min(tm, T)
    return pl.pallas_call(
        _proj_gates_kernel,
        grid=(T // tm, N // tn),
        in_specs=[pl.BlockSpec((tm, D), lambda i, j: (i, 0)),
                  pl.BlockSpec((tn, D), lambda i, j: (j, 0))],
        out_specs=pl.BlockSpec((tm, tn), lambda i, j: (i, j)),
        out_shape=jax.ShapeDtypeStruct((T, N), BF16),
        compiler_params=_cparams(2),
        name="proj_gates",
    )(xb, wT)


def _proj_t_kernel(wT_ref, x_ref, o_ref):
    acc = lax.dot_general(wT_ref[...], x_ref[...], (((1,), (1,)), ((), ())), preferred_element_type=F32)
    o_ref[...] = acc.astype(o_ref.dtype)


def _proj_t(xb, wT, B, S, out_dtype, tm=1024, tn=640):
    T, D = xb.shape
    N = wT.shape[0]
    tn = min(N, tn)
    tm = min(tm, S)
    nsb = S // tm
    return pl.pallas_call(
        _proj_t_kernel,
        grid=(T // tm, N // tn),
        in_specs=[pl.BlockSpec((tn, D), lambda i, j: (j, 0)),
                  pl.BlockSpec((tm, D), lambda i, j: (i, 0))],
        out_specs=pl.BlockSpec((pl.Squeezed(), tn, tm), lambda i, j: (i // nsb, j, i % nsb)),
        out_shape=jax.ShapeDtypeStruct((B, N, S), out_dtype),
        compiler_params=_cparams(2),
        name="proj_t",
    )(wT, xb)


def _proj_rope_kernel(x_ref, w_ref, wr_ref, cos_ref, sin_ref, o_ref):
    x = x_ref[...]
    a = _x_wt(x, w_ref[...])
    r = _x_wt(x, wr_ref[...])
    reps = a.shape[1] // LANES
    cos = jnp.concatenate([cos_ref[...]] * reps, axis=1)
    sin = jnp.concatenate([sin_ref[...]] * reps, axis=1)
    o_ref[...] = (a * cos + r * sin).astype(o_ref.dtype)


def _proj_rope(xb, wT, wT_rot, cos, sin, tm=1024, tn=512):
    T, D = xb.shape
    N = wT.shape[0]
    tm = min(tm, T)
    return pl.pallas_call(
        _proj_rope_kernel,
        grid=(T // tm, N // tn),
        in_specs=[pl.BlockSpec((tm, D), lambda i, j: (i, 0)),
                  pl.BlockSpec((tn, D), lambda i, j: (j, 0)),
                  pl.BlockSpec((tn, D), lambda i, j: (j, 0)),
                  pl.BlockSpec((tm, LANES), lambda i, j: (i, 0)),
                  pl.BlockSpec((tm, LANES), lambda i, j: (i, 0))],
        out_specs=pl.BlockSpec((tm, tn), lambda i, j: (i, j)),
        out_shape=jax.ShapeDtypeStruct((T, N), BF16),
        compiler_params=_cparams(2),
        name="proj_rope",
    )(xb, wT, wT_rot, cos, sin)


def _proj_idx_kernel(x_ref, w_ref, g_ref, b_ref, o_ref, *, iw_scale):
    acc = _x_wt(x_ref[...], w_ref[...])
    lane = lax.broadcasted_iota(jnp.int32, acc.shape, 1)
    is_k = lane < IDX_HEAD_DIM
    mu = jnp.sum(jnp.where(is_k, acc, 0.0), axis=-1, keepdims=True) / IDX_HEAD_DIM
    d = acc - mu
    var = jnp.sum(jnp.where(is_k, d * d, 0.0), axis=-1, keepdims=True) / IDX_HEAD_DIM
    ln = d * lax.rsqrt(var + LN_EPS) * g_ref[...] + b_ref[...]
    o_ref[...] = jnp.where(is_k, ln, acc * iw_scale)


def _proj_idx(xb, w_pad, g_pad, b_pad, iw_scale, tm=1024):
    T, D = xb.shape
    tm = min(tm, T)
    return pl.pallas_call(
        functools.partial(_proj_idx_kernel, iw_scale=iw_scale),
        grid=(T // tm,),
        in_specs=[pl.BlockSpec((tm, D), lambda i: (i, 0)),
                  pl.BlockSpec((LANES, D), lambda i: (0, 0)),
                  pl.BlockSpec((1, LANES), lambda i: (0, 0)),
                  pl.BlockSpec((1, LANES), lambda i: (0, 0))],
        out_specs=pl.BlockSpec((tm, LANES), lambda i: (i, 0)),
        out_shape=jax.ShapeDtypeStruct((T, LANES), F32),
        compiler_params=_cparams(1),
        name="proj_idx",
    )(xb, w_pad, g_pad, b_pad)


def _t5_bucket_table():
    n = np.arange(BIAS_TABLE_N)
    max_exact = NUM_BUCKETS // 2
    nf = np.maximum(n, 1).astype(np.float64)
    large = max_exact + (np.log(nf / max_exact) / math.log(MAX_DISTANCE / max_exact)
                         * (NUM_BUCKETS - max_exact)).astype(np.int64)
    large = np.minimum(large, NUM_BUCKETS - 1)
    bucket = np.where(n < max_exact, n, large)
    assert np.all(bucket[FAR_N:] == NUM_BUCKETS - 1) and bucket[FAR_N - 1] != NUM_BUCKETS - 1
    return bucket.astype(np.int32)


def _fold_rows(a, op):
    parts = [a[r:r + SUBLANES] for r in range(0, a.shape[0], SUBLANES)]
    while len(parts) > 1:
        nxt = [op(parts[k], parts[k + 1]) for k in range(0, len(parts) - 1, 2)]
        if len(parts) % 2:
            nxt.append(parts[-1])
        parts = nxt
    return parts[0]


def _attn_kernel(biasc_ref, pinfo_ref, ckmax_ref, iqT_ref, iwT_ref, posqc_ref, posk_ref, ik_ref, qT_ref, k_ref, vT_ref,
                 tbl_ref, toep_ref, tril_ref, o_ref, sc_ref, s_ref, s2_ref, b_ref, qm_ref, m_ref, l_ref, *acc_refs, k_sel, seq):
    i = pl.program_id(1)
    nch = (i * TQ + TQ + CK - 1) // CK
    q_idx = i * TQ + lax.broadcasted_iota(jnp.int32, (1, TQ), 1)
    kf = float(k_sel)

    def chunk_off(c):
        return pl.multiple_of(c * CK, CK)

    def key_idx(off):
        return off + lax.broadcasted_iota(jnp.int32, (CK, TQ), 0)

    def col_reduce(part, op):
        return op(part, axis=0, keepdims=True)

    iqT = iqT_ref[...]
    iwT = iwT_ref[...]
    iq_wide = jnp.concatenate([iqT[h * IDX_HEAD_DIM:(h + 1) * IDX_HEAD_DIM, :] for h in range(IDX_HEADS)], axis=1)

    def score_matmul(c, z_buf):
        ikc = ik_ref[pl.ds(chunk_off(c), CK), :]
        z = jnp.dot(ikc, iq_wide, preferred_element_type=F32)
        for h in range(IDX_HEADS):
            z_buf[h] = z[:, h * TQ:(h + 1) * TQ]

    def score_finish(c, z_buf, carry, masked):
        mn, mx = carry
        off = chunk_off(c)
        s = None
        for h in range(IDX_HEADS):
            t = jnp.maximum(z_buf[h], 0.0) * iwT[h:h + 1, :]
            s = t if s is None else s + t
        if masked:
            causal = key_idx(off) <= q_idx
            s_lo = jnp.where(causal, s, -jnp.inf)
            s_hi = jnp.where(causal, s, jnp.inf)
        else:
            s_lo = s_hi = s
        sc_ref[pl.ds(off, CK), :] = s_lo
        mn = jnp.minimum(mn, _fold_rows(s_hi, jnp.minimum))
        mx = jnp.maximum(mx, _fold_rows(s_lo, jnp.maximum))
        return mn, mx

    def score_chunk(c, carry, masked):
        score_matmul(c, s_ref)
        return score_finish(c, s_ref, carry, masked)

    n_inner = nch - 1
    n_pairs_sc = n_inner // 2

    @pl.when(n_pairs_sc > 0)
    def _():
        score_matmul(0, s_ref)

    def score_step(t, carry):
        c = 2 * t
        score_matmul(c + 1, s2_ref)
        carry = score_finish(c, s_ref, carry, False)
        score_matmul(jnp.minimum(c + 2, 2 * n_pairs_sc - 2), s_ref)
        return score_finish(c + 1, s2_ref, carry, False)

    mn8, mx8 = lax.fori_loop(0, n_pairs_sc, score_step,
                             (jnp.full((SUBLANES, TQ), jnp.inf, F32), jnp.full((SUBLANES, TQ), -jnp.inf, F32)))
    mn8, mx8 = lax.fori_loop(2 * n_pairs_sc, n_inner, functools.partial(score_chunk, masked=False), (mn8, mx8))
    mn8, mx8 = score_chunk(nch - 1, (mn8, mx8), masked=True)
    mn = col_reduce(mn8, jnp.min)
    mx = col_reduce(mx8, jnp.max)

    def over_chunks(fn, init):
        return lax.fori_loop(0, nch, lambda c, carry: fn(sc_ref[pl.ds(chunk_off(c), CK), :], carry), init)

    def count(pred_fn):
        acc = over_chunks(lambda blk, acc: acc + _fold_rows(jnp.where(pred_fn(blk), 1.0, 0.0), jnp.add),
                          jnp.zeros((SUBLANES, TQ), F32))
        return col_reduce(acc, jnp.sum)

    def bisect_round(_, st):
        lo, hi, c_lo = st
        mid = 0.5 * (lo + hi)
        c = count(lambda blk: blk >= mid)
        ok = c >= kf
        return jnp.where(ok, mid, lo), jnp.where(ok, hi, mid), jnp.where(ok, c, c_lo)

    c_all = (q_idx + 1).astype(F32)
    lo, hi, c_lo = lax.fori_loop(0, BISECT_ROUNDS, bisect_round, (mn, mx, c_all))

    cur0 = col_reduce(over_chunks(
        lambda blk, acc: jnp.minimum(acc, _fold_rows(jnp.where(blk >= lo, blk, jnp.inf), jnp.minimum)),
        jnp.full((SUBLANES, TQ), jnp.inf, F32)), jnp.min)

    def walk_cond(st):
        return st[3] > 0.0

    def walk_body(st):
        cur, c_ge, _, _ = st

        def body(blk, carry):
            cnt, nxt = carry
            gt = blk > cur
            cnt = cnt + _fold_rows(jnp.where(gt, 1.0, 0.0), jnp.add)
            nxt = jnp.minimum(nxt, _fold_rows(jnp.where(gt, blk, jnp.inf), jnp.minimum))
            return cnt, nxt

        cnt, nxt = over_chunks(body, (jnp.zeros((SUBLANES, TQ), F32), jnp.full((SUBLANES, TQ), jnp.inf, F32)))
        c_gt = col_reduce(cnt, jnp.sum)
        nxt = col_reduce(nxt, jnp.min)
        adv = c_gt >= kf
        cur = jnp.where(adv, nxt, cur)
        c_ge = jnp.where(adv, c_gt, c_ge)
        return cur, c_ge, c_gt, jnp.max(jnp.where(adv, 1.0, 0.0))

    tau, c_ge, c_gt, _ = lax.while_loop(
        walk_cond, walk_body, (cur0, c_lo, jnp.zeros((1, TQ), F32), jnp.float32(1.0)))

    room = kf - c_gt

    def selection_mask(c, seen):
        off = chunk_off(c)
        tiles = range(CK // LANES)
        blks = [sc_ref[pl.ds(off + j * LANES, LANES), :] for j in tiles]
        ties = [blk == tau for blk in blks]
        local = [jnp.dot(tril_ref[...], jnp.where(tie, 1.0, 0.0).astype(BF16), preferred_element_type=F32)
                 for tie in ties]
        masks = []
        for j in tiles:
            keep = (blks[j] > tau) | (ties[j] & (local[j] + seen <= room))
            masks.append(jnp.where(keep, 0.0, NEG))
            seen = seen + local[j][LANES - 1:LANES, :]
        return jnp.concatenate(masks, axis=0), seen

    m_ref[...] = jnp.full(m_ref.shape, NEG, F32)
    l_ref[...] = jnp.zeros(l_ref.shape, F32)
    for acc in acc_refs:
        acc[...] = jnp.zeros(acc.shape, F32)

    rowi = lax.broadcasted_iota(jnp.int32, (LANES, TQ), 0)
    for pair in range(ATTN_HEADS // 2):
        qp = qT_ref[pair * LANES:(pair + 1) * LANES, :]
        zero = jnp.zeros_like(qp)
        qm_ref[pair] = jnp.concatenate([jnp.where(rowi < ATTN_HEAD_DIM, qp, zero),
                                        jnp.where(rowi >= ATTN_HEAD_DIM, qp, zero)], axis=1)
    ones_rows = jnp.ones((2 * SUBLANES, CK), BF16)

    pq_col = posqc_ref[...]
    batch = pl.program_id(0)
    pq_first, pq_consec, pq_min = (pinfo_ref[batch, r, i] for r in (0, 1, 3))
    far_bias = [biasc_ref[0, h] for h in range(ATTN_HEADS)]
    zero_bias = [biasc_ref[1, h] for h in range(ATTN_HEADS)]
    q_ge_k = (lax.broadcasted_iota(jnp.int32, (LANES, TQ), 1) >= lax.broadcasted_iota(jnp.int32, (LANES, TQ), 0))
    n_sub = CK // LANES
    n_pairs = ATTN_HEADS // 2

    def chunk_is_far(c):
        return (pq_min - ckmax_ref[batch, jnp.minimum(c, seq // CK - 1)]) >= FAR_N

    def stage_bias(c):
        off = chunk_off(c)
        pk_row = posk_ref[:, pl.ds(off, CK)]
        for j in range(n_sub):
            rows = slice(j * LANES, (j + 1) * LANES)
            g = c * n_sub + j
            pk_first, pk_consec, pk_max = (pinfo_ref[batch, r, g] for r in (0, 1, 2))
            all_far = (pq_min - pk_max) >= FAR_N
            all_masked = (off + j * LANES) > (i * TQ + TQ - 1)
            consecutive = (pq_consec > 0) & (pk_consec > 0)
            gap = pq_first - pk_first

            def fill_const(rows=rows):
                for h in range(ATTN_HEADS):
                    b_ref[h, rows, :] = jnp.full((LANES, TQ), far_bias[h], F32)

            def fill_gap0(rows=rows):
                for h in range(ATTN_HEADS):
                    b_ref[h, rows, :] = jnp.where(q_ge_k, toep_ref[h], zero_bias[h])

            def fill_gap128(rows=rows):
                for h in range(ATTN_HEADS):
                    b_ref[h, rows, :] = jnp.where(q_ge_k, far_bias[h], toep_ref[h])

            def fill_lookup(rows=rows):
                pk_sub = pk_row[:, rows]
                n_qk = jnp.clip(pq_col - pk_sub, 0, BIAS_TABLE_N - 1).astype(F32)
                n_kq = n_qk.T.astype(jnp.int32)
                for h in range(ATTN_HEADS):
                    tb = jnp.broadcast_to(tbl_ref[h:h + 1, :], (LANES, BIAS_TABLE_N))
                    b_ref[h, rows, :] = jnp.take_along_axis(tb, n_kq, axis=1)

            def fill_near(fill_gap0=fill_gap0, fill_gap128=fill_gap128, fill_lookup=fill_lookup,
                          consecutive=consecutive, gap=gap):
                lax.cond(consecutive & (gap == 0), fill_gap0,
                         lambda: lax.cond(consecutive & (gap == LANES), fill_gap128, fill_lookup))

            lax.cond(all_far | all_masked, fill_const, fill_near)

    def logits_phase(c, s_buf, const_bias, seen):
        off = chunk_off(c)
        mb, seen = selection_mask(c, seen)
        m_cur = []
        for pair in range(n_pairs):
            kc = k_ref[pl.ds(off, CK), pair * LANES:(pair + 1) * LANES]
            s2 = jnp.dot(kc, qm_ref[pair], preferred_element_type=F32)
            for sub, h in enumerate((2 * pair, 2 * pair + 1)):
                s = s2[:, sub * TQ:(sub + 1) * TQ] + mb
                if not const_bias:
                    s = s + b_ref[h]
                s_buf[h] = s
                top = col_reduce(_fold_rows(s, jnp.maximum), jnp.max)
                m_cur.append(top + far_bias[h] if const_bias else top)
        return m_cur, seen

    def update_phase(c, s_buf, m_cur, const_bias):
        off = chunk_off(c)
        m_prev = m_ref[...]
        l_prev = l_ref[...]
        if isinstance(m_cur, list):
            m_new = [jnp.maximum(m_prev[h:h + 1, :], m_cur[h]) for h in range(ATTN_HEADS)]
            alpha = [jnp.exp2(m_prev[h:h + 1, :] - m_new[h]) for h in range(ATTN_HEADS)]
        else:
            m_all = jnp.maximum(m_prev, m_cur)
            a_all = jnp.exp2(m_prev - m_all)
            m_new = [m_all[h:h + 1, :] for h in range(ATTN_HEADS)]
            alpha = [a_all[h:h + 1, :] for h in range(ATTN_HEADS)]
        l_new = []
        for pair in range(n_pairs):
            heads = (2 * pair, 2 * pair + 1)
            sub_m = [m_new[h] - far_bias[h] if const_bias else m_new[h] for h in heads]
            p2 = jnp.concatenate([jnp.exp2(s_buf[h] - sm).astype(BF16) for h, sm in zip(heads, sub_m)],
                                 axis=1)
            lhs = jnp.concatenate([vT_ref[pair * LANES:(pair + 1) * LANES, pl.ds(off, CK)], ones_rows],
                                  axis=0)
            out = jnp.dot(lhs, p2, preferred_element_type=F32)
            for sub, h in enumerate(heads):
                cols = slice(sub * TQ, (sub + 1) * TQ)
                acc = acc_refs[h]
                acc[...] = alpha[h] * acc[...] + out[sub * ATTN_HEAD_DIM:(sub + 1) * ATTN_HEAD_DIM, cols]
                l_new.append(alpha[h] * l_prev[h:h + 1, :] + out[LANES:LANES + 1, cols])
        l_ref[...] = jnp.concatenate(l_new, axis=0)
        m_ref[...] = jnp.concatenate(m_new, axis=0)

    n_far = lax.while_loop(lambda c: (c < nch) & chunk_is_far(c), lambda c: c + 1, jnp.int32(0))
    n_steps = n_far // 2
    def far_logits(c, s_buf, seen):
        m_cur, seen = logits_phase(c, s_buf, True, seen)
        return jnp.concatenate(m_cur, axis=0), seen

    no_ties = jnp.zeros((1, TQ), F32)
    m_first, seen_first = lax.cond(n_steps > 0, lambda: far_logits(0, s_ref, no_ties),
                                   lambda: (jnp.zeros((ATTN_HEADS, TQ), F32), no_ties))

    def far_step(t, carry):
        m_even, seen_even, _ = carry
        c = 2 * t
        m_odd, seen_odd = far_logits(c + 1, s2_ref, seen_even)
        update_phase(c, s_ref, m_even, True)
        m_even, seen_even = far_logits(jnp.minimum(c + 2, 2 * n_steps - 2), s_ref, seen_odd)
        update_phase(c + 1, s2_ref, m_odd, True)
        return m_even, seen_even, seen_odd

    _, _, seen = lax.fori_loop(0, n_steps, far_step, (m_first, seen_first, no_ties))

    def tail_chunk(c, seen):
        def run(const_bias):
            if not const_bias:
                stage_bias(c)
            m_cur, seen_out = logits_phase(c, s_ref, const_bias, seen)
            update_phase(c, s_ref, m_cur, const_bias)
            return seen_out

        return lax.cond(chunk_is_far(c), lambda: run(True), lambda: run(False))

    lax.fori_loop(2 * n_steps, nch, tail_chunk, seen)

    outT = jnp.concatenate([acc_refs[h][...] / l_ref[h:h + 1, :] for h in range(ATTN_HEADS)], axis=0)
    o_ref[...] = outT.T.astype(o_ref.dtype)


def _toeplitz_kernel(tbl_ref, o_ref):
    q = lax.broadcasted_iota(jnp.int32, (LANES, LANES), 1)
    k = lax.broadcasted_iota(jnp.int32, (LANES, LANES), 0)
    idx = (q - k) & (BIAS_TABLE_N - 1)
    for h in range(ATTN_HEADS):
        tb = jnp.broadcast_to(tbl_ref[h:h + 1, :], (LANES, BIAS_TABLE_N))
        o_ref[h] = jnp.take_along_axis(tb, idx, axis=1)


def _sparse_attention(fm, tokb, ik, iwT, positions, rel_bias):
    B, S, _ = tokb.shape
    k_sel = min(TOPK_MAX, S // 4)
    bucket = _t5_bucket_table()
    tbl = rel_bias[bucket].T.astype(F32) * LOG2E
    bias_c = jnp.stack([tbl[:, BIAS_TABLE_N - 1], tbl[:, 0]])
    toep = pl.pallas_call(
        _toeplitz_kernel,
        out_shape=jax.ShapeDtypeStruct((ATTN_HEADS, LANES, LANES), F32),
        name="bias_toeplitz",
    )(tbl)
    pos_row = positions.reshape(B, 1, S)
    pos_col = positions.reshape(B, S, 1)
    pos_t = positions.reshape(B, S // LANES, LANES)
    consec = jnp.all(pos_t - pos_t[:, :, :1] == jnp.arange(LANES, dtype=positions.dtype), axis=-1)
    pinfo = jnp.stack([pos_t[:, :, 0], consec.astype(jnp.int32), jnp.max(pos_t, axis=-1),
                       jnp.min(pos_t, axis=-1)], axis=1).astype(jnp.int32)
    ck_max = jnp.max(positions.reshape(B, S // CK, CK), axis=-1)
    tril = jnp.tril(jnp.ones((LANES, LANES), BF16))
    sq = pl.Squeezed()
    iq_blk = (2 * ATTN_W) // IDX_Q_W
    k_blk = (tokb.shape[2] - ATTN_W) // ATTN_W
    return pl.pallas_call(
        functools.partial(_attn_kernel, k_sel=k_sel, seq=S),
        grid=(B, S // TQ),
        in_specs=[pl.BlockSpec(memory_space=pltpu.SMEM),
                  pl.BlockSpec(memory_space=pltpu.SMEM),
                  pl.BlockSpec(memory_space=pltpu.SMEM),
                  pl.BlockSpec((sq, IDX_Q_W, TQ), lambda b, i: (b, iq_blk, i)),
                  pl.BlockSpec((sq, SUBLANES, TQ), lambda b, i: (b, 0, i)),
                  pl.BlockSpec((sq, TQ, 1), lambda b, i: (b, i, 0)),
                  pl.BlockSpec((sq, 1, S), lambda b, i: (b, 0, 0)),
                  pl.BlockSpec((sq, S, IDX_HEAD_DIM), lambda b, i: (b, 0, 0)),
                  pl.BlockSpec((sq, ATTN_W, TQ), lambda b, i: (b, 0, i)),
                  pl.BlockSpec((sq, S, ATTN_W), lambda b, i: (b, 0, k_blk)),
                  pl.BlockSpec((sq, ATTN_W, S), lambda b, i: (b, 1, 0)),
                  pl.BlockSpec((ATTN_HEADS, BIAS_TABLE_N), lambda b, i: (0, 0)),
                  pl.BlockSpec((ATTN_HEADS, LANES, LANES), lambda b, i: (0, 0, 0)),
                  pl.BlockSpec((LANES, LANES), lambda b, i: (0, 0))],
        out_specs=pl.BlockSpec((sq, TQ, ATTN_W), lambda b, i: (b, i, 0)),
        out_shape=jax.ShapeDtypeStruct((B, S, ATTN_W), BF16),
        scratch_shapes=[pltpu.VMEM((S, TQ), F32),
                        pltpu.VMEM((ATTN_HEADS, CK, TQ), F32),
                        pltpu.VMEM((ATTN_HEADS, CK, TQ), F32),
                        pltpu.VMEM((ATTN_HEADS, CK, TQ), F32),
                        pltpu.VMEM((ATTN_HEADS // 2, LANES, 2 * TQ), BF16),
                        pltpu.VMEM((ATTN_HEADS, TQ), F32),
                        pltpu.VMEM((ATTN_HEADS, TQ), F32)]
                       + [pltpu.VMEM((ATTN_HEAD_DIM, TQ), F32)] * ATTN_HEADS,
        compiler_params=_cparams(2),
        name="sparse_attention",
    )(bias_c, pinfo, ck_max, fm, iwT, pos_col, pos_row, ik, fm, tokb, fm, tbl, toep, tril)


def _retention_kernel(q_ref, k_ref, v_ref, g_ref, decay_ref, xi_ref, zeta_ref, gch_ref, o_ref, r_ref):
    @pl.when(pl.program_id(0) == 0)
    def _():
        r_ref[...] = jnp.zeros(r_ref.shape, F32)

    C = RET_CHUNK
    lane = lax.broadcasted_iota(jnp.int32, (C, LANES), 1)
    row = lax.broadcasted_iota(jnp.int32, (LANES, RET_V_DIM), 0)
    for pair in range(RET_HEADS // 2):
        for b in range(q_ref.shape[0]):
            q_pair = q_ref[b, :, pair * LANES:(pair + 1) * LANES]
            k_pair = k_ref[b, :, pair * LANES:(pair + 1) * LANES]
            v_pair = v_ref[b, :, 2 * pair * RET_V_DIM:(2 * pair + 2) * RET_V_DIM]
            r_pair = r_ref[b, pair]
            r_bf = r_pair.astype(BF16)
            for sub in range(2):
                h = 2 * pair + sub
                in_head = (lane >= sub * RET_QK_DIM) & (lane < (sub + 1) * RET_QK_DIM)
                qm = jnp.where(in_head, q_pair, jnp.zeros_like(q_pair))
                v_h = v_pair[:, sub * RET_V_DIM:(sub + 1) * RET_V_DIM]
                inner = lax.dot_general(qm, k_pair, (((1,), (1,)), ((), ())),
                                        preferred_element_type=F32) * decay_ref[h]
                o = (jnp.dot(inner.astype(BF16), v_h, preferred_element_type=F32)
                     + jnp.dot(qm, r_bf, preferred_element_type=F32) * xi_ref[h])
                mu = jnp.mean(o, axis=-1, keepdims=True)
                d = o - mu
                var = jnp.mean(d * d, axis=-1, keepdims=True)
                hn = d * lax.rsqrt(var + LN_EPS)
                gate = g_ref[b, :, h * RET_V_DIM:(h + 1) * RET_V_DIM].astype(F32)
                o_ref[b, :, h * RET_V_DIM:(h + 1) * RET_V_DIM] = (gate * hn).astype(o_ref.dtype)
            kz = (k_pair.astype(F32) * zeta_ref[pair]).astype(BF16)
            upd = lax.dot_general(kz, v_pair, (((0,), (0,)), ((), ())), preferred_element_type=F32)
            r_ref[b, pair] = (r_pair * gch_ref[pair]
                              + jnp.where(row < RET_QK_DIM, upd[:, :RET_V_DIM], upd[:, RET_V_DIM:]))


def _retention(qk, tokb, gates, B, S):
    C = RET_CHUNK
    H = RET_HEADS
    nc = S // C
    gamma = 1.0 - 2.0 ** (-5.0 - jnp.arange(H, dtype=F32))
    log_g = jnp.log(gamma)
    n = jnp.arange(C, dtype=F32)
    diff = n[:, None] - n[None, :]
    decay_in = jnp.where(diff[None] >= 0, jnp.exp(log_g[:, None, None] * jnp.maximum(diff, 0.0)[None]), 0.0)
    xi = jnp.exp(log_g[None, :] * (n[:, None] + 1.0))
    zeta = jnp.exp(log_g[None, :] * (C - 1.0 - n[:, None]))
    g_chunk = jnp.exp(log_g * C)
    xi_b = jnp.broadcast_to(xi.T[:, :, None], (H, C, RET_V_DIM))
    zeta_b = jnp.repeat(zeta, RET_QK_DIM, axis=1).reshape(C, H // 2, LANES).transpose(1, 0, 2)
    gch_b = jnp.broadcast_to(jnp.repeat(g_chunk, RET_QK_DIM).reshape(H // 2, LANES, 1),
                             (H // 2, LANES, RET_V_DIM))
    qk3, tok3, gate3 = (a.reshape(B, S, a.shape[-1]) for a in (qk, tokb, gates))
    out = pl.pallas_call(
        _retention_kernel,
        grid=(nc,),
        in_specs=[pl.BlockSpec((B, C, RET_QK_W), lambda i: (0, i, 0)),
                  pl.BlockSpec((B, C, RET_QK_W), lambda i: (0, i, 1)),
                  pl.BlockSpec((B, C, RET_V_W), lambda i: (0, i, 0)),
                  pl.BlockSpec((B, C, RET_V_W), lambda i: (0, i, 0)),
                  pl.BlockSpec((H, C, C), lambda i: (0, 0, 0)),
                  pl.BlockSpec((H, C, RET_V_DIM), lambda i: (0, 0, 0)),
                  pl.BlockSpec((H // 2, C, LANES), lambda i: (0, 0, 0)),
                  pl.BlockSpec((H // 2, LANES, RET_V_DIM), lambda i: (0, 0, 0))],
        out_specs=pl.BlockSpec((B, C, RET_V_W), lambda i: (0, i, 0)),
        out_shape=jax.ShapeDtypeStruct((B, S, RET_V_W), BF16),
        scratch_shapes=[pltpu.VMEM((B, H // 2, LANES, RET_V_DIM), F32)],
        compiler_params=_cparams(1),
        name="retention",
    )(qk3, qk3, tok3, gate3, decay_in, xi_b, zeta_b, gch_b)
    return out.reshape(B * S, RET_V_W)


def _layer_norm(z, g, b):
    mu = jnp.mean(z, axis=-1, keepdims=True)
    d = z - mu
    var = jnp.mean(d * d, axis=-1, keepdims=True)
    return d * lax.rsqrt(var + LN_EPS) * g + b


def _merge_kernel(x_ref, ya_ref, yr_ref, ga_ref, gr_ref, wa_ref, wr_ref, wo_ref, g_ref, b_ref,
                  x1_ref, x1b_ref):
    a = jnp.dot(ya_ref[...], wa_ref[...], preferred_element_type=F32)
    r = jnp.dot(yr_ref[...], wr_ref[...], preferred_element_type=F32)
    h = ga_ref[...].astype(F32) * a + gr_ref[...].astype(F32) * r
    mix = jnp.dot(h.astype(BF16), wo_ref[...], preferred_element_type=F32)
    x1 = _layer_norm(DEEPNORM_ALPHA * x_ref[...] + mix, g_ref[...], b_ref[...])
    x1_ref[...] = x1
    x1b_ref[...] = x1.astype(BF16)


def _merge(x, ya, yr, gates, wa, wr, wo, g, b, tm=512):
    T, D = x.shape
    tm = min(tm, T)
    row = lambda i: (i, 0)
    fixed = lambda i: (0, 0)
    return pl.pallas_call(
        _merge_kernel,
        grid=(T // tm,),
        in_specs=[pl.BlockSpec((tm, D), row),
                  pl.BlockSpec((tm, ya.shape[1]), row),
                  pl.BlockSpec((tm, yr.shape[1]), row),
                  pl.BlockSpec((tm, D), lambda i: (i, 1)),
                  pl.BlockSpec((tm, D), lambda i: (i, 2)),
                  pl.BlockSpec(wa.shape, fixed),
                  pl.BlockSpec(wr.shape, fixed),
                  pl.BlockSpec(wo.shape, fixed),
                  pl.BlockSpec((1, D), fixed),
                  pl.BlockSpec((1, D), fixed)],
        out_specs=[pl.BlockSpec((tm, D), row), pl.BlockSpec((tm, D), row)],
        out_shape=[jax.ShapeDtypeStruct((T, D), F32), jax.ShapeDtypeStruct((T, D), BF16)],
        compiler_params=_cparams(1),
        name="merge",
    )(x, ya, yr, gates, gates, wa, wr, wo, g, b)


def _ffn_kernel(x1b_ref, x1_ref, wu_ref, wd_ref, g_ref, b_ref, o_ref, acc_ref):
    f = pl.program_id(1)

    @pl.when(f == 0)
    def _():
        acc_ref[...] = jnp.zeros(acc_ref.shape, F32)

    hid = jnp.maximum(jnp.dot(x1b_ref[...], wu_ref[...], preferred_element_type=F32), 0.0)
    acc_ref[...] += jnp.dot((hid * hid).astype(BF16), wd_ref[...], preferred_element_type=F32)

    @pl.when(f == pl.num_programs(1) - 1)
    def _():
        o_ref[...] = _layer_norm(DEEPNORM_ALPHA * x1_ref[...] + acc_ref[...], g_ref[...], b_ref[...])


def _ffn(x1b, x1, wu, wd, g, b, tm=1024, tf=1024):
    T, D = x1.shape
    F = wu.shape[1]
    tm = min(tm, T)
    return pl.pallas_call(
        _ffn_kernel,
        grid=(T // tm, F // tf),
        in_specs=[pl.BlockSpec((tm, D), lambda i, f: (i, 0)),
                  pl.BlockSpec((tm, D), lambda i, f: (i, 0)),
                  pl.BlockSpec((D, tf), lambda i, f: (0, f)),
                  pl.BlockSpec((tf, D), lambda i, f: (f, 0)),
                  pl.BlockSpec((1, D), lambda i, f: (0, 0)),
                  pl.BlockSpec((1, D), lambda i, f: (0, 0))],
        out_specs=pl.BlockSpec((tm, D), lambda i, f: (i, 0)),
        out_shape=jax.ShapeDtypeStruct((T, D), F32),
        scratch_shapes=[pltpu.VMEM((tm, D), F32)],
        compiler_params=_cparams(2),
        name="ffn",
    )(x1b, x1, wu, wd, g, b)


def _rot_half_weight(wT):
    N, D = wT.shape
    half = RET_QK_DIM // 2
    wh = wT.reshape(N // RET_QK_DIM, 2, half, D)
    return jnp.stack([-wh[:, 1], wh[:, 0]], axis=1).reshape(N, D)


def kernel(x, positions, w_in, rel_bias, idx_k_ln_g, idx_k_ln_b, w_attn_branch, w_ret_branch,
           w_out, ln_mix_g, ln_mix_b, w_up, w_down, ln_ffn_g, ln_ffn_b):
    B, S, D = x.shape
    T = B * S
    sizes = (ATTN_W, ATTN_W, ATTN_W, IDX_Q_W, IDX_HEAD_DIM, IDX_HEADS,
             RET_QK_W, RET_QK_W, RET_V_W, RET_V_W, D, D)
    offs = [0] + [int(o) for o in np.cumsum(sizes)]
    cos, sin = _rope_tables(positions)
    xf = x.reshape(T, D)
    for l in range(DEPTH):
        wT_f32 = jnp.swapaxes(w_in[l], 0, 1)
        wT = wT_f32.astype(BF16)
        rows = [wT[offs[k]:offs[k + 1]] for k in range(len(sizes))]
        (w_qa, w_ka, w_va, w_iq, w_ik, w_iw, w_qr, w_kr, w_vr, w_gr, w_ga, w_gtr) = rows
        xb = xf.astype(BF16)
        w_qa = (wT_f32[offs[0]:offs[1]] * (ATTN_HEAD_DIM ** -0.5 * LOG2E)).astype(BF16)
        w_kr = w_kr * (RET_QK_DIM ** -0.5)

        fm = _proj_t(xb, jnp.concatenate([w_qa, w_va, w_iq], axis=0), B, S, BF16, tn=1280)
        tokb = _proj(xb, jnp.concatenate([w_vr, w_ka], axis=0), BF16, tn=1536)
        gates = _proj_gates(xb, jnp.concatenate([w_gr, w_ga, w_gtr], axis=0), tn=D)
        w_rope = jnp.concatenate([w_qr, w_kr], axis=0)
        w_rope_rot = jnp.concatenate([_rot_half_weight(w_qr), _rot_half_weight(w_kr)], axis=0)
        qk_r = _proj_rope(xb, w_rope, w_rope_rot, cos, sin, tn=1024)
        pad = LANES - IDX_HEAD_DIM - IDX_HEADS
        w_idx = jnp.concatenate([w_ik, w_iw, jnp.zeros((pad, D), BF16)], axis=0)
        g_pad = jnp.concatenate([idx_k_ln_g[l], jnp.zeros((LANES - IDX_HEAD_DIM,), F32)]).reshape(1, LANES)
        b_pad = jnp.concatenate([idx_k_ln_b[l], jnp.zeros((LANES - IDX_HEAD_DIM,), F32)]).reshape(1, LANES)
        idx = _proj_idx(xb, w_idx, g_pad, b_pad,
                        (IDX_HEAD_DIM ** -0.5) * (IDX_HEADS ** -0.5)).reshape(B, S, LANES)
        ik = idx[:, :, :IDX_HEAD_DIM].astype(BF16)
        iwT = jnp.swapaxes(idx[:, :, IDX_HEAD_DIM:IDX_HEAD_DIM + SUBLANES], 1, 2)

        y_a = _sparse_attention(fm, tokb.reshape(B, S, -1), ik, iwT, positions, rel_bias)
        y_r = _retention(qk_r, tokb, gates, B, S)
        x1, x1b = _merge(xf, y_a.reshape(T, ATTN_W), y_r, gates,
                         w_attn_branch[l].astype(BF16), w_ret_branch[l].astype(BF16),
                         w_out[l].astype(BF16), ln_mix_g[l].reshape(1, D), ln_mix_b[l].reshape(1, D))
        xf = _ffn(x1b, x1, w_up[l].astype(BF16), w_down[l].astype(BF16),
                  ln_ffn_g[l].reshape(1, D), ln_ffn_b[l].reshape(1, D))
    return xf.reshape(B, S, D)
```

```python
import functools
import math

import numpy as np
import jax
import jax.numpy as jnp
from jax import lax
from jax.experimental import pallas as pl
from jax.experimental.pallas import tpu as pltpu

F32 = jnp.float32
BF16 = jnp.bfloat16

ATTN_HEADS = 8
ATTN_HEAD_DIM = 64
ATTN_W = ATTN_HEADS * ATTN_HEAD_DIM
IDX_HEADS = 4
IDX_HEAD_DIM = 64
IDX_Q_W = IDX_HEADS * IDX_HEAD_DIM
TOPK_MAX = 256
RET_HEADS = 8
RET_QK_DIM = 64
RET_V_DIM = 128
RET_QK_W = RET_HEADS * RET_QK_DIM
RET_V_W = RET_HEADS * RET_V_DIM
RET_CHUNK = 128
ROPE_BASE = 10000.0
NUM_BUCKETS = 32
MAX_DISTANCE = 128
LN_EPS = 1e-5
DEPTH = 1
DEEPNORM_ALPHA = (2.0 * DEPTH) ** 0.25

LANES = 128
SUBLANES = 8
VMEM_LIMIT = 56 * 1024 * 1024

TQ = 128
CK = 512
NEG = -1e30
LOG2E = math.log2(math.e)
BISECT_ROUNDS = 20
BIAS_TABLE_N = 128
FAR_N = 113


def _cparams(n_grid):
    return pltpu.CompilerParams(
        dimension_semantics=("arbitrary",) * n_grid,
        vmem_limit_bytes=VMEM_LIMIT)


def _trig_kernel(pos_ref, inv_ref, cos_ref, sin_ref):
    ang = pos_ref[...] * inv_ref[...]
    cos_ref[...] = jnp.cos(ang)
    sin_ref[...] = jnp.sin(ang)


def _rope_tables(positions):
    B, S = positions.shape
    half = RET_QK_DIM // 2
    inv = ROPE_BASE ** (-jnp.arange(half, dtype=F32) / half)
    per_row = LANES // half
    rows = B * S // per_row
    pos_e = jnp.repeat(positions.astype(F32).reshape(rows, per_row), half, axis=1)
    inv_e = jnp.tile(inv, per_row).reshape(1, LANES)
    tr = min(rows, 1024)
    cos, sin = pl.pallas_call(
        _trig_kernel,
        grid=(rows // tr,),
        in_specs=[pl.BlockSpec((tr, LANES), lambda i: (i, 0)),
                  pl.BlockSpec((1, LANES), lambda i: (0, 0))],
        out_specs=[pl.BlockSpec((tr, LANES), lambda i: (i, 0))] * 2,
        out_shape=[jax.ShapeDtypeStruct((rows, LANES), F32)] * 2,
        compiler_params=_cparams(1),
        name="rope_tables",
    )(pos_e, inv_e)
    cos = jnp.tile(cos.reshape(B * S, half), (1, per_row))
    sin = jnp.tile(sin.reshape(B * S, half), (1, per_row))
    return cos, sin


def _x_wt(x, wT):
    return lax.dot_general(x, wT, (((1,), (1,)), ((), ())), preferred_element_type=F32)


def _proj_kernel(x_ref, w_ref, o_ref):
    o_ref[...] = _x_wt(x_ref[...], w_ref[...]).astype(o_ref.dtype)


def _proj(xb, wT, out_dtype, tm=1024, tn=512):
    T, D = xb.shape
    N = wT.shape[0]
    tn = min(N, tn)
    tm = min(tm, T)
    return pl.pallas_call(
        _proj_kernel,
        grid=(T // tm, N // tn),
        in_specs=[pl.BlockSpec((tm, D), lambda i, j: (i, 0)),
                  pl.BlockSpec((tn, D), lambda i, j: (j, 0))],
        out_specs=pl.BlockSpec((tm, tn), lambda i, j: (i, j)),
        out_shape=jax.ShapeDtypeStruct((T, N), out_dtype),
        compiler_params=_cparams(2),
        name="proj",
    )(xb, wT)


def _proj_gates_kernel(x_ref, w_ref, o_ref):
    acc = _x_wt(x_ref[...], w_ref[...])
    sig = 0.5 * jnp.tanh(0.5 * acc) + 0.5
    o_ref[...] = jnp.where(pl.program_id(1) == 0, acc * sig, sig).astype(o_ref.dtype)


def _proj_gates(xb, wT, tn, tm=1024):
    T, D = xb.shape
    N = wT.shape[0]
    tm = min(tm, T)
    return pl.pallas_call(
        _proj_gates_kernel,
        grid=(T // tm, N // tn),
        in_specs=[pl.BlockSpec((tm, D), lambda i, j: (i, 0)),
                  pl.BlockSpec((tn, D), lambda i, j: (j, 0))],
        out_specs=pl.BlockSpec((tm, tn), lambda i, j: (i, j)),
        out_shape=jax.ShapeDtypeStruct((T, N), BF16),
        compiler_params=_cparams(2),
        name="proj_gates",
    )(xb, wT)


def _proj_t_kernel(wT_ref, x_ref, o_ref):
    acc = lax.dot_general(wT_ref[...], x_ref[...], (((1,), (1,)), ((), ())), preferred_element_type=F32)
    o_ref[...] = acc.astype(o_ref.dtype)


def _proj_t(xb, wT, B, S, out_dtype, tm=1024, tn=640):
    T, D = xb.shape
    N = wT.shape[0]
    tn = min(N, tn)
    tm = min(tm, S)
    nsb = S // tm
    return pl.pallas_call(
        _proj_t_kernel,
        grid=(T // tm, N // tn),
        in_specs=[pl.BlockSpec((tn, D), lambda i, j: (j, 0)),
                  pl.BlockSpec((tm, D), lambda i, j: (i, 0))],
        out_specs=pl.BlockSpec((pl.Squeezed(), tn, tm), lambda i, j: (i // nsb, j, i % nsb)),
        out_shape=jax.ShapeDtypeStruct((B, N, S), out_dtype),
        compiler_params=_cparams(2),
        name="proj_t",
    )(wT, xb)


def _proj_rope_kernel(x_ref, w_ref, wr_ref, cos_ref, sin_ref, o_ref):
    x = x_ref[...]
    a = _x_wt(x, w_ref[...])
    r = _x_wt(x, wr_ref[...])
    reps = a.shape[1] // LANES
    cos = jnp.concatenate([cos_ref[...]] * reps, axis=1)
    sin = jnp.concatenate([sin_ref[...]] * reps, axis=1)
    o_ref[...] = (a * cos + r * sin).astype(o_ref.dtype)


def _proj_rope(xb, wT, wT_rot, cos, sin, tm=1024, tn=512):
    T, D = xb.shape
    N = wT.shape[0]
    tm = min(tm, T)
    return pl.pallas_call(
        _proj_rope_kernel,
        grid=(T // tm, N // tn),
        in_specs=[pl.BlockSpec((tm, D), lambda i, j: (i, 0)),
                  pl.BlockSpec((tn, D), lambda i, j: (j, 0)),
                  pl.BlockSpec((tn, D), lambda i, j: (j, 0)),
                  pl.BlockSpec((tm, LANES), lambda i, j: (i, 0)),
                  pl.BlockSpec((tm, LANES), lambda i, j: (i, 0))],
        out_specs=pl.BlockSpec((tm, tn), lambda i, j: (i, j)),
        out_shape=jax.ShapeDtypeStruct((T, N), BF16),
        compiler_params=_cparams(2),
        name="proj_rope",
    )(xb, wT, wT_rot, cos, sin)


def _proj_idx_kernel(x_ref, w_ref, g_ref, b_ref, o_ref, xb_ref, *, iw_scale):
    xb = x_ref[...].astype(BF16)
    xb_ref[...] = xb
    acc = _x_wt(xb, w_ref[...])
    lane = lax.broadcasted_iota(jnp.int32, acc.shape, 1)
    is_k = lane < IDX_HEAD_DIM
    mu = jnp.sum(jnp.where(is_k, acc, 0.0), axis=-1, keepdims=True) / IDX_HEAD_DIM
    d = acc - mu
    var = jnp.sum(jnp.where(is_k, d * d, 0.0), axis=-1, keepdims=True) / IDX_HEAD_DIM
    ln = d * lax.rsqrt(var + LN_EPS) * g_ref[...] + b_ref[...]
    o_ref[...] = jnp.where(is_k, ln, acc * iw_scale)


def _proj_idx(x, w_pad, g_pad, b_pad, iw_scale, tm=1024):
    T, D = x.shape
    tm = min(tm, T)
    return pl.pallas_call(
        functools.partial(_proj_idx_kernel, iw_scale=iw_scale),
        grid=(T // tm,),
        in_specs=[pl.BlockSpec((tm, D), lambda i: (i, 0)),
                  pl.BlockSpec((LANES, D), lambda i: (0, 0)),
                  pl.BlockSpec((1, LANES), lambda i: (0, 0)),
                  pl.BlockSpec((1, LANES), lambda i: (0, 0))],
        out_specs=[pl.BlockSpec((tm, LANES), lambda i: (i, 0)), pl.BlockSpec((tm, D), lambda i: (i, 0))],
        out_shape=[jax.ShapeDtypeStruct((T, LANES), F32), jax.ShapeDtypeStruct((T, D), BF16)],
        compiler_params=_cparams(1),
        name="proj_idx",
    )(x, w_pad, g_pad, b_pad)


def _t5_bucket_table():
    n = np.arange(BIAS_TABLE_N)
    max_exact = NUM_BUCKETS // 2
    nf = np.maximum(n, 1).astype(np.float64)
    large = max_exact + (np.log(nf / max_exact) / math.log(MAX_DISTANCE / max_exact)
                         * (NUM_BUCKETS - max_exact)).astype(np.int64)
    large = np.minimum(large, NUM_BUCKETS - 1)
    bucket = np.where(n < max_exact, n, large)
    assert np.all(bucket[FAR_N:] == NUM_BUCKETS - 1) and bucket[FAR_N - 1] != NUM_BUCKETS - 1
    return bucket.astype(np.int32)


def _fold_rows(a, op):
    parts = [a[r:r + SUBLANES] for r in range(0, a.shape[0], SUBLANES)]
    while len(parts) > 1:
        nxt = [op(parts[k], parts[k + 1]) for k in range(0, len(parts) - 1, 2)]
        if len(parts) % 2:
            nxt.append(parts[-1])
        parts = nxt
    return parts[0]


def _attn_kernel(biasc_ref, pinfo_ref, ckmax_ref, iqT_ref, iwT_ref, posqc_ref, posk_ref, ik_ref, qT_ref, k_ref, vT_ref,
                 tbl_ref, toep_ref, tril_ref, o_ref, sc_ref, s_ref, s2_ref, b_ref, qm_ref, m_ref, l_ref, *acc_refs, k_sel, seq):
    i = pl.program_id(1)
    nch = (i * TQ + TQ + CK - 1) // CK
    q_idx = i * TQ + lax.broadcasted_iota(jnp.int32, (1, TQ), 1)
    kf = float(k_sel)

    def chunk_off(c):
        return pl.multiple_of(c * CK, CK)

    def key_idx(off):
        return off + lax.broadcasted_iota(jnp.int32, (CK, TQ), 0)

    def col_reduce(part, op):
        return op(part, axis=0, keepdims=True)

    iqT = iqT_ref[...]
    iwT = iwT_ref[...]
    iq_wide = jnp.concatenate([iqT[h * IDX_HEAD_DIM:(h + 1) * IDX_HEAD_DIM, :] for h in range(IDX_HEADS)], axis=1)

    def score_matmul(c, z_buf):
        ikc = ik_ref[pl.ds(chunk_off(c), CK), :]
        z = jnp.dot(ikc, iq_wide, preferred_element_type=F32)
        for h in range(IDX_HEADS):
            z_buf[h] = z[:, h * TQ:(h + 1) * TQ]

    def score_finish(c, z_buf, carry, masked):
        mn, mx = carry
        off = chunk_off(c)
        s = None
        for h in range(IDX_HEADS):
            t = jnp.maximum(z_buf[h], 0.0) * iwT[h:h + 1, :]
            s = t if s is None else s + t
        if masked:
            causal = key_idx(off) <= q_idx
            s_lo = jnp.where(causal, s, -jnp.inf)
            s_hi = jnp.where(causal, s, jnp.inf)
        else:
            s_lo = s_hi = s
        sc_ref[pl.ds(off, CK), :] = s_lo
        mn = jnp.minimum(mn, _fold_rows(s_hi, jnp.minimum))
        mx = jnp.maximum(mx, _fold_rows(s_lo, jnp.maximum))
        return mn, mx

    def score_chunk(c, carry, masked):
        score_matmul(c, s_ref)
        return score_finish(c, s_ref, carry, masked)

    n_inner = nch - 1
    n_pairs_sc = n_inner // 2

    @pl.when(n_pairs_sc > 0)
    def _():
        score_matmul(0, s_ref)

    def score_step(t, carry):
        c = 2 * t
        score_matmul(c + 1, s2_ref)
        carry = score_finish(c, s_ref, carry, False)
        score_matmul(jnp.minimum(c + 2, 2 * n_pairs_sc - 2), s_ref)
        return score_finish(c + 1, s2_ref, carry, False)

    mn8, mx8 = lax.fori_loop(0, n_pairs_sc, score_step,
                             (jnp.full((SUBLANES, TQ), jnp.inf, F32), jnp.full((SUBLANES, TQ), -jnp.inf, F32)))
    mn8, mx8 = lax.fori_loop(2 * n_pairs_sc, n_inner, functools.partial(score_chunk, masked=False), (mn8, mx8))
    mn8, mx8 = score_chunk(nch - 1, (mn8, mx8), masked=True)
    mn = col_reduce(mn8, jnp.min)
    mx = col_reduce(mx8, jnp.max)

    def over_chunks(fn, init):
        def one(c, carry):
            return fn(sc_ref[pl.ds(chunk_off(c), CK), :], carry)

        carry = lax.fori_loop(0, nch // 2, lambda t, carry: one(2 * t + 1, one(2 * t, carry)), init)
        return lax.fori_loop(2 * (nch // 2), nch, one, carry)

    def count(pred_fn):
        acc = over_chunks(lambda blk, acc: acc + _fold_rows(jnp.where(pred_fn(blk), 1.0, 0.0), jnp.add),
                          jnp.zeros((SUBLANES, TQ), F32))
        return col_reduce(acc, jnp.sum)

    def bisect_round(_, st):
        lo, hi, c_lo = st
        mid = 0.5 * (lo + hi)
        c = count(lambda blk: blk >= mid)
        ok = c >= kf
        return jnp.where(ok, mid, lo), jnp.where(ok, hi, mid), jnp.where(ok, c, c_lo)

    c_all = (q_idx + 1).astype(F32)
    lo, hi, c_lo = lax.fori_loop(0, BISECT_ROUNDS, bisect_round, (mn, mx, c_all))

    cur0 = col_reduce(over_chunks(
        lambda blk, acc: jnp.minimum(acc, _fold_rows(jnp.where(blk >= lo, blk, jnp.inf), jnp.minimum)),
        jnp.full((SUBLANES, TQ), jnp.inf, F32)), jnp.min)

    def walk_cond(st):
        return st[3] > 0.0

    def walk_body(st):
        cur, c_ge, _, _ = st

        def body(blk, carry):
            cnt, nxt = carry
            gt = blk > cur
            cnt = cnt + _fold_rows(jnp.where(gt, 1.0, 0.0), jnp.add)
            nxt = jnp.minimum(nxt, _fold_rows(jnp.where(gt, blk, jnp.inf), jnp.minimum))
            return cnt, nxt

        cnt, nxt = over_chunks(body, (jnp.zeros((SUBLANES, TQ), F32), jnp.full((SUBLANES, TQ), jnp.inf, F32)))
        c_gt = col_reduce(cnt, jnp.sum)
        nxt = col_reduce(nxt, jnp.min)
        adv = c_gt >= kf
        cur = jnp.where(adv, nxt, cur)
        c_ge = jnp.where(adv, c_gt, c_ge)
        return cur, c_ge, c_gt, jnp.max(jnp.where(adv, 1.0, 0.0))

    tau, c_ge, c_gt, _ = lax.while_loop(
        walk_cond, walk_body, (cur0, c_lo, jnp.zeros((1, TQ), F32), jnp.float32(1.0)))

    room = kf - c_gt

    def selection_mask(c, seen):
        off = chunk_off(c)
        tiles = range(CK // LANES)
        blks = [sc_ref[pl.ds(off + j * LANES, LANES), :] for j in tiles]
        ties = [blk == tau for blk in blks]
        local = [jnp.dot(tril_ref[...], jnp.where(tie, 1.0, 0.0).astype(BF16), preferred_element_type=F32)
                 for tie in ties]
        masks = []
        for j in tiles:
            keep = (blks[j] > tau) | (ties[j] & (local[j] + seen <= room))
            masks.append(jnp.where(keep, 0.0, NEG))
            seen = seen + local[j][LANES - 1:LANES, :]
        return jnp.concatenate(masks, axis=0), seen

    m_ref[...] = jnp.full(m_ref.shape, NEG, F32)
    l_ref[...] = jnp.zeros(l_ref.shape, F32)
    for acc in acc_refs:
        acc[...] = jnp.zeros(acc.shape, F32)

    rowi = lax.broadcasted_iota(jnp.int32, (LANES, TQ), 0)
    for pair in range(ATTN_HEADS // 2):
        qp = qT_ref[pair * LANES:(pair + 1) * LANES, :]
        zero = jnp.zeros_like(qp)
        qm_ref[pair] = jnp.concatenate([jnp.where(rowi < ATTN_HEAD_DIM, qp, zero),
                                        jnp.where(rowi >= ATTN_HEAD_DIM, qp, zero)], axis=1)
    ones_rows = jnp.ones((2 * SUBLANES, CK), BF16)

    pq_col = posqc_ref[...]
    batch = pl.program_id(0)
    pq_first, pq_consec, pq_min = (pinfo_ref[batch, r, i] for r in (0, 1, 3))
    far_bias = [biasc_ref[0, h] for h in range(ATTN_HEADS)]
    zero_bias = [biasc_ref[1, h] for h in range(ATTN_HEADS)]
    q_ge_k = (lax.broadcasted_iota(jnp.int32, (LANES, TQ), 1) >= lax.broadcasted_iota(jnp.int32, (LANES, TQ), 0))
    n_sub = CK // LANES
    n_pairs = ATTN_HEADS // 2

    def chunk_is_far(c):
        return (pq_min - ckmax_ref[batch, jnp.minimum(c, seq // CK - 1)]) >= FAR_N

    def stage_bias(c):
        off = chunk_off(c)
        pk_row = posk_ref[:, pl.ds(off, CK)]
        for j in range(n_sub):
            rows = slice(j * LANES, (j + 1) * LANES)
            g = c * n_sub + j
            pk_first, pk_consec, pk_max = (pinfo_ref[batch, r, g] for r in (0, 1, 2))
            all_far = (pq_min - pk_max) >= FAR_N
            all_masked = (off + j * LANES) > (i * TQ + TQ - 1)
            consecutive = (pq_consec > 0) & (pk_consec > 0)
            gap = pq_first - pk_first

            def fill_const(rows=rows):
                for h in range(ATTN_HEADS):
                    b_ref[h, rows, :] = jnp.full((LANES, TQ), far_bias[h], F32)

            def fill_gap0(rows=rows):
                for h in range(ATTN_HEADS):
                    b_ref[h, rows, :] = jnp.where(q_ge_k, toep_ref[h], zero_bias[h])

            def fill_gap128(rows=rows):
                for h in range(ATTN_HEADS):
                    b_ref[h, rows, :] = jnp.where(q_ge_k, far_bias[h], toep_ref[h])

            def fill_lookup(rows=rows):
                pk_sub = pk_row[:, rows]
                n_qk = jnp.clip(pq_col - pk_sub, 0, BIAS_TABLE_N - 1).astype(F32)
                n_kq = n_qk.T.astype(jnp.int32)
                for h in range(ATTN_HEADS):
                    tb = jnp.broadcast_to(tbl_ref[h:h + 1, :], (LANES, BIAS_TABLE_N))
                    b_ref[h, rows, :] = jnp.take_along_axis(tb, n_kq, axis=1)

            def fill_near(fill_gap0=fill_gap0, fill_gap128=fill_gap128, fill_lookup=fill_lookup,
                          consecutive=consecutive, gap=gap):
                lax.cond(consecutive & (gap == 0), fill_gap0,
                         lambda: lax.cond(consecutive & (gap == LANES), fill_gap128, fill_lookup))

            lax.cond(all_far | all_masked, fill_const, fill_near)

    def logits_phase(c, s_buf, const_bias, seen):
        off = chunk_off(c)
        mb, seen = selection_mask(c, seen)
        m_cur = []
        for pair in range(n_pairs):
            kc = k_ref[pl.ds(off, CK), pair * LANES:(pair + 1) * LANES]
            s2 = jnp.dot(kc, qm_ref[pair], preferred_element_type=F32)
            for sub, h in enumerate((2 * pair, 2 * pair + 1)):
                s = s2[:, sub * TQ:(sub + 1) * TQ] + mb
                if not const_bias:
                    s = s + b_ref[h]
                s_buf[h] = s
                top = col_reduce(_fold_rows(s, jnp.maximum), jnp.max)
                m_cur.append(top + far_bias[h] if const_bias else top)
        return m_cur, seen

    def update_phase(c, s_buf, m_cur, const_bias):
        off = chunk_off(c)
        m_prev = m_ref[...]
        l_prev = l_ref[...]
        if isinstance(m_cur, list):
            m_new = [jnp.maximum(m_prev[h:h + 1, :], m_cur[h]) for h in range(ATTN_HEADS)]
            alpha = [jnp.exp2(m_prev[h:h + 1, :] - m_new[h]) for h in range(ATTN_HEADS)]
        else:
            m_all = jnp.maximum(m_prev, m_cur)
            a_all = jnp.exp2(m_prev - m_all)
            m_new = [m_all[h:h + 1, :] for h in range(ATTN_HEADS)]
            alpha = [a_all[h:h + 1, :] for h in range(ATTN_HEADS)]
        l_new = []
        for pair in range(n_pairs):
            heads = (2 * pair, 2 * pair + 1)
            sub_m = [m_new[h] - far_bias[h] if const_bias else m_new[h] for h in heads]
            p2 = jnp.concatenate([jnp.exp2(s_buf[h] - sm).astype(BF16) for h, sm in zip(heads, sub_m)],
                                 axis=1)
            lhs = jnp.concatenate([vT_ref[pair * LANES:(pair + 1) * LANES, pl.ds(off, CK)], ones_rows],
                                  axis=0)
            out = jnp.dot(lhs, p2, preferred_element_type=F32)
            for sub, h in enumerate(heads):
                cols = slice(sub * TQ, (sub + 1) * TQ)
                acc = acc_refs[h]
                acc[...] = alpha[h] * acc[...] + out[sub * ATTN_HEAD_DIM:(sub + 1) * ATTN_HEAD_DIM, cols]
                l_new.append(alpha[h] * l_prev[h:h + 1, :] + out[LANES:LANES + 1, cols])
        l_ref[...] = jnp.concatenate(l_new, axis=0)
        m_ref[...] = jnp.concatenate(m_new, axis=0)

    n_far = lax.while_loop(lambda c: (c < nch) & chunk_is_far(c), lambda c: c + 1, jnp.int32(0))
    n_steps = n_far // 2
    def far_logits(c, s_buf, seen):
        m_cur, seen = logits_phase(c, s_buf, True, seen)
        return jnp.concatenate(m_cur, axis=0), seen

    no_ties = jnp.zeros((1, TQ), F32)
    m_first, seen_first = lax.cond(n_steps > 0, lambda: far_logits(0, s_ref, no_ties),
                                   lambda: (jnp.zeros((ATTN_HEADS, TQ), F32), no_ties))

    def far_step(t, carry):
        m_even, seen_even, _ = carry
        c = 2 * t
        m_odd, seen_odd = far_logits(c + 1, s2_ref, seen_even)
        update_phase(c, s_ref, m_even, True)
        m_even, seen_even = far_logits(jnp.minimum(c + 2, 2 * n_steps - 2), s_ref, seen_odd)
        update_phase(c + 1, s2_ref, m_odd, True)
        return m_even, seen_even, seen_odd

    _, _, seen = lax.fori_loop(0, n_steps, far_step, (m_first, seen_first, no_ties))

    def tail_chunk(c, seen):
        def run(const_bias):
            if not const_bias:
                stage_bias(c)
            m_cur, seen_out = logits_phase(c, s_ref, const_bias, seen)
            update_phase(c, s_ref, m_cur, const_bias)
            return seen_out

        return lax.cond(chunk_is_far(c), lambda: run(True), lambda: run(False))

    lax.fori_loop(2 * n_steps, nch, tail_chunk, seen)

    outT = jnp.concatenate([acc_refs[h][...] / l_ref[h:h + 1, :] for h in range(ATTN_HEADS)], axis=0)
    o_ref[...] = outT.T.astype(o_ref.dtype)


def _toeplitz_kernel(tbl_ref, o_ref):
    q = lax.broadcasted_iota(jnp.int32, (LANES, LANES), 1)
    k = lax.broadcasted_iota(jnp.int32, (LANES, LANES), 0)
    idx = (q - k) & (BIAS_TABLE_N - 1)
    for h in range(ATTN_HEADS):
        tb = jnp.broadcast_to(tbl_ref[h:h + 1, :], (LANES, BIAS_TABLE_N))
        o_ref[h] = jnp.take_along_axis(tb, idx, axis=1)


def _sparse_attention(fm, tokb, ik, iwT, positions, rel_bias):
    B, S, _ = tokb.shape
    k_sel = min(TOPK_MAX, S // 4)
    bucket = _t5_bucket_table()
    tbl = rel_bias[bucket].T.astype(F32) * LOG2E
    bias_c = jnp.stack([tbl[:, BIAS_TABLE_N - 1], tbl[:, 0]])
    toep = pl.pallas_call(
        _toeplitz_kernel,
        out_shape=jax.ShapeDtypeStruct((ATTN_HEADS, LANES, LANES), F32),
        name="bias_toeplitz",
    )(tbl)
    pos_row = positions.reshape(B, 1, S)
    pos_col = positions.reshape(B, S, 1)
    pos_t = positions.reshape(B, S // LANES, LANES)
    consec = jnp.all(pos_t - pos_t[:, :, :1] == jnp.arange(LANES, dtype=positions.dtype), axis=-1)
    pinfo = jnp.stack([pos_t[:, :, 0], consec.astype(jnp.int32), jnp.max(pos_t, axis=-1),
                       jnp.min(pos_t, axis=-1)], axis=1).astype(jnp.int32)
    ck_max = jnp.max(positions.reshape(B, S // CK, CK), axis=-1)
    tril = jnp.tril(jnp.ones((LANES, LANES), BF16))
    sq = pl.Squeezed()
    iq_blk = (2 * ATTN_W) // IDX_Q_W
    k_blk = (tokb.shape[2] - ATTN_W) // ATTN_W
    return pl.pallas_call(
        functools.partial(_attn_kernel, k_sel=k_sel, seq=S),
        grid=(B, S // TQ),
        in_specs=[pl.BlockSpec(memory_space=pltpu.SMEM),
                  pl.BlockSpec(memory_space=pltpu.SMEM),
                  pl.BlockSpec(memory_space=pltpu.SMEM),
                  pl.BlockSpec((sq, IDX_Q_W, TQ), lambda b, i: (b, iq_blk, i)),
                  pl.BlockSpec((sq, SUBLANES, TQ), lambda b, i: (b, 0, i)),
                  pl.BlockSpec((sq, TQ, 1), lambda b, i: (b, i, 0)),
                  pl.BlockSpec((sq, 1, S), lambda b, i: (b, 0, 0)),
                  pl.BlockSpec((sq, S, IDX_HEAD_DIM), lambda b, i: (b, 0, 0)),
                  pl.BlockSpec((sq, ATTN_W, TQ), lambda b, i: (b, 0, i)),
                  pl.BlockSpec((sq, S, ATTN_W), lambda b, i: (b, 0, k_blk)),
                  pl.BlockSpec((sq, ATTN_W, S), lambda b, i: (b, 1, 0)),
                  pl.BlockSpec((ATTN_HEADS, BIAS_TABLE_N), lambda b, i: (0, 0)),
                  pl.BlockSpec((ATTN_HEADS, LANES, LANES), lambda b, i: (0, 0, 0)),
                  pl.BlockSpec((LANES, LANES), lambda b, i: (0, 0))],
        out_specs=pl.BlockSpec((sq, TQ, ATTN_W), lambda b, i: (b, i, 0)),
        out_shape=jax.ShapeDtypeStruct((B, S, ATTN_W), BF16),
        scratch_shapes=[pltpu.VMEM((S, TQ), F32),
                        pltpu.VMEM((ATTN_HEADS, CK, TQ), F32),
                        pltpu.VMEM((ATTN_HEADS, CK, TQ), F32),
                        pltpu.VMEM((ATTN_HEADS, CK, TQ), F32),
                        pltpu.VMEM((ATTN_HEADS // 2, LANES, 2 * TQ), BF16),
                        pltpu.VMEM((ATTN_HEADS, TQ), F32),
                        pltpu.VMEM((ATTN_HEADS, TQ), F32)]
                       + [pltpu.VMEM((ATTN_HEAD_DIM, TQ), F32)] * ATTN_HEADS,
        compiler_params=_cparams(2),
        name="sparse_attention",
    )(bias_c, pinfo, ck_max, fm, iwT, pos_col, pos_row, ik, fm, tokb, fm, tbl, toep, tril)


def _retention_kernel(q_ref, k_ref, v_ref, g_ref, decay_ref, xi_ref, zeta_ref, gch_ref, o_ref, r_ref):
    @pl.when(pl.program_id(0) == 0)
    def _():
        r_ref[...] = jnp.zeros(r_ref.shape, F32)

    C = RET_CHUNK
    lane = lax.broadcasted_iota(jnp.int32, (C, LANES), 1)
    row = lax.broadcasted_iota(jnp.int32, (LANES, RET_V_DIM), 0)
    for pair in range(RET_HEADS // 2):
        for b in range(q_ref.shape[0]):
            q_pair = q_ref[b, :, pair * LANES:(pair + 1) * LANES]
            k_pair = k_ref[b, :, pair * LANES:(pair + 1) * LANES]
            v_pair = v_ref[b, :, 2 * pair * RET_V_DIM:(2 * pair + 2) * RET_V_DIM]
            r_pair = r_ref[b, pair]
            r_bf = r_pair.astype(BF16)
            for sub in range(2):
                h = 2 * pair + sub
                in_head = (lane >= sub * RET_QK_DIM) & (lane < (sub + 1) * RET_QK_DIM)
                qm = jnp.where(in_head, q_pair, jnp.zeros_like(q_pair))
                v_h = v_pair[:, sub * RET_V_DIM:(sub + 1) * RET_V_DIM]
                inner = lax.dot_general(qm, k_pair, (((1,), (1,)), ((), ())),
                                        preferred_element_type=F32) * decay_ref[h]
                o = (jnp.dot(inner.astype(BF16), v_h, preferred_element_type=F32)
                     + jnp.dot(qm, r_bf, preferred_element_type=F32) * xi_ref[h])
                mu = jnp.mean(o, axis=-1, keepdims=True)
                d = o - mu
                var = jnp.mean(d * d, axis=-1, keepdims=True)
                hn = d * lax.rsqrt(var + LN_EPS)
                gate = g_ref[b, :, h * RET_V_DIM:(h + 1) * RET_V_DIM].astype(F32)
                o_ref[b, :, h * RET_V_DIM:(h + 1) * RET_V_DIM] = (gate * hn).astype(o_ref.dtype)
            kz = (k_pair.astype(F32) * zeta_ref[pair]).astype(BF16)
            upd = lax.dot_general(kz, v_pair, (((0,), (0,)), ((), ())), preferred_element_type=F32)
            r_ref[b, pair] = (r_pair * gch_ref[pair]
                              + jnp.where(row < RET_QK_DIM, upd[:, :RET_V_DIM], upd[:, RET_V_DIM:]))


def _retention(qk, tokb, gates, B, S):
    C = RET_CHUNK
    H = RET_HEADS
    nc = S // C
    gamma = 1.0 - 2.0 ** (-5.0 - jnp.arange(H, dtype=F32))
    log_g = jnp.log(gamma)
    n = jnp.arange(C, dtype=F32)
    diff = n[:, None] - n[None, :]
    decay_in = jnp.where(diff[None] >= 0, jnp.exp(log_g[:, None, None] * jnp.maximum(diff, 0.0)[None]), 0.0)
    xi = jnp.exp(log_g[None, :] * (n[:, None] + 1.0))
    zeta = jnp.exp(log_g[None, :] * (C - 1.0 - n[:, None]))
    g_chunk = jnp.exp(log_g * C)
    xi_b = jnp.broadcast_to(xi.T[:, :, None], (H, C, RET_V_DIM))
    zeta_b = jnp.repeat(zeta, RET_QK_DIM, axis=1).reshape(C, H // 2, LANES).transpose(1, 0, 2)
    gch_b = jnp.broadcast_to(jnp.repeat(g_chunk, RET_QK_DIM).reshape(H // 2, LANES, 1),
                             (H // 2, LANES, RET_V_DIM))
    qk3, tok3, gate3 = (a.reshape(B, S, a.shape[-1]) for a in (qk, tokb, gates))
    out = pl.pallas_call(
        _retention_kernel,
        grid=(nc,),
        in_specs=[pl.BlockSpec((B, C, RET_QK_W), lambda i: (0, i, 0)),
                  pl.BlockSpec((B, C, RET_QK_W), lambda i: (0, i, 1)),
                  pl.BlockSpec((B, C, RET_V_W), lambda i: (0, i, 0)),
                  pl.BlockSpec((B, C, RET_V_W), lambda i: (0, i, 0)),
                  pl.BlockSpec((H, C, C), lambda i: (0, 0, 0)),
                  pl.BlockSpec((H, C, RET_V_DIM), lambda i: (0, 0, 0)),
                  pl.BlockSpec((H // 2, C, LANES), lambda i: (0, 0, 0)),
                  pl.BlockSpec((H // 2, LANES, RET_V_DIM), lambda i: (0, 0, 0))],
        out_specs=pl.BlockSpec((B, C, RET_V_W), lambda i: (0, i, 0)),
        out_shape=jax.ShapeDtypeStruct((B, S, RET_V_W), BF16),
        scratch_shapes=[pltpu.VMEM((B, H // 2, LANES, RET_V_DIM), F32)],
        compiler_params=_cparams(1),
        name="retention",
    )(qk3, qk3, tok3, gate3, decay_in, xi_b, zeta_b, gch_b)
    return out.reshape(B * S, RET_V_W)


def _layer_norm(z, g, b):
    mu = jnp.mean(z, axis=-1, keepdims=True)
    d = z - mu
    var = jnp.mean(d * d, axis=-1, keepdims=True)
    return d * lax.rsqrt(var + LN_EPS) * g + b


def _merge_kernel(x_ref, ya_ref, yr_ref, ga_ref, gr_ref, wa_ref, wr_ref, wo_ref, g_ref, b_ref,
                  x1_ref, x1b_ref):
    a = jnp.dot(ya_ref[...], wa_ref[...], preferred_element_type=F32)
    r = jnp.dot(yr_ref[...], wr_ref[...], preferred_element_type=F32)
    h = ga_ref[...].astype(F32) * a + gr_ref[...].astype(F32) * r
    mix = jnp.dot(h.astype(BF16), wo_ref[...], preferred_element_type=F32)
    x1 = _layer_norm(DEEPNORM_ALPHA * x_ref[...] + mix, g_ref[...], b_ref[...])
    x1_ref[...] = x1
    x1b_ref[...] = x1.astype(BF16)


def _merge(x, ya, yr, gates, wa, wr, wo, g, b, tm=512):
    T, D = x.shape
    tm = min(tm, T)
    row = lambda i: (i, 0)
    fixed = lambda i: (0, 0)
    return pl.pallas_call(
        _merge_kernel,
        grid=(T // tm,),
        in_specs=[pl.BlockSpec((tm, D), row),
                  pl.BlockSpec((tm, ya.shape[1]), row),
                  pl.BlockSpec((tm, yr.shape[1]), row),
                  pl.BlockSpec((tm, D), lambda i: (i, 1)),
                  pl.BlockSpec((tm, D), lambda i: (i, 2)),
                  pl.BlockSpec(wa.shape, fixed),
                  pl.BlockSpec(wr.shape, fixed),
                  pl.BlockSpec(wo.shape, fixed),
                  pl.BlockSpec((1, D), fixed),
                  pl.BlockSpec((1, D), fixed)],
        out_specs=[pl.BlockSpec((tm, D), row), pl.BlockSpec((tm, D), row)],
        out_shape=[jax.ShapeDtypeStruct((T, D), F32), jax.ShapeDtypeStruct((T, D), BF16)],
        compiler_params=_cparams(1),
        name="merge",
    )(x, ya, yr, gates, gates, wa, wr, wo, g, b)


def _ffn_kernel(x1b_ref, x1_ref, wu_ref, wd_ref, g_ref, b_ref, o_ref, acc_ref):
    f = pl.program_id(1)

    @pl.when(f == 0)
    def _():
        acc_ref[...] = jnp.zeros(acc_ref.shape, F32)

    hid = jnp.maximum(jnp.dot(x1b_ref[...], wu_ref[...], preferred_element_type=F32), 0.0)
    acc_ref[...] += jnp.dot((hid * hid).astype(BF16), wd_ref[...], preferred_element_type=F32)

    @pl.when(f == pl.num_programs(1) - 1)
    def _():
        o_ref[...] = _layer_norm(DEEPNORM_ALPHA * x1_ref[...] + acc_ref[...], g_ref[...], b_ref[...])


def _ffn(x1b, x1, wu, wd, g, b, tm=1024, tf=1024):
    T, D = x1.shape
    F = wu.shape[1]
    tm = min(tm, T)
    return pl.pallas_call(
        _ffn_kernel,
        grid=(T // tm, F // tf),
        in_specs=[pl.BlockSpec((tm, D), lambda i, f: (i, 0)),
                  pl.BlockSpec((tm, D), lambda i, f: (i, 0)),
                  pl.BlockSpec((D, tf), lambda i, f: (0, f)),
                  pl.BlockSpec((tf, D), lambda i, f: (f, 0)),
                  pl.BlockSpec((1, D), lambda i, f: (0, 0)),
                  pl.BlockSpec((1, D), lambda i, f: (0, 0))],
        out_specs=pl.BlockSpec((tm, D), lambda i, f: (i, 0)),
        out_shape=jax.ShapeDtypeStruct((T, D), F32),
        scratch_shapes=[pltpu.VMEM((tm, D), F32)],
        compiler_params=_cparams(2),
        name="ffn",
    )(x1b, x1, wu, wd, g, b)


def _rot_half_weight(wT):
    N, D = wT.shape
    half = RET_QK_DIM // 2
    wh = wT.reshape(N // RET_QK_DIM, 2, half, D)
    return jnp.stack([-wh[:, 1], wh[:, 0]], axis=1).reshape(N, D)


def kernel(x, positions, w_in, rel_bias, idx_k_ln_g, idx_k_ln_b, w_attn_branch, w_ret_branch,
           w_out, ln_mix_g, ln_mix_b, w_up, w_down, ln_ffn_g, ln_ffn_b):
    B, S, D = x.shape
    T = B * S
    sizes = (ATTN_W, ATTN_W, ATTN_W, IDX_Q_W, IDX_HEAD_DIM, IDX_HEADS,
             RET_QK_W, RET_QK_W, RET_V_W, RET_V_W, D, D)
    offs = [0] + [int(o) for o in np.cumsum(sizes)]
    cos, sin = _rope_tables(positions)
    xf = x.reshape(T, D)
    for l in range(DEPTH):
        wT_f32 = jnp.swapaxes(w_in[l], 0, 1)
        wT = wT_f32.astype(BF16)
        rows = [wT[offs[k]:offs[k + 1]] for k in range(len(sizes))]
        (w_qa, w_ka, w_va, w_iq, w_ik, w_iw, w_qr, w_kr, w_vr, w_gr, w_ga, w_gtr) = rows
        w_qa = (wT_f32[offs[0]:offs[1]] * (ATTN_HEAD_DIM ** -0.5 * LOG2E)).astype(BF16)
        w_kr = w_kr * (RET_QK_DIM ** -0.5)

        pad = LANES - IDX_HEAD_DIM - IDX_HEADS
        w_idx = jnp.concatenate([w_ik, w_iw, jnp.zeros((pad, D), BF16)], axis=0)
        g_pad = jnp.concatenate([idx_k_ln_g[l], jnp.zeros((LANES - IDX_HEAD_DIM,), F32)]).reshape(1, LANES)
        b_pad = jnp.concatenate([idx_k_ln_b[l], jnp.zeros((LANES - IDX_HEAD_DIM,), F32)]).reshape(1, LANES)
        idx, xb = _proj_idx(xf, w_idx, g_pad, b_pad, (IDX_HEAD_DIM ** -0.5) * (IDX_HEADS ** -0.5))
        idx = idx.reshape(B, S, LANES)
        ik = idx[:, :, :IDX_HEAD_DIM].astype(BF16)
        iwT = jnp.swapaxes(idx[:, :, IDX_HEAD_DIM:IDX_HEAD_DIM + SUBLANES], 1, 2)

        fm = _proj_t(xb, jnp.concatenate([w_qa, w_va, w_iq], axis=0), B, S, BF16, tn=1280)
        tokb = _proj(xb, jnp.concatenate([w_vr, w_ka], axis=0), BF16, tn=1536)
        gates = _proj_gates(xb, jnp.concatenate([w_gr, w_ga, w_gtr], axis=0), tn=D)
        w_rope = jnp.concatenate([w_qr, w_kr], axis=0)
        w_rope_rot = jnp.concatenate([_rot_half_weight(w_qr), _rot_half_weight(w_kr)], axis=0)
        qk_r = _proj_rope(xb, w_rope, w_rope_rot, cos, sin, tn=1024)

        y_a = _sparse_attention(fm, tokb.reshape(B, S, -1), ik, iwT, positions, rel_bias)
        y_r = _retention(qk_r, tokb, gates, B, S)
        x1, x1b = _merge(xf, y_a.reshape(T, ATTN_W), y_r, gates,
                         w_attn_branch[l].astype(BF16), w_ret_branch[l].astype(BF16),
                         w_out[l].astype(BF16), ln_mix_g[l].reshape(1, D), ln_mix_b[l].reshape(1, D))
        xf = _ffn(x1b, x1, w_up[l].astype(BF16), w_down[l].astype(BF16),
                  ln_ffn_g[l].reshape(1, D), ln_ffn_b[l].reshape(1, D))
    return xf.reshape(B, S, D)
```

```python
import functools
import math

import numpy as np
import jax
import jax.numpy as jnp
from jax import lax
from jax.experimental import pallas as pl
from jax.experimental.pallas import tpu as pltpu

F32 = jnp.float32
BF16 = jnp.bfloat16

ATTN_HEADS = 8
ATTN_HEAD_DIM = 64
ATTN_W = ATTN_HEADS * ATTN_HEAD_DIM
IDX_HEADS = 4
IDX_HEAD_DIM = 64
IDX_Q_W = IDX_HEADS * IDX_HEAD_DIM
TOPK_MAX = 256
RET_HEADS = 8
RET_QK_DIM = 64
RET_V_DIM = 128
RET_QK_W = RET_HEADS * RET_QK_DIM
RET_V_W = RET_HEADS * RET_V_DIM
RET_CHUNK = 128
ROPE_BASE = 10000.0
NUM_BUCKETS = 32
MAX_DISTANCE = 128
LN_EPS = 1e-5
DEPTH = 1
DEEPNORM_ALPHA = (2.0 * DEPTH) ** 0.25

LANES = 128
SUBLANES = 8
VMEM_LIMIT = 56 * 1024 * 1024

TQ = 128
CK = 512
NEG = -1e30
LOG2E = math.log2(math.e)
BISECT_ROUNDS = 20
BIAS_TABLE_N = 128
FAR_N = 113


def _cparams(n_grid):
    return pltpu.CompilerParams(
        dimension_semantics=("arbitrary",) * n_grid,
        vmem_limit_bytes=VMEM_LIMIT)


def _trig_kernel(pos_ref, inv_ref, cos_ref, sin_ref):
    ang = pos_ref[...] * inv_ref[...]
    cos_ref[...] = jnp.cos(ang)
    sin_ref[...] = jnp.sin(ang)


def _rope_tables(positions):
    B, S = positions.shape
    half = RET_QK_DIM // 2
    inv = ROPE_BASE ** (-jnp.arange(half, dtype=F32) / half)
    per_row = LANES // half
    rows = B * S // per_row
    pos_e = jnp.repeat(positions.astype(F32).reshape(rows, per_row), half, axis=1)
    inv_e = jnp.tile(inv, per_row).reshape(1, LANES)
    tr = min(rows, 1024)
    cos, sin = pl.pallas_call(
        _trig_kernel,
        grid=(rows // tr,),
        in_specs=[pl.BlockSpec((tr, LANES), lambda i: (i, 0)),
                  pl.BlockSpec((1, LANES), lambda i: (0, 0))],
        out_specs=[pl.BlockSpec((tr, LANES), lambda i: (i, 0))] * 2,
        out_shape=[jax.ShapeDtypeStruct((rows, LANES), F32)] * 2,
        compiler_params=_cparams(1),
        name="rope_tables",
    )(pos_e, inv_e)
    cos = jnp.tile(cos.reshape(B * S, half), (1, per_row))
    sin = jnp.tile(sin.reshape(B * S, half), (1, per_row))
    return cos, sin


def _x_wt(x, wT):
    return lax.dot_general(x, wT, (((1,), (1,)), ((), ())), preferred_element_type=F32)


def _proj_kernel(x_ref, w_ref, o_ref):
    o_ref[...] = _x_wt(x_ref[...], w_ref[...]).astype(o_ref.dtype)


def _proj(xb, wT, out_dtype, tm=1024, tn=512):
    T, D = xb.shape
    N = wT.shape[0]
    tn = min(N, tn)
    tm = min(tm, T)
    return pl.pallas_call(
        _proj_kernel,
        grid=(T // tm, N // tn),
        in_specs=[pl.BlockSpec((tm, D), lambda i, j: (i, 0)),
                  pl.BlockSpec((tn, D), lambda i, j: (j, 0))],
        out_specs=pl.BlockSpec((tm, tn), lambda i, j: (i, j)),
        out_shape=jax.ShapeDtypeStruct((T, N), out_dtype),
        compiler_params=_cparams(2),
        name="proj",
    )(xb, wT)


def _proj_gates_kernel(x_ref, w_ref, o_ref):
    acc = _x_wt(x_ref[...], w_ref[...])
    sig = 0.5 * jnp.tanh(0.5 * acc) + 0.5
    o_ref[...] = jnp.where(pl.program_id(1) == 0, acc * sig, sig).astype(o_ref.dtype)


def _proj_gates(xb, wT, tn, tm=1024):
    T, D = xb.shape
    N = wT.shape[0]
    tm = min(tm, T)
    return pl.pallas_call(
        _proj_gates_kernel,
        grid=(T // tm, N // tn),
        in_specs=[pl.BlockSpec((tm, D), lambda i, j: (i, 0)),
                  pl.BlockSpec((tn, D), lambda i, j: (j, 0))],
        out_specs=pl.BlockSpec((tm, tn), lambda i, j: (i, j)),
        out_shape=jax.ShapeDtypeStruct((T, N), BF16),
        compiler_params=_cparams(2),
        name="proj_gates",
    )(xb, wT)


def _proj_t_kernel(wT_ref, x_ref, o_ref, *, scaled_rows, scale):
    acc = lax.dot_general(wT_ref[...], x_ref[...], (((1,), (1,)), ((), ())), preferred_element_type=F32)
    o_ref[:scaled_rows, :] = (acc[:scaled_rows] * scale).astype(o_ref.dtype)
    o_ref[scaled_rows:, :] = acc[scaled_rows:].astype(o_ref.dtype)


def _proj_t(xb, wT, B, S, out_dtype, scaled_rows, scale, tm=1024):
    T, D = xb.shape
    N = wT.shape[0]
    tn = N
    tm = min(tm, S)
    nsb = S // tm
    return pl.pallas_call(
        functools.partial(_proj_t_kernel, scaled_rows=scaled_rows, scale=scale),
        grid=(T // tm, N // tn),
        in_specs=[pl.BlockSpec((tn, D), lambda i, j: (j, 0)),
                  pl.BlockSpec((tm, D), lambda i, j: (i, 0))],
        out_specs=pl.BlockSpec((pl.Squeezed(), tn, tm), lambda i, j: (i // nsb, j, i % nsb)),
        out_shape=jax.ShapeDtypeStruct((B, N, S), out_dtype),
        compiler_params=_cparams(2),
        name="proj_t",
    )(wT, xb)


def _proj_rope_kernel(x_ref, w_ref, wr_ref, cos_ref, sin_ref, o_ref):
    x = x_ref[...]
    a = _x_wt(x, w_ref[...])
    r = _x_wt(x, wr_ref[...])
    reps = a.shape[1] // LANES
    cos = jnp.concatenate([cos_ref[...]] * reps, axis=1)
    sin = jnp.concatenate([sin_ref[...]] * reps, axis=1)
    o_ref[...] = (a * cos + r * sin).astype(o_ref.dtype)


def _proj_rope(xb, wT, wT_rot, cos, sin, tm=1024, tn=512):
    T, D = xb.shape
    N = wT.shape[0]
    tm = min(tm, T)
    return pl.pallas_call(
        _proj_rope_kernel,
        grid=(T // tm, N // tn),
        in_specs=[pl.BlockSpec((tm, D), lambda i, j: (i, 0)),
                  pl.BlockSpec((tn, D), lambda i, j: (j, 0)),
                  pl.BlockSpec((tn, D), lambda i, j: (j, 0)),
                  pl.BlockSpec((tm, LANES), lambda i, j: (i, 0)),
                  pl.BlockSpec((tm, LANES), lambda i, j: (i, 0))],
        out_specs=pl.BlockSpec((tm, tn), lambda i, j: (i, j)),
        out_shape=jax.ShapeDtypeStruct((T, N), BF16),
        compiler_params=_cparams(2),
        name="proj_rope",
    )(xb, wT, wT_rot, cos, sin)


def _proj_idx_kernel(x_ref, w_ref, g_ref, b_ref, o_ref, xb_ref, *, iw_scale):
    xb = x_ref[...].astype(BF16)
    xb_ref[...] = xb
    acc = _x_wt(xb, w_ref[...])
    lane = lax.broadcasted_iota(jnp.int32, acc.shape, 1)
    is_k = lane < IDX_HEAD_DIM
    mu = jnp.sum(jnp.where(is_k, acc, 0.0), axis=-1, keepdims=True) / IDX_HEAD_DIM
    d = acc - mu
    var = jnp.sum(jnp.where(is_k, d * d, 0.0), axis=-1, keepdims=True) / IDX_HEAD_DIM
    ln = d * lax.rsqrt(var + LN_EPS) * g_ref[...] + b_ref[...]
    o_ref[...] = jnp.where(is_k, ln, acc * iw_scale)


def _proj_idx(x, w_pad, g_pad, b_pad, iw_scale, tm=1024):
    T, D = x.shape
    tm = min(tm, T)
    return pl.pallas_call(
        functools.partial(_proj_idx_kernel, iw_scale=iw_scale),
        grid=(T // tm,),
        in_specs=[pl.BlockSpec((tm, D), lambda i: (i, 0)),
                  pl.BlockSpec((LANES, D), lambda i: (0, 0)),
                  pl.BlockSpec((1, LANES), lambda i: (0, 0)),
                  pl.BlockSpec((1, LANES), lambda i: (0, 0))],
        out_specs=[pl.BlockSpec((tm, LANES), lambda i: (i, 0)), pl.BlockSpec((tm, D), lambda i: (i, 0))],
        out_shape=[jax.ShapeDtypeStruct((T, LANES), F32), jax.ShapeDtypeStruct((T, D), BF16)],
        compiler_params=_cparams(1),
        name="proj_idx",
    )(x, w_pad, g_pad, b_pad)


def _t5_bucket_table():
    n = np.arange(BIAS_TABLE_N)
    max_exact = NUM_BUCKETS // 2
    nf = np.maximum(n, 1).astype(np.float64)
    large = max_exact + (np.log(nf / max_exact) / math.log(MAX_DISTANCE / max_exact)
                         * (NUM_BUCKETS - max_exact)).astype(np.int64)
    large = np.minimum(large, NUM_BUCKETS - 1)
    bucket = np.where(n < max_exact, n, large)
    assert np.all(bucket[FAR_N:] == NUM_BUCKETS - 1) and bucket[FAR_N - 1] != NUM_BUCKETS - 1
    return bucket.astype(np.int32)


def _fold_rows(a, op):
    parts = [a[r:r + SUBLANES] for r in range(0, a.shape[0], SUBLANES)]
    while len(parts) > 1:
        nxt = [op(parts[k], parts[k + 1]) for k in range(0, len(parts) - 1, 2)]
        if len(parts) % 2:
            nxt.append(parts[-1])
        parts = nxt
    return parts[0]


def _attn_kernel(biasc_ref, pinfo_ref, ckmax_ref, iqT_ref, iwT_ref, posqc_ref, posk_ref, ik_ref, qT_ref, k_ref, vT_ref,
                 tbl_ref, toep_ref, tril_ref, o_ref, sc_ref, s_ref, s2_ref, b_ref, qm_ref, m_ref, l_ref, *acc_refs, k_sel, seq):
    i = pl.program_id(1)
    nch = (i * TQ + TQ + CK - 1) // CK
    q_idx = i * TQ + lax.broadcasted_iota(jnp.int32, (1, TQ), 1)
    kf = float(k_sel)

    def chunk_off(c):
        return pl.multiple_of(c * CK, CK)

    def key_idx(off):
        return off + lax.broadcasted_iota(jnp.int32, (CK, TQ), 0)

    def col_reduce(part, op):
        return op(part, axis=0, keepdims=True)

    iqT = iqT_ref[...]
    iwT = iwT_ref[...]
    iq_wide = jnp.concatenate([iqT[h * IDX_HEAD_DIM:(h + 1) * IDX_HEAD_DIM, :] for h in range(IDX_HEADS)], axis=1)

    def score_matmul(c, z_buf):
        ikc = ik_ref[pl.ds(chunk_off(c), CK), :]
        z = jnp.dot(ikc, iq_wide, preferred_element_type=F32)
        for h in range(IDX_HEADS):
            z_buf[h] = z[:, h * TQ:(h + 1) * TQ]

    def score_finish(c, z_buf, carry, masked):
        mn, mx = carry
        off = chunk_off(c)
        s = None
        for h in range(IDX_HEADS):
            t = jnp.maximum(z_buf[h], 0.0) * iwT[h:h + 1, :]
            s = t if s is None else s + t
        if masked:
            causal = key_idx(off) <= q_idx
            s_lo = jnp.where(causal, s, -jnp.inf)
            s_hi = jnp.where(causal, s, jnp.inf)
        else:
            s_lo = s_hi = s
        sc_ref[pl.ds(off, CK), :] = s_lo
        mn = jnp.minimum(mn, _fold_rows(s_hi, jnp.minimum))
        mx = jnp.maximum(mx, _fold_rows(s_lo, jnp.maximum))
        return mn, mx

    def score_chunk(c, carry, masked):
        score_matmul(c, s_ref)
        return score_finish(c, s_ref, carry, masked)

    n_inner = nch - 1
    n_pairs_sc = n_inner // 2

    @pl.when(n_pairs_sc > 0)
    def _():
        score_matmul(0, s_ref)

    def score_step(t, carry):
        c = 2 * t
        score_matmul(c + 1, s2_ref)
        carry = score_finish(c, s_ref, carry, False)
        score_matmul(jnp.minimum(c + 2, 2 * n_pairs_sc - 2), s_ref)
        return score_finish(c + 1, s2_ref, carry, False)

    mn8, mx8 = lax.fori_loop(0, n_pairs_sc, score_step,
                             (jnp.full((SUBLANES, TQ), jnp.inf, F32), jnp.full((SUBLANES, TQ), -jnp.inf, F32)))
    mn8, mx8 = lax.fori_loop(2 * n_pairs_sc, n_inner, functools.partial(score_chunk, masked=False), (mn8, mx8))
    mn8, mx8 = score_chunk(nch - 1, (mn8, mx8), masked=True)
    mn = col_reduce(mn8, jnp.min)
    mx = col_reduce(mx8, jnp.max)

    def over_chunks(fn, init):
        def one(c, carry):
            return fn(sc_ref[pl.ds(chunk_off(c), CK), :], carry)

        carry = lax.fori_loop(0, nch // 2, lambda t, carry: one(2 * t + 1, one(2 * t, carry)), init)
        return lax.fori_loop(2 * (nch // 2), nch, one, carry)

    def count(pred_fn):
        acc = over_chunks(lambda blk, acc: acc + _fold_rows(jnp.where(pred_fn(blk), 1.0, 0.0), jnp.add),
                          jnp.zeros((SUBLANES, TQ), F32))
        return col_reduce(acc, jnp.sum)

    def bisect_round(_, st):
        lo, hi, c_lo = st
        mid = 0.5 * (lo + hi)
        c = count(lambda blk: blk >= mid)
        ok = c >= kf
        return jnp.where(ok, mid, lo), jnp.where(ok, hi, mid), jnp.where(ok, c, c_lo)

    c_all = (q_idx + 1).astype(F32)
    lo, hi, c_lo = lax.fori_loop(0, BISECT_ROUNDS, bisect_round, (mn, mx, c_all))

    cur0 = col_reduce(over_chunks(
        lambda blk, acc: jnp.minimum(acc, _fold_rows(jnp.where(blk >= lo, blk, jnp.inf), jnp.minimum)),
        jnp.full((SUBLANES, TQ), jnp.inf, F32)), jnp.min)

    def walk_cond(st):
        return st[3] > 0.0

    def walk_body(st):
        cur, c_ge, _, _ = st

        def body(blk, carry):
            cnt, nxt = carry
            gt = blk > cur
            cnt = cnt + _fold_rows(jnp.where(gt, 1.0, 0.0), jnp.add)
            nxt = jnp.minimum(nxt, _fold_rows(jnp.where(gt, blk, jnp.inf), jnp.minimum))
            return cnt, nxt

        cnt, nxt = over_chunks(body, (jnp.zeros((SUBLANES, TQ), F32), jnp.full((SUBLANES, TQ), jnp.inf, F32)))
        c_gt = col_reduce(cnt, jnp.sum)
        nxt = col_reduce(nxt, jnp.min)
        adv = c_gt >= kf
        cur = jnp.where(adv, nxt, cur)
        c_ge = jnp.where(adv, c_gt, c_ge)
        return cur, c_ge, c_gt, jnp.max(jnp.where(adv, 1.0, 0.0))

    tau, c_ge, c_gt, _ = lax.while_loop(
        walk_cond, walk_body, (cur0, c_lo, jnp.zeros((1, TQ), F32), jnp.float32(1.0)))

    room = kf - c_gt

    def selection_mask(c, seen):
        off = chunk_off(c)
        tiles = range(CK // LANES)
        blks = [sc_ref[pl.ds(off + j * LANES, LANES), :] for j in tiles]
        ties = [blk == tau for blk in blks]
        local = [jnp.dot(tril_ref[...], jnp.where(tie, 1.0, 0.0).astype(BF16), preferred_element_type=F32)
                 for tie in ties]
        masks = []
        for j in tiles:
            keep = (blks[j] > tau) | (ties[j] & (local[j] + seen <= room))
            masks.append(jnp.where(keep, 0.0, NEG))
            seen = seen + local[j][LANES - 1:LANES, :]
        return jnp.concatenate(masks, axis=0), seen

    m_ref[...] = jnp.full(m_ref.shape, NEG, F32)
    l_ref[...] = jnp.zeros(l_ref.shape, F32)
    for acc in acc_refs:
        acc[...] = jnp.zeros(acc.shape, F32)

    rowi = lax.broadcasted_iota(jnp.int32, (LANES, TQ), 0)
    for pair in range(ATTN_HEADS // 2):
        qp = qT_ref[pair * LANES:(pair + 1) * LANES, :]
        zero = jnp.zeros_like(qp)
        qm_ref[pair] = jnp.concatenate([jnp.where(rowi < ATTN_HEAD_DIM, qp, zero),
                                        jnp.where(rowi >= ATTN_HEAD_DIM, qp, zero)], axis=1)
    ones_rows = jnp.ones((2 * SUBLANES, CK), BF16)

    pq_col = posqc_ref[...]
    batch = pl.program_id(0)
    pq_first, pq_consec, pq_min = (pinfo_ref[batch, r, i] for r in (0, 1, 3))
    far_bias = [biasc_ref[0, h] for h in range(ATTN_HEADS)]
    zero_bias = [biasc_ref[1, h] for h in range(ATTN_HEADS)]
    q_ge_k = (lax.broadcasted_iota(jnp.int32, (LANES, TQ), 1) >= lax.broadcasted_iota(jnp.int32, (LANES, TQ), 0))
    n_sub = CK // LANES
    n_pairs = ATTN_HEADS // 2

    def chunk_is_far(c):
        return (pq_min - ckmax_ref[batch, jnp.minimum(c, seq // CK - 1)]) >= FAR_N

    def stage_bias(c):
        off = chunk_off(c)
        pk_row = posk_ref[:, pl.ds(off, CK)]
        for j in range(n_sub):
            rows = slice(j * LANES, (j + 1) * LANES)
            g = c * n_sub + j
            pk_first, pk_consec, pk_max = (pinfo_ref[batch, r, g] for r in (0, 1, 2))
            all_far = (pq_min - pk_max) >= FAR_N
            all_masked = (off + j * LANES) > (i * TQ + TQ - 1)
            consecutive = (pq_consec > 0) & (pk_consec > 0)
            gap = pq_first - pk_first

            def fill_const(rows=rows):
                for h in range(ATTN_HEADS):
                    b_ref[h, rows, :] = jnp.full((LANES, TQ), far_bias[h], F32)

            def fill_gap0(rows=rows):
                for h in range(ATTN_HEADS):
                    b_ref[h, rows, :] = jnp.where(q_ge_k, toep_ref[h], zero_bias[h])

            def fill_gap128(rows=rows):
                for h in range(ATTN_HEADS):
                    b_ref[h, rows, :] = jnp.where(q_ge_k, far_bias[h], toep_ref[h])

            def fill_lookup(rows=rows):
                pk_sub = pk_row[:, rows]
                n_qk = jnp.clip(pq_col - pk_sub, 0, BIAS_TABLE_N - 1).astype(F32)
                n_kq = n_qk.T.astype(jnp.int32)
                for h in range(ATTN_HEADS):
                    tb = jnp.broadcast_to(tbl_ref[h:h + 1, :], (LANES, BIAS_TABLE_N))
                    b_ref[h, rows, :] = jnp.take_along_axis(tb, n_kq, axis=1)

            def fill_near(fill_gap0=fill_gap0, fill_gap128=fill_gap128, fill_lookup=fill_lookup,
                          consecutive=consecutive, gap=gap):
                lax.cond(consecutive & (gap == 0), fill_gap0,
                         lambda: lax.cond(consecutive & (gap == LANES), fill_gap128, fill_lookup))

            lax.cond(all_far | all_masked, fill_const, fill_near)

    def logits_phase(c, s_buf, const_bias, seen):
        off = chunk_off(c)
        mb, seen = selection_mask(c, seen)
        m_cur = []
        for pair in range(n_pairs):
            kc = k_ref[pl.ds(off, CK), pair * LANES:(pair + 1) * LANES]
            s2 = jnp.dot(kc, qm_ref[pair], preferred_element_type=F32)
            for sub, h in enumerate((2 * pair, 2 * pair + 1)):
                s = s2[:, sub * TQ:(sub + 1) * TQ] + mb
                if not const_bias:
                    s = s + b_ref[h]
                s_buf[h] = s
                top = col_reduce(_fold_rows(s, jnp.maximum), jnp.max)
                m_cur.append(top + far_bias[h] if const_bias else top)
        return m_cur, seen

    def update_phase(c, s_buf, m_cur, const_bias):
        off = chunk_off(c)
        m_prev = m_ref[...]
        l_prev = l_ref[...]
        if isinstance(m_cur, list):
            m_new = [jnp.maximum(m_prev[h:h + 1, :], m_cur[h]) for h in range(ATTN_HEADS)]
            alpha = [jnp.exp2(m_prev[h:h + 1, :] - m_new[h]) for h in range(ATTN_HEADS)]
        else:
            m_all = jnp.maximum(m_prev, m_cur)
            a_all = jnp.exp2(m_prev - m_all)
            m_new = [m_all[h:h + 1, :] for h in range(ATTN_HEADS)]
            alpha = [a_all[h:h + 1, :] for h in range(ATTN_HEADS)]
        l_new = []
        for pair in range(n_pairs):
            heads = (2 * pair, 2 * pair + 1)
            sub_m = [m_new[h] - far_bias[h] if const_bias else m_new[h] for h in heads]
            p2 = jnp.concatenate([jnp.exp2(s_buf[h] - sm).astype(BF16) for h, sm in zip(heads, sub_m)],
                                 axis=1)
            lhs = jnp.concatenate([vT_ref[pair * LANES:(pair + 1) * LANES, pl.ds(off, CK)], ones_rows],
                                  axis=0)
            out = jnp.dot(lhs, p2, preferred_element_type=F32)
            for sub, h in enumerate(heads):
                cols = slice(sub * TQ, (sub + 1) * TQ)
                acc = acc_refs[h]
                acc[...] = alpha[h] * acc[...] + out[sub * ATTN_HEAD_DIM:(sub + 1) * ATTN_HEAD_DIM, cols]
                l_new.append(alpha[h] * l_prev[h:h + 1, :] + out[LANES:LANES + 1, cols])
        l_ref[...] = jnp.concatenate(l_new, axis=0)
        m_ref[...] = jnp.concatenate(m_new, axis=0)

    n_far = lax.while_loop(lambda c: (c < nch) & chunk_is_far(c), lambda c: c + 1, jnp.int32(0))
    n_steps = n_far // 2
    def far_logits(c, s_buf, seen):
        m_cur, seen = logits_phase(c, s_buf, True, seen)
        return jnp.concatenate(m_cur, axis=0), seen

    no_ties = jnp.zeros((1, TQ), F32)
    m_first, seen_first = lax.cond(n_steps > 0, lambda: far_logits(0, s_ref, no_ties),
                                   lambda: (jnp.zeros((ATTN_HEADS, TQ), F32), no_ties))

    def far_step(t, carry):
        m_even, seen_even, _ = carry
        c = 2 * t
        m_odd, seen_odd = far_logits(c + 1, s2_ref, seen_even)
        update_phase(c, s_ref, m_even, True)
        m_even, seen_even = far_logits(jnp.minimum(c + 2, 2 * n_steps - 2), s_ref, seen_odd)
        update_phase(c + 1, s2_ref, m_odd, True)
        return m_even, seen_even, seen_odd

    _, _, seen = lax.fori_loop(0, n_steps, far_step, (m_first, seen_first, no_ties))

    def tail_chunk(c, seen):
        def run(const_bias):
            if not const_bias:
                stage_bias(c)
            m_cur, seen_out = logits_phase(c, s_ref, const_bias, seen)
            update_phase(c, s_ref, m_cur, const_bias)
            return seen_out

        return lax.cond(chunk_is_far(c), lambda: run(True), lambda: run(False))

    lax.fori_loop(2 * n_steps, nch, tail_chunk, seen)

    outT = jnp.concatenate([acc_refs[h][...] / l_ref[h:h + 1, :] for h in range(ATTN_HEADS)], axis=0)
    o_ref[...] = outT.T.astype(o_ref.dtype)


def _toeplitz_kernel(tbl_ref, o_ref):
    q = lax.broadcasted_iota(jnp.int32, (LANES, LANES), 1)
    k = lax.broadcasted_iota(jnp.int32, (LANES, LANES), 0)
    idx = (q - k) & (BIAS_TABLE_N - 1)
    for h in range(ATTN_HEADS):
        tb = jnp.broadcast_to(tbl_ref[h:h + 1, :], (LANES, BIAS_TABLE_N))
        o_ref[h] = jnp.take_along_axis(tb, idx, axis=1)


def _sparse_attention(fm, tokb, ik, iwT, positions, rel_bias):
    B, S, _ = tokb.shape
    k_sel = min(TOPK_MAX, S // 4)
    bucket = _t5_bucket_table()
    tbl = rel_bias[bucket].T.astype(F32) * LOG2E
    bias_c = jnp.stack([tbl[:, BIAS_TABLE_N - 1], tbl[:, 0]])
    toep = pl.pallas_call(
        _toeplitz_kernel,
        out_shape=jax.ShapeDtypeStruct((ATTN_HEADS, LANES, LANES), F32),
        name="bias_toeplitz",
    )(tbl)
    pos_row = positions.reshape(B, 1, S)
    pos_col = positions.reshape(B, S, 1)
    pos_t = positions.reshape(B, S // LANES, LANES)
    consec = jnp.all(pos_t - pos_t[:, :, :1] == jnp.arange(LANES, dtype=positions.dtype), axis=-1)
    pinfo = jnp.stack([pos_t[:, :, 0], consec.astype(jnp.int32), jnp.max(pos_t, axis=-1),
                       jnp.min(pos_t, axis=-1)], axis=1).astype(jnp.int32)
    ck_max = jnp.max(positions.reshape(B, S // CK, CK), axis=-1)
    tril = jnp.tril(jnp.ones((LANES, LANES), BF16))
    sq = pl.Squeezed()
    iq_blk = (2 * ATTN_W) // IDX_Q_W
    k_blk = (tokb.shape[2] - ATTN_W) // ATTN_W
    return pl.pallas_call(
        functools.partial(_attn_kernel, k_sel=k_sel, seq=S),
        grid=(B, S // TQ),
        in_specs=[pl.BlockSpec(memory_space=pltpu.SMEM),
                  pl.BlockSpec(memory_space=pltpu.SMEM),
                  pl.BlockSpec(memory_space=pltpu.SMEM),
                  pl.BlockSpec((sq, IDX_Q_W, TQ), lambda b, i: (b, iq_blk, i)),
                  pl.BlockSpec((sq, SUBLANES, TQ), lambda b, i: (b, 0, i)),
                  pl.BlockSpec((sq, TQ, 1), lambda b, i: (b, i, 0)),
                  pl.BlockSpec((sq, 1, S), lambda b, i: (b, 0, 0)),
                  pl.BlockSpec((sq, S, IDX_HEAD_DIM), lambda b, i: (b, 0, 0)),
                  pl.BlockSpec((sq, ATTN_W, TQ), lambda b, i: (b, 0, i)),
                  pl.BlockSpec((sq, S, ATTN_W), lambda b, i: (b, 0, k_blk)),
                  pl.BlockSpec((sq, ATTN_W, S), lambda b, i: (b, 1, 0)),
                  pl.BlockSpec((ATTN_HEADS, BIAS_TABLE_N), lambda b, i: (0, 0)),
                  pl.BlockSpec((ATTN_HEADS, LANES, LANES), lambda b, i: (0, 0, 0)),
                  pl.BlockSpec((LANES, LANES), lambda b, i: (0, 0))],
        out_specs=pl.BlockSpec((sq, TQ, ATTN_W), lambda b, i: (b, i, 0)),
        out_shape=jax.ShapeDtypeStruct((B, S, ATTN_W), BF16),
        scratch_shapes=[pltpu.VMEM((S, TQ), F32),
                        pltpu.VMEM((ATTN_HEADS, CK, TQ), F32),
                        pltpu.VMEM((ATTN_HEADS, CK, TQ), F32),
                        pltpu.VMEM((ATTN_HEADS, CK, TQ), F32),
                        pltpu.VMEM((ATTN_HEADS // 2, LANES, 2 * TQ), BF16),
                        pltpu.VMEM((ATTN_HEADS, TQ), F32),
                        pltpu.VMEM((ATTN_HEADS, TQ), F32)]
                       + [pltpu.VMEM((ATTN_HEAD_DIM, TQ), F32)] * ATTN_HEADS,
        compiler_params=_cparams(2),
        name="sparse_attention",
    )(bias_c, pinfo, ck_max, fm, iwT, pos_col, pos_row, ik, fm, tokb, fm, tbl, toep, tril)


def _retention_kernel(q_ref, k_ref, v_ref, g_ref, decay_ref, xi_ref, zeta_ref, gch_ref, o_ref, r_ref):
    @pl.when(pl.program_id(0) == 0)
    def _():
        r_ref[...] = jnp.zeros(r_ref.shape, F32)

    C = RET_CHUNK
    lane = lax.broadcasted_iota(jnp.int32, (C, LANES), 1)
    row = lax.broadcasted_iota(jnp.int32, (LANES, RET_V_DIM), 0)
    for pair in range(RET_HEADS // 2):
        for b in range(q_ref.shape[0]):
            q_pair = q_ref[b, :, pair * LANES:(pair + 1) * LANES]
            k_pair = k_ref[b, :, pair * LANES:(pair + 1) * LANES]
            v_pair = v_ref[b, :, 2 * pair * RET_V_DIM:(2 * pair + 2) * RET_V_DIM]
            r_pair = r_ref[b, pair]
            r_bf = r_pair.astype(BF16)
            for sub in range(2):
                h = 2 * pair + sub
                in_head = (lane >= sub * RET_QK_DIM) & (lane < (sub + 1) * RET_QK_DIM)
                qm = jnp.where(in_head, q_pair, jnp.zeros_like(q_pair))
                v_h = v_pair[:, sub * RET_V_DIM:(sub + 1) * RET_V_DIM]
                inner = lax.dot_general(qm, k_pair, (((1,), (1,)), ((), ())),
                                        preferred_element_type=F32) * decay_ref[h]
                o = (jnp.dot(inner.astype(BF16), v_h, preferred_element_type=F32)
                     + jnp.dot(qm, r_bf, preferred_element_type=F32) * xi_ref[h])
                mu = jnp.mean(o, axis=-1, keepdims=True)
                d = o - mu
                var = jnp.mean(d * d, axis=-1, keepdims=True)
                hn = d * lax.rsqrt(var + LN_EPS)
                gate = g_ref[b, :, h * RET_V_DIM:(h + 1) * RET_V_DIM].astype(F32)
                o_ref[b, :, h * RET_V_DIM:(h + 1) * RET_V_DIM] = (gate * hn).astype(o_ref.dtype)
            kz = (k_pair.astype(F32) * zeta_ref[pair]).astype(BF16)
            upd = lax.dot_general(kz, v_pair, (((0,), (0,)), ((), ())), preferred_element_type=F32)
            r_ref[b, pair] = (r_pair * gch_ref[pair]
                              + jnp.where(row < RET_QK_DIM, upd[:, :RET_V_DIM], upd[:, RET_V_DIM:]))


def _retention(qk, tokb, gates, B, S):
    C = RET_CHUNK
    H = RET_HEADS
    nc = S // C
    gamma = 1.0 - 2.0 ** (-5.0 - jnp.arange(H, dtype=F32))
    log_g = jnp.log(gamma)
    n = jnp.arange(C, dtype=F32)
    diff = n[:, None] - n[None, :]
    decay_in = jnp.where(diff[None] >= 0, jnp.exp(log_g[:, None, None] * jnp.maximum(diff, 0.0)[None]), 0.0)
    xi = jnp.exp(log_g[None, :] * (n[:, None] + 1.0))
    zeta = jnp.exp(log_g[None, :] * (C - 1.0 - n[:, None]))
    g_chunk = jnp.exp(log_g * C)
    xi_b = jnp.broadcast_to(xi.T[:, :, None], (H, C, RET_V_DIM))
    zeta_b = jnp.repeat(zeta, RET_QK_DIM, axis=1).reshape(C, H // 2, LANES).transpose(1, 0, 2)
    gch_b = jnp.broadcast_to(jnp.repeat(g_chunk, RET_QK_DIM).reshape(H // 2, LANES, 1),
                             (H // 2, LANES, RET_V_DIM))
    qk3, tok3, gate3 = (a.reshape(B, S, a.shape[-1]) for a in (qk, tokb, gates))
    out = pl.pallas_call(
        _retention_kernel,
        grid=(nc,),
        in_specs=[pl.BlockSpec((B, C, RET_QK_W), lambda i: (0, i, 0)),
                  pl.BlockSpec((B, C, RET_QK_W), lambda i: (0, i, 1)),
                  pl.BlockSpec((B, C, RET_V_W), lambda i: (0, i, 0)),
                  pl.BlockSpec((B, C, RET_V_W), lambda i: (0, i, 0)),
                  pl.BlockSpec((H, C, C), lambda i: (0, 0, 0)),
                  pl.BlockSpec((H, C, RET_V_DIM), lambda i: (0, 0, 0)),
                  pl.BlockSpec((H // 2, C, LANES), lambda i: (0, 0, 0)),
                  pl.BlockSpec((H // 2, LANES, RET_V_DIM), lambda i: (0, 0, 0))],
        out_specs=pl.BlockSpec((B, C, RET_V_W), lambda i: (0, i, 0)),
        out_shape=jax.ShapeDtypeStruct((B, S, RET_V_W), BF16),
        scratch_shapes=[pltpu.VMEM((B, H // 2, LANES, RET_V_DIM), F32)],
        compiler_params=_cparams(1),
        name="retention",
    )(qk3, qk3, tok3, gate3, decay_in, xi_b, zeta_b, gch_b)
    return out.reshape(B * S, RET_V_W)


def _layer_norm(z, g, b):
    mu = jnp.mean(z, axis=-1, keepdims=True)
    d = z - mu
    var = jnp.mean(d * d, axis=-1, keepdims=True)
    return d * lax.rsqrt(var + LN_EPS) * g + b


def _merge_kernel(x_ref, ya_ref, yr_ref, ga_ref, gr_ref, wa_ref, wr_ref, wo_ref, g_ref, b_ref,
                  x1_ref, x1b_ref):
    a = jnp.dot(ya_ref[...], wa_ref[...], preferred_element_type=F32)
    r = jnp.dot(yr_ref[...], wr_ref[...], preferred_element_type=F32)
    h = ga_ref[...].astype(F32) * a + gr_ref[...].astype(F32) * r
    mix = jnp.dot(h.astype(BF16), wo_ref[...], preferred_element_type=F32)
    x1 = _layer_norm(DEEPNORM_ALPHA * x_ref[...] + mix, g_ref[...], b_ref[...])
    x1_ref[...] = x1
    x1b_ref[...] = x1.astype(BF16)


def _merge(x, ya, yr, gates, wa, wr, wo, g, b, tm=512):
    T, D = x.shape
    tm = min(tm, T)
    row = lambda i: (i, 0)
    fixed = lambda i: (0, 0)
    return pl.pallas_call(
        _merge_kernel,
        grid=(T // tm,),
        in_specs=[pl.BlockSpec((tm, D), row),
                  pl.BlockSpec((tm, ya.shape[1]), row),
                  pl.BlockSpec((tm, yr.shape[1]), row),
                  pl.BlockSpec((tm, D), lambda i: (i, 1)),
                  pl.BlockSpec((tm, D), lambda i: (i, 2)),
                  pl.BlockSpec(wa.shape, fixed),
                  pl.BlockSpec(wr.shape, fixed),
                  pl.BlockSpec(wo.shape, fixed),
                  pl.BlockSpec((1, D), fixed),
                  pl.BlockSpec((1, D), fixed)],
        out_specs=[pl.BlockSpec((tm, D), row), pl.BlockSpec((tm, D), row)],
        out_shape=[jax.ShapeDtypeStruct((T, D), F32), jax.ShapeDtypeStruct((T, D), BF16)],
        compiler_params=_cparams(1),
        name="merge",
    )(x, ya, yr, gates, gates, wa, wr, wo, g, b)


def _ffn_kernel(x1b_ref, x1_ref, wu_ref, wd_ref, g_ref, b_ref, o_ref, acc_ref):
    f = pl.program_id(1)

    @pl.when(f == 0)
    def _():
        acc_ref[...] = jnp.zeros(acc_ref.shape, F32)

    hid = jnp.maximum(jnp.dot(x1b_ref[...], wu_ref[...], preferred_element_type=F32), 0.0)
    acc_ref[...] += jnp.dot((hid * hid).astype(BF16), wd_ref[...], preferred_element_type=F32)

    @pl.when(f == pl.num_programs(1) - 1)
    def _():
        o_ref[...] = _layer_norm(DEEPNORM_ALPHA * x1_ref[...] + acc_ref[...], g_ref[...], b_ref[...])


def _ffn(x1b, x1, wu, wd, g, b, tm=1024, tf=1024):
    T, D = x1.shape
    F = wu.shape[1]
    tm = min(tm, T)
    return pl.pallas_call(
        _ffn_kernel,
        grid=(T // tm, F // tf),
        in_specs=[pl.BlockSpec((tm, D), lambda i, f: (i, 0)),
                  pl.BlockSpec((tm, D), lambda i, f: (i, 0)),
                  pl.BlockSpec((D, tf), lambda i, f: (0, f)),
                  pl.BlockSpec((tf, D), lambda i, f: (f, 0)),
                  pl.BlockSpec((1, D), lambda i, f: (0, 0)),
                  pl.BlockSpec((1, D), lambda i, f: (0, 0))],
        out_specs=pl.BlockSpec((tm, D), lambda i, f: (i, 0)),
        out_shape=jax.ShapeDtypeStruct((T, D), F32),
        scratch_shapes=[pltpu.VMEM((tm, D), F32)],
        compiler_params=_cparams(2),
        name="ffn",
    )(x1b, x1, wu, wd, g, b)


def _rot_half_weight(wT):
    N, D = wT.shape
    half = RET_QK_DIM // 2
    wh = wT.reshape(N // RET_QK_DIM, 2, half, D)
    return jnp.stack([-wh[:, 1], wh[:, 0]], axis=1).reshape(N, D)


def kernel(x, positions, w_in, rel_bias, idx_k_ln_g, idx_k_ln_b, w_attn_branch, w_ret_branch,
           w_out, ln_mix_g, ln_mix_b, w_up, w_down, ln_ffn_g, ln_ffn_b):
    B, S, D = x.shape
    T = B * S
    sizes = (ATTN_W, ATTN_W, ATTN_W, IDX_Q_W, IDX_HEAD_DIM, IDX_HEADS,
             RET_QK_W, RET_QK_W, RET_V_W, RET_V_W, D, D)
    offs = [0] + [int(o) for o in np.cumsum(sizes)]
    cos, sin = _rope_tables(positions)
    xf = x.reshape(T, D)
    for l in range(DEPTH):
        wT = jnp.swapaxes(w_in[l], 0, 1).astype(BF16)
        rows = [wT[offs[k]:offs[k + 1]] for k in range(len(sizes))]
        (w_qa, w_ka, w_va, w_iq, w_ik, w_iw, w_qr, w_kr, w_vr, w_gr, w_ga, w_gtr) = rows
        w_kr = w_kr * (RET_QK_DIM ** -0.5)

        pad = LANES - IDX_HEAD_DIM - IDX_HEADS
        w_idx = jnp.concatenate([w_ik, w_iw, jnp.zeros((pad, D), BF16)], axis=0)
        g_pad = jnp.concatenate([idx_k_ln_g[l], jnp.zeros((LANES - IDX_HEAD_DIM,), F32)]).reshape(1, LANES)
        b_pad = jnp.concatenate([idx_k_ln_b[l], jnp.zeros((LANES - IDX_HEAD_DIM,), F32)]).reshape(1, LANES)
        idx, xb = _proj_idx(xf, w_idx, g_pad, b_pad, (IDX_HEAD_DIM ** -0.5) * (IDX_HEADS ** -0.5))
        idx = idx.reshape(B, S, LANES)
        ik = idx[:, :, :IDX_HEAD_DIM].astype(BF16)
        iwT = jnp.swapaxes(idx[:, :, IDX_HEAD_DIM:IDX_HEAD_DIM + SUBLANES], 1, 2)

        fm = _proj_t(xb, jnp.concatenate([w_qa, w_va, w_iq], axis=0), B, S, BF16,
                     scaled_rows=ATTN_W, scale=ATTN_HEAD_DIM ** -0.5 * LOG2E)
        tokb = _proj(xb, jnp.concatenate([w_vr, w_ka], axis=0), BF16, tn=1536)
        gates = _proj_gates(xb, jnp.concatenate([w_gr, w_ga, w_gtr], axis=0), tn=D, tm=2048)
        w_rope = jnp.concatenate([w_qr, w_kr], axis=0)
        w_rope_rot = jnp.concatenate([_rot_half_weight(w_qr), _rot_half_weight(w_kr)], axis=0)
        qk_r = _proj_rope(xb, w_rope, w_rope_rot, cos, sin, tn=1024)

        y_a = _sparse_attention(fm, tokb.reshape(B, S, -1), ik, iwT, positions, rel_bias)
        y_r = _retention(qk_r, tokb, gates, B, S)
        x1, x1b = _merge(xf, y_a.reshape(T, ATTN_W), y_r, gates,
                         w_attn_branch[l].astype(BF16), w_ret_branch[l].astype(BF16),
                         w_out[l].astype(BF16), ln_mix_g[l].reshape(1, D), ln_mix_b[l].reshape(1, D))
        xf = _ffn(x1b, x1, w_up[l].astype(BF16), w_down[l].astype(BF16),
                  ln_ffn_g[l].reshape(1, D), ln_ffn_b[l].reshape(1, D))
    return xf.reshape(B, S, D)
```

```python
import functools
import math

import numpy as np
import jax
import jax.numpy as jnp
from jax import lax
from jax.experimental import pallas as pl
from jax.experimental.pallas import tpu as pltpu

F32 = jnp.float32
BF16 = jnp.bfloat16

ATTN_HEADS = 8
ATTN_HEAD_DIM = 64
ATTN_W = ATTN_HEADS * ATTN_HEAD_DIM
IDX_HEADS = 4
IDX_HEAD_DIM = 64
IDX_Q_W = IDX_HEADS * IDX_HEAD_DIM
TOPK_MAX = 256
RET_HEADS = 8
RET_QK_DIM = 64
RET_V_DIM = 128
RET_QK_W = RET_HEADS * RET_QK_DIM
RET_V_W = RET_HEADS * RET_V_DIM
RET_CHUNK = 128
ROPE_BASE = 10000.0
NUM_BUCKETS = 32
MAX_DISTANCE = 128
LN_EPS = 1e-5
DEPTH = 1
DEEPNORM_ALPHA = (2.0 * DEPTH) ** 0.25

LANES = 128
SUBLANES = 8
VMEM_LIMIT = 56 * 1024 * 1024

PROJ_TM = 1024
GATES_TM = 2048
MERGE_TM = 512
FFN_TM = 1024
FFN_TF = 1024

TQ = 128
CK = 512
NEG = -1e30
LOG2E = math.log2(math.e)
BISECT_ROUNDS = 20
BIAS_TABLE_N = 128
FAR_N = 113


def _cparams(n_grid):
    return pltpu.CompilerParams(
        dimension_semantics=("arbitrary",) * n_grid,
        vmem_limit_bytes=VMEM_LIMIT)


def _trig_kernel(pos_ref, inv_ref, cos_ref, sin_ref):
    ang = pos_ref[...] * inv_ref[...]
    cos_ref[...] = jnp.cos(ang)
    sin_ref[...] = jnp.sin(ang)


def _rope_tables(positions):
    B, S = positions.shape
    half = RET_QK_DIM // 2
    inv = ROPE_BASE ** (-jnp.arange(half, dtype=F32) / half)
    per_row = LANES // half
    rows = B * S // per_row
    pos_e = jnp.repeat(positions.astype(F32).reshape(rows, per_row), half, axis=1)
    inv_e = jnp.tile(inv, per_row).reshape(1, LANES)
    tr = min(rows, 1024)
    cos, sin = pl.pallas_call(
        _trig_kernel,
        grid=(rows // tr,),
        in_specs=[pl.BlockSpec((tr, LANES), lambda i: (i, 0)),
                  pl.BlockSpec((1, LANES), lambda i: (0, 0))],
        out_specs=[pl.BlockSpec((tr, LANES), lambda i: (i, 0))] * 2,
        out_shape=[jax.ShapeDtypeStruct((rows, LANES), F32)] * 2,
        compiler_params=_cparams(1),
        name="rope_tables",
    )(pos_e, inv_e)
    cos = jnp.tile(cos.reshape(B * S, half), (1, per_row))
    sin = jnp.tile(sin.reshape(B * S, half), (1, per_row))
    return cos, sin


def _x_wt(x, wT):
    return lax.dot_general(x, wT, (((1,), (1,)), ((), ())), preferred_element_type=F32)


def _proj_kernel(x_ref, w_ref, o_ref):
    o_ref[...] = _x_wt(x_ref[...], w_ref[...]).astype(o_ref.dtype)


def _proj(xb, wT, out_dtype, tm=PROJ_TM):
    T, D = xb.shape
    N = wT.shape[0]
    tn = N
    tm = min(tm, T)
    return pl.pallas_call(
        _proj_kernel,
        grid=(T // tm, N // tn),
        in_specs=[pl.BlockSpec((tm, D), lambda i, j: (i, 0)),
                  pl.BlockSpec((tn, D), lambda i, j: (j, 0))],
        out_specs=pl.BlockSpec((tm, tn), lambda i, j: (i, j)),
        out_shape=jax.ShapeDtypeStruct((T, N), out_dtype),
        compiler_params=_cparams(2),
        name="proj",
    )(xb, wT)


def _proj_gates_kernel(x_ref, w_ref, o_ref):
    acc = _x_wt(x_ref[...], w_ref[...])
    sig = 0.5 * jnp.tanh(0.5 * acc) + 0.5
    o_ref[...] = jnp.where(pl.program_id(1) == 0, acc * sig, sig).astype(o_ref.dtype)


def _proj_gates(xb, wT, tn, tm=GATES_TM):
    T, D = xb.shape
    N = wT.shape[0]
    tm = min(tm, T)
    return pl.pallas_call(
        _proj_gates_kernel,
        grid=(T // tm, N // tn),
        in_specs=[pl.BlockSpec((tm, D), lambda i, j: (i, 0)),
                  pl.BlockSpec((tn, D), lambda i, j: (j, 0))],
        out_specs=pl.BlockSpec((tm, tn), lambda i, j: (i, j)),
        out_shape=jax.ShapeDtypeStruct((T, N), BF16),
        compiler_params=_cparams(2),
        name="proj_gates",
    )(xb, wT)


def _proj_t_kernel(wT_ref, x_ref, o_ref, *, scaled_rows, scale):
    acc = lax.dot_general(wT_ref[...], x_ref[...], (((1,), (1,)), ((), ())), preferred_element_type=F32)
    o_ref[:scaled_rows, :] = (acc[:scaled_rows] * scale).astype(o_ref.dtype)
    o_ref[scaled_rows:, :] = acc[scaled_rows:].astype(o_ref.dtype)


def _proj_t(xb, wT, B, S, out_dtype, scaled_rows, scale, tm=PROJ_TM):
    T, D = xb.shape
    N = wT.shape[0]
    tn = N
    tm = min(tm, S)
    nsb = S // tm
    return pl.pallas_call(
        functools.partial(_proj_t_kernel, scaled_rows=scaled_rows, scale=scale),
        grid=(T // tm, N // tn),
        in_specs=[pl.BlockSpec((tn, D), lambda i, j: (j, 0)),
                  pl.BlockSpec((tm, D), lambda i, j: (i, 0))],
        out_specs=pl.BlockSpec((pl.Squeezed(), tn, tm), lambda i, j: (i // nsb, j, i % nsb)),
        out_shape=jax.ShapeDtypeStruct((B, N, S), out_dtype),
        compiler_params=_cparams(2),
        name="proj_t",
    )(wT, xb)


def _proj_rope_kernel(x_ref, w_ref, wr_ref, cos_ref, sin_ref, o_ref):
    x = x_ref[...]
    a = _x_wt(x, w_ref[...])
    r = _x_wt(x, wr_ref[...])
    reps = a.shape[1] // LANES
    cos = jnp.concatenate([cos_ref[...]] * reps, axis=1)
    sin = jnp.concatenate([sin_ref[...]] * reps, axis=1)
    o_ref[...] = (a * cos + r * sin).astype(o_ref.dtype)


def _proj_rope(xb, wT, wT_rot, cos, sin, tm=PROJ_TM):
    T, D = xb.shape
    N = wT.shape[0]
    tn = N
    tm = min(tm, T)
    return pl.pallas_call(
        _proj_rope_kernel,
        grid=(T // tm, N // tn),
        in_specs=[pl.BlockSpec((tm, D), lambda i, j: (i, 0)),
                  pl.BlockSpec((tn, D), lambda i, j: (j, 0)),
                  pl.BlockSpec((tn, D), lambda i, j: (j, 0)),
                  pl.BlockSpec((tm, LANES), lambda i, j: (i, 0)),
                  pl.BlockSpec((tm, LANES), lambda i, j: (i, 0))],
        out_specs=pl.BlockSpec((tm, tn), lambda i, j: (i, j)),
        out_shape=jax.ShapeDtypeStruct((T, N), BF16),
        compiler_params=_cparams(2),
        name="proj_rope",
    )(xb, wT, wT_rot, cos, sin)


def _proj_idx_kernel(x_ref, w_ref, g_ref, b_ref, o_ref, xb_ref, *, iw_scale):
    xb = x_ref[...].astype(BF16)
    xb_ref[...] = xb
    acc = _x_wt(xb, w_ref[...])
    lane = lax.broadcasted_iota(jnp.int32, acc.shape, 1)
    is_k = lane < IDX_HEAD_DIM
    mu = jnp.sum(jnp.where(is_k, acc, 0.0), axis=-1, keepdims=True) / IDX_HEAD_DIM
    d = acc - mu
    var = jnp.sum(jnp.where(is_k, d * d, 0.0), axis=-1, keepdims=True) / IDX_HEAD_DIM
    ln = d * lax.rsqrt(var + LN_EPS) * g_ref[...] + b_ref[...]
    o_ref[...] = jnp.where(is_k, ln, acc * iw_scale)


def _proj_idx(x, w_pad, g_pad, b_pad, iw_scale, tm=PROJ_TM):
    T, D = x.shape
    tm = min(tm, T)
    return pl.pallas_call(
        functools.partial(_proj_idx_kernel, iw_scale=iw_scale),
        grid=(T // tm,),
        in_specs=[pl.BlockSpec((tm, D), lambda i: (i, 0)),
                  pl.BlockSpec((LANES, D), lambda i: (0, 0)),
                  pl.BlockSpec((1, LANES), lambda i: (0, 0)),
                  pl.BlockSpec((1, LANES), lambda i: (0, 0))],
        out_specs=[pl.BlockSpec((tm, LANES), lambda i: (i, 0)), pl.BlockSpec((tm, D), lambda i: (i, 0))],
        out_shape=[jax.ShapeDtypeStruct((T, LANES), F32), jax.ShapeDtypeStruct((T, D), BF16)],
        compiler_params=_cparams(1),
        name="proj_idx",
    )(x, w_pad, g_pad, b_pad)


def _t5_bucket_table():
    n = np.arange(BIAS_TABLE_N)
    max_exact = NUM_BUCKETS // 2
    nf = np.maximum(n, 1).astype(np.float64)
    large = max_exact + (np.log(nf / max_exact) / math.log(MAX_DISTANCE / max_exact)
                         * (NUM_BUCKETS - max_exact)).astype(np.int64)
    large = np.minimum(large, NUM_BUCKETS - 1)
    bucket = np.where(n < max_exact, n, large)
    assert np.all(bucket[FAR_N:] == NUM_BUCKETS - 1) and bucket[FAR_N - 1] != NUM_BUCKETS - 1
    return bucket.astype(np.int32)


def _fold_rows(a, op):
    parts = [a[r:r + SUBLANES] for r in range(0, a.shape[0], SUBLANES)]
    while len(parts) > 1:
        nxt = [op(parts[k], parts[k + 1]) for k in range(0, len(parts) - 1, 2)]
        if len(parts) % 2:
            nxt.append(parts[-1])
        parts = nxt
    return parts[0]


def _attn_kernel(biasc_ref, pinfo_ref, ckmax_ref, iqT_ref, iwT_ref, posqc_ref, posk_ref, ik_ref, qT_ref, k_ref, vT_ref,
                 tbl_ref, toep_ref, tril_ref, o_ref, sc_ref, s_ref, s2_ref, b_ref, qm_ref, m_ref, l_ref, *acc_refs, k_sel, seq):
    i = pl.program_id(1)
    nch = (i * TQ + TQ + CK - 1) // CK
    q_idx = i * TQ + lax.broadcasted_iota(jnp.int32, (1, TQ), 1)
    kf = float(k_sel)

    def chunk_off(c):
        return pl.multiple_of(c * CK, CK)

    def key_idx(off):
        return off + lax.broadcasted_iota(jnp.int32, (CK, TQ), 0)

    def col_reduce(part, op):
        return op(part, axis=0, keepdims=True)

    iqT = iqT_ref[...]
    iwT = iwT_ref[...]
    iq_wide = jnp.concatenate([iqT[h * IDX_HEAD_DIM:(h + 1) * IDX_HEAD_DIM, :] for h in range(IDX_HEADS)], axis=1)

    def score_matmul(c, z_buf):
        ikc = ik_ref[pl.ds(chunk_off(c), CK), :]
        z = jnp.dot(ikc, iq_wide, preferred_element_type=F32)
        for h in range(IDX_HEADS):
            z_buf[h] = z[:, h * TQ:(h + 1) * TQ]

    def score_finish(c, z_buf, carry, masked):
        mn, mx = carry
        off = chunk_off(c)
        s = None
        for h in range(IDX_HEADS):
            t = jnp.maximum(z_buf[h], 0.0) * iwT[h:h + 1, :]
            s = t if s is None else s + t
        if masked:
            causal = key_idx(off) <= q_idx
            s_lo = jnp.where(causal, s, -jnp.inf)
            s_hi = jnp.where(causal, s, jnp.inf)
        else:
            s_lo = s_hi = s
        sc_ref[pl.ds(off, CK), :] = s_lo
        mn = jnp.minimum(mn, _fold_rows(s_hi, jnp.minimum))
        mx = jnp.maximum(mx, _fold_rows(s_lo, jnp.maximum))
        return mn, mx

    def score_chunk(c, carry, masked):
        score_matmul(c, s_ref)
        return score_finish(c, s_ref, carry, masked)

    n_inner = nch - 1
    n_pairs_sc = n_inner // 2

    @pl.when(n_pairs_sc > 0)
    def _():
        score_matmul(0, s_ref)

    def score_step(t, carry):
        c = 2 * t
        score_matmul(c + 1, s2_ref)
        carry = score_finish(c, s_ref, carry, False)
        score_matmul(jnp.minimum(c + 2, 2 * n_pairs_sc - 2), s_ref)
        return score_finish(c + 1, s2_ref, carry, False)

    mn8, mx8 = lax.fori_loop(0, n_pairs_sc, score_step,
                             (jnp.full((SUBLANES, TQ), jnp.inf, F32), jnp.full((SUBLANES, TQ), -jnp.inf, F32)))
    mn8, mx8 = lax.fori_loop(2 * n_pairs_sc, n_inner, functools.partial(score_chunk, masked=False), (mn8, mx8))
    mn8, mx8 = score_chunk(nch - 1, (mn8, mx8), masked=True)
    mn = col_reduce(mn8, jnp.min)
    mx = col_reduce(mx8, jnp.max)

    def over_chunks(fn, init):
        def one(c, carry):
            return fn(sc_ref[pl.ds(chunk_off(c), CK), :], carry)

        carry = lax.fori_loop(0, nch // 2, lambda t, carry: one(2 * t + 1, one(2 * t, carry)), init)
        return lax.fori_loop(2 * (nch // 2), nch, one, carry)

    def count(pred_fn):
        acc = over_chunks(lambda blk, acc: acc + _fold_rows(jnp.where(pred_fn(blk), 1.0, 0.0), jnp.add),
                          jnp.zeros((SUBLANES, TQ), F32))
        return col_reduce(acc, jnp.sum)

    def bisect_round(_, st):
        lo, hi, c_lo = st
        mid = 0.5 * (lo + hi)
        c = count(lambda blk: blk >= mid)
        ok = c >= kf
        return jnp.where(ok, mid, lo), jnp.where(ok, hi, mid), jnp.where(ok, c, c_lo)

    c_all = (q_idx + 1).astype(F32)
    lo, hi, c_lo = lax.fori_loop(0, BISECT_ROUNDS, bisect_round, (mn, mx, c_all))

    cur0 = col_reduce(over_chunks(
        lambda blk, acc: jnp.minimum(acc, _fold_rows(jnp.where(blk >= lo, blk, jnp.inf), jnp.minimum)),
        jnp.full((SUBLANES, TQ), jnp.inf, F32)), jnp.min)

    def walk_cond(st):
        return st[3] > 0.0

    def walk_body(st):
        cur, c_ge, _, _ = st

        def body(blk, carry):
            cnt, nxt = carry
            gt = blk > cur
            cnt = cnt + _fold_rows(jnp.where(gt, 1.0, 0.0), jnp.add)
            nxt = jnp.minimum(nxt, _fold_rows(jnp.where(gt, blk, jnp.inf), jnp.minimum))
            return cnt, nxt

        cnt, nxt = over_chunks(body, (jnp.zeros((SUBLANES, TQ), F32), jnp.full((SUBLANES, TQ), jnp.inf, F32)))
        c_gt = col_reduce(cnt, jnp.sum)
        nxt = col_reduce(nxt, jnp.min)
        adv = c_gt >= kf
        cur = jnp.where(adv, nxt, cur)
        c_ge = jnp.where(adv, c_gt, c_ge)
        return cur, c_ge, c_gt, jnp.max(jnp.where(adv, 1.0, 0.0))

    tau, c_ge, c_gt, _ = lax.while_loop(
        walk_cond, walk_body, (cur0, c_lo, jnp.zeros((1, TQ), F32), jnp.float32(1.0)))

    room = kf - c_gt

    def selection_mask(c, seen):
        off = chunk_off(c)
        tiles = range(CK // LANES)
        blks = [sc_ref[pl.ds(off + j * LANES, LANES), :] for j in tiles]
        ties = [blk == tau for blk in blks]
        local = [jnp.dot(tril_ref[...], jnp.where(tie, 1.0, 0.0).astype(BF16), preferred_element_type=F32)
                 for tie in ties]
        masks = []
        for j in tiles:
            keep = (blks[j] > tau) | (ties[j] & (local[j] + seen <= room))
            masks.append(jnp.where(keep, 0.0, NEG))
            seen = seen + local[j][LANES - 1:LANES, :]
        return jnp.concatenate(masks, axis=0), seen

    m_ref[...] = jnp.full(m_ref.shape, NEG, F32)
    l_ref[...] = jnp.zeros(l_ref.shape, F32)
    for acc in acc_refs:
        acc[...] = jnp.zeros(acc.shape, F32)

    rowi = lax.broadcasted_iota(jnp.int32, (LANES, TQ), 0)
    for pair in range(ATTN_HEADS // 2):
        qp = qT_ref[pair * LANES:(pair + 1) * LANES, :]
        zero = jnp.zeros_like(qp)
        qm_ref[pair] = jnp.concatenate([jnp.where(rowi < ATTN_HEAD_DIM, qp, zero),
                                        jnp.where(rowi >= ATTN_HEAD_DIM, qp, zero)], axis=1)
    ones_rows = jnp.ones((2 * SUBLANES, CK), BF16)

    pq_col = posqc_ref[...]
    batch = pl.program_id(0)
    pq_first, pq_consec, pq_min = (pinfo_ref[batch, r, i] for r in (0, 1, 3))
    far_bias = [biasc_ref[0, h] for h in range(ATTN_HEADS)]
    zero_bias = [biasc_ref[1, h] for h in range(ATTN_HEADS)]
    q_ge_k = (lax.broadcasted_iota(jnp.int32, (LANES, TQ), 1) >= lax.broadcasted_iota(jnp.int32, (LANES, TQ), 0))
    n_sub = CK // LANES
    n_pairs = ATTN_HEADS // 2

    def chunk_is_far(c):
        return (pq_min - ckmax_ref[batch, jnp.minimum(c, seq // CK - 1)]) >= FAR_N

    def stage_bias(c):
        off = chunk_off(c)
        pk_row = posk_ref[:, pl.ds(off, CK)]
        for j in range(n_sub):
            rows = slice(j * LANES, (j + 1) * LANES)
            g = c * n_sub + j
            pk_first, pk_consec, pk_max = (pinfo_ref[batch, r, g] for r in (0, 1, 2))
            all_far = (pq_min - pk_max) >= FAR_N
            all_masked = (off + j * LANES) > (i * TQ + TQ - 1)
            consecutive = (pq_consec > 0) & (pk_consec > 0)
            gap = pq_first - pk_first

            def fill_const(rows=rows):
                for h in range(ATTN_HEADS):
                    b_ref[h, rows, :] = jnp.full((LANES, TQ), far_bias[h], F32)

            def fill_gap0(rows=rows):
                for h in range(ATTN_HEADS):
                    b_ref[h, rows, :] = jnp.where(q_ge_k, toep_ref[h], zero_bias[h])

            def fill_gap128(rows=rows):
                for h in range(ATTN_HEADS):
                    b_ref[h, rows, :] = jnp.where(q_ge_k, far_bias[h], toep_ref[h])

            def fill_lookup(rows=rows):
                pk_sub = pk_row[:, rows]
                n_qk = jnp.clip(pq_col - pk_sub, 0, BIAS_TABLE_N - 1).astype(F32)
                n_kq = n_qk.T.astype(jnp.int32)
                for h in range(ATTN_HEADS):
                    tb = jnp.broadcast_to(tbl_ref[h:h + 1, :], (LANES, BIAS_TABLE_N))
                    b_ref[h, rows, :] = jnp.take_along_axis(tb, n_kq, axis=1)

            def fill_near(fill_gap0=fill_gap0, fill_gap128=fill_gap128, fill_lookup=fill_lookup,
                          consecutive=consecutive, gap=gap):
                lax.cond(consecutive & (gap == 0), fill_gap0,
                         lambda: lax.cond(consecutive & (gap == LANES), fill_gap128, fill_lookup))

            lax.cond(all_far | all_masked, fill_const, fill_near)

    def logits_phase(c, s_buf, const_bias, seen):
        off = chunk_off(c)
        mb, seen = selection_mask(c, seen)
        m_cur = []
        for pair in range(n_pairs):
            kc = k_ref[pl.ds(off, CK), pair * LANES:(pair + 1) * LANES]
            s2 = jnp.dot(kc, qm_ref[pair], preferred_element_type=F32)
            for sub, h in enumerate((2 * pair, 2 * pair + 1)):
                s = s2[:, sub * TQ:(sub + 1) * TQ] + mb
                if not const_bias:
                    s = s + b_ref[h]
                s_buf[h] = s
                top = col_reduce(_fold_rows(s, jnp.maximum), jnp.max)
                m_cur.append(top + far_bias[h] if const_bias else top)
        return m_cur, seen

    def update_phase(c, s_buf, m_cur, const_bias):
        off = chunk_off(c)
        m_prev = m_ref[...]
        l_prev = l_ref[...]
        if isinstance(m_cur, list):
            m_new = [jnp.maximum(m_prev[h:h + 1, :], m_cur[h]) for h in range(ATTN_HEADS)]
            alpha = [jnp.exp2(m_prev[h:h + 1, :] - m_new[h]) for h in range(ATTN_HEADS)]
        else:
            m_all = jnp.maximum(m_prev, m_cur)
            a_all = jnp.exp2(m_prev - m_all)
            m_new = [m_all[h:h + 1, :] for h in range(ATTN_HEADS)]
            alpha = [a_all[h:h + 1, :] for h in range(ATTN_HEADS)]
        l_new = []
        for pair in range(n_pairs):
            heads = (2 * pair, 2 * pair + 1)
            sub_m = [m_new[h] - far_bias[h] if const_bias else m_new[h] for h in heads]
            p2 = jnp.concatenate([jnp.exp2(s_buf[h] - sm).astype(BF16) for h, sm in zip(heads, sub_m)],
                                 axis=1)
            lhs = jnp.concatenate([vT_ref[pair * LANES:(pair + 1) * LANES, pl.ds(off, CK)], ones_rows],
                                  axis=0)
            out = jnp.dot(lhs, p2, preferred_element_type=F32)
            for sub, h in enumerate(heads):
                cols = slice(sub * TQ, (sub + 1) * TQ)
                acc = acc_refs[h]
                acc[...] = alpha[h] * acc[...] + out[sub * ATTN_HEAD_DIM:(sub + 1) * ATTN_HEAD_DIM, cols]
                l_new.append(alpha[h] * l_prev[h:h + 1, :] + out[LANES:LANES + 1, cols])
        l_ref[...] = jnp.concatenate(l_new, axis=0)
        m_ref[...] = jnp.concatenate(m_new, axis=0)

    n_far = lax.while_loop(lambda c: (c < nch) & chunk_is_far(c), lambda c: c + 1, jnp.int32(0))
    n_steps = n_far // 2
    def far_logits(c, s_buf, seen):
        m_cur, seen = logits_phase(c, s_buf, True, seen)
        return jnp.concatenate(m_cur, axis=0), seen

    no_ties = jnp.zeros((1, TQ), F32)
    m_first, seen_first = lax.cond(n_steps > 0, lambda: far_logits(0, s_ref, no_ties),
                                   lambda: (jnp.zeros((ATTN_HEADS, TQ), F32), no_ties))

    def far_step(t, carry):
        m_even, seen_even, _ = carry
        c = 2 * t
        m_odd, seen_odd = far_logits(c + 1, s2_ref, seen_even)
        update_phase(c, s_ref, m_even, True)
        m_even, seen_even = far_logits(jnp.minimum(c + 2, 2 * n_steps - 2), s_ref, seen_odd)
        update_phase(c + 1, s2_ref, m_odd, True)
        return m_even, seen_even, seen_odd

    _, _, seen = lax.fori_loop(0, n_steps, far_step, (m_first, seen_first, no_ties))

    def tail_chunk(c, seen):
        def run(const_bias):
            if not const_bias:
                stage_bias(c)
            m_cur, seen_out = logits_phase(c, s_ref, const_bias, seen)
            update_phase(c, s_ref, m_cur, const_bias)
            return seen_out

        return lax.cond(chunk_is_far(c), lambda: run(True), lambda: run(False))

    lax.fori_loop(2 * n_steps, nch, tail_chunk, seen)

    outT = jnp.concatenate([acc_refs[h][...] / l_ref[h:h + 1, :] for h in range(ATTN_HEADS)], axis=0)
    o_ref[...] = outT.T.astype(o_ref.dtype)


def _toeplitz_kernel(tbl_ref, o_ref):
    q = lax.broadcasted_iota(jnp.int32, (LANES, LANES), 1)
    k = lax.broadcasted_iota(jnp.int32, (LANES, LANES), 0)
    idx = (q - k) & (BIAS_TABLE_N - 1)
    for h in range(ATTN_HEADS):
        tb = jnp.broadcast_to(tbl_ref[h:h + 1, :], (LANES, BIAS_TABLE_N))
        o_ref[h] = jnp.take_along_axis(tb, idx, axis=1)


def _sparse_attention(fm, tokb, ik, iwT, positions, rel_bias):
    B, S, _ = tokb.shape
    k_sel = min(TOPK_MAX, S // 4)
    bucket = _t5_bucket_table()
    tbl = rel_bias[bucket].T.astype(F32) * LOG2E
    bias_c = jnp.stack([tbl[:, BIAS_TABLE_N - 1], tbl[:, 0]])
    toep = pl.pallas_call(
        _toeplitz_kernel,
        out_shape=jax.ShapeDtypeStruct((ATTN_HEADS, LANES, LANES), F32),
        name="bias_toeplitz",
    )(tbl)
    pos_row = positions.reshape(B, 1, S)
    pos_col = positions.reshape(B, S, 1)
    pos_t = positions.reshape(B, S // LANES, LANES)
    consec = jnp.all(pos_t - pos_t[:, :, :1] == jnp.arange(LANES, dtype=positions.dtype), axis=-1)
    pinfo = jnp.stack([pos_t[:, :, 0], consec.astype(jnp.int32), jnp.max(pos_t, axis=-1),
                       jnp.min(pos_t, axis=-1)], axis=1).astype(jnp.int32)
    ck_max = jnp.max(positions.reshape(B, S // CK, CK), axis=-1)
    tril = jnp.tril(jnp.ones((LANES, LANES), BF16))
    sq = pl.Squeezed()
    iq_blk = (2 * ATTN_W) // IDX_Q_W
    k_blk = (tokb.shape[2] - ATTN_W) // ATTN_W
    return pl.pallas_call(
        functools.partial(_attn_kernel, k_sel=k_sel, seq=S),
        grid=(B, S // TQ),
        in_specs=[pl.BlockSpec(memory_space=pltpu.SMEM),
                  pl.BlockSpec(memory_space=pltpu.SMEM),
                  pl.BlockSpec(memory_space=pltpu.SMEM),
                  pl.BlockSpec((sq, IDX_Q_W, TQ), lambda b, i: (b, iq_blk, i)),
                  pl.BlockSpec((sq, SUBLANES, TQ), lambda b, i: (b, 0, i)),
                  pl.BlockSpec((sq, TQ, 1), lambda b, i: (b, i, 0)),
                  pl.BlockSpec((sq, 1, S), lambda b, i: (b, 0, 0)),
                  pl.BlockSpec((sq, S, IDX_HEAD_DIM), lambda b, i: (b, 0, 0)),
                  pl.BlockSpec((sq, ATTN_W, TQ), lambda b, i: (b, 0, i)),
                  pl.BlockSpec((sq, S, ATTN_W), lambda b, i: (b, 0, k_blk)),
                  pl.BlockSpec((sq, ATTN_W, S), lambda b, i: (b, 1, 0)),
                  pl.BlockSpec((ATTN_HEADS, BIAS_TABLE_N), lambda b, i: (0, 0)),
                  pl.BlockSpec((ATTN_HEADS, LANES, LANES), lambda b, i: (0, 0, 0)),
                  pl.BlockSpec((LANES, LANES), lambda b, i: (0, 0))],
        out_specs=pl.BlockSpec((sq, TQ, ATTN_W), lambda b, i: (b, i, 0)),
        out_shape=jax.ShapeDtypeStruct((B, S, ATTN_W), BF16),
        scratch_shapes=[pltpu.VMEM((S, TQ), F32),
                        pltpu.VMEM((ATTN_HEADS, CK, TQ), F32),
                        pltpu.VMEM((ATTN_HEADS, CK, TQ), F32),
                        pltpu.VMEM((ATTN_HEADS, CK, TQ), F32),
                        pltpu.VMEM((ATTN_HEADS // 2, LANES, 2 * TQ), BF16),
                        pltpu.VMEM((ATTN_HEADS, TQ), F32),
                        pltpu.VMEM((ATTN_HEADS, TQ), F32)]
                       + [pltpu.VMEM((ATTN_HEAD_DIM, TQ), F32)] * ATTN_HEADS,
        compiler_params=_cparams(2),
        name="sparse_attention",
    )(bias_c, pinfo, ck_max, fm, iwT, pos_col, pos_row, ik, fm, tokb, fm, tbl, toep, tril)


def _retention_kernel(q_ref, k_ref, v_ref, g_ref, decay_ref, xi_ref, zeta_ref, gch_ref, o_ref, r_ref):
    @pl.when(pl.program_id(0) == 0)
    def _():
        r_ref[...] = jnp.zeros(r_ref.shape, F32)

    C = RET_CHUNK
    lane = lax.broadcasted_iota(jnp.int32, (C, LANES), 1)
    row = lax.broadcasted_iota(jnp.int32, (LANES, RET_V_DIM), 0)
    for pair in range(RET_HEADS // 2):
        for b in range(q_ref.shape[0]):
            q_pair = q_ref[b, :, pair * LANES:(pair + 1) * LANES]
            k_pair = k_ref[b, :, pair * LANES:(pair + 1) * LANES]
            v_pair = v_ref[b, :, 2 * pair * RET_V_DIM:(2 * pair + 2) * RET_V_DIM]
            r_pair = r_ref[b, pair]
            r_bf = r_pair.astype(BF16)
            for sub in range(2):
                h = 2 * pair + sub
                in_head = (lane >= sub * RET_QK_DIM) & (lane < (sub + 1) * RET_QK_DIM)
                qm = jnp.where(in_head, q_pair, jnp.zeros_like(q_pair))
                v_h = v_pair[:, sub * RET_V_DIM:(sub + 1) * RET_V_DIM]
                inner = lax.dot_general(qm, k_pair, (((1,), (1,)), ((), ())),
                                        preferred_element_type=F32) * decay_ref[h]
                o = (jnp.dot(inner.astype(BF16), v_h, preferred_element_type=F32)
                     + jnp.dot(qm, r_bf, preferred_element_type=F32) * xi_ref[h])
                mu = jnp.mean(o, axis=-1, keepdims=True)
                d = o - mu
                var = jnp.mean(d * d, axis=-1, keepdims=True)
                hn = d * lax.rsqrt(var + LN_EPS)
                gate = g_ref[b, :, h * RET_V_DIM:(h + 1) * RET_V_DIM].astype(F32)
                o_ref[b, :, h * RET_V_DIM:(h + 1) * RET_V_DIM] = (gate * hn).astype(o_ref.dtype)
            kz = (k_pair.astype(F32) * zeta_ref[pair]).astype(BF16)
            upd = lax.dot_general(kz, v_pair, (((0,), (0,)), ((), ())), preferred_element_type=F32)
            r_ref[b, pair] = (r_pair * gch_ref[pair]
                              + jnp.where(row < RET_QK_DIM, upd[:, :RET_V_DIM], upd[:, RET_V_DIM:]))


def _retention(qk, tokb, gates, B, S):
    C = RET_CHUNK
    H = RET_HEADS
    nc = S // C
    gamma = 1.0 - 2.0 ** (-5.0 - jnp.arange(H, dtype=F32))
    log_g = jnp.log(gamma)
    n = jnp.arange(C, dtype=F32)
    diff = n[:, None] - n[None, :]
    decay_in = jnp.where(diff[None] >= 0, jnp.exp(log_g[:, None, None] * jnp.maximum(diff, 0.0)[None]), 0.0)
    xi = jnp.exp(log_g[None, :] * (n[:, None] + 1.0))
    zeta = jnp.exp(log_g[None, :] * (C - 1.0 - n[:, None]))
    g_chunk = jnp.exp(log_g * C)
    xi_b = jnp.broadcast_to(xi.T[:, :, None], (H, C, RET_V_DIM))
    zeta_b = jnp.repeat(zeta, RET_QK_DIM, axis=1).reshape(C, H // 2, LANES).transpose(1, 0, 2)
    gch_b = jnp.broadcast_to(jnp.repeat(g_chunk, RET_QK_DIM).reshape(H // 2, LANES, 1),
                             (H // 2, LANES, RET_V_DIM))
    qk3, tok3, gate3 = (a.reshape(B, S, a.shape[-1]) for a in (qk, tokb, gates))
    out = pl.pallas_call(
        _retention_kernel,
        grid=(nc,),
        in_specs=[pl.BlockSpec((B, C, RET_QK_W), lambda i: (0, i, 0)),
                  pl.BlockSpec((B, C, RET_QK_W), lambda i: (0, i, 1)),
                  pl.BlockSpec((B, C, RET_V_W), lambda i: (0, i, 0)),
                  pl.BlockSpec((B, C, RET_V_W), lambda i: (0, i, 0)),
                  pl.BlockSpec((H, C, C), lambda i: (0, 0, 0)),
                  pl.BlockSpec((H, C, RET_V_DIM), lambda i: (0, 0, 0)),
                  pl.BlockSpec((H // 2, C, LANES), lambda i: (0, 0, 0)),
                  pl.BlockSpec((H // 2, LANES, RET_V_DIM), lambda i: (0, 0, 0))],
        out_specs=pl.BlockSpec((B, C, RET_V_W), lambda i: (0, i, 0)),
        out_shape=jax.ShapeDtypeStruct((B, S, RET_V_W), BF16),
        scratch_shapes=[pltpu.VMEM((B, H // 2, LANES, RET_V_DIM), F32)],
        compiler_params=_cparams(1),
        name="retention",
    )(qk3, qk3, tok3, gate3, decay_in, xi_b, zeta_b, gch_b)
    return out.reshape(B * S, RET_V_W)


def _layer_norm(z, g, b):
    mu = jnp.mean(z, axis=-1, keepdims=True)
    d = z - mu
    var = jnp.mean(d * d, axis=-1, keepdims=True)
    return d * lax.rsqrt(var + LN_EPS) * g + b


def _merge_kernel(x_ref, ya_ref, yr_ref, ga_ref, gr_ref, wa_ref, wr_ref, wo_ref, g_ref, b_ref,
                  x1_ref):
    a = jnp.dot(ya_ref[...], wa_ref[...], preferred_element_type=F32)
    r = jnp.dot(yr_ref[...], wr_ref[...], preferred_element_type=F32)
    h = ga_ref[...].astype(F32) * a + gr_ref[...].astype(F32) * r
    mix = jnp.dot(h.astype(BF16), wo_ref[...], preferred_element_type=F32)
    x1_ref[...] = _layer_norm(DEEPNORM_ALPHA * x_ref[...] + mix, g_ref[...], b_ref[...])


def _merge(x, ya, yr, gates, wa, wr, wo, g, b, tm=MERGE_TM):
    T, D = x.shape
    tm = min(tm, T)
    row = lambda i: (i, 0)
    fixed = lambda i: (0, 0)
    return pl.pallas_call(
        _merge_kernel,
        grid=(T // tm,),
        in_specs=[pl.BlockSpec((tm, D), row),
                  pl.BlockSpec((tm, ya.shape[1]), row),
                  pl.BlockSpec((tm, yr.shape[1]), row),
                  pl.BlockSpec((tm, D), lambda i: (i, 1)),
                  pl.BlockSpec((tm, D), lambda i: (i, 2)),
                  pl.BlockSpec(wa.shape, fixed),
                  pl.BlockSpec(wr.shape, fixed),
                  pl.BlockSpec(wo.shape, fixed),
                  pl.BlockSpec((1, D), fixed),
                  pl.BlockSpec((1, D), fixed)],
        out_specs=pl.BlockSpec((tm, D), row),
        out_shape=jax.ShapeDtypeStruct((T, D), F32),
        compiler_params=_cparams(1),
        name="merge",
    )(x, ya, yr, gates, gates, wa, wr, wo, g, b)


def _ffn_kernel(x1_ref, wu_ref, wd_ref, g_ref, b_ref, o_ref, acc_ref, xb_ref):
    f = pl.program_id(1)

    @pl.when(f == 0)
    def _():
        acc_ref[...] = jnp.zeros(acc_ref.shape, F32)
        xb_ref[...] = x1_ref[...].astype(BF16)

    hid = jnp.maximum(jnp.dot(xb_ref[...], wu_ref[...], preferred_element_type=F32), 0.0)
    acc_ref[...] += jnp.dot((hid * hid).astype(BF16), wd_ref[...], preferred_element_type=F32)

    @pl.when(f == pl.num_programs(1) - 1)
    def _():
        o_ref[...] = _layer_norm(DEEPNORM_ALPHA * x1_ref[...] + acc_ref[...], g_ref[...], b_ref[...])


def _ffn(x1, wu, wd, g, b, tm=FFN_TM, tf=FFN_TF):
    T, D = x1.shape
    F = wu.shape[1]
    tm = min(tm, T)
    return pl.pallas_call(
        _ffn_kernel,
        grid=(T // tm, F // tf),
        in_specs=[pl.BlockSpec((tm, D), lambda i, f: (i, 0)),
                  pl.BlockSpec((D, tf), lambda i, f: (0, f)),
                  pl.BlockSpec((tf, D), lambda i, f: (f, 0)),
                  pl.BlockSpec((1, D), lambda i, f: (0, 0)),
                  pl.BlockSpec((1, D), lambda i, f: (0, 0))],
        out_specs=pl.BlockSpec((tm, D), lambda i, f: (i, 0)),
        out_shape=jax.ShapeDtypeStruct((T, D), F32),
        scratch_shapes=[pltpu.VMEM((tm, D), F32), pltpu.VMEM((tm, D), BF16)],
        compiler_params=_cparams(2),
        name="ffn",
    )(x1, wu, wd, g, b)


def _rot_half_weight(wT):
    N, D = wT.shape
    half = RET_QK_DIM // 2
    wh = wT.reshape(N // RET_QK_DIM, 2, half, D)
    return jnp.stack([-wh[:, 1], wh[:, 0]], axis=1).reshape(N, D)


def kernel(x, positions, w_in, rel_bias, idx_k_ln_g, idx_k_ln_b, w_attn_branch, w_ret_branch,
           w_out, ln_mix_g, ln_mix_b, w_up, w_down, ln_ffn_g, ln_ffn_b):
    B, S, D = x.shape
    T = B * S
    sizes = (ATTN_W, ATTN_W, ATTN_W, IDX_Q_W, IDX_HEAD_DIM, IDX_HEADS,
             RET_QK_W, RET_QK_W, RET_V_W, RET_V_W, D, D)
    offs = [0] + [int(o) for o in np.cumsum(sizes)]
    cos, sin = _rope_tables(positions)
    xf = x.reshape(T, D)
    for l in range(DEPTH):
        wT = jnp.swapaxes(w_in[l], 0, 1).astype(BF16)
        rows = [wT[offs[k]:offs[k + 1]] for k in range(len(sizes))]
        (w_qa, w_ka, w_va, w_iq, w_ik, w_iw, w_qr, w_kr, w_vr, w_gr, w_ga, w_gtr) = rows
        w_kr = w_kr * (RET_QK_DIM ** -0.5)

        pad = LANES - IDX_HEAD_DIM - IDX_HEADS
        w_idx = jnp.concatenate([w_ik, w_iw, jnp.zeros((pad, D), BF16)], axis=0)
        g_pad = jnp.concatenate([idx_k_ln_g[l], jnp.zeros((LANES - IDX_HEAD_DIM,), F32)]).reshape(1, LANES)
        b_pad = jnp.concatenate([idx_k_ln_b[l], jnp.zeros((LANES - IDX_HEAD_DIM,), F32)]).reshape(1, LANES)
        idx, xb = _proj_idx(xf, w_idx, g_pad, b_pad, (IDX_HEAD_DIM ** -0.5) * (IDX_HEADS ** -0.5))
        idx = idx.reshape(B, S, LANES)
        ik = idx[:, :, :IDX_HEAD_DIM].astype(BF16)
        iwT = jnp.swapaxes(idx[:, :, IDX_HEAD_DIM:IDX_HEAD_DIM + SUBLANES], 1, 2)

        fm = _proj_t(xb, jnp.concatenate([w_qa, w_va, w_iq], axis=0), B, S, BF16,
                     scaled_rows=ATTN_W, scale=ATTN_HEAD_DIM ** -0.5 * LOG2E)
        tokb = _proj(xb, jnp.concatenate([w_vr, w_ka], axis=0), BF16)
        gates = _proj_gates(xb, jnp.concatenate([w_gr, w_ga, w_gtr], axis=0), tn=D)
        w_rope = jnp.concatenate([w_qr, w_kr], axis=0)
        w_rope_rot = jnp.concatenate([_rot_half_weight(w_qr), _rot_half_weight(w_kr)], axis=0)
        qk_r = _proj_rope(xb, w_rope, w_rope_rot, cos, sin)

        y_a = _sparse_attention(fm, tokb.reshape(B, S, -1), ik, iwT, positions, rel_bias)
        y_r = _retention(qk_r, tokb, gates, B, S)
        x1 = _merge(xf, y_a.reshape(T, ATTN_W), y_r, gates,
                    w_attn_branch[l].astype(BF16), w_ret_branch[l].astype(BF16),
                    w_out[l].astype(BF16), ln_mix_g[l].reshape(1, D), ln_mix_b[l].reshape(1, D))
        xf = _ffn(x1, w_up[l].astype(BF16), w_down[l].astype(BF16),
                  ln_ffn_g[l].reshape(1, D), ln_ffn_b[l].reshape(1, D))
    return xf.reshape(B, S, D)
```

```python
import functools
import math

import numpy as np
import jax
import jax.numpy as jnp
from jax import lax
from jax.experimental import pallas as pl
from jax.experimental.pallas import tpu as pltpu

F32 = jnp.float32
BF16 = jnp.bfloat16

ATTN_HEADS = 8
ATTN_HEAD_DIM = 64
ATTN_W = ATTN_HEADS * ATTN_HEAD_DIM
IDX_HEADS = 4
IDX_HEAD_DIM = 64
IDX_Q_W = IDX_HEADS * IDX_HEAD_DIM
TOPK_MAX = 256
RET_HEADS = 8
RET_QK_DIM = 64
RET_V_DIM = 128
RET_QK_W = RET_HEADS * RET_QK_DIM
RET_V_W = RET_HEADS * RET_V_DIM
RET_CHUNK = 128
ROPE_BASE = 10000.0
NUM_BUCKETS = 32
MAX_DISTANCE = 128
LN_EPS = 1e-5
DEPTH = 1
DEEPNORM_ALPHA = (2.0 * DEPTH) ** 0.25

LANES = 128
SUBLANES = 8
VMEM_LIMIT = 56 * 1024 * 1024

PROJ_TM = 1024
GATES_TM = 2048
MERGE_TM = 512
FFN_TM = 1024
FFN_TF = 1024

TQ = 128
CK = 512
NEG = -1e30
LOG2E = math.log2(math.e)
BISECT_ROUNDS = 20
BIAS_TABLE_N = 128
FAR_N = 113


def _cparams(n_grid):
    return pltpu.CompilerParams(
        dimension_semantics=("arbitrary",) * n_grid,
        vmem_limit_bytes=VMEM_LIMIT)


def _trig_kernel(pos_ref, inv_ref, cos_ref, sin_ref):
    ang = pos_ref[...] * inv_ref[...]
    cos_ref[...] = jnp.cos(ang)
    sin_ref[...] = jnp.sin(ang)


def _rope_tables(positions):
    B, S = positions.shape
    half = RET_QK_DIM // 2
    inv = ROPE_BASE ** (-jnp.arange(half, dtype=F32) / half)
    per_row = LANES // half
    rows = B * S // per_row
    pos_e = jnp.repeat(positions.astype(F32).reshape(rows, per_row), half, axis=1)
    inv_e = jnp.tile(inv, per_row).reshape(1, LANES)
    tr = min(rows, 1024)
    cos, sin = pl.pallas_call(
        _trig_kernel,
        grid=(rows // tr,),
        in_specs=[pl.BlockSpec((tr, LANES), lambda i: (i, 0)),
                  pl.BlockSpec((1, LANES), lambda i: (0, 0))],
        out_specs=[pl.BlockSpec((tr, LANES), lambda i: (i, 0))] * 2,
        out_shape=[jax.ShapeDtypeStruct((rows, LANES), F32)] * 2,
        compiler_params=_cparams(1),
        name="rope_tables",
    )(pos_e, inv_e)
    cos = jnp.tile(cos.reshape(B * S, half), (1, per_row))
    sin = jnp.tile(sin.reshape(B * S, half), (1, per_row))
    return cos, sin


def _x_wt(x, wT):
    return lax.dot_general(x, wT, (((1,), (1,)), ((), ())), preferred_element_type=F32)


def _proj_kernel(x_ref, w_ref, o_ref):
    o_ref[...] = _x_wt(x_ref[...], w_ref[...]).astype(o_ref.dtype)


def _proj(xb, wT, out_dtype, tm=PROJ_TM):
    T, D = xb.shape
    N = wT.shape[0]
    tn = N
    tm = min(tm, T)
    return pl.pallas_call(
        _proj_kernel,
        grid=(T // tm, N // tn),
        in_specs=[pl.BlockSpec((tm, D), lambda i, j: (i, 0)),
                  pl.BlockSpec((tn, D), lambda i, j: (j, 0))],
        out_specs=pl.BlockSpec((tm, tn), lambda i, j: (i, j)),
        out_shape=jax.ShapeDtypeStruct((T, N), out_dtype),
        compiler_params=_cparams(2),
        name="proj",
    )(xb, wT)


def _proj_gates_kernel(x_ref, w_ref, o_ref):
    acc = _x_wt(x_ref[...], w_ref[...])
    sig = 0.5 * jnp.tanh(0.5 * acc) + 0.5
    o_ref[...] = jnp.where(pl.program_id(1) == 0, acc * sig, sig).astype(o_ref.dtype)


def _proj_gates(xb, wT, tn, tm=GATES_TM):
    T, D = xb.shape
    N = wT.shape[0]
    tm = min(tm, T)
    return pl.pallas_call(
        _proj_gates_kernel,
        grid=(T // tm, N // tn),
        in_specs=[pl.BlockSpec((tm, D), lambda i, j: (i, 0)),
                  pl.BlockSpec((tn, D), lambda i, j: (j, 0))],
        out_specs=pl.BlockSpec((tm, tn), lambda i, j: (i, j)),
        out_shape=jax.ShapeDtypeStruct((T, N), BF16),
        compiler_params=_cparams(2),
        name="proj_gates",
    )(xb, wT)


def _proj_t_kernel(wT_ref, x_ref, o_ref, *, scaled_rows, scale):
    acc = lax.dot_general(wT_ref[...], x_ref[...], (((1,), (1,)), ((), ())), preferred_element_type=F32)
    o_ref[:scaled_rows, :] = (acc[:scaled_rows] * scale).astype(o_ref.dtype)
    o_ref[scaled_rows:, :] = acc[scaled_rows:].astype(o_ref.dtype)


def _proj_t(xb, wT, B, S, out_dtype, scaled_rows, scale, tm=PROJ_TM):
    T, D = xb.shape
    N = wT.shape[0]
    tn = N
    tm = min(tm, S)
    nsb = S // tm
    return pl.pallas_call(
        functools.partial(_proj_t_kernel, scaled_rows=scaled_rows, scale=scale),
        grid=(T // tm, N // tn),
        in_specs=[pl.BlockSpec((tn, D), lambda i, j: (j, 0)),
                  pl.BlockSpec((tm, D), lambda i, j: (i, 0))],
        out_specs=pl.BlockSpec((pl.Squeezed(), tn, tm), lambda i, j: (i // nsb, j, i % nsb)),
        out_shape=jax.ShapeDtypeStruct((B, N, S), out_dtype),
        compiler_params=_cparams(2),
        name="proj_t",
    )(wT, xb)


def _proj_rope_kernel(x_ref, w_ref, wr_ref, cos_ref, sin_ref, o_ref):
    x = x_ref[...]
    a = _x_wt(x, w_ref[...])
    r = _x_wt(x, wr_ref[...])
    reps = a.shape[1] // LANES
    cos = jnp.concatenate([cos_ref[...]] * reps, axis=1)
    sin = jnp.concatenate([sin_ref[...]] * reps, axis=1)
    o_ref[...] = (a * cos + r * sin).astype(o_ref.dtype)


def _proj_rope(xb, wT, wT_rot, cos, sin, tm=PROJ_TM):
    T, D = xb.shape
    N = wT.shape[0]
    tn = N
    tm = min(tm, T)
    return pl.pallas_call(
        _proj_rope_kernel,
        grid=(T // tm, N // tn),
        in_specs=[pl.BlockSpec((tm, D), lambda i, j: (i, 0)),
                  pl.BlockSpec((tn, D), lambda i, j: (j, 0)),
                  pl.BlockSpec((tn, D), lambda i, j: (j, 0)),
                  pl.BlockSpec((tm, LANES), lambda i, j: (i, 0)),
                  pl.BlockSpec((tm, LANES), lambda i, j: (i, 0))],
        out_specs=pl.BlockSpec((tm, tn), lambda i, j: (i, j)),
        out_shape=jax.ShapeDtypeStruct((T, N), BF16),
        compiler_params=_cparams(2),
        name="proj_rope",
    )(xb, wT, wT_rot, cos, sin)


def _proj_idx_kernel(x_ref, w_ref, g_ref, b_ref, o_ref, xb_ref, *, iw_scale):
    xb = x_ref[...].astype(BF16)
    xb_ref[...] = xb
    acc = _x_wt(xb, w_ref[...])
    lane = lax.broadcasted_iota(jnp.int32, acc.shape, 1)
    is_k = lane < IDX_HEAD_DIM
    mu = jnp.sum(jnp.where(is_k, acc, 0.0), axis=-1, keepdims=True) / IDX_HEAD_DIM
    d = acc - mu
    var = jnp.sum(jnp.where(is_k, d * d, 0.0), axis=-1, keepdims=True) / IDX_HEAD_DIM
    ln = d * lax.rsqrt(var + LN_EPS) * g_ref[...] + b_ref[...]
    o_ref[...] = jnp.where(is_k, ln, acc * iw_scale)


def _proj_idx(x, w_pad, g_pad, b_pad, iw_scale, tm=PROJ_TM):
    T, D = x.shape
    tm = min(tm, T)
    return pl.pallas_call(
        functools.partial(_proj_idx_kernel, iw_scale=iw_scale),
        grid=(T // tm,),
        in_specs=[pl.BlockSpec((tm, D), lambda i: (i, 0)),
                  pl.BlockSpec((LANES, D), lambda i: (0, 0)),
                  pl.BlockSpec((1, LANES), lambda i: (0, 0)),
                  pl.BlockSpec((1, LANES), lambda i: (0, 0))],
        out_specs=[pl.BlockSpec((tm, LANES), lambda i: (i, 0)), pl.BlockSpec((tm, D), lambda i: (i, 0))],
        out_shape=[jax.ShapeDtypeStruct((T, LANES), F32), jax.ShapeDtypeStruct((T, D), BF16)],
        compiler_params=_cparams(1),
        name="proj_idx",
    )(x, w_pad, g_pad, b_pad)


def _t5_bucket_table():
    n = np.arange(BIAS_TABLE_N)
    max_exact = NUM_BUCKETS // 2
    nf = np.maximum(n, 1).astype(np.float64)
    large = max_exact + (np.log(nf / max_exact) / math.log(MAX_DISTANCE / max_exact)
                         * (NUM_BUCKETS - max_exact)).astype(np.int64)
    large = np.minimum(large, NUM_BUCKETS - 1)
    bucket = np.where(n < max_exact, n, large)
    assert np.all(bucket[FAR_N:] == NUM_BUCKETS - 1) and bucket[FAR_N - 1] != NUM_BUCKETS - 1
    return bucket.astype(np.int32)


def _fold_rows(a, op):
    parts = [a[r:r + SUBLANES] for r in range(0, a.shape[0], SUBLANES)]
    while len(parts) > 1:
        nxt = [op(parts[k], parts[k + 1]) for k in range(0, len(parts) - 1, 2)]
        if len(parts) % 2:
            nxt.append(parts[-1])
        parts = nxt
    return parts[0]


def _attn_kernel(biasc_ref, pinfo_ref, ckmax_ref, iqT_ref, iwT_ref, posqc_ref, posk_ref, ik_ref, qT_ref, k_ref, vT_ref,
                 tbl_ref, toep_ref, tril_ref, o_ref, sc_ref, s_ref, s2_ref, b_ref, qm_ref, m_ref, l_ref, *acc_refs, k_sel, seq):
    i = pl.program_id(1)
    nch = (i * TQ + TQ + CK - 1) // CK
    q_idx = i * TQ + lax.broadcasted_iota(jnp.int32, (1, TQ), 1)
    kf = float(k_sel)

    def chunk_off(c):
        return pl.multiple_of(c * CK, CK)

    def key_idx(off):
        return off + lax.broadcasted_iota(jnp.int32, (CK, TQ), 0)

    def col_reduce(part, op):
        return op(part, axis=0, keepdims=True)

    iqT = iqT_ref[...]
    iwT = iwT_ref[...]
    iq_wide = jnp.concatenate([iqT[h * IDX_HEAD_DIM:(h + 1) * IDX_HEAD_DIM, :] for h in range(IDX_HEADS)], axis=1)

    def score_matmul(c, z_buf):
        ikc = ik_ref[pl.ds(chunk_off(c), CK), :]
        z = jnp.dot(ikc, iq_wide, preferred_element_type=F32)
        for h in range(IDX_HEADS):
            z_buf[h] = z[:, h * TQ:(h + 1) * TQ]

    def score_finish(c, z_buf, carry, masked):
        mn, mx = carry
        off = chunk_off(c)
        s = None
        for h in range(IDX_HEADS):
            t = jnp.maximum(z_buf[h], 0.0) * iwT[h:h + 1, :]
            s = t if s is None else s + t
        if masked:
            causal = key_idx(off) <= q_idx
            s_lo = jnp.where(causal, s, -jnp.inf)
            s_hi = jnp.where(causal, s, jnp.inf)
        else:
            s_lo = s_hi = s
        sc_ref[pl.ds(off, CK), :] = s_lo
        mn = jnp.minimum(mn, _fold_rows(s_hi, jnp.minimum))
        mx = jnp.maximum(mx, _fold_rows(s_lo, jnp.maximum))
        return mn, mx

    def score_chunk(c, carry, masked):
        score_matmul(c, s_ref)
        return score_finish(c, s_ref, carry, masked)

    n_inner = nch - 1
    n_pairs_sc = n_inner // 2

    @pl.when(n_pairs_sc > 0)
    def _():
        score_matmul(0, s_ref)

    def score_step(t, carry):
        c = 2 * t
        score_matmul(c + 1, s2_ref)
        carry = score_finish(c, s_ref, carry, False)
        score_matmul(jnp.minimum(c + 2, 2 * n_pairs_sc - 2), s_ref)
        return score_finish(c + 1, s2_ref, carry, False)

    mn8, mx8 = lax.fori_loop(0, n_pairs_sc, score_step,
                             (jnp.full((SUBLANES, TQ), jnp.inf, F32), jnp.full((SUBLANES, TQ), -jnp.inf, F32)))
    mn8, mx8 = lax.fori_loop(2 * n_pairs_sc, n_inner, functools.partial(score_chunk, masked=False), (mn8, mx8))
    mn8, mx8 = score_chunk(nch - 1, (mn8, mx8), masked=True)
    mn = col_reduce(mn8, jnp.min)
    mx = col_reduce(mx8, jnp.max)

    def over_chunks(fn, init):
        def one(c, carry):
            return fn(sc_ref[pl.ds(chunk_off(c), CK), :], carry)

        carry = lax.fori_loop(0, nch // 2, lambda t, carry: one(2 * t + 1, one(2 * t, carry)), init)
        return lax.fori_loop(2 * (nch // 2), nch, one, carry)

    def count(pred_fn):
        acc = over_chunks(lambda blk, acc: acc + _fold_rows(jnp.where(pred_fn(blk), 1.0, 0.0), jnp.add),
                          jnp.zeros((SUBLANES, TQ), F32))
        return col_reduce(acc, jnp.sum)

    def bisect_round(_, st):
        lo, hi, c_lo = st
        mid = 0.5 * (lo + hi)
        c = count(lambda blk: blk >= mid)
        ok = c >= kf
        return jnp.where(ok, mid, lo), jnp.where(ok, hi, mid), jnp.where(ok, c, c_lo)

    c_all = (q_idx + 1).astype(F32)
    lo, hi, c_lo = lax.fori_loop(0, BISECT_ROUNDS, bisect_round, (mn, mx, c_all))

    cur0 = col_reduce(over_chunks(
        lambda blk, acc: jnp.minimum(acc, _fold_rows(jnp.where(blk >= lo, blk, jnp.inf), jnp.minimum)),
        jnp.full((SUBLANES, TQ), jnp.inf, F32)), jnp.min)

    def walk_cond(st):
        return st[3] > 0.0

    def walk_body(st):
        cur, c_ge, _, _ = st

        def body(blk, carry):
            cnt, nxt = carry
            gt = blk > cur
            cnt = cnt + _fold_rows(jnp.where(gt, 1.0, 0.0), jnp.add)
            nxt = jnp.minimum(nxt, _fold_rows(jnp.where(gt, blk, jnp.inf), jnp.minimum))
            return cnt, nxt

        cnt, nxt = over_chunks(body, (jnp.zeros((SUBLANES, TQ), F32), jnp.full((SUBLANES, TQ), jnp.inf, F32)))
        c_gt = col_reduce(cnt, jnp.sum)
        nxt = col_reduce(nxt, jnp.min)
        adv = c_gt >= kf
        cur = jnp.where(adv, nxt, cur)
        c_ge = jnp.where(adv, c_gt, c_ge)
        return cur, c_ge, c_gt, jnp.max(jnp.where(adv, 1.0, 0.0))

    tau, c_ge, c_gt, _ = lax.while_loop(
        walk_cond, walk_body, (cur0, c_lo, jnp.zeros((1, TQ), F32), jnp.float32(1.0)))

    room = kf - c_gt

    def selection_mask(c, seen):
        off = chunk_off(c)
        tiles = range(CK // LANES)
        blks = [sc_ref[pl.ds(off + j * LANES, LANES), :] for j in tiles]
        ties = [blk == tau for blk in blks]
        local = [jnp.dot(tril_ref[...], jnp.where(tie, 1.0, 0.0).astype(BF16), preferred_element_type=F32)
                 for tie in ties]
        masks = []
        for j in tiles:
            keep = (blks[j] > tau) | (ties[j] & (local[j] + seen <= room))
            masks.append(jnp.where(keep, 0.0, NEG))
            seen = seen + local[j][LANES - 1:LANES, :]
        return jnp.concatenate(masks, axis=0), seen

    m_ref[...] = jnp.full(m_ref.shape, NEG, F32)
    l_ref[...] = jnp.zeros(l_ref.shape, F32)
    for acc in acc_refs:
        acc[...] = jnp.zeros(acc.shape, F32)

    rowi = lax.broadcasted_iota(jnp.int32, (LANES, TQ), 0)
    for pair in range(ATTN_HEADS // 2):
        qp = qT_ref[pair * LANES:(pair + 1) * LANES, :]
        zero = jnp.zeros_like(qp)
        qm_ref[pair] = jnp.concatenate([jnp.where(rowi < ATTN_HEAD_DIM, qp, zero),
                                        jnp.where(rowi >= ATTN_HEAD_DIM, qp, zero)], axis=1)
    ones_rows = jnp.ones((2 * SUBLANES, CK), BF16)

    pq_col = posqc_ref[...]
    batch = pl.program_id(0)
    pq_first, pq_consec, pq_min = (pinfo_ref[batch, r, i] for r in (0, 1, 3))
    far_bias = [biasc_ref[0, h] for h in range(ATTN_HEADS)]
    zero_bias = [biasc_ref[1, h] for h in range(ATTN_HEADS)]
    q_ge_k = (lax.broadcasted_iota(jnp.int32, (LANES, TQ), 1) >= lax.broadcasted_iota(jnp.int32, (LANES, TQ), 0))
    n_sub = CK // LANES
    n_pairs = ATTN_HEADS // 2

    def chunk_is_far(c):
        return (pq_min - ckmax_ref[batch, jnp.minimum(c, seq // CK - 1)]) >= FAR_N

    def stage_bias(c):
        off = chunk_off(c)
        pk_row = posk_ref[:, pl.ds(off, CK)]
        for j in range(n_sub):
            rows = slice(j * LANES, (j + 1) * LANES)
            g = c * n_sub + j
            pk_first, pk_consec, pk_max = (pinfo_ref[batch, r, g] for r in (0, 1, 2))
            all_far = (pq_min - pk_max) >= FAR_N
            all_masked = (off + j * LANES) > (i * TQ + TQ - 1)
            consecutive = (pq_consec > 0) & (pk_consec > 0)
            gap = pq_first - pk_first

            def fill_const(rows=rows):
                for h in range(ATTN_HEADS):
                    b_ref[h, rows, :] = jnp.full((LANES, TQ), far_bias[h], F32)

            def fill_gap0(rows=rows):
                for h in range(ATTN_HEADS):
                    b_ref[h, rows, :] = jnp.where(q_ge_k, toep_ref[h], zero_bias[h])

            def fill_gap128(rows=rows):
                for h in range(ATTN_HEADS):
                    b_ref[h, rows, :] = jnp.where(q_ge_k, far_bias[h], toep_ref[h])

            def fill_lookup(rows=rows):
                pk_sub = pk_row[:, rows]
                n_qk = jnp.clip(pq_col - pk_sub, 0, BIAS_TABLE_N - 1).astype(F32)
                n_kq = n_qk.T.astype(jnp.int32)
                for h in range(ATTN_HEADS):
                    tb = jnp.broadcast_to(tbl_ref[h:h + 1, :], (LANES, BIAS_TABLE_N))
                    b_ref[h, rows, :] = jnp.take_along_axis(tb, n_kq, axis=1)

            def fill_near(fill_gap0=fill_gap0, fill_gap128=fill_gap128, fill_lookup=fill_lookup,
                          consecutive=consecutive, gap=gap):
                lax.cond(consecutive & (gap == 0), fill_gap0,
                         lambda: lax.cond(consecutive & (gap == LANES), fill_gap128, fill_lookup))

            lax.cond(all_far | all_masked, fill_const, fill_near)

    def logits_phase(c, s_buf, const_bias, seen):
        off = chunk_off(c)
        mb, seen = selection_mask(c, seen)
        m_cur = []
        for pair in range(n_pairs):
            kc = k_ref[pl.ds(off, CK), pair * LANES:(pair + 1) * LANES]
            s2 = jnp.dot(kc, qm_ref[pair], preferred_element_type=F32)
            for sub, h in enumerate((2 * pair, 2 * pair + 1)):
                s = s2[:, sub * TQ:(sub + 1) * TQ] + mb
                if not const_bias:
                    s = s + b_ref[h]
                s_buf[h] = s
                top = col_reduce(_fold_rows(s, jnp.maximum), jnp.max)
                m_cur.append(top + far_bias[h] if const_bias else top)
        return m_cur, seen

    def update_phase(c, s_buf, m_cur, const_bias):
        off = chunk_off(c)
        m_prev = m_ref[...]
        l_prev = l_ref[...]
        if isinstance(m_cur, list):
            m_new = [jnp.maximum(m_prev[h:h + 1, :], m_cur[h]) for h in range(ATTN_HEADS)]
            alpha = [jnp.exp2(m_prev[h:h + 1, :] - m_new[h]) for h in range(ATTN_HEADS)]
        else:
            m_all = jnp.maximum(m_prev, m_cur)
            a_all = jnp.exp2(m_prev - m_all)
            m_new = [m_all[h:h + 1, :] for h in range(ATTN_HEADS)]
            alpha = [a_all[h:h + 1, :] for h in range(ATTN_HEADS)]
        l_new = []
        for pair in range(n_pairs):
            heads = (2 * pair, 2 * pair + 1)
            sub_m = [m_new[h] - far_bias[h] if const_bias else m_new[h] for h in heads]
            p2 = jnp.concatenate([jnp.exp2(s_buf[h] - sm).astype(BF16) for h, sm in zip(heads, sub_m)],
                                 axis=1)
            lhs = jnp.concatenate([vT_ref[pair * LANES:(pair + 1) * LANES, pl.ds(off, CK)], ones_rows],
                                  axis=0)
            out = jnp.dot(lhs, p2, preferred_element_type=F32)
            for sub, h in enumerate(heads):
                cols = slice(sub * TQ, (sub + 1) * TQ)
                acc = acc_refs[h]
                acc[...] = alpha[h] * acc[...] + out[sub * ATTN_HEAD_DIM:(sub + 1) * ATTN_HEAD_DIM, cols]
                l_new.append(alpha[h] * l_prev[h:h + 1, :] + out[LANES:LANES + 1, cols])
        l_ref[...] = jnp.concatenate(l_new, axis=0)
        m_ref[...] = jnp.concatenate(m_new, axis=0)

    n_far = lax.while_loop(lambda c: (c < nch) & chunk_is_far(c), lambda c: c + 1, jnp.int32(0))
    n_steps = n_far // 2
    def far_logits(c, s_buf, seen):
        m_cur, seen = logits_phase(c, s_buf, True, seen)
        return jnp.concatenate(m_cur, axis=0), seen

    no_ties = jnp.zeros((1, TQ), F32)
    m_first, seen_first = lax.cond(n_steps > 0, lambda: far_logits(0, s_ref, no_ties),
                                   lambda: (jnp.zeros((ATTN_HEADS, TQ), F32), no_ties))

    def far_step(t, carry):
        m_even, seen_even, _ = carry
        c = 2 * t
        m_odd, seen_odd = far_logits(c + 1, s2_ref, seen_even)
        update_phase(c, s_ref, m_even, True)
        m_even, seen_even = far_logits(jnp.minimum(c + 2, n_far - 1), s_ref, seen_odd)
        update_phase(c + 1, s2_ref, m_odd, True)
        return m_even, seen_even, seen_odd

    m_even, seen_even, seen = lax.fori_loop(0, n_steps, far_step, (m_first, seen_first, no_ties))
    leftover_staged = (n_steps > 0) & (n_far % 2 == 1)

    def finish_leftover():
        update_phase(n_far - 1, s_ref, m_even, True)
        return seen_even

    seen = lax.cond(leftover_staged, finish_leftover, lambda: seen)
    n_done = jnp.where(leftover_staged, n_far, 2 * n_steps)

    def tail_chunk(c, seen):
        def run(const_bias):
            if not const_bias:
                stage_bias(c)
            m_cur, seen_out = logits_phase(c, s_ref, const_bias, seen)
            update_phase(c, s_ref, m_cur, const_bias)
            return seen_out

        return lax.cond(chunk_is_far(c), lambda: run(True), lambda: run(False))

    lax.fori_loop(n_done, nch, tail_chunk, seen)

    outT = jnp.concatenate([acc_refs[h][...] / l_ref[h:h + 1, :] for h in range(ATTN_HEADS)], axis=0)
    o_ref[...] = outT.T.astype(o_ref.dtype)


def _toeplitz_kernel(tbl_ref, o_ref):
    q = lax.broadcasted_iota(jnp.int32, (LANES, LANES), 1)
    k = lax.broadcasted_iota(jnp.int32, (LANES, LANES), 0)
    idx = (q - k) & (BIAS_TABLE_N - 1)
    for h in range(ATTN_HEADS):
        tb = jnp.broadcast_to(tbl_ref[h:h + 1, :], (LANES, BIAS_TABLE_N))
        o_ref[h] = jnp.take_along_axis(tb, idx, axis=1)


def _sparse_attention(fm, tokb, ik, iwT, positions, rel_bias):
    B, S, _ = tokb.shape
    k_sel = min(TOPK_MAX, S // 4)
    bucket = _t5_bucket_table()
    tbl = rel_bias[bucket].T.astype(F32) * LOG2E
    bias_c = jnp.stack([tbl[:, BIAS_TABLE_N - 1], tbl[:, 0]])
    toep = pl.pallas_call(
        _toeplitz_kernel,
        out_shape=jax.ShapeDtypeStruct((ATTN_HEADS, LANES, LANES), F32),
        name="bias_toeplitz",
    )(tbl)
    pos_row = positions.reshape(B, 1, S)
    pos_col = positions.reshape(B, S, 1)
    pos_t = positions.reshape(B, S // LANES, LANES)
    consec = jnp.all(pos_t - pos_t[:, :, :1] == jnp.arange(LANES, dtype=positions.dtype), axis=-1)
    pinfo = jnp.stack([pos_t[:, :, 0], consec.astype(jnp.int32), jnp.max(pos_t, axis=-1),
                       jnp.min(pos_t, axis=-1)], axis=1).astype(jnp.int32)
    ck_max = jnp.max(positions.reshape(B, S // CK, CK), axis=-1)
    tril = jnp.tril(jnp.ones((LANES, LANES), BF16))
    sq = pl.Squeezed()
    iq_blk = (2 * ATTN_W) // IDX_Q_W
    k_blk = (tokb.shape[2] - ATTN_W) // ATTN_W
    return pl.pallas_call(
        functools.partial(_attn_kernel, k_sel=k_sel, seq=S),
        grid=(B, S // TQ),
        in_specs=[pl.BlockSpec(memory_space=pltpu.SMEM),
                  pl.BlockSpec(memory_space=pltpu.SMEM),
                  pl.BlockSpec(memory_space=pltpu.SMEM),
                  pl.BlockSpec((sq, IDX_Q_W, TQ), lambda b, i: (b, iq_blk, i)),
                  pl.BlockSpec((sq, SUBLANES, TQ), lambda b, i: (b, 0, i)),
                  pl.BlockSpec((sq, TQ, 1), lambda b, i: (b, i, 0)),
                  pl.BlockSpec((sq, 1, S), lambda b, i: (b, 0, 0)),
                  pl.BlockSpec((sq, S, IDX_HEAD_DIM), lambda b, i: (b, 0, 0)),
                  pl.BlockSpec((sq, ATTN_W, TQ), lambda b, i: (b, 0, i)),
                  pl.BlockSpec((sq, S, ATTN_W), lambda b, i: (b, 0, k_blk)),
                  pl.BlockSpec((sq, ATTN_W, S), lambda b, i: (b, 1, 0)),
                  pl.BlockSpec((ATTN_HEADS, BIAS_TABLE_N), lambda b, i: (0, 0)),
                  pl.BlockSpec((ATTN_HEADS, LANES, LANES), lambda b, i: (0, 0, 0)),
                  pl.BlockSpec((LANES, LANES), lambda b, i: (0, 0))],
        out_specs=pl.BlockSpec((sq, TQ, ATTN_W), lambda b, i: (b, i, 0)),
        out_shape=jax.ShapeDtypeStruct((B, S, ATTN_W), BF16),
        scratch_shapes=[pltpu.VMEM((S, TQ), F32),
                        pltpu.VMEM((ATTN_HEADS, CK, TQ), F32),
                        pltpu.VMEM((ATTN_HEADS, CK, TQ), F32),
                        pltpu.VMEM((ATTN_HEADS, CK, TQ), F32),
                        pltpu.VMEM((ATTN_HEADS // 2, LANES, 2 * TQ), BF16),
                        pltpu.VMEM((ATTN_HEADS, TQ), F32),
                        pltpu.VMEM((ATTN_HEADS, TQ), F32)]
                       + [pltpu.VMEM((ATTN_HEAD_DIM, TQ), F32)] * ATTN_HEADS,
        compiler_params=_cparams(2),
        name="sparse_attention",
    )(bias_c, pinfo, ck_max, fm, iwT, pos_col, pos_row, ik, fm, tokb, fm, tbl, toep, tril)


def _retention_kernel(q_ref, k_ref, v_ref, g_ref, decay_ref, xi_ref, zeta_ref, gch_ref, o_ref, r_ref):
    @pl.when(pl.program_id(0) == 0)
    def _():
        r_ref[...] = jnp.zeros(r_ref.shape, F32)

    C = RET_CHUNK
    lane = lax.broadcasted_iota(jnp.int32, (C, LANES), 1)
    row = lax.broadcasted_iota(jnp.int32, (LANES, RET_V_DIM), 0)
    for pair in range(RET_HEADS // 2):
        for b in range(q_ref.shape[0]):
            q_pair = q_ref[b, :, pair * LANES:(pair + 1) * LANES]
            k_pair = k_ref[b, :, pair * LANES:(pair + 1) * LANES]
            v_pair = v_ref[b, :, 2 * pair * RET_V_DIM:(2 * pair + 2) * RET_V_DIM]
            r_pair = r_ref[b, pair]
            r_bf = r_pair.astype(BF16)
            for sub in range(2):
                h = 2 * pair + sub
                in_head = (lane >= sub * RET_QK_DIM) & (lane < (sub + 1) * RET_QK_DIM)
                qm = jnp.where(in_head, q_pair, jnp.zeros_like(q_pair))
                v_h = v_pair[:, sub * RET_V_DIM:(sub + 1) * RET_V_DIM]
                inner = lax.dot_general(qm, k_pair, (((1,), (1,)), ((), ())),
                                        preferred_element_type=F32) * decay_ref[h]
                o = (jnp.dot(inner.astype(BF16), v_h, preferred_element_type=F32)
                     + jnp.dot(qm, r_bf, preferred_element_type=F32) * xi_ref[h])
                mu = jnp.mean(o, axis=-1, keepdims=True)
                d = o - mu
                var = jnp.mean(d * d, axis=-1, keepdims=True)
                hn = d * lax.rsqrt(var + LN_EPS)
                gate = g_ref[b, :, h * RET_V_DIM:(h + 1) * RET_V_DIM].astype(F32)
                o_ref[b, :, h * RET_V_DIM:(h + 1) * RET_V_DIM] = (gate * hn).astype(o_ref.dtype)
            kz = (k_pair.astype(F32) * zeta_ref[pair]).astype(BF16)
            upd = lax.dot_general(kz, v_pair, (((0,), (0,)), ((), ())), preferred_element_type=F32)
            r_ref[b, pair] = (r_pair * gch_ref[pair]
                              + jnp.where(row < RET_QK_DIM, upd[:, :RET_V_DIM], upd[:, RET_V_DIM:]))


def _retention(qk, tokb, gates, B, S):
    C = RET_CHUNK
    H = RET_HEADS
    nc = S // C
    gamma = 1.0 - 2.0 ** (-5.0 - jnp.arange(H, dtype=F32))
    log_g = jnp.log(gamma)
    n = jnp.arange(C, dtype=F32)
    diff = n[:, None] - n[None, :]
    decay_in = jnp.where(diff[None] >= 0, jnp.exp(log_g[:, None, None] * jnp.maximum(diff, 0.0)[None]), 0.0)
    xi = jnp.exp(log_g[None, :] * (n[:, None] + 1.0))
    zeta = jnp.exp(log_g[None, :] * (C - 1.0 - n[:, None]))
    g_chunk = jnp.exp(log_g * C)
    xi_b = jnp.broadcast_to(xi.T[:, :, None], (H, C, RET_V_DIM))
    zeta_b = jnp.repeat(zeta, RET_QK_DIM, axis=1).reshape(C, H // 2, LANES).transpose(1, 0, 2)
    gch_b = jnp.broadcast_to(jnp.repeat(g_chunk, RET_QK_DIM).reshape(H // 2, LANES, 1),
                             (H // 2, LANES, RET_V_DIM))
    qk3, tok3, gate3 = (a.reshape(B, S, a.shape[-1]) for a in (qk, tokb, gates))
    out = pl.pallas_call(
        _retention_kernel,
        grid=(nc,),
        in_specs=[pl.BlockSpec((B, C, RET_QK_W), lambda i: (0, i, 0)),
                  pl.BlockSpec((B, C, RET_QK_W), lambda i: (0, i, 1)),
                  pl.BlockSpec((B, C, RET_V_W), lambda i: (0, i, 0)),
                  pl.BlockSpec((B, C, RET_V_W), lambda i: (0, i, 0)),
                  pl.BlockSpec((H, C, C), lambda i: (0, 0, 0)),
                  pl.BlockSpec((H, C, RET_V_DIM), lambda i: (0, 0, 0)),
                  pl.BlockSpec((H // 2, C, LANES), lambda i: (0, 0, 0)),
                  pl.BlockSpec((H // 2, LANES, RET_V_DIM), lambda i: (0, 0, 0))],
        out_specs=pl.BlockSpec((B, C, RET_V_W), lambda i: (0, i, 0)),
        out_shape=jax.ShapeDtypeStruct((B, S, RET_V_W), BF16),
        scratch_shapes=[pltpu.VMEM((B, H // 2, LANES, RET_V_DIM), F32)],
        compiler_params=_cparams(1),
        name="retention",
    )(qk3, qk3, tok3, gate3, decay_in, xi_b, zeta_b, gch_b)
    return out.reshape(B * S, RET_V_W)


def _layer_norm(z, g, b):
    mu = jnp.mean(z, axis=-1, keepdims=True)
    d = z - mu
    var = jnp.mean(d * d, axis=-1, keepdims=True)
    return d * lax.rsqrt(var + LN_EPS) * g + b


def _merge_kernel(x_ref, ya_ref, yr_ref, ga_ref, gr_ref, wa_ref, wr_ref, wo_ref, g_ref, b_ref,
                  x1_ref):
    a = jnp.dot(ya_ref[...], wa_ref[...], preferred_element_type=F32)
    r = jnp.dot(yr_ref[...], wr_ref[...], preferred_element_type=F32)
    h = ga_ref[...].astype(F32) * a + gr_ref[...].astype(F32) * r
    mix = jnp.dot(h.astype(BF16), wo_ref[...], preferred_element_type=F32)
    x1_ref[...] = _layer_norm(DEEPNORM_ALPHA * x_ref[...] + mix, g_ref[...], b_ref[...])


def _merge(x, ya, yr, gates, wa, wr, wo, g, b, tm=MERGE_TM):
    T, D = x.shape
    tm = min(tm, T)
    row = lambda i: (i, 0)
    fixed = lambda i: (0, 0)
    return pl.pallas_call(
        _merge_kernel,
        grid=(T // tm,),
        in_specs=[pl.BlockSpec((tm, D), row),
                  pl.BlockSpec((tm, ya.shape[1]), row),
                  pl.BlockSpec((tm, yr.shape[1]), row),
                  pl.BlockSpec((tm, D), lambda i: (i, 1)),
                  pl.BlockSpec((tm, D), lambda i: (i, 2)),
                  pl.BlockSpec(wa.shape, fixed),
                  pl.BlockSpec(wr.shape, fixed),
                  pl.BlockSpec(wo.shape, fixed),
                  pl.BlockSpec((1, D), fixed),
                  pl.BlockSpec((1, D), fixed)],
        out_specs=pl.BlockSpec((tm, D), row),
        out_shape=jax.ShapeDtypeStruct((T, D), F32),
        compiler_params=_cparams(1),
        name="merge",
    )(x, ya, yr, gates, gates, wa, wr, wo, g, b)


def _ffn_kernel(x1_ref, wu_ref, wd_ref, g_ref, b_ref, o_ref, acc_ref, xb_ref):
    f = pl.program_id(1)

    @pl.when(f == 0)
    def _():
        acc_ref[...] = jnp.zeros(acc_ref.shape, F32)
        xb_ref[...] = x1_ref[...].astype(BF16)

    hid = jnp.maximum(jnp.dot(xb_ref[...], wu_ref[...], preferred_element_type=F32), 0.0)
    acc_ref[...] += jnp.dot((hid * hid).astype(BF16), wd_ref[...], preferred_element_type=F32)

    @pl.when(f == pl.num_programs(1) - 1)
    def _():
        o_ref[...] = _layer_norm(DEEPNORM_ALPHA * x1_ref[...] + acc_ref[...], g_ref[...], b_ref[...])


def _ffn(x1, wu, wd, g, b, tm=FFN_TM, tf=FFN_TF):
    T, D = x1.shape
    F = wu.shape[1]
    tm = min(tm, T)
    return pl.pallas_call(
        _ffn_kernel,
        grid=(T // tm, F // tf),
        in_specs=[pl.BlockSpec((tm, D), lambda i, f: (i, 0)),
                  pl.BlockSpec((D, tf), lambda i, f: (0, f)),
                  pl.BlockSpec((tf, D), lambda i, f: (f, 0)),
                  pl.BlockSpec((1, D), lambda i, f: (0, 0)),
                  pl.BlockSpec((1, D), lambda i, f: (0, 0))],
        out_specs=pl.BlockSpec((tm, D), lambda i, f: (i, 0)),
        out_shape=jax.ShapeDtypeStruct((T, D), F32),
        scratch_shapes=[pltpu.VMEM((tm, D), F32), pltpu.VMEM((tm, D), BF16)],
        compiler_params=_cparams(2),
        name="ffn",
    )(x1, wu, wd, g, b)


def _rot_half_weight(wT):
    N, D = wT.shape
    half = RET_QK_DIM // 2
    wh = wT.reshape(N // RET_QK_DIM, 2, half, D)
    return jnp.stack([-wh[:, 1], wh[:, 0]], axis=1).reshape(N, D)


def kernel(x, positions, w_in, rel_bias, idx_k_ln_g, idx_k_ln_b, w_attn_branch, w_ret_branch,
           w_out, ln_mix_g, ln_mix_b, w_up, w_down, ln_ffn_g, ln_ffn_b):
    B, S, D = x.shape
    T = B * S
    sizes = (ATTN_W, ATTN_W, ATTN_W, IDX_Q_W, IDX_HEAD_DIM, IDX_HEADS,
             RET_QK_W, RET_QK_W, RET_V_W, RET_V_W, D, D)
    offs = [0] + [int(o) for o in np.cumsum(sizes)]
    cos, sin = _rope_tables(positions)
    xf = x.reshape(T, D)
    for l in range(DEPTH):
        wT = jnp.swapaxes(w_in[l], 0, 1).astype(BF16)
        rows = [wT[offs[k]:offs[k + 1]] for k in range(len(sizes))]
        (w_qa, w_ka, w_va, w_iq, w_ik, w_iw, w_qr, w_kr, w_vr, w_gr, w_ga, w_gtr) = rows
        w_kr = w_kr * (RET_QK_DIM ** -0.5)

        pad = LANES - IDX_HEAD_DIM - IDX_HEADS
        w_idx = jnp.concatenate([w_ik, w_iw, jnp.zeros((pad, D), BF16)], axis=0)
        g_pad = jnp.concatenate([idx_k_ln_g[l], jnp.zeros((LANES - IDX_HEAD_DIM,), F32)]).reshape(1, LANES)
        b_pad = jnp.concatenate([idx_k_ln_b[l], jnp.zeros((LANES - IDX_HEAD_DIM,), F32)]).reshape(1, LANES)
        idx, xb = _proj_idx(xf, w_idx, g_pad, b_pad, (IDX_HEAD_DIM ** -0.5) * (IDX_HEADS ** -0.5))
        idx = idx.reshape(B, S, LANES)
        ik = idx[:, :, :IDX_HEAD_DIM].astype(BF16)
        iwT = jnp.swapaxes(idx[:, :, IDX_HEAD_DIM:IDX_HEAD_DIM + SUBLANES], 1, 2)

        fm = _proj_t(xb, jnp.concatenate([w_qa, w_va, w_iq], axis=0), B, S, BF16,
                     scaled_rows=ATTN_W, scale=ATTN_HEAD_DIM ** -0.5 * LOG2E)
        tokb = _proj(xb, jnp.concatenate([w_vr, w_ka], axis=0), BF16)
        gates = _proj_gates(xb, jnp.concatenate([w_gr, w_ga, w_gtr], axis=0), tn=D)
        w_rope = jnp.concatenate([w_qr, w_kr], axis=0)
        w_rope_rot = jnp.concatenate([_rot_half_weight(w_qr), _rot_half_weight(w_kr)], axis=0)
        qk_r = _proj_rope(xb, w_rope, w_rope_rot, cos, sin)

        y_a = _sparse_attention(fm, tokb.reshape(B, S, -1), ik, iwT, positions, rel_bias)
        y_r = _retention(qk_r, tokb, gates, B, S)
        x1 = _merge(xf, y_a.reshape(T, ATTN_W), y_r, gates,
                    w_attn_branch[l].astype(BF16), w_ret_branch[l].astype(BF16),
                    w_out[l].astype(BF16), ln_mix_g[l].reshape(1, D), ln_mix_b[l].reshape(1, D))
        xf = _ffn(x1, w_up[l].astype(BF16), w_down[l].astype(BF16),
                  ln_ffn_g[l].reshape(1, D), ln_ffn_b[l].reshape(1, D))
    return xf.reshape(B, S, D)
```

```python
import functools
import math

import numpy as np
import jax
import jax.numpy as jnp
from jax import lax
from jax.experimental import pallas as pl
from jax.experimental.pallas import tpu as pltpu

F32 = jnp.float32
BF16 = jnp.bfloat16

ATTN_HEADS = 8
ATTN_HEAD_DIM = 64
ATTN_W = ATTN_HEADS * ATTN_HEAD_DIM
IDX_HEADS = 4
IDX_HEAD_DIM = 64
IDX_Q_W = IDX_HEADS * IDX_HEAD_DIM
TOPK_MAX = 256
RET_HEADS = 8
RET_QK_DIM = 64
RET_V_DIM = 128
RET_QK_W = RET_HEADS * RET_QK_DIM
RET_V_W = RET_HEADS * RET_V_DIM
RET_CHUNK = 128
ROPE_BASE = 10000.0
NUM_BUCKETS = 32
MAX_DISTANCE = 128
LN_EPS = 1e-5
DEPTH = 1
DEEPNORM_ALPHA = (2.0 * DEPTH) ** 0.25

LANES = 128
SUBLANES = 8
VMEM_LIMIT = 56 * 1024 * 1024

PROJ_TM = 1024
GATES_TM = 2048
MERGE_TM = 512
FFN_TM = 1024
FFN_TF = 1024

TQ = 128
CK = 512
NEG = -1e30
LOG2E = math.log2(math.e)
BISECT_ROUNDS = 20
BIAS_TABLE_N = 128
FAR_N = 113


def _cparams(n_grid):
    return pltpu.CompilerParams(
        dimension_semantics=("arbitrary",) * n_grid,
        vmem_limit_bytes=VMEM_LIMIT)


def _trig_kernel(pos_ref, inv_ref, cos_ref, sin_ref):
    ang = pos_ref[...] * inv_ref[...]
    tr = ang.shape[0]
    half = RET_QK_DIM // 2
    per_row = LANES // half
    lane = lax.broadcasted_iota(jnp.int32, ang.shape, 1)
    for ref, val in ((cos_ref, jnp.cos(ang)), (sin_ref, jnp.sin(ang))):
        for k in range(per_row):
            own = jnp.where((lane >= k * half) & (lane < (k + 1) * half), val, 0.0)
            spread = own
            for j in range(1, per_row):
                spread = spread + pltpu.roll(own, j * half, 1)
            ref[pl.ds(k, tr, stride=per_row), :] = spread


def _rope_tables(positions):
    B, S = positions.shape
    half = RET_QK_DIM // 2
    inv = ROPE_BASE ** (-jnp.arange(half, dtype=F32) / half)
    per_row = LANES // half
    rows = B * S // per_row
    pos_e = jnp.repeat(positions.astype(F32).reshape(rows, per_row), half, axis=1)
    inv_e = jnp.tile(inv, per_row).reshape(1, LANES)
    tr = min(rows, 1024)
    return pl.pallas_call(
        _trig_kernel,
        grid=(rows // tr,),
        in_specs=[pl.BlockSpec((tr, LANES), lambda i: (i, 0)),
                  pl.BlockSpec((1, LANES), lambda i: (0, 0))],
        out_specs=[pl.BlockSpec((per_row * tr, LANES), lambda i: (i, 0))] * 2,
        out_shape=[jax.ShapeDtypeStruct((B * S, LANES), F32)] * 2,
        compiler_params=_cparams(1),
        name="rope_tables",
    )(pos_e, inv_e)


def _x_wt(x, wT):
    return lax.dot_general(x, wT, (((1,), (1,)), ((), ())), preferred_element_type=F32)


def _proj_kernel(x_ref, w_ref, o_ref):
    o_ref[...] = _x_wt(x_ref[...], w_ref[...]).astype(o_ref.dtype)


def _proj(xb, wT, out_dtype, tm=PROJ_TM):
    T, D = xb.shape
    N = wT.shape[0]
    tn = N
    tm = min(tm, T)
    return pl.pallas_call(
        _proj_kernel,
        grid=(T // tm, N // tn),
        in_specs=[pl.BlockSpec((tm, D), lambda i, j: (i, 0)),
                  pl.BlockSpec((tn, D), lambda i, j: (j, 0))],
        out_specs=pl.BlockSpec((tm, tn), lambda i, j: (i, j)),
        out_shape=jax.ShapeDtypeStruct((T, N), out_dtype),
        compiler_params=_cparams(2),
        name="proj",
    )(xb, wT)


def _proj_gates_kernel(x_ref, w_ref, o_ref):
    acc = _x_wt(x_ref[...], w_ref[...])
    sig = 0.5 * jnp.tanh(0.5 * acc) + 0.5
    o_ref[...] = jnp.where(pl.program_id(1) == 0, acc * sig, sig).astype(o_ref.dtype)


def _proj_gates(xb, wT, tn, tm=GATES_TM):
    T, D = xb.shape
    N = wT.shape[0]
    tm = min(tm, T)
    return pl.pallas_call(
        _proj_gates_kernel,
        grid=(T // tm, N // tn),
        in_specs=[pl.BlockSpec((tm, D), lambda i, j: (i, 0)),
                  pl.BlockSpec((tn, D), lambda i, j: (j, 0))],
        out_specs=pl.BlockSpec((tm, tn), lambda i, j: (i, j)),
        out_shape=jax.ShapeDtypeStruct((T, N), BF16),
        compiler_params=_cparams(2),
        name="proj_gates",
    )(xb, wT)


def _proj_t_kernel(wT_ref, x_ref, o_ref, *, scaled_rows, scale):
    acc = lax.dot_general(wT_ref[...], x_ref[...], (((1,), (1,)), ((), ())), preferred_element_type=F32)
    o_ref[:scaled_rows, :] = (acc[:scaled_rows] * scale).astype(o_ref.dtype)
    o_ref[scaled_rows:, :] = acc[scaled_rows:].astype(o_ref.dtype)


def _proj_t(xb, wT, B, S, out_dtype, scaled_rows, scale, tm=PROJ_TM):
    T, D = xb.shape
    N = wT.shape[0]
    tn = N
    tm = min(tm, S)
    nsb = S // tm
    return pl.pallas_call(
        functools.partial(_proj_t_kernel, scaled_rows=scaled_rows, scale=scale),
        grid=(T // tm, N // tn),
        in_specs=[pl.BlockSpec((tn, D), lambda i, j: (j, 0)),
                  pl.BlockSpec((tm, D), lambda i, j: (i, 0))],
        out_specs=pl.BlockSpec((pl.Squeezed(), tn, tm), lambda i, j: (i // nsb, j, i % nsb)),
        out_shape=jax.ShapeDtypeStruct((B, N, S), out_dtype),
        compiler_params=_cparams(2),
        name="proj_t",
    )(wT, xb)


def _proj_rope_kernel(x_ref, w_ref, wr_ref, cos_ref, sin_ref, o_ref):
    x = x_ref[...]
    a = _x_wt(x, w_ref[...])
    r = _x_wt(x, wr_ref[...])
    reps = a.shape[1] // LANES
    cos = jnp.concatenate([cos_ref[...]] * reps, axis=1)
    sin = jnp.concatenate([sin_ref[...]] * reps, axis=1)
    o_ref[...] = (a * cos + r * sin).astype(o_ref.dtype)


def _proj_rope(xb, wT, wT_rot, cos, sin, tm=PROJ_TM):
    T, D = xb.shape
    N = wT.shape[0]
    tn = N
    tm = min(tm, T)
    return pl.pallas_call(
        _proj_rope_kernel,
        grid=(T // tm, N // tn),
        in_specs=[pl.BlockSpec((tm, D), lambda i, j: (i, 0)),
                  pl.BlockSpec((tn, D), lambda i, j: (j, 0)),
                  pl.BlockSpec((tn, D), lambda i, j: (j, 0)),
                  pl.BlockSpec((tm, LANES), lambda i, j: (i, 0)),
                  pl.BlockSpec((tm, LANES), lambda i, j: (i, 0))],
        out_specs=pl.BlockSpec((tm, tn), lambda i, j: (i, j)),
        out_shape=jax.ShapeDtypeStruct((T, N), BF16),
        compiler_params=_cparams(2),
        name="proj_rope",
    )(xb, wT, wT_rot, cos, sin)


def _proj_idx_kernel(x_ref, w_ref, g_ref, b_ref, o_ref, xb_ref, *, iw_scale):
    xb = x_ref[...].astype(BF16)
    xb_ref[...] = xb
    acc = _x_wt(xb, w_ref[...])
    lane = lax.broadcasted_iota(jnp.int32, acc.shape, 1)
    is_k = lane < IDX_HEAD_DIM
    mu = jnp.sum(jnp.where(is_k, acc, 0.0), axis=-1, keepdims=True) / IDX_HEAD_DIM
    d = acc - mu
    var = jnp.sum(jnp.where(is_k, d * d, 0.0), axis=-1, keepdims=True) / IDX_HEAD_DIM
    ln = d * lax.rsqrt(var + LN_EPS) * g_ref[...] + b_ref[...]
    o_ref[...] = jnp.where(is_k, ln, acc * iw_scale)


def _proj_idx(x, w_pad, g_pad, b_pad, iw_scale, tm=PROJ_TM):
    T, D = x.shape
    tm = min(tm, T)
    return pl.pallas_call(
        functools.partial(_proj_idx_kernel, iw_scale=iw_scale),
        grid=(T // tm,),
        in_specs=[pl.BlockSpec((tm, D), lambda i: (i, 0)),
                  pl.BlockSpec((LANES, D), lambda i: (0, 0)),
                  pl.BlockSpec((1, LANES), lambda i: (0, 0)),
                  pl.BlockSpec((1, LANES), lambda i: (0, 0))],
        out_specs=[pl.BlockSpec((tm, LANES), lambda i: (i, 0)), pl.BlockSpec((tm, D), lambda i: (i, 0))],
        out_shape=[jax.ShapeDtypeStruct((T, LANES), F32), jax.ShapeDtypeStruct((T, D), BF16)],
        compiler_params=_cparams(1),
        name="proj_idx",
    )(x, w_pad, g_pad, b_pad)


def _t5_bucket_table():
    n = np.arange(BIAS_TABLE_N)
    max_exact = NUM_BUCKETS // 2
    nf = np.maximum(n, 1).astype(np.float64)
    large = max_exact + (np.log(nf / max_exact) / math.log(MAX_DISTANCE / max_exact)
                         * (NUM_BUCKETS - max_exact)).astype(np.int64)
    large = np.minimum(large, NUM_BUCKETS - 1)
    bucket = np.where(n < max_exact, n, large)
    assert np.all(bucket[FAR_N:] == NUM_BUCKETS - 1) and bucket[FAR_N - 1] != NUM_BUCKETS - 1
    return bucket.astype(np.int32)


def _fold_rows(a, op):
    parts = [a[r:r + SUBLANES] for r in range(0, a.shape[0], SUBLANES)]
    while len(parts) > 1:
        nxt = [op(parts[k], parts[k + 1]) for k in range(0, len(parts) - 1, 2)]
        if len(parts) % 2:
            nxt.append(parts[-1])
        parts = nxt
    return parts[0]


def _attn_kernel(biasc_ref, pinfo_ref, ckmax_ref, iqT_ref, iwT_ref, posqc_ref, posk_ref, ik_ref, qT_ref, k_ref, vT_ref,
                 tbl_ref, toep_ref, tril_ref, o_ref, sc_ref, s_ref, s2_ref, b_ref, qm_ref, m_ref, l_ref, *acc_refs, k_sel, seq):
    i = pl.program_id(1)
    nch = (i * TQ + TQ + CK - 1) // CK
    q_idx = i * TQ + lax.broadcasted_iota(jnp.int32, (1, TQ), 1)
    kf = float(k_sel)

    def chunk_off(c):
        return pl.multiple_of(c * CK, CK)

    def key_idx(off):
        return off + lax.broadcasted_iota(jnp.int32, (CK, TQ), 0)

    def col_reduce(part, op):
        return op(part, axis=0, keepdims=True)

    iqT = iqT_ref[...]
    iwT = iwT_ref[...]
    iq_wide = jnp.concatenate([iqT[h * IDX_HEAD_DIM:(h + 1) * IDX_HEAD_DIM, :] for h in range(IDX_HEADS)], axis=1)

    def score_matmul(c, z_buf):
        ikc = ik_ref[pl.ds(chunk_off(c), CK), :]
        z = jnp.dot(ikc, iq_wide, preferred_element_type=F32)
        for h in range(IDX_HEADS):
            z_buf[h] = z[:, h * TQ:(h + 1) * TQ]

    def score_finish(c, z_buf, carry, masked):
        mn, mx = carry
        off = chunk_off(c)
        s = None
        for h in range(IDX_HEADS):
            t = jnp.maximum(z_buf[h], 0.0) * iwT[h:h + 1, :]
            s = t if s is None else s + t
        if masked:
            causal = key_idx(off) <= q_idx
            s_lo = jnp.where(causal, s, -jnp.inf)
            s_hi = jnp.where(causal, s, jnp.inf)
        else:
            s_lo = s_hi = s
        sc_ref[pl.ds(off, CK), :] = s_lo
        mn = jnp.minimum(mn, _fold_rows(s_hi, jnp.minimum))
        mx = jnp.maximum(mx, _fold_rows(s_lo, jnp.maximum))
        return mn, mx

    def score_chunk(c, carry, masked):
        score_matmul(c, s_ref)
        return score_finish(c, s_ref, carry, masked)

    n_inner = nch - 1
    n_pairs_sc = n_inner // 2

    @pl.when(n_pairs_sc > 0)
    def _():
        score_matmul(0, s_ref)

    def score_step(t, carry):
        c = 2 * t
        score_matmul(c + 1, s2_ref)
        carry = score_finish(c, s_ref, carry, False)
        score_matmul(jnp.minimum(c + 2, 2 * n_pairs_sc - 2), s_ref)
        return score_finish(c + 1, s2_ref, carry, False)

    mn8, mx8 = lax.fori_loop(0, n_pairs_sc, score_step,
                             (jnp.full((SUBLANES, TQ), jnp.inf, F32), jnp.full((SUBLANES, TQ), -jnp.inf, F32)))
    mn8, mx8 = lax.fori_loop(2 * n_pairs_sc, n_inner, functools.partial(score_chunk, masked=False), (mn8, mx8))
    mn8, mx8 = score_chunk(nch - 1, (mn8, mx8), masked=True)
    mn = col_reduce(mn8, jnp.min)
    mx = col_reduce(mx8, jnp.max)

    def over_chunks(fn, init):
        def one(c, carry):
            return fn(sc_ref[pl.ds(chunk_off(c), CK), :], carry)

        carry = lax.fori_loop(0, nch // 2, lambda t, carry: one(2 * t + 1, one(2 * t, carry)), init)
        return lax.fori_loop(2 * (nch // 2), nch, one, carry)

    def count(pred_fn):
        acc = over_chunks(lambda blk, acc: acc + _fold_rows(jnp.where(pred_fn(blk), 1.0, 0.0), jnp.add),
                          jnp.zeros((SUBLANES, TQ), F32))
        return col_reduce(acc, jnp.sum)

    def bisect_round(_, st):
        lo, hi, c_lo = st
        mid = 0.5 * (lo + hi)
        c = count(lambda blk: blk >= mid)
        ok = c >= kf
        return jnp.where(ok, mid, lo), jnp.where(ok, hi, mid), jnp.where(ok, c, c_lo)

    c_all = (q_idx + 1).astype(F32)
    lo, hi, c_lo = lax.fori_loop(0, BISECT_ROUNDS, bisect_round, (mn, mx, c_all))

    cur0 = col_reduce(over_chunks(
        lambda blk, acc: jnp.minimum(acc, _fold_rows(jnp.where(blk >= lo, blk, jnp.inf), jnp.minimum)),
        jnp.full((SUBLANES, TQ), jnp.inf, F32)), jnp.min)

    def walk_cond(st):
        return st[3] > 0.0

    def walk_body(st):
        cur, c_ge, _, _ = st

        def body(blk, carry):
            cnt, nxt = carry
            gt = blk > cur
            cnt = cnt + _fold_rows(jnp.where(gt, 1.0, 0.0), jnp.add)
            nxt = jnp.minimum(nxt, _fold_rows(jnp.where(gt, blk, jnp.inf), jnp.minimum))
            return cnt, nxt

        cnt, nxt = over_chunks(body, (jnp.zeros((SUBLANES, TQ), F32), jnp.full((SUBLANES, TQ), jnp.inf, F32)))
        c_gt = col_reduce(cnt, jnp.sum)
        nxt = col_reduce(nxt, jnp.min)
        adv = c_gt >= kf
        cur = jnp.where(adv, nxt, cur)
        c_ge = jnp.where(adv, c_gt, c_ge)
        return cur, c_ge, c_gt, jnp.max(jnp.where(adv, 1.0, 0.0))

    tau, c_ge, c_gt, _ = lax.while_loop(
        walk_cond, walk_body, (cur0, c_lo, jnp.zeros((1, TQ), F32), jnp.float32(1.0)))

    room = kf - c_gt

    def selection_mask(c, seen):
        off = chunk_off(c)
        tiles = range(CK // LANES)
        blks = [sc_ref[pl.ds(off + j * LANES, LANES), :] for j in tiles]
        ties = [blk == tau for blk in blks]
        local = [jnp.dot(tril_ref[...], jnp.where(tie, 1.0, 0.0).astype(BF16), preferred_element_type=F32)
                 for tie in ties]
        masks = []
        for j in tiles:
            keep = (blks[j] > tau) | (ties[j] & (local[j] + seen <= room))
            masks.append(jnp.where(keep, 0.0, NEG))
            seen = seen + local[j][LANES - 1:LANES, :]
        return jnp.concatenate(masks, axis=0), seen

    m_ref[...] = jnp.full(m_ref.shape, NEG, F32)
    l_ref[...] = jnp.zeros(l_ref.shape, F32)
    for acc in acc_refs:
        acc[...] = jnp.zeros(acc.shape, F32)

    rowi = lax.broadcasted_iota(jnp.int32, (LANES, TQ), 0)
    for pair in range(ATTN_HEADS // 2):
        qp = qT_ref[pair * LANES:(pair + 1) * LANES, :]
        zero = jnp.zeros_like(qp)
        qm_ref[pair] = jnp.concatenate([jnp.where(rowi < ATTN_HEAD_DIM, qp, zero),
                                        jnp.where(rowi >= ATTN_HEAD_DIM, qp, zero)], axis=1)
    ones_rows = jnp.ones((2 * SUBLANES, CK), BF16)

    pq_col = posqc_ref[...]
    batch = pl.program_id(0)
    pq_first, pq_consec, pq_min = (pinfo_ref[batch, r, i] for r in (0, 1, 3))
    far_bias = [biasc_ref[0, h] for h in range(ATTN_HEADS)]
    zero_bias = [biasc_ref[1, h] for h in range(ATTN_HEADS)]
    q_ge_k = (lax.broadcasted_iota(jnp.int32, (LANES, TQ), 1) >= lax.broadcasted_iota(jnp.int32, (LANES, TQ), 0))
    n_sub = CK // LANES
    n_pairs = ATTN_HEADS // 2

    def chunk_is_far(c):
        return (pq_min - ckmax_ref[batch, jnp.minimum(c, seq // CK - 1)]) >= FAR_N

    def stage_bias(c):
        off = chunk_off(c)
        pk_row = posk_ref[:, pl.ds(off, CK)]
        for j in range(n_sub):
            rows = slice(j * LANES, (j + 1) * LANES)
            g = c * n_sub + j
            pk_first, pk_consec, pk_max = (pinfo_ref[batch, r, g] for r in (0, 1, 2))
            all_far = (pq_min - pk_max) >= FAR_N
            all_masked = (off + j * LANES) > (i * TQ + TQ - 1)
            consecutive = (pq_consec > 0) & (pk_consec > 0)
            gap = pq_first - pk_first

            def fill_const(rows=rows):
                for h in range(ATTN_HEADS):
                    b_ref[h, rows, :] = jnp.full((LANES, TQ), far_bias[h], F32)

            def fill_gap0(rows=rows):
                for h in range(ATTN_HEADS):
                    b_ref[h, rows, :] = jnp.where(q_ge_k, toep_ref[h], zero_bias[h])

            def fill_gap128(rows=rows):
                for h in range(ATTN_HEADS):
                    b_ref[h, rows, :] = jnp.where(q_ge_k, far_bias[h], toep_ref[h])

            def fill_lookup(rows=rows):
                pk_sub = pk_row[:, rows]
                n_qk = jnp.clip(pq_col - pk_sub, 0, BIAS_TABLE_N - 1).astype(F32)
                n_kq = n_qk.T.astype(jnp.int32)
                for h in range(ATTN_HEADS):
                    tb = jnp.broadcast_to(tbl_ref[h:h + 1, :], (LANES, BIAS_TABLE_N))
                    b_ref[h, rows, :] = jnp.take_along_axis(tb, n_kq, axis=1)

            def fill_near(fill_gap0=fill_gap0, fill_gap128=fill_gap128, fill_lookup=fill_lookup,
                          consecutive=consecutive, gap=gap):
                lax.cond(consecutive & (gap == 0), fill_gap0,
                         lambda: lax.cond(consecutive & (gap == LANES), fill_gap128, fill_lookup))

            lax.cond(all_far | all_masked, fill_const, fill_near)

    def logits_phase(c, s_buf, const_bias, seen):
        off = chunk_off(c)
        mb, seen = selection_mask(c, seen)
        m_cur = []
        for pair in range(n_pairs):
            kc = k_ref[pl.ds(off, CK), pair * LANES:(pair + 1) * LANES]
            s2 = jnp.dot(kc, qm_ref[pair], preferred_element_type=F32)
            for sub, h in enumerate((2 * pair, 2 * pair + 1)):
                s = s2[:, sub * TQ:(sub + 1) * TQ] + mb
                if not const_bias:
                    s = s + b_ref[h]
                s_buf[h] = s
                top = col_reduce(_fold_rows(s, jnp.maximum), jnp.max)
                m_cur.append(top + far_bias[h] if const_bias else top)
        return m_cur, seen

    def update_phase(c, s_buf, m_cur, const_bias):
        off = chunk_off(c)
        m_prev = m_ref[...]
        l_prev = l_ref[...]
        if isinstance(m_cur, list):
            m_new = [jnp.maximum(m_prev[h:h + 1, :], m_cur[h]) for h in range(ATTN_HEADS)]
            alpha = [jnp.exp2(m_prev[h:h + 1, :] - m_new[h]) for h in range(ATTN_HEADS)]
        else:
            m_all = jnp.maximum(m_prev, m_cur)
            a_all = jnp.exp2(m_prev - m_all)
            m_new = [m_all[h:h + 1, :] for h in range(ATTN_HEADS)]
            alpha = [a_all[h:h + 1, :] for h in range(ATTN_HEADS)]
        l_new = []
        for pair in range(n_pairs):
            heads = (2 * pair, 2 * pair + 1)
            sub_m = [m_new[h] - far_bias[h] if const_bias else m_new[h] for h in heads]
            p2 = jnp.concatenate([jnp.exp2(s_buf[h] - sm).astype(BF16) for h, sm in zip(heads, sub_m)],
                                 axis=1)
            lhs = jnp.concatenate([vT_ref[pair * LANES:(pair + 1) * LANES, pl.ds(off, CK)], ones_rows],
                                  axis=0)
            out = jnp.dot(lhs, p2, preferred_element_type=F32)
            for sub, h in enumerate(heads):
                cols = slice(sub * TQ, (sub + 1) * TQ)
                acc = acc_refs[h]
                acc[...] = alpha[h] * acc[...] + out[sub * ATTN_HEAD_DIM:(sub + 1) * ATTN_HEAD_DIM, cols]
                l_new.append(alpha[h] * l_prev[h:h + 1, :] + out[LANES:LANES + 1, cols])
        l_ref[...] = jnp.concatenate(l_new, axis=0)
        m_ref[...] = jnp.concatenate(m_new, axis=0)

    n_far = lax.while_loop(lambda c: (c < nch) & chunk_is_far(c), lambda c: c + 1, jnp.int32(0))
    n_steps = n_far // 2
    def far_logits(c, s_buf, seen):
        m_cur, seen = logits_phase(c, s_buf, True, seen)
        return jnp.concatenate(m_cur, axis=0), seen

    no_ties = jnp.zeros((1, TQ), F32)
    m_first, seen_first = lax.cond(n_steps > 0, lambda: far_logits(0, s_ref, no_ties),
                                   lambda: (jnp.zeros((ATTN_HEADS, TQ), F32), no_ties))

    def far_step(t, carry):
        m_even, seen_even, _ = carry
        c = 2 * t
        m_odd, seen_odd = far_logits(c + 1, s2_ref, seen_even)
        update_phase(c, s_ref, m_even, True)
        m_even, seen_even = far_logits(jnp.minimum(c + 2, n_far - 1), s_ref, seen_odd)
        update_phase(c + 1, s2_ref, m_odd, True)
        return m_even, seen_even, seen_odd

    m_even, seen_even, seen = lax.fori_loop(0, n_steps, far_step, (m_first, seen_first, no_ties))
    leftover_staged = (n_steps > 0) & (n_far % 2 == 1)

    def finish_leftover():
        update_phase(n_far - 1, s_ref, m_even, True)
        return seen_even

    seen = lax.cond(leftover_staged, finish_leftover, lambda: seen)
    n_done = jnp.where(leftover_staged, n_far, 2 * n_steps)

    def tail_chunk(c, seen):
        def run(const_bias):
            if not const_bias:
                stage_bias(c)
            m_cur, seen_out = logits_phase(c, s_ref, const_bias, seen)
            update_phase(c, s_ref, m_cur, const_bias)
            return seen_out

        return lax.cond(chunk_is_far(c), lambda: run(True), lambda: run(False))

    lax.fori_loop(n_done, nch, tail_chunk, seen)

    outT = jnp.concatenate([acc_refs[h][...] / l_ref[h:h + 1, :] for h in range(ATTN_HEADS)], axis=0)
    o_ref[...] = outT.T.astype(o_ref.dtype)


def _toeplitz_kernel(tbl_ref, o_ref):
    q = lax.broadcasted_iota(jnp.int32, (LANES, LANES), 1)
    k = lax.broadcasted_iota(jnp.int32, (LANES, LANES), 0)
    idx = (q - k) & (BIAS_TABLE_N - 1)
    for h in range(ATTN_HEADS):
        tb = jnp.broadcast_to(tbl_ref[h:h + 1, :], (LANES, BIAS_TABLE_N))
        o_ref[h] = jnp.take_along_axis(tb, idx, axis=1)


def _sparse_attention(fm, tokb, ik, iwT, positions, rel_bias):
    B, S, _ = tokb.shape
    k_sel = min(TOPK_MAX, S // 4)
    bucket = _t5_bucket_table()
    tbl = rel_bias[bucket].T.astype(F32) * LOG2E
    bias_c = jnp.stack([tbl[:, BIAS_TABLE_N - 1], tbl[:, 0]])
    toep = pl.pallas_call(
        _toeplitz_kernel,
        out_shape=jax.ShapeDtypeStruct((ATTN_HEADS, LANES, LANES), F32),
        name="bias_toeplitz",
    )(tbl)
    pos_row = positions.reshape(B, 1, S)
    pos_col = positions.reshape(B, S, 1)
    pos_t = positions.reshape(B, S // LANES, LANES)
    consec = jnp.all(pos_t - pos_t[:, :, :1] == jnp.arange(LANES, dtype=positions.dtype), axis=-1)
    pinfo = jnp.stack([pos_t[:, :, 0], consec.astype(jnp.int32), jnp.max(pos_t, axis=-1),
                       jnp.min(pos_t, axis=-1)], axis=1).astype(jnp.int32)
    ck_max = jnp.max(positions.reshape(B, S // CK, CK), axis=-1)
    tril = jnp.tril(jnp.ones((LANES, LANES), BF16))
    sq = pl.Squeezed()
    iq_blk = (2 * ATTN_W) // IDX_Q_W
    k_blk = (tokb.shape[2] - ATTN_W) // ATTN_W
    return pl.pallas_call(
        functools.partial(_attn_kernel, k_sel=k_sel, seq=S),
        grid=(B, S // TQ),
        in_specs=[pl.BlockSpec(memory_space=pltpu.SMEM),
                  pl.BlockSpec(memory_space=pltpu.SMEM),
                  pl.BlockSpec(memory_space=pltpu.SMEM),
                  pl.BlockSpec((sq, IDX_Q_W, TQ), lambda b, i: (b, iq_blk, i)),
                  pl.BlockSpec((sq, SUBLANES, TQ), lambda b, i: (b, 0, i)),
                  pl.BlockSpec((sq, TQ, 1), lambda b, i: (b, i, 0)),
                  pl.BlockSpec((sq, 1, S), lambda b, i: (b, 0, 0)),
                  pl.BlockSpec((sq, S, IDX_HEAD_DIM), lambda b, i: (b, 0, 0)),
                  pl.BlockSpec((sq, ATTN_W, TQ), lambda b, i: (b, 0, i)),
                  pl.BlockSpec((sq, S, ATTN_W), lambda b, i: (b, 0, k_blk)),
                  pl.BlockSpec((sq, ATTN_W, S), lambda b, i: (b, 1, 0)),
                  pl.BlockSpec((ATTN_HEADS, BIAS_TABLE_N), lambda b, i: (0, 0)),
                  pl.BlockSpec((ATTN_HEADS, LANES, LANES), lambda b, i: (0, 0, 0)),
                  pl.BlockSpec((LANES, LANES), lambda b, i: (0, 0))],
        out_specs=pl.BlockSpec((sq, TQ, ATTN_W), lambda b, i: (b, i, 0)),
        out_shape=jax.ShapeDtypeStruct((B, S, ATTN_W), BF16),
        scratch_shapes=[pltpu.VMEM((S, TQ), F32),
                        pltpu.VMEM((ATTN_HEADS, CK, TQ), F32),
                        pltpu.VMEM((ATTN_HEADS, CK, TQ), F32),
                        pltpu.VMEM((ATTN_HEADS, CK, TQ), F32),
                        pltpu.VMEM((ATTN_HEADS // 2, LANES, 2 * TQ), BF16),
                        pltpu.VMEM((ATTN_HEADS, TQ), F32),
                        pltpu.VMEM((ATTN_HEADS, TQ), F32)]
                       + [pltpu.VMEM((ATTN_HEAD_DIM, TQ), F32)] * ATTN_HEADS,
        compiler_params=_cparams(2),
        name="sparse_attention",
    )(bias_c, pinfo, ck_max, fm, iwT, pos_col, pos_row, ik, fm, tokb, fm, tbl, toep, tril)


def _retention_kernel(q_ref, k_ref, v_ref, g_ref, decay_ref, xi_ref, zeta_ref, gch_ref, o_ref, r_ref):
    @pl.when(pl.program_id(0) == 0)
    def _():
        r_ref[...] = jnp.zeros(r_ref.shape, F32)

    C = RET_CHUNK
    lane = lax.broadcasted_iota(jnp.int32, (C, LANES), 1)
    row = lax.broadcasted_iota(jnp.int32, (LANES, RET_V_DIM), 0)
    for pair in range(RET_HEADS // 2):
        for b in range(q_ref.shape[0]):
            q_pair = q_ref[b, :, pair * LANES:(pair + 1) * LANES]
            k_pair = k_ref[b, :, pair * LANES:(pair + 1) * LANES]
            v_pair = v_ref[b, :, 2 * pair * RET_V_DIM:(2 * pair + 2) * RET_V_DIM]
            r_pair = r_ref[b, pair]
            r_bf = r_pair.astype(BF16)
            for sub in range(2):
                h = 2 * pair + sub
                in_head = (lane >= sub * RET_QK_DIM) & (lane < (sub + 1) * RET_QK_DIM)
                qm = jnp.where(in_head, q_pair, jnp.zeros_like(q_pair))
                v_h = v_pair[:, sub * RET_V_DIM:(sub + 1) * RET_V_DIM]
                inner = lax.dot_general(qm, k_pair, (((1,), (1,)), ((), ())),
                                        preferred_element_type=F32) * decay_ref[h]
                o = (jnp.dot(inner.astype(BF16), v_h, preferred_element_type=F32)
                     + jnp.dot(qm, r_bf, preferred_element_type=F32) * xi_ref[h])
                mu = jnp.mean(o, axis=-1, keepdims=True)
                d = o - mu
                var = jnp.mean(d * d, axis=-1, keepdims=True)
                hn = d * lax.rsqrt(var + LN_EPS)
                gate = g_ref[b, :, h * RET_V_DIM:(h + 1) * RET_V_DIM].astype(F32)
                o_ref[b, :, h * RET_V_DIM:(h + 1) * RET_V_DIM] = (gate * hn).astype(o_ref.dtype)
            kz = (k_pair.astype(F32) * zeta_ref[pair]).astype(BF16)
            upd = lax.dot_general(kz, v_pair, (((0,), (0,)), ((), ())), preferred_element_type=F32)
            r_ref[b, pair] = (r_pair * gch_ref[pair]
                              + jnp.where(row < RET_QK_DIM, upd[:, :RET_V_DIM], upd[:, RET_V_DIM:]))


def _retention(qk, tokb, gates, B, S):
    C = RET_CHUNK
    H = RET_HEADS
    nc = S // C
    gamma = 1.0 - 2.0 ** (-5.0 - jnp.arange(H, dtype=F32))
    log_g = jnp.log(gamma)
    n = jnp.arange(C, dtype=F32)
    diff = n[:, None] - n[None, :]
    decay_in = jnp.where(diff[None] >= 0, jnp.exp(log_g[:, None, None] * jnp.maximum(diff, 0.0)[None]), 0.0)
    xi = jnp.exp(log_g[None, :] * (n[:, None] + 1.0))
    zeta = jnp.exp(log_g[None, :] * (C - 1.0 - n[:, None]))
    g_chunk = jnp.exp(log_g * C)
    xi_b = jnp.broadcast_to(xi.T[:, :, None], (H, C, RET_V_DIM))
    zeta_b = jnp.repeat(zeta, RET_QK_DIM, axis=1).reshape(C, H // 2, LANES).transpose(1, 0, 2)
    gch_b = jnp.broadcast_to(jnp.repeat(g_chunk, RET_QK_DIM).reshape(H // 2, LANES, 1),
                             (H // 2, LANES, RET_V_DIM))
    qk3, tok3, gate3 = (a.reshape(B, S, a.shape[-1]) for a in (qk, tokb, gates))
    out = pl.pallas_call(
        _retention_kernel,
        grid=(nc,),
        in_specs=[pl.BlockSpec((B, C, RET_QK_W), lambda i: (0, i, 0)),
                  pl.BlockSpec((B, C, RET_QK_W), lambda i: (0, i, 1)),
                  pl.BlockSpec((B, C, RET_V_W), lambda i: (0, i, 0)),
                  pl.BlockSpec((B, C, RET_V_W), lambda i: (0, i, 0)),
                  pl.BlockSpec((H, C, C), lambda i: (0, 0, 0)),
                  pl.BlockSpec((H, C, RET_V_DIM), lambda i: (0, 0, 0)),
                  pl.BlockSpec((H // 2, C, LANES), lambda i: (0, 0, 0)),
                  pl.BlockSpec((H // 2, LANES, RET_V_DIM), lambda i: (0, 0, 0))],
        out_specs=pl.BlockSpec((B, C, RET_V_W), lambda i: (0, i, 0)),
        out_shape=jax.ShapeDtypeStruct((B, S, RET_V_W), BF16),
        scratch_shapes=[pltpu.VMEM((B, H // 2, LANES, RET_V_DIM), F32)],
        compiler_params=_cparams(1),
        name="retention",
    )(qk3, qk3, tok3, gate3, decay_in, xi_b, zeta_b, gch_b)
    return out.reshape(B * S, RET_V_W)


def _layer_norm(z, g, b):
    mu = jnp.mean(z, axis=-1, keepdims=True)
    d = z - mu
    var = jnp.mean(d * d, axis=-1, keepdims=True)
    return d * lax.rsqrt(var + LN_EPS) * g + b


def _merge_kernel(x_ref, ya_ref, yr_ref, ga_ref, gr_ref, wa_ref, wr_ref, wo_ref, g_ref, b_ref,
                  x1_ref):
    a = jnp.dot(ya_ref[...], wa_ref[...], preferred_element_type=F32)
    r = jnp.dot(yr_ref[...], wr_ref[...], preferred_element_type=F32)
    h = ga_ref[...].astype(F32) * a + gr_ref[...].astype(F32) * r
    mix = jnp.dot(h.astype(BF16), wo_ref[...], preferred_element_type=F32)
    x1_ref[...] = _layer_norm(DEEPNORM_ALPHA * x_ref[...] + mix, g_ref[...], b_ref[...])


def _merge(x, ya, yr, gates, wa, wr, wo, g, b, tm=MERGE_TM):
    T, D = x.shape
    tm = min(tm, T)
    row = lambda i: (i, 0)
    fixed = lambda i: (0, 0)
    return pl.pallas_call(
        _merge_kernel,
        grid=(T // tm,),
        in_specs=[pl.BlockSpec((tm, D), row),
                  pl.BlockSpec((tm, ya.shape[1]), row),
                  pl.BlockSpec((tm, yr.shape[1]), row),
                  pl.BlockSpec((tm, D), lambda i: (i, 1)),
                  pl.BlockSpec((tm, D), lambda i: (i, 2)),
                  pl.BlockSpec(wa.shape, fixed),
                  pl.BlockSpec(wr.shape, fixed),
                  pl.BlockSpec(wo.shape, fixed),
                  pl.BlockSpec((1, D), fixed),
                  pl.BlockSpec((1, D), fixed)],
        out_specs=pl.BlockSpec((tm, D), row),
        out_shape=jax.ShapeDtypeStruct((T, D), F32),
        compiler_params=_cparams(1),
        name="merge",
    )(x, ya, yr, gates, gates, wa, wr, wo, g, b)


def _ffn_kernel(x1_ref, wu_ref, wd_ref, g_ref, b_ref, o_ref, acc_ref, xb_ref):
    f = pl.program_id(1)

    @pl.when(f == 0)
    def _():
        acc_ref[...] = jnp.zeros(acc_ref.shape, F32)
        xb_ref[...] = x1_ref[...].astype(BF16)

    hid = jnp.maximum(jnp.dot(xb_ref[...], wu_ref[...], preferred_element_type=F32), 0.0)
    acc_ref[...] += jnp.dot((hid * hid).astype(BF16), wd_ref[...], preferred_element_type=F32)

    @pl.when(f == pl.num_programs(1) - 1)
    def _():
        o_ref[...] = _layer_norm(DEEPNORM_ALPHA * x1_ref[...] + acc_ref[...], g_ref[...], b_ref[...])


def _ffn(x1, wu, wd, g, b, tm=FFN_TM, tf=FFN_TF):
    T, D = x1.shape
    F = wu.shape[1]
    tm = min(tm, T)
    return pl.pallas_call(
        _ffn_kernel,
        grid=(T // tm, F // tf),
        in_specs=[pl.BlockSpec((tm, D), lambda i, f: (i, 0)),
                  pl.BlockSpec((D, tf), lambda i, f: (0, f)),
                  pl.BlockSpec((tf, D), lambda i, f: (f, 0)),
                  pl.BlockSpec((1, D), lambda i, f: (0, 0)),
                  pl.BlockSpec((1, D), lambda i, f: (0, 0))],
        out_specs=pl.BlockSpec((tm, D), lambda i, f: (i, 0)),
        out_shape=jax.ShapeDtypeStruct((T, D), F32),
        scratch_shapes=[pltpu.VMEM((tm, D), F32), pltpu.VMEM((tm, D), BF16)],
        compiler_params=_cparams(2),
        name="ffn",
    )(x1, wu, wd, g, b)


def _rot_half_weight(wT):
    N, D = wT.shape
    half = RET_QK_DIM // 2
    wh = wT.reshape(N // RET_QK_DIM, 2, half, D)
    return jnp.stack([-wh[:, 1], wh[:, 0]], axis=1).reshape(N, D)


def kernel(x, positions, w_in, rel_bias, idx_k_ln_g, idx_k_ln_b, w_attn_branch, w_ret_branch,
           w_out, ln_mix_g, ln_mix_b, w_up, w_down, ln_ffn_g, ln_ffn_b):
    B, S, D = x.shape
    T = B * S
    sizes = (ATTN_W, ATTN_W, ATTN_W, IDX_Q_W, IDX_HEAD_DIM, IDX_HEADS,
             RET_QK_W, RET_QK_W, RET_V_W, RET_V_W, D, D)
    offs = [0] + [int(o) for o in np.cumsum(sizes)]
    cos, sin = _rope_tables(positions)
    xf = x.reshape(T, D)
    for l in range(DEPTH):
        wT = jnp.swapaxes(w_in[l], 0, 1).astype(BF16)
        rows = [wT[offs[k]:offs[k + 1]] for k in range(len(sizes))]
        (w_qa, w_ka, w_va, w_iq, w_ik, w_iw, w_qr, w_kr, w_vr, w_gr, w_ga, w_gtr) = rows
        w_kr = w_kr * (RET_QK_DIM ** -0.5)

        pad = LANES - IDX_HEAD_DIM - IDX_HEADS
        w_idx = jnp.concatenate([w_ik, w_iw, jnp.zeros((pad, D), BF16)], axis=0)
        g_pad = jnp.concatenate([idx_k_ln_g[l], jnp.zeros((LANES - IDX_HEAD_DIM,), F32)]).reshape(1, LANES)
        b_pad = jnp.concatenate([idx_k_ln_b[l], jnp.zeros((LANES - IDX_HEAD_DIM,), F32)]).reshape(1, LANES)
        idx, xb = _proj_idx(xf, w_idx, g_pad, b_pad, (IDX_HEAD_DIM ** -0.5) * (IDX_HEADS ** -0.5))
        idx = idx.reshape(B, S, LANES)
        ik = idx[:, :, :IDX_HEAD_DIM].astype(BF16)
        iwT = jnp.swapaxes(idx[:, :, IDX_HEAD_DIM:IDX_HEAD_DIM + SUBLANES], 1, 2)

        fm = _proj_t(xb, jnp.concatenate([w_qa, w_va, w_iq], axis=0), B, S, BF16,
                     scaled_rows=ATTN_W, scale=ATTN_HEAD_DIM ** -0.5 * LOG2E)
        tokb = _proj(xb, jnp.concatenate([w_vr, w_ka], axis=0), BF16)
        gates = _proj_gates(xb, jnp.concatenate([w_gr, w_ga, w_gtr], axis=0), tn=D)
        w_rope = jnp.concatenate([w_qr, w_kr], axis=0)
        w_rope_rot = jnp.concatenate([_rot_half_weight(w_qr), _rot_half_weight(w_kr)], axis=0)
        qk_r = _proj_rope(xb, w_rope, w_rope_rot, cos, sin)

        y_a = _sparse_attention(fm, tokb.reshape(B, S, -1), ik, iwT, positions, rel_bias)
        y_r = _retention(qk_r, tokb, gates, B, S)
        x1 = _merge(xf, y_a.reshape(T, ATTN_W), y_r, gates,
                    w_attn_branch[l].astype(BF16), w_ret_branch[l].astype(BF16),
                    w_out[l].astype(BF16), ln_mix_g[l].reshape(1, D), ln_mix_b[l].reshape(1, D))
        xf = _ffn(x1, w_up[l].astype(BF16), w_down[l].astype(BF16),
                  ln_ffn_g[l].reshape(1, D), ln_ffn_b[l].reshape(1, D))
    return xf.reshape(B, S, D)
```

```python
import functools
import math

import numpy as np
import jax
import jax.numpy as jnp
from jax import lax
from jax.experimental import pallas as pl
from jax.experimental.pallas import tpu as pltpu

F32 = jnp.float32
BF16 = jnp.bfloat16

ATTN_HEADS = 8
ATTN_HEAD_DIM = 64
ATTN_W = ATTN_HEADS * ATTN_HEAD_DIM
IDX_HEADS = 4
IDX_HEAD_DIM = 64
IDX_Q_W = IDX_HEADS * IDX_HEAD_DIM
TOPK_MAX = 256
RET_HEADS = 8
RET_QK_DIM = 64
RET_V_DIM = 128
RET_QK_W = RET_HEADS * RET_QK_DIM
RET_V_W = RET_HEADS * RET_V_DIM
RET_CHUNK = 128
ROPE_BASE = 10000.0
NUM_BUCKETS = 32
MAX_DISTANCE = 128
LN_EPS = 1e-5
DEPTH = 1
DEEPNORM_ALPHA = (2.0 * DEPTH) ** 0.25

LANES = 128
SUBLANES = 8
VMEM_LIMIT = 56 * 1024 * 1024

PROJ_TM = 1024
GATES_TM = 2048
MERGE_TM = 512
FFN_TM = 1024
FFN_TF = 1024

TQ = 128
CK = 512
NEG = -1e30
LOG2E = math.log2(math.e)
BISECT_ROUNDS = 20
BIAS_TABLE_N = 128
FAR_N = 113


def _cparams(n_grid):
    return pltpu.CompilerParams(
        dimension_semantics=("arbitrary",) * n_grid,
        vmem_limit_bytes=VMEM_LIMIT)


def _trig_kernel(pos_ref, inv_ref, cos_ref, sin_ref):
    ang = pos_ref[...] * inv_ref[...]
    tr = ang.shape[0]
    half = RET_QK_DIM // 2
    per_row = LANES // half
    lane = lax.broadcasted_iota(jnp.int32, ang.shape, 1)
    for ref, val in ((cos_ref, jnp.cos(ang)), (sin_ref, jnp.sin(ang))):
        for k in range(per_row):
            own = jnp.where((lane >= k * half) & (lane < (k + 1) * half), val, 0.0)
            spread = own
            for j in range(1, per_row):
                spread = spread + pltpu.roll(own, j * half, 1)
            ref[pl.ds(k, tr, stride=per_row), :] = spread


def _rope_tables(positions):
    B, S = positions.shape
    half = RET_QK_DIM // 2
    inv = ROPE_BASE ** (-jnp.arange(half, dtype=F32) / half)
    per_row = LANES // half
    rows = B * S // per_row
    pos_e = jnp.repeat(positions.astype(F32).reshape(rows, per_row), half, axis=1)
    inv_e = jnp.tile(inv, per_row).reshape(1, LANES)
    tr = min(rows, 1024)
    return pl.pallas_call(
        _trig_kernel,
        grid=(rows // tr,),
        in_specs=[pl.BlockSpec((tr, LANES), lambda i: (i, 0)),
                  pl.BlockSpec((1, LANES), lambda i: (0, 0))],
        out_specs=[pl.BlockSpec((per_row * tr, LANES), lambda i: (i, 0))] * 2,
        out_shape=[jax.ShapeDtypeStruct((B * S, LANES), F32)] * 2,
        compiler_params=_cparams(1),
        name="rope_tables",
    )(pos_e, inv_e)


def _x_wt(x, wT):
    return lax.dot_general(x, wT, (((1,), (1,)), ((), ())), preferred_element_type=F32)


def _proj_kernel(x_ref, w_ref, o_ref):
    o_ref[...] = _x_wt(x_ref[...], w_ref[...]).astype(o_ref.dtype)


def _proj(xb, wT, out_dtype, tm=PROJ_TM):
    T, D = xb.shape
    N = wT.shape[0]
    tn = N
    tm = min(tm, T)
    return pl.pallas_call(
        _proj_kernel,
        grid=(T // tm, N // tn),
        in_specs=[pl.BlockSpec((tm, D), lambda i, j: (i, 0)),
                  pl.BlockSpec((tn, D), lambda i, j: (j, 0))],
        out_specs=pl.BlockSpec((tm, tn), lambda i, j: (i, j)),
        out_shape=jax.ShapeDtypeStruct((T, N), out_dtype),
        compiler_params=_cparams(2),
        name="proj",
    )(xb, wT)


def _proj_gates_kernel(x_ref, w_ref, o_ref):
    acc = _x_wt(x_ref[...], w_ref[...])
    sig = 0.5 * jnp.tanh(0.5 * acc) + 0.5
    o_ref[...] = jnp.where(pl.program_id(1) == 0, acc * sig, sig).astype(o_ref.dtype)


def _proj_gates(xb, wT, tn, tm=GATES_TM):
    T, D = xb.shape
    N = wT.shape[0]
    tm = min(tm, T)
    return pl.pallas_call(
        _proj_gates_kernel,
        grid=(T // tm, N // tn),
        in_specs=[pl.BlockSpec((tm, D), lambda i, j: (i, 0)),
                  pl.BlockSpec((tn, D), lambda i, j: (j, 0))],
        out_specs=pl.BlockSpec((tm, tn), lambda i, j: (i, j)),
        out_shape=jax.ShapeDtypeStruct((T, N), BF16),
        compiler_params=_cparams(2),
        name="proj_gates",
    )(xb, wT)


def _proj_t_kernel(wT_ref, x_ref, o_ref, *, scaled_rows, scale):
    acc = lax.dot_general(wT_ref[...], x_ref[...], (((1,), (1,)), ((), ())), preferred_element_type=F32)
    o_ref[:scaled_rows, :] = (acc[:scaled_rows] * scale).astype(o_ref.dtype)
    o_ref[scaled_rows:, :] = acc[scaled_rows:].astype(o_ref.dtype)


def _proj_t(xb, wT, B, S, out_dtype, scaled_rows, scale, tm=PROJ_TM):
    T, D = xb.shape
    N = wT.shape[0]
    tn = N
    tm = min(tm, S)
    nsb = S // tm
    return pl.pallas_call(
        functools.partial(_proj_t_kernel, scaled_rows=scaled_rows, scale=scale),
        grid=(T // tm, N // tn),
        in_specs=[pl.BlockSpec((tn, D), lambda i, j: (j, 0)),
                  pl.BlockSpec((tm, D), lambda i, j: (i, 0))],
        out_specs=pl.BlockSpec((pl.Squeezed(), tn, tm), lambda i, j: (i // nsb, j, i % nsb)),
        out_shape=jax.ShapeDtypeStruct((B, N, S), out_dtype),
        compiler_params=_cparams(2),
        name="proj_t",
    )(wT, xb)


def _proj_rope_kernel(x_ref, w_ref, wr_ref, cos_ref, sin_ref, o_ref):
    x = x_ref[...]
    a = _x_wt(x, w_ref[...])
    r = _x_wt(x, wr_ref[...])
    reps = a.shape[1] // LANES
    cos = jnp.concatenate([cos_ref[...]] * reps, axis=1)
    sin = jnp.concatenate([sin_ref[...]] * reps, axis=1)
    o_ref[...] = (a * cos + r * sin).astype(o_ref.dtype)


def _proj_rope(xb, wT, wT_rot, cos, sin, tm=PROJ_TM):
    T, D = xb.shape
    N = wT.shape[0]
    tn = N
    tm = min(tm, T)
    return pl.pallas_call(
        _proj_rope_kernel,
        grid=(T // tm, N // tn),
        in_specs=[pl.BlockSpec((tm, D), lambda i, j: (i, 0)),
                  pl.BlockSpec((tn, D), lambda i, j: (j, 0)),
                  pl.BlockSpec((tn, D), lambda i, j: (j, 0)),
                  pl.BlockSpec((tm, LANES), lambda i, j: (i, 0)),
                  pl.BlockSpec((tm, LANES), lambda i, j: (i, 0))],
        out_specs=pl.BlockSpec((tm, tn), lambda i, j: (i, j)),
        out_shape=jax.ShapeDtypeStruct((T, N), BF16),
        compiler_params=_cparams(2),
        name="proj_rope",
    )(xb, wT, wT_rot, cos, sin)


def _proj_idx_kernel(x_ref, w_ref, g_ref, b_ref, ik_ref, iwT_ref, xb_ref, *, iw_scale):
    xb = x_ref[...].astype(BF16)
    xb_ref[...] = xb
    acc = _x_wt(xb, w_ref[...])
    lane = lax.broadcasted_iota(jnp.int32, acc.shape, 1)
    is_k = lane < IDX_HEAD_DIM
    mu = jnp.sum(jnp.where(is_k, acc, 0.0), axis=-1, keepdims=True) / IDX_HEAD_DIM
    d = acc - mu
    var = jnp.sum(jnp.where(is_k, d * d, 0.0), axis=-1, keepdims=True) / IDX_HEAD_DIM
    ln = d * lax.rsqrt(var + LN_EPS) * g_ref[...] + b_ref[...]
    ik_ref[...] = ln[:, :IDX_HEAD_DIM].astype(ik_ref.dtype)
    iwT_ref[...] = (acc * iw_scale).T[IDX_HEAD_DIM:IDX_HEAD_DIM + SUBLANES, :]


def _proj_idx(x, w_pad, g_pad, b_pad, iw_scale, B, S, tm=PROJ_TM):
    T, D = x.shape
    tm = min(tm, S)
    nsb = S // tm
    return pl.pallas_call(
        functools.partial(_proj_idx_kernel, iw_scale=iw_scale),
        grid=(T // tm,),
        in_specs=[pl.BlockSpec((tm, D), lambda i: (i, 0)),
                  pl.BlockSpec((LANES, D), lambda i: (0, 0)),
                  pl.BlockSpec((1, LANES), lambda i: (0, 0)),
                  pl.BlockSpec((1, LANES), lambda i: (0, 0))],
        out_specs=[pl.BlockSpec((tm, IDX_HEAD_DIM), lambda i: (i, 0)),
                   pl.BlockSpec((pl.Squeezed(), SUBLANES, tm), lambda i: (i // nsb, 0, i % nsb)),
                   pl.BlockSpec((tm, D), lambda i: (i, 0))],
        out_shape=[jax.ShapeDtypeStruct((T, IDX_HEAD_DIM), BF16),
                   jax.ShapeDtypeStruct((B, SUBLANES, S), F32),
                   jax.ShapeDtypeStruct((T, D), BF16)],
        compiler_params=_cparams(1),
        name="proj_idx",
    )(x, w_pad, g_pad, b_pad)


def _t5_bucket_table():
    n = np.arange(BIAS_TABLE_N)
    max_exact = NUM_BUCKETS // 2
    nf = np.maximum(n, 1).astype(np.float64)
    large = max_exact + (np.log(nf / max_exact) / math.log(MAX_DISTANCE / max_exact)
                         * (NUM_BUCKETS - max_exact)).astype(np.int64)
    large = np.minimum(large, NUM_BUCKETS - 1)
    bucket = np.where(n < max_exact, n, large)
    assert np.all(bucket[FAR_N:] == NUM_BUCKETS - 1) and bucket[FAR_N - 1] != NUM_BUCKETS - 1
    return bucket.astype(np.int32)


def _fold_rows(a, op):
    parts = [a[r:r + SUBLANES] for r in range(0, a.shape[0], SUBLANES)]
    while len(parts) > 1:
        nxt = [op(parts[k], parts[k + 1]) for k in range(0, len(parts) - 1, 2)]
        if len(parts) % 2:
            nxt.append(parts[-1])
        parts = nxt
    return parts[0]


def _attn_kernel(biasc_ref, pinfo_ref, ckmax_ref, iqT_ref, iwT_ref, posqc_ref, posk_ref, ik_ref, qT_ref, k_ref, vT_ref,
                 tbl_ref, toep_ref, tril_ref, o_ref, sc_ref, s_ref, s2_ref, b_ref, qm_ref, m_ref, l_ref, *acc_refs, k_sel, seq):
    i = pl.program_id(1)
    nch = (i * TQ + TQ + CK - 1) // CK
    q_idx = i * TQ + lax.broadcasted_iota(jnp.int32, (1, TQ), 1)
    kf = float(k_sel)

    def chunk_off(c):
        return pl.multiple_of(c * CK, CK)

    def key_idx(off):
        return off + lax.broadcasted_iota(jnp.int32, (CK, TQ), 0)

    def col_reduce(part, op):
        return op(part, axis=0, keepdims=True)

    iqT = iqT_ref[...]
    iwT = iwT_ref[...]
    iq_wide = jnp.concatenate([iqT[h * IDX_HEAD_DIM:(h + 1) * IDX_HEAD_DIM, :] for h in range(IDX_HEADS)], axis=1)

    def score_matmul(c, z_buf):
        ikc = ik_ref[pl.ds(chunk_off(c), CK), :]
        z = jnp.dot(ikc, iq_wide, preferred_element_type=F32)
        for h in range(IDX_HEADS):
            z_buf[h] = z[:, h * TQ:(h + 1) * TQ]

    def score_finish(c, z_buf, carry, masked):
        mn, mx = carry
        off = chunk_off(c)
        s = None
        for h in range(IDX_HEADS):
            t = jnp.maximum(z_buf[h], 0.0) * iwT[h:h + 1, :]
            s = t if s is None else s + t
        if masked:
            causal = key_idx(off) <= q_idx
            s_lo = jnp.where(causal, s, -jnp.inf)
            s_hi = jnp.where(causal, s, jnp.inf)
        else:
            s_lo = s_hi = s
        sc_ref[pl.ds(off, CK), :] = s_lo
        mn = jnp.minimum(mn, _fold_rows(s_hi, jnp.minimum))
        mx = jnp.maximum(mx, _fold_rows(s_lo, jnp.maximum))
        return mn, mx

    def score_chunk(c, carry, masked):
        score_matmul(c, s_ref)
        return score_finish(c, s_ref, carry, masked)

    n_inner = nch - 1
    n_pairs_sc = n_inner // 2

    @pl.when(n_pairs_sc > 0)
    def _():
        score_matmul(0, s_ref)

    def score_step(t, carry):
        c = 2 * t
        score_matmul(c + 1, s2_ref)
        carry = score_finish(c, s_ref, carry, False)
        score_matmul(jnp.minimum(c + 2, 2 * n_pairs_sc - 2), s_ref)
        return score_finish(c + 1, s2_ref, carry, False)

    mn8, mx8 = lax.fori_loop(0, n_pairs_sc, score_step,
                             (jnp.full((SUBLANES, TQ), jnp.inf, F32), jnp.full((SUBLANES, TQ), -jnp.inf, F32)))
    mn8, mx8 = lax.fori_loop(2 * n_pairs_sc, n_inner, functools.partial(score_chunk, masked=False), (mn8, mx8))
    mn8, mx8 = score_chunk(nch - 1, (mn8, mx8), masked=True)
    mn = col_reduce(mn8, jnp.min)
    mx = col_reduce(mx8, jnp.max)

    def over_chunks(fn, init):
        def one(c, carry):
            return fn(sc_ref[pl.ds(chunk_off(c), CK), :], carry)

        carry = lax.fori_loop(0, nch // 2, lambda t, carry: one(2 * t + 1, one(2 * t, carry)), init)
        return lax.fori_loop(2 * (nch // 2), nch, one, carry)

    def count(pred_fn):
        acc = over_chunks(lambda blk, acc: acc + _fold_rows(jnp.where(pred_fn(blk), 1.0, 0.0), jnp.add),
                          jnp.zeros((SUBLANES, TQ), F32))
        return col_reduce(acc, jnp.sum)

    def bisect_round(_, st):
        lo, hi, c_lo = st
        mid = 0.5 * (lo + hi)
        c = count(lambda blk: blk >= mid)
        ok = c >= kf
        return jnp.where(ok, mid, lo), jnp.where(ok, hi, mid), jnp.where(ok, c, c_lo)

    c_all = (q_idx + 1).astype(F32)
    lo, hi, c_lo = lax.fori_loop(0, BISECT_ROUNDS, bisect_round, (mn, mx, c_all))

    cur0 = col_reduce(over_chunks(
        lambda blk, acc: jnp.minimum(acc, _fold_rows(jnp.where(blk >= lo, blk, jnp.inf), jnp.minimum)),
        jnp.full((SUBLANES, TQ), jnp.inf, F32)), jnp.min)

    def walk_cond(st):
        return st[3] > 0.0

    def walk_body(st):
        cur, c_ge, _, _ = st

        def body(blk, carry):
            cnt, nxt = carry
            gt = blk > cur
            cnt = cnt + _fold_rows(jnp.where(gt, 1.0, 0.0), jnp.add)
            nxt = jnp.minimum(nxt, _fold_rows(jnp.where(gt, blk, jnp.inf), jnp.minimum))
            return cnt, nxt

        cnt, nxt = over_chunks(body, (jnp.zeros((SUBLANES, TQ), F32), jnp.full((SUBLANES, TQ), jnp.inf, F32)))
        c_gt = col_reduce(cnt, jnp.sum)
        nxt = col_reduce(nxt, jnp.min)
        adv = c_gt >= kf
        cur = jnp.where(adv, nxt, cur)
        c_ge = jnp.where(adv, c_gt, c_ge)
        return cur, c_ge, c_gt, jnp.max(jnp.where(adv, 1.0, 0.0))

    tau, c_ge, c_gt, _ = lax.while_loop(
        walk_cond, walk_body, (cur0, c_lo, jnp.zeros((1, TQ), F32), jnp.float32(1.0)))

    room = kf - c_gt

    def selection_mask(c, seen):
        off = chunk_off(c)
        tiles = range(CK // LANES)
        blks = [sc_ref[pl.ds(off + j * LANES, LANES), :] for j in tiles]
        ties = [blk == tau for blk in blks]
        local = [jnp.dot(tril_ref[...], jnp.where(tie, 1.0, 0.0).astype(BF16), preferred_element_type=F32)
                 for tie in ties]
        masks = []
        for j in tiles:
            keep = (blks[j] > tau) | (ties[j] & (local[j] + seen <= room))
            masks.append(jnp.where(keep, 0.0, NEG))
            seen = seen + local[j][LANES - 1:LANES, :]
        return jnp.concatenate(masks, axis=0), seen

    m_ref[...] = jnp.full(m_ref.shape, NEG, F32)
    l_ref[...] = jnp.zeros(l_ref.shape, F32)
    for acc in acc_refs:
        acc[...] = jnp.zeros(acc.shape, F32)

    rowi = lax.broadcasted_iota(jnp.int32, (LANES, TQ), 0)
    for pair in range(ATTN_HEADS // 2):
        qp = qT_ref[pair * LANES:(pair + 1) * LANES, :]
        zero = jnp.zeros_like(qp)
        qm_ref[pair] = jnp.concatenate([jnp.where(rowi < ATTN_HEAD_DIM, qp, zero),
                                        jnp.where(rowi >= ATTN_HEAD_DIM, qp, zero)], axis=1)
    ones_rows = jnp.ones((2 * SUBLANES, CK), BF16)

    pq_col = posqc_ref[...]
    batch = pl.program_id(0)
    pq_first, pq_consec, pq_min = (pinfo_ref[batch, r, i] for r in (0, 1, 3))
    far_bias = [biasc_ref[0, h] for h in range(ATTN_HEADS)]
    zero_bias = [biasc_ref[1, h] for h in range(ATTN_HEADS)]
    q_ge_k = (lax.broadcasted_iota(jnp.int32, (LANES, TQ), 1) >= lax.broadcasted_iota(jnp.int32, (LANES, TQ), 0))
    n_sub = CK // LANES
    n_pairs = ATTN_HEADS // 2

    def chunk_is_far(c):
        return (pq_min - ckmax_ref[batch, jnp.minimum(c, seq // CK - 1)]) >= FAR_N

    def stage_bias(c):
        off = chunk_off(c)
        pk_row = posk_ref[:, pl.ds(off, CK)]
        for j in range(n_sub):
            rows = slice(j * LANES, (j + 1) * LANES)
            g = c * n_sub + j
            pk_first, pk_consec, pk_max = (pinfo_ref[batch, r, g] for r in (0, 1, 2))
            all_far = (pq_min - pk_max) >= FAR_N
            all_masked = (off + j * LANES) > (i * TQ + TQ - 1)
            consecutive = (pq_consec > 0) & (pk_consec > 0)
            gap = pq_first - pk_first

            def fill_const(rows=rows):
                for h in range(ATTN_HEADS):
                    b_ref[h, rows, :] = jnp.full((LANES, TQ), far_bias[h], F32)

            def fill_gap0(rows=rows):
                for h in range(ATTN_HEADS):
                    b_ref[h, rows, :] = jnp.where(q_ge_k, toep_ref[h], zero_bias[h])

            def fill_gap128(rows=rows):
                for h in range(ATTN_HEADS):
                    b_ref[h, rows, :] = jnp.where(q_ge_k, far_bias[h], toep_ref[h])

            def fill_lookup(rows=rows):
                pk_sub = pk_row[:, rows]
                n_qk = jnp.clip(pq_col - pk_sub, 0, BIAS_TABLE_N - 1).astype(F32)
                n_kq = n_qk.T.astype(jnp.int32)
                for h in range(ATTN_HEADS):
                    tb = jnp.broadcast_to(tbl_ref[h:h + 1, :], (LANES, BIAS_TABLE_N))
                    b_ref[h, rows, :] = jnp.take_along_axis(tb, n_kq, axis=1)

            def fill_near(fill_gap0=fill_gap0, fill_gap128=fill_gap128, fill_lookup=fill_lookup,
                          consecutive=consecutive, gap=gap):
                lax.cond(consecutive & (gap == 0), fill_gap0,
                         lambda: lax.cond(consecutive & (gap == LANES), fill_gap128, fill_lookup))

            lax.cond(all_far | all_masked, fill_const, fill_near)

    def logits_phase(c, s_buf, const_bias, seen):
        off = chunk_off(c)
        mb, seen = selection_mask(c, seen)
        m_cur = []
        for pair in range(n_pairs):
            kc = k_ref[pl.ds(off, CK), pair * LANES:(pair + 1) * LANES]
            s2 = jnp.dot(kc, qm_ref[pair], preferred_element_type=F32)
            for sub, h in enumerate((2 * pair, 2 * pair + 1)):
                s = s2[:, sub * TQ:(sub + 1) * TQ] + mb
                if not const_bias:
                    s = s + b_ref[h]
                s_buf[h] = s
                top = col_reduce(_fold_rows(s, jnp.maximum), jnp.max)
                m_cur.append(top + far_bias[h] if const_bias else top)
        return m_cur, seen

    def update_phase(c, s_buf, m_cur, const_bias):
        off = chunk_off(c)
        m_prev = m_ref[...]
        l_prev = l_ref[...]
        if isinstance(m_cur, list):
            m_new = [jnp.maximum(m_prev[h:h + 1, :], m_cur[h]) for h in range(ATTN_HEADS)]
            alpha = [jnp.exp2(m_prev[h:h + 1, :] - m_new[h]) for h in range(ATTN_HEADS)]
        else:
            m_all = jnp.maximum(m_prev, m_cur)
            a_all = jnp.exp2(m_prev - m_all)
            m_new = [m_all[h:h + 1, :] for h in range(ATTN_HEADS)]
            alpha = [a_all[h:h + 1, :] for h in range(ATTN_HEADS)]
        l_new = []
        for pair in range(n_pairs):
            heads = (2 * pair, 2 * pair + 1)
            sub_m = [m_new[h] - far_bias[h] if const_bias else m_new[h] for h in heads]
            p2 = jnp.concatenate([jnp.exp2(s_buf[h] - sm).astype(BF16) for h, sm in zip(heads, sub_m)],
                                 axis=1)
            lhs = jnp.concatenate([vT_ref[pair * LANES:(pair + 1) * LANES, pl.ds(off, CK)], ones_rows],
                                  axis=0)
            out = jnp.dot(lhs, p2, preferred_element_type=F32)
            for sub, h in enumerate(heads):
                cols = slice(sub * TQ, (sub + 1) * TQ)
                acc = acc_refs[h]
                acc[...] = alpha[h] * acc[...] + out[sub * ATTN_HEAD_DIM:(sub + 1) * ATTN_HEAD_DIM, cols]
                l_new.append(alpha[h] * l_prev[h:h + 1, :] + out[LANES:LANES + 1, cols])
        l_ref[...] = jnp.concatenate(l_new, axis=0)
        m_ref[...] = jnp.concatenate(m_new, axis=0)

    n_far = lax.while_loop(lambda c: (c < nch) & chunk_is_far(c), lambda c: c + 1, jnp.int32(0))
    n_steps = n_far // 2
    def far_logits(c, s_buf, seen):
        m_cur, seen = logits_phase(c, s_buf, True, seen)
        return jnp.concatenate(m_cur, axis=0), seen

    no_ties = jnp.zeros((1, TQ), F32)
    m_first, seen_first = lax.cond(n_steps > 0, lambda: far_logits(0, s_ref, no_ties),
                                   lambda: (jnp.zeros((ATTN_HEADS, TQ), F32), no_ties))

    def far_step(t, carry):
        m_even, seen_even, _ = carry
        c = 2 * t
        m_odd, seen_odd = far_logits(c + 1, s2_ref, seen_even)
        update_phase(c, s_ref, m_even, True)
        m_even, seen_even = far_logits(jnp.minimum(c + 2, n_far - 1), s_ref, seen_odd)
        update_phase(c + 1, s2_ref, m_odd, True)
        return m_even, seen_even, seen_odd

    m_even, seen_even, seen = lax.fori_loop(0, n_steps, far_step, (m_first, seen_first, no_ties))
    leftover_staged = (n_steps > 0) & (n_far % 2 == 1)

    def finish_leftover():
        update_phase(n_far - 1, s_ref, m_even, True)
        return seen_even

    seen = lax.cond(leftover_staged, finish_leftover, lambda: seen)
    n_done = jnp.where(leftover_staged, n_far, 2 * n_steps)

    def tail_chunk(c, seen):
        def run(const_bias):
            if not const_bias:
                stage_bias(c)
            m_cur, seen_out = logits_phase(c, s_ref, const_bias, seen)
            update_phase(c, s_ref, m_cur, const_bias)
            return seen_out

        return lax.cond(chunk_is_far(c), lambda: run(True), lambda: run(False))

    lax.fori_loop(n_done, nch, tail_chunk, seen)

    outT = jnp.concatenate([acc_refs[h][...] / l_ref[h:h + 1, :] for h in range(ATTN_HEADS)], axis=0)
    o_ref[...] = outT.T.astype(o_ref.dtype)


def _toeplitz_kernel(tbl_ref, o_ref):
    q = lax.broadcasted_iota(jnp.int32, (LANES, LANES), 1)
    k = lax.broadcasted_iota(jnp.int32, (LANES, LANES), 0)
    idx = (q - k) & (BIAS_TABLE_N - 1)
    for h in range(ATTN_HEADS):
        tb = jnp.broadcast_to(tbl_ref[h:h + 1, :], (LANES, BIAS_TABLE_N))
        o_ref[h] = jnp.take_along_axis(tb, idx, axis=1)


def _sparse_attention(fm, tokb, ik, iwT, positions, rel_bias):
    B, S, _ = tokb.shape
    k_sel = min(TOPK_MAX, S // 4)
    bucket = _t5_bucket_table()
    tbl = rel_bias[bucket].T.astype(F32) * LOG2E
    bias_c = jnp.stack([tbl[:, BIAS_TABLE_N - 1], tbl[:, 0]])
    toep = pl.pallas_call(
        _toeplitz_kernel,
        out_shape=jax.ShapeDtypeStruct((ATTN_HEADS, LANES, LANES), F32),
        name="bias_toeplitz",
    )(tbl)
    pos_row = positions.reshape(B, 1, S)
    pos_col = positions.reshape(B, S, 1)
    pos_t = positions.reshape(B, S // LANES, LANES)
    consec = jnp.all(pos_t - pos_t[:, :, :1] == jnp.arange(LANES, dtype=positions.dtype), axis=-1)
    pinfo = jnp.stack([pos_t[:, :, 0], consec.astype(jnp.int32), jnp.max(pos_t, axis=-1),
                       jnp.min(pos_t, axis=-1)], axis=1).astype(jnp.int32)
    ck_max = jnp.max(positions.reshape(B, S // CK, CK), axis=-1)
    tril = jnp.tril(jnp.ones((LANES, LANES), BF16))
    sq = pl.Squeezed()
    iq_blk = (2 * ATTN_W) // IDX_Q_W
    k_blk = (tokb.shape[2] - ATTN_W) // ATTN_W
    return pl.pallas_call(
        functools.partial(_attn_kernel, k_sel=k_sel, seq=S),
        grid=(B, S // TQ),
        in_specs=[pl.BlockSpec(memory_space=pltpu.SMEM),
                  pl.BlockSpec(memory_space=pltpu.SMEM),
                  pl.BlockSpec(memory_space=pltpu.SMEM),
                  pl.BlockSpec((sq, IDX_Q_W, TQ), lambda b, i: (b, iq_blk, i)),
                  pl.BlockSpec((sq, SUBLANES, TQ), lambda b, i: (b, 0, i)),
                  pl.BlockSpec((sq, TQ, 1), lambda b, i: (b, i, 0)),
                  pl.BlockSpec((sq, 1, S), lambda b, i: (b, 0, 0)),
                  pl.BlockSpec((sq, S, IDX_HEAD_DIM), lambda b, i: (b, 0, 0)),
                  pl.BlockSpec((sq, ATTN_W, TQ), lambda b, i: (b, 0, i)),
                  pl.BlockSpec((sq, S, ATTN_W), lambda b, i: (b, 0, k_blk)),
                  pl.BlockSpec((sq, ATTN_W, S), lambda b, i: (b, 1, 0)),
                  pl.BlockSpec((ATTN_HEADS, BIAS_TABLE_N), lambda b, i: (0, 0)),
                  pl.BlockSpec((ATTN_HEADS, LANES, LANES), lambda b, i: (0, 0, 0)),
                  pl.BlockSpec((LANES, LANES), lambda b, i: (0, 0))],
        out_specs=pl.BlockSpec((sq, TQ, ATTN_W), lambda b, i: (b, i, 0)),
        out_shape=jax.ShapeDtypeStruct((B, S, ATTN_W), BF16),
        scratch_shapes=[pltpu.VMEM((S, TQ), F32),
                        pltpu.VMEM((ATTN_HEADS, CK, TQ), F32),
                        pltpu.VMEM((ATTN_HEADS, CK, TQ), F32),
                        pltpu.VMEM((ATTN_HEADS, CK, TQ), F32),
                        pltpu.VMEM((ATTN_HEADS // 2, LANES, 2 * TQ), BF16),
                        pltpu.VMEM((ATTN_HEADS, TQ), F32),
                        pltpu.VMEM((ATTN_HEADS, TQ), F32)]
                       + [pltpu.VMEM((ATTN_HEAD_DIM, TQ), F32)] * ATTN_HEADS,
        compiler_params=_cparams(2),
        name="sparse_attention",
    )(bias_c, pinfo, ck_max, fm, iwT, pos_col, pos_row, ik, fm, tokb, fm, tbl, toep, tril)


def _retention_kernel(q_ref, k_ref, v_ref, g_ref, decay_ref, xi_ref, zeta_ref, gch_ref, o_ref, r_ref):
    @pl.when(pl.program_id(0) == 0)
    def _():
        r_ref[...] = jnp.zeros(r_ref.shape, F32)

    C = RET_CHUNK
    lane = lax.broadcasted_iota(jnp.int32, (C, LANES), 1)
    row = lax.broadcasted_iota(jnp.int32, (LANES, RET_V_DIM), 0)
    for pair in range(RET_HEADS // 2):
        for b in range(q_ref.shape[0]):
            q_pair = q_ref[b, :, pair * LANES:(pair + 1) * LANES]
            k_pair = k_ref[b, :, pair * LANES:(pair + 1) * LANES]
            v_pair = v_ref[b, :, 2 * pair * RET_V_DIM:(2 * pair + 2) * RET_V_DIM]
            r_pair = r_ref[b, pair]
            r_bf = r_pair.astype(BF16)
            for sub in range(2):
                h = 2 * pair + sub
                in_head = (lane >= sub * RET_QK_DIM) & (lane < (sub + 1) * RET_QK_DIM)
                qm = jnp.where(in_head, q_pair, jnp.zeros_like(q_pair))
                v_h = v_pair[:, sub * RET_V_DIM:(sub + 1) * RET_V_DIM]
                inner = lax.dot_general(qm, k_pair, (((1,), (1,)), ((), ())),
                                        preferred_element_type=F32) * decay_ref[h]
                o = (jnp.dot(inner.astype(BF16), v_h, preferred_element_type=F32)
                     + jnp.dot(qm, r_bf, preferred_element_type=F32) * xi_ref[h])
                mu = jnp.mean(o, axis=-1, keepdims=True)
                d = o - mu
                var = jnp.mean(d * d, axis=-1, keepdims=True)
                hn = d * lax.rsqrt(var + LN_EPS)
                gate = g_ref[b, :, h * RET_V_DIM:(h + 1) * RET_V_DIM].astype(F32)
                o_ref[b, :, h * RET_V_DIM:(h + 1) * RET_V_DIM] = (gate * hn).astype(o_ref.dtype)
            kz = (k_pair.astype(F32) * zeta_ref[pair]).astype(BF16)
            upd = lax.dot_general(kz, v_pair, (((0,), (0,)), ((), ())), preferred_element_type=F32)
            r_ref[b, pair] = (r_pair * gch_ref[pair]
                              + jnp.where(row < RET_QK_DIM, upd[:, :RET_V_DIM], upd[:, RET_V_DIM:]))


def _retention(qk, tokb, gates, B, S):
    C = RET_CHUNK
    H = RET_HEADS
    nc = S // C
    gamma = 1.0 - 2.0 ** (-5.0 - jnp.arange(H, dtype=F32))
    log_g = jnp.log(gamma)
    n = jnp.arange(C, dtype=F32)
    diff = n[:, None] - n[None, :]
    decay_in = jnp.where(diff[None] >= 0, jnp.exp(log_g[:, None, None] * jnp.maximum(diff, 0.0)[None]), 0.0)
    xi = jnp.exp(log_g[None, :] * (n[:, None] + 1.0))
    zeta = jnp.exp(log_g[None, :] * (C - 1.0 - n[:, None]))
    g_chunk = jnp.exp(log_g * C)
    xi_b = jnp.broadcast_to(xi.T[:, :, None], (H, C, RET_V_DIM))
    zeta_b = jnp.repeat(zeta, RET_QK_DIM, axis=1).reshape(C, H // 2, LANES).transpose(1, 0, 2)
    gch_b = jnp.broadcast_to(jnp.repeat(g_chunk, RET_QK_DIM).reshape(H // 2, LANES, 1),
                             (H // 2, LANES, RET_V_DIM))
    qk3, tok3, gate3 = (a.reshape(B, S, a.shape[-1]) for a in (qk, tokb, gates))
    out = pl.pallas_call(
        _retention_kernel,
        grid=(nc,),
        in_specs=[pl.BlockSpec((B, C, RET_QK_W), lambda i: (0, i, 0)),
                  pl.BlockSpec((B, C, RET_QK_W), lambda i: (0, i, 1)),
                  pl.BlockSpec((B, C, RET_V_W), lambda i: (0, i, 0)),
                  pl.BlockSpec((B, C, RET_V_W), lambda i: (0, i, 0)),
                  pl.BlockSpec((H, C, C), lambda i: (0, 0, 0)),
                  pl.BlockSpec((H, C, RET_V_DIM), lambda i: (0, 0, 0)),
                  pl.BlockSpec((H // 2, C, LANES), lambda i: (0, 0, 0)),
                  pl.BlockSpec((H // 2, LANES, RET_V_DIM), lambda i: (0, 0, 0))],
        out_specs=pl.BlockSpec((B, C, RET_V_W), lambda i: (0, i, 0)),
        out_shape=jax.ShapeDtypeStruct((B, S, RET_V_W), BF16),
        scratch_shapes=[pltpu.VMEM((B, H // 2, LANES, RET_V_DIM), F32)],
        compiler_params=_cparams(1),
        name="retention",
    )(qk3, qk3, tok3, gate3, decay_in, xi_b, zeta_b, gch_b)
    return out.reshape(B * S, RET_V_W)


def _layer_norm(z, g, b):
    mu = jnp.mean(z, axis=-1, keepdims=True)
    d = z - mu
    var = jnp.mean(d * d, axis=-1, keepdims=True)
    return d * lax.rsqrt(var + LN_EPS) * g + b


def _merge_kernel(x_ref, ya_ref, yr_ref, ga_ref, gr_ref, wa_ref, wr_ref, wo_ref, g_ref, b_ref,
                  x1_ref):
    a = jnp.dot(ya_ref[...], wa_ref[...], preferred_element_type=F32)
    r = jnp.dot(yr_ref[...], wr_ref[...], preferred_element_type=F32)
    h = ga_ref[...].astype(F32) * a + gr_ref[...].astype(F32) * r
    mix = jnp.dot(h.astype(BF16), wo_ref[...], preferred_element_type=F32)
    x1_ref[...] = _layer_norm(DEEPNORM_ALPHA * x_ref[...] + mix, g_ref[...], b_ref[...])


def _merge(x, ya, yr, gates, wa, wr, wo, g, b, tm=MERGE_TM):
    T, D = x.shape
    tm = min(tm, T)
    row = lambda i: (i, 0)
    fixed = lambda i: (0, 0)
    return pl.pallas_call(
        _merge_kernel,
        grid=(T // tm,),
        in_specs=[pl.BlockSpec((tm, D), row),
                  pl.BlockSpec((tm, ya.shape[1]), row),
                  pl.BlockSpec((tm, yr.shape[1]), row),
                  pl.BlockSpec((tm, D), lambda i: (i, 1)),
                  pl.BlockSpec((tm, D), lambda i: (i, 2)),
                  pl.BlockSpec(wa.shape, fixed),
                  pl.BlockSpec(wr.shape, fixed),
                  pl.BlockSpec(wo.shape, fixed),
                  pl.BlockSpec((1, D), fixed),
                  pl.BlockSpec((1, D), fixed)],
        out_specs=pl.BlockSpec((tm, D), row),
        out_shape=jax.ShapeDtypeStruct((T, D), F32),
        compiler_params=_cparams(1),
        name="merge",
    )(x, ya, yr, gates, gates, wa, wr, wo, g, b)


def _ffn_kernel(x1_ref, wu_ref, wd_ref, g_ref, b_ref, o_ref, acc_ref, xb_ref):
    f = pl.program_id(1)

    @pl.when(f == 0)
    def _():
        acc_ref[...] = jnp.zeros(acc_ref.shape, F32)
        xb_ref[...] = x1_ref[...].astype(BF16)

    hid = jnp.maximum(jnp.dot(xb_ref[...], wu_ref[...], preferred_element_type=F32), 0.0)
    acc_ref[...] += jnp.dot((hid * hid).astype(BF16), wd_ref[...], preferred_element_type=F32)

    @pl.when(f == pl.num_programs(1) - 1)
    def _():
        o_ref[...] = _layer_norm(DEEPNORM_ALPHA * x1_ref[...] + acc_ref[...], g_ref[...], b_ref[...])


def _ffn(x1, wu, wd, g, b, tm=FFN_TM, tf=FFN_TF):
    T, D = x1.shape
    F = wu.shape[1]
    tm = min(tm, T)
    return pl.pallas_call(
        _ffn_kernel,
        grid=(T // tm, F // tf),
        in_specs=[pl.BlockSpec((tm, D), lambda i, f: (i, 0)),
                  pl.BlockSpec((D, tf), lambda i, f: (0, f)),
                  pl.BlockSpec((tf, D), lambda i, f: (f, 0)),
                  pl.BlockSpec((1, D), lambda i, f: (0, 0)),
                  pl.BlockSpec((1, D), lambda i, f: (0, 0))],
        out_specs=pl.BlockSpec((tm, D), lambda i, f: (i, 0)),
        out_shape=jax.ShapeDtypeStruct((T, D), F32),
        scratch_shapes=[pltpu.VMEM((tm, D), F32), pltpu.VMEM((tm, D), BF16)],
        compiler_params=_cparams(2),
        name="ffn",
    )(x1, wu, wd, g, b)


def _rot_half_weight(wT):
    N, D = wT.shape
    half = RET_QK_DIM // 2
    wh = wT.reshape(N // RET_QK_DIM, 2, half, D)
    return jnp.stack([-wh[:, 1], wh[:, 0]], axis=1).reshape(N, D)


def kernel(x, positions, w_in, rel_bias, idx_k_ln_g, idx_k_ln_b, w_attn_branch, w_ret_branch,
           w_out, ln_mix_g, ln_mix_b, w_up, w_down, ln_ffn_g, ln_ffn_b):
    B, S, D = x.shape
    T = B * S
    sizes = (ATTN_W, ATTN_W, ATTN_W, IDX_Q_W, IDX_HEAD_DIM, IDX_HEADS,
             RET_QK_W, RET_QK_W, RET_V_W, RET_V_W, D, D)
    offs = [0] + [int(o) for o in np.cumsum(sizes)]
    cos, sin = _rope_tables(positions)
    xf = x.reshape(T, D)
    for l in range(DEPTH):
        wT = jnp.swapaxes(w_in[l], 0, 1).astype(BF16)
        rows = [wT[offs[k]:offs[k + 1]] for k in range(len(sizes))]
        (w_qa, w_ka, w_va, w_iq, w_ik, w_iw, w_qr, w_kr, w_vr, w_gr, w_ga, w_gtr) = rows
        w_kr = w_kr * (RET_QK_DIM ** -0.5)

        pad = LANES - IDX_HEAD_DIM - IDX_HEADS
        w_idx = jnp.concatenate([w_ik, w_iw, jnp.zeros((pad, D), BF16)], axis=0)
        g_pad = jnp.concatenate([idx_k_ln_g[l], jnp.zeros((LANES - IDX_HEAD_DIM,), F32)]).reshape(1, LANES)
        b_pad = jnp.concatenate([idx_k_ln_b[l], jnp.zeros((LANES - IDX_HEAD_DIM,), F32)]).reshape(1, LANES)
        ik, iwT, xb = _proj_idx(xf, w_idx, g_pad, b_pad, (IDX_HEAD_DIM ** -0.5) * (IDX_HEADS ** -0.5), B, S)
        ik = ik.reshape(B, S, IDX_HEAD_DIM)

        fm = _proj_t(xb, jnp.concatenate([w_qa, w_va, w_iq], axis=0), B, S, BF16,
                     scaled_rows=ATTN_W, scale=ATTN_HEAD_DIM ** -0.5 * LOG2E)
        tokb = _proj(xb, jnp.concatenate([w_vr, w_ka], axis=0), BF16)
        gates = _proj_gates(xb, jnp.concatenate([w_gr, w_ga, w_gtr], axis=0), tn=D)
        w_rope = jnp.concatenate([w_qr, w_kr], axis=0)
        w_rope_rot = jnp.concatenate([_rot_half_weight(w_qr), _rot_half_weight(w_kr)], axis=0)
        qk_r = _proj_rope(xb, w_rope, w_rope_rot, cos, sin)

        y_a = _sparse_attention(fm, tokb.reshape(B, S, -1), ik, iwT, positions, rel_bias)
        y_r = _retention(qk_r, tokb, gates, B, S)
        x1 = _merge(xf, y_a.reshape(T, ATTN_W), y_r, gates,
                    w_attn_branch[l].astype(BF16), w_ret_branch[l].astype(BF16),
                    w_out[l].astype(BF16), ln_mix_g[l].reshape(1, D), ln_mix_b[l].reshape(1, D))
        xf = _ffn(x1, w_up[l].astype(BF16), w_down[l].astype(BF16),
                  ln_ffn_g[l].reshape(1, D), ln_ffn_b[l].reshape(1, D))
    return xf.reshape(B, S, D)
```

```python
import functools
import math

import numpy as np
import jax
import jax.numpy as jnp
from jax import lax
from jax.experimental import pallas as pl
from jax.experimental.pallas import tpu as pltpu

F32 = jnp.float32
BF16 = jnp.bfloat16

ATTN_HEADS = 8
ATTN_HEAD_DIM = 64
ATTN_W = ATTN_HEADS * ATTN_HEAD_DIM
IDX_HEADS = 4
IDX_HEAD_DIM = 64
IDX_Q_W = IDX_HEADS * IDX_HEAD_DIM
TOPK_MAX = 256
RET_HEADS = 8
RET_QK_DIM = 64
RET_V_DIM = 128
RET_QK_W = RET_HEADS * RET_QK_DIM
RET_V_W = RET_HEADS * RET_V_DIM
RET_CHUNK = 128
RET_STEP_CHUNKS = 2
ROPE_BASE = 10000.0
NUM_BUCKETS = 32
MAX_DISTANCE = 128
LN_EPS = 1e-5
DEPTH = 1
DEEPNORM_ALPHA = (2.0 * DEPTH) ** 0.25

LANES = 128
SUBLANES = 8
VMEM_LIMIT = 56 * 1024 * 1024

PROJ_TM = 1024
GATES_TM = 2048
MERGE_TM = 512
FFN_TM = 1024
FFN_TF = 1024

TQ = 128
CK = 512
NEG = -1e30
LOG2E = math.log2(math.e)
BISECT_ROUNDS = 20
BIAS_TABLE_N = 128
FAR_N = 113


def _cparams(n_grid):
    return pltpu.CompilerParams(
        dimension_semantics=("arbitrary",) * n_grid,
        vmem_limit_bytes=VMEM_LIMIT)


def _trig_kernel(pos_ref, inv_ref, cos_ref, sin_ref):
    ang = pos_ref[...] * inv_ref[...]
    tr = ang.shape[0]
    half = RET_QK_DIM // 2
    per_row = LANES // half
    lane = lax.broadcasted_iota(jnp.int32, ang.shape, 1)
    for ref, val in ((cos_ref, jnp.cos(ang)), (sin_ref, jnp.sin(ang))):
        for k in range(per_row):
            own = jnp.where((lane >= k * half) & (lane < (k + 1) * half), val, 0.0)
            spread = own
            for j in range(1, per_row):
                spread = spread + pltpu.roll(own, j * half, 1)
            ref[pl.ds(k, tr, stride=per_row), :] = spread


def _rope_tables(positions):
    B, S = positions.shape
    half = RET_QK_DIM // 2
    inv = ROPE_BASE ** (-jnp.arange(half, dtype=F32) / half)
    per_row = LANES // half
    rows = B * S // per_row
    pos_e = jnp.repeat(positions.astype(F32).reshape(rows, per_row), half, axis=1)
    inv_e = jnp.tile(inv, per_row).reshape(1, LANES)
    tr = min(rows, 1024)
    return pl.pallas_call(
        _trig_kernel,
        grid=(rows // tr,),
        in_specs=[pl.BlockSpec((tr, LANES), lambda i: (i, 0)),
                  pl.BlockSpec((1, LANES), lambda i: (0, 0))],
        out_specs=[pl.BlockSpec((per_row * tr, LANES), lambda i: (i, 0))] * 2,
        out_shape=[jax.ShapeDtypeStruct((B * S, LANES), F32)] * 2,
        compiler_params=_cparams(1),
        name="rope_tables",
    )(pos_e, inv_e)


def _x_wt(x, wT):
    return lax.dot_general(x, wT, (((1,), (1,)), ((), ())), preferred_element_type=F32)


def _proj_kernel(x_ref, w_ref, o_ref):
    o_ref[...] = _x_wt(x_ref[...], w_ref[...]).astype(o_ref.dtype)


def _proj(xb, wT, out_dtype, tm=PROJ_TM):
    T, D = xb.shape
    N = wT.shape[0]
    tn = N
    tm = min(tm, T)
    return pl.pallas_call(
        _proj_kernel,
        grid=(T // tm, N // tn),
        in_specs=[pl.BlockSpec((tm, D), lambda i, j: (i, 0)),
                  pl.BlockSpec((tn, D), lambda i, j: (j, 0))],
        out_specs=pl.BlockSpec((tm, tn), lambda i, j: (i, j)),
        out_shape=jax.ShapeDtypeStruct((T, N), out_dtype),
        compiler_params=_cparams(2),
        name="proj",
    )(xb, wT)


def _proj_gates_kernel(x_ref, w_ref, o_ref):
    acc = _x_wt(x_ref[...], w_ref[...])
    sig = 0.5 * jnp.tanh(0.5 * acc) + 0.5
    o_ref[...] = jnp.where(pl.program_id(1) == 0, acc * sig, sig).astype(o_ref.dtype)


def _proj_gates(xb, wT, tn, tm=GATES_TM):
    T, D = xb.shape
    N = wT.shape[0]
    tm = min(tm, T)
    return pl.pallas_call(
        _proj_gates_kernel,
        grid=(T // tm, N // tn),
        in_specs=[pl.BlockSpec((tm, D), lambda i, j: (i, 0)),
                  pl.BlockSpec((tn, D), lambda i, j: (j, 0))],
        out_specs=pl.BlockSpec((tm, tn), lambda i, j: (i, j)),
        out_shape=jax.ShapeDtypeStruct((T, N), BF16),
        compiler_params=_cparams(2),
        name="proj_gates",
    )(xb, wT)


def _proj_t_kernel(wT_ref, x_ref, o_ref, *, scaled_rows, scale):
    acc = lax.dot_general(wT_ref[...], x_ref[...], (((1,), (1,)), ((), ())), preferred_element_type=F32)
    o_ref[:scaled_rows, :] = (acc[:scaled_rows] * scale).astype(o_ref.dtype)
    o_ref[scaled_rows:, :] = acc[scaled_rows:].astype(o_ref.dtype)


def _proj_t(xb, wT, B, S, out_dtype, scaled_rows, scale, tm=PROJ_TM):
    T, D = xb.shape
    N = wT.shape[0]
    tn = N
    tm = min(tm, S)
    nsb = S // tm
    return pl.pallas_call(
        functools.partial(_proj_t_kernel, scaled_rows=scaled_rows, scale=scale),
        grid=(T // tm, N // tn),
        in_specs=[pl.BlockSpec((tn, D), lambda i, j: (j, 0)),
                  pl.BlockSpec((tm, D), lambda i, j: (i, 0))],
        out_specs=pl.BlockSpec((pl.Squeezed(), tn, tm), lambda i, j: (i // nsb, j, i % nsb)),
        out_shape=jax.ShapeDtypeStruct((B, N, S), out_dtype),
        compiler_params=_cparams(2),
        name="proj_t",
    )(wT, xb)


def _proj_rope_kernel(x_ref, w_ref, wr_ref, cos_ref, sin_ref, o_ref):
    x = x_ref[...]
    a = _x_wt(x, w_ref[...])
    r = _x_wt(x, wr_ref[...])
    reps = a.shape[1] // LANES
    cos = jnp.concatenate([cos_ref[...]] * reps, axis=1)
    sin = jnp.concatenate([sin_ref[...]] * reps, axis=1)
    o_ref[...] = (a * cos + r * sin).astype(o_ref.dtype)


def _proj_rope(xb, wT, wT_rot, cos, sin, tm=PROJ_TM):
    T, D = xb.shape
    N = wT.shape[0]
    tn = N
    tm = min(tm, T)
    return pl.pallas_call(
        _proj_rope_kernel,
        grid=(T // tm, N // tn),
        in_specs=[pl.BlockSpec((tm, D), lambda i, j: (i, 0)),
                  pl.BlockSpec((tn, D), lambda i, j: (j, 0)),
                  pl.BlockSpec((tn, D), lambda i, j: (j, 0)),
                  pl.BlockSpec((tm, LANES), lambda i, j: (i, 0)),
                  pl.BlockSpec((tm, LANES), lambda i, j: (i, 0))],
        out_specs=pl.BlockSpec((tm, tn), lambda i, j: (i, j)),
        out_shape=jax.ShapeDtypeStruct((T, N), BF16),
        compiler_params=_cparams(2),
        name="proj_rope",
    )(xb, wT, wT_rot, cos, sin)


def _proj_idx_kernel(x_ref, w_ref, g_ref, b_ref, ik_ref, iwT_ref, xb_ref, *, iw_scale):
    xb = x_ref[...].astype(BF16)
    xb_ref[...] = xb
    acc = _x_wt(xb, w_ref[...])
    lane = lax.broadcasted_iota(jnp.int32, acc.shape, 1)
    is_k = lane < IDX_HEAD_DIM
    mu = jnp.sum(jnp.where(is_k, acc, 0.0), axis=-1, keepdims=True) / IDX_HEAD_DIM
    d = acc - mu
    var = jnp.sum(jnp.where(is_k, d * d, 0.0), axis=-1, keepdims=True) / IDX_HEAD_DIM
    ln = d * lax.rsqrt(var + LN_EPS) * g_ref[...] + b_ref[...]
    ik_ref[...] = ln[:, :IDX_HEAD_DIM].astype(ik_ref.dtype)
    iwT_ref[...] = (acc * iw_scale).T[IDX_HEAD_DIM:IDX_HEAD_DIM + SUBLANES, :]


def _proj_idx(x, w_pad, g_pad, b_pad, iw_scale, B, S, tm=PROJ_TM):
    T, D = x.shape
    tm = min(tm, S)
    nsb = S // tm
    return pl.pallas_call(
        functools.partial(_proj_idx_kernel, iw_scale=iw_scale),
        grid=(T // tm,),
        in_specs=[pl.BlockSpec((tm, D), lambda i: (i, 0)),
                  pl.BlockSpec((LANES, D), lambda i: (0, 0)),
                  pl.BlockSpec((1, LANES), lambda i: (0, 0)),
                  pl.BlockSpec((1, LANES), lambda i: (0, 0))],
        out_specs=[pl.BlockSpec((tm, IDX_HEAD_DIM), lambda i: (i, 0)),
                   pl.BlockSpec((pl.Squeezed(), SUBLANES, tm), lambda i: (i // nsb, 0, i % nsb)),
                   pl.BlockSpec((tm, D), lambda i: (i, 0))],
        out_shape=[jax.ShapeDtypeStruct((T, IDX_HEAD_DIM), BF16),
                   jax.ShapeDtypeStruct((B, SUBLANES, S), F32),
                   jax.ShapeDtypeStruct((T, D), BF16)],
        compiler_params=_cparams(1),
        name="proj_idx",
    )(x, w_pad, g_pad, b_pad)


def _t5_bucket_table():
    n = np.arange(BIAS_TABLE_N)
    max_exact = NUM_BUCKETS // 2
    nf = np.maximum(n, 1).astype(np.float64)
    large = max_exact + (np.log(nf / max_exact) / math.log(MAX_DISTANCE / max_exact)
                         * (NUM_BUCKETS - max_exact)).astype(np.int64)
    large = np.minimum(large, NUM_BUCKETS - 1)
    bucket = np.where(n < max_exact, n, large)
    assert np.all(bucket[FAR_N:] == NUM_BUCKETS - 1) and bucket[FAR_N - 1] != NUM_BUCKETS - 1
    return bucket.astype(np.int32)


def _fold_rows(a, op):
    parts = [a[r:r + SUBLANES] for r in range(0, a.shape[0], SUBLANES)]
    while len(parts) > 1:
        nxt = [op(parts[k], parts[k + 1]) for k in range(0, len(parts) - 1, 2)]
        if len(parts) % 2:
            nxt.append(parts[-1])
        parts = nxt
    return parts[0]


def _attn_kernel(biasc_ref, pinfo_ref, ckmax_ref, iqT_ref, iwT_ref, posqc_ref, posk_ref, ik_ref, qT_ref, k_ref, vT_ref,
                 tbl_ref, toep_ref, tril_ref, o_ref, sc_ref, s_ref, s2_ref, b_ref, qm_ref, m_ref, l_ref, *acc_refs, k_sel, seq):
    i = pl.program_id(1)
    nch = (i * TQ + TQ + CK - 1) // CK
    q_idx = i * TQ + lax.broadcasted_iota(jnp.int32, (1, TQ), 1)
    kf = float(k_sel)

    def chunk_off(c):
        return pl.multiple_of(c * CK, CK)

    def key_idx(off):
        return off + lax.broadcasted_iota(jnp.int32, (CK, TQ), 0)

    def col_reduce(part, op):
        return op(part, axis=0, keepdims=True)

    iqT = iqT_ref[...]
    iwT = iwT_ref[...]
    iq_wide = jnp.concatenate([iqT[h * IDX_HEAD_DIM:(h + 1) * IDX_HEAD_DIM, :] for h in range(IDX_HEADS)], axis=1)

    def score_matmul(c, z_buf):
        ikc = ik_ref[pl.ds(chunk_off(c), CK), :]
        z = jnp.dot(ikc, iq_wide, preferred_element_type=F32)
        for h in range(IDX_HEADS):
            z_buf[h] = z[:, h * TQ:(h + 1) * TQ]

    def score_finish(c, z_buf, carry, masked):
        mn, mx = carry
        off = chunk_off(c)
        s = None
        for h in range(IDX_HEADS):
            t = jnp.maximum(z_buf[h], 0.0) * iwT[h:h + 1, :]
            s = t if s is None else s + t
        if masked:
            causal = key_idx(off) <= q_idx
            s_lo = jnp.where(causal, s, -jnp.inf)
            s_hi = jnp.where(causal, s, jnp.inf)
        else:
            s_lo = s_hi = s
        sc_ref[pl.ds(off, CK), :] = s_lo
        mn = jnp.minimum(mn, _fold_rows(s_hi, jnp.minimum))
        mx = jnp.maximum(mx, _fold_rows(s_lo, jnp.maximum))
        return mn, mx

    def score_chunk(c, carry, masked):
        score_matmul(c, s_ref)
        return score_finish(c, s_ref, carry, masked)

    n_inner = nch - 1
    n_pairs_sc = n_inner // 2

    @pl.when(n_pairs_sc > 0)
    def _():
        score_matmul(0, s_ref)

    def score_step(t, carry):
        c = 2 * t
        score_matmul(c + 1, s2_ref)
        carry = score_finish(c, s_ref, carry, False)
        score_matmul(jnp.minimum(c + 2, 2 * n_pairs_sc - 2), s_ref)
        return score_finish(c + 1, s2_ref, carry, False)

    mn8, mx8 = lax.fori_loop(0, n_pairs_sc, score_step,
                             (jnp.full((SUBLANES, TQ), jnp.inf, F32), jnp.full((SUBLANES, TQ), -jnp.inf, F32)))
    mn8, mx8 = lax.fori_loop(2 * n_pairs_sc, n_inner, functools.partial(score_chunk, masked=False), (mn8, mx8))
    mn8, mx8 = score_chunk(nch - 1, (mn8, mx8), masked=True)
    mn = col_reduce(mn8, jnp.min)
    mx = col_reduce(mx8, jnp.max)

    def over_chunks(fn, init):
        def one(c, carry):
            return fn(sc_ref[pl.ds(chunk_off(c), CK), :], carry)

        carry = lax.fori_loop(0, nch // 2, lambda t, carry: one(2 * t + 1, one(2 * t, carry)), init)
        return lax.fori_loop(2 * (nch // 2), nch, one, carry)

    def count(pred_fn):
        acc = over_chunks(lambda blk, acc: acc + _fold_rows(jnp.where(pred_fn(blk), 1.0, 0.0), jnp.add),
                          jnp.zeros((SUBLANES, TQ), F32))
        return col_reduce(acc, jnp.sum)

    def bisect_round(_, st):
        lo, hi, c_lo = st
        mid = 0.5 * (lo + hi)
        c = count(lambda blk: blk >= mid)
        ok = c >= kf
        return jnp.where(ok, mid, lo), jnp.where(ok, hi, mid), jnp.where(ok, c, c_lo)

    c_all = (q_idx + 1).astype(F32)
    lo, hi, c_lo = lax.fori_loop(0, BISECT_ROUNDS, bisect_round, (mn, mx, c_all))

    cur0 = col_reduce(over_chunks(
        lambda blk, acc: jnp.minimum(acc, _fold_rows(jnp.where(blk >= lo, blk, jnp.inf), jnp.minimum)),
        jnp.full((SUBLANES, TQ), jnp.inf, F32)), jnp.min)

    def walk_cond(st):
        return st[3] > 0.0

    def walk_body(st):
        cur, c_ge, _, _ = st

        def body(blk, carry):
            cnt, nxt = carry
            gt = blk > cur
            cnt = cnt + _fold_rows(jnp.where(gt, 1.0, 0.0), jnp.add)
            nxt = jnp.minimum(nxt, _fold_rows(jnp.where(gt, blk, jnp.inf), jnp.minimum))
            return cnt, nxt

        cnt, nxt = over_chunks(body, (jnp.zeros((SUBLANES, TQ), F32), jnp.full((SUBLANES, TQ), jnp.inf, F32)))
        c_gt = col_reduce(cnt, jnp.sum)
        nxt = col_reduce(nxt, jnp.min)
        adv = c_gt >= kf
        cur = jnp.where(adv, nxt, cur)
        c_ge = jnp.where(adv, c_gt, c_ge)
        return cur, c_ge, c_gt, jnp.max(jnp.where(adv, 1.0, 0.0))

    tau, c_ge, c_gt, _ = lax.while_loop(
        walk_cond, walk_body, (cur0, c_lo, jnp.zeros((1, TQ), F32), jnp.float32(1.0)))

    room = kf - c_gt

    def selection_mask(c, seen):
        off = chunk_off(c)
        tiles = range(CK // LANES)
        blks = [sc_ref[pl.ds(off + j * LANES, LANES), :] for j in tiles]
        ties = [blk == tau for blk in blks]
        local = [jnp.dot(tril_ref[...], jnp.where(tie, 1.0, 0.0).astype(BF16), preferred_element_type=F32)
                 for tie in ties]
        masks = []
        for j in tiles:
            keep = (blks[j] > tau) | (ties[j] & (local[j] + seen <= room))
            masks.append(jnp.where(keep, 0.0, NEG))
            seen = seen + local[j][LANES - 1:LANES, :]
        return jnp.concatenate(masks, axis=0), seen

    m_ref[...] = jnp.full(m_ref.shape, NEG, F32)
    l_ref[...] = jnp.zeros(l_ref.shape, F32)
    for acc in acc_refs:
        acc[...] = jnp.zeros(acc.shape, F32)

    rowi = lax.broadcasted_iota(jnp.int32, (LANES, TQ), 0)
    for pair in range(ATTN_HEADS // 2):
        qp = qT_ref[pair * LANES:(pair + 1) * LANES, :]
        zero = jnp.zeros_like(qp)
        qm_ref[pair] = jnp.concatenate([jnp.where(rowi < ATTN_HEAD_DIM, qp, zero),
                                        jnp.where(rowi >= ATTN_HEAD_DIM, qp, zero)], axis=1)
    ones_rows = jnp.ones((2 * SUBLANES, CK), BF16)

    pq_col = posqc_ref[...]
    batch = pl.program_id(0)
    pq_first, pq_consec, pq_min = (pinfo_ref[batch, r, i] for r in (0, 1, 3))
    far_bias = [biasc_ref[0, h] for h in range(ATTN_HEADS)]
    zero_bias = [biasc_ref[1, h] for h in range(ATTN_HEADS)]
    q_ge_k = (lax.broadcasted_iota(jnp.int32, (LANES, TQ), 1) >= lax.broadcasted_iota(jnp.int32, (LANES, TQ), 0))
    n_sub = CK // LANES
    n_pairs = ATTN_HEADS // 2

    def chunk_is_far(c):
        return (pq_min - ckmax_ref[batch, jnp.minimum(c, seq // CK - 1)]) >= FAR_N

    def stage_bias(c):
        off = chunk_off(c)
        pk_row = posk_ref[:, pl.ds(off, CK)]
        for j in range(n_sub):
            rows = slice(j * LANES, (j + 1) * LANES)
            g = c * n_sub + j
            pk_first, pk_consec, pk_max = (pinfo_ref[batch, r, g] for r in (0, 1, 2))
            all_far = (pq_min - pk_max) >= FAR_N
            all_masked = (off + j * LANES) > (i * TQ + TQ - 1)
            consecutive = (pq_consec > 0) & (pk_consec > 0)
            gap = pq_first - pk_first

            def fill_const(rows=rows):
                for h in range(ATTN_HEADS):
                    b_ref[h, rows, :] = jnp.full((LANES, TQ), far_bias[h], F32)

            def fill_gap0(rows=rows):
                for h in range(ATTN_HEADS):
                    b_ref[h, rows, :] = jnp.where(q_ge_k, toep_ref[h], zero_bias[h])

            def fill_gap128(rows=rows):
                for h in range(ATTN_HEADS):
                    b_ref[h, rows, :] = jnp.where(q_ge_k, far_bias[h], toep_ref[h])

            def fill_lookup(rows=rows):
                pk_sub = pk_row[:, rows]
                n_qk = jnp.clip(pq_col - pk_sub, 0, BIAS_TABLE_N - 1).astype(F32)
                n_kq = n_qk.T.astype(jnp.int32)
                for h in range(ATTN_HEADS):
                    tb = jnp.broadcast_to(tbl_ref[h:h + 1, :], (LANES, BIAS_TABLE_N))
                    b_ref[h, rows, :] = jnp.take_along_axis(tb, n_kq, axis=1)

            def fill_near(fill_gap0=fill_gap0, fill_gap128=fill_gap128, fill_lookup=fill_lookup,
                          consecutive=consecutive, gap=gap):
                lax.cond(consecutive & (gap == 0), fill_gap0,
                         lambda: lax.cond(consecutive & (gap == LANES), fill_gap128, fill_lookup))

            lax.cond(all_far | all_masked, fill_const, fill_near)

    def logits_phase(c, s_buf, const_bias, seen):
        off = chunk_off(c)
        mb, seen = selection_mask(c, seen)
        m_cur = []
        for pair in range(n_pairs):
            kc = k_ref[pl.ds(off, CK), pair * LANES:(pair + 1) * LANES]
            s2 = jnp.dot(kc, qm_ref[pair], preferred_element_type=F32)
            for sub, h in enumerate((2 * pair, 2 * pair + 1)):
                s = s2[:, sub * TQ:(sub + 1) * TQ] + mb
                if not const_bias:
                    s = s + b_ref[h]
                s_buf[h] = s
                top = col_reduce(_fold_rows(s, jnp.maximum), jnp.max)
                m_cur.append(top + far_bias[h] if const_bias else top)
        return m_cur, seen

    def update_phase(c, s_buf, m_cur, const_bias):
        off = chunk_off(c)
        m_prev = m_ref[...]
        l_prev = l_ref[...]
        if isinstance(m_cur, list):
            m_new = [jnp.maximum(m_prev[h:h + 1, :], m_cur[h]) for h in range(ATTN_HEADS)]
            alpha = [jnp.exp2(m_prev[h:h + 1, :] - m_new[h]) for h in range(ATTN_HEADS)]
        else:
            m_all = jnp.maximum(m_prev, m_cur)
            a_all = jnp.exp2(m_prev - m_all)
            m_new = [m_all[h:h + 1, :] for h in range(ATTN_HEADS)]
            alpha = [a_all[h:h + 1, :] for h in range(ATTN_HEADS)]
        l_new = []
        for pair in range(n_pairs):
            heads = (2 * pair, 2 * pair + 1)
            sub_m = [m_new[h] - far_bias[h] if const_bias else m_new[h] for h in heads]
            p2 = jnp.concatenate([jnp.exp2(s_buf[h] - sm).astype(BF16) for h, sm in zip(heads, sub_m)],
                                 axis=1)
            lhs = jnp.concatenate([vT_ref[pair * LANES:(pair + 1) * LANES, pl.ds(off, CK)], ones_rows],
                                  axis=0)
            out = jnp.dot(lhs, p2, preferred_element_type=F32)
            for sub, h in enumerate(heads):
                cols = slice(sub * TQ, (sub + 1) * TQ)
                acc = acc_refs[h]
                acc[...] = alpha[h] * acc[...] + out[sub * ATTN_HEAD_DIM:(sub + 1) * ATTN_HEAD_DIM, cols]
                l_new.append(alpha[h] * l_prev[h:h + 1, :] + out[LANES:LANES + 1, cols])
        l_ref[...] = jnp.concatenate(l_new, axis=0)
        m_ref[...] = jnp.concatenate(m_new, axis=0)

    n_far = lax.while_loop(lambda c: (c < nch) & chunk_is_far(c), lambda c: c + 1, jnp.int32(0))
    n_steps = n_far // 2
    def far_logits(c, s_buf, seen):
        m_cur, seen = logits_phase(c, s_buf, True, seen)
        return jnp.concatenate(m_cur, axis=0), seen

    no_ties = jnp.zeros((1, TQ), F32)
    m_first, seen_first = lax.cond(n_steps > 0, lambda: far_logits(0, s_ref, no_ties),
                                   lambda: (jnp.zeros((ATTN_HEADS, TQ), F32), no_ties))

    def far_step(t, carry):
        m_even, seen_even, _ = carry
        c = 2 * t
        m_odd, seen_odd = far_logits(c + 1, s2_ref, seen_even)
        update_phase(c, s_ref, m_even, True)
        m_even, seen_even = far_logits(jnp.minimum(c + 2, n_far - 1), s_ref, seen_odd)
        update_phase(c + 1, s2_ref, m_odd, True)
        return m_even, seen_even, seen_odd

    m_even, seen_even, seen = lax.fori_loop(0, n_steps, far_step, (m_first, seen_first, no_ties))
    leftover_staged = (n_steps > 0) & (n_far % 2 == 1)

    def finish_leftover():
        update_phase(n_far - 1, s_ref, m_even, True)
        return seen_even

    seen = lax.cond(leftover_staged, finish_leftover, lambda: seen)
    n_done = jnp.where(leftover_staged, n_far, 2 * n_steps)

    def tail_chunk(c, seen):
        def run(const_bias):
            if not const_bias:
                stage_bias(c)
            m_cur, seen_out = logits_phase(c, s_ref, const_bias, seen)
            update_phase(c, s_ref, m_cur, const_bias)
            return seen_out

        return lax.cond(chunk_is_far(c), lambda: run(True), lambda: run(False))

    lax.fori_loop(n_done, nch, tail_chunk, seen)

    outT = jnp.concatenate([acc_refs[h][...] / l_ref[h:h + 1, :] for h in range(ATTN_HEADS)], axis=0)
    o_ref[...] = outT.T.astype(o_ref.dtype)


def _toeplitz_kernel(tbl_ref, o_ref):
    q = lax.broadcasted_iota(jnp.int32, (LANES, LANES), 1)
    k = lax.broadcasted_iota(jnp.int32, (LANES, LANES), 0)
    idx = (q - k) & (BIAS_TABLE_N - 1)
    for h in range(ATTN_HEADS):
        tb = jnp.broadcast_to(tbl_ref[h:h + 1, :], (LANES, BIAS_TABLE_N))
        o_ref[h] = jnp.take_along_axis(tb, idx, axis=1)


def _sparse_attention(fm, tokb, ik, iwT, positions, rel_bias):
    B, S, _ = tokb.shape
    k_sel = min(TOPK_MAX, S // 4)
    bucket = _t5_bucket_table()
    tbl = rel_bias[bucket].T.astype(F32) * LOG2E
    bias_c = jnp.stack([tbl[:, BIAS_TABLE_N - 1], tbl[:, 0]])
    toep = pl.pallas_call(
        _toeplitz_kernel,
        out_shape=jax.ShapeDtypeStruct((ATTN_HEADS, LANES, LANES), F32),
        name="bias_toeplitz",
    )(tbl)
    pos_row = positions.reshape(B, 1, S)
    pos_col = positions.reshape(B, S, 1)
    pos_t = positions.reshape(B, S // LANES, LANES)
    consec = jnp.all(pos_t - pos_t[:, :, :1] == jnp.arange(LANES, dtype=positions.dtype), axis=-1)
    pinfo = jnp.stack([pos_t[:, :, 0], consec.astype(jnp.int32), jnp.max(pos_t, axis=-1),
                       jnp.min(pos_t, axis=-1)], axis=1).astype(jnp.int32)
    ck_max = jnp.max(positions.reshape(B, S // CK, CK), axis=-1)
    tril = jnp.tril(jnp.ones((LANES, LANES), BF16))
    sq = pl.Squeezed()
    iq_blk = (2 * ATTN_W) // IDX_Q_W
    k_blk = (tokb.shape[2] - ATTN_W) // ATTN_W
    return pl.pallas_call(
        functools.partial(_attn_kernel, k_sel=k_sel, seq=S),
        grid=(B, S // TQ),
        in_specs=[pl.BlockSpec(memory_space=pltpu.SMEM),
                  pl.BlockSpec(memory_space=pltpu.SMEM),
                  pl.BlockSpec(memory_space=pltpu.SMEM),
                  pl.BlockSpec((sq, IDX_Q_W, TQ), lambda b, i: (b, iq_blk, i)),
                  pl.BlockSpec((sq, SUBLANES, TQ), lambda b, i: (b, 0, i)),
                  pl.BlockSpec((sq, TQ, 1), lambda b, i: (b, i, 0)),
                  pl.BlockSpec((sq, 1, S), lambda b, i: (b, 0, 0)),
                  pl.BlockSpec((sq, S, IDX_HEAD_DIM), lambda b, i: (b, 0, 0)),
                  pl.BlockSpec((sq, ATTN_W, TQ), lambda b, i: (b, 0, i)),
                  pl.BlockSpec((sq, S, ATTN_W), lambda b, i: (b, 0, k_blk)),
                  pl.BlockSpec((sq, ATTN_W, S), lambda b, i: (b, 1, 0)),
                  pl.BlockSpec((ATTN_HEADS, BIAS_TABLE_N), lambda b, i: (0, 0)),
                  pl.BlockSpec((ATTN_HEADS, LANES, LANES), lambda b, i: (0, 0, 0)),
                  pl.BlockSpec((LANES, LANES), lambda b, i: (0, 0))],
        out_specs=pl.BlockSpec((sq, TQ, ATTN_W), lambda b, i: (b, i, 0)),
        out_shape=jax.ShapeDtypeStruct((B, S, ATTN_W), BF16),
        scratch_shapes=[pltpu.VMEM((S, TQ), F32),
                        pltpu.VMEM((ATTN_HEADS, CK, TQ), F32),
                        pltpu.VMEM((ATTN_HEADS, CK, TQ), F32),
                        pltpu.VMEM((ATTN_HEADS, CK, TQ), F32),
                        pltpu.VMEM((ATTN_HEADS // 2, LANES, 2 * TQ), BF16),
                        pltpu.VMEM((ATTN_HEADS, TQ), F32),
                        pltpu.VMEM((ATTN_HEADS, TQ), F32)]
                       + [pltpu.VMEM((ATTN_HEAD_DIM, TQ), F32)] * ATTN_HEADS,
        compiler_params=_cparams(2),
        name="sparse_attention",
    )(bias_c, pinfo, ck_max, fm, iwT, pos_col, pos_row, ik, fm, tokb, fm, tbl, toep, tril)


def _retention_kernel(q_ref, k_ref, v_ref, g_ref, decay_ref, xi_ref, zeta_ref, gch_ref, o_ref, r_ref):
    @pl.when(pl.program_id(0) == 0)
    def _():
        r_ref[...] = jnp.zeros(r_ref.shape, F32)

    C = RET_CHUNK
    lane = lax.broadcasted_iota(jnp.int32, (C, LANES), 1)
    row = lax.broadcasted_iota(jnp.int32, (LANES, RET_V_DIM), 0)
    for ci in range(q_ref.shape[1] // C):
        rows = slice(ci * C, (ci + 1) * C)
        for pair in range(RET_HEADS // 2):
            for b in range(q_ref.shape[0]):
                q_pair = q_ref[b, rows, pair * LANES:(pair + 1) * LANES]
                k_pair = k_ref[b, rows, pair * LANES:(pair + 1) * LANES]
                v_pair = v_ref[b, rows, 2 * pair * RET_V_DIM:(2 * pair + 2) * RET_V_DIM]
                r_pair = r_ref[b, pair]
                r_bf = r_pair.astype(BF16)
                for sub in range(2):
                    h = 2 * pair + sub
                    in_head = (lane >= sub * RET_QK_DIM) & (lane < (sub + 1) * RET_QK_DIM)
                    qm = jnp.where(in_head, q_pair, jnp.zeros_like(q_pair))
                    v_h = v_pair[:, sub * RET_V_DIM:(sub + 1) * RET_V_DIM]
                    inner = lax.dot_general(qm, k_pair, (((1,), (1,)), ((), ())),
                                            preferred_element_type=F32) * decay_ref[h]
                    o = (jnp.dot(inner.astype(BF16), v_h, preferred_element_type=F32)
                         + jnp.dot(qm, r_bf, preferred_element_type=F32) * xi_ref[h])
                    mu = jnp.mean(o, axis=-1, keepdims=True)
                    d = o - mu
                    var = jnp.mean(d * d, axis=-1, keepdims=True)
                    hn = d * lax.rsqrt(var + LN_EPS)
                    gate = g_ref[b, rows, h * RET_V_DIM:(h + 1) * RET_V_DIM].astype(F32)
                    o_ref[b, rows, h * RET_V_DIM:(h + 1) * RET_V_DIM] = (gate * hn).astype(o_ref.dtype)
                kz = (k_pair.astype(F32) * zeta_ref[pair]).astype(BF16)
                upd = lax.dot_general(kz, v_pair, (((0,), (0,)), ((), ())),
                                      preferred_element_type=F32)
                r_ref[b, pair] = (r_pair * gch_ref[pair]
                                  + jnp.where(row < RET_QK_DIM, upd[:, :RET_V_DIM], upd[:, RET_V_DIM:]))


def _retention(qk, tokb, gates, B, S):
    C = RET_CHUNK
    H = RET_HEADS
    nc = S // C
    gamma = 1.0 - 2.0 ** (-5.0 - jnp.arange(H, dtype=F32))
    log_g = jnp.log(gamma)
    n = jnp.arange(C, dtype=F32)
    diff = n[:, None] - n[None, :]
    decay_in = jnp.where(diff[None] >= 0, jnp.exp(log_g[:, None, None] * jnp.maximum(diff, 0.0)[None]), 0.0)
    xi = jnp.exp(log_g[None, :] * (n[:, None] + 1.0))
    zeta = jnp.exp(log_g[None, :] * (C - 1.0 - n[:, None]))
    g_chunk = jnp.exp(log_g * C)
    xi_b = jnp.broadcast_to(xi.T[:, :, None], (H, C, RET_V_DIM))
    zeta_b = jnp.repeat(zeta, RET_QK_DIM, axis=1).reshape(C, H // 2, LANES).transpose(1, 0, 2)
    gch_b = jnp.broadcast_to(jnp.repeat(g_chunk, RET_QK_DIM).reshape(H // 2, LANES, 1),
                             (H // 2, LANES, RET_V_DIM))
    qk3, tok3, gate3 = (a.reshape(B, S, a.shape[-1]) for a in (qk, tokb, gates))
    CS = C * RET_STEP_CHUNKS
    out = pl.pallas_call(
        _retention_kernel,
        grid=(nc // RET_STEP_CHUNKS,),
        in_specs=[pl.BlockSpec((B, CS, RET_QK_W), lambda i: (0, i, 0)),
                  pl.BlockSpec((B, CS, RET_QK_W), lambda i: (0, i, 1)),
                  pl.BlockSpec((B, CS, RET_V_W), lambda i: (0, i, 0)),
                  pl.BlockSpec((B, CS, RET_V_W), lambda i: (0, i, 0)),
                  pl.BlockSpec((H, C, C), lambda i: (0, 0, 0)),
                  pl.BlockSpec((H, C, RET_V_DIM), lambda i: (0, 0, 0)),
                  pl.BlockSpec((H // 2, C, LANES), lambda i: (0, 0, 0)),
                  pl.BlockSpec((H // 2, LANES, RET_V_DIM), lambda i: (0, 0, 0))],
        out_specs=pl.BlockSpec((B, CS, RET_V_W), lambda i: (0, i, 0)),
        out_shape=jax.ShapeDtypeStruct((B, S, RET_V_W), BF16),
        scratch_shapes=[pltpu.VMEM((B, H // 2, LANES, RET_V_DIM), F32)],
        compiler_params=_cparams(1),
        name="retention",
    )(qk3, qk3, tok3, gate3, decay_in, xi_b, zeta_b, gch_b)
    return out.reshape(B * S, RET_V_W)


def _layer_norm(z, g, b):
    mu = jnp.mean(z, axis=-1, keepdims=True)
    d = z - mu
    var = jnp.mean(d * d, axis=-1, keepdims=True)
    return d * lax.rsqrt(var + LN_EPS) * g + b


def _merge_kernel(x_ref, ya_ref, yr_ref, ga_ref, gr_ref, wa_ref, wr_ref, wo_ref, g_ref, b_ref,
                  x1_ref):
    a = jnp.dot(ya_ref[...], wa_ref[...], preferred_element_type=F32)
    r = jnp.dot(yr_ref[...], wr_ref[...], preferred_element_type=F32)
    h = ga_ref[...].astype(F32) * a + gr_ref[...].astype(F32) * r
    mix = jnp.dot(h.astype(BF16), wo_ref[...], preferred_element_type=F32)
    x1_ref[...] = _layer_norm(DEEPNORM_ALPHA * x_ref[...] + mix, g_ref[...], b_ref[...])


def _merge(x, ya, yr, gates, wa, wr, wo, g, b, tm=MERGE_TM):
    T, D = x.shape
    tm = min(tm, T)
    row = lambda i: (i, 0)
    fixed = lambda i: (0, 0)
    return pl.pallas_call(
        _merge_kernel,
        grid=(T // tm,),
        in_specs=[pl.BlockSpec((tm, D), row),
                  pl.BlockSpec((tm, ya.shape[1]), row),
                  pl.BlockSpec((tm, yr.shape[1]), row),
                  pl.BlockSpec((tm, D), lambda i: (i, 1)),
                  pl.BlockSpec((tm, D), lambda i: (i, 2)),
                  pl.BlockSpec(wa.shape, fixed),
                  pl.BlockSpec(wr.shape, fixed),
                  pl.BlockSpec(wo.shape, fixed),
                  pl.BlockSpec((1, D), fixed),
                  pl.BlockSpec((1, D), fixed)],
        out_specs=pl.BlockSpec((tm, D), row),
        out_shape=jax.ShapeDtypeStruct((T, D), F32),
        compiler_params=_cparams(1),
        name="merge",
    )(x, ya, yr, gates, gates, wa, wr, wo, g, b)


def _ffn_kernel(x1_ref, wu_ref, wd_ref, g_ref, b_ref, o_ref, acc_ref, xb_ref):
    f = pl.program_id(1)

    @pl.when(f == 0)
    def _():
        acc_ref[...] = jnp.zeros(acc_ref.shape, F32)
        xb_ref[...] = x1_ref[...].astype(BF16)

    hid = jnp.maximum(jnp.dot(xb_ref[...], wu_ref[...], preferred_element_type=F32), 0.0)
    acc_ref[...] += jnp.dot((hid * hid).astype(BF16), wd_ref[...], preferred_element_type=F32)

    @pl.when(f == pl.num_programs(1) - 1)
    def _():
        o_ref[...] = _layer_norm(DEEPNORM_ALPHA * x1_ref[...] + acc_ref[...], g_ref[...], b_ref[...])


def _ffn(x1, wu, wd, g, b, tm=FFN_TM, tf=FFN_TF):
    T, D = x1.shape
    F = wu.shape[1]
    tm = min(tm, T)
    return pl.pallas_call(
        _ffn_kernel,
        grid=(T // tm, F // tf),
        in_specs=[pl.BlockSpec((tm, D), lambda i, f: (i, 0)),
                  pl.BlockSpec((D, tf), lambda i, f: (0, f)),
                  pl.BlockSpec((tf, D), lambda i, f: (f, 0)),
                  pl.BlockSpec((1, D), lambda i, f: (0, 0)),
                  pl.BlockSpec((1, D), lambda i, f: (0, 0))],
        out_specs=pl.BlockSpec((tm, D), lambda i, f: (i, 0)),
        out_shape=jax.ShapeDtypeStruct((T, D), F32),
        scratch_shapes=[pltpu.VMEM((tm, D), F32), pltpu.VMEM((tm, D), BF16)],
        compiler_params=_cparams(2),
        name="ffn",
    )(x1, wu, wd, g, b)


def _rot_half_weight(wT):
    N, D = wT.shape
    half = RET_QK_DIM // 2
    wh = wT.reshape(N // RET_QK_DIM, 2, half, D)
    return jnp.stack([-wh[:, 1], wh[:, 0]], axis=1).reshape(N, D)


def kernel(x, positions, w_in, rel_bias, idx_k_ln_g, idx_k_ln_b, w_attn_branch, w_ret_branch,
           w_out, ln_mix_g, ln_mix_b, w_up, w_down, ln_ffn_g, ln_ffn_b):
    B, S, D = x.shape
    T = B * S
    sizes = (ATTN_W, ATTN_W, ATTN_W, IDX_Q_W, IDX_HEAD_DIM, IDX_HEADS,
             RET_QK_W, RET_QK_W, RET_V_W, RET_V_W, D, D)
    offs = [0] + [int(o) for o in np.cumsum(sizes)]
    cos, sin = _rope_tables(positions)
    xf = x.reshape(T, D)
    for l in range(DEPTH):
        wT = jnp.swapaxes(w_in[l], 0, 1).astype(BF16)
        rows = [wT[offs[k]:offs[k + 1]] for k in range(len(sizes))]
        (w_qa, w_ka, w_va, w_iq, w_ik, w_iw, w_qr, w_kr, w_vr, w_gr, w_ga, w_gtr) = rows
        w_kr = w_kr * (RET_QK_DIM ** -0.5)

        pad = LANES - IDX_HEAD_DIM - IDX_HEADS
        w_idx = jnp.concatenate([w_ik, w_iw, jnp.zeros((pad, D), BF16)], axis=0)
        g_pad = jnp.concatenate([idx_k_ln_g[l], jnp.zeros((LANES - IDX_HEAD_DIM,), F32)]).reshape(1, LANES)
        b_pad = jnp.concatenate([idx_k_ln_b[l], jnp.zeros((LANES - IDX_HEAD_DIM,), F32)]).reshape(1, LANES)
        ik, iwT, xb = _proj_idx(xf, w_idx, g_pad, b_pad, (IDX_HEAD_DIM ** -0.5) * (IDX_HEADS ** -0.5), B, S)
        ik = ik.reshape(B, S, IDX_HEAD_DIM)

        fm = _proj_t(xb, jnp.concatenate([w_qa, w_va, w_iq], axis=0), B, S, BF16,
                     scaled_rows=ATTN_W, scale=ATTN_HEAD_DIM ** -0.5 * LOG2E)
        tokb = _proj(xb, jnp.concatenate([w_vr, w_ka], axis=0), BF16)
        gates = _proj_gates(xb, jnp.concatenate([w_gr, w_ga, w_gtr], axis=0), tn=D)
        w_rope = jnp.concatenate([w_qr, w_kr], axis=0)
        w_rope_rot = jnp.concatenate([_rot_half_weight(w_qr), _rot_half_weight(w_kr)], axis=0)
        qk_r = _proj_rope(xb, w_rope, w_rope_rot, cos, sin)

        y_a = _sparse_attention(fm, tokb.reshape(B, S, -1), ik, iwT, positions, rel_bias)
        y_r = _retention(qk_r, tokb, gates, B, S)
        x1 = _merge(xf, y_a.reshape(T, ATTN_W), y_r, gates,
                    w_attn_branch[l].astype(BF16), w_ret_branch[l].astype(BF16),
                    w_out[l].astype(BF16), ln_mix_g[l].reshape(1, D), ln_mix_b[l].reshape(1, D))
        xf = _ffn(x1, w_up[l].astype(BF16), w_down[l].astype(BF16),
                  ln_ffn_g[l].reshape(1, D), ln_ffn_b[l].reshape(1, D))
    return xf.reshape(B, S, D)
```

```python
import functools
import math

import numpy as np
import jax
import jax.numpy as jnp
from jax import lax
from jax.experimental import pallas as pl
from jax.experimental.pallas import tpu as pltpu

F32 = jnp.float32
BF16 = jnp.bfloat16

ATTN_HEADS = 8
ATTN_HEAD_DIM = 64
ATTN_W = ATTN_HEADS * ATTN_HEAD_DIM
IDX_HEADS = 4
IDX_HEAD_DIM = 64
IDX_Q_W = IDX_HEADS * IDX_HEAD_DIM
TOPK_MAX = 256
RET_HEADS = 8
RET_QK_DIM = 64
RET_V_DIM = 128
RET_QK_W = RET_HEADS * RET_QK_DIM
RET_V_W = RET_HEADS * RET_V_DIM
RET_CHUNK = 128
RET_STEP_CHUNKS = 4
ROPE_BASE = 10000.0
NUM_BUCKETS = 32
MAX_DISTANCE = 128
LN_EPS = 1e-5
DEPTH = 1
DEEPNORM_ALPHA = (2.0 * DEPTH) ** 0.25

LANES = 128
SUBLANES = 8
VMEM_LIMIT = 56 * 1024 * 1024

PROJ_TM = 1024
GATES_TM = 2048
MERGE_TM = 512
FFN_TM = 1024
FFN_TF = 1024

TQ = 128
CK = 512
NEG = -1e30
LOG2E = math.log2(math.e)
BISECT_ROUNDS = 20
BIAS_TABLE_N = 128
FAR_N = 113


def _cparams(n_grid):
    return pltpu.CompilerParams(
        dimension_semantics=("arbitrary",) * n_grid,
        vmem_limit_bytes=VMEM_LIMIT)


def _trig_kernel(pos_ref, inv_ref, cos_ref, sin_ref):
    ang = pos_ref[...] * inv_ref[...]
    tr = ang.shape[0]
    half = RET_QK_DIM // 2
    per_row = LANES // half
    lane = lax.broadcasted_iota(jnp.int32, ang.shape, 1)
    for ref, val in ((cos_ref, jnp.cos(ang)), (sin_ref, jnp.sin(ang))):
        for k in range(per_row):
            own = jnp.where((lane >= k * half) & (lane < (k + 1) * half), val, 0.0)
            spread = own
            for j in range(1, per_row):
                spread = spread + pltpu.roll(own, j * half, 1)
            ref[pl.ds(k, tr, stride=per_row), :] = spread


def _rope_tables(positions):
    B, S = positions.shape
    half = RET_QK_DIM // 2
    inv = ROPE_BASE ** (-jnp.arange(half, dtype=F32) / half)
    per_row = LANES // half
    rows = B * S // per_row
    pos_e = jnp.repeat(positions.astype(F32).reshape(rows, per_row), half, axis=1)
    inv_e = jnp.tile(inv, per_row).reshape(1, LANES)
    tr = min(rows, 1024)
    return pl.pallas_call(
        _trig_kernel,
        grid=(rows // tr,),
        in_specs=[pl.BlockSpec((tr, LANES), lambda i: (i, 0)),
                  pl.BlockSpec((1, LANES), lambda i: (0, 0))],
        out_specs=[pl.BlockSpec((per_row * tr, LANES), lambda i: (i, 0))] * 2,
        out_shape=[jax.ShapeDtypeStruct((B * S, LANES), F32)] * 2,
        compiler_params=_cparams(1),
        name="rope_tables",
    )(pos_e, inv_e)


def _x_wt(x, wT):
    return lax.dot_general(x, wT, (((1,), (1,)), ((), ())), preferred_element_type=F32)


def _proj_kernel(x_ref, w_ref, o_ref):
    o_ref[...] = _x_wt(x_ref[...], w_ref[...]).astype(o_ref.dtype)


def _proj(xb, wT, out_dtype, tm=PROJ_TM):
    T, D = xb.shape
    N = wT.shape[0]
    tn = N
    tm = min(tm, T)
    return pl.pallas_call(
        _proj_kernel,
        grid=(T // tm, N // tn),
        in_specs=[pl.BlockSpec((tm, D), lambda i, j: (i, 0)),
                  pl.BlockSpec((tn, D), lambda i, j: (j, 0))],
        out_specs=pl.BlockSpec((tm, tn), lambda i, j: (i, j)),
        out_shape=jax.ShapeDtypeStruct((T, N), out_dtype),
        compiler_params=_cparams(2),
        name="proj",
    )(xb, wT)


def _proj_gates_kernel(x_ref, w_ref, o_ref):
    acc = _x_wt(x_ref[...], w_ref[...])
    sig = 0.5 * jnp.tanh(0.5 * acc) + 0.5
    o_ref[...] = jnp.where(pl.program_id(1) == 0, acc * sig, sig).astype(o_ref.dtype)


def _proj_gates(xb, wT, tn, tm=GATES_TM):
    T, D = xb.shape
    N = wT.shape[0]
    tm = min(tm, T)
    return pl.pallas_call(
        _proj_gates_kernel,
        grid=(T // tm, N // tn),
        in_specs=[pl.BlockSpec((tm, D), lambda i, j: (i, 0)),
                  pl.BlockSpec((tn, D), lambda i, j: (j, 0))],
        out_specs=pl.BlockSpec((tm, tn), lambda i, j: (i, j)),
        out_shape=jax.ShapeDtypeStruct((T, N), BF16),
        compiler_params=_cparams(2),
        name="proj_gates",
    )(xb, wT)


def _proj_t_kernel(wT_ref, x_ref, o_ref, *, scaled_rows, scale):
    acc = lax.dot_general(wT_ref[...], x_ref[...], (((1,), (1,)), ((), ())), preferred_element_type=F32)
    o_ref[:scaled_rows, :] = (acc[:scaled_rows] * scale).astype(o_ref.dtype)
    o_ref[scaled_rows:, :] = acc[scaled_rows:].astype(o_ref.dtype)


def _proj_t(xb, wT, B, S, out_dtype, scaled_rows, scale, tm=PROJ_TM):
    T, D = xb.shape
    N = wT.shape[0]
    tn = N
    tm = min(tm, S)
    nsb = S // tm
    return pl.pallas_call(
        functools.partial(_proj_t_kernel, scaled_rows=scaled_rows, scale=scale),
        grid=(T // tm, N // tn),
        in_specs=[pl.BlockSpec((tn, D), lambda i, j: (j, 0)),
                  pl.BlockSpec((tm, D), lambda i, j: (i, 0))],
        out_specs=pl.BlockSpec((pl.Squeezed(), tn, tm), lambda i, j: (i // nsb, j, i % nsb)),
        out_shape=jax.ShapeDtypeStruct((B, N, S), out_dtype),
        compiler_params=_cparams(2),
        name="proj_t",
    )(wT, xb)


def _proj_rope_kernel(x_ref, w_ref, wr_ref, cos_ref, sin_ref, o_ref):
    x = x_ref[...]
    a = _x_wt(x, w_ref[...])
    r = _x_wt(x, wr_ref[...])
    reps = a.shape[1] // LANES
    cos = jnp.concatenate([cos_ref[...]] * reps, axis=1)
    sin = jnp.concatenate([sin_ref[...]] * reps, axis=1)
    o_ref[...] = (a * cos + r * sin).astype(o_ref.dtype)


def _proj_rope(xb, wT, wT_rot, cos, sin, tm=PROJ_TM):
    T, D = xb.shape
    N = wT.shape[0]
    tn = N
    tm = min(tm, T)
    return pl.pallas_call(
        _proj_rope_kernel,
        grid=(T // tm, N // tn),
        in_specs=[pl.BlockSpec((tm, D), lambda i, j: (i, 0)),
                  pl.BlockSpec((tn, D), lambda i, j: (j, 0)),
                  pl.BlockSpec((tn, D), lambda i, j: (j, 0)),
                  pl.BlockSpec((tm, LANES), lambda i, j: (i, 0)),
                  pl.BlockSpec((tm, LANES), lambda i, j: (i, 0))],
        out_specs=pl.BlockSpec((tm, tn), lambda i, j: (i, j)),
        out_shape=jax.ShapeDtypeStruct((T, N), BF16),
        compiler_params=_cparams(2),
        name="proj_rope",
    )(xb, wT, wT_rot, cos, sin)


def _proj_idx_kernel(x_ref, w_ref, g_ref, b_ref, ik_ref, iwT_ref, xb_ref, *, iw_scale):
    xb = x_ref[...].astype(BF16)
    xb_ref[...] = xb
    acc = _x_wt(xb, w_ref[...])
    lane = lax.broadcasted_iota(jnp.int32, acc.shape, 1)
    is_k = lane < IDX_HEAD_DIM
    mu = jnp.sum(jnp.where(is_k, acc, 0.0), axis=-1, keepdims=True) / IDX_HEAD_DIM
    d = acc - mu
    var = jnp.sum(jnp.where(is_k, d * d, 0.0), axis=-1, keepdims=True) / IDX_HEAD_DIM
    ln = d * lax.rsqrt(var + LN_EPS) * g_ref[...] + b_ref[...]
    ik_ref[...] = ln[:, :IDX_HEAD_DIM].astype(ik_ref.dtype)
    iwT_ref[...] = (acc * iw_scale).T[IDX_HEAD_DIM:IDX_HEAD_DIM + SUBLANES, :]


def _proj_idx(x, w_pad, g_pad, b_pad, iw_scale, B, S, tm=PROJ_TM):
    T, D = x.shape
    tm = min(tm, S)
    nsb = S // tm
    return pl.pallas_call(
        functools.partial(_proj_idx_kernel, iw_scale=iw_scale),
        grid=(T // tm,),
        in_specs=[pl.BlockSpec((tm, D), lambda i: (i, 0)),
                  pl.BlockSpec((LANES, D), lambda i: (0, 0)),
                  pl.BlockSpec((1, LANES), lambda i: (0, 0)),
                  pl.BlockSpec((1, LANES), lambda i: (0, 0))],
        out_specs=[pl.BlockSpec((tm, IDX_HEAD_DIM), lambda i: (i, 0)),
                   pl.BlockSpec((pl.Squeezed(), SUBLANES, tm), lambda i: (i // nsb, 0, i % nsb)),
                   pl.BlockSpec((tm, D), lambda i: (i, 0))],
        out_shape=[jax.ShapeDtypeStruct((T, IDX_HEAD_DIM), BF16),
                   jax.ShapeDtypeStruct((B, SUBLANES, S), F32),
                   jax.ShapeDtypeStruct((T, D), BF16)],
        compiler_params=_cparams(1),
        name="proj_idx",
    )(x, w_pad, g_pad, b_pad)


def _t5_bucket_table():
    n = np.arange(BIAS_TABLE_N)
    max_exact = NUM_BUCKETS // 2
    nf = np.maximum(n, 1).astype(np.float64)
    large = max_exact + (np.log(nf / max_exact) / math.log(MAX_DISTANCE / max_exact)
                         * (NUM_BUCKETS - max_exact)).astype(np.int64)
    large = np.minimum(large, NUM_BUCKETS - 1)
    bucket = np.where(n < max_exact, n, large)
    assert np.all(bucket[FAR_N:] == NUM_BUCKETS - 1) and bucket[FAR_N - 1] != NUM_BUCKETS - 1
    return bucket.astype(np.int32)


def _fold_rows(a, op):
    parts = [a[r:r + SUBLANES] for r in range(0, a.shape[0], SUBLANES)]
    while len(parts) > 1:
        nxt = [op(parts[k], parts[k + 1]) for k in range(0, len(parts) - 1, 2)]
        if len(parts) % 2:
            nxt.append(parts[-1])
        parts = nxt
    return parts[0]


def _attn_kernel(biasc_ref, pinfo_ref, ckmax_ref, iqT_ref, iwT_ref, posqc_ref, posk_ref, ik_ref, qT_ref, k_ref, vT_ref,
                 tbl_ref, toep_ref, tril_ref, o_ref, sc_ref, s_ref, s2_ref, b_ref, qm_ref, m_ref, l_ref, *acc_refs, k_sel, seq):
    i = pl.program_id(1)
    nch = (i * TQ + TQ + CK - 1) // CK
    q_idx = i * TQ + lax.broadcasted_iota(jnp.int32, (1, TQ), 1)
    kf = float(k_sel)

    def chunk_off(c):
        return pl.multiple_of(c * CK, CK)

    def key_idx(off):
        return off + lax.broadcasted_iota(jnp.int32, (CK, TQ), 0)

    def col_reduce(part, op):
        return op(part, axis=0, keepdims=True)

    iqT = iqT_ref[...]
    iwT = iwT_ref[...]
    iq_wide = jnp.concatenate([iqT[h * IDX_HEAD_DIM:(h + 1) * IDX_HEAD_DIM, :] for h in range(IDX_HEADS)], axis=1)

    def score_matmul(c, z_buf):
        ikc = ik_ref[pl.ds(chunk_off(c), CK), :]
        z = jnp.dot(ikc, iq_wide, preferred_element_type=F32)
        for h in range(IDX_HEADS):
            z_buf[h] = z[:, h * TQ:(h + 1) * TQ]

    def score_finish(c, z_buf, carry, masked):
        mn, mx = carry
        off = chunk_off(c)
        s = None
        for h in range(IDX_HEADS):
            t = jnp.maximum(z_buf[h], 0.0) * iwT[h:h + 1, :]
            s = t if s is None else s + t
        if masked:
            causal = key_idx(off) <= q_idx
            s_lo = jnp.where(causal, s, -jnp.inf)
            s_hi = jnp.where(causal, s, jnp.inf)
        else:
            s_lo = s_hi = s
        sc_ref[pl.ds(off, CK), :] = s_lo
        mn = jnp.minimum(mn, _fold_rows(s_hi, jnp.minimum))
        mx = jnp.maximum(mx, _fold_rows(s_lo, jnp.maximum))
        return mn, mx

    def score_chunk(c, carry, masked):
        score_matmul(c, s_ref)
        return score_finish(c, s_ref, carry, masked)

    n_inner = nch - 1
    n_pairs_sc = n_inner // 2

    @pl.when(n_pairs_sc > 0)
    def _():
        score_matmul(0, s_ref)

    def score_step(t, carry):
        c = 2 * t
        score_matmul(c + 1, s2_ref)
        carry = score_finish(c, s_ref, carry, False)
        score_matmul(jnp.minimum(c + 2, 2 * n_pairs_sc - 2), s_ref)
        return score_finish(c + 1, s2_ref, carry, False)

    mn8, mx8 = lax.fori_loop(0, n_pairs_sc, score_step,
                             (jnp.full((SUBLANES, TQ), jnp.inf, F32), jnp.full((SUBLANES, TQ), -jnp.inf, F32)))
    mn8, mx8 = lax.fori_loop(2 * n_pairs_sc, n_inner, functools.partial(score_chunk, masked=False), (mn8, mx8))
    mn8, mx8 = score_chunk(nch - 1, (mn8, mx8), masked=True)
    mn = col_reduce(mn8, jnp.min)
    mx = col_reduce(mx8, jnp.max)

    def over_chunks(fn, init):
        def one(c, carry):
            return fn(sc_ref[pl.ds(chunk_off(c), CK), :], carry)

        carry = lax.fori_loop(0, nch // 2, lambda t, carry: one(2 * t + 1, one(2 * t, carry)), init)
        return lax.fori_loop(2 * (nch // 2), nch, one, carry)

    def count(pred_fn):
        acc = over_chunks(lambda blk, acc: acc + _fold_rows(jnp.where(pred_fn(blk), 1.0, 0.0), jnp.add),
                          jnp.zeros((SUBLANES, TQ), F32))
        return col_reduce(acc, jnp.sum)

    def bisect_round(_, st):
        lo, hi, c_lo = st
        mid = 0.5 * (lo + hi)
        c = count(lambda blk: blk >= mid)
        ok = c >= kf
        return jnp.where(ok, mid, lo), jnp.where(ok, hi, mid), jnp.where(ok, c, c_lo)

    c_all = (q_idx + 1).astype(F32)
    lo, hi, c_lo = lax.fori_loop(0, BISECT_ROUNDS, bisect_round, (mn, mx, c_all))

    cur0 = col_reduce(over_chunks(
        lambda blk, acc: jnp.minimum(acc, _fold_rows(jnp.where(blk >= lo, blk, jnp.inf), jnp.minimum)),
        jnp.full((SUBLANES, TQ), jnp.inf, F32)), jnp.min)

    def walk_cond(st):
        return st[3] > 0.0

    def walk_body(st):
        cur, c_ge, _, _ = st

        def body(blk, carry):
            cnt, nxt = carry
            gt = blk > cur
            cnt = cnt + _fold_rows(jnp.where(gt, 1.0, 0.0), jnp.add)
            nxt = jnp.minimum(nxt, _fold_rows(jnp.where(gt, blk, jnp.inf), jnp.minimum))
            return cnt, nxt

        cnt, nxt = over_chunks(body, (jnp.zeros((SUBLANES, TQ), F32), jnp.full((SUBLANES, TQ), jnp.inf, F32)))
        c_gt = col_reduce(cnt, jnp.sum)
        nxt = col_reduce(nxt, jnp.min)
        adv = c_gt >= kf
        cur = jnp.where(adv, nxt, cur)
        c_ge = jnp.where(adv, c_gt, c_ge)
        return cur, c_ge, c_gt, jnp.max(jnp.where(adv, 1.0, 0.0))

    tau, c_ge, c_gt, _ = lax.while_loop(
        walk_cond, walk_body, (cur0, c_lo, jnp.zeros((1, TQ), F32), jnp.float32(1.0)))

    room = kf - c_gt

    def selection_mask(c, seen):
        off = chunk_off(c)
        tiles = range(CK // LANES)
        blks = [sc_ref[pl.ds(off + j * LANES, LANES), :] for j in tiles]
        ties = [blk == tau for blk in blks]
        local = [jnp.dot(tril_ref[...], jnp.where(tie, 1.0, 0.0).astype(BF16), preferred_element_type=F32)
                 for tie in ties]
        masks = []
        for j in tiles:
            keep = (blks[j] > tau) | (ties[j] & (local[j] + seen <= room))
            masks.append(jnp.where(keep, 0.0, NEG))
            seen = seen + local[j][LANES - 1:LANES, :]
        return jnp.concatenate(masks, axis=0), seen

    m_ref[...] = jnp.full(m_ref.shape, NEG, F32)
    l_ref[...] = jnp.zeros(l_ref.shape, F32)
    for acc in acc_refs:
        acc[...] = jnp.zeros(acc.shape, F32)

    rowi = lax.broadcasted_iota(jnp.int32, (LANES, TQ), 0)
    for pair in range(ATTN_HEADS // 2):
        qp = qT_ref[pair * LANES:(pair + 1) * LANES, :]
        zero = jnp.zeros_like(qp)
        qm_ref[pair] = jnp.concatenate([jnp.where(rowi < ATTN_HEAD_DIM, qp, zero),
                                        jnp.where(rowi >= ATTN_HEAD_DIM, qp, zero)], axis=1)
    ones_rows = jnp.ones((2 * SUBLANES, CK), BF16)

    pq_col = posqc_ref[...]
    batch = pl.program_id(0)
    pq_first, pq_consec, pq_min = (pinfo_ref[batch, r, i] for r in (0, 1, 3))
    far_bias = [biasc_ref[0, h] for h in range(ATTN_HEADS)]
    zero_bias = [biasc_ref[1, h] for h in range(ATTN_HEADS)]
    q_ge_k = (lax.broadcasted_iota(jnp.int32, (LANES, TQ), 1) >= lax.broadcasted_iota(jnp.int32, (LANES, TQ), 0))
    n_sub = CK // LANES
    n_pairs = ATTN_HEADS // 2

    def chunk_is_far(c):
        return (pq_min - ckmax_ref[batch, jnp.minimum(c, seq // CK - 1)]) >= FAR_N

    def stage_bias(c):
        off = chunk_off(c)
        pk_row = posk_ref[:, pl.ds(off, CK)]
        for j in range(n_sub):
            rows = slice(j * LANES, (j + 1) * LANES)
            g = c * n_sub + j
            pk_first, pk_consec, pk_max = (pinfo_ref[batch, r, g] for r in (0, 1, 2))
            all_far = (pq_min - pk_max) >= FAR_N
            all_masked = (off + j * LANES) > (i * TQ + TQ - 1)
            consecutive = (pq_consec > 0) & (pk_consec > 0)
            gap = pq_first - pk_first

            def fill_const(rows=rows):
                for h in range(ATTN_HEADS):
                    b_ref[h, rows, :] = jnp.full((LANES, TQ), far_bias[h], F32)

            def fill_gap0(rows=rows):
                for h in range(ATTN_HEADS):
                    b_ref[h, rows, :] = jnp.where(q_ge_k, toep_ref[h], zero_bias[h])

            def fill_gap128(rows=rows):
                for h in range(ATTN_HEADS):
                    b_ref[h, rows, :] = jnp.where(q_ge_k, far_bias[h], toep_ref[h])

            def fill_lookup(rows=rows):
                pk_sub = pk_row[:, rows]
                n_qk = jnp.clip(pq_col - pk_sub, 0, BIAS_TABLE_N - 1).astype(F32)
                n_kq = n_qk.T.astype(jnp.int32)
                for h in range(ATTN_HEADS):
                    tb = jnp.broadcast_to(tbl_ref[h:h + 1, :], (LANES, BIAS_TABLE_N))
                    b_ref[h, rows, :] = jnp.take_along_axis(tb, n_kq, axis=1)

            def fill_near(fill_gap0=fill_gap0, fill_gap128=fill_gap128, fill_lookup=fill_lookup,
                          consecutive=consecutive, gap=gap):
                lax.cond(consecutive & (gap == 0), fill_gap0,
                         lambda: lax.cond(consecutive & (gap == LANES), fill_gap128, fill_lookup))

            lax.cond(all_far | all_masked, fill_const, fill_near)

    def logits_phase(c, s_buf, const_bias, seen):
        off = chunk_off(c)
        mb, seen = selection_mask(c, seen)
        m_cur = []
        for pair in range(n_pairs):
            kc = k_ref[pl.ds(off, CK), pair * LANES:(pair + 1) * LANES]
            s2 = jnp.dot(kc, qm_ref[pair], preferred_element_type=F32)
            for sub, h in enumerate((2 * pair, 2 * pair + 1)):
                s = s2[:, sub * TQ:(sub + 1) * TQ] + mb
                if not const_bias:
                    s = s + b_ref[h]
                s_buf[h] = s
                top = col_reduce(_fold_rows(s, jnp.maximum), jnp.max)
                m_cur.append(top + far_bias[h] if const_bias else top)
        return m_cur, seen

    def update_phase(c, s_buf, m_cur, const_bias):
        off = chunk_off(c)
        m_prev = m_ref[...]
        l_prev = l_ref[...]
        if isinstance(m_cur, list):
            m_new = [jnp.maximum(m_prev[h:h + 1, :], m_cur[h]) for h in range(ATTN_HEADS)]
            alpha = [jnp.exp2(m_prev[h:h + 1, :] - m_new[h]) for h in range(ATTN_HEADS)]
        else:
            m_all = jnp.maximum(m_prev, m_cur)
            a_all = jnp.exp2(m_prev - m_all)
            m_new = [m_all[h:h + 1, :] for h in range(ATTN_HEADS)]
            alpha = [a_all[h:h + 1, :] for h in range(ATTN_HEADS)]
        l_new = []
        for pair in range(n_pairs):
            heads = (2 * pair, 2 * pair + 1)
            sub_m = [m_new[h] - far_bias[h] if const_bias else m_new[h] for h in heads]
            p2 = jnp.concatenate([jnp.exp2(s_buf[h] - sm).astype(BF16) for h, sm in zip(heads, sub_m)],
                                 axis=1)
            lhs = jnp.concatenate([vT_ref[pair * LANES:(pair + 1) * LANES, pl.ds(off, CK)], ones_rows],
                                  axis=0)
            out = jnp.dot(lhs, p2, preferred_element_type=F32)
            for sub, h in enumerate(heads):
                cols = slice(sub * TQ, (sub + 1) * TQ)
                acc = acc_refs[h]
                acc[...] = alpha[h] * acc[...] + out[sub * ATTN_HEAD_DIM:(sub + 1) * ATTN_HEAD_DIM, cols]
                l_new.append(alpha[h] * l_prev[h:h + 1, :] + out[LANES:LANES + 1, cols])
        l_ref[...] = jnp.concatenate(l_new, axis=0)
        m_ref[...] = jnp.concatenate(m_new, axis=0)

    n_far = lax.while_loop(lambda c: (c < nch) & chunk_is_far(c), lambda c: c + 1, jnp.int32(0))
    n_steps = n_far // 2
    def far_logits(c, s_buf, seen):
        m_cur, seen = logits_phase(c, s_buf, True, seen)
        return jnp.concatenate(m_cur, axis=0), seen

    no_ties = jnp.zeros((1, TQ), F32)
    m_first, seen_first = lax.cond(n_steps > 0, lambda: far_logits(0, s_ref, no_ties),
                                   lambda: (jnp.zeros((ATTN_HEADS, TQ), F32), no_ties))

    def far_step(t, carry):
        m_even, seen_even, _ = carry
        c = 2 * t
        m_odd, seen_odd = far_logits(c + 1, s2_ref, seen_even)
        update_phase(c, s_ref, m_even, True)
        m_even, seen_even = far_logits(jnp.minimum(c + 2, n_far - 1), s_ref, seen_odd)
        update_phase(c + 1, s2_ref, m_odd, True)
        return m_even, seen_even, seen_odd

    m_even, seen_even, seen = lax.fori_loop(0, n_steps, far_step, (m_first, seen_first, no_ties))
    leftover_staged = (n_steps > 0) & (n_far % 2 == 1)

    def finish_leftover():
        update_phase(n_far - 1, s_ref, m_even, True)
        return seen_even

    seen = lax.cond(leftover_staged, finish_leftover, lambda: seen)
    n_done = jnp.where(leftover_staged, n_far, 2 * n_steps)

    def tail_chunk(c, seen):
        def run(const_bias):
            if not const_bias:
                stage_bias(c)
            m_cur, seen_out = logits_phase(c, s_ref, const_bias, seen)
            update_phase(c, s_ref, m_cur, const_bias)
            return seen_out

        return lax.cond(chunk_is_far(c), lambda: run(True), lambda: run(False))

    lax.fori_loop(n_done, nch, tail_chunk, seen)

    outT = jnp.concatenate([acc_refs[h][...] / l_ref[h:h + 1, :] for h in range(ATTN_HEADS)], axis=0)
    o_ref[...] = outT.T.astype(o_ref.dtype)


def _toeplitz_kernel(tbl_ref, o_ref):
    q = lax.broadcasted_iota(jnp.int32, (LANES, LANES), 1)
    k = lax.broadcasted_iota(jnp.int32, (LANES, LANES), 0)
    idx = (q - k) & (BIAS_TABLE_N - 1)
    for h in range(ATTN_HEADS):
        tb = jnp.broadcast_to(tbl_ref[h:h + 1, :], (LANES, BIAS_TABLE_N))
        o_ref[h] = jnp.take_along_axis(tb, idx, axis=1)


def _sparse_attention(fm, tokb, ik, iwT, positions, rel_bias):
    B, S, _ = tokb.shape
    k_sel = min(TOPK_MAX, S // 4)
    bucket = _t5_bucket_table()
    tbl = rel_bias[bucket].T.astype(F32) * LOG2E
    bias_c = jnp.stack([tbl[:, BIAS_TABLE_N - 1], tbl[:, 0]])
    toep = pl.pallas_call(
        _toeplitz_kernel,
        out_shape=jax.ShapeDtypeStruct((ATTN_HEADS, LANES, LANES), F32),
        name="bias_toeplitz",
    )(tbl)
    pos_row = positions.reshape(B, 1, S)
    pos_col = positions.reshape(B, S, 1)
    pos_t = positions.reshape(B, S // LANES, LANES)
    consec = jnp.all(pos_t - pos_t[:, :, :1] == jnp.arange(LANES, dtype=positions.dtype), axis=-1)
    pinfo = jnp.stack([pos_t[:, :, 0], consec.astype(jnp.int32), jnp.max(pos_t, axis=-1),
                       jnp.min(pos_t, axis=-1)], axis=1).astype(jnp.int32)
    ck_max = jnp.max(positions.reshape(B, S // CK, CK), axis=-1)
    tril = jnp.tril(jnp.ones((LANES, LANES), BF16))
    sq = pl.Squeezed()
    iq_blk = (2 * ATTN_W) // IDX_Q_W
    k_blk = (tokb.shape[2] - ATTN_W) // ATTN_W
    return pl.pallas_call(
        functools.partial(_attn_kernel, k_sel=k_sel, seq=S),
        grid=(B, S // TQ),
        in_specs=[pl.BlockSpec(memory_space=pltpu.SMEM),
                  pl.BlockSpec(memory_space=pltpu.SMEM),
                  pl.BlockSpec(memory_space=pltpu.SMEM),
                  pl.BlockSpec((sq, IDX_Q_W, TQ), lambda b, i: (b, iq_blk, i)),
                  pl.BlockSpec((sq, SUBLANES, TQ), lambda b, i: (b, 0, i)),
                  pl.BlockSpec((sq, TQ, 1), lambda b, i: (b, i, 0)),
                  pl.BlockSpec((sq, 1, S), lambda b, i: (b, 0, 0)),
                  pl.BlockSpec((sq, S, IDX_HEAD_DIM), lambda b, i: (b, 0, 0)),
                  pl.BlockSpec((sq, ATTN_W, TQ), lambda b, i: (b, 0, i)),
                  pl.BlockSpec((sq, S, ATTN_W), lambda b, i: (b, 0, k_blk)),
                  pl.BlockSpec((sq, ATTN_W, S), lambda b, i: (b, 1, 0)),
                  pl.BlockSpec((ATTN_HEADS, BIAS_TABLE_N), lambda b, i: (0, 0)),
                  pl.BlockSpec((ATTN_HEADS, LANES, LANES), lambda b, i: (0, 0, 0)),
                  pl.BlockSpec((LANES, LANES), lambda b, i: (0, 0))],
        out_specs=pl.BlockSpec((sq, TQ, ATTN_W), lambda b, i: (b, i, 0)),
        out_shape=jax.ShapeDtypeStruct((B, S, ATTN_W), BF16),
        scratch_shapes=[pltpu.VMEM((S, TQ), F32),
                        pltpu.VMEM((ATTN_HEADS, CK, TQ), F32),
                        pltpu.VMEM((ATTN_HEADS, CK, TQ), F32),
                        pltpu.VMEM((ATTN_HEADS, CK, TQ), F32),
                        pltpu.VMEM((ATTN_HEADS // 2, LANES, 2 * TQ), BF16),
                        pltpu.VMEM((ATTN_HEADS, TQ), F32),
                        pltpu.VMEM((ATTN_HEADS, TQ), F32)]
                       + [pltpu.VMEM((ATTN_HEAD_DIM, TQ), F32)] * ATTN_HEADS,
        compiler_params=_cparams(2),
        name="sparse_attention",
    )(bias_c, pinfo, ck_max, fm, iwT, pos_col, pos_row, ik, fm, tokb, fm, tbl, toep, tril)


def _retention_kernel(q_ref, k_ref, v_ref, g_ref, decay_ref, xi_ref, zeta_ref, gch_ref, o_ref, r_ref):
    @pl.when(pl.program_id(0) == 0)
    def _():
        r_ref[...] = jnp.zeros(r_ref.shape, F32)

    C = RET_CHUNK
    lane = lax.broadcasted_iota(jnp.int32, (C, LANES), 1)
    row = lax.broadcasted_iota(jnp.int32, (LANES, RET_V_DIM), 0)
    for ci in range(q_ref.shape[1] // C):
        rows = slice(ci * C, (ci + 1) * C)
        for pair in range(RET_HEADS // 2):
            for b in range(q_ref.shape[0]):
                q_pair = q_ref[b, rows, pair * LANES:(pair + 1) * LANES]
                k_pair = k_ref[b, rows, pair * LANES:(pair + 1) * LANES]
                v_pair = v_ref[b, rows, 2 * pair * RET_V_DIM:(2 * pair + 2) * RET_V_DIM]
                r_pair = r_ref[b, pair]
                r_bf = r_pair.astype(BF16)
                for sub in range(2):
                    h = 2 * pair + sub
                    in_head = (lane >= sub * RET_QK_DIM) & (lane < (sub + 1) * RET_QK_DIM)
                    qm = jnp.where(in_head, q_pair, jnp.zeros_like(q_pair))
                    v_h = v_pair[:, sub * RET_V_DIM:(sub + 1) * RET_V_DIM]
                    inner = lax.dot_general(qm, k_pair, (((1,), (1,)), ((), ())),
                                            preferred_element_type=F32) * decay_ref[h]
                    o = (jnp.dot(inner.astype(BF16), v_h, preferred_element_type=F32)
                         + jnp.dot(qm, r_bf, preferred_element_type=F32) * xi_ref[h])
                    mu = jnp.mean(o, axis=-1, keepdims=True)
                    d = o - mu
                    var = jnp.mean(d * d, axis=-1, keepdims=True)
                    hn = d * lax.rsqrt(var + LN_EPS)
                    gate = g_ref[b, rows, h * RET_V_DIM:(h + 1) * RET_V_DIM].astype(F32)
                    o_ref[b, rows, h * RET_V_DIM:(h + 1) * RET_V_DIM] = (gate * hn).astype(o_ref.dtype)
                kz = (k_pair.astype(F32) * zeta_ref[pair]).astype(BF16)
                upd = lax.dot_general(kz, v_pair, (((0,), (0,)), ((), ())),
                                      preferred_element_type=F32)
                r_ref[b, pair] = (r_pair * gch_ref[pair]
                                  + jnp.where(row < RET_QK_DIM, upd[:, :RET_V_DIM], upd[:, RET_V_DIM:]))


def _retention(qk, tokb, gates, B, S):
    C = RET_CHUNK
    H = RET_HEADS
    nc = S // C
    gamma = 1.0 - 2.0 ** (-5.0 - jnp.arange(H, dtype=F32))
    log_g = jnp.log(gamma)
    n = jnp.arange(C, dtype=F32)
    diff = n[:, None] - n[None, :]
    decay_in = jnp.where(diff[None] >= 0, jnp.exp(log_g[:, None, None] * jnp.maximum(diff, 0.0)[None]), 0.0)
    xi = jnp.exp(log_g[None, :] * (n[:, None] + 1.0))
    zeta = jnp.exp(log_g[None, :] * (C - 1.0 - n[:, None]))
    g_chunk = jnp.exp(log_g * C)
    xi_b = jnp.broadcast_to(xi.T[:, :, None], (H, C, RET_V_DIM))
    zeta_b = jnp.repeat(zeta, RET_QK_DIM, axis=1).reshape(C, H // 2, LANES).transpose(1, 0, 2)
    gch_b = jnp.broadcast_to(jnp.repeat(g_chunk, RET_QK_DIM).reshape(H // 2, LANES, 1),
                             (H // 2, LANES, RET_V_DIM))
    qk3, tok3, gate3 = (a.reshape(B, S, a.shape[-1]) for a in (qk, tokb, gates))
    CS = C * RET_STEP_CHUNKS
    out = pl.pallas_call(
        _retention_kernel,
        grid=(nc // RET_STEP_CHUNKS,),
        in_specs=[pl.BlockSpec((B, CS, RET_QK_W), lambda i: (0, i, 0)),
                  pl.BlockSpec((B, CS, RET_QK_W), lambda i: (0, i, 1)),
                  pl.BlockSpec((B, CS, RET_V_W), lambda i: (0, i, 0)),
                  pl.BlockSpec((B, CS, RET_V_W), lambda i: (0, i, 0)),
                  pl.BlockSpec((H, C, C), lambda i: (0, 0, 0)),
                  pl.BlockSpec((H, C, RET_V_DIM), lambda i: (0, 0, 0)),
                  pl.BlockSpec((H // 2, C, LANES), lambda i: (0, 0, 0)),
                  pl.BlockSpec((H // 2, LANES, RET_V_DIM), lambda i: (0, 0, 0))],
        out_specs=pl.BlockSpec((B, CS, RET_V_W), lambda i: (0, i, 0)),
        out_shape=jax.ShapeDtypeStruct((B, S, RET_V_W), BF16),
        scratch_shapes=[pltpu.VMEM((B, H // 2, LANES, RET_V_DIM), F32)],
        compiler_params=_cparams(1),
        name="retention",
    )(qk3, qk3, tok3, gate3, decay_in, xi_b, zeta_b, gch_b)
    return out.reshape(B * S, RET_V_W)


def _layer_norm(z, g, b):
    mu = jnp.mean(z, axis=-1, keepdims=True)
    d = z - mu
    var = jnp.mean(d * d, axis=-1, keepdims=True)
    return d * lax.rsqrt(var + LN_EPS) * g + b


def _merge_kernel(x_ref, ya_ref, yr_ref, ga_ref, gr_ref, wa_ref, wr_ref, wo_ref, g_ref, b_ref,
                  x1_ref):
    a = jnp.dot(ya_ref[...], wa_ref[...], preferred_element_type=F32)
    r = jnp.dot(yr_ref[...], wr_ref[...], preferred_element_type=F32)
    h = ga_ref[...].astype(F32) * a + gr_ref[...].astype(F32) * r
    mix = jnp.dot(h.astype(BF16), wo_ref[...], preferred_element_type=F32)
    x1_ref[...] = _layer_norm(DEEPNORM_ALPHA * x_ref[...] + mix, g_ref[...], b_ref[...])


def _merge(x, ya, yr, gates, wa, wr, wo, g, b, tm=MERGE_TM):
    T, D = x.shape
    tm = min(tm, T)
    row = lambda i: (i, 0)
    fixed = lambda i: (0, 0)
    return pl.pallas_call(
        _merge_kernel,
        grid=(T // tm,),
        in_specs=[pl.BlockSpec((tm, D), row),
                  pl.BlockSpec((tm, ya.shape[1]), row),
                  pl.BlockSpec((tm, yr.shape[1]), row),
                  pl.BlockSpec((tm, D), lambda i: (i, 1)),
                  pl.BlockSpec((tm, D), lambda i: (i, 2)),
                  pl.BlockSpec(wa.shape, fixed),
                  pl.BlockSpec(wr.shape, fixed),
                  pl.BlockSpec(wo.shape, fixed),
                  pl.BlockSpec((1, D), fixed),
                  pl.BlockSpec((1, D), fixed)],
        out_specs=pl.BlockSpec((tm, D), row),
        out_shape=jax.ShapeDtypeStruct((T, D), F32),
        compiler_params=_cparams(1),
        name="merge",
    )(x, ya, yr, gates, gates, wa, wr, wo, g, b)


def _ffn_kernel(x1_ref, wu_ref, wd_ref, g_ref, b_ref, o_ref, acc_ref, xb_ref):
    f = pl.program_id(1)

    @pl.when(f == 0)
    def _():
        acc_ref[...] = jnp.zeros(acc_ref.shape, F32)
        xb_ref[...] = x1_ref[...].astype(BF16)

    hid = jnp.maximum(jnp.dot(xb_ref[...], wu_ref[...], preferred_element_type=F32), 0.0)
    acc_ref[...] += jnp.dot((hid * hid).astype(BF16), wd_ref[...], preferred_element_type=F32)

    @pl.when(f == pl.num_programs(1) - 1)
    def _():
        o_ref[...] = _layer_norm(DEEPNORM_ALPHA * x1_ref[...] + acc_ref[...], g_ref[...], b_ref[...])


def _ffn(x1, wu, wd, g, b, tm=FFN_TM, tf=FFN_TF):
    T, D = x1.shape
    F = wu.shape[1]
    tm = min(tm, T)
    return pl.pallas_call(
        _ffn_kernel,
        grid=(T // tm, F // tf),
        in_specs=[pl.BlockSpec((tm, D), lambda i, f: (i, 0)),
                  pl.BlockSpec((D, tf), lambda i, f: (0, f)),
                  pl.BlockSpec((tf, D), lambda i, f: (f, 0)),
                  pl.BlockSpec((1, D), lambda i, f: (0, 0)),
                  pl.BlockSpec((1, D), lambda i, f: (0, 0))],
        out_specs=pl.BlockSpec((tm, D), lambda i, f: (i, 0)),
        out_shape=jax.ShapeDtypeStruct((T, D), F32),
        scratch_shapes=[pltpu.VMEM((tm, D), F32), pltpu.VMEM((tm, D), BF16)],
        compiler_params=_cparams(2),
        name="ffn",
    )(x1, wu, wd, g, b)


def _rot_half_weight(wT):
    N, D = wT.shape
    half = RET_QK_DIM // 2
    wh = wT.reshape(N // RET_QK_DIM, 2, half, D)
    return jnp.stack([-wh[:, 1], wh[:, 0]], axis=1).reshape(N, D)


def kernel(x, positions, w_in, rel_bias, idx_k_ln_g, idx_k_ln_b, w_attn_branch, w_ret_branch,
           w_out, ln_mix_g, ln_mix_b, w_up, w_down, ln_ffn_g, ln_ffn_b):
    B, S, D = x.shape
    T = B * S
    sizes = (ATTN_W, ATTN_W, ATTN_W, IDX_Q_W, IDX_HEAD_DIM, IDX_HEADS,
             RET_QK_W, RET_QK_W, RET_V_W, RET_V_W, D, D)
    offs = [0] + [int(o) for o in np.cumsum(sizes)]
    cos, sin = _rope_tables(positions)
    xf = x.reshape(T, D)
    for l in range(DEPTH):
        wT = jnp.swapaxes(w_in[l], 0, 1).astype(BF16)
        rows = [wT[offs[k]:offs[k + 1]] for k in range(len(sizes))]
        (w_qa, w_ka, w_va, w_iq, w_ik, w_iw, w_qr, w_kr, w_vr, w_gr, w_ga, w_gtr) = rows
        w_kr = w_kr * (RET_QK_DIM ** -0.5)

        pad = LANES - IDX_HEAD_DIM - IDX_HEADS
        w_idx = jnp.concatenate([w_ik, w_iw, jnp.zeros((pad, D), BF16)], axis=0)
        g_pad = jnp.concatenate([idx_k_ln_g[l], jnp.zeros((LANES - IDX_HEAD_DIM,), F32)]).reshape(1, LANES)
        b_pad = jnp.concatenate([idx_k_ln_b[l], jnp.zeros((LANES - IDX_HEAD_DIM,), F32)]).reshape(1, LANES)
        ik, iwT, xb = _proj_idx(xf, w_idx, g_pad, b_pad, (IDX_HEAD_DIM ** -0.5) * (IDX_HEADS ** -0.5), B, S)
        ik = ik.reshape(B, S, IDX_HEAD_DIM)

        fm = _proj_t(xb, jnp.concatenate([w_qa, w_va, w_iq], axis=0), B, S, BF16,
                     scaled_rows=ATTN_W, scale=ATTN_HEAD_DIM ** -0.5 * LOG2E)
        tokb = _proj(xb, jnp.concatenate([w_vr, w_ka], axis=0), BF16)
        gates = _proj_gates(xb, jnp.concatenate([w_gr, w_ga, w_gtr], axis=0), tn=D)
        w_rope = jnp.concatenate([w_qr, w_kr], axis=0)
        w_rope_rot = jnp.concatenate([_rot_half_weight(w_qr), _rot_half_weight(w_kr)], axis=0)
        qk_r = _proj_rope(xb, w_rope, w_rope_rot, cos, sin)

        y_a = _sparse_attention(fm, tokb.reshape(B, S, -1), ik, iwT, positions, rel_bias)
        y_r = _retention(qk_r, tokb, gates, B, S)
        x1 = _merge(xf, y_a.reshape(T, ATTN_W), y_r, gates,
                    w_attn_branch[l].astype(BF16), w_ret_branch[l].astype(BF16),
                    w_out[l].astype(BF16), ln_mix_g[l].reshape(1, D), ln_mix_b[l].reshape(1, D))
        xf = _ffn(x1, w_up[l].astype(BF16), w_down[l].astype(BF16),
                  ln_ffn_g[l].reshape(1, D), ln_ffn_b[l].reshape(1, D))
    return xf.reshape(B, S, D)
```

```python
import functools
import math

import numpy as np
import jax
import jax.numpy as jnp
from jax import lax
from jax.experimental import pallas as pl
from jax.experimental.pallas import tpu as pltpu

F32 = jnp.float32
BF16 = jnp.bfloat16

ATTN_HEADS = 8
ATTN_HEAD_DIM = 64
ATTN_W = ATTN_HEADS * ATTN_HEAD_DIM
IDX_HEADS = 4
IDX_HEAD_DIM = 64
IDX_Q_W = IDX_HEADS * IDX_HEAD_DIM
TOPK_MAX = 256
RET_HEADS = 8
RET_QK_DIM = 64
RET_V_DIM = 128
RET_QK_W = RET_HEADS * RET_QK_DIM
RET_V_W = RET_HEADS * RET_V_DIM
RET_CHUNK = 128
RET_STEP_CHUNKS = 2
ROPE_BASE = 10000.0
NUM_BUCKETS = 32
MAX_DISTANCE = 128
LN_EPS = 1e-5
DEPTH = 1
DEEPNORM_ALPHA = (2.0 * DEPTH) ** 0.25

LANES = 128
SUBLANES = 8
VMEM_LIMIT = 56 * 1024 * 1024

PROJ_TM = 1024
GATES_TM = 2048
MERGE_TM = 512
FFN_TM = 1024
FFN_TF = 1024

TQ = 128
CK = 512
NEG = -1e30
LOG2E = math.log2(math.e)
BISECT_ROUNDS = 19
BIAS_TABLE_N = 128
FAR_N = 113


def _cparams(n_grid):
    return pltpu.CompilerParams(
        dimension_semantics=("arbitrary",) * n_grid,
        vmem_limit_bytes=VMEM_LIMIT)


def _trig_kernel(pos_ref, inv_ref, cos_ref, sin_ref):
    ang = pos_ref[...] * inv_ref[...]
    tr = ang.shape[0]
    half = RET_QK_DIM // 2
    per_row = LANES // half
    lane = lax.broadcasted_iota(jnp.int32, ang.shape, 1)
    for ref, val in ((cos_ref, jnp.cos(ang)), (sin_ref, jnp.sin(ang))):
        for k in range(per_row):
            own = jnp.where((lane >= k * half) & (lane < (k + 1) * half), val, 0.0)
            spread = own
            for j in range(1, per_row):
                spread = spread + pltpu.roll(own, j * half, 1)
            ref[pl.ds(k, tr, stride=per_row), :] = spread


def _rope_tables(positions):
    B, S = positions.shape
    half = RET_QK_DIM // 2
    inv = ROPE_BASE ** (-jnp.arange(half, dtype=F32) / half)
    per_row = LANES // half
    rows = B * S // per_row
    pos_e = jnp.repeat(positions.astype(F32).reshape(rows, per_row), half, axis=1)
    inv_e = jnp.tile(inv, per_row).reshape(1, LANES)
    tr = min(rows, 1024)
    return pl.pallas_call(
        _trig_kernel,
        grid=(rows // tr,),
        in_specs=[pl.BlockSpec((tr, LANES), lambda i: (i, 0)),
                  pl.BlockSpec((1, LANES), lambda i: (0, 0))],
        out_specs=[pl.BlockSpec((per_row * tr, LANES), lambda i: (i, 0))] * 2,
        out_shape=[jax.ShapeDtypeStruct((B * S, LANES), F32)] * 2,
        compiler_params=_cparams(1),
        name="rope_tables",
    )(pos_e, inv_e)


def _x_wt(x, wT):
    return lax.dot_general(x, wT, (((1,), (1,)), ((), ())), preferred_element_type=F32)


def _proj_kernel(x_ref, w_ref, o_ref):
    o_ref[...] = _x_wt(x_ref[...], w_ref[...]).astype(o_ref.dtype)


def _proj(xb, wT, out_dtype, tm=PROJ_TM):
    T, D = xb.shape
    N = wT.shape[0]
    tn = N
    tm = min(tm, T)
    return pl.pallas_call(
        _proj_kernel,
        grid=(T // tm, N // tn),
        in_specs=[pl.BlockSpec((tm, D), lambda i, j: (i, 0)),
                  pl.BlockSpec((tn, D), lambda i, j: (j, 0))],
        out_specs=pl.BlockSpec((tm, tn), lambda i, j: (i, j)),
        out_shape=jax.ShapeDtypeStruct((T, N), out_dtype),
        compiler_params=_cparams(2),
        name="proj",
    )(xb, wT)


def _proj_gates_kernel(x_ref, w_ref, o_ref):
    acc = _x_wt(x_ref[...], w_ref[...])
    sig = 0.5 * jnp.tanh(0.5 * acc) + 0.5
    o_ref[...] = jnp.where(pl.program_id(1) == 0, acc * sig, sig).astype(o_ref.dtype)


def _proj_gates(xb, wT, tn, tm=GATES_TM):
    T, D = xb.shape
    N = wT.shape[0]
    tm = min(tm, T)
    return pl.pallas_call(
        _proj_gates_kernel,
        grid=(T // tm, N // tn),
        in_specs=[pl.BlockSpec((tm, D), lambda i, j: (i, 0)),
                  pl.BlockSpec((tn, D), lambda i, j: (j, 0))],
        out_specs=pl.BlockSpec((tm, tn), lambda i, j: (i, j)),
        out_shape=jax.ShapeDtypeStruct((T, N), BF16),
        compiler_params=_cparams(2),
        name="proj_gates",
    )(xb, wT)


def _proj_t_kernel(wT_ref, x_ref, o_ref, *, scaled_rows, scale):
    acc = lax.dot_general(wT_ref[...], x_ref[...], (((1,), (1,)), ((), ())), preferred_element_type=F32)
    o_ref[:scaled_rows, :] = (acc[:scaled_rows] * scale).astype(o_ref.dtype)
    o_ref[scaled_rows:, :] = acc[scaled_rows:].astype(o_ref.dtype)


def _proj_t(xb, wT, B, S, out_dtype, scaled_rows, scale, tm=PROJ_TM):
    T, D = xb.shape
    N = wT.shape[0]
    tn = N
    tm = min(tm, S)
    nsb = S // tm
    return pl.pallas_call(
        functools.partial(_proj_t_kernel, scaled_rows=scaled_rows, scale=scale),
        grid=(T // tm, N // tn),
        in_specs=[pl.BlockSpec((tn, D), lambda i, j: (j, 0)),
                  pl.BlockSpec((tm, D), lambda i, j: (i, 0))],
        out_specs=pl.BlockSpec((pl.Squeezed(), tn, tm), lambda i, j: (i // nsb, j, i % nsb)),
        out_shape=jax.ShapeDtypeStruct((B, N, S), out_dtype),
        compiler_params=_cparams(2),
        name="proj_t",
    )(wT, xb)


def _proj_rope_kernel(x_ref, w_ref, wr_ref, cos_ref, sin_ref, o_ref):
    x = x_ref[...]
    a = _x_wt(x, w_ref[...])
    r = _x_wt(x, wr_ref[...])
    reps = a.shape[1] // LANES
    cos = jnp.concatenate([cos_ref[...]] * reps, axis=1)
    sin = jnp.concatenate([sin_ref[...]] * reps, axis=1)
    o_ref[...] = (a * cos + r * sin).astype(o_ref.dtype)


def _proj_rope(xb, wT, wT_rot, cos, sin, tm=PROJ_TM):
    T, D = xb.shape
    N = wT.shape[0]
    tn = N
    tm = min(tm, T)
    return pl.pallas_call(
        _proj_rope_kernel,
        grid=(T // tm, N // tn),
        in_specs=[pl.BlockSpec((tm, D), lambda i, j: (i, 0)),
                  pl.BlockSpec((tn, D), lambda i, j: (j, 0)),
                  pl.BlockSpec((tn, D), lambda i, j: (j, 0)),
                  pl.BlockSpec((tm, LANES), lambda i, j: (i, 0)),
                  pl.BlockSpec((tm, LANES), lambda i, j: (i, 0))],
        out_specs=pl.BlockSpec((tm, tn), lambda i, j: (i, j)),
        out_shape=jax.ShapeDtypeStruct((T, N), BF16),
        compiler_params=_cparams(2),
        name="proj_rope",
    )(xb, wT, wT_rot, cos, sin)


def _proj_idx_kernel(x_ref, w_ref, g_ref, b_ref, ik_ref, iwT_ref, xb_ref, *, iw_scale):
    xb = x_ref[...].astype(BF16)
    xb_ref[...] = xb
    acc = _x_wt(xb, w_ref[...])
    lane = lax.broadcasted_iota(jnp.int32, acc.shape, 1)
    is_k = lane < IDX_HEAD_DIM
    mu = jnp.sum(jnp.where(is_k, acc, 0.0), axis=-1, keepdims=True) / IDX_HEAD_DIM
    d = acc - mu
    var = jnp.sum(jnp.where(is_k, d * d, 0.0), axis=-1, keepdims=True) / IDX_HEAD_DIM
    ln = d * lax.rsqrt(var + LN_EPS) * g_ref[...] + b_ref[...]
    ik_ref[...] = ln[:, :IDX_HEAD_DIM].astype(ik_ref.dtype)
    iwT_ref[...] = (acc * iw_scale).T[IDX_HEAD_DIM:IDX_HEAD_DIM + SUBLANES, :]


def _proj_idx(x, w_pad, g_pad, b_pad, iw_scale, B, S, tm=PROJ_TM):
    T, D = x.shape
    tm = min(tm, S)
    nsb = S // tm
    return pl.pallas_call(
        functools.partial(_proj_idx_kernel, iw_scale=iw_scale),
        grid=(T // tm,),
        in_specs=[pl.BlockSpec((tm, D), lambda i: (i, 0)),
                  pl.BlockSpec((LANES, D), lambda i: (0, 0)),
                  pl.BlockSpec((1, LANES), lambda i: (0, 0)),
                  pl.BlockSpec((1, LANES), lambda i: (0, 0))],
        out_specs=[pl.BlockSpec((tm, IDX_HEAD_DIM), lambda i: (i, 0)),
                   pl.BlockSpec((pl.Squeezed(), SUBLANES, tm), lambda i: (i // nsb, 0, i % nsb)),
                   pl.BlockSpec((tm, D), lambda i: (i, 0))],
        out_shape=[jax.ShapeDtypeStruct((T, IDX_HEAD_DIM), BF16),
                   jax.ShapeDtypeStruct((B, SUBLANES, S), F32),
                   jax.ShapeDtypeStruct((T, D), BF16)],
        compiler_params=_cparams(1),
        name="proj_idx",
    )(x, w_pad, g_pad, b_pad)


def _t5_bucket_table():
    n = np.arange(BIAS_TABLE_N)
    max_exact = NUM_BUCKETS // 2
    nf = np.maximum(n, 1).astype(np.float64)
    large = max_exact + (np.log(nf / max_exact) / math.log(MAX_DISTANCE / max_exact)
                         * (NUM_BUCKETS - max_exact)).astype(np.int64)
    large = np.minimum(large, NUM_BUCKETS - 1)
    bucket = np.where(n < max_exact, n, large)
    assert np.all(bucket[FAR_N:] == NUM_BUCKETS - 1) and bucket[FAR_N - 1] != NUM_BUCKETS - 1
    return bucket.astype(np.int32)


def _fold_rows(a, op):
    parts = [a[r:r + SUBLANES] for r in range(0, a.shape[0], SUBLANES)]
    while len(parts) > 1:
        nxt = [op(parts[k], parts[k + 1]) for k in range(0, len(parts) - 1, 2)]
        if len(parts) % 2:
            nxt.append(parts[-1])
        parts = nxt
    return parts[0]


def _attn_kernel(biasc_ref, pinfo_ref, ckmax_ref, iqT_ref, iwT_ref, posqc_ref, posk_ref, ik_ref, qT_ref, k_ref, vT_ref,
                 tbl_ref, toep_ref, tril_ref, o_ref, sc_ref, s_ref, s2_ref, b_ref, qm_ref, m_ref, l_ref, *acc_refs, k_sel, seq):
    i = pl.program_id(1)
    nch = (i * TQ + TQ + CK - 1) // CK
    q_idx = i * TQ + lax.broadcasted_iota(jnp.int32, (1, TQ), 1)
    kf = float(k_sel)

    def chunk_off(c):
        return pl.multiple_of(c * CK, CK)

    def key_idx(off):
        return off + lax.broadcasted_iota(jnp.int32, (CK, TQ), 0)

    def col_reduce(part, op):
        return op(part, axis=0, keepdims=True)

    iqT = iqT_ref[...]
    iwT = iwT_ref[...]
    iq_wide = jnp.concatenate([iqT[h * IDX_HEAD_DIM:(h + 1) * IDX_HEAD_DIM, :] for h in range(IDX_HEADS)], axis=1)

    def score_matmul(c, z_buf):
        ikc = ik_ref[pl.ds(chunk_off(c), CK), :]
        z = jnp.dot(ikc, iq_wide, preferred_element_type=F32)
        for h in range(IDX_HEADS):
            z_buf[h] = z[:, h * TQ:(h + 1) * TQ]

    def score_finish(c, z_buf, carry, masked):
        mn, mx = carry
        off = chunk_off(c)
        s = None
        for h in range(IDX_HEADS):
            t = jnp.maximum(z_buf[h], 0.0) * iwT[h:h + 1, :]
            s = t if s is None else s + t
        if masked:
            causal = key_idx(off) <= q_idx
            s_lo = jnp.where(causal, s, -jnp.inf)
            s_hi = jnp.where(causal, s, jnp.inf)
        else:
            s_lo = s_hi = s
        sc_ref[pl.ds(off, CK), :] = s_lo
        mn = jnp.minimum(mn, _fold_rows(s_hi, jnp.minimum))
        mx = jnp.maximum(mx, _fold_rows(s_lo, jnp.maximum))
        return mn, mx

    def score_chunk(c, carry, masked):
        score_matmul(c, s_ref)
        return score_finish(c, s_ref, carry, masked)

    n_inner = nch - 1
    n_pairs_sc = n_inner // 2

    @pl.when(n_pairs_sc > 0)
    def _():
        score_matmul(0, s_ref)

    def score_step(t, carry):
        c = 2 * t
        score_matmul(c + 1, s2_ref)
        carry = score_finish(c, s_ref, carry, False)
        score_matmul(jnp.minimum(c + 2, 2 * n_pairs_sc - 2), s_ref)
        return score_finish(c + 1, s2_ref, carry, False)

    mn8, mx8 = lax.fori_loop(0, n_pairs_sc, score_step,
                             (jnp.full((SUBLANES, TQ), jnp.inf, F32), jnp.full((SUBLANES, TQ), -jnp.inf, F32)))
    mn8, mx8 = lax.fori_loop(2 * n_pairs_sc, n_inner, functools.partial(score_chunk, masked=False), (mn8, mx8))
    mn8, mx8 = score_chunk(nch - 1, (mn8, mx8), masked=True)
    mn = col_reduce(mn8, jnp.min)
    mx = col_reduce(mx8, jnp.max)

    def over_chunks(fn, init):
        def one(c, carry):
            return fn(sc_ref[pl.ds(chunk_off(c), CK), :], carry)

        carry = lax.fori_loop(0, nch // 2, lambda t, carry: one(2 * t + 1, one(2 * t, carry)), init)
        return lax.fori_loop(2 * (nch // 2), nch, one, carry)

    def count(pred_fn):
        acc = over_chunks(lambda blk, acc: acc + _fold_rows(jnp.where(pred_fn(blk), 1.0, 0.0), jnp.add),
                          jnp.zeros((SUBLANES, TQ), F32))
        return col_reduce(acc, jnp.sum)

    def bisect_round(_, st):
        lo, hi, c_lo = st
        mid = 0.5 * (lo + hi)
        c = count(lambda blk: blk >= mid)
        ok = c >= kf
        return jnp.where(ok, mid, lo), jnp.where(ok, hi, mid), jnp.where(ok, c, c_lo)

    c_all = (q_idx + 1).astype(F32)
    lo, hi, c_lo = lax.fori_loop(0, BISECT_ROUNDS, bisect_round, (mn, mx, c_all))

    cur0 = col_reduce(over_chunks(
        lambda blk, acc: jnp.minimum(acc, _fold_rows(jnp.where(blk >= lo, blk, jnp.inf), jnp.minimum)),
        jnp.full((SUBLANES, TQ), jnp.inf, F32)), jnp.min)

    def walk_cond(st):
        return st[3] > 0.0

    def walk_body(st):
        cur, c_ge, _, _ = st

        def body(blk, carry):
            cnt, nxt = carry
            gt = blk > cur
            cnt = cnt + _fold_rows(jnp.where(gt, 1.0, 0.0), jnp.add)
            nxt = jnp.minimum(nxt, _fold_rows(jnp.where(gt, blk, jnp.inf), jnp.minimum))
            return cnt, nxt

        cnt, nxt = over_chunks(body, (jnp.zeros((SUBLANES, TQ), F32), jnp.full((SUBLANES, TQ), jnp.inf, F32)))
        c_gt = col_reduce(cnt, jnp.sum)
        nxt = col_reduce(nxt, jnp.min)
        adv = c_gt >= kf
        cur = jnp.where(adv, nxt, cur)
        c_ge = jnp.where(adv, c_gt, c_ge)
        return cur, c_ge, c_gt, jnp.max(jnp.where(adv, 1.0, 0.0))

    tau, c_ge, c_gt, _ = lax.while_loop(
        walk_cond, walk_body, (cur0, c_lo, jnp.zeros((1, TQ), F32), jnp.float32(1.0)))

    room = kf - c_gt

    def selection_mask(c, seen):
        off = chunk_off(c)
        tiles = range(CK // LANES)
        blks = [sc_ref[pl.ds(off + j * LANES, LANES), :] for j in tiles]
        ties = [blk == tau for blk in blks]
        local = [jnp.dot(tril_ref[...], jnp.where(tie, 1.0, 0.0).astype(BF16), preferred_element_type=F32)
                 for tie in ties]
        masks = []
        for j in tiles:
            keep = (blks[j] > tau) | (ties[j] & (local[j] + seen <= room))
            masks.append(jnp.where(keep, 0.0, NEG))
            seen = seen + local[j][LANES - 1:LANES, :]
        return jnp.concatenate(masks, axis=0), seen

    m_ref[...] = jnp.full(m_ref.shape, NEG, F32)
    l_ref[...] = jnp.zeros(l_ref.shape, F32)
    for acc in acc_refs:
        acc[...] = jnp.zeros(acc.shape, F32)

    rowi = lax.broadcasted_iota(jnp.int32, (LANES, TQ), 0)
    for pair in range(ATTN_HEADS // 2):
        qp = qT_ref[pair * LANES:(pair + 1) * LANES, :]
        zero = jnp.zeros_like(qp)
        qm_ref[pair] = jnp.concatenate([jnp.where(rowi < ATTN_HEAD_DIM, qp, zero),
                                        jnp.where(rowi >= ATTN_HEAD_DIM, qp, zero)], axis=1)
    ones_rows = jnp.ones((2 * SUBLANES, CK), BF16)

    pq_col = posqc_ref[...]
    batch = pl.program_id(0)
    pq_first, pq_consec, pq_min = (pinfo_ref[batch, r, i] for r in (0, 1, 3))
    far_bias = [biasc_ref[0, h] for h in range(ATTN_HEADS)]
    zero_bias = [biasc_ref[1, h] for h in range(ATTN_HEADS)]
    q_ge_k = (lax.broadcasted_iota(jnp.int32, (LANES, TQ), 1) >= lax.broadcasted_iota(jnp.int32, (LANES, TQ), 0))
    n_sub = CK // LANES
    n_pairs = ATTN_HEADS // 2

    def chunk_is_far(c):
        return (pq_min - ckmax_ref[batch, jnp.minimum(c, seq // CK - 1)]) >= FAR_N

    def stage_bias(c):
        off = chunk_off(c)
        pk_row = posk_ref[:, pl.ds(off, CK)]
        for j in range(n_sub):
            rows = slice(j * LANES, (j + 1) * LANES)
            g = c * n_sub + j
            pk_first, pk_consec, pk_max = (pinfo_ref[batch, r, g] for r in (0, 1, 2))
            all_far = (pq_min - pk_max) >= FAR_N
            all_masked = (off + j * LANES) > (i * TQ + TQ - 1)
            consecutive = (pq_consec > 0) & (pk_consec > 0)
            gap = pq_first - pk_first

            def fill_const(rows=rows):
                for h in range(ATTN_HEADS):
                    b_ref[h, rows, :] = jnp.full((LANES, TQ), far_bias[h], F32)

            def fill_gap0(rows=rows):
                for h in range(ATTN_HEADS):
                    b_ref[h, rows, :] = jnp.where(q_ge_k, toep_ref[h], zero_bias[h])

            def fill_gap128(rows=rows):
                for h in range(ATTN_HEADS):
                    b_ref[h, rows, :] = jnp.where(q_ge_k, far_bias[h], toep_ref[h])

            def fill_lookup(rows=rows):
                pk_sub = pk_row[:, rows]
                n_qk = jnp.clip(pq_col - pk_sub, 0, BIAS_TABLE_N - 1).astype(F32)
                n_kq = n_qk.T.astype(jnp.int32)
                for h in range(ATTN_HEADS):
                    tb = jnp.broadcast_to(tbl_ref[h:h + 1, :], (LANES, BIAS_TABLE_N))
                    b_ref[h, rows, :] = jnp.take_along_axis(tb, n_kq, axis=1)

            def fill_near(fill_gap0=fill_gap0, fill_gap128=fill_gap128, fill_lookup=fill_lookup,
                          consecutive=consecutive, gap=gap):
                lax.cond(consecutive & (gap == 0), fill_gap0,
                         lambda: lax.cond(consecutive & (gap == LANES), fill_gap128, fill_lookup))

            lax.cond(all_far | all_masked, fill_const, fill_near)

    def logits_phase(c, s_buf, const_bias, seen):
        off = chunk_off(c)
        mb, seen = selection_mask(c, seen)
        m_cur = []
        for pair in range(n_pairs):
            kc = k_ref[pl.ds(off, CK), pair * LANES:(pair + 1) * LANES]
            s2 = jnp.dot(kc, qm_ref[pair], preferred_element_type=F32)
            for sub, h in enumerate((2 * pair, 2 * pair + 1)):
                s = s2[:, sub * TQ:(sub + 1) * TQ] + mb
                if not const_bias:
                    s = s + b_ref[h]
                s_buf[h] = s
                top = col_reduce(_fold_rows(s, jnp.maximum), jnp.max)
                m_cur.append(top + far_bias[h] if const_bias else top)
        return m_cur, seen

    def update_phase(c, s_buf, m_cur, const_bias):
        off = chunk_off(c)
        m_prev = m_ref[...]
        l_prev = l_ref[...]
        if isinstance(m_cur, list):
            m_new = [jnp.maximum(m_prev[h:h + 1, :], m_cur[h]) for h in range(ATTN_HEADS)]
            alpha = [jnp.exp2(m_prev[h:h + 1, :] - m_new[h]) for h in range(ATTN_HEADS)]
        else:
            m_all = jnp.maximum(m_prev, m_cur)
            a_all = jnp.exp2(m_prev - m_all)
            m_new = [m_all[h:h + 1, :] for h in range(ATTN_HEADS)]
            alpha = [a_all[h:h + 1, :] for h in range(ATTN_HEADS)]
        l_new = []
        for pair in range(n_pairs):
            heads = (2 * pair, 2 * pair + 1)
            sub_m = [m_new[h] - far_bias[h] if const_bias else m_new[h] for h in heads]
            p2 = jnp.concatenate([jnp.exp2(s_buf[h] - sm).astype(BF16) for h, sm in zip(heads, sub_m)],
                                 axis=1)
            lhs = jnp.concatenate([vT_ref[pair * LANES:(pair + 1) * LANES, pl.ds(off, CK)], ones_rows],
                                  axis=0)
            out = jnp.dot(lhs, p2, preferred_element_type=F32)
            for sub, h in enumerate(heads):
                cols = slice(sub * TQ, (sub + 1) * TQ)
                acc = acc_refs[h]
                acc[...] = alpha[h] * acc[...] + out[sub * ATTN_HEAD_DIM:(sub + 1) * ATTN_HEAD_DIM, cols]
                l_new.append(alpha[h] * l_prev[h:h + 1, :] + out[LANES:LANES + 1, cols])
        l_ref[...] = jnp.concatenate(l_new, axis=0)
        m_ref[...] = jnp.concatenate(m_new, axis=0)

    n_far = lax.while_loop(lambda c: (c < nch) & chunk_is_far(c), lambda c: c + 1, jnp.int32(0))
    n_steps = n_far // 2
    def far_logits(c, s_buf, seen):
        m_cur, seen = logits_phase(c, s_buf, True, seen)
        return jnp.concatenate(m_cur, axis=0), seen

    no_ties = jnp.zeros((1, TQ), F32)
    m_first, seen_first = lax.cond(n_steps > 0, lambda: far_logits(0, s_ref, no_ties),
                                   lambda: (jnp.zeros((ATTN_HEADS, TQ), F32), no_ties))

    def far_step(t, carry):
        m_even, seen_even, _ = carry
        c = 2 * t
        m_odd, seen_odd = far_logits(c + 1, s2_ref, seen_even)
        update_phase(c, s_ref, m_even, True)
        m_even, seen_even = far_logits(jnp.minimum(c + 2, n_far - 1), s_ref, seen_odd)
        update_phase(c + 1, s2_ref, m_odd, True)
        return m_even, seen_even, seen_odd

    m_even, seen_even, seen = lax.fori_loop(0, n_steps, far_step, (m_first, seen_first, no_ties))
    leftover_staged = (n_steps > 0) & (n_far % 2 == 1)

    def finish_leftover():
        update_phase(n_far - 1, s_ref, m_even, True)
        return seen_even

    seen = lax.cond(leftover_staged, finish_leftover, lambda: seen)
    n_done = jnp.where(leftover_staged, n_far, 2 * n_steps)

    def tail_chunk(c, seen):
        def run(const_bias):
            if not const_bias:
                stage_bias(c)
            m_cur, seen_out = logits_phase(c, s_ref, const_bias, seen)
            update_phase(c, s_ref, m_cur, const_bias)
            return seen_out

        return lax.cond(chunk_is_far(c), lambda: run(True), lambda: run(False))

    lax.fori_loop(n_done, nch, tail_chunk, seen)

    outT = jnp.concatenate([acc_refs[h][...] / l_ref[h:h + 1, :] for h in range(ATTN_HEADS)], axis=0)
    o_ref[...] = outT.T.astype(o_ref.dtype)


def _toeplitz_kernel(tbl_ref, o_ref):
    q = lax.broadcasted_iota(jnp.int32, (LANES, LANES), 1)
    k = lax.broadcasted_iota(jnp.int32, (LANES, LANES), 0)
    idx = (q - k) & (BIAS_TABLE_N - 1)
    for h in range(ATTN_HEADS):
        tb = jnp.broadcast_to(tbl_ref[h:h + 1, :], (LANES, BIAS_TABLE_N))
        o_ref[h] = jnp.take_along_axis(tb, idx, axis=1)


def _sparse_attention(fm, tokb, ik, iwT, positions, rel_bias):
    B, S, _ = tokb.shape
    k_sel = min(TOPK_MAX, S // 4)
    bucket = _t5_bucket_table()
    tbl = rel_bias[bucket].T.astype(F32) * LOG2E
    bias_c = jnp.stack([tbl[:, BIAS_TABLE_N - 1], tbl[:, 0]])
    toep = pl.pallas_call(
        _toeplitz_kernel,
        out_shape=jax.ShapeDtypeStruct((ATTN_HEADS, LANES, LANES), F32),
        name="bias_toeplitz",
    )(tbl)
    pos_row = positions.reshape(B, 1, S)
    pos_col = positions.reshape(B, S, 1)
    pos_t = positions.reshape(B, S // LANES, LANES)
    consec = jnp.all(pos_t - pos_t[:, :, :1] == jnp.arange(LANES, dtype=positions.dtype), axis=-1)
    pinfo = jnp.stack([pos_t[:, :, 0], consec.astype(jnp.int32), jnp.max(pos_t, axis=-1),
                       jnp.min(pos_t, axis=-1)], axis=1).astype(jnp.int32)
    ck_max = jnp.max(positions.reshape(B, S // CK, CK), axis=-1)
    tril = jnp.tril(jnp.ones((LANES, LANES), BF16))
    sq = pl.Squeezed()
    iq_blk = (2 * ATTN_W) // IDX_Q_W
    k_blk = (tokb.shape[2] - ATTN_W) // ATTN_W
    return pl.pallas_call(
        functools.partial(_attn_kernel, k_sel=k_sel, seq=S),
        grid=(B, S // TQ),
        in_specs=[pl.BlockSpec(memory_space=pltpu.SMEM),
                  pl.BlockSpec(memory_space=pltpu.SMEM),
                  pl.BlockSpec(memory_space=pltpu.SMEM),
                  pl.BlockSpec((sq, IDX_Q_W, TQ), lambda b, i: (b, iq_blk, i)),
                  pl.BlockSpec((sq, SUBLANES, TQ), lambda b, i: (b, 0, i)),
                  pl.BlockSpec((sq, TQ, 1), lambda b, i: (b, i, 0)),
                  pl.BlockSpec((sq, 1, S), lambda b, i: (b, 0, 0)),
                  pl.BlockSpec((sq, S, IDX_HEAD_DIM), lambda b, i: (b, 0, 0)),
                  pl.BlockSpec((sq, ATTN_W, TQ), lambda b, i: (b, 0, i)),
                  pl.BlockSpec((sq, S, ATTN_W), lambda b, i: (b, 0, k_blk)),
                  pl.BlockSpec((sq, ATTN_W, S), lambda b, i: (b, 1, 0)),
                  pl.BlockSpec((ATTN_HEADS, BIAS_TABLE_N), lambda b, i: (0, 0)),
                  pl.BlockSpec((ATTN_HEADS, LANES, LANES), lambda b, i: (0, 0, 0)),
                  pl.BlockSpec((LANES, LANES), lambda b, i: (0, 0))],
        out_specs=pl.BlockSpec((sq, TQ, ATTN_W), lambda b, i: (b, i, 0)),
        out_shape=jax.ShapeDtypeStruct((B, S, ATTN_W), BF16),
        scratch_shapes=[pltpu.VMEM((S, TQ), F32),
                        pltpu.VMEM((ATTN_HEADS, CK, TQ), F32),
                        pltpu.VMEM((ATTN_HEADS, CK, TQ), F32),
                        pltpu.VMEM((ATTN_HEADS, CK, TQ), F32),
                        pltpu.VMEM((ATTN_HEADS // 2, LANES, 2 * TQ), BF16),
                        pltpu.VMEM((ATTN_HEADS, TQ), F32),
                        pltpu.VMEM((ATTN_HEADS, TQ), F32)]
                       + [pltpu.VMEM((ATTN_HEAD_DIM, TQ), F32)] * ATTN_HEADS,
        compiler_params=_cparams(2),
        name="sparse_attention",
    )(bias_c, pinfo, ck_max, fm, iwT, pos_col, pos_row, ik, fm, tokb, fm, tbl, toep, tril)


def _retention_kernel(q_ref, k_ref, v_ref, g_ref, decay_ref, xi_ref, zeta_ref, gch_ref, o_ref, r_ref):
    @pl.when(pl.program_id(0) == 0)
    def _():
        r_ref[...] = jnp.zeros(r_ref.shape, F32)

    C = RET_CHUNK
    lane = lax.broadcasted_iota(jnp.int32, (C, LANES), 1)
    row = lax.broadcasted_iota(jnp.int32, (LANES, RET_V_DIM), 0)
    for ci in range(q_ref.shape[1] // C):
        rows = slice(ci * C, (ci + 1) * C)
        for pair in range(RET_HEADS // 2):
            for b in range(q_ref.shape[0]):
                q_pair = q_ref[b, rows, pair * LANES:(pair + 1) * LANES]
                k_pair = k_ref[b, rows, pair * LANES:(pair + 1) * LANES]
                v_pair = v_ref[b, rows, 2 * pair * RET_V_DIM:(2 * pair + 2) * RET_V_DIM]
                r_pair = r_ref[b, pair]
                r_bf = r_pair.astype(BF16)
                for sub in range(2):
                    h = 2 * pair + sub
                    in_head = (lane >= sub * RET_QK_DIM) & (lane < (sub + 1) * RET_QK_DIM)
                    qm = jnp.where(in_head, q_pair, jnp.zeros_like(q_pair))
                    v_h = v_pair[:, sub * RET_V_DIM:(sub + 1) * RET_V_DIM]
                    inner = lax.dot_general(qm, k_pair, (((1,), (1,)), ((), ())),
                                            preferred_element_type=F32) * decay_ref[h]
                    o = (jnp.dot(inner.astype(BF16), v_h, preferred_element_type=F32)
                         + jnp.dot(qm, r_bf, preferred_element_type=F32) * xi_ref[h])
                    mu = jnp.mean(o, axis=-1, keepdims=True)
                    d = o - mu
                    var = jnp.mean(d * d, axis=-1, keepdims=True)
                    hn = d * lax.rsqrt(var + LN_EPS)
                    gate = g_ref[b, rows, h * RET_V_DIM:(h + 1) * RET_V_DIM].astype(F32)
                    o_ref[b, rows, h * RET_V_DIM:(h + 1) * RET_V_DIM] = (gate * hn).astype(o_ref.dtype)
                kz = (k_pair.astype(F32) * zeta_ref[pair]).astype(BF16)
                upd = lax.dot_general(kz, v_pair, (((0,), (0,)), ((), ())),
                                      preferred_element_type=F32)
                r_ref[b, pair] = (r_pair * gch_ref[pair]
                                  + jnp.where(row < RET_QK_DIM, upd[:, :RET_V_DIM], upd[:, RET_V_DIM:]))


def _retention(qk, tokb, gates, B, S):
    C = RET_CHUNK
    H = RET_HEADS
    nc = S // C
    gamma = 1.0 - 2.0 ** (-5.0 - jnp.arange(H, dtype=F32))
    log_g = jnp.log(gamma)
    n = jnp.arange(C, dtype=F32)
    diff = n[:, None] - n[None, :]
    decay_in = jnp.where(diff[None] >= 0, jnp.exp(log_g[:, None, None] * jnp.maximum(diff, 0.0)[None]), 0.0)
    xi = jnp.exp(log_g[None, :] * (n[:, None] + 1.0))
    zeta = jnp.exp(log_g[None, :] * (C - 1.0 - n[:, None]))
    g_chunk = jnp.exp(log_g * C)
    xi_b = jnp.broadcast_to(xi.T[:, :, None], (H, C, RET_V_DIM))
    zeta_b = jnp.repeat(zeta, RET_QK_DIM, axis=1).reshape(C, H // 2, LANES).transpose(1, 0, 2)
    gch_b = jnp.broadcast_to(jnp.repeat(g_chunk, RET_QK_DIM).reshape(H // 2, LANES, 1),
                             (H // 2, LANES, RET_V_DIM))
    qk3, tok3, gate3 = (a.reshape(B, S, a.shape[-1]) for a in (qk, tokb, gates))
    CS = C * RET_STEP_CHUNKS
    out = pl.pallas_call(
        _retention_kernel,
        grid=(nc // RET_STEP_CHUNKS,),
        in_specs=[pl.BlockSpec((B, CS, RET_QK_W), lambda i: (0, i, 0)),
                  pl.BlockSpec((B, CS, RET_QK_W), lambda i: (0, i, 1)),
                  pl.BlockSpec((B, CS, RET_V_W), lambda i: (0, i, 0)),
                  pl.BlockSpec((B, CS, RET_V_W), lambda i: (0, i, 0)),
                  pl.BlockSpec((H, C, C), lambda i: (0, 0, 0)),
                  pl.BlockSpec((H, C, RET_V_DIM), lambda i: (0, 0, 0)),
                  pl.BlockSpec((H // 2, C, LANES), lambda i: (0, 0, 0)),
                  pl.BlockSpec((H // 2, LANES, RET_V_DIM), lambda i: (0, 0, 0))],
        out_specs=pl.BlockSpec((B, CS, RET_V_W), lambda i: (0, i, 0)),
        out_shape=jax.ShapeDtypeStruct((B, S, RET_V_W), BF16),
        scratch_shapes=[pltpu.VMEM((B, H // 2, LANES, RET_V_DIM), F32)],
        compiler_params=_cparams(1),
        name="retention",
    )(qk3, qk3, tok3, gate3, decay_in, xi_b, zeta_b, gch_b)
    return out.reshape(B * S, RET_V_W)


def _layer_norm(z, g, b):
    mu = jnp.mean(z, axis=-1, keepdims=True)
    d = z - mu
    var = jnp.mean(d * d, axis=-1, keepdims=True)
    return d * lax.rsqrt(var + LN_EPS) * g + b


def _merge_kernel(x_ref, ya_ref, yr_ref, ga_ref, gr_ref, wa_ref, wr_ref, wo_ref, g_ref, b_ref,
                  x1_ref):
    a = jnp.dot(ya_ref[...], wa_ref[...], preferred_element_type=F32)
    r = jnp.dot(yr_ref[...], wr_ref[...], preferred_element_type=F32)
    h = ga_ref[...].astype(F32) * a + gr_ref[...].astype(F32) * r
    mix = jnp.dot(h.astype(BF16), wo_ref[...], preferred_element_type=F32)
    x1_ref[...] = _layer_norm(DEEPNORM_ALPHA * x_ref[...] + mix, g_ref[...], b_ref[...])


def _merge(x, ya, yr, gates, wa, wr, wo, g, b, tm=MERGE_TM):
    T, D = x.shape
    tm = min(tm, T)
    row = lambda i: (i, 0)
    fixed = lambda i: (0, 0)
    return pl.pallas_call(
        _merge_kernel,
        grid=(T // tm,),
        in_specs=[pl.BlockSpec((tm, D), row),
                  pl.BlockSpec((tm, ya.shape[1]), row),
                  pl.BlockSpec((tm, yr.shape[1]), row),
                  pl.BlockSpec((tm, D), lambda i: (i, 1)),
                  pl.BlockSpec((tm, D), lambda i: (i, 2)),
                  pl.BlockSpec(wa.shape, fixed),
                  pl.BlockSpec(wr.shape, fixed),
                  pl.BlockSpec(wo.shape, fixed),
                  pl.BlockSpec((1, D), fixed),
                  pl.BlockSpec((1, D), fixed)],
        out_specs=pl.BlockSpec((tm, D), row),
        out_shape=jax.ShapeDtypeStruct((T, D), F32),
        compiler_params=_cparams(1),
        name="merge",
    )(x, ya, yr, gates, gates, wa, wr, wo, g, b)


def _ffn_kernel(x1_ref, wu_ref, wd_ref, g_ref, b_ref, o_ref, acc_ref, xb_ref):
    f = pl.program_id(1)

    @pl.when(f == 0)
    def _():
        acc_ref[...] = jnp.zeros(acc_ref.shape, F32)
        xb_ref[...] = x1_ref[...].astype(BF16)

    hid = jnp.maximum(jnp.dot(xb_ref[...], wu_ref[...], preferred_element_type=F32), 0.0)
    acc_ref[...] += jnp.dot((hid * hid).astype(BF16), wd_ref[...], preferred_element_type=F32)

    @pl.when(f == pl.num_programs(1) - 1)
    def _():
        o_ref[...] = _layer_norm(DEEPNORM_ALPHA * x1_ref[...] + acc_ref[...], g_ref[...], b_ref[...])


def _ffn(x1, wu, wd, g, b, tm=FFN_TM, tf=FFN_TF):
    T, D = x1.shape
    F = wu.shape[1]
    tm = min(tm, T)
    return pl.pallas_call(
        _ffn_kernel,
        grid=(T // tm, F // tf),
        in_specs=[pl.BlockSpec((tm, D), lambda i, f: (i, 0)),
                  pl.BlockSpec((D, tf), lambda i, f: (0, f)),
                  pl.BlockSpec((tf, D), lambda i, f: (f, 0)),
                  pl.BlockSpec((1, D), lambda i, f: (0, 0)),
                  pl.BlockSpec((1, D), lambda i, f: (0, 0))],
        out_specs=pl.BlockSpec((tm, D), lambda i, f: (i, 0)),
        out_shape=jax.ShapeDtypeStruct((T, D), F32),
        scratch_shapes=[pltpu.VMEM((tm, D), F32), pltpu.VMEM((tm, D), BF16)],
        compiler_params=_cparams(2),
        name="ffn",
    )(x1, wu, wd, g, b)


def _rot_half_weight(wT):
    N, D = wT.shape
    half = RET_QK_DIM // 2
    wh = wT.reshape(N // RET_QK_DIM, 2, half, D)
    return jnp.stack([-wh[:, 1], wh[:, 0]], axis=1).reshape(N, D)


def kernel(x, positions, w_in, rel_bias, idx_k_ln_g, idx_k_ln_b, w_attn_branch, w_ret_branch,
           w_out, ln_mix_g, ln_mix_b, w_up, w_down, ln_ffn_g, ln_ffn_b):
    B, S, D = x.shape
    T = B * S
    sizes = (ATTN_W, ATTN_W, ATTN_W, IDX_Q_W, IDX_HEAD_DIM, IDX_HEADS,
             RET_QK_W, RET_QK_W, RET_V_W, RET_V_W, D, D)
    offs = [0] + [int(o) for o in np.cumsum(sizes)]
    cos, sin = _rope_tables(positions)
    xf = x.reshape(T, D)
    for l in range(DEPTH):
        wT = jnp.swapaxes(w_in[l], 0, 1).astype(BF16)
        rows = [wT[offs[k]:offs[k + 1]] for k in range(len(sizes))]
        (w_qa, w_ka, w_va, w_iq, w_ik, w_iw, w_qr, w_kr, w_vr, w_gr, w_ga, w_gtr) = rows
        w_kr = w_kr * (RET_QK_DIM ** -0.5)

        pad = LANES - IDX_HEAD_DIM - IDX_HEADS
        w_idx = jnp.concatenate([w_ik, w_iw, jnp.zeros((pad, D), BF16)], axis=0)
        g_pad = jnp.concatenate([idx_k_ln_g[l], jnp.zeros((LANES - IDX_HEAD_DIM,), F32)]).reshape(1, LANES)
        b_pad = jnp.concatenate([idx_k_ln_b[l], jnp.zeros((LANES - IDX_HEAD_DIM,), F32)]).reshape(1, LANES)
        ik, iwT, xb = _proj_idx(xf, w_idx, g_pad, b_pad, (IDX_HEAD_DIM ** -0.5) * (IDX_HEADS ** -0.5), B, S)
        ik = ik.reshape(B, S, IDX_HEAD_DIM)

        fm = _proj_t(xb, jnp.concatenate([w_qa, w_va, w_iq], axis=0), B, S, BF16,
                     scaled_rows=ATTN_W, scale=ATTN_HEAD_DIM ** -0.5 * LOG2E)
        tokb = _proj(xb, jnp.concatenate([w_vr, w_ka], axis=0), BF16)
        gates = _proj_gates(xb, jnp.concatenate([w_gr, w_ga, w_gtr], axis=0), tn=D)
        w_rope = jnp.concatenate([w_qr, w_kr], axis=0)
        w_rope_rot = jnp.concatenate([_rot_half_weight(w_qr), _rot_half_weight(w_kr)], axis=0)
        qk_r = _proj_rope(xb, w_rope, w_rope_rot, cos, sin)

        y_a = _sparse_attention(fm, tokb.reshape(B, S, -1), ik, iwT, positions, rel_bias)
        y_r = _retention(qk_r, tokb, gates, B, S)
        x1 = _merge(xf, y_a.reshape(T, ATTN_W), y_r, gates,
                    w_attn_branch[l].astype(BF16), w_ret_branch[l].astype(BF16),
                    w_out[l].astype(BF16), ln_mix_g[l].reshape(1, D), ln_mix_b[l].reshape(1, D))
        xf = _ffn(x1, w_up[l].astype(BF16), w_down[l].astype(BF16),
                  ln_ffn_g[l].reshape(1, D), ln_ffn_b[l].reshape(1, D))
    return xf.reshape(B, S, D)
```

```python
import functools
import math

import numpy as np
import jax
import jax.numpy as jnp
from jax import lax
from jax.experimental import pallas as pl
from jax.experimental.pallas import tpu as pltpu

F32 = jnp.float32
BF16 = jnp.bfloat16

ATTN_HEADS = 8
ATTN_HEAD_DIM = 64
ATTN_W = ATTN_HEADS * ATTN_HEAD_DIM
IDX_HEADS = 4
IDX_HEAD_DIM = 64
IDX_Q_W = IDX_HEADS * IDX_HEAD_DIM
TOPK_MAX = 256
RET_HEADS = 8
RET_QK_DIM = 64
RET_V_DIM = 128
RET_QK_W = RET_HEADS * RET_QK_DIM
RET_V_W = RET_HEADS * RET_V_DIM
RET_CHUNK = 128
RET_STEP_CHUNKS = 4
ROPE_BASE = 10000.0
NUM_BUCKETS = 32
MAX_DISTANCE = 128
LN_EPS = 1e-5
DEPTH = 1
DEEPNORM_ALPHA = (2.0 * DEPTH) ** 0.25

LANES = 128
SUBLANES = 8
VMEM_LIMIT = 56 * 1024 * 1024

PROJ_TM = 1024
GATES_TM = 2048
MERGE_TM = 512
FFN_TM = 1024
FFN_TF = 1024

TQ = 128
CK = 512
NEG = -1e30
LOG2E = math.log2(math.e)
BISECT_ROUNDS = 19
BIAS_TABLE_N = 128
FAR_N = 113


def _cparams(n_grid):
    return pltpu.CompilerParams(
        dimension_semantics=("arbitrary",) * n_grid,
        vmem_limit_bytes=VMEM_LIMIT)


def _trig_kernel(pos_ref, inv_ref, cos_ref, sin_ref):
    ang = pos_ref[...] * inv_ref[...]
    tr = ang.shape[0]
    half = RET_QK_DIM // 2
    per_row = LANES // half
    lane = lax.broadcasted_iota(jnp.int32, ang.shape, 1)
    for ref, val in ((cos_ref, jnp.cos(ang)), (sin_ref, jnp.sin(ang))):
        for k in range(per_row):
            own = jnp.where((lane >= k * half) & (lane < (k + 1) * half), val, 0.0)
            spread = own
            for j in range(1, per_row):
                spread = spread + pltpu.roll(own, j * half, 1)
            ref[pl.ds(k, tr, stride=per_row), :] = spread


def _rope_tables(positions):
    B, S = positions.shape
    half = RET_QK_DIM // 2
    inv = ROPE_BASE ** (-jnp.arange(half, dtype=F32) / half)
    per_row = LANES // half
    rows = B * S // per_row
    pos_e = jnp.repeat(positions.astype(F32).reshape(rows, per_row), half, axis=1)
    inv_e = jnp.tile(inv, per_row).reshape(1, LANES)
    tr = min(rows, 1024)
    return pl.pallas_call(
        _trig_kernel,
        grid=(rows // tr,),
        in_specs=[pl.BlockSpec((tr, LANES), lambda i: (i, 0)),
                  pl.BlockSpec((1, LANES), lambda i: (0, 0))],
        out_specs=[pl.BlockSpec((per_row * tr, LANES), lambda i: (i, 0))] * 2,
        out_shape=[jax.ShapeDtypeStruct((B * S, LANES), F32)] * 2,
        compiler_params=_cparams(1),
        name="rope_tables",
    )(pos_e, inv_e)


def _x_wt(x, wT):
    return lax.dot_general(x, wT, (((1,), (1,)), ((), ())), preferred_element_type=F32)


def _proj_kernel(x_ref, w_ref, o_ref):
    o_ref[...] = _x_wt(x_ref[...], w_ref[...]).astype(o_ref.dtype)


def _proj(xb, wT, out_dtype, tm=PROJ_TM):
    T, D = xb.shape
    N = wT.shape[0]
    tn = N
    tm = min(tm, T)
    return pl.pallas_call(
        _proj_kernel,
        grid=(T // tm, N // tn),
        in_specs=[pl.BlockSpec((tm, D), lambda i, j: (i, 0)),
                  pl.BlockSpec((tn, D), lambda i, j: (j, 0))],
        out_specs=pl.BlockSpec((tm, tn), lambda i, j: (i, j)),
        out_shape=jax.ShapeDtypeStruct((T, N), out_dtype),
        compiler_params=_cparams(2),
        name="proj",
    )(xb, wT)


def _proj_gates_kernel(x_ref, w_ref, o_ref):
    acc = _x_wt(x_ref[...], w_ref[...])
    sig = 0.5 * jnp.tanh(0.5 * acc) + 0.5
    o_ref[...] = jnp.where(pl.program_id(1) == 0, acc * sig, sig).astype(o_ref.dtype)


def _proj_gates(xb, wT, tn, tm=GATES_TM):
    T, D = xb.shape
    N = wT.shape[0]
    tm = min(tm, T)
    return pl.pallas_call(
        _proj_gates_kernel,
        grid=(T // tm, N // tn),
        in_specs=[pl.BlockSpec((tm, D), lambda i, j: (i, 0)),
                  pl.BlockSpec((tn, D), lambda i, j: (j, 0))],
        out_specs=pl.BlockSpec((tm, tn), lambda i, j: (i, j)),
        out_shape=jax.ShapeDtypeStruct((T, N), BF16),
        compiler_params=_cparams(2),
        name="proj_gates",
    )(xb, wT)


def _proj_t_kernel(wT_ref, x_ref, o_ref, *, scaled_rows, scale):
    acc = lax.dot_general(wT_ref[...], x_ref[...], (((1,), (1,)), ((), ())), preferred_element_type=F32)
    o_ref[:scaled_rows, :] = (acc[:scaled_rows] * scale).astype(o_ref.dtype)
    o_ref[scaled_rows:, :] = acc[scaled_rows:].astype(o_ref.dtype)


def _proj_t(xb, wT, B, S, out_dtype, scaled_rows, scale, tm=PROJ_TM):
    T, D = xb.shape
    N = wT.shape[0]
    tn = N
    tm = min(tm, S)
    nsb = S // tm
    return pl.pallas_call(
        functools.partial(_proj_t_kernel, scaled_rows=scaled_rows, scale=scale),
        grid=(T // tm, N // tn),
        in_specs=[pl.BlockSpec((tn, D), lambda i, j: (j, 0)),
                  pl.BlockSpec((tm, D), lambda i, j: (i, 0))],
        out_specs=pl.BlockSpec((pl.Squeezed(), tn, tm), lambda i, j: (i // nsb, j, i % nsb)),
        out_shape=jax.ShapeDtypeStruct((B, N, S), out_dtype),
        compiler_params=_cparams(2),
        name="proj_t",
    )(wT, xb)


def _proj_rope_kernel(x_ref, w_ref, wr_ref, cos_ref, sin_ref, o_ref):
    x = x_ref[...]
    a = _x_wt(x, w_ref[...])
    r = _x_wt(x, wr_ref[...])
    reps = a.shape[1] // LANES
    cos = jnp.concatenate([cos_ref[...]] * reps, axis=1)
    sin = jnp.concatenate([sin_ref[...]] * reps, axis=1)
    o_ref[...] = (a * cos + r * sin).astype(o_ref.dtype)


def _proj_rope(xb, wT, wT_rot, cos, sin, tm=PROJ_TM):
    T, D = xb.shape
    N = wT.shape[0]
    tn = N
    tm = min(tm, T)
    return pl.pallas_call(
        _proj_rope_kernel,
        grid=(T // tm, N // tn),
        in_specs=[pl.BlockSpec((tm, D), lambda i, j: (i, 0)),
                  pl.BlockSpec((tn, D), lambda i, j: (j, 0)),
                  pl.BlockSpec((tn, D), lambda i, j: (j, 0)),
                  pl.BlockSpec((tm, LANES), lambda i, j: (i, 0)),
                  pl.BlockSpec((tm, LANES), lambda i, j: (i, 0))],
        out_specs=pl.BlockSpec((tm, tn), lambda i, j: (i, j)),
        out_shape=jax.ShapeDtypeStruct((T, N), BF16),
        compiler_params=_cparams(2),
        name="proj_rope",
    )(xb, wT, wT_rot, cos, sin)


def _proj_idx_kernel(x_ref, w_ref, g_ref, b_ref, ik_ref, iwT_ref, xb_ref, *, iw_scale):
    xb = x_ref[...].astype(BF16)
    xb_ref[...] = xb
    acc = _x_wt(xb, w_ref[...])
    lane = lax.broadcasted_iota(jnp.int32, acc.shape, 1)
    is_k = lane < IDX_HEAD_DIM
    mu = jnp.sum(jnp.where(is_k, acc, 0.0), axis=-1, keepdims=True) / IDX_HEAD_DIM
    d = acc - mu
    var = jnp.sum(jnp.where(is_k, d * d, 0.0), axis=-1, keepdims=True) / IDX_HEAD_DIM
    ln = d * lax.rsqrt(var + LN_EPS) * g_ref[...] + b_ref[...]
    ik_ref[...] = ln[:, :IDX_HEAD_DIM].astype(ik_ref.dtype)
    iwT_ref[...] = (acc * iw_scale).T[IDX_HEAD_DIM:IDX_HEAD_DIM + SUBLANES, :]


def _proj_idx(x, w_pad, g_pad, b_pad, iw_scale, B, S, tm=PROJ_TM):
    T, D = x.shape
    tm = min(tm, S)
    nsb = S // tm
    return pl.pallas_call(
        functools.partial(_proj_idx_kernel, iw_scale=iw_scale),
        grid=(T // tm,),
        in_specs=[pl.BlockSpec((tm, D), lambda i: (i, 0)),
                  pl.BlockSpec((LANES, D), lambda i: (0, 0)),
                  pl.BlockSpec((1, LANES), lambda i: (0, 0)),
                  pl.BlockSpec((1, LANES), lambda i: (0, 0))],
        out_specs=[pl.BlockSpec((tm, IDX_HEAD_DIM), lambda i: (i, 0)),
                   pl.BlockSpec((pl.Squeezed(), SUBLANES, tm), lambda i: (i // nsb, 0, i % nsb)),
                   pl.BlockSpec((tm, D), lambda i: (i, 0))],
        out_shape=[jax.ShapeDtypeStruct((T, IDX_HEAD_DIM), BF16),
                   jax.ShapeDtypeStruct((B, SUBLANES, S), F32),
                   jax.ShapeDtypeStruct((T, D), BF16)],
        compiler_params=_cparams(1),
        name="proj_idx",
    )(x, w_pad, g_pad, b_pad)


def _t5_bucket_table():
    n = np.arange(BIAS_TABLE_N)
    max_exact = NUM_BUCKETS // 2
    nf = np.maximum(n, 1).astype(np.float64)
    large = max_exact + (np.log(nf / max_exact) / math.log(MAX_DISTANCE / max_exact)
                         * (NUM_BUCKETS - max_exact)).astype(np.int64)
    large = np.minimum(large, NUM_BUCKETS - 1)
    bucket = np.where(n < max_exact, n, large)
    assert np.all(bucket[FAR_N:] == NUM_BUCKETS - 1) and bucket[FAR_N - 1] != NUM_BUCKETS - 1
    return bucket.astype(np.int32)


def _fold_rows(a, op):
    parts = [a[r:r + SUBLANES] for r in range(0, a.shape[0], SUBLANES)]
    while len(parts) > 1:
        nxt = [op(parts[k], parts[k + 1]) for k in range(0, len(parts) - 1, 2)]
        if len(parts) % 2:
            nxt.append(parts[-1])
        parts = nxt
    return parts[0]


def _attn_kernel(biasc_ref, pinfo_ref, ckmax_ref, iqT_ref, iwT_ref, posqc_ref, posk_ref, ik_ref, qT_ref, k_ref, vT_ref,
                 tbl_ref, toep_ref, tril_ref, o_ref, sc_ref, s_ref, s2_ref, b_ref, qm_ref, m_ref, l_ref, *acc_refs, k_sel, seq):
    i = pl.program_id(1)
    nch = (i * TQ + TQ + CK - 1) // CK
    q_idx = i * TQ + lax.broadcasted_iota(jnp.int32, (1, TQ), 1)
    kf = float(k_sel)

    def chunk_off(c):
        return pl.multiple_of(c * CK, CK)

    def key_idx(off):
        return off + lax.broadcasted_iota(jnp.int32, (CK, TQ), 0)

    def col_reduce(part, op):
        return op(part, axis=0, keepdims=True)

    iqT = iqT_ref[...]
    iwT = iwT_ref[...]
    iq_wide = jnp.concatenate([iqT[h * IDX_HEAD_DIM:(h + 1) * IDX_HEAD_DIM, :] for h in range(IDX_HEADS)], axis=1)

    def score_matmul(c, z_buf):
        ikc = ik_ref[pl.ds(chunk_off(c), CK), :]
        z = jnp.dot(ikc, iq_wide, preferred_element_type=F32)
        for h in range(IDX_HEADS):
            z_buf[h] = z[:, h * TQ:(h + 1) * TQ]

    def score_finish(c, z_buf, carry, masked):
        mn, mx = carry
        off = chunk_off(c)
        s = None
        for h in range(IDX_HEADS):
            t = jnp.maximum(z_buf[h], 0.0) * iwT[h:h + 1, :]
            s = t if s is None else s + t
        if masked:
            causal = key_idx(off) <= q_idx
            s_lo = jnp.where(causal, s, -jnp.inf)
            s_hi = jnp.where(causal, s, jnp.inf)
        else:
            s_lo = s_hi = s
        sc_ref[pl.ds(off, CK), :] = s_lo
        mn = jnp.minimum(mn, _fold_rows(s_hi, jnp.minimum))
        mx = jnp.maximum(mx, _fold_rows(s_lo, jnp.maximum))
        return mn, mx

    def score_chunk(c, carry, masked):
        score_matmul(c, s_ref)
        return score_finish(c, s_ref, carry, masked)

    n_inner = nch - 1
    n_pairs_sc = n_inner // 2

    @pl.when(n_pairs_sc > 0)
    def _():
        score_matmul(0, s_ref)

    def score_step(t, carry):
        c = 2 * t
        score_matmul(c + 1, s2_ref)
        carry = score_finish(c, s_ref, carry, False)
        score_matmul(jnp.minimum(c + 2, 2 * n_pairs_sc - 2), s_ref)
        return score_finish(c + 1, s2_ref, carry, False)

    mn8, mx8 = lax.fori_loop(0, n_pairs_sc, score_step,
                             (jnp.full((SUBLANES, TQ), jnp.inf, F32), jnp.full((SUBLANES, TQ), -jnp.inf, F32)))
    mn8, mx8 = lax.fori_loop(2 * n_pairs_sc, n_inner, functools.partial(score_chunk, masked=False), (mn8, mx8))
    mn8, mx8 = score_chunk(nch - 1, (mn8, mx8), masked=True)
    mn = col_reduce(mn8, jnp.min)
    mx = col_reduce(mx8, jnp.max)

    def over_chunks(fn, init):
        def one(c, carry):
            return fn(sc_ref[pl.ds(chunk_off(c), CK), :], carry)

        carry = lax.fori_loop(0, nch // 2, lambda t, carry: one(2 * t + 1, one(2 * t, carry)), init)
        return lax.fori_loop(2 * (nch // 2), nch, one, carry)

    def count(pred_fn):
        acc = over_chunks(lambda blk, acc: acc + _fold_rows(jnp.where(pred_fn(blk), 1.0, 0.0), jnp.add),
                          jnp.zeros((SUBLANES, TQ), F32))
        return col_reduce(acc, jnp.sum)

    def bisect_round(_, st):
        lo, hi, c_lo = st
        mid = 0.5 * (lo + hi)
        c = count(lambda blk: blk >= mid)
        ok = c >= kf
        return jnp.where(ok, mid, lo), jnp.where(ok, hi, mid), jnp.where(ok, c, c_lo)

    c_all = (q_idx + 1).astype(F32)
    lo, hi, c_lo = lax.fori_loop(0, BISECT_ROUNDS, bisect_round, (mn, mx, c_all))

    cur0 = col_reduce(over_chunks(
        lambda blk, acc: jnp.minimum(acc, _fold_rows(jnp.where(blk >= lo, blk, jnp.inf), jnp.minimum)),
        jnp.full((SUBLANES, TQ), jnp.inf, F32)), jnp.min)

    def walk_cond(st):
        return st[3] > 0.0

    def walk_body(st):
        cur, c_ge, _, _ = st

        def body(blk, carry):
            cnt, nxt = carry
            gt = blk > cur
            cnt = cnt + _fold_rows(jnp.where(gt, 1.0, 0.0), jnp.add)
            nxt = jnp.minimum(nxt, _fold_rows(jnp.where(gt, blk, jnp.inf), jnp.minimum))
            return cnt, nxt

        cnt, nxt = over_chunks(body, (jnp.zeros((SUBLANES, TQ), F32), jnp.full((SUBLANES, TQ), jnp.inf, F32)))
        c_gt = col_reduce(cnt, jnp.sum)
        nxt = col_reduce(nxt, jnp.min)
        adv = c_gt >= kf
        cur = jnp.where(adv, nxt, cur)
        c_ge = jnp.where(adv, c_gt, c_ge)
        return cur, c_ge, c_gt, jnp.max(jnp.where(adv, 1.0, 0.0))

    tau, c_ge, c_gt, _ = lax.while_loop(
        walk_cond, walk_body, (cur0, c_lo, jnp.zeros((1, TQ), F32), jnp.float32(1.0)))

    room = kf - c_gt

    def selection_mask(c, seen):
        off = chunk_off(c)
        tiles = range(CK // LANES)
        blks = [sc_ref[pl.ds(off + j * LANES, LANES), :] for j in tiles]
        ties = [blk == tau for blk in blks]
        local = [jnp.dot(tril_ref[...], jnp.where(tie, 1.0, 0.0).astype(BF16), preferred_element_type=F32)
                 for tie in ties]
        masks = []
        for j in tiles:
            keep = (blks[j] > tau) | (ties[j] & (local[j] + seen <= room))
            masks.append(jnp.where(keep, 0.0, NEG))
            seen = seen + local[j][LANES - 1:LANES, :]
        return jnp.concatenate(masks, axis=0), seen

    m_ref[...] = jnp.full(m_ref.shape, NEG, F32)
    l_ref[...] = jnp.zeros(l_ref.shape, F32)
    for acc in acc_refs:
        acc[...] = jnp.zeros(acc.shape, F32)

    rowi = lax.broadcasted_iota(jnp.int32, (LANES, TQ), 0)
    for pair in range(ATTN_HEADS // 2):
        qp = qT_ref[pair * LANES:(pair + 1) * LANES, :]
        zero = jnp.zeros_like(qp)
        qm_ref[pair] = jnp.concatenate([jnp.where(rowi < ATTN_HEAD_DIM, qp, zero),
                                        jnp.where(rowi >= ATTN_HEAD_DIM, qp, zero)], axis=1)
    ones_rows = jnp.ones((2 * SUBLANES, CK), BF16)

    pq_col = posqc_ref[...]
    batch = pl.program_id(0)
    pq_first, pq_consec, pq_min = (pinfo_ref[batch, r, i] for r in (0, 1, 3))
    far_bias = [biasc_ref[0, h] for h in range(ATTN_HEADS)]
    zero_bias = [biasc_ref[1, h] for h in range(ATTN_HEADS)]
    q_ge_k = (lax.broadcasted_iota(jnp.int32, (LANES, TQ), 1) >= lax.broadcasted_iota(jnp.int32, (LANES, TQ), 0))
    n_sub = CK // LANES
    n_pairs = ATTN_HEADS // 2

    def chunk_is_far(c):
        return (pq_min - ckmax_ref[batch, jnp.minimum(c, seq // CK - 1)]) >= FAR_N

    def stage_bias(c):
        off = chunk_off(c)
        pk_row = posk_ref[:, pl.ds(off, CK)]
        for j in range(n_sub):
            rows = slice(j * LANES, (j + 1) * LANES)
            g = c * n_sub + j
            pk_first, pk_consec, pk_max = (pinfo_ref[batch, r, g] for r in (0, 1, 2))
            all_far = (pq_min - pk_max) >= FAR_N
            all_masked = (off + j * LANES) > (i * TQ + TQ - 1)
            consecutive = (pq_consec > 0) & (pk_consec > 0)
            gap = pq_first - pk_first

            def fill_const(rows=rows):
                for h in range(ATTN_HEADS):
                    b_ref[h, rows, :] = jnp.full((LANES, TQ), far_bias[h], F32)

            def fill_gap0(rows=rows):
                for h in range(ATTN_HEADS):
                    b_ref[h, rows, :] = jnp.where(q_ge_k, toep_ref[h], zero_bias[h])

            def fill_gap128(rows=rows):
                for h in range(ATTN_HEADS):
                    b_ref[h, rows, :] = jnp.where(q_ge_k, far_bias[h], toep_ref[h])

            def fill_lookup(rows=rows):
                pk_sub = pk_row[:, rows]
                n_qk = jnp.clip(pq_col - pk_sub, 0, BIAS_TABLE_N - 1).astype(F32)
                n_kq = n_qk.T.astype(jnp.int32)
                for h in range(ATTN_HEADS):
                    tb = jnp.broadcast_to(tbl_ref[h:h + 1, :], (LANES, BIAS_TABLE_N))
                    b_ref[h, rows, :] = jnp.take_along_axis(tb, n_kq, axis=1)

            def fill_near(fill_gap0=fill_gap0, fill_gap128=fill_gap128, fill_lookup=fill_lookup,
                          consecutive=consecutive, gap=gap):
                lax.cond(consecutive & (gap == 0), fill_gap0,
                         lambda: lax.cond(consecutive & (gap == LANES), fill_gap128, fill_lookup))

            lax.cond(all_far | all_masked, fill_const, fill_near)

    def logits_phase(c, s_buf, const_bias, seen):
        off = chunk_off(c)
        mb, seen = selection_mask(c, seen)
        m_cur = []
        for pair in range(n_pairs):
            kc = k_ref[pl.ds(off, CK), pair * LANES:(pair + 1) * LANES]
            s2 = jnp.dot(kc, qm_ref[pair], preferred_element_type=F32)
            for sub, h in enumerate((2 * pair, 2 * pair + 1)):
                s = s2[:, sub * TQ:(sub + 1) * TQ] + mb
                if not const_bias:
                    s = s + b_ref[h]
                s_buf[h] = s
                top = col_reduce(_fold_rows(s, jnp.maximum), jnp.max)
                m_cur.append(top + far_bias[h] if const_bias else top)
        return m_cur, seen

    def update_phase(c, s_buf, m_cur, const_bias):
        off = chunk_off(c)
        m_prev = m_ref[...]
        l_prev = l_ref[...]
        if isinstance(m_cur, list):
            m_new = [jnp.maximum(m_prev[h:h + 1, :], m_cur[h]) for h in range(ATTN_HEADS)]
            alpha = [jnp.exp2(m_prev[h:h + 1, :] - m_new[h]) for h in range(ATTN_HEADS)]
        else:
            m_all = jnp.maximum(m_prev, m_cur)
            a_all = jnp.exp2(m_prev - m_all)
            m_new = [m_all[h:h + 1, :] for h in range(ATTN_HEADS)]
            alpha = [a_all[h:h + 1, :] for h in range(ATTN_HEADS)]
        l_new = []
        for pair in range(n_pairs):
            heads = (2 * pair, 2 * pair + 1)
            sub_m = [m_new[h] - far_bias[h] if const_bias else m_new[h] for h in heads]
            p2 = jnp.concatenate([jnp.exp2(s_buf[h] - sm).astype(BF16) for h, sm in zip(heads, sub_m)],
                                 axis=1)
            lhs = jnp.concatenate([vT_ref[pair * LANES:(pair + 1) * LANES, pl.ds(off, CK)], ones_rows],
                                  axis=0)
            out = jnp.dot(lhs, p2, preferred_element_type=F32)
            for sub, h in enumerate(heads):
                cols = slice(sub * TQ, (sub + 1) * TQ)
                acc = acc_refs[h]
                acc[...] = alpha[h] * acc[...] + out[sub * ATTN_HEAD_DIM:(sub + 1) * ATTN_HEAD_DIM, cols]
                l_new.append(alpha[h] * l_prev[h:h + 1, :] + out[LANES:LANES + 1, cols])
        l_ref[...] = jnp.concatenate(l_new, axis=0)
        m_ref[...] = jnp.concatenate(m_new, axis=0)

    n_far = lax.while_loop(lambda c: (c < nch) & chunk_is_far(c), lambda c: c + 1, jnp.int32(0))
    n_steps = n_far // 2
    def far_logits(c, s_buf, seen):
        m_cur, seen = logits_phase(c, s_buf, True, seen)
        return jnp.concatenate(m_cur, axis=0), seen

    no_ties = jnp.zeros((1, TQ), F32)
    m_first, seen_first = lax.cond(n_steps > 0, lambda: far_logits(0, s_ref, no_ties),
                                   lambda: (jnp.zeros((ATTN_HEADS, TQ), F32), no_ties))

    def far_step(t, carry):
        m_even, seen_even, _ = carry
        c = 2 * t
        m_odd, seen_odd = far_logits(c + 1, s2_ref, seen_even)
        update_phase(c, s_ref, m_even, True)
        m_even, seen_even = far_logits(jnp.minimum(c + 2, n_far - 1), s_ref, seen_odd)
        update_phase(c + 1, s2_ref, m_odd, True)
        return m_even, seen_even, seen_odd

    m_even, seen_even, seen = lax.fori_loop(0, n_steps, far_step, (m_first, seen_first, no_ties))
    leftover_staged = (n_steps > 0) & (n_far % 2 == 1)

    def finish_leftover():
        update_phase(n_far - 1, s_ref, m_even, True)
        return seen_even

    seen = lax.cond(leftover_staged, finish_leftover, lambda: seen)
    n_done = jnp.where(leftover_staged, n_far, 2 * n_steps)

    def tail_chunk(c, seen):
        def run(const_bias):
            if not const_bias:
                stage_bias(c)
            m_cur, seen_out = logits_phase(c, s_ref, const_bias, seen)
            update_phase(c, s_ref, m_cur, const_bias)
            return seen_out

        return lax.cond(chunk_is_far(c), lambda: run(True), lambda: run(False))

    lax.fori_loop(n_done, nch, tail_chunk, seen)

    outT = jnp.concatenate([acc_refs[h][...] / l_ref[h:h + 1, :] for h in range(ATTN_HEADS)], axis=0)
    o_ref[...] = outT.T.astype(o_ref.dtype)


def _toeplitz_kernel(tbl_ref, o_ref):
    q = lax.broadcasted_iota(jnp.int32, (LANES, LANES), 1)
    k = lax.broadcasted_iota(jnp.int32, (LANES, LANES), 0)
    idx = (q - k) & (BIAS_TABLE_N - 1)
    for h in range(ATTN_HEADS):
        tb = jnp.broadcast_to(tbl_ref[h:h + 1, :], (LANES, BIAS_TABLE_N))
        o_ref[h] = jnp.take_along_axis(tb, idx, axis=1)


def _sparse_attention(fm, tokb, ik, iwT, positions, rel_bias):
    B, S, _ = tokb.shape
    k_sel = min(TOPK_MAX, S // 4)
    bucket = _t5_bucket_table()
    tbl = rel_bias[bucket].T.astype(F32) * LOG2E
    bias_c = jnp.stack([tbl[:, BIAS_TABLE_N - 1], tbl[:, 0]])
    toep = pl.pallas_call(
        _toeplitz_kernel,
        out_shape=jax.ShapeDtypeStruct((ATTN_HEADS, LANES, LANES), F32),
        name="bias_toeplitz",
    )(tbl)
    pos_row = positions.reshape(B, 1, S)
    pos_col = positions.reshape(B, S, 1)
    pos_t = positions.reshape(B, S // LANES, LANES)
    consec = jnp.all(pos_t - pos_t[:, :, :1] == jnp.arange(LANES, dtype=positions.dtype), axis=-1)
    pinfo = jnp.stack([pos_t[:, :, 0], consec.astype(jnp.int32), jnp.max(pos_t, axis=-1),
                       jnp.min(pos_t, axis=-1)], axis=1).astype(jnp.int32)
    ck_max = jnp.max(positions.reshape(B, S // CK, CK), axis=-1)
    tril = jnp.tril(jnp.ones((LANES, LANES), BF16))
    sq = pl.Squeezed()
    iq_blk = (2 * ATTN_W) // IDX_Q_W
    k_blk = (tokb.shape[2] - ATTN_W) // ATTN_W
    return pl.pallas_call(
        functools.partial(_attn_kernel, k_sel=k_sel, seq=S),
        grid=(B, S // TQ),
        in_specs=[pl.BlockSpec(memory_space=pltpu.SMEM),
                  pl.BlockSpec(memory_space=pltpu.SMEM),
                  pl.BlockSpec(memory_space=pltpu.SMEM),
                  pl.BlockSpec((sq, IDX_Q_W, TQ), lambda b, i: (b, iq_blk, i)),
                  pl.BlockSpec((sq, SUBLANES, TQ), lambda b, i: (b, 0, i)),
                  pl.BlockSpec((sq, TQ, 1), lambda b, i: (b, i, 0)),
                  pl.BlockSpec((sq, 1, S), lambda b, i: (b, 0, 0)),
                  pl.BlockSpec((sq, S, IDX_HEAD_DIM), lambda b, i: (b, 0, 0)),
                  pl.BlockSpec((sq, ATTN_W, TQ), lambda b, i: (b, 0, i)),
                  pl.BlockSpec((sq, S, ATTN_W), lambda b, i: (b, 0, k_blk)),
                  pl.BlockSpec((sq, ATTN_W, S), lambda b, i: (b, 1, 0)),
                  pl.BlockSpec((ATTN_HEADS, BIAS_TABLE_N), lambda b, i: (0, 0)),
                  pl.BlockSpec((ATTN_HEADS, LANES, LANES), lambda b, i: (0, 0, 0)),
                  pl.BlockSpec((LANES, LANES), lambda b, i: (0, 0))],
        out_specs=pl.BlockSpec((sq, TQ, ATTN_W), lambda b, i: (b, i, 0)),
        out_shape=jax.ShapeDtypeStruct((B, S, ATTN_W), BF16),
        scratch_shapes=[pltpu.VMEM((S, TQ), F32),
                        pltpu.VMEM((ATTN_HEADS, CK, TQ), F32),
                        pltpu.VMEM((ATTN_HEADS, CK, TQ), F32),
                        pltpu.VMEM((ATTN_HEADS, CK, TQ), F32),
                        pltpu.VMEM((ATTN_HEADS // 2, LANES, 2 * TQ), BF16),
                        pltpu.VMEM((ATTN_HEADS, TQ), F32),
                        pltpu.VMEM((ATTN_HEADS, TQ), F32)]
                       + [pltpu.VMEM((ATTN_HEAD_DIM, TQ), F32)] * ATTN_HEADS,
        compiler_params=_cparams(2),
        name="sparse_attention",
    )(bias_c, pinfo, ck_max, fm, iwT, pos_col, pos_row, ik, fm, tokb, fm, tbl, toep, tril)


def _retention_kernel(q_ref, k_ref, v_ref, g_ref, decay_ref, xi_ref, zeta_ref, gch_ref, o_ref, r_ref):
    @pl.when(pl.program_id(0) == 0)
    def _():
        r_ref[...] = jnp.zeros(r_ref.shape, F32)

    C = RET_CHUNK
    lane = lax.broadcasted_iota(jnp.int32, (C, LANES), 1)
    row = lax.broadcasted_iota(jnp.int32, (LANES, RET_V_DIM), 0)
    for ci in range(q_ref.shape[1] // C):
        rows = slice(ci * C, (ci + 1) * C)
        for pair in range(RET_HEADS // 2):
            for b in range(q_ref.shape[0]):
                q_pair = q_ref[b, rows, pair * LANES:(pair + 1) * LANES]
                k_pair = k_ref[b, rows, pair * LANES:(pair + 1) * LANES]
                v_pair = v_ref[b, rows, 2 * pair * RET_V_DIM:(2 * pair + 2) * RET_V_DIM]
                r_pair = r_ref[b, pair]
                r_bf = r_pair.astype(BF16)
                for sub in range(2):
                    h = 2 * pair + sub
                    in_head = (lane >= sub * RET_QK_DIM) & (lane < (sub + 1) * RET_QK_DIM)
                    qm = jnp.where(in_head, q_pair, jnp.zeros_like(q_pair))
                    v_h = v_pair[:, sub * RET_V_DIM:(sub + 1) * RET_V_DIM]
                    inner = lax.dot_general(qm, k_pair, (((1,), (1,)), ((), ())),
                                            preferred_element_type=F32) * decay_ref[h]
                    o = (jnp.dot(inner.astype(BF16), v_h, preferred_element_type=F32)
                         + jnp.dot(qm, r_bf, preferred_element_type=F32) * xi_ref[h])
                    mu = jnp.mean(o, axis=-1, keepdims=True)
                    d = o - mu
                    var = jnp.mean(d * d, axis=-1, keepdims=True)
                    hn = d * lax.rsqrt(var + LN_EPS)
                    gate = g_ref[b, rows, h * RET_V_DIM:(h + 1) * RET_V_DIM].astype(F32)
                    o_ref[b, rows, h * RET_V_DIM:(h + 1) * RET_V_DIM] = (gate * hn).astype(o_ref.dtype)
                kz = (k_pair.astype(F32) * zeta_ref[pair]).astype(BF16)
                upd = lax.dot_general(kz, v_pair, (((0,), (0,)), ((), ())),
                                      preferred_element_type=F32)
                r_ref[b, pair] = (r_pair * gch_ref[pair]
                                  + jnp.where(row < RET_QK_DIM, upd[:, :RET_V_DIM], upd[:, RET_V_DIM:]))


def _retention(qk, tokb, gates, B, S):
    C = RET_CHUNK
    H = RET_HEADS
    nc = S // C
    gamma = 1.0 - 2.0 ** (-5.0 - jnp.arange(H, dtype=F32))
    log_g = jnp.log(gamma)
    n = jnp.arange(C, dtype=F32)
    diff = n[:, None] - n[None, :]
    decay_in = jnp.where(diff[None] >= 0, jnp.exp(log_g[:, None, None] * jnp.maximum(diff, 0.0)[None]), 0.0)
    xi = jnp.exp(log_g[None, :] * (n[:, None] + 1.0))
    zeta = jnp.exp(log_g[None, :] * (C - 1.0 - n[:, None]))
    g_chunk = jnp.exp(log_g * C)
    xi_b = jnp.broadcast_to(xi.T[:, :, None], (H, C, RET_V_DIM))
    zeta_b = jnp.repeat(zeta, RET_QK_DIM, axis=1).reshape(C, H // 2, LANES).transpose(1, 0, 2)
    gch_b = jnp.broadcast_to(jnp.repeat(g_chunk, RET_QK_DIM).reshape(H // 2, LANES, 1),
                             (H // 2, LANES, RET_V_DIM))
    qk3, tok3, gate3 = (a.reshape(B, S, a.shape[-1]) for a in (qk, tokb, gates))
    CS = C * RET_STEP_CHUNKS
    out = pl.pallas_call(
        _retention_kernel,
        grid=(nc // RET_STEP_CHUNKS,),
        in_specs=[pl.BlockSpec((B, CS, RET_QK_W), lambda i: (0, i, 0)),
                  pl.BlockSpec((B, CS, RET_QK_W), lambda i: (0, i, 1)),
                  pl.BlockSpec((B, CS, RET_V_W), lambda i: (0, i, 0)),
                  pl.BlockSpec((B, CS, RET_V_W), lambda i: (0, i, 0)),
                  pl.BlockSpec((H, C, C), lambda i: (0, 0, 0)),
                  pl.BlockSpec((H, C, RET_V_DIM), lambda i: (0, 0, 0)),
                  pl.BlockSpec((H // 2, C, LANES), lambda i: (0, 0, 0)),
                  pl.BlockSpec((H // 2, LANES, RET_V_DIM), lambda i: (0, 0, 0))],
        out_specs=pl.BlockSpec((B, CS, RET_V_W), lambda i: (0, i, 0)),
        out_shape=jax.ShapeDtypeStruct((B, S, RET_V_W), BF16),
        scratch_shapes=[pltpu.VMEM((B, H // 2, LANES, RET_V_DIM), F32)],
        compiler_params=_cparams(1),
        name="retention",
    )(qk3, qk3, tok3, gate3, decay_in, xi_b, zeta_b, gch_b)
    return out.reshape(B * S, RET_V_W)


def _layer_norm(z, g, b):
    mu = jnp.mean(z, axis=-1, keepdims=True)
    d = z - mu
    var = jnp.mean(d * d, axis=-1, keepdims=True)
    return d * lax.rsqrt(var + LN_EPS) * g + b


def _merge_kernel(x_ref, ya_ref, yr_ref, ga_ref, gr_ref, wa_ref, wr_ref, wo_ref, g_ref, b_ref,
                  x1_ref):
    a = jnp.dot(ya_ref[...], wa_ref[...], preferred_element_type=F32)
    r = jnp.dot(yr_ref[...], wr_ref[...], preferred_element_type=F32)
    h = ga_ref[...].astype(F32) * a + gr_ref[...].astype(F32) * r
    mix = jnp.dot(h.astype(BF16), wo_ref[...], preferred_element_type=F32)
    x1_ref[...] = _layer_norm(DEEPNORM_ALPHA * x_ref[...] + mix, g_ref[...], b_ref[...])


def _merge(x, ya, yr, gates, wa, wr, wo, g, b, tm=MERGE_TM):
    T, D = x.shape
    tm = min(tm, T)
    row = lambda i: (i, 0)
    fixed = lambda i: (0, 0)
    return pl.pallas_call(
        _merge_kernel,
        grid=(T // tm,),
        in_specs=[pl.BlockSpec((tm, D), row),
                  pl.BlockSpec((tm, ya.shape[1]), row),
                  pl.BlockSpec((tm, yr.shape[1]), row),
                  pl.BlockSpec((tm, D), lambda i: (i, 1)),
                  pl.BlockSpec((tm, D), lambda i: (i, 2)),
                  pl.BlockSpec(wa.shape, fixed),
                  pl.BlockSpec(wr.shape, fixed),
                  pl.BlockSpec(wo.shape, fixed),
                  pl.BlockSpec((1, D), fixed),
                  pl.BlockSpec((1, D), fixed)],
        out_specs=pl.BlockSpec((tm, D), row),
        out_shape=jax.ShapeDtypeStruct((T, D), F32),
        compiler_params=_cparams(1),
        name="merge",
    )(x, ya, yr, gates, gates, wa, wr, wo, g, b)


def _ffn_kernel(x1_ref, wu_ref, wd_ref, g_ref, b_ref, o_ref, acc_ref, xb_ref):
    f = pl.program_id(1)

    @pl.when(f == 0)
    def _():
        acc_ref[...] = jnp.zeros(acc_ref.shape, F32)
        xb_ref[...] = x1_ref[...].astype(BF16)

    hid = jnp.maximum(jnp.dot(xb_ref[...], wu_ref[...], preferred_element_type=F32), 0.0)
    acc_ref[...] += jnp.dot((hid * hid).astype(BF16), wd_ref[...], preferred_element_type=F32)

    @pl.when(f == pl.num_programs(1) - 1)
    def _():
        o_ref[...] = _layer_norm(DEEPNORM_ALPHA * x1_ref[...] + acc_ref[...], g_ref[...], b_ref[...])


def _ffn(x1, wu, wd, g, b, tm=FFN_TM, tf=FFN_TF):
    T, D = x1.shape
    F = wu.shape[1]
    tm = min(tm, T)
    return pl.pallas_call(
        _ffn_kernel,
        grid=(T // tm, F // tf),
        in_specs=[pl.BlockSpec((tm, D), lambda i, f: (i, 0)),
                  pl.BlockSpec((D, tf), lambda i, f: (0, f)),
                  pl.BlockSpec((tf, D), lambda i, f: (f, 0)),
                  pl.BlockSpec((1, D), lambda i, f: (0, 0)),
                  pl.BlockSpec((1, D), lambda i, f: (0, 0))],
        out_specs=pl.BlockSpec((tm, D), lambda i, f: (i, 0)),
        out_shape=jax.ShapeDtypeStruct((T, D), F32),
        scratch_shapes=[pltpu.VMEM((tm, D), F32), pltpu.VMEM((tm, D), BF16)],
        compiler_params=_cparams(2),
        name="ffn",
    )(x1, wu, wd, g, b)


def _rot_half_weight(wT):
    N, D = wT.shape
    half = RET_QK_DIM // 2
    wh = wT.reshape(N // RET_QK_DIM, 2, half, D)
    return jnp.stack([-wh[:, 1], wh[:, 0]], axis=1).reshape(N, D)


def kernel(x, positions, w_in, rel_bias, idx_k_ln_g, idx_k_ln_b, w_attn_branch, w_ret_branch,
           w_out, ln_mix_g, ln_mix_b, w_up, w_down, ln_ffn_g, ln_ffn_b):
    B, S, D = x.shape
    T = B * S
    sizes = (ATTN_W, ATTN_W, ATTN_W, IDX_Q_W, IDX_HEAD_DIM, IDX_HEADS,
             RET_QK_W, RET_QK_W, RET_V_W, RET_V_W, D, D)
    offs = [0] + [int(o) for o in np.cumsum(sizes)]
    cos, sin = _rope_tables(positions)
    xf = x.reshape(T, D)
    for l in range(DEPTH):
        wT = jnp.swapaxes(w_in[l], 0, 1).astype(BF16)
        rows = [wT[offs[k]:offs[k + 1]] for k in range(len(sizes))]
        (w_qa, w_ka, w_va, w_iq, w_ik, w_iw, w_qr, w_kr, w_vr, w_gr, w_ga, w_gtr) = rows
        w_kr = w_kr * (RET_QK_DIM ** -0.5)

        pad = LANES - IDX_HEAD_DIM - IDX_HEADS
        w_idx = jnp.concatenate([w_ik, w_iw, jnp.zeros((pad, D), BF16)], axis=0)
        g_pad = jnp.concatenate([idx_k_ln_g[l], jnp.zeros((LANES - IDX_HEAD_DIM,), F32)]).reshape(1, LANES)
        b_pad = jnp.concatenate([idx_k_ln_b[l], jnp.zeros((LANES - IDX_HEAD_DIM,), F32)]).reshape(1, LANES)
        ik, iwT, xb = _proj_idx(xf, w_idx, g_pad, b_pad, (IDX_HEAD_DIM ** -0.5) * (IDX_HEADS ** -0.5), B, S)
        ik = ik.reshape(B, S, IDX_HEAD_DIM)

        fm = _proj_t(xb, jnp.concatenate([w_qa, w_va, w_iq], axis=0), B, S, BF16,
                     scaled_rows=ATTN_W, scale=ATTN_HEAD_DIM ** -0.5 * LOG2E)
        tokb = _proj(xb, jnp.concatenate([w_vr, w_ka], axis=0), BF16)
        gates = _proj_gates(xb, jnp.concatenate([w_gr, w_ga, w_gtr], axis=0), tn=D)
        w_rope = jnp.concatenate([w_qr, w_kr], axis=0)
        w_rope_rot = jnp.concatenate([_rot_half_weight(w_qr), _rot_half_weight(w_kr)], axis=0)
        qk_r = _proj_rope(xb, w_rope, w_rope_rot, cos, sin)

        y_a = _sparse_attention(fm, tokb.reshape(B, S, -1), ik, iwT, positions, rel_bias)
        y_r = _retention(qk_r, tokb, gates, B, S)
        x1 = _merge(xf, y_a.reshape(T, ATTN_W), y_r, gates,
                    w_attn_branch[l].astype(BF16), w_ret_branch[l].astype(BF16),
                    w_out[l].astype(BF16), ln_mix_g[l].reshape(1, D), ln_mix_b[l].reshape(1, D))
        xf = _ffn(x1, w_up[l].astype(BF16), w_down[l].astype(BF16),
                  ln_ffn_g[l].reshape(1, D), ln_ffn_b[l].reshape(1, D))
    return xf.reshape(B, S, D)
```
